```python
import math
import jax, jax.numpy as jnp
from jax import lax
import numpy as np


D_MODEL = 1024
BATCH = 8
SEQ = 4096
DEPTH = 4

N_A = DEPTH // 2
N_B = DEPTH - N_A
N_HEADS = 16
HEAD_DIM = D_MODEL // N_HEADS
CONV_W = 31
FFN_DIM = 2816
FFN_CONV_W = 3
PLE_DIM = 256
Q_BLOCK = 128
LN_EPS = 1e-5
DN_ALPHA = (2.0 * DEPTH) ** 0.25
DN_BETA = (8.0 * DEPTH) ** -0.25

kernel_name = "yoco_conformer_stickbreaking_hybrid"


def layer_norm(x, g, b):
    xf = x.astype(jnp.float32)
    mu = jnp.mean(xf, axis=-1, keepdims=True)
    var = jnp.mean(jnp.square(xf - mu), axis=-1, keepdims=True)
    y = (xf - mu) * lax.rsqrt(var + LN_EPS)
    return (y * g.astype(jnp.float32) + b.astype(jnp.float32)).astype(x.dtype)


def causal_dwconv(x, w, b):
    k = w.shape[0]
    y = lax.conv_general_dilated(
        x, w[:, None, :].astype(x.dtype), window_strides=(1,), padding=[(k - 1, 0)],
        dimension_numbers=("NWC", "WIO", "NWC"), feature_group_count=x.shape[-1])
    return y + b


def conformer_conv(x, pw1_w, pw1_b, dw_w, dw_b, ln_g, ln_b, pw2_w, pw2_b):
    h = x @ pw1_w + pw1_b
    a, g = jnp.split(h, 2, axis=-1)
    h = a * jax.nn.sigmoid(g)
    h = causal_dwconv(h, dw_w, dw_b)
    h = layer_norm(h, ln_g, ln_b)
    h = jax.nn.silu(h)
    return h @ pw2_w + pw2_b


def stick_breaking_attention(q, k, v):
    b, s, h, dh = q.shape
    nb = s // Q_BLOCK
    scale = 1.0 / math.sqrt(dh)
    kh = jnp.transpose(k, (0, 2, 1, 3))
    vh = jnp.transpose(v, (0, 2, 1, 3))
    qb = jnp.transpose(q.reshape(b, nb, Q_BLOCK, h, dh), (1, 0, 3, 2, 4))
    t0s = jnp.arange(nb, dtype=jnp.int32) * Q_BLOCK
    ts = jnp.arange(s, dtype=jnp.int32)

    def block(args):
        qblk, t0 = args
        z = jnp.einsum('bhqd,bhkd->bhqk', qblk, kh).astype(jnp.float32) * scale
        tq = t0 + jnp.arange(Q_BLOCK, dtype=jnp.int32)
        mask = ts[None, :] < tq[:, None]
        log_1m = jnp.where(mask, jax.nn.log_sigmoid(-z), 0.0)
        rev = lax.cumsum(log_1m, axis=3, reverse=True)
        log_a = jax.nn.log_sigmoid(z) + (rev - log_1m)
        a = jnp.where(mask, jnp.exp(log_a), 0.0)
        return jnp.einsum('bhqk,bhkd->bhqd', a.astype(vh.dtype), vh)

    out = lax.map(block, (qb, t0s))
    return jnp.transpose(out, (1, 0, 3, 2, 4)).reshape(b, s, h * dh)


def conv_gated_ffn(x, w_up, w_gate, conv_w, conv_b, w_down):
    u = x @ w_up
    g = causal_dwconv(x @ w_gate, conv_w, conv_b)
    return (jax.nn.silu(g) * u) @ w_down


def _fwd_setup_inputs(seed: int = 0) -> dict:
    key = jax.random.key(seed)
    ks = jax.random.split(key, 32)
    D, F = D_MODEL, FFN_DIM

    def nrm(k, shape, scale):
        return jax.random.normal(k, shape, jnp.float32) * scale

    return {
        "x": nrm(ks[0], (BATCH, SEQ, D), 1.0),
        "p": nrm(ks[1], (DEPTH, BATCH, SEQ, PLE_DIM), 1.0),
        "a_pw1_w": nrm(ks[2], (N_A, D, 2 * D), D ** -0.5),
        "a_pw1_b": nrm(ks[3], (N_A, 2 * D), 0.02),
        "a_dw_w": nrm(ks[4], (N_A, CONV_W, D), CONV_W ** -0.5),
        "a_dw_b": nrm(ks[5], (N_A, D), 0.02),
        "a_ln_g": 1.0 + nrm(ks[6], (N_A, D), 0.02),
        "a_ln_b": nrm(ks[7], (N_A, D), 0.02),
        "a_pw2_w": nrm(ks[8], (N_A, D, D), D ** -0.5 * DN_BETA),
        "a_pw2_b": nrm(ks[9], (N_A, D), 0.02),
        "b_wq": nrm(ks[10], (N_B, D, D), D ** -0.5),
        "kv_wk": nrm(ks[11], (D, D), D ** -0.5),
        "kv_wv": nrm(ks[12], (D, D), D ** -0.5 * DN_BETA),
        "b_wo": nrm(ks[13], (N_B, D, D), D ** -0.5 * DN_BETA),
        "ln_mix_g": 1.0 + nrm(ks[14], (DEPTH, D), 0.02),
        "ln_mix_b": nrm(ks[15], (DEPTH, D), 0.02),
        "ffn_w_up": nrm(ks[16], (DEPTH, D, F), D ** -0.5),
        "ffn_w_gate": nrm(ks[17], (DEPTH, D, F), D ** -0.5),
        "ffn_conv_w": nrm(ks[18], (DEPTH, FFN_CONV_W, F), FFN_CONV_W ** -0.5),
        "ffn_conv_b": nrm(ks[19], (DEPTH, F), 0.02),
        "ffn_w_down": nrm(ks[20], (DEPTH, F, D), F ** -0.5 * DN_BETA),
        "ple_w_gate": nrm(ks[21], (DEPTH, D, D), D ** -0.5),
        "ple_w_proj": nrm(ks[22], (DEPTH, PLE_DIM, D), PLE_DIM ** -0.5 * DN_BETA),
        "ln_ffn_g": 1.0 + nrm(ks[23], (DEPTH, D), 0.02),
        "ln_ffn_b": nrm(ks[24], (DEPTH, D), 0.02),
    }


def _fwd_reference(x, p, a_pw1_w, a_pw1_b, a_dw_w, a_dw_b, a_ln_g, a_ln_b, a_pw2_w, a_pw2_b,
              b_wq, kv_wk, kv_wv, b_wo, ln_mix_g, ln_mix_b,
              ffn_w_up, ffn_w_gate, ffn_conv_w, ffn_conv_b, ffn_w_down,
              ple_w_gate, ple_w_proj, ln_ffn_g, ln_ffn_b):
    b, s, d = x.shape
    k_shared = None
    v_shared = None
    for i in range(DEPTH):
        if i < N_A:
            mix = conformer_conv(x, a_pw1_w[i], a_pw1_b[i], a_dw_w[i], a_dw_b[i],
                                 a_ln_g[i], a_ln_b[i], a_pw2_w[i], a_pw2_b[i])
        else:
            j = i - N_A
            if k_shared is None:
                k_shared = (x @ kv_wk).reshape(b, s, N_HEADS, HEAD_DIM)
                v_shared = (x @ kv_wv).reshape(b, s, N_HEADS, HEAD_DIM)
            q = (x @ b_wq[j]).reshape(b, s, N_HEADS, HEAD_DIM)
            mix = stick_breaking_attention(q, k_shared, v_shared) @ b_wo[j]
        x = layer_norm(DN_ALPHA * x + mix, ln_mix_g[i], ln_mix_b[i])
        ffn = conv_gated_ffn(x, ffn_w_up[i], ffn_w_gate[i], ffn_conv_w[i], ffn_conv_b[i], ffn_w_down[i])
        ple = jax.nn.sigmoid(x @ ple_w_gate[i]) * (p[i] @ ple_w_proj[i])
        x = layer_norm(DN_ALPHA * x + ffn + ple, ln_ffn_g[i], ln_ffn_b[i])
    return x


import jax as _jax
import jax.numpy as _jnp

TWIN_FORMAT = 'train_step'
FWD_PARAMS = ['x', 'p', 'a_pw1_w', 'a_pw1_b', 'a_dw_w', 'a_dw_b', 'a_ln_g', 'a_ln_b', 'a_pw2_w', 'a_pw2_b', 'b_wq', 'kv_wk', 'kv_wv', 'b_wo', 'ln_mix_g', 'ln_mix_b', 'ffn_w_up', 'ffn_w_gate', 'ffn_conv_w', 'ffn_conv_b', 'ffn_w_down', 'ple_w_gate', 'ple_w_proj', 'ln_ffn_g', 'ln_ffn_b']
TWIN_WEIGHTS = ['a_pw1_w', 'a_pw1_b', 'a_dw_w', 'a_dw_b', 'a_ln_g', 'a_ln_b', 'a_pw2_w', 'a_pw2_b', 'b_wq', 'kv_wk', 'kv_wv', 'b_wo', 'ln_mix_g', 'ln_mix_b', 'ffn_w_up', 'ffn_w_gate', 'ffn_conv_w', 'ffn_conv_b', 'ffn_w_down', 'ple_w_gate', 'ple_w_proj', 'ln_ffn_g', 'ln_ffn_b']
TWIN_DIFF_INPUT = 'x'
TWIN_INPUTS = ['x', 'p', 'a_pw1_w', 'a_pw1_b', 'a_dw_w', 'a_dw_b', 'a_ln_g', 'a_ln_b', 'a_pw2_w', 'a_pw2_b', 'b_wq', 'kv_wk', 'kv_wv', 'b_wo', 'ln_mix_g', 'ln_mix_b', 'ffn_w_up', 'ffn_w_gate', 'ffn_conv_w', 'ffn_conv_b', 'ffn_w_down', 'ple_w_gate', 'ple_w_proj', 'ln_ffn_g', 'ln_ffn_b', 'loss_target', 'm_a_pw1_w', 'm_a_pw1_b', 'm_a_dw_w', 'm_a_dw_b', 'm_a_ln_g', 'm_a_ln_b', 'm_a_pw2_w', 'm_a_pw2_b', 'm_b_wq', 'm_kv_wk', 'm_kv_wv', 'm_b_wo', 'm_ln_mix_g', 'm_ln_mix_b', 'm_ffn_w_up', 'm_ffn_w_gate', 'm_ffn_conv_w', 'm_ffn_conv_b', 'm_ffn_w_down', 'm_ple_w_gate', 'm_ple_w_proj', 'm_ln_ffn_g', 'm_ln_ffn_b', 'v_a_pw1_w', 'v_a_pw1_b', 'v_a_dw_w', 'v_a_dw_b', 'v_a_ln_g', 'v_a_ln_b', 'v_a_pw2_w', 'v_a_pw2_b', 'v_b_wq', 'v_kv_wk', 'v_kv_wv', 'v_b_wo', 'v_ln_mix_g', 'v_ln_mix_b', 'v_ffn_w_up', 'v_ffn_w_gate', 'v_ffn_conv_w', 'v_ffn_conv_b', 'v_ffn_w_down', 'v_ple_w_gate', 'v_ple_w_proj', 'v_ln_ffn_g', 'v_ln_ffn_b']
TWIN_OUTPUTS = ['loss', 'grad_x', 'grad_a_pw1_w', 'grad_a_pw1_b', 'grad_a_dw_w', 'grad_a_dw_b', 'grad_a_ln_g', 'grad_a_ln_b', 'grad_a_pw2_w', 'grad_a_pw2_b', 'grad_b_wq', 'grad_kv_wk', 'grad_kv_wv', 'grad_b_wo', 'grad_ln_mix_g', 'grad_ln_mix_b', 'grad_ffn_w_up', 'grad_ffn_w_gate', 'grad_ffn_conv_w', 'grad_ffn_conv_b', 'grad_ffn_w_down', 'grad_ple_w_gate', 'grad_ple_w_proj', 'grad_ln_ffn_g', 'grad_ln_ffn_b', 'delta_a_pw1_w', 'delta_a_pw1_b', 'delta_a_dw_w', 'delta_a_dw_b', 'delta_a_ln_g', 'delta_a_ln_b', 'delta_a_pw2_w', 'delta_a_pw2_b', 'delta_b_wq', 'delta_kv_wk', 'delta_kv_wv', 'delta_b_wo', 'delta_ln_mix_g', 'delta_ln_mix_b', 'delta_ffn_w_up', 'delta_ffn_w_gate', 'delta_ffn_conv_w', 'delta_ffn_conv_b', 'delta_ffn_w_down', 'delta_ple_w_gate', 'delta_ple_w_proj', 'delta_ln_ffn_g', 'delta_ln_ffn_b', 'new_m_a_pw1_w', 'new_m_a_pw1_b', 'new_m_a_dw_w', 'new_m_a_dw_b', 'new_m_a_ln_g', 'new_m_a_ln_b', 'new_m_a_pw2_w', 'new_m_a_pw2_b', 'new_m_b_wq', 'new_m_kv_wk', 'new_m_kv_wv', 'new_m_b_wo', 'new_m_ln_mix_g', 'new_m_ln_mix_b', 'new_m_ffn_w_up', 'new_m_ffn_w_gate', 'new_m_ffn_conv_w', 'new_m_ffn_conv_b', 'new_m_ffn_w_down', 'new_m_ple_w_gate', 'new_m_ple_w_proj', 'new_m_ln_ffn_g', 'new_m_ln_ffn_b', 'new_v_a_pw1_w', 'new_v_a_pw1_b', 'new_v_a_dw_w', 'new_v_a_dw_b', 'new_v_a_ln_g', 'new_v_a_ln_b', 'new_v_a_pw2_w', 'new_v_a_pw2_b', 'new_v_b_wq', 'new_v_kv_wk', 'new_v_kv_wv', 'new_v_b_wo', 'new_v_ln_mix_g', 'new_v_ln_mix_b', 'new_v_ffn_w_up', 'new_v_ffn_w_gate', 'new_v_ffn_conv_w', 'new_v_ffn_conv_b', 'new_v_ffn_w_down', 'new_v_ple_w_gate', 'new_v_ple_w_proj', 'new_v_ln_ffn_g', 'new_v_ln_ffn_b']
TWIN_LEAF_KINDS = {'loss': 'loss', 'grad_x': 'grad_x', 'grad_a_pw1_w': 'grad_w', 'grad_a_pw1_b': 'grad_w', 'grad_a_dw_w': 'grad_w', 'grad_a_dw_b': 'grad_w', 'grad_a_ln_g': 'grad_w', 'grad_a_ln_b': 'grad_w', 'grad_a_pw2_w': 'grad_w', 'grad_a_pw2_b': 'grad_w', 'grad_b_wq': 'grad_w', 'grad_kv_wk': 'grad_w', 'grad_kv_wv': 'grad_w', 'grad_b_wo': 'grad_w', 'grad_ln_mix_g': 'grad_w', 'grad_ln_mix_b': 'grad_w', 'grad_ffn_w_up': 'grad_w', 'grad_ffn_w_gate': 'grad_w', 'grad_ffn_conv_w': 'grad_w', 'grad_ffn_conv_b': 'grad_w', 'grad_ffn_w_down': 'grad_w', 'grad_ple_w_gate': 'grad_w', 'grad_ple_w_proj': 'grad_w', 'grad_ln_ffn_g': 'grad_w', 'grad_ln_ffn_b': 'grad_w', 'delta_a_pw1_w': 'delta_w', 'delta_a_pw1_b': 'delta_w', 'delta_a_dw_w': 'delta_w', 'delta_a_dw_b': 'delta_w', 'delta_a_ln_g': 'delta_w', 'delta_a_ln_b': 'delta_w', 'delta_a_pw2_w': 'delta_w', 'delta_a_pw2_b': 'delta_w', 'delta_b_wq': 'delta_w', 'delta_kv_wk': 'delta_w', 'delta_kv_wv': 'delta_w', 'delta_b_wo': 'delta_w', 'delta_ln_mix_g': 'delta_w', 'delta_ln_mix_b': 'delta_w', 'delta_ffn_w_up': 'delta_w', 'delta_ffn_w_gate': 'delta_w', 'delta_ffn_conv_w': 'delta_w', 'delta_ffn_conv_b': 'delta_w', 'delta_ffn_w_down': 'delta_w', 'delta_ple_w_gate': 'delta_w', 'delta_ple_w_proj': 'delta_w', 'delta_ln_ffn_g': 'delta_w', 'delta_ln_ffn_b': 'delta_w', 'new_m_a_pw1_w': 'new_m', 'new_m_a_pw1_b': 'new_m', 'new_m_a_dw_w': 'new_m', 'new_m_a_dw_b': 'new_m', 'new_m_a_ln_g': 'new_m', 'new_m_a_ln_b': 'new_m', 'new_m_a_pw2_w': 'new_m', 'new_m_a_pw2_b': 'new_m', 'new_m_b_wq': 'new_m', 'new_m_kv_wk': 'new_m', 'new_m_kv_wv': 'new_m', 'new_m_b_wo': 'new_m', 'new_m_ln_mix_g': 'new_m', 'new_m_ln_mix_b': 'new_m', 'new_m_ffn_w_up': 'new_m', 'new_m_ffn_w_gate': 'new_m', 'new_m_ffn_conv_w': 'new_m', 'new_m_ffn_conv_b': 'new_m', 'new_m_ffn_w_down': 'new_m', 'new_m_ple_w_gate': 'new_m', 'new_m_ple_w_proj': 'new_m', 'new_m_ln_ffn_g': 'new_m', 'new_m_ln_ffn_b': 'new_m', 'new_v_a_pw1_w': 'new_v', 'new_v_a_pw1_b': 'new_v', 'new_v_a_dw_w': 'new_v', 'new_v_a_dw_b': 'new_v', 'new_v_a_ln_g': 'new_v', 'new_v_a_ln_b': 'new_v', 'new_v_a_pw2_w': 'new_v', 'new_v_a_pw2_b': 'new_v', 'new_v_b_wq': 'new_v', 'new_v_kv_wk': 'new_v', 'new_v_kv_wv': 'new_v', 'new_v_b_wo': 'new_v', 'new_v_ln_mix_g': 'new_v', 'new_v_ln_mix_b': 'new_v', 'new_v_ffn_w_up': 'new_v', 'new_v_ffn_w_gate': 'new_v', 'new_v_ffn_conv_w': 'new_v', 'new_v_ffn_conv_b': 'new_v', 'new_v_ffn_w_down': 'new_v', 'new_v_ple_w_gate': 'new_v', 'new_v_ple_w_proj': 'new_v', 'new_v_ln_ffn_g': 'new_v', 'new_v_ln_ffn_b': 'new_v'}


def _forward(args):
    return _fwd_reference(*[args[k] for k in FWD_PARAMS])


def _output_shape():
    out = _jax.eval_shape(lambda: _forward(_fwd_setup_inputs(0)))
    return out.shape, out.dtype

N_MICROBATCH = 1
ADAM_LR = 0.001
ADAM_B1 = 0.9
ADAM_B2 = 0.999
ADAM_EPS = 1e-08
ADAM_WD = 0.01
ADAM_STEP = 10
PER_EXAMPLE_BATCH_AXIS = {'x': 0, 'p': 1, 'loss_target': 0}
SHARED_INPUTS = []
_WEIGHT_DTYPES = {'a_pw1_w': _jnp.float32, 'a_pw1_b': _jnp.float32, 'a_dw_w': _jnp.float32, 'a_dw_b': _jnp.float32, 'a_ln_g': _jnp.float32, 'a_ln_b': _jnp.float32, 'a_pw2_w': _jnp.float32, 'a_pw2_b': _jnp.float32, 'b_wq': _jnp.float32, 'kv_wk': _jnp.float32, 'kv_wv': _jnp.float32, 'b_wo': _jnp.float32, 'ln_mix_g': _jnp.float32, 'ln_mix_b': _jnp.float32, 'ffn_w_up': _jnp.float32, 'ffn_w_gate': _jnp.float32, 'ffn_conv_w': _jnp.float32, 'ffn_conv_b': _jnp.float32, 'ffn_w_down': _jnp.float32, 'ple_w_gate': _jnp.float32, 'ple_w_proj': _jnp.float32, 'ln_ffn_g': _jnp.float32, 'ln_ffn_b': _jnp.float32}
MOMENT_SCALE = {'a_pw1_w': 2.096429e-02, 'a_pw1_b': 3.768230e-02, 'a_dw_w': 2.783651e-02, 'a_dw_b': 8.753425e-02, 'a_ln_g': 4.287896e-02, 'a_ln_b': 5.490774e-02, 'a_pw2_w': 7.535216e-02, 'a_pw2_b': 2.500214e-01, 'b_wq': 5.609297e-03, 'kv_wk': 7.915095e-03, 'kv_wv': 4.174037e-02, 'b_wo': 2.963409e-02, 'ln_mix_g': 9.688972e-01, 'ln_mix_b': 4.374415e-01, 'ffn_w_up': 1.611854e-02, 'ffn_w_gate': 1.653811e-02, 'ffn_conv_w': 1.664067e-02, 'ffn_conv_b': 1.600491e-02, 'ffn_w_down': 6.347770e-02, 'ple_w_gate': 9.253733e-03, 'ple_w_proj': 5.636077e-02, 'ln_ffn_g': 1.606768e+01, 'ln_ffn_b': 1.287323e+00}


def _to_microbatches(a, axis):
    t = _jnp.moveaxis(a, axis, 0)
    t = t.reshape((N_MICROBATCH, t.shape[0] // N_MICROBATCH) + t.shape[1:])
    return _jnp.moveaxis(t, 1, axis + 1)


def setup_inputs(seed: int = 0) -> dict:
    inp = _fwd_setup_inputs(seed)
    key = _jax.random.fold_in(_jax.random.key(seed), 7919)
    shape, _ = _output_shape()
    out = dict(inp)
    out["loss_target"] = _jax.random.normal(_jax.random.fold_in(key, 0), shape, _jnp.float32)
    for i, name in enumerate(TWIN_WEIGHTS):
        w = inp[name].astype(_jnp.float32)
        if MOMENT_SCALE is None:
            s = _jnp.sqrt(_jnp.mean(_jnp.square(w)) + 1e-30)
        else:
            s = MOMENT_SCALE[name]
        km, kv = _jax.random.split(_jax.random.fold_in(key, i + 1))
        out[name] = w
        out["m_" + name] = s * _jax.random.normal(km, w.shape, _jnp.float32)
        out["v_" + name] = (s * s) * _jax.random.uniform(kv, w.shape, _jnp.float32, 0.5, 1.5)
    if N_MICROBATCH > 1:
        for name, axis in PER_EXAMPLE_BATCH_AXIS.items():
            out[name] = _to_microbatches(out[name], axis)
    return {'x': out['x'], 'p': out['p'], 'a_pw1_w': out['a_pw1_w'], 'a_pw1_b': out['a_pw1_b'], 'a_dw_w': out['a_dw_w'], 'a_dw_b': out['a_dw_b'], 'a_ln_g': out['a_ln_g'], 'a_ln_b': out['a_ln_b'], 'a_pw2_w': out['a_pw2_w'], 'a_pw2_b': out['a_pw2_b'], 'b_wq': out['b_wq'], 'kv_wk': out['kv_wk'], 'kv_wv': out['kv_wv'], 'b_wo': out['b_wo'], 'ln_mix_g': out['ln_mix_g'], 'ln_mix_b': out['ln_mix_b'], 'ffn_w_up': out['ffn_w_up'], 'ffn_w_gate': out['ffn_w_gate'], 'ffn_conv_w': out['ffn_conv_w'], 'ffn_conv_b': out['ffn_conv_b'], 'ffn_w_down': out['ffn_w_down'], 'ple_w_gate': out['ple_w_gate'], 'ple_w_proj': out['ple_w_proj'], 'ln_ffn_g': out['ln_ffn_g'], 'ln_ffn_b': out['ln_ffn_b'], 'loss_target': out['loss_target'], 'm_a_pw1_w': out['m_a_pw1_w'], 'm_a_pw1_b': out['m_a_pw1_b'], 'm_a_dw_w': out['m_a_dw_w'], 'm_a_dw_b': out['m_a_dw_b'], 'm_a_ln_g': out['m_a_ln_g'], 'm_a_ln_b': out['m_a_ln_b'], 'm_a_pw2_w': out['m_a_pw2_w'], 'm_a_pw2_b': out['m_a_pw2_b'], 'm_b_wq': out['m_b_wq'], 'm_kv_wk': out['m_kv_wk'], 'm_kv_wv': out['m_kv_wv'], 'm_b_wo': out['m_b_wo'], 'm_ln_mix_g': out['m_ln_mix_g'], 'm_ln_mix_b': out['m_ln_mix_b'], 'm_ffn_w_up': out['m_ffn_w_up'], 'm_ffn_w_gate': out['m_ffn_w_gate'], 'm_ffn_conv_w': out['m_ffn_conv_w'], 'm_ffn_conv_b': out['m_ffn_conv_b'], 'm_ffn_w_down': out['m_ffn_w_down'], 'm_ple_w_gate': out['m_ple_w_gate'], 'm_ple_w_proj': out['m_ple_w_proj'], 'm_ln_ffn_g': out['m_ln_ffn_g'], 'm_ln_ffn_b': out['m_ln_ffn_b'], 'v_a_pw1_w': out['v_a_pw1_w'], 'v_a_pw1_b': out['v_a_pw1_b'], 'v_a_dw_w': out['v_a_dw_w'], 'v_a_dw_b': out['v_a_dw_b'], 'v_a_ln_g': out['v_a_ln_g'], 'v_a_ln_b': out['v_a_ln_b'], 'v_a_pw2_w': out['v_a_pw2_w'], 'v_a_pw2_b': out['v_a_pw2_b'], 'v_b_wq': out['v_b_wq'], 'v_kv_wk': out['v_kv_wk'], 'v_kv_wv': out['v_kv_wv'], 'v_b_wo': out['v_b_wo'], 'v_ln_mix_g': out['v_ln_mix_g'], 'v_ln_mix_b': out['v_ln_mix_b'], 'v_ffn_w_up': out['v_ffn_w_up'], 'v_ffn_w_gate': out['v_ffn_w_gate'], 'v_ffn_conv_w': out['v_ffn_conv_w'], 'v_ffn_conv_b': out['v_ffn_conv_b'], 'v_ffn_w_down': out['v_ffn_w_down'], 'v_ple_w_gate': out['v_ple_w_gate'], 'v_ple_w_proj': out['v_ple_w_proj'], 'v_ln_ffn_g': out['v_ln_ffn_g'], 'v_ln_ffn_b': out['v_ln_ffn_b']}


def _loss(weights, diff, rest, loss_target):
    with _jax.named_scope("forward"):
        args = {**rest, TWIN_DIFF_INPUT: diff, **{k: w.astype(_WEIGHT_DTYPES[k]) for k, w in weights.items()}}
        y = _forward(args)
    with _jax.named_scope("loss_head"):
        err = _jnp.square(y.astype(_jnp.float32) - loss_target)
        return 0.5 * _jnp.sum(_jnp.mean(err, axis=-1)) if err.ndim else 0.5 * err


def _adamw(w, g, m, v):
    m = ADAM_B1 * m + (1.0 - ADAM_B1) * g
    v = ADAM_B2 * v + (1.0 - ADAM_B2) * _jnp.square(g)
    m_hat = m / (1.0 - ADAM_B1 ** ADAM_STEP)
    v_hat = v / (1.0 - ADAM_B2 ** ADAM_STEP)
    delta = -ADAM_LR * (m_hat / (_jnp.sqrt(v_hat) + ADAM_EPS) + ADAM_WD * w)
    return delta, m, v


def reference(x, p, a_pw1_w, a_pw1_b, a_dw_w, a_dw_b, a_ln_g, a_ln_b, a_pw2_w, a_pw2_b, b_wq, kv_wk, kv_wv, b_wo, ln_mix_g, ln_mix_b, ffn_w_up, ffn_w_gate, ffn_conv_w, ffn_conv_b, ffn_w_down, ple_w_gate, ple_w_proj, ln_ffn_g, ln_ffn_b, loss_target, m_a_pw1_w, m_a_pw1_b, m_a_dw_w, m_a_dw_b, m_a_ln_g, m_a_ln_b, m_a_pw2_w, m_a_pw2_b, m_b_wq, m_kv_wk, m_kv_wv, m_b_wo, m_ln_mix_g, m_ln_mix_b, m_ffn_w_up, m_ffn_w_gate, m_ffn_conv_w, m_ffn_conv_b, m_ffn_w_down, m_ple_w_gate, m_ple_w_proj, m_ln_ffn_g, m_ln_ffn_b, v_a_pw1_w, v_a_pw1_b, v_a_dw_w, v_a_dw_b, v_a_ln_g, v_a_ln_b, v_a_pw2_w, v_a_pw2_b, v_b_wq, v_kv_wk, v_kv_wv, v_b_wo, v_ln_mix_g, v_ln_mix_b, v_ffn_w_up, v_ffn_w_gate, v_ffn_conv_w, v_ffn_conv_b, v_ffn_w_down, v_ple_w_gate, v_ple_w_proj, v_ln_ffn_g, v_ln_ffn_b):
    given = dict(x=x, p=p, a_pw1_w=a_pw1_w, a_pw1_b=a_pw1_b, a_dw_w=a_dw_w, a_dw_b=a_dw_b, a_ln_g=a_ln_g, a_ln_b=a_ln_b, a_pw2_w=a_pw2_w, a_pw2_b=a_pw2_b, b_wq=b_wq, kv_wk=kv_wk, kv_wv=kv_wv, b_wo=b_wo, ln_mix_g=ln_mix_g, ln_mix_b=ln_mix_b, ffn_w_up=ffn_w_up, ffn_w_gate=ffn_w_gate, ffn_conv_w=ffn_conv_w, ffn_conv_b=ffn_conv_b, ffn_w_down=ffn_w_down, ple_w_gate=ple_w_gate, ple_w_proj=ple_w_proj, ln_ffn_g=ln_ffn_g, ln_ffn_b=ln_ffn_b, loss_target=loss_target, m_a_pw1_w=m_a_pw1_w, m_a_pw1_b=m_a_pw1_b, m_a_dw_w=m_a_dw_w, m_a_dw_b=m_a_dw_b, m_a_ln_g=m_a_ln_g, m_a_ln_b=m_a_ln_b, m_a_pw2_w=m_a_pw2_w, m_a_pw2_b=m_a_pw2_b, m_b_wq=m_b_wq, m_kv_wk=m_kv_wk, m_kv_wv=m_kv_wv, m_b_wo=m_b_wo, m_ln_mix_g=m_ln_mix_g, m_ln_mix_b=m_ln_mix_b, m_ffn_w_up=m_ffn_w_up, m_ffn_w_gate=m_ffn_w_gate, m_ffn_conv_w=m_ffn_conv_w, m_ffn_conv_b=m_ffn_conv_b, m_ffn_w_down=m_ffn_w_down, m_ple_w_gate=m_ple_w_gate, m_ple_w_proj=m_ple_w_proj, m_ln_ffn_g=m_ln_ffn_g, m_ln_ffn_b=m_ln_ffn_b, v_a_pw1_w=v_a_pw1_w, v_a_pw1_b=v_a_pw1_b, v_a_dw_w=v_a_dw_w, v_a_dw_b=v_a_dw_b, v_a_ln_g=v_a_ln_g, v_a_ln_b=v_a_ln_b, v_a_pw2_w=v_a_pw2_w, v_a_pw2_b=v_a_pw2_b, v_b_wq=v_b_wq, v_kv_wk=v_kv_wk, v_kv_wv=v_kv_wv, v_b_wo=v_b_wo, v_ln_mix_g=v_ln_mix_g, v_ln_mix_b=v_ln_mix_b, v_ffn_w_up=v_ffn_w_up, v_ffn_w_gate=v_ffn_w_gate, v_ffn_conv_w=v_ffn_conv_w, v_ffn_conv_b=v_ffn_conv_b, v_ffn_w_down=v_ffn_w_down, v_ple_w_gate=v_ple_w_gate, v_ple_w_proj=v_ple_w_proj, v_ln_ffn_g=v_ln_ffn_g, v_ln_ffn_b=v_ln_ffn_b)
    weights = {n: given[n] for n in TWIN_WEIGHTS}
    shared = {n: given[n] for n in SHARED_INPUTS}
    per_example = {n: given[n] for n in ['x', 'p']}
    grad_fn = _jax.value_and_grad(_loss, argnums=(0, 1))

    def one_microbatch(ex, loss_target):
        ex = dict(ex)
        diff = ex.pop(TWIN_DIFF_INPUT)
        return grad_fn(weights, diff, {**shared, **ex}, loss_target)

    if N_MICROBATCH == 1:
        loss, (grad_w, grad_x) = one_microbatch(per_example, given["loss_target"])
    else:
        def body(carry, xs):
            loss_sum, grad_sum = carry
            l_k, (gw_k, gx_k) = one_microbatch(xs[0], xs[1])
            with _jax.named_scope("update"):
                return (loss_sum + l_k, _jax.tree.map(_jnp.add, grad_sum, gw_k)), gx_k

        init = (_jnp.zeros((), _jnp.float32), _jax.tree.map(_jnp.zeros_like, weights))
        (loss, grad_w), grad_x = _jax.lax.scan(body, init, (per_example, given["loss_target"]))
    with _jax.named_scope("update"):
        delta_w, new_m, new_v = {}, {}, {}
        for n in TWIN_WEIGHTS:
            delta_w[n], new_m[n], new_v[n] = _adamw(weights[n], grad_w[n], given["m_" + n], given["v_" + n])
    return (loss, grad_x, *[grad_w[n] for n in TWIN_WEIGHTS], *[delta_w[n] for n in TWIN_WEIGHTS],
            *[new_m[n] for n in TWIN_WEIGHTS], *[new_v[n] for n in TWIN_WEIGHTS])
```

```python
import functools
import math

import jax
import jax.numpy as jnp
from jax import lax
from jax.experimental import pallas as pl
from jax.experimental.pallas import tpu as pltpu

F32 = jnp.float32
BF16 = jnp.bfloat16
MESH = pl.DeviceIdType.MESH

N_DEV = 8
DEPTH = 4
N_A = 2
HEAD_DIM = 64
Q_BLOCK = 128
CONV_W = 31
FFN_CONV_W = 3
LN_EPS = 1e-5
DN_ALPHA = (2.0 * DEPTH) ** 0.25
ADAM_LR = 0.001
ADAM_B1 = 0.9
ADAM_B2 = 0.999
ADAM_EPS = 1e-08
ADAM_WD = 0.01
ADAM_STEP = 10

LANES = 128
SUBLANES = 8
PACK_W = 1024
VMEM_LIMIT = 56 * 1024 * 1024

BIG = (("a_pw1_w", 2), ("a_pw2_w", 1), ("b_wq", 1), ("kv_wk", 0), ("kv_wv", 0), ("b_wo", 1),
       ("ffn_w_up", 2), ("ffn_w_gate", 2), ("ffn_w_down", 1), ("ple_w_gate", 1), ("ple_w_proj", 2))
SMALL = (("a_pw1_b", 1), ("a_dw_w", 2), ("a_dw_b", 1), ("a_ln_g", 1), ("a_ln_b", 1), ("a_pw2_b", 1),
         ("ffn_conv_w", 2))
REPL = ("ln_mix_g", "ln_mix_b", "ffn_conv_b", "ln_ffn_g", "ln_ffn_b")
WEIGHTS = ("a_pw1_w", "a_pw1_b", "a_dw_w", "a_dw_b", "a_ln_g", "a_ln_b", "a_pw2_w", "a_pw2_b", "b_wq", "kv_wk",
           "kv_wv", "b_wo", "ln_mix_g", "ln_mix_b", "ffn_w_up", "ffn_w_gate", "ffn_conv_w", "ffn_conv_b",
           "ffn_w_down", "ple_w_gate", "ple_w_proj", "ln_ffn_g", "ln_ffn_b")


def _cp(*sem):
    return pltpu.CompilerParams(dimension_semantics=sem, vmem_limit_bytes=VMEM_LIMIT)


def _pick(dim, target, align=LANES):
    if dim <= target:
        return dim
    t = (target // align) * align
    while t >= align:
        if dim % t == 0:
            return t
        t -= align
    return dim


_DOT_DIMS = {"nn": (((1,), (0,)), ((), ())), "nt": (((1,), (1,)), ((), ())), "tn": (((0,), (0,)), ((), ()))}


def mm(a, b, mode, *, bias=None, add=None, add_scale=1.0, out_dtype=F32, name):
    if mode == "tn":
        K, M = a.shape
    else:
        M, K = a.shape
    N = b.shape[0] if mode == "nt" else b.shape[1]
    tm, tn, tk = _pick(M, 512), _pick(N, 1536), _pick(K, 1536)
    nk = K // tk
    dims = _DOT_DIMS[mode]

    def body(*refs):
        a_ref, b_ref = refs[0], refs[1]
        pos = 2
        bias_ref = add_ref = None
        if bias is not None:
            bias_ref = refs[pos]
            pos += 1
        if add is not None:
            add_ref = refs[pos]
            pos += 1
        o_ref, acc_ref = refs[pos], refs[pos + 1]
        k = pl.program_id(2)

        @pl.when(k == 0)
        def _():
            acc_ref[...] = jnp.zeros_like(acc_ref)

        acc_ref[...] += lax.dot_general(a_ref[...].astype(BF16), b_ref[...].astype(BF16), dims,
                                        preferred_element_type=F32)

        @pl.when(k == nk - 1)
        def _():
            r = acc_ref[...]
            if bias_ref is not None:
                r = r + bias_ref[...]
            if add_ref is not None:
                r = r + add_scale * add_ref[...].astype(F32)
            o_ref[...] = r.astype(o_ref.dtype)

    a_spec = pl.BlockSpec((tk, tm), lambda i, j, k: (k, i)) if mode == "tn" else pl.BlockSpec((tm, tk), lambda i, j, k: (i, k))
    b_spec = pl.BlockSpec((tn, tk), lambda i, j, k: (j, k)) if mode == "nt" else pl.BlockSpec((tk, tn), lambda i, j, k: (k, j))
    in_specs, args = [a_spec, b_spec], [a, b]
    if bias is not None:
        in_specs.append(pl.BlockSpec((1, tn), lambda i, j, k: (0, j)))
        args.append(bias)
    if add is not None:
        in_specs.append(pl.BlockSpec((tm, tn), lambda i, j, k: (i, j)))
        args.append(add)
    return pl.pallas_call(
        body, grid=(M // tm, N // tn, nk), in_specs=in_specs,
        out_specs=pl.BlockSpec((tm, tn), lambda i, j, k: (i, j)),
        out_shape=jax.ShapeDtypeStruct((M, N), out_dtype),
        scratch_shapes=[pltpu.VMEM((tm, tn), F32)],
        compiler_params=_cp("parallel", "parallel", "arbitrary"), name=name)(*args)


def _rows(body, *, n_rows, tm, row_ins, full_ins=(), row_outs=(), acc_outs=(), scratch=(), reverse=False, name):
    n = n_rows // tm

    def rmap(i):
        return (n - 1 - i, 0) if reverse else (i, 0)

    in_specs = [pl.BlockSpec((tm, a.shape[1]), rmap) for a in row_ins]
    in_specs += [pl.BlockSpec(a.shape, lambda i, nd=a.ndim: (0,) * nd) for a in full_ins]
    out_shape = [jax.ShapeDtypeStruct((n_rows, w), dt) for (w, dt) in row_outs]
    out_shape += [jax.ShapeDtypeStruct(s, dt) for (s, dt) in acc_outs]
    out_specs = [pl.BlockSpec((tm, w), rmap) for (w, dt) in row_outs]
    out_specs += [pl.BlockSpec(s, lambda i, nd=len(s): (0,) * nd) for (s, dt) in acc_outs]
    return pl.pallas_call(
        functools.partial(body, n), grid=(n,), in_specs=in_specs, out_specs=out_specs, out_shape=out_shape,
        scratch_shapes=list(scratch), compiler_params=_cp("arbitrary"), name=name)(*row_ins, *full_ins)


def _sigmoid(x):
    return 1.0 / (1.0 + jnp.exp(-x))


def _ln_hat(r):
    mu = jnp.mean(r, axis=-1, keepdims=True)
    xc = r - mu
    var = jnp.mean(xc * xc, axis=-1, keepdims=True)
    rstd = lax.rsqrt(var + LN_EPS)
    return xc * rstd, rstd


def _ln_back(xhat, rstd, g, dy):
    dxh = dy * g
    m1 = jnp.mean(dxh, axis=-1, keepdims=True)
    m2 = jnp.mean(dxh * xhat, axis=-1, keepdims=True)
    return rstd * (dxh - m1 - xhat * m2)


def _colsum(x):
    return jnp.sum(x, axis=0, keepdims=True)


def _acc(i, ref, val):
    @pl.when(i == 0)
    def _():
        ref[...] = val

    @pl.when(i > 0)
    def _():
        ref[...] += val


def res_ln(x, mix, g, b, *, ple=None, name):
    S, D = x.shape

    def body(n, *refs):
        if ple is None:
            x_ref, m_ref, g_ref, b_ref, r_ref, y_ref = refs
            r = DN_ALPHA * x_ref[...] + m_ref[...]
        else:
            x_ref, m_ref, pgl_ref, pp_ref, g_ref, b_ref, r_ref, y_ref = refs
            r = DN_ALPHA * x_ref[...] + m_ref[...] + _sigmoid(pgl_ref[...]) * pp_ref[...]
        xhat, _ = _ln_hat(r)
        r_ref[...] = r
        y_ref[...] = xhat * g_ref[...] + b_ref[...]

    row_ins = [x, mix] + ([] if ple is None else list(ple))
    return _rows(body, n_rows=S, tm=_pick(S, 256, SUBLANES), row_ins=row_ins, full_ins=[g, b],
                 row_outs=[(D, F32), (D, F32)], name=name)


def ln_bwd(r, g, dy, *, name):
    S, D = r.shape

    def body(n, r_ref, dy_ref, g_ref, dr_ref, dg_ref, db_ref, ds_ref):
        i = pl.program_id(0)
        xhat, rstd = _ln_hat(r_ref[...])
        dy_v = dy_ref[...]
        dr = _ln_back(xhat, rstd, g_ref[...], dy_v)
        dr_ref[...] = dr
        _acc(i, dg_ref, _colsum(dy_v * xhat))
        _acc(i, db_ref, _colsum(dy_v))
        _acc(i, ds_ref, _colsum(dr))

    return _rows(body, n_rows=S, tm=_pick(S, 256, SUBLANES), row_ins=[r, dy], full_ins=[g],
                 row_outs=[(D, F32)], acc_outs=[((1, D), F32)] * 3, name=name)


def glu_fwd(h1, *, name):
    S, D2 = h1.shape
    D = D2 // 2

    def body(n, h_ref, o_ref):
        o_ref[...] = h_ref[:, :D] * _sigmoid(h_ref[:, D:])

    return _rows(body, n_rows=S, tm=_pick(S, 256, SUBLANES), row_ins=[h1], row_outs=[(D, F32)], name=name)[0]


def glu_bwd(h1, dh2, *, name):
    S, D2 = h1.shape
    D = D2 // 2

    def body(n, h_ref, d_ref, o_ref, s_ref):
        i = pl.program_id(0)
        a, sg, d = h_ref[:, :D], _sigmoid(h_ref[:, D:]), d_ref[...]
        da = d * sg
        dg = d * a * sg * (1.0 - sg)
        o_ref[:, :D] = da.astype(o_ref.dtype)
        o_ref[:, D:] = dg.astype(o_ref.dtype)
        _acc(i, s_ref, jnp.concatenate([_colsum(da), _colsum(dg)], axis=1))

    return _rows(body, n_rows=S, tm=_pick(S, 256, SUBLANES), row_ins=[h1, dh2], row_outs=[(D2, BF16)],
                 acc_outs=[((1, D2), F32)], name=name)


CONV_ROWS = 32
CONV_LANES = 256


def _halo(k):
    return -(-(k - 1) // SUBLANES) * SUBLANES


def _conv_taps(buf_ref, w_ref, offs, tm, width, emit):
    rows = min(CONV_ROWS, tm)
    for lc in range(0, width, CONV_LANES):
        lw = min(CONV_LANES, width - lc)
        for rc in range(0, tm, rows):
            acc = None
            for k, off in enumerate(offs):
                t = buf_ref[pl.ds(off + rc, rows), pl.ds(lc, lw)] * w_ref[pl.ds(k, 1), pl.ds(lc, lw)]
                acc = t if acc is None else acc + t
            emit(rc, lc, lw, rows, acc)


def _fill_causal(i, buf_ref, x_ref, halo, tm):
    @pl.when(i == 0)
    def _():
        buf_ref[pl.ds(0, halo), :] = jnp.zeros((halo, buf_ref.shape[1]), F32)

    @pl.when(i > 0)
    def _():
        buf_ref[pl.ds(0, halo), :] = buf_ref[pl.ds(tm, halo), :]

    buf_ref[pl.ds(halo, tm), :] = x_ref[...]


def conv_ln_silu_fwd(x, w, b, g, beta, *, name):
    S, C = x.shape
    K = w.shape[0]
    halo = _halo(K)
    tm = _pick(S, 256, SUBLANES)
    offs = [halo - (K - 1) + k for k in range(K)]

    def body(n, x_ref, w_ref, b_ref, g_ref, beta_ref, h3_ref, h5_ref, buf_ref):
        i = pl.program_id(0)
        _fill_causal(i, buf_ref, x_ref, halo, tm)

        def emit(rc, lc, lw, rows, acc):
            h3_ref[pl.ds(rc, rows), pl.ds(lc, lw)] = acc + b_ref[:, pl.ds(lc, lw)]

        _conv_taps(buf_ref, w_ref, offs, tm, C, emit)
        xhat, _ = _ln_hat(h3_ref[...])
        h4 = xhat * g_ref[...] + beta_ref[...]
        h5_ref[...] = (h4 * _sigmoid(h4)).astype(h5_ref.dtype)

    return _rows(body, n_rows=S, tm=tm, row_ins=[x], full_ins=[w, b, g, beta], row_outs=[(C, F32), (C, BF16)],
                 scratch=[pltpu.VMEM((tm + halo, C), F32)], name=name)


def conv_act_fwd(gp, u, w, b, *, name):
    S, C = gp.shape
    K = w.shape[0]
    halo = _halo(K)
    tm = _pick(S, 256, SUBLANES)
    offs = [halo - (K - 1) + k for k in range(K)]

    def body(n, x_ref, u_ref, w_ref, b_ref, g_ref, hh_ref, buf_ref):
        i = pl.program_id(0)
        _fill_causal(i, buf_ref, x_ref, halo, tm)

        def emit(rc, lc, lw, rows, acc):
            gv = acc + b_ref[:, pl.ds(lc, lw)]
            g_ref[pl.ds(rc, rows), pl.ds(lc, lw)] = gv
            hh_ref[pl.ds(rc, rows), pl.ds(lc, lw)] = (gv * _sigmoid(gv) * u_ref[pl.ds(rc, rows), pl.ds(lc, lw)]).astype(hh_ref.dtype)

        _conv_taps(buf_ref, w_ref, offs, tm, C, emit)

    return _rows(body, n_rows=S, tm=tm, row_ins=[gp, u], full_ins=[w, b], row_outs=[(C, F32), (C, BF16)],
                 scratch=[pltpu.VMEM((tm + halo, C), F32)], name=name)


def conv_bwd_x(dy, w, *, out_dtype, name):
    S, C = dy.shape
    K = w.shape[0]
    halo = _halo(K)
    tm = _pick(S, 256, SUBLANES)
    offs = [K - 1 - k for k in range(K)]

    def body(n, dy_ref, w_ref, dx_ref, buf_ref):
        i = pl.program_id(0)

        @pl.when(i == 0)
        def _():
            buf_ref[pl.ds(tm, halo), :] = jnp.zeros((halo, C), F32)

        @pl.when(i > 0)
        def _():
            buf_ref[pl.ds(tm, halo), :] = buf_ref[pl.ds(0, halo), :]

        buf_ref[pl.ds(0, tm), :] = dy_ref[...]

        def emit(rc, lc, lw, rows, acc):
            dx_ref[pl.ds(rc, rows), pl.ds(lc, lw)] = acc.astype(dx_ref.dtype)

        _conv_taps(buf_ref, w_ref, offs, tm, C, emit)

    return _rows(body, n_rows=S, tm=tm, row_ins=[dy], full_ins=[w], row_outs=[(C, out_dtype)],
                 scratch=[pltpu.VMEM((tm + halo, C), F32)], reverse=True, name=name)[0]


def conv_bwd_w(x, dy, K, *, name):
    S, C = x.shape
    halo = _halo(K)
    tm = _pick(S, 256, SUBLANES)
    offs = [halo - (K - 1) + k for k in range(K)]
    rows = min(CONV_ROWS, tm)

    def body(n, x_ref, dy_ref, dw_ref, db_ref, buf_ref, acc_ref):
        i = pl.program_id(0)
        _fill_causal(i, buf_ref, x_ref, halo, tm)

        @pl.when(i == 0)
        def _():
            acc_ref[...] = jnp.zeros_like(acc_ref)

        for lc in range(0, C, CONV_LANES):
            lw = min(CONV_LANES, C - lc)
            for k, off in enumerate(offs):
                s = None
                for rc in range(0, tm, rows):
                    t = dy_ref[pl.ds(rc, rows), pl.ds(lc, lw)] * buf_ref[pl.ds(off + rc, rows), pl.ds(lc, lw)]
                    s = t if s is None else s + t
                s8 = s[0:SUBLANES]
                for q in range(1, rows // SUBLANES):
                    s8 = s8 + s[q * SUBLANES:(q + 1) * SUBLANES]
                acc_ref[pl.ds(k * SUBLANES, SUBLANES), pl.ds(lc, lw)] += s8
        _acc(i, db_ref, _colsum(dy_ref[...]))

        @pl.when(i == n - 1)
        def _():
            for k in range(K):
                dw_ref[pl.ds(k, 1), :] = _colsum(acc_ref[pl.ds(k * SUBLANES, SUBLANES), :])

    return _rows(body, n_rows=S, tm=tm, row_ins=[x, dy], acc_outs=[((K, C), F32), ((1, C), F32)],
                 scratch=[pltpu.VMEM((tm + halo, C), F32), pltpu.VMEM((K * SUBLANES, C), F32)], name=name)


def ln_silu_bwd(h3, g, beta, dh5, *, name):
    S, C = h3.shape

    def body(n, h_ref, d_ref, g_ref, beta_ref, dh_ref, dg_ref, db_ref):
        i = pl.program_id(0)
        xhat, rstd = _ln_hat(h_ref[...])
        h4 = xhat * g_ref[...] + beta_ref[...]
        sg = _sigmoid(h4)
        dh4 = d_ref[...] * sg * (1.0 + h4 * (1.0 - sg))
        dh_ref[...] = _ln_back(xhat, rstd, g_ref[...], dh4)
        _acc(i, dg_ref, _colsum(dh4 * xhat))
        _acc(i, db_ref, _colsum(dh4))

    return _rows(body, n_rows=S, tm=_pick(S, 256, SUBLANES), row_ins=[h3, dh5], full_ins=[g, beta],
                 row_outs=[(C, F32)], acc_outs=[((1, C), F32)] * 2, name=name)


def ffn_act_bwd(dhh, u, g, *, name):
    S, C = u.shape

    def body(n, d_ref, u_ref, g_ref, du_ref, dg_ref):
        d, gv = d_ref[...], g_ref[...]
        sg = _sigmoid(gv)
        du_ref[...] = (d * gv * sg).astype(du_ref.dtype)
        dg_ref[...] = d * u_ref[...] * sg * (1.0 + gv * (1.0 - sg))

    return _rows(body, n_rows=S, tm=_pick(S, 256, SUBLANES), row_ins=[dhh, u, g], row_outs=[(C, BF16), (C, F32)],
                 name=name)


def ple_bwd(dr, pgl, pp, *, name):
    S, D = dr.shape

    def body(n, d_ref, l_ref, p_ref, dpp_ref, dpl_ref):
        d, sg = d_ref[...], _sigmoid(l_ref[...])
        dpp_ref[...] = (d * sg).astype(dpp_ref.dtype)
        dpl_ref[...] = (d * p_ref[...] * sg * (1.0 - sg)).astype(dpl_ref.dtype)

    return _rows(body, n_rows=S, tm=_pick(S, 256, SUBLANES), row_ins=[dr, pgl, pp], row_outs=[(D, BF16), (D, BF16)],
                 name=name)


def loss_grad(y, target, *, name):
    S, D = y.shape

    def body(n, y_ref, t_ref, dy_ref, l_ref):
        i = pl.program_id(0)
        e = y_ref[...] - t_ref[...]
        dy_ref[...] = e * (1.0 / D)
        s = jnp.sum(_colsum(e * e), axis=1, keepdims=True) * (0.5 / D)
        _acc(i, l_ref, jnp.broadcast_to(s, (1, LANES)))

    return _rows(body, n_rows=S, tm=_pick(S, 256, SUBLANES), row_ins=[y, target], row_outs=[(D, F32)],
                 acc_outs=[((1, LANES), F32)], name=name)


def _attn_consts():
    lane = lax.broadcasted_iota(jnp.int32, (1, LANES), 1)
    heads = (lane < HEAD_DIM, lane >= HEAD_DIM)
    row = lax.broadcasted_iota(jnp.int32, (Q_BLOCK, Q_BLOCK), 0)
    col = lax.broadcasted_iota(jnp.int32, (Q_BLOCK, Q_BLOCK), 1)
    causal = col < row
    return heads, row, col, causal


def _tri(cond):
    return jnp.where(cond, 1.0, 0.0).astype(BF16)


def _keysum(x, tri):
    hi = x.astype(BF16)
    lo = (x - hi.astype(F32)).astype(BF16)
    return jnp.dot(hi, tri, preferred_element_type=F32) + jnp.dot(lo, tri, preferred_element_type=F32)


def _sb_logits(qh, k2, mask, scale):
    z = lax.dot_general(qh, k2, _DOT_DIMS["nt"], preferred_element_type=F32) * scale
    lr = -(jnp.maximum(z, 0.0) + jnp.log(1.0 + jnp.exp(-jnp.abs(z))))
    return z, lr, (lr if mask is None else jnp.where(mask, lr, 0.0))


def attn_fwd(q, k, v, *, name):
    S, D = q.shape
    nb = S // Q_BLOCK
    scale = 1.0 / math.sqrt(HEAD_DIM)

    def body(q_ref, k_ref, v_ref, o_ref, tot_ref):
        heads, row, col, causal = _attn_consts()
        above = _tri(row > col)

        def kstep(j, carry, qh, mask):
            acc, cls = carry[0], list(carry[1:])
            c0 = pl.multiple_of(j * Q_BLOCK, Q_BLOCK)
            k2, v2 = k_ref[pl.ds(c0, Q_BLOCK), :], v_ref[pl.ds(c0, Q_BLOCK), :]
            for h in range(2):
                z, lr, l = _sb_logits(qh[h], k2, mask, scale)
                a = jnp.exp(z + lr + _keysum(l, above) + cls[h])
                if mask is not None:
                    a = jnp.where(mask, a, 0.0)
                vh = jnp.where(heads[h], v2, jnp.zeros_like(v2))
                acc = acc + jnp.dot(a.astype(BF16), vh, preferred_element_type=F32)
                cls[h] = cls[h] + jnp.sum(l, axis=1, keepdims=True)
            return (acc, *cls)

        def qblock(i, _):
            r0 = pl.multiple_of(i * Q_BLOCK, Q_BLOCK)
            q2 = q_ref[pl.ds(r0, Q_BLOCK), :]
            qh = [jnp.where(m, q2, jnp.zeros_like(q2)) for m in heads]
            zc = jnp.zeros((Q_BLOCK, 1), F32)
            carry = kstep(i, (jnp.zeros((Q_BLOCK, LANES), F32), zc, zc), qh, causal)
            carry = lax.fori_loop(0, i, lambda jj, c: kstep(i - 1 - jj, c, qh, None), carry)
            o_ref[pl.ds(r0, Q_BLOCK), :] = carry[0]
            tot_ref[pl.ds(r0, Q_BLOCK), :] = jnp.where(heads[0], carry[1], carry[2])
            return 0

        lax.fori_loop(0, nb, qblock, 0)

    spec = pl.BlockSpec((S, LANES), lambda h: (0, h))
    return pl.pallas_call(body, grid=(D // LANES,), in_specs=[spec] * 3, out_specs=[spec] * 2,
                          out_shape=[jax.ShapeDtypeStruct((S, D), F32)] * 2, compiler_params=_cp("parallel"),
                          name=name)(q, k, v)


def attn_bwd(q, k, v, tot, do, dk0, dv0, *, name):
    S, D = q.shape
    nb = S // Q_BLOCK
    scale = 1.0 / math.sqrt(HEAD_DIM)
    has_init = dk0 is not None

    def body(*refs):
        if has_init:
            q_ref, k_ref, v_ref, tot_ref, do_ref, dk0_ref, dv0_ref, dq_ref, dk_ref, dv_ref = refs
            dk_ref[...] = dk0_ref[...]
            dv_ref[...] = dv0_ref[...]
        else:
            q_ref, k_ref, v_ref, tot_ref, do_ref, dq_ref, dk_ref, dv_ref = refs
            dk_ref[...] = jnp.zeros_like(dk_ref)
            dv_ref[...] = jnp.zeros_like(dv_ref)
        heads, row, col, causal = _attn_consts()
        upto = _tri(row <= col)
        before = _tri(row < col)

        def kstep(j, carry, qh, doh, totl, mask):
            dq = carry[0]
            pls, pgs = list(carry[1:3]), list(carry[3:5])
            c0 = pl.multiple_of(j * Q_BLOCK, Q_BLOCK)
            k2, v2 = k_ref[pl.ds(c0, Q_BLOCK), :], v_ref[pl.ds(c0, Q_BLOCK), :]
            dk_t = jnp.zeros((Q_BLOCK, LANES), F32)
            dv_t = jnp.zeros((Q_BLOCK, LANES), F32)
            for h in range(2):
                z, lr, l = _sb_logits(qh[h], k2, mask, scale)
                a = jnp.exp(z + lr + (totl[h] - pls[h] - _keysum(l, upto)))
                if mask is not None:
                    a = jnp.where(mask, a, 0.0)
                da = lax.dot_general(doh[h], v2, _DOT_DIMS["nt"], preferred_element_type=F32)
                g = a * da
                dz = g * jnp.exp(lr) - jnp.exp(z + lr) * (pgs[h] + _keysum(g, before))
                if mask is not None:
                    dz = jnp.where(mask, dz, 0.0)
                dzs = (dz * scale).astype(BF16)
                kh = jnp.where(heads[h], k2, jnp.zeros_like(k2))
                dq = dq + jnp.dot(dzs, kh, preferred_element_type=F32)
                dk_t = dk_t + lax.dot_general(dzs, qh[h], _DOT_DIMS["tn"], preferred_element_type=F32)
                dv_t = dv_t + lax.dot_general(a.astype(BF16), doh[h], _DOT_DIMS["tn"], preferred_element_type=F32)
                pls[h] = pls[h] + jnp.sum(l, axis=1, keepdims=True)
                pgs[h] = pgs[h] + jnp.sum(g, axis=1, keepdims=True)
            dk_ref[pl.ds(c0, Q_BLOCK), :] += dk_t
            dv_ref[pl.ds(c0, Q_BLOCK), :] += dv_t
            return (dq, *pls, *pgs)

        def qblock(i, _):
            r0 = pl.multiple_of(i * Q_BLOCK, Q_BLOCK)
            q2 = q_ref[pl.ds(r0, Q_BLOCK), :]
            do2 = do_ref[pl.ds(r0, Q_BLOCK), :].astype(BF16)
            tot2 = tot_ref[pl.ds(r0, Q_BLOCK), :]
            qh = [jnp.where(m, q2, jnp.zeros_like(q2)) for m in heads]
            doh = [jnp.where(m, do2, jnp.zeros_like(do2)) for m in heads]
            totl = [tot2[:, 0:1], tot2[:, HEAD_DIM:HEAD_DIM + 1]]
            zc = jnp.zeros((Q_BLOCK, 1), F32)
            carry = (jnp.zeros((Q_BLOCK, LANES), F32), zc, zc, zc, zc)
            carry = lax.fori_loop(0, i, lambda j, c: kstep(j, c, qh, doh, totl, None), carry)
            carry = kstep(i, carry, qh, doh, totl, causal)
            dq_ref[pl.ds(r0, Q_BLOCK), :] = carry[0].astype(dq_ref.dtype)
            return 0

        lax.fori_loop(0, nb, qblock, 0)

    spec = pl.BlockSpec((S, LANES), lambda h: (0, h))
    args = [q, k, v, tot, do] + ([dk0, dv0] if has_init else [])
    return pl.pallas_call(
        body, grid=(D // LANES,), in_specs=[spec] * len(args), out_specs=[spec] * 3,
        out_shape=[jax.ShapeDtypeStruct((S, D), BF16), jax.ShapeDtypeStruct((S, D), F32), jax.ShapeDtypeStruct((S, D), F32)],
        compiler_params=_cp("parallel"), name=name)(*args)


def _dev_index(px, py, pc):
    return 4 * px + 2 * py + pc


def all_gather(big, small):
    bufs = (big, small)

    def body(big_ref, small_ref, obig_ref, osmall_ref, send_sems, recv_sems, local_sems):
        x, y, c = lax.axis_index("x"), lax.axis_index("y"), lax.axis_index("c")
        me, sibling = (x, y, c), (x, y, 1 - c)
        chips = [(1 - x, y), (x, 1 - y), (1 - x, 1 - y)]
        ins, outs = (big_ref, small_ref), (obig_ref, osmall_ref)

        def copy(b, k, block, to, from_input=False):
            slot = outs[b].at[_dev_index(*block)]
            return pltpu.make_async_remote_copy(
                src_ref=ins[b] if from_input else slot, dst_ref=slot,
                send_sem=send_sems.at[7 * b + k], recv_sem=recv_sems.at[7 * b + k], device_id=to, device_id_type=MESH)

        mine = [pltpu.make_async_copy(ins[b], outs[b].at[_dev_index(*me)], local_sems.at[b]) for b in range(2)]
        for cp in mine:
            cp.start()
        first = []
        for b in range(2):
            first.append(copy(b, 0, me, sibling, from_input=True))
            first += [copy(b, 1 + j, me, (*chip, c), from_input=True) for j, chip in enumerate(chips)]
        for cp in first:
            cp.start()
        passed = []
        for j, chip in enumerate(chips):
            for b in range(2):
                copy(b, 1 + j, (*chip, c), me).wait_recv()
                fwd = copy(b, 4 + j, (*chip, c), sibling)
                fwd.start()
                passed.append(fwd)
        for b in range(2):
            copy(b, 0, sibling, me).wait_recv()
            for j, chip in enumerate(chips):
                copy(b, 4 + j, (*chip, 1 - c), me).wait_recv()
        for cp in first + passed:
            cp.wait_send()
        for cp in mine:
            cp.wait()

    any_spec = pl.BlockSpec(memory_space=pl.ANY)
    return pl.pallas_call(
        body, in_specs=[any_spec] * 2, out_specs=[any_spec] * 2,
        out_shape=[jax.ShapeDtypeStruct((N_DEV,) + b.shape, b.dtype) for b in bufs],
        scratch_shapes=[pltpu.SemaphoreType.DMA((14,)), pltpu.SemaphoreType.DMA((14,)), pltpu.SemaphoreType.DMA((2,))],
        name="all_gather_weights")(*bufs)


def exchange_grads(big, small):
    bufs = (big, small)

    def body(big_ref, small_ref, obig_ref, osmall_ref, send_sems, recv_sems, local_sems):
        x, y, c = lax.axis_index("x"), lax.axis_index("y"), lax.axis_index("c")
        me = _dev_index(x, y, c)
        ins, outs = (big_ref, small_ref), (obig_ref, osmall_ref)
        mine = [pltpu.make_async_copy(ins[b].at[me], outs[b].at[me], local_sems.at[b]) for b in range(2)]
        for cp in mine:
            cp.start()
        copies = []
        for flip in range(1, N_DEV):
            fx, fy, fc = (flip >> 2) & 1, (flip >> 1) & 1, flip & 1
            peer = (1 - x if fx else x, 1 - y if fy else y, 1 - c if fc else c)
            pidx = _dev_index(*peer)
            for b in range(2):
                k = 7 * b + flip - 1
                send = pltpu.make_async_remote_copy(
                    src_ref=ins[b].at[pidx], dst_ref=outs[b].at[me], send_sem=send_sems.at[k], recv_sem=recv_sems.at[k],
                    device_id=peer, device_id_type=MESH)
                land = pltpu.make_async_remote_copy(
                    src_ref=ins[b].at[pidx], dst_ref=outs[b].at[pidx], send_sem=send_sems.at[k], recv_sem=recv_sems.at[k],
                    device_id=peer, device_id_type=MESH)
                send.start()
                copies.append((send, land))
        for send, land in copies:
            land.wait_recv()
        for send, land in copies:
            send.wait_send()
        for cp in mine:
            cp.wait()

    any_spec = pl.BlockSpec(memory_space=pl.ANY)
    return pl.pallas_call(
        body, in_specs=[any_spec] * 2, out_specs=[any_spec] * 2,
        out_shape=[jax.ShapeDtypeStruct(b.shape, b.dtype) for b in bufs],
        scratch_shapes=[pltpu.SemaphoreType.DMA((14,)), pltpu.SemaphoreType.DMA((14,)), pltpu.SemaphoreType.DMA((2,))],
        name="exchange_grads")(*bufs)


def adamw(recv, w, m, v, *, name):
    _, R, W = recv.shape
    tr = _pick(R, 128, 16)
    c1 = 1.0 - ADAM_B1 ** ADAM_STEP
    c2 = 1.0 - ADAM_B2 ** ADAM_STEP

    def body(r_ref, w_ref, m_ref, v_ref, g_ref, d_ref, mo_ref, vo_ref):
        g = r_ref[0].astype(F32)
        for j in range(1, N_DEV):
            g = g + r_ref[j].astype(F32)
        mn = ADAM_B1 * m_ref[...] + (1.0 - ADAM_B1) * g
        vn = ADAM_B2 * v_ref[...] + (1.0 - ADAM_B2) * (g * g)
        g_ref[...] = g
        mo_ref[...] = mn
        vo_ref[...] = vn
        d_ref[...] = -ADAM_LR * ((mn / c1) / (jnp.sqrt(vn / c2) + ADAM_EPS) + ADAM_WD * w_ref[...])

    spec = pl.BlockSpec((tr, W), lambda i: (i, 0))
    return pl.pallas_call(
        body, grid=(R // tr,), in_specs=[pl.BlockSpec((N_DEV, tr, W), lambda i: (0, i, 0)), spec, spec, spec],
        out_specs=[spec] * 4, out_shape=[jax.ShapeDtypeStruct((R, W), F32)] * 4,
        compiler_params=_cp("parallel"), name=name)(recv, w, m, v)


def _pack(arrs, dtype, row_mult):
    flat = jnp.concatenate([a.reshape(-1).astype(dtype) for a in arrs])
    rows = -(-flat.shape[0] // PACK_W)
    rows = -(-rows // row_mult) * row_mult
    return jnp.pad(flat, (0, rows * PACK_W - flat.shape[0])).reshape(rows, PACK_W)


def _unpack(buf, shapes):
    lead = buf.shape[:-2]
    flat = buf.reshape(lead + (-1,))
    outs, off = [], 0
    for s in shapes:
        n = math.prod(s)
        outs.append(flat[..., off:off + n].reshape(lead + tuple(s)))
        off += n
    return outs


def _join(g, axis):
    g = jnp.moveaxis(g, 0, axis)
    return g.reshape(g.shape[:axis] + (g.shape[axis] * g.shape[axis + 1],) + g.shape[axis + 2:])


def _split(full, axis):
    s = full.shape
    g = full.reshape(s[:axis] + (N_DEV, s[axis] // N_DEV) + s[axis + 1:])
    return jnp.moveaxis(g, axis, 0)


def kernel(x, p, a_pw1_w, a_pw1_b, a_dw_w, a_dw_b, a_ln_g, a_ln_b, a_pw2_w, a_pw2_b, b_wq, kv_wk, kv_wv, b_wo, ln_mix_g, ln_mix_b, ffn_w_up, ffn_w_gate, ffn_conv_w, ffn_conv_b, ffn_w_down, ple_w_gate, ple_w_proj, ln_ffn_g, ln_ffn_b, loss_target, m_a_pw1_w, m_a_pw1_b, m_a_dw_w, m_a_dw_b, m_a_ln_g, m_a_ln_b, m_a_pw2_w, m_a_pw2_b, m_b_wq, m_kv_wk, m_kv_wv, m_b_wo, m_ln_mix_g, m_ln_mix_b, m_ffn_w_up, m_ffn_w_gate, m_ffn_conv_w, m_ffn_conv_b, m_ffn_w_down, m_ple_w_gate, m_ple_w_proj, m_ln_ffn_g, m_ln_ffn_b, v_a_pw1_w, v_a_pw1_b, v_a_dw_w, v_a_dw_b, v_a_ln_g, v_a_ln_b, v_a_pw2_w, v_a_pw2_b, v_b_wq, v_kv_wk, v_kv_wv, v_b_wo, v_ln_mix_g, v_ln_mix_b, v_ffn_w_up, v_ffn_w_gate, v_ffn_conv_w, v_ffn_conv_b, v_ffn_w_down, v_ple_w_gate, v_ple_w_proj, v_ln_ffn_g, v_ln_ffn_b):
    local = dict(a_pw1_w=a_pw1_w, a_pw1_b=a_pw1_b, a_dw_w=a_dw_w, a_dw_b=a_dw_b, a_ln_g=a_ln_g, a_ln_b=a_ln_b, a_pw2_w=a_pw2_w, a_pw2_b=a_pw2_b, b_wq=b_wq, kv_wk=kv_wk, kv_wv=kv_wv, b_wo=b_wo, ln_mix_g=ln_mix_g, ln_mix_b=ln_mix_b, ffn_w_up=ffn_w_up, ffn_w_gate=ffn_w_gate, ffn_conv_w=ffn_conv_w, ffn_conv_b=ffn_conv_b, ffn_w_down=ffn_w_down, ple_w_gate=ple_w_gate, ple_w_proj=ple_w_proj, ln_ffn_g=ln_ffn_g, ln_ffn_b=ln_ffn_b)
    mom1 = dict(a_pw1_w=m_a_pw1_w, a_pw1_b=m_a_pw1_b, a_dw_w=m_a_dw_w, a_dw_b=m_a_dw_b, a_ln_g=m_a_ln_g, a_ln_b=m_a_ln_b, a_pw2_w=m_a_pw2_w, a_pw2_b=m_a_pw2_b, b_wq=m_b_wq, kv_wk=m_kv_wk, kv_wv=m_kv_wv, b_wo=m_b_wo, ln_mix_g=m_ln_mix_g, ln_mix_b=m_ln_mix_b, ffn_w_up=m_ffn_w_up, ffn_w_gate=m_ffn_w_gate, ffn_conv_w=m_ffn_conv_w, ffn_conv_b=m_ffn_conv_b, ffn_w_down=m_ffn_w_down, ple_w_gate=m_ple_w_gate, ple_w_proj=m_ple_w_proj, ln_ffn_g=m_ln_ffn_g, ln_ffn_b=m_ln_ffn_b)
    mom2 = dict(a_pw1_w=v_a_pw1_w, a_pw1_b=v_a_pw1_b, a_dw_w=v_a_dw_w, a_dw_b=v_a_dw_b, a_ln_g=v_a_ln_g, a_ln_b=v_a_ln_b, a_pw2_w=v_a_pw2_w, a_pw2_b=v_a_pw2_b, b_wq=v_b_wq, kv_wk=v_kv_wk, kv_wv=v_kv_wv, b_wo=v_b_wo, ln_mix_g=v_ln_mix_g, ln_mix_b=v_ln_mix_b, ffn_w_up=v_ffn_w_up, ffn_w_gate=v_ffn_w_gate, ffn_conv_w=v_ffn_conv_w, ffn_conv_b=v_ffn_conv_b, ffn_w_down=v_ffn_w_down, ple_w_gate=v_ple_w_gate, ple_w_proj=v_ple_w_proj, ln_ffn_g=v_ln_ffn_g, ln_ffn_b=v_ln_ffn_b)
    big_names = [n for n, _ in BIG]
    small_names = [n for n, _ in SMALL]
    big_shapes = [local[n].shape for n in big_names]
    small_shapes = [local[n].shape for n in small_names]
    repl_shapes = [local[n].shape for n in REPL]

    gbig, gsmall = all_gather(_pack([local[n] for n in big_names], BF16, 16),
                              _pack([local[n] for n in small_names], F32, SUBLANES))
    W = {n: _join(g, ax) for (n, ax), g in zip(BIG, _unpack(gbig, big_shapes))}
    W.update({n: _join(g, ax) for (n, ax), g in zip(SMALL, _unpack(gsmall, small_shapes))})
    W.update({n: local[n] for n in REPL})

    xs = x[0]
    S, D = xs.shape
    x_in, r1s, x1s, r2s, us, gps, gs, hhs, pgls, pps = [], [], [], [], [], [], [], [], [], []
    h1s, h2s, h3s, h5s, qs, os_, tots = {}, {}, {}, {}, {}, {}, {}
    kk = vv = None
    for i in range(DEPTH):
        x_in.append(xs)
        if i < N_A:
            h1 = mm(xs, W["a_pw1_w"][i], "nn", bias=W["a_pw1_b"][i][None], name=f"pw1_{i}")
            h2 = glu_fwd(h1, name=f"glu_{i}")
            h3, h5 = conv_ln_silu_fwd(h2, W["a_dw_w"][i], W["a_dw_b"][i][None], W["a_ln_g"][i][None],
                                      W["a_ln_b"][i][None], name=f"dwconv_{i}")
            mix = mm(h5, W["a_pw2_w"][i], "nn", bias=W["a_pw2_b"][i][None], name=f"pw2_{i}")
            h1s[i], h2s[i], h3s[i], h5s[i] = h1, h2, h3, h5
        else:
            j = i - N_A
            if kk is None:
                kk = mm(xs, W["kv_wk"], "nn", out_dtype=BF16, name="proj_k")
                vv = mm(xs, W["kv_wv"], "nn", out_dtype=BF16, name="proj_v")
            q = mm(xs, W["b_wq"][j], "nn", out_dtype=BF16, name=f"proj_q_{i}")
            o, tot = attn_fwd(q, kk, vv, name=f"attn_{i}")
            mix = mm(o, W["b_wo"][j], "nn", name=f"proj_o_{i}")
            qs[i], os_[i], tots[i] = q, o, tot
        r1, x1 = res_ln(xs, mix, W["ln_mix_g"][i][None], W["ln_mix_b"][i][None], name=f"ln_mix_{i}")
        u = mm(x1, W["ffn_w_up"][i], "nn", name=f"ffn_up_{i}")
        gp = mm(x1, W["ffn_w_gate"][i], "nn", name=f"ffn_gate_{i}")
        g, hh = conv_act_fwd(gp, u, W["ffn_conv_w"][i], W["ffn_conv_b"][i][None], name=f"ffn_conv_{i}")
        f = mm(hh, W["ffn_w_down"][i], "nn", name=f"ffn_down_{i}")
        pgl = mm(x1, W["ple_w_gate"][i], "nn", name=f"ple_gate_{i}")
        pp = mm(p[i, 0], W["ple_w_proj"][i], "nn", name=f"ple_proj_{i}")
        r2, xs = res_ln(x1, f, W["ln_ffn_g"][i][None], W["ln_ffn_b"][i][None], ple=(pgl, pp), name=f"ln_ffn_{i}")
        for lst, val in ((r1s, r1), (x1s, x1), (r2s, r2), (us, u), (gps, gp), (gs, g), (hhs, hh), (pgls, pgl), (pps, pp)):
            lst.append(val)

    dx, loss_part = loss_grad(xs, loss_target[0], name="loss")
    G = {n: [None] * local[n].shape[0] for n in WEIGHTS if n not in ("kv_wk", "kv_wv")}
    dk = dv = None
    for i in reversed(range(DEPTH)):
        x1 = x1s[i]
        dr2, G["ln_ffn_g"][i], G["ln_ffn_b"][i], _ = ln_bwd(r2s[i], W["ln_ffn_g"][i][None], dx, name=f"ln_ffn_bwd_{i}")
        dhh = mm(dr2, W["ffn_w_down"][i], "nt", name=f"ffn_down_dx_{i}")
        G["ffn_w_down"][i] = mm(hhs[i], dr2, "tn", name=f"ffn_down_dw_{i}")
        dpp, dpgl = ple_bwd(dr2, pgls[i], pps[i], name=f"ple_bwd_{i}")
        G["ple_w_proj"][i] = mm(p[i, 0], dpp, "tn", name=f"ple_proj_dw_{i}")
        G["ple_w_gate"][i] = mm(x1, dpgl, "tn", name=f"ple_gate_dw_{i}")
        du, dg = ffn_act_bwd(dhh, us[i], gs[i], name=f"ffn_act_bwd_{i}")
        dgp = conv_bwd_x(dg, W["ffn_conv_w"][i], out_dtype=BF16, name=f"ffn_conv_dx_{i}")
        G["ffn_conv_w"][i], G["ffn_conv_b"][i] = conv_bwd_w(gps[i], dg, FFN_CONV_W, name=f"ffn_conv_dw_{i}")
        G["ffn_w_up"][i] = mm(x1, du, "tn", name=f"ffn_up_dw_{i}")
        G["ffn_w_gate"][i] = mm(x1, dgp, "tn", name=f"ffn_gate_dw_{i}")
        dx1 = mm(du, W["ffn_w_up"][i], "nt", add=dr2, add_scale=DN_ALPHA, name=f"ffn_up_dx_{i}")
        dx1 = mm(dgp, W["ffn_w_gate"][i], "nt", add=dx1, name=f"ffn_gate_dx_{i}")
        dx1 = mm(dpgl, W["ple_w_gate"][i], "nt", add=dx1, name=f"ple_gate_dx_{i}")
        dr1, G["ln_mix_g"][i], G["ln_mix_b"][i], dr1_sum = ln_bwd(r1s[i], W["ln_mix_g"][i][None], dx1, name=f"ln_mix_bwd_{i}")
        if i < N_A:
            G["a_pw2_w"][i] = mm(h5s[i], dr1, "tn", name=f"pw2_dw_{i}")
            G["a_pw2_b"][i] = dr1_sum
            dh5 = mm(dr1, W["a_pw2_w"][i], "nt", name=f"pw2_dx_{i}")
            dh3, G["a_ln_g"][i], G["a_ln_b"][i] = ln_silu_bwd(h3s[i], W["a_ln_g"][i][None], W["a_ln_b"][i][None], dh5,
                                                             name=f"dwconv_ln_bwd_{i}")
            dh2 = conv_bwd_x(dh3, W["a_dw_w"][i], out_dtype=F32, name=f"dwconv_dx_{i}")
            G["a_dw_w"][i], G["a_dw_b"][i] = conv_bwd_w(h2s[i], dh3, CONV_W, name=f"dwconv_dw_{i}")
            dh1, G["a_pw1_b"][i] = glu_bwd(h1s[i], dh2, name=f"glu_bwd_{i}")
            G["a_pw1_w"][i] = mm(x_in[i], dh1, "tn", name=f"pw1_dw_{i}")
            dx = mm(dh1, W["a_pw1_w"][i], "nt", add=dr1, add_scale=DN_ALPHA, name=f"pw1_dx_{i}")
        else:
            j = i - N_A
            G["b_wo"][j] = mm(os_[i], dr1, "tn", name=f"proj_o_dw_{i}")
            do = mm(dr1, W["b_wo"][j], "nt", name=f"proj_o_dx_{i}")
            dq, dk, dv = attn_bwd(qs[i], kk, vv, tots[i], do, dk, dv, name=f"attn_bwd_{i}")
            G["b_wq"][j] = mm(x_in[i], dq, "tn", name=f"proj_q_dw_{i}")
            dx = mm(dq, W["b_wq"][j], "nt", add=dr1, add_scale=DN_ALPHA, name=f"proj_q_dx_{i}")
            if j == 0:
                G["kv_wk"] = mm(x_in[i], dk, "tn", name="proj_k_dw")
                G["kv_wv"] = mm(x_in[i], dv, "tn", name="proj_v_dw")
                dx = mm(dk, W["kv_wk"], "nt", add=dx, name="proj_k_dx")
                dx = mm(dv, W["kv_wv"], "nt", add=dx, name="proj_v_dx")
    grad_x = dx[None]
    shard_axis = dict(BIG + SMALL)
    for n in WEIGHTS:
        full = list(local[n].shape)
        if n in shard_axis:
            full[shard_axis[n]] *= N_DEV
        G[n] = (G[n] if n in ("kv_wk", "kv_wv") else jnp.stack(G[n])).reshape(full)

    n_small = sum(math.prod(s) for s in small_shapes)
    n_repl = sum(math.prod(s) for s in repl_shapes)
    send_big = _pack_dev([_split(G[n], ax) for n, ax in BIG], BF16, 16)
    repl_flat = jnp.concatenate([G[n].reshape(-1) for n in REPL] + [loss_part.reshape(-1)[:1]])
    send_small = _pack_dev([_split(G[n], ax) for n, ax in SMALL] + [jnp.broadcast_to(repl_flat, (N_DEV, n_repl + 1))],
                           F32, SUBLANES)
    recv_big, recv_small = exchange_grads(send_big, send_small)

    def state(d):
        big = _pack([d[n] for n in big_names], F32, 16)
        small = _pack([d[n] for n in small_names] + [d[n] for n in REPL], F32, SUBLANES)
        return big, jnp.pad(small, ((0, recv_small.shape[1] - small.shape[0]), (0, 0)))

    (w_big, w_small), (m_big, m_small), (v_big, v_small) = state(local), state(mom1), state(mom2)
    out_big = adamw(recv_big, w_big, m_big, v_big, name="adamw_matrices")
    out_small = adamw(recv_small, w_small, m_small, v_small, name="adamw_vectors")
    loss = out_small[0].reshape(-1)[n_small + n_repl]
    per_kind = []
    for ob, osm in zip(out_big, out_small):
        vals = dict(zip(big_names, _unpack(ob, big_shapes)))
        vals.update(zip(small_names + list(REPL), _unpack(osm, small_shapes + repl_shapes)))
        per_kind.append([vals[n] for n in WEIGHTS])
    grads, deltas, new_m, new_v = per_kind
    return (loss, grad_x, *grads, *deltas, *new_m, *new_v)


def _pack_dev(arrs, dtype, row_mult):
    flat = jnp.concatenate([a.reshape(N_DEV, -1).astype(dtype) for a in arrs], axis=1)
    rows = -(-flat.shape[1] // PACK_W)
    rows = -(-rows // row_mult) * row_mult
    return jnp.pad(flat, ((0, 0), (0, rows * PACK_W - flat.shape[1]))).reshape(N_DEV, rows, PACK_W)
```

```python
import functools
import math

import jax
import jax.numpy as jnp
from jax import lax
from jax.experimental import pallas as pl
from jax.experimental.pallas import tpu as pltpu

F32 = jnp.float32
BF16 = jnp.bfloat16
MESH = pl.DeviceIdType.MESH

N_DEV = 8
DEPTH = 4
N_A = 2
HEAD_DIM = 64
Q_BLOCK = 128
CONV_W = 31
FFN_CONV_W = 3
LN_EPS = 1e-5
DN_ALPHA = (2.0 * DEPTH) ** 0.25
ADAM_LR = 0.001
ADAM_B1 = 0.9
ADAM_B2 = 0.999
ADAM_EPS = 1e-08
ADAM_WD = 0.01
ADAM_STEP = 10

LANES = 128
SUBLANES = 8
PACK_W = 1024
VMEM_LIMIT = 56 * 1024 * 1024

BIG = (("a_pw1_w", 2), ("a_pw2_w", 1), ("b_wq", 1), ("kv_wk", 0), ("kv_wv", 0), ("b_wo", 1),
       ("ffn_w_up", 2), ("ffn_w_gate", 2), ("ffn_w_down", 1), ("ple_w_gate", 1), ("ple_w_proj", 2))
SMALL = (("a_pw1_b", 1), ("a_dw_w", 2), ("a_dw_b", 1), ("a_ln_g", 1), ("a_ln_b", 1), ("a_pw2_b", 1),
         ("ffn_conv_w", 2))
REPL = ("ln_mix_g", "ln_mix_b", "ffn_conv_b", "ln_ffn_g", "ln_ffn_b")
WEIGHTS = ("a_pw1_w", "a_pw1_b", "a_dw_w", "a_dw_b", "a_ln_g", "a_ln_b", "a_pw2_w", "a_pw2_b", "b_wq", "kv_wk",
           "kv_wv", "b_wo", "ln_mix_g", "ln_mix_b", "ffn_w_up", "ffn_w_gate", "ffn_conv_w", "ffn_conv_b",
           "ffn_w_down", "ple_w_gate", "ple_w_proj", "ln_ffn_g", "ln_ffn_b")


def _cp(*sem):
    return pltpu.CompilerParams(dimension_semantics=sem, vmem_limit_bytes=VMEM_LIMIT)


def _pick(dim, target, align=LANES):
    if dim <= target:
        return dim
    t = (target // align) * align
    while t >= align:
        if dim % t == 0:
            return t
        t -= align
    return dim


_DOT_DIMS = {"nn": (((1,), (0,)), ((), ())), "nt": (((1,), (1,)), ((), ())), "tn": (((0,), (0,)), ((), ()))}


def mm(a, b, mode, *, bias=None, add=None, add_scale=1.0, out_dtype=F32, name):
    if mode == "tn":
        K, M = a.shape
    else:
        M, K = a.shape
    N = b.shape[0] if mode == "nt" else b.shape[1]
    tm, tn, tk = _pick(M, 512), _pick(N, 1536), _pick(K, 1536)
    nk = K // tk
    dims = _DOT_DIMS[mode]

    def body(*refs):
        a_ref, b_ref = refs[0], refs[1]
        pos = 2
        bias_ref = add_ref = None
        if bias is not None:
            bias_ref = refs[pos]
            pos += 1
        if add is not None:
            add_ref = refs[pos]
            pos += 1
        o_ref, acc_ref = refs[pos], refs[pos + 1]
        k = pl.program_id(2)

        @pl.when(k == 0)
        def _():
            acc_ref[...] = jnp.zeros_like(acc_ref)

        acc_ref[...] += lax.dot_general(a_ref[...].astype(BF16), b_ref[...].astype(BF16), dims,
                                        preferred_element_type=F32)

        @pl.when(k == nk - 1)
        def _():
            r = acc_ref[...]
            if bias_ref is not None:
                r = r + bias_ref[...]
            if add_ref is not None:
                r = r + add_scale * add_ref[...].astype(F32)
            o_ref[...] = r.astype(o_ref.dtype)

    a_spec = pl.BlockSpec((tk, tm), lambda i, j, k: (k, i)) if mode == "tn" else pl.BlockSpec((tm, tk), lambda i, j, k: (i, k))
    b_spec = pl.BlockSpec((tn, tk), lambda i, j, k: (j, k)) if mode == "nt" else pl.BlockSpec((tk, tn), lambda i, j, k: (k, j))
    in_specs, args = [a_spec, b_spec], [a, b]
    if bias is not None:
        in_specs.append(pl.BlockSpec((1, tn), lambda i, j, k: (0, j)))
        args.append(bias)
    if add is not None:
        in_specs.append(pl.BlockSpec((tm, tn), lambda i, j, k: (i, j)))
        args.append(add)
    return pl.pallas_call(
        body, grid=(M // tm, N // tn, nk), in_specs=in_specs,
        out_specs=pl.BlockSpec((tm, tn), lambda i, j, k: (i, j)),
        out_shape=jax.ShapeDtypeStruct((M, N), out_dtype),
        scratch_shapes=[pltpu.VMEM((tm, tn), F32)],
        compiler_params=_cp("parallel", "parallel", "arbitrary"), name=name)(*args)


def _rows(body, *, n_rows, tm, row_ins, full_ins=(), row_outs=(), acc_outs=(), scratch=(), reverse=False, name):
    n = n_rows // tm

    def rmap(i):
        return (n - 1 - i, 0) if reverse else (i, 0)

    in_specs = [pl.BlockSpec((tm, a.shape[1]), rmap) for a in row_ins]
    in_specs += [pl.BlockSpec(a.shape, lambda i, nd=a.ndim: (0,) * nd) for a in full_ins]
    out_shape = [jax.ShapeDtypeStruct((n_rows, w), dt) for (w, dt) in row_outs]
    out_shape += [jax.ShapeDtypeStruct(s, dt) for (s, dt) in acc_outs]
    out_specs = [pl.BlockSpec((tm, w), rmap) for (w, dt) in row_outs]
    out_specs += [pl.BlockSpec(s, lambda i, nd=len(s): (0,) * nd) for (s, dt) in acc_outs]
    return pl.pallas_call(
        functools.partial(body, n), grid=(n,), in_specs=in_specs, out_specs=out_specs, out_shape=out_shape,
        scratch_shapes=list(scratch), compiler_params=_cp("arbitrary"), name=name)(*row_ins, *full_ins)


def _sigmoid(x):
    return 1.0 / (1.0 + jnp.exp(-x))


def _ln_hat(r):
    mu = jnp.mean(r, axis=-1, keepdims=True)
    xc = r - mu
    var = jnp.mean(xc * xc, axis=-1, keepdims=True)
    rstd = lax.rsqrt(var + LN_EPS)
    return xc * rstd, rstd


def _ln_back(xhat, rstd, g, dy):
    dxh = dy * g
    m1 = jnp.mean(dxh, axis=-1, keepdims=True)
    m2 = jnp.mean(dxh * xhat, axis=-1, keepdims=True)
    return rstd * (dxh - m1 - xhat * m2)


def _colsum(x):
    return jnp.sum(x, axis=0, keepdims=True)


def _acc(i, ref, val):
    @pl.when(i == 0)
    def _():
        ref[...] = val

    @pl.when(i > 0)
    def _():
        ref[...] += val


def res_ln(x, mix, g, b, *, ple=None, name):
    S, D = x.shape

    def body(n, *refs):
        if ple is None:
            x_ref, m_ref, g_ref, b_ref, r_ref, y_ref = refs
            r = DN_ALPHA * x_ref[...] + m_ref[...]
        else:
            x_ref, m_ref, pgl_ref, pp_ref, g_ref, b_ref, r_ref, y_ref = refs
            r = DN_ALPHA * x_ref[...] + m_ref[...] + _sigmoid(pgl_ref[...]) * pp_ref[...]
        xhat, _ = _ln_hat(r)
        r_ref[...] = r
        y_ref[...] = xhat * g_ref[...] + b_ref[...]

    row_ins = [x, mix] + ([] if ple is None else list(ple))
    return _rows(body, n_rows=S, tm=_pick(S, 256, SUBLANES), row_ins=row_ins, full_ins=[g, b],
                 row_outs=[(D, F32), (D, F32)], name=name)


def ln_bwd(r, g, dy, *, name):
    S, D = r.shape

    def body(n, r_ref, dy_ref, g_ref, dr_ref, dg_ref, db_ref, ds_ref):
        i = pl.program_id(0)
        xhat, rstd = _ln_hat(r_ref[...])
        dy_v = dy_ref[...]
        dr = _ln_back(xhat, rstd, g_ref[...], dy_v)
        dr_ref[...] = dr
        _acc(i, dg_ref, _colsum(dy_v * xhat))
        _acc(i, db_ref, _colsum(dy_v))
        _acc(i, ds_ref, _colsum(dr))

    return _rows(body, n_rows=S, tm=_pick(S, 256, SUBLANES), row_ins=[r, dy], full_ins=[g],
                 row_outs=[(D, F32)], acc_outs=[((1, D), F32)] * 3, name=name)


def glu_fwd(h1, *, name):
    S, D2 = h1.shape
    D = D2 // 2

    def body(n, h_ref, o_ref):
        o_ref[...] = h_ref[:, :D] * _sigmoid(h_ref[:, D:])

    return _rows(body, n_rows=S, tm=_pick(S, 256, SUBLANES), row_ins=[h1], row_outs=[(D, F32)], name=name)[0]


def glu_bwd(h1, dh2, *, name):
    S, D2 = h1.shape
    D = D2 // 2

    def body(n, h_ref, d_ref, o_ref, s_ref):
        i = pl.program_id(0)
        a, sg, d = h_ref[:, :D], _sigmoid(h_ref[:, D:]), d_ref[...]
        da = d * sg
        dg = d * a * sg * (1.0 - sg)
        o_ref[:, :D] = da.astype(o_ref.dtype)
        o_ref[:, D:] = dg.astype(o_ref.dtype)
        _acc(i, s_ref, jnp.concatenate([_colsum(da), _colsum(dg)], axis=1))

    return _rows(body, n_rows=S, tm=_pick(S, 256, SUBLANES), row_ins=[h1, dh2], row_outs=[(D2, BF16)],
                 acc_outs=[((1, D2), F32)], name=name)


CONV_ROWS = 32
CONV_LANES = 256


def _halo(k):
    return -(-(k - 1) // SUBLANES) * SUBLANES


def _conv_taps(buf_ref, w_ref, offs, tm, width, emit):
    rows = min(CONV_ROWS, tm)
    for lc in range(0, width, CONV_LANES):
        lw = min(CONV_LANES, width - lc)
        for rc in range(0, tm, rows):
            acc = None
            for k, off in enumerate(offs):
                t = buf_ref[pl.ds(off + rc, rows), pl.ds(lc, lw)] * w_ref[pl.ds(k, 1), pl.ds(lc, lw)]
                acc = t if acc is None else acc + t
            emit(rc, lc, lw, rows, acc)


def _fill_causal(i, buf_ref, x_ref, halo, tm):
    @pl.when(i == 0)
    def _():
        buf_ref[pl.ds(0, halo), :] = jnp.zeros((halo, buf_ref.shape[1]), F32)

    @pl.when(i > 0)
    def _():
        buf_ref[pl.ds(0, halo), :] = buf_ref[pl.ds(tm, halo), :]

    buf_ref[pl.ds(halo, tm), :] = x_ref[...]


def conv_ln_silu_fwd(x, w, b, g, beta, *, name):
    S, C = x.shape
    K = w.shape[0]
    halo = _halo(K)
    tm = _pick(S, 256, SUBLANES)
    offs = [halo - (K - 1) + k for k in range(K)]

    def body(n, x_ref, w_ref, b_ref, g_ref, beta_ref, h3_ref, h5_ref, buf_ref):
        i = pl.program_id(0)
        _fill_causal(i, buf_ref, x_ref, halo, tm)

        def emit(rc, lc, lw, rows, acc):
            h3_ref[pl.ds(rc, rows), pl.ds(lc, lw)] = acc + b_ref[:, pl.ds(lc, lw)]

        _conv_taps(buf_ref, w_ref, offs, tm, C, emit)
        xhat, _ = _ln_hat(h3_ref[...])
        h4 = xhat * g_ref[...] + beta_ref[...]
        h5_ref[...] = (h4 * _sigmoid(h4)).astype(h5_ref.dtype)

    return _rows(body, n_rows=S, tm=tm, row_ins=[x], full_ins=[w, b, g, beta], row_outs=[(C, F32), (C, BF16)],
                 scratch=[pltpu.VMEM((tm + halo, C), F32)], name=name)


def conv_act_fwd(gp, u, w, b, *, name):
    S, C = gp.shape
    K = w.shape[0]
    halo = _halo(K)
    tm = _pick(S, 256, SUBLANES)
    offs = [halo - (K - 1) + k for k in range(K)]

    def body(n, x_ref, u_ref, w_ref, b_ref, g_ref, hh_ref, buf_ref):
        i = pl.program_id(0)
        _fill_causal(i, buf_ref, x_ref, halo, tm)

        def emit(rc, lc, lw, rows, acc):
            gv = acc + b_ref[:, pl.ds(lc, lw)]
            g_ref[pl.ds(rc, rows), pl.ds(lc, lw)] = gv
            hh_ref[pl.ds(rc, rows), pl.ds(lc, lw)] = (gv * _sigmoid(gv) * u_ref[pl.ds(rc, rows), pl.ds(lc, lw)]).astype(hh_ref.dtype)

        _conv_taps(buf_ref, w_ref, offs, tm, C, emit)

    return _rows(body, n_rows=S, tm=tm, row_ins=[gp, u], full_ins=[w, b], row_outs=[(C, F32), (C, BF16)],
                 scratch=[pltpu.VMEM((tm + halo, C), F32)], name=name)


def conv_bwd_x(dy, w, *, out_dtype, name):
    S, C = dy.shape
    K = w.shape[0]
    halo = _halo(K)
    tm = _pick(S, 256, SUBLANES)
    offs = [K - 1 - k for k in range(K)]

    def body(n, dy_ref, w_ref, dx_ref, buf_ref):
        i = pl.program_id(0)

        @pl.when(i == 0)
        def _():
            buf_ref[pl.ds(tm, halo), :] = jnp.zeros((halo, C), F32)

        @pl.when(i > 0)
        def _():
            buf_ref[pl.ds(tm, halo), :] = buf_ref[pl.ds(0, halo), :]

        buf_ref[pl.ds(0, tm), :] = dy_ref[...]

        def emit(rc, lc, lw, rows, acc):
            dx_ref[pl.ds(rc, rows), pl.ds(lc, lw)] = acc.astype(dx_ref.dtype)

        _conv_taps(buf_ref, w_ref, offs, tm, C, emit)

    return _rows(body, n_rows=S, tm=tm, row_ins=[dy], full_ins=[w], row_outs=[(C, out_dtype)],
                 scratch=[pltpu.VMEM((tm + halo, C), F32)], reverse=True, name=name)[0]


def conv_bwd_w(x, dy, K, *, name):
    S, C = x.shape
    halo = _halo(K)
    tm = _pick(S, 256, SUBLANES)
    offs = [halo - (K - 1) + k for k in range(K)]
    rows = min(CONV_ROWS, tm)

    def body(n, x_ref, dy_ref, dw_ref, db_ref, buf_ref, acc_ref):
        i = pl.program_id(0)
        _fill_causal(i, buf_ref, x_ref, halo, tm)

        @pl.when(i == 0)
        def _():
            acc_ref[...] = jnp.zeros_like(acc_ref)

        for lc in range(0, C, CONV_LANES):
            lw = min(CONV_LANES, C - lc)
            for k, off in enumerate(offs):
                s = None
                for rc in range(0, tm, rows):
                    t = dy_ref[pl.ds(rc, rows), pl.ds(lc, lw)] * buf_ref[pl.ds(off + rc, rows), pl.ds(lc, lw)]
                    s = t if s is None else s + t
                s8 = s[0:SUBLANES]
                for q in range(1, rows // SUBLANES):
                    s8 = s8 + s[q * SUBLANES:(q + 1) * SUBLANES]
                acc_ref[pl.ds(k * SUBLANES, SUBLANES), pl.ds(lc, lw)] += s8
        _acc(i, db_ref, _colsum(dy_ref[...]))

        @pl.when(i == n - 1)
        def _():
            for k in range(K):
                dw_ref[pl.ds(k, 1), :] = _colsum(acc_ref[pl.ds(k * SUBLANES, SUBLANES), :])

    return _rows(body, n_rows=S, tm=tm, row_ins=[x, dy], acc_outs=[((K, C), F32), ((1, C), F32)],
                 scratch=[pltpu.VMEM((tm + halo, C), F32), pltpu.VMEM((K * SUBLANES, C), F32)], name=name)


def ln_silu_bwd(h3, g, beta, dh5, *, name):
    S, C = h3.shape

    def body(n, h_ref, d_ref, g_ref, beta_ref, dh_ref, dg_ref, db_ref):
        i = pl.program_id(0)
        xhat, rstd = _ln_hat(h_ref[...])
        h4 = xhat * g_ref[...] + beta_ref[...]
        sg = _sigmoid(h4)
        dh4 = d_ref[...] * sg * (1.0 + h4 * (1.0 - sg))
        dh_ref[...] = _ln_back(xhat, rstd, g_ref[...], dh4)
        _acc(i, dg_ref, _colsum(dh4 * xhat))
        _acc(i, db_ref, _colsum(dh4))

    return _rows(body, n_rows=S, tm=_pick(S, 256, SUBLANES), row_ins=[h3, dh5], full_ins=[g, beta],
                 row_outs=[(C, F32)], acc_outs=[((1, C), F32)] * 2, name=name)


def ffn_act_bwd(dhh, u, g, *, name):
    S, C = u.shape

    def body(n, d_ref, u_ref, g_ref, du_ref, dg_ref):
        d, gv = d_ref[...], g_ref[...]
        sg = _sigmoid(gv)
        du_ref[...] = (d * gv * sg).astype(du_ref.dtype)
        dg_ref[...] = d * u_ref[...] * sg * (1.0 + gv * (1.0 - sg))

    return _rows(body, n_rows=S, tm=_pick(S, 256, SUBLANES), row_ins=[dhh, u, g], row_outs=[(C, BF16), (C, F32)],
                 name=name)


def ple_bwd(dr, pgl, pp, *, name):
    S, D = dr.shape

    def body(n, d_ref, l_ref, p_ref, dpp_ref, dpl_ref):
        d, sg = d_ref[...], _sigmoid(l_ref[...])
        dpp_ref[...] = (d * sg).astype(dpp_ref.dtype)
        dpl_ref[...] = (d * p_ref[...] * sg * (1.0 - sg)).astype(dpl_ref.dtype)

    return _rows(body, n_rows=S, tm=_pick(S, 256, SUBLANES), row_ins=[dr, pgl, pp], row_outs=[(D, BF16), (D, BF16)],
                 name=name)


def loss_grad(y, target, *, name):
    S, D = y.shape

    def body(n, y_ref, t_ref, dy_ref, l_ref):
        i = pl.program_id(0)
        e = y_ref[...] - t_ref[...]
        dy_ref[...] = e * (1.0 / D)
        s = jnp.sum(_colsum(e * e), axis=1, keepdims=True) * (0.5 / D)
        _acc(i, l_ref, jnp.broadcast_to(s, (1, LANES)))

    return _rows(body, n_rows=S, tm=_pick(S, 256, SUBLANES), row_ins=[y, target], row_outs=[(D, F32)],
                 acc_outs=[((1, LANES), F32)], name=name)


def _key_step(S):
    return min(512, S // 2)


def _attn_consts():
    lane = lax.broadcasted_iota(jnp.int32, (1, LANES), 1)
    heads = (lane < HEAD_DIM, lane >= HEAD_DIM)
    row = lax.broadcasted_iota(jnp.int32, (Q_BLOCK, Q_BLOCK), 0)
    col = lax.broadcasted_iota(jnp.int32, (Q_BLOCK, Q_BLOCK), 1)
    return heads, row, col


def _tri2(cond):
    t = jnp.where(cond, 1.0, 0.0).astype(BF16)
    return jnp.concatenate([t, t], axis=0)


def _keysum(x, tri2):
    hi = x.astype(BF16)
    lo = (x - hi.astype(F32)).astype(BF16)
    return jnp.dot(jnp.concatenate([hi, lo], axis=1), tri2, preferred_element_type=F32)


def _stack_heads(x, heads):
    return jnp.concatenate([jnp.where(m, x, jnp.zeros_like(x)) for m in heads], axis=0)


def _log1m_beta(z):
    return -(jnp.maximum(z, 0.0) + jnp.log(1.0 + jnp.exp(-jnp.abs(z))))


def _block_mask(row, col, off):
    if off is None:
        return None
    m = (col + off) < row
    return jnp.concatenate([m, m], axis=0)


def attn_fwd(q, k, v, *, name):
    S, D = q.shape
    nb = S // Q_BLOCK
    tk = _key_step(S)
    nkb = tk // Q_BLOCK
    scale = 1.0 / math.sqrt(HEAD_DIM)

    def body(q_ref, k_ref, v_ref, o_ref, tot_ref, vm_ref):
        heads, row, col = _attn_consts()
        above = _tri2(row > col)
        for h in range(2):
            vm_ref[h] = jnp.where(heads[h], v_ref[...], jnp.zeros_like(v_ref[...]))

        def step(sb, carry, qq, r0, masked):
            acc, cl = carry
            c0 = pl.multiple_of(sb * tk, tk)
            z = lax.dot_general(qq, k_ref[pl.ds(c0, tk), :], _DOT_DIMS["nt"], preferred_element_type=F32) * scale
            zl, es, rs, masks = [], [], [], []
            for jb in range(nkb):
                zb = z[:, jb * Q_BLOCK:(jb + 1) * Q_BLOCK]
                lr = _log1m_beta(zb)
                mask = _block_mask(row, col, c0 + jb * Q_BLOCK - r0 if masked else None)
                l = lr if mask is None else jnp.where(mask, lr, 0.0)
                zl.append(zb + lr)
                es.append(_keysum(l, above))
                rs.append(jnp.sum(l, axis=1, keepdims=True))
                masks.append(mask)
            a = [None] * nkb
            for jb in reversed(range(nkb)):
                ab = jnp.exp(zl[jb] + es[jb] + cl)
                if masks[jb] is not None:
                    ab = jnp.where(masks[jb], ab, 0.0)
                a[jb] = ab.astype(BF16)
                cl = cl + rs[jb]
            a = jnp.concatenate(a, axis=1)
            for h in range(2):
                acc = acc + jnp.dot(a[h * Q_BLOCK:(h + 1) * Q_BLOCK], vm_ref[h, pl.ds(c0, tk), :],
                                    preferred_element_type=F32)
            return acc, cl

        def qblock(i, _):
            r0 = pl.multiple_of(i * Q_BLOCK, Q_BLOCK)
            qq = _stack_heads(q_ref[pl.ds(r0, Q_BLOCK), :], heads)
            last = i // nkb
            carry = (jnp.zeros((Q_BLOCK, LANES), F32), jnp.zeros((2 * Q_BLOCK, 1), F32))
            carry = step(last, carry, qq, r0, True)
            acc, cl = lax.fori_loop(0, last, lambda jj, c: step(last - 1 - jj, c, qq, r0, False), carry)
            o_ref[pl.ds(r0, Q_BLOCK), :] = acc
            tot_ref[pl.ds(r0, Q_BLOCK), :] = jnp.where(heads[0], cl[:Q_BLOCK], cl[Q_BLOCK:])
            return 0

        lax.fori_loop(0, nb, qblock, 0)

    spec = pl.BlockSpec((S, LANES), lambda h: (0, h))
    return pl.pallas_call(body, grid=(D // LANES,), in_specs=[spec] * 3, out_specs=[spec] * 2,
                          out_shape=[jax.ShapeDtypeStruct((S, D), F32)] * 2,
                          scratch_shapes=[pltpu.VMEM((2, S, LANES), BF16)], compiler_params=_cp("parallel"),
                          name=name)(q, k, v)


def attn_bwd(q, k, v, tot, do, dk0, dv0, *, name):
    S, D = q.shape
    nb = S // Q_BLOCK
    tk = _key_step(S)
    nkb = tk // Q_BLOCK
    scale = 1.0 / math.sqrt(HEAD_DIM)
    has_init = dk0 is not None

    def body(*refs):
        if has_init:
            q_ref, k_ref, v_ref, tot_ref, do_ref, dk0_ref, dv0_ref, dq_ref, dk_ref, dv_ref, km_ref = refs
            dk_ref[...] = dk0_ref[...]
            dv_ref[...] = dv0_ref[...]
        else:
            q_ref, k_ref, v_ref, tot_ref, do_ref, dq_ref, dk_ref, dv_ref, km_ref = refs
            dk_ref[...] = jnp.zeros_like(dk_ref)
            dv_ref[...] = jnp.zeros_like(dv_ref)
        heads, row, col = _attn_consts()
        upto = _tri2(row <= col)
        before = _tri2(row < col)
        for h in range(2):
            km_ref[h] = jnp.where(heads[h], k_ref[...], jnp.zeros_like(k_ref[...]))

        def step(sb, carry, qq, dd, totl, r0, masked):
            dq, pl_, pg = carry
            c0 = pl.multiple_of(sb * tk, tk)
            z = lax.dot_general(qq, k_ref[pl.ds(c0, tk), :], _DOT_DIMS["nt"], preferred_element_type=F32) * scale
            da = lax.dot_general(dd, v_ref[pl.ds(c0, tk), :], _DOT_DIMS["nt"], preferred_element_type=F32)
            a, dz = [None] * nkb, [None] * nkb
            for jb in range(nkb):
                zb = z[:, jb * Q_BLOCK:(jb + 1) * Q_BLOCK]
                lr = _log1m_beta(zb)
                mask = _block_mask(row, col, c0 + jb * Q_BLOCK - r0 if masked else None)
                l = lr if mask is None else jnp.where(mask, lr, 0.0)
                ab = jnp.exp(zb + lr + (totl - pl_ - _keysum(l, upto)))
                if mask is not None:
                    ab = jnp.where(mask, ab, 0.0)
                g = ab * da[:, jb * Q_BLOCK:(jb + 1) * Q_BLOCK]
                dzb = g * jnp.exp(lr) - jnp.exp(zb + lr) * (pg + _keysum(g, before))
                if mask is not None:
                    dzb = jnp.where(mask, dzb, 0.0)
                a[jb] = ab.astype(BF16)
                dz[jb] = (dzb * scale).astype(BF16)
                pl_ = pl_ + jnp.sum(l, axis=1, keepdims=True)
                pg = pg + jnp.sum(g, axis=1, keepdims=True)
            a = jnp.concatenate(a, axis=1)
            dz = jnp.concatenate(dz, axis=1)
            for h in range(2):
                dq = dq + jnp.dot(dz[h * Q_BLOCK:(h + 1) * Q_BLOCK], km_ref[h, pl.ds(c0, tk), :],
                                  preferred_element_type=F32)
            dk_ref[pl.ds(c0, tk), :] += lax.dot_general(dz, qq, _DOT_DIMS["tn"], preferred_element_type=F32)
            dv_ref[pl.ds(c0, tk), :] += lax.dot_general(a, dd, _DOT_DIMS["tn"], preferred_element_type=F32)
            return dq, pl_, pg

        def qblock(i, _):
            r0 = pl.multiple_of(i * Q_BLOCK, Q_BLOCK)
            qq = _stack_heads(q_ref[pl.ds(r0, Q_BLOCK), :], heads)
            dd = _stack_heads(do_ref[pl.ds(r0, Q_BLOCK), :].astype(BF16), heads)
            tot2 = tot_ref[pl.ds(r0, Q_BLOCK), :]
            totl = jnp.concatenate([tot2[:, 0:1], tot2[:, HEAD_DIM:HEAD_DIM + 1]], axis=0)
            last = i // nkb
            zc = jnp.zeros((2 * Q_BLOCK, 1), F32)
            carry = (jnp.zeros((Q_BLOCK, LANES), F32), zc, zc)
            carry = lax.fori_loop(0, last, lambda sb, c: step(sb, c, qq, dd, totl, r0, False), carry)
            carry = step(last, carry, qq, dd, totl, r0, True)
            dq_ref[pl.ds(r0, Q_BLOCK), :] = carry[0].astype(dq_ref.dtype)
            return 0

        lax.fori_loop(0, nb, qblock, 0)

    spec = pl.BlockSpec((S, LANES), lambda h: (0, h))
    args = [q, k, v, tot, do] + ([dk0, dv0] if has_init else [])
    return pl.pallas_call(
        body, grid=(D // LANES,), in_specs=[spec] * len(args), out_specs=[spec] * 3,
        out_shape=[jax.ShapeDtypeStruct((S, D), BF16), jax.ShapeDtypeStruct((S, D), F32), jax.ShapeDtypeStruct((S, D), F32)],
        scratch_shapes=[pltpu.VMEM((2, S, LANES), BF16)], compiler_params=_cp("parallel"), name=name)(*args)


def _dev_index(px, py, pc):
    return 4 * px + 2 * py + pc


def all_gather(big, small):
    bufs = (big, small)

    def body(big_ref, small_ref, obig_ref, osmall_ref, send_sems, recv_sems, local_sems):
        x, y, c = lax.axis_index("x"), lax.axis_index("y"), lax.axis_index("c")
        me, sibling = (x, y, c), (x, y, 1 - c)
        chips = [(1 - x, y), (x, 1 - y), (1 - x, 1 - y)]
        ins, outs = (big_ref, small_ref), (obig_ref, osmall_ref)

        def copy(b, k, block, to, from_input=False):
            slot = outs[b].at[_dev_index(*block)]
            return pltpu.make_async_remote_copy(
                src_ref=ins[b] if from_input else slot, dst_ref=slot,
                send_sem=send_sems.at[7 * b + k], recv_sem=recv_sems.at[7 * b + k], device_id=to, device_id_type=MESH)

        mine = [pltpu.make_async_copy(ins[b], outs[b].at[_dev_index(*me)], local_sems.at[b]) for b in range(2)]
        for cp in mine:
            cp.start()
        first = []
        for b in range(2):
            first.append(copy(b, 0, me, sibling, from_input=True))
            first += [copy(b, 1 + j, me, (*chip, c), from_input=True) for j, chip in enumerate(chips)]
        for cp in first:
            cp.start()
        passed = []
        for j, chip in enumerate(chips):
            for b in range(2):
                copy(b, 1 + j, (*chip, c), me).wait_recv()
                fwd = copy(b, 4 + j, (*chip, c), sibling)
                fwd.start()
                passed.append(fwd)
        for b in range(2):
            copy(b, 0, sibling, me).wait_recv()
            for j, chip in enumerate(chips):
                copy(b, 4 + j, (*chip, 1 - c), me).wait_recv()
        for cp in first + passed:
            cp.wait_send()
        for cp in mine:
            cp.wait()

    any_spec = pl.BlockSpec(memory_space=pl.ANY)
    return pl.pallas_call(
        body, in_specs=[any_spec] * 2, out_specs=[any_spec] * 2,
        out_shape=[jax.ShapeDtypeStruct((N_DEV,) + b.shape, b.dtype) for b in bufs],
        scratch_shapes=[pltpu.SemaphoreType.DMA((14,)), pltpu.SemaphoreType.DMA((14,)), pltpu.SemaphoreType.DMA((2,))],
        name="all_gather_weights")(*bufs)


def exchange_grads(big, small):
    bufs = (big, small)

    def body(big_ref, small_ref, obig_ref, osmall_ref, send_sems, recv_sems, local_sems):
        x, y, c = lax.axis_index("x"), lax.axis_index("y"), lax.axis_index("c")
        me = _dev_index(x, y, c)
        ins, outs = (big_ref, small_ref), (obig_ref, osmall_ref)
        mine = [pltpu.make_async_copy(ins[b].at[me], outs[b].at[me], local_sems.at[b]) for b in range(2)]
        for cp in mine:
            cp.start()
        copies = []
        for flip in range(1, N_DEV):
            fx, fy, fc = (flip >> 2) & 1, (flip >> 1) & 1, flip & 1
            peer = (1 - x if fx else x, 1 - y if fy else y, 1 - c if fc else c)
            pidx = _dev_index(*peer)
            for b in range(2):
                k = 7 * b + flip - 1
                send = pltpu.make_async_remote_copy(
                    src_ref=ins[b].at[pidx], dst_ref=outs[b].at[me], send_sem=send_sems.at[k], recv_sem=recv_sems.at[k],
                    device_id=peer, device_id_type=MESH)
                land = pltpu.make_async_remote_copy(
                    src_ref=ins[b].at[pidx], dst_ref=outs[b].at[pidx], send_sem=send_sems.at[k], recv_sem=recv_sems.at[k],
                    device_id=peer, device_id_type=MESH)
                send.start()
                copies.append((send, land))
        for send, land in copies:
            land.wait_recv()
        for send, land in copies:
            send.wait_send()
        for cp in mine:
            cp.wait()

    any_spec = pl.BlockSpec(memory_space=pl.ANY)
    return pl.pallas_call(
        body, in_specs=[any_spec] * 2, out_specs=[any_spec] * 2,
        out_shape=[jax.ShapeDtypeStruct(b.shape, b.dtype) for b in bufs],
        scratch_shapes=[pltpu.SemaphoreType.DMA((14,)), pltpu.SemaphoreType.DMA((14,)), pltpu.SemaphoreType.DMA((2,))],
        name="exchange_grads")(*bufs)


def adamw(recv, w, m, v, *, name):
    _, R, W = recv.shape
    tr = _pick(R, 128, 16)
    c1 = 1.0 - ADAM_B1 ** ADAM_STEP
    c2 = 1.0 - ADAM_B2 ** ADAM_STEP

    def body(r_ref, w_ref, m_ref, v_ref, g_ref, d_ref, mo_ref, vo_ref):
        g = r_ref[0].astype(F32)
        for j in range(1, N_DEV):
            g = g + r_ref[j].astype(F32)
        mn = ADAM_B1 * m_ref[...] + (1.0 - ADAM_B1) * g
        vn = ADAM_B2 * v_ref[...] + (1.0 - ADAM_B2) * (g * g)
        g_ref[...] = g
        mo_ref[...] = mn
        vo_ref[...] = vn
        d_ref[...] = -ADAM_LR * ((mn / c1) / (jnp.sqrt(vn / c2) + ADAM_EPS) + ADAM_WD * w_ref[...])

    spec = pl.BlockSpec((tr, W), lambda i: (i, 0))
    return pl.pallas_call(
        body, grid=(R // tr,), in_specs=[pl.BlockSpec((N_DEV, tr, W), lambda i: (0, i, 0)), spec, spec, spec],
        out_specs=[spec] * 4, out_shape=[jax.ShapeDtypeStruct((R, W), F32)] * 4,
        compiler_params=_cp("parallel"), name=name)(recv, w, m, v)


def _pack(arrs, dtype, row_mult):
    flat = jnp.concatenate([a.reshape(-1).astype(dtype) for a in arrs])
    rows = -(-flat.shape[0] // PACK_W)
    rows = -(-rows // row_mult) * row_mult
    return jnp.pad(flat, (0, rows * PACK_W - flat.shape[0])).reshape(rows, PACK_W)


def _unpack(buf, shapes):
    lead = buf.shape[:-2]
    flat = buf.reshape(lead + (-1,))
    outs, off = [], 0
    for s in shapes:
        n = math.prod(s)
        outs.append(flat[..., off:off + n].reshape(lead + tuple(s)))
        off += n
    return outs


def _join(g, axis):
    g = jnp.moveaxis(g, 0, axis)
    return g.reshape(g.shape[:axis] + (g.shape[axis] * g.shape[axis + 1],) + g.shape[axis + 2:])


def _split(full, axis):
    s = full.shape
    g = full.reshape(s[:axis] + (N_DEV, s[axis] // N_DEV) + s[axis + 1:])
    return jnp.moveaxis(g, axis, 0)


def kernel(x, p, a_pw1_w, a_pw1_b, a_dw_w, a_dw_b, a_ln_g, a_ln_b, a_pw2_w, a_pw2_b, b_wq, kv_wk, kv_wv, b_wo, ln_mix_g, ln_mix_b, ffn_w_up, ffn_w_gate, ffn_conv_w, ffn_conv_b, ffn_w_down, ple_w_gate, ple_w_proj, ln_ffn_g, ln_ffn_b, loss_target, m_a_pw1_w, m_a_pw1_b, m_a_dw_w, m_a_dw_b, m_a_ln_g, m_a_ln_b, m_a_pw2_w, m_a_pw2_b, m_b_wq, m_kv_wk, m_kv_wv, m_b_wo, m_ln_mix_g, m_ln_mix_b, m_ffn_w_up, m_ffn_w_gate, m_ffn_conv_w, m_ffn_conv_b, m_ffn_w_down, m_ple_w_gate, m_ple_w_proj, m_ln_ffn_g, m_ln_ffn_b, v_a_pw1_w, v_a_pw1_b, v_a_dw_w, v_a_dw_b, v_a_ln_g, v_a_ln_b, v_a_pw2_w, v_a_pw2_b, v_b_wq, v_kv_wk, v_kv_wv, v_b_wo, v_ln_mix_g, v_ln_mix_b, v_ffn_w_up, v_ffn_w_gate, v_ffn_conv_w, v_ffn_conv_b, v_ffn_w_down, v_ple_w_gate, v_ple_w_proj, v_ln_ffn_g, v_ln_ffn_b):
    local = dict(a_pw1_w=a_pw1_w, a_pw1_b=a_pw1_b, a_dw_w=a_dw_w, a_dw_b=a_dw_b, a_ln_g=a_ln_g, a_ln_b=a_ln_b, a_pw2_w=a_pw2_w, a_pw2_b=a_pw2_b, b_wq=b_wq, kv_wk=kv_wk, kv_wv=kv_wv, b_wo=b_wo, ln_mix_g=ln_mix_g, ln_mix_b=ln_mix_b, ffn_w_up=ffn_w_up, ffn_w_gate=ffn_w_gate, ffn_conv_w=ffn_conv_w, ffn_conv_b=ffn_conv_b, ffn_w_down=ffn_w_down, ple_w_gate=ple_w_gate, ple_w_proj=ple_w_proj, ln_ffn_g=ln_ffn_g, ln_ffn_b=ln_ffn_b)
    mom1 = dict(a_pw1_w=m_a_pw1_w, a_pw1_b=m_a_pw1_b, a_dw_w=m_a_dw_w, a_dw_b=m_a_dw_b, a_ln_g=m_a_ln_g, a_ln_b=m_a_ln_b, a_pw2_w=m_a_pw2_w, a_pw2_b=m_a_pw2_b, b_wq=m_b_wq, kv_wk=m_kv_wk, kv_wv=m_kv_wv, b_wo=m_b_wo, ln_mix_g=m_ln_mix_g, ln_mix_b=m_ln_mix_b, ffn_w_up=m_ffn_w_up, ffn_w_gate=m_ffn_w_gate, ffn_conv_w=m_ffn_conv_w, ffn_conv_b=m_ffn_conv_b, ffn_w_down=m_ffn_w_down, ple_w_gate=m_ple_w_gate, ple_w_proj=m_ple_w_proj, ln_ffn_g=m_ln_ffn_g, ln_ffn_b=m_ln_ffn_b)
    mom2 = dict(a_pw1_w=v_a_pw1_w, a_pw1_b=v_a_pw1_b, a_dw_w=v_a_dw_w, a_dw_b=v_a_dw_b, a_ln_g=v_a_ln_g, a_ln_b=v_a_ln_b, a_pw2_w=v_a_pw2_w, a_pw2_b=v_a_pw2_b, b_wq=v_b_wq, kv_wk=v_kv_wk, kv_wv=v_kv_wv, b_wo=v_b_wo, ln_mix_g=v_ln_mix_g, ln_mix_b=v_ln_mix_b, ffn_w_up=v_ffn_w_up, ffn_w_gate=v_ffn_w_gate, ffn_conv_w=v_ffn_conv_w, ffn_conv_b=v_ffn_conv_b, ffn_w_down=v_ffn_w_down, ple_w_gate=v_ple_w_gate, ple_w_proj=v_ple_w_proj, ln_ffn_g=v_ln_ffn_g, ln_ffn_b=v_ln_ffn_b)
    big_names = [n for n, _ in BIG]
    small_names = [n for n, _ in SMALL]
    big_shapes = [local[n].shape for n in big_names]
    small_shapes = [local[n].shape for n in small_names]
    repl_shapes = [local[n].shape for n in REPL]

    gbig, gsmall = all_gather(_pack([local[n] for n in big_names], BF16, 16),
                              _pack([local[n] for n in small_names], F32, SUBLANES))
    W = {n: _join(g, ax) for (n, ax), g in zip(BIG, _unpack(gbig, big_shapes))}
    W.update({n: _join(g, ax) for (n, ax), g in zip(SMALL, _unpack(gsmall, small_shapes))})
    W.update({n: local[n] for n in REPL})

    xs = x[0]
    S, D = xs.shape
    x_in, r1s, x1s, r2s, us, gps, gs, hhs, pgls, pps = [], [], [], [], [], [], [], [], [], []
    h1s, h2s, h3s, h5s, qs, os_, tots = {}, {}, {}, {}, {}, {}, {}
    kk = vv = None
    for i in range(DEPTH):
        x_in.append(xs)
        if i < N_A:
            h1 = mm(xs, W["a_pw1_w"][i], "nn", bias=W["a_pw1_b"][i][None], name=f"pw1_{i}")
            h2 = glu_fwd(h1, name=f"glu_{i}")
            h3, h5 = conv_ln_silu_fwd(h2, W["a_dw_w"][i], W["a_dw_b"][i][None], W["a_ln_g"][i][None],
                                      W["a_ln_b"][i][None], name=f"dwconv_{i}")
            mix = mm(h5, W["a_pw2_w"][i], "nn", bias=W["a_pw2_b"][i][None], name=f"pw2_{i}")
            h1s[i], h2s[i], h3s[i], h5s[i] = h1, h2, h3, h5
        else:
            j = i - N_A
            if kk is None:
                kk = mm(xs, W["kv_wk"], "nn", out_dtype=BF16, name="proj_k")
                vv = mm(xs, W["kv_wv"], "nn", out_dtype=BF16, name="proj_v")
            q = mm(xs, W["b_wq"][j], "nn", out_dtype=BF16, name=f"proj_q_{i}")
            o, tot = attn_fwd(q, kk, vv, name=f"attn_{i}")
            mix = mm(o, W["b_wo"][j], "nn", name=f"proj_o_{i}")
            qs[i], os_[i], tots[i] = q, o, tot
        r1, x1 = res_ln(xs, mix, W["ln_mix_g"][i][None], W["ln_mix_b"][i][None], name=f"ln_mix_{i}")
        u = mm(x1, W["ffn_w_up"][i], "nn", name=f"ffn_up_{i}")
        gp = mm(x1, W["ffn_w_gate"][i], "nn", name=f"ffn_gate_{i}")
        g, hh = conv_act_fwd(gp, u, W["ffn_conv_w"][i], W["ffn_conv_b"][i][None], name=f"ffn_conv_{i}")
        f = mm(hh, W["ffn_w_down"][i], "nn", name=f"ffn_down_{i}")
        pgl = mm(x1, W["ple_w_gate"][i], "nn", name=f"ple_gate_{i}")
        pp = mm(p[i, 0], W["ple_w_proj"][i], "nn", name=f"ple_proj_{i}")
        r2, xs = res_ln(x1, f, W["ln_ffn_g"][i][None], W["ln_ffn_b"][i][None], ple=(pgl, pp), name=f"ln_ffn_{i}")
        for lst, val in ((r1s, r1), (x1s, x1), (r2s, r2), (us, u), (gps, gp), (gs, g), (hhs, hh), (pgls, pgl), (pps, pp)):
            lst.append(val)

    dx, loss_part = loss_grad(xs, loss_target[0], name="loss")
    G = {n: [None] * local[n].shape[0] for n in WEIGHTS if n not in ("kv_wk", "kv_wv")}
    dk = dv = None
    for i in reversed(range(DEPTH)):
        x1 = x1s[i]
        dr2, G["ln_ffn_g"][i], G["ln_ffn_b"][i], _ = ln_bwd(r2s[i], W["ln_ffn_g"][i][None], dx, name=f"ln_ffn_bwd_{i}")
        dhh = mm(dr2, W["ffn_w_down"][i], "nt", name=f"ffn_down_dx_{i}")
        G["ffn_w_down"][i] = mm(hhs[i], dr2, "tn", name=f"ffn_down_dw_{i}")
        dpp, dpgl = ple_bwd(dr2, pgls[i], pps[i], name=f"ple_bwd_{i}")
        G["ple_w_proj"][i] = mm(p[i, 0], dpp, "tn", name=f"ple_proj_dw_{i}")
        G["ple_w_gate"][i] = mm(x1, dpgl, "tn", name=f"ple_gate_dw_{i}")
        du, dg = ffn_act_bwd(dhh, us[i], gs[i], name=f"ffn_act_bwd_{i}")
        dgp = conv_bwd_x(dg, W["ffn_conv_w"][i], out_dtype=BF16, name=f"ffn_conv_dx_{i}")
        G["ffn_conv_w"][i], G["ffn_conv_b"][i] = conv_bwd_w(gps[i], dg, FFN_CONV_W, name=f"ffn_conv_dw_{i}")
        G["ffn_w_up"][i] = mm(x1, du, "tn", name=f"ffn_up_dw_{i}")
        G["ffn_w_gate"][i] = mm(x1, dgp, "tn", name=f"ffn_gate_dw_{i}")
        dx1 = mm(du, W["ffn_w_up"][i], "nt", add=dr2, add_scale=DN_ALPHA, name=f"ffn_up_dx_{i}")
        dx1 = mm(dgp, W["ffn_w_gate"][i], "nt", add=dx1, name=f"ffn_gate_dx_{i}")
        dx1 = mm(dpgl, W["ple_w_gate"][i], "nt", add=dx1, name=f"ple_gate_dx_{i}")
        dr1, G["ln_mix_g"][i], G["ln_mix_b"][i], dr1_sum = ln_bwd(r1s[i], W["ln_mix_g"][i][None], dx1, name=f"ln_mix_bwd_{i}")
        if i < N_A:
            G["a_pw2_w"][i] = mm(h5s[i], dr1, "tn", name=f"pw2_dw_{i}")
            G["a_pw2_b"][i] = dr1_sum
            dh5 = mm(dr1, W["a_pw2_w"][i], "nt", name=f"pw2_dx_{i}")
            dh3, G["a_ln_g"][i], G["a_ln_b"][i] = ln_silu_bwd(h3s[i], W["a_ln_g"][i][None], W["a_ln_b"][i][None], dh5,
                                                             name=f"dwconv_ln_bwd_{i}")
            dh2 = conv_bwd_x(dh3, W["a_dw_w"][i], out_dtype=F32, name=f"dwconv_dx_{i}")
            G["a_dw_w"][i], G["a_dw_b"][i] = conv_bwd_w(h2s[i], dh3, CONV_W, name=f"dwconv_dw_{i}")
            dh1, G["a_pw1_b"][i] = glu_bwd(h1s[i], dh2, name=f"glu_bwd_{i}")
            G["a_pw1_w"][i] = mm(x_in[i], dh1, "tn", name=f"pw1_dw_{i}")
            dx = mm(dh1, W["a_pw1_w"][i], "nt", add=dr1, add_scale=DN_ALPHA, name=f"pw1_dx_{i}")
        else:
            j = i - N_A
            G["b_wo"][j] = mm(os_[i], dr1, "tn", name=f"proj_o_dw_{i}")
            do = mm(dr1, W["b_wo"][j], "nt", name=f"proj_o_dx_{i}")
            dq, dk, dv = attn_bwd(qs[i], kk, vv, tots[i], do, dk, dv, name=f"attn_bwd_{i}")
            G["b_wq"][j] = mm(x_in[i], dq, "tn", name=f"proj_q_dw_{i}")
            dx = mm(dq, W["b_wq"][j], "nt", add=dr1, add_scale=DN_ALPHA, name=f"proj_q_dx_{i}")
            if j == 0:
                G["kv_wk"] = mm(x_in[i], dk, "tn", name="proj_k_dw")
                G["kv_wv"] = mm(x_in[i], dv, "tn", name="proj_v_dw")
                dx = mm(dk, W["kv_wk"], "nt", add=dx, name="proj_k_dx")
                dx = mm(dv, W["kv_wv"], "nt", add=dx, name="proj_v_dx")
    grad_x = dx[None]
    shard_axis = dict(BIG + SMALL)
    for n in WEIGHTS:
        full = list(local[n].shape)
        if n in shard_axis:
            full[shard_axis[n]] *= N_DEV
        G[n] = (G[n] if n in ("kv_wk", "kv_wv") else jnp.stack(G[n])).reshape(full)

    n_small = sum(math.prod(s) for s in small_shapes)
    n_repl = sum(math.prod(s) for s in repl_shapes)
    send_big = _pack_dev([_split(G[n], ax) for n, ax in BIG], BF16, 16)
    repl_flat = jnp.concatenate([G[n].reshape(-1) for n in REPL] + [loss_part.reshape(-1)[:1]])
    send_small = _pack_dev([_split(G[n], ax) for n, ax in SMALL] + [jnp.broadcast_to(repl_flat, (N_DEV, n_repl + 1))],
                           F32, SUBLANES)
    recv_big, recv_small = exchange_grads(send_big, send_small)

    def state(d):
        big = _pack([d[n] for n in big_names], F32, 16)
        small = _pack([d[n] for n in small_names] + [d[n] for n in REPL], F32, SUBLANES)
        return big, jnp.pad(small, ((0, recv_small.shape[1] - small.shape[0]), (0, 0)))

    (w_big, w_small), (m_big, m_small), (v_big, v_small) = state(local), state(mom1), state(mom2)
    out_big = adamw(recv_big, w_big, m_big, v_big, name="adamw_matrices")
    out_small = adamw(recv_small, w_small, m_small, v_small, name="adamw_vectors")
    loss = out_small[0].reshape(-1)[n_small + n_repl]
    per_kind = []
    for ob, osm in zip(out_big, out_small):
        vals = dict(zip(big_names, _unpack(ob, big_shapes)))
        vals.update(zip(small_names + list(REPL), _unpack(osm, small_shapes + repl_shapes)))
        per_kind.append([vals[n] for n in WEIGHTS])
    grads, deltas, new_m, new_v = per_kind
    return (loss, grad_x, *grads, *deltas, *new_m, *new_v)


def _pack_dev(arrs, dtype, row_mult):
    flat = jnp.concatenate([a.reshape(N_DEV, -1).astype(dtype) for a in arrs], axis=1)
    rows = -(-flat.shape[1] // PACK_W)
    rows = -(-rows // row_mult) * row_mult
    return jnp.pad(flat, ((0, 0), (0, rows * PACK_W - flat.shape[1]))).reshape(N_DEV, rows, PACK_W)
```

```python
import functools
import math

import jax
import jax.numpy as jnp
from jax import lax
from jax.experimental import pallas as pl
from jax.experimental.pallas import tpu as pltpu

F32 = jnp.float32
BF16 = jnp.bfloat16
MESH = pl.DeviceIdType.MESH

N_DEV = 8
DEPTH = 4
N_A = 2
HEAD_DIM = 64
Q_BLOCK = 128
CONV_W = 31
FFN_CONV_W = 3
LN_EPS = 1e-5
DN_ALPHA = (2.0 * DEPTH) ** 0.25
ADAM_LR = 0.001
ADAM_B1 = 0.9
ADAM_B2 = 0.999
ADAM_EPS = 1e-08
ADAM_WD = 0.01
ADAM_STEP = 10

LANES = 128
SUBLANES = 8
PACK_W = 1024
VMEM_LIMIT = 56 * 1024 * 1024

BIG = (("a_pw1_w", 2), ("a_pw2_w", 1), ("b_wq", 1), ("kv_wk", 0), ("kv_wv", 0), ("b_wo", 1),
       ("ffn_w_up", 2), ("ffn_w_gate", 2), ("ffn_w_down", 1), ("ple_w_gate", 1), ("ple_w_proj", 2))
SMALL = (("a_pw1_b", 1), ("a_dw_w", 2), ("a_dw_b", 1), ("a_ln_g", 1), ("a_ln_b", 1), ("a_pw2_b", 1),
         ("ffn_conv_w", 2))
REPL = ("ln_mix_g", "ln_mix_b", "ffn_conv_b", "ln_ffn_g", "ln_ffn_b")
WEIGHTS = ("a_pw1_w", "a_pw1_b", "a_dw_w", "a_dw_b", "a_ln_g", "a_ln_b", "a_pw2_w", "a_pw2_b", "b_wq", "kv_wk",
           "kv_wv", "b_wo", "ln_mix_g", "ln_mix_b", "ffn_w_up", "ffn_w_gate", "ffn_conv_w", "ffn_conv_b",
           "ffn_w_down", "ple_w_gate", "ple_w_proj", "ln_ffn_g", "ln_ffn_b")


def _cp(*sem):
    return pltpu.CompilerParams(dimension_semantics=sem, vmem_limit_bytes=VMEM_LIMIT)


def _pick(dim, target, align=LANES):
    if dim <= target:
        return dim
    t = (target // align) * align
    while t >= align:
        if dim % t == 0:
            return t
        t -= align
    return dim


_DOT_DIMS = {"nn": (((1,), (0,)), ((), ())), "nt": (((1,), (1,)), ((), ())), "tn": (((0,), (0,)), ((), ()))}


def mm(a, b, mode, *, bias=None, add=None, add_scale=1.0, out_dtype=F32, name):
    if mode == "tn":
        K, M = a.shape
    else:
        M, K = a.shape
    N = b.shape[0] if mode == "nt" else b.shape[1]
    tm, tn, tk = _pick(M, 512), _pick(N, 1536), _pick(K, 1536)
    nk = K // tk
    dims = _DOT_DIMS[mode]

    def body(*refs):
        a_ref, b_ref = refs[0], refs[1]
        pos = 2
        bias_ref = add_ref = None
        if bias is not None:
            bias_ref = refs[pos]
            pos += 1
        if add is not None:
            add_ref = refs[pos]
            pos += 1
        o_ref, acc_ref = refs[pos], refs[pos + 1]
        k = pl.program_id(2)

        @pl.when(k == 0)
        def _():
            acc_ref[...] = jnp.zeros_like(acc_ref)

        acc_ref[...] += lax.dot_general(a_ref[...].astype(BF16), b_ref[...].astype(BF16), dims,
                                        preferred_element_type=F32)

        @pl.when(k == nk - 1)
        def _():
            r = acc_ref[...]
            if bias_ref is not None:
                r = r + bias_ref[...]
            if add_ref is not None:
                r = r + add_scale * add_ref[...].astype(F32)
            o_ref[...] = r.astype(o_ref.dtype)

    a_spec = pl.BlockSpec((tk, tm), lambda i, j, k: (k, i)) if mode == "tn" else pl.BlockSpec((tm, tk), lambda i, j, k: (i, k))
    b_spec = pl.BlockSpec((tn, tk), lambda i, j, k: (j, k)) if mode == "nt" else pl.BlockSpec((tk, tn), lambda i, j, k: (k, j))
    in_specs, args = [a_spec, b_spec], [a, b]
    if bias is not None:
        in_specs.append(pl.BlockSpec((1, tn), lambda i, j, k: (0, j)))
        args.append(bias)
    if add is not None:
        in_specs.append(pl.BlockSpec((tm, tn), lambda i, j, k: (i, j)))
        args.append(add)
    return pl.pallas_call(
        body, grid=(M // tm, N // tn, nk), in_specs=in_specs,
        out_specs=pl.BlockSpec((tm, tn), lambda i, j, k: (i, j)),
        out_shape=jax.ShapeDtypeStruct((M, N), out_dtype),
        scratch_shapes=[pltpu.VMEM((tm, tn), F32)],
        compiler_params=_cp("parallel", "parallel", "arbitrary"), name=name)(*args)


def _rows(body, *, n_rows, tm, row_ins, full_ins=(), row_outs=(), acc_outs=(), scratch=(), reverse=False, name):
    n = n_rows // tm

    def rmap(i):
        return (n - 1 - i, 0) if reverse else (i, 0)

    in_specs = [pl.BlockSpec((tm, a.shape[1]), rmap) for a in row_ins]
    in_specs += [pl.BlockSpec(a.shape, lambda i, nd=a.ndim: (0,) * nd) for a in full_ins]
    out_shape = [jax.ShapeDtypeStruct((n_rows, w), dt) for (w, dt) in row_outs]
    out_shape += [jax.ShapeDtypeStruct(s, dt) for (s, dt) in acc_outs]
    out_specs = [pl.BlockSpec((tm, w), rmap) for (w, dt) in row_outs]
    out_specs += [pl.BlockSpec(s, lambda i, nd=len(s): (0,) * nd) for (s, dt) in acc_outs]
    return pl.pallas_call(
        functools.partial(body, n), grid=(n,), in_specs=in_specs, out_specs=out_specs, out_shape=out_shape,
        scratch_shapes=list(scratch), compiler_params=_cp("arbitrary"), name=name)(*row_ins, *full_ins)


def _sigmoid(x):
    return 1.0 / (1.0 + jnp.exp(-x))


def _ln_hat(r):
    mu = jnp.mean(r, axis=-1, keepdims=True)
    xc = r - mu
    var = jnp.mean(xc * xc, axis=-1, keepdims=True)
    rstd = lax.rsqrt(var + LN_EPS)
    return xc * rstd, rstd


def _ln_back(xhat, rstd, g, dy):
    dxh = dy * g
    m1 = jnp.mean(dxh, axis=-1, keepdims=True)
    m2 = jnp.mean(dxh * xhat, axis=-1, keepdims=True)
    return rstd * (dxh - m1 - xhat * m2)


def _colsum(x):
    return jnp.sum(x, axis=0, keepdims=True)


def _acc(i, ref, val):
    @pl.when(i == 0)
    def _():
        ref[...] = val

    @pl.when(i > 0)
    def _():
        ref[...] += val


def res_ln(x, mix, g, b, *, ple=None, name):
    S, D = x.shape

    def body(n, *refs):
        if ple is None:
            x_ref, m_ref, g_ref, b_ref, r_ref, y_ref = refs
            r = DN_ALPHA * x_ref[...] + m_ref[...]
        else:
            x_ref, m_ref, pgl_ref, pp_ref, g_ref, b_ref, r_ref, y_ref = refs
            r = DN_ALPHA * x_ref[...] + m_ref[...] + _sigmoid(pgl_ref[...]) * pp_ref[...]
        xhat, _ = _ln_hat(r)
        r_ref[...] = r
        y_ref[...] = xhat * g_ref[...] + b_ref[...]

    row_ins = [x, mix] + ([] if ple is None else list(ple))
    return _rows(body, n_rows=S, tm=_pick(S, 256, SUBLANES), row_ins=row_ins, full_ins=[g, b],
                 row_outs=[(D, F32), (D, F32)], name=name)


def ln_bwd(r, g, dy, *, name):
    S, D = r.shape

    def body(n, r_ref, dy_ref, g_ref, dr_ref, dg_ref, db_ref, ds_ref):
        i = pl.program_id(0)
        xhat, rstd = _ln_hat(r_ref[...])
        dy_v = dy_ref[...]
        dr = _ln_back(xhat, rstd, g_ref[...], dy_v)
        dr_ref[...] = dr
        _acc(i, dg_ref, _colsum(dy_v * xhat))
        _acc(i, db_ref, _colsum(dy_v))
        _acc(i, ds_ref, _colsum(dr))

    return _rows(body, n_rows=S, tm=_pick(S, 256, SUBLANES), row_ins=[r, dy], full_ins=[g],
                 row_outs=[(D, F32)], acc_outs=[((1, D), F32)] * 3, name=name)


def glu_fwd(h1, *, name):
    S, D2 = h1.shape
    D = D2 // 2

    def body(n, h_ref, o_ref):
        o_ref[...] = h_ref[:, :D] * _sigmoid(h_ref[:, D:])

    return _rows(body, n_rows=S, tm=_pick(S, 256, SUBLANES), row_ins=[h1], row_outs=[(D, F32)], name=name)[0]


def glu_bwd(h1, dh2, *, name):
    S, D2 = h1.shape
    D = D2 // 2

    def body(n, h_ref, d_ref, o_ref, s_ref):
        i = pl.program_id(0)
        a, sg, d = h_ref[:, :D], _sigmoid(h_ref[:, D:]), d_ref[...]
        da = d * sg
        dg = d * a * sg * (1.0 - sg)
        o_ref[:, :D] = da.astype(o_ref.dtype)
        o_ref[:, D:] = dg.astype(o_ref.dtype)
        _acc(i, s_ref, jnp.concatenate([_colsum(da), _colsum(dg)], axis=1))

    return _rows(body, n_rows=S, tm=_pick(S, 256, SUBLANES), row_ins=[h1, dh2], row_outs=[(D2, BF16)],
                 acc_outs=[((1, D2), F32)], name=name)


CONV_ROWS = 32
CONV_LANES = 256


def _halo(k):
    return -(-(k - 1) // SUBLANES) * SUBLANES


def _conv_taps(buf_ref, w_ref, offs, tm, width, emit):
    rows = min(CONV_ROWS, tm)
    for lc in range(0, width, CONV_LANES):
        lw = min(CONV_LANES, width - lc)
        for rc in range(0, tm, rows):
            acc = None
            for k, off in enumerate(offs):
                t = buf_ref[pl.ds(off + rc, rows), pl.ds(lc, lw)] * w_ref[pl.ds(k, 1), pl.ds(lc, lw)]
                acc = t if acc is None else acc + t
            emit(rc, lc, lw, rows, acc)


def _fill_causal(i, buf_ref, x_ref, halo, tm):
    @pl.when(i == 0)
    def _():
        buf_ref[pl.ds(0, halo), :] = jnp.zeros((halo, buf_ref.shape[1]), F32)

    @pl.when(i > 0)
    def _():
        buf_ref[pl.ds(0, halo), :] = buf_ref[pl.ds(tm, halo), :]

    buf_ref[pl.ds(halo, tm), :] = x_ref[...]


def conv_ln_silu_fwd(x, w, b, g, beta, *, name):
    S, C = x.shape
    K = w.shape[0]
    halo = _halo(K)
    tm = _pick(S, 256, SUBLANES)
    offs = [halo - (K - 1) + k for k in range(K)]

    def body(n, x_ref, w_ref, b_ref, g_ref, beta_ref, h3_ref, h5_ref, buf_ref):
        i = pl.program_id(0)
        _fill_causal(i, buf_ref, x_ref, halo, tm)

        def emit(rc, lc, lw, rows, acc):
            h3_ref[pl.ds(rc, rows), pl.ds(lc, lw)] = acc + b_ref[:, pl.ds(lc, lw)]

        _conv_taps(buf_ref, w_ref, offs, tm, C, emit)
        xhat, _ = _ln_hat(h3_ref[...])
        h4 = xhat * g_ref[...] + beta_ref[...]
        h5_ref[...] = (h4 * _sigmoid(h4)).astype(h5_ref.dtype)

    return _rows(body, n_rows=S, tm=tm, row_ins=[x], full_ins=[w, b, g, beta], row_outs=[(C, F32), (C, BF16)],
                 scratch=[pltpu.VMEM((tm + halo, C), F32)], name=name)


def conv_act_fwd(gp, u, w, b, *, name):
    S, C = gp.shape
    K = w.shape[0]
    halo = _halo(K)
    tm = _pick(S, 256, SUBLANES)
    offs = [halo - (K - 1) + k for k in range(K)]

    def body(n, x_ref, u_ref, w_ref, b_ref, g_ref, hh_ref, buf_ref):
        i = pl.program_id(0)
        _fill_causal(i, buf_ref, x_ref, halo, tm)

        def emit(rc, lc, lw, rows, acc):
            gv = acc + b_ref[:, pl.ds(lc, lw)]
            g_ref[pl.ds(rc, rows), pl.ds(lc, lw)] = gv
            hh_ref[pl.ds(rc, rows), pl.ds(lc, lw)] = (gv * _sigmoid(gv) * u_ref[pl.ds(rc, rows), pl.ds(lc, lw)]).astype(hh_ref.dtype)

        _conv_taps(buf_ref, w_ref, offs, tm, C, emit)

    return _rows(body, n_rows=S, tm=tm, row_ins=[gp, u], full_ins=[w, b], row_outs=[(C, F32), (C, BF16)],
                 scratch=[pltpu.VMEM((tm + halo, C), F32)], name=name)


def conv_bwd_x(dy, w, *, out_dtype, name):
    S, C = dy.shape
    K = w.shape[0]
    halo = _halo(K)
    tm = _pick(S, 256, SUBLANES)
    offs = [K - 1 - k for k in range(K)]

    def body(n, dy_ref, w_ref, dx_ref, buf_ref):
        i = pl.program_id(0)

        @pl.when(i == 0)
        def _():
            buf_ref[pl.ds(tm, halo), :] = jnp.zeros((halo, C), F32)

        @pl.when(i > 0)
        def _():
            buf_ref[pl.ds(tm, halo), :] = buf_ref[pl.ds(0, halo), :]

        buf_ref[pl.ds(0, tm), :] = dy_ref[...]

        def emit(rc, lc, lw, rows, acc):
            dx_ref[pl.ds(rc, rows), pl.ds(lc, lw)] = acc.astype(dx_ref.dtype)

        _conv_taps(buf_ref, w_ref, offs, tm, C, emit)

    return _rows(body, n_rows=S, tm=tm, row_ins=[dy], full_ins=[w], row_outs=[(C, out_dtype)],
                 scratch=[pltpu.VMEM((tm + halo, C), F32)], reverse=True, name=name)[0]


def conv_bwd_w(x, dy, K, *, name):
    S, C = x.shape
    halo = _halo(K)
    tm = _pick(S, 256, SUBLANES)
    offs = [halo - (K - 1) + k for k in range(K)]
    rows = min(CONV_ROWS, tm)

    def body(n, x_ref, dy_ref, dw_ref, db_ref, buf_ref, acc_ref):
        i = pl.program_id(0)
        _fill_causal(i, buf_ref, x_ref, halo, tm)

        @pl.when(i == 0)
        def _():
            acc_ref[...] = jnp.zeros_like(acc_ref)

        for lc in range(0, C, CONV_LANES):
            lw = min(CONV_LANES, C - lc)
            for k, off in enumerate(offs):
                s = None
                for rc in range(0, tm, rows):
                    t = dy_ref[pl.ds(rc, rows), pl.ds(lc, lw)] * buf_ref[pl.ds(off + rc, rows), pl.ds(lc, lw)]
                    s = t if s is None else s + t
                s8 = s[0:SUBLANES]
                for q in range(1, rows // SUBLANES):
                    s8 = s8 + s[q * SUBLANES:(q + 1) * SUBLANES]
                acc_ref[pl.ds(k * SUBLANES, SUBLANES), pl.ds(lc, lw)] += s8
        _acc(i, db_ref, _colsum(dy_ref[...]))

        @pl.when(i == n - 1)
        def _():
            for k in range(K):
                dw_ref[pl.ds(k, 1), :] = _colsum(acc_ref[pl.ds(k * SUBLANES, SUBLANES), :])

    return _rows(body, n_rows=S, tm=tm, row_ins=[x, dy], acc_outs=[((K, C), F32), ((1, C), F32)],
                 scratch=[pltpu.VMEM((tm + halo, C), F32), pltpu.VMEM((K * SUBLANES, C), F32)], name=name)


def ln_silu_bwd(h3, g, beta, dh5, *, name):
    S, C = h3.shape

    def body(n, h_ref, d_ref, g_ref, beta_ref, dh_ref, dg_ref, db_ref):
        i = pl.program_id(0)
        xhat, rstd = _ln_hat(h_ref[...])
        h4 = xhat * g_ref[...] + beta_ref[...]
        sg = _sigmoid(h4)
        dh4 = d_ref[...] * sg * (1.0 + h4 * (1.0 - sg))
        dh_ref[...] = _ln_back(xhat, rstd, g_ref[...], dh4)
        _acc(i, dg_ref, _colsum(dh4 * xhat))
        _acc(i, db_ref, _colsum(dh4))

    return _rows(body, n_rows=S, tm=_pick(S, 256, SUBLANES), row_ins=[h3, dh5], full_ins=[g, beta],
                 row_outs=[(C, F32)], acc_outs=[((1, C), F32)] * 2, name=name)


def ffn_act_bwd(dhh, u, g, *, name):
    S, C = u.shape

    def body(n, d_ref, u_ref, g_ref, du_ref, dg_ref):
        d, gv = d_ref[...], g_ref[...]
        sg = _sigmoid(gv)
        du_ref[...] = (d * gv * sg).astype(du_ref.dtype)
        dg_ref[...] = d * u_ref[...] * sg * (1.0 + gv * (1.0 - sg))

    return _rows(body, n_rows=S, tm=_pick(S, 256, SUBLANES), row_ins=[dhh, u, g], row_outs=[(C, BF16), (C, F32)],
                 name=name)


def ple_bwd(dr, pgl, pp, *, name):
    S, D = dr.shape

    def body(n, d_ref, l_ref, p_ref, dpp_ref, dpl_ref):
        d, sg = d_ref[...], _sigmoid(l_ref[...])
        dpp_ref[...] = (d * sg).astype(dpp_ref.dtype)
        dpl_ref[...] = (d * p_ref[...] * sg * (1.0 - sg)).astype(dpl_ref.dtype)

    return _rows(body, n_rows=S, tm=_pick(S, 256, SUBLANES), row_ins=[dr, pgl, pp], row_outs=[(D, BF16), (D, BF16)],
                 name=name)


def loss_grad(y, target, *, name):
    S, D = y.shape

    def body(n, y_ref, t_ref, dy_ref, l_ref):
        i = pl.program_id(0)
        e = y_ref[...] - t_ref[...]
        dy_ref[...] = e * (1.0 / D)
        s = jnp.sum(_colsum(e * e), axis=1, keepdims=True) * (0.5 / D)
        _acc(i, l_ref, jnp.broadcast_to(s, (1, LANES)))

    return _rows(body, n_rows=S, tm=_pick(S, 256, SUBLANES), row_ins=[y, target], row_outs=[(D, F32)],
                 acc_outs=[((1, LANES), F32)], name=name)


def _key_step(S):
    return min(512, S // 2)


def _attn_consts():
    lane = lax.broadcasted_iota(jnp.int32, (1, LANES), 1)
    heads = (lane < HEAD_DIM, lane >= HEAD_DIM)
    row = lax.broadcasted_iota(jnp.int32, (Q_BLOCK, Q_BLOCK), 0)
    col = lax.broadcasted_iota(jnp.int32, (Q_BLOCK, Q_BLOCK), 1)
    return heads, row, col


def _tri2(cond):
    t = jnp.where(cond, 1.0, 0.0).astype(BF16)
    return jnp.concatenate([t, t], axis=0)


def _keysum(x, tri2):
    hi = x.astype(BF16)
    lo = (x - hi.astype(F32)).astype(BF16)
    return jnp.dot(jnp.concatenate([hi, lo], axis=1), tri2, preferred_element_type=F32)


def _stack_heads(x, heads):
    return jnp.concatenate([jnp.where(m, x, jnp.zeros_like(x)) for m in heads], axis=0)


def _log1m_beta(z):
    return -(jnp.maximum(z, 0.0) + jnp.log(1.0 + jnp.exp(-jnp.abs(z))))


def _block_mask(row, col, off):
    if off is None:
        return None
    m = (col + off) < row
    return jnp.concatenate([m, m], axis=0)


def attn_fwd(q, k, v, *, name):
    S, D = q.shape
    nb = S // Q_BLOCK
    tk = _key_step(S)
    nkb = tk // Q_BLOCK
    scale = 1.0 / math.sqrt(HEAD_DIM)

    def body(q_ref, k_ref, v_ref, o_ref, tot_ref, vm_ref):
        heads, row, col = _attn_consts()
        above = _tri2(row > col)
        for h in range(2):
            vm_ref[h] = jnp.where(heads[h], v_ref[...], jnp.zeros_like(v_ref[...]))

        def step(sb, carry, qq, r0, masked):
            acc, cl = carry
            c0 = pl.multiple_of(sb * tk, tk)
            z = lax.dot_general(qq, k_ref[pl.ds(c0, tk), :], _DOT_DIMS["nt"], preferred_element_type=F32) * scale
            zl, es, rs, masks = [], [], [], []
            for jb in range(nkb):
                zb = z[:, jb * Q_BLOCK:(jb + 1) * Q_BLOCK]
                lr = _log1m_beta(zb)
                mask = _block_mask(row, col, c0 + jb * Q_BLOCK - r0 if masked else None)
                l = lr if mask is None else jnp.where(mask, lr, 0.0)
                zl.append(zb + lr)
                es.append(_keysum(l, above))
                rs.append(jnp.sum(l, axis=1, keepdims=True))
                masks.append(mask)
            a = [None] * nkb
            for jb in reversed(range(nkb)):
                ab = jnp.exp(zl[jb] + es[jb] + cl)
                if masks[jb] is not None:
                    ab = jnp.where(masks[jb], ab, 0.0)
                a[jb] = ab.astype(BF16)
                cl = cl + rs[jb]
            a = jnp.concatenate(a, axis=1)
            for h in range(2):
                acc = acc + jnp.dot(a[h * Q_BLOCK:(h + 1) * Q_BLOCK], vm_ref[h, pl.ds(c0, tk), :],
                                    preferred_element_type=F32)
            return acc, cl

        def qblock(i, _):
            r0 = pl.multiple_of(i * Q_BLOCK, Q_BLOCK)
            qq = _stack_heads(q_ref[pl.ds(r0, Q_BLOCK), :], heads)
            last = i // nkb
            carry = (jnp.zeros((Q_BLOCK, LANES), F32), jnp.zeros((2 * Q_BLOCK, 1), F32))
            carry = step(last, carry, qq, r0, True)
            acc, cl = lax.fori_loop(0, last, lambda jj, c: step(last - 1 - jj, c, qq, r0, False), carry)
            o_ref[pl.ds(r0, Q_BLOCK), :] = acc
            tot_ref[pl.ds(r0, Q_BLOCK), :] = jnp.where(heads[0], cl[:Q_BLOCK], cl[Q_BLOCK:])
            return 0

        lax.fori_loop(0, nb, qblock, 0)

    spec = pl.BlockSpec((S, LANES), lambda h: (0, h))
    return pl.pallas_call(body, grid=(D // LANES,), in_specs=[spec] * 3, out_specs=[spec] * 2,
                          out_shape=[jax.ShapeDtypeStruct((S, D), F32)] * 2,
                          scratch_shapes=[pltpu.VMEM((2, S, LANES), BF16)], compiler_params=_cp("parallel"),
                          name=name)(q, k, v)


def attn_bwd(q, k, v, tot, do, dk0, dv0, *, name):
    S, D = q.shape
    nb = S // Q_BLOCK
    tk = _key_step(S)
    nkb = tk // Q_BLOCK
    scale = 1.0 / math.sqrt(HEAD_DIM)
    has_init = dk0 is not None

    def body(*refs):
        if has_init:
            q_ref, k_ref, v_ref, tot_ref, do_ref, dk0_ref, dv0_ref, dq_ref, dk_ref, dv_ref, km_ref = refs
            dk_ref[...] = dk0_ref[...]
            dv_ref[...] = dv0_ref[...]
        else:
            q_ref, k_ref, v_ref, tot_ref, do_ref, dq_ref, dk_ref, dv_ref, km_ref = refs
            dk_ref[...] = jnp.zeros_like(dk_ref)
            dv_ref[...] = jnp.zeros_like(dv_ref)
        heads, row, col = _attn_consts()
        upto = _tri2(row <= col)
        before = _tri2(row < col)
        for h in range(2):
            km_ref[h] = jnp.where(heads[h], k_ref[...], jnp.zeros_like(k_ref[...]))

        def step(sb, carry, qq, dd, totl, r0, masked):
            dq, pl_, pg = carry
            c0 = pl.multiple_of(sb * tk, tk)
            z = lax.dot_general(qq, k_ref[pl.ds(c0, tk), :], _DOT_DIMS["nt"], preferred_element_type=F32) * scale
            da = lax.dot_general(dd, v_ref[pl.ds(c0, tk), :], _DOT_DIMS["nt"], preferred_element_type=F32)
            a, dz = [None] * nkb, [None] * nkb
            for jb in range(nkb):
                zb = z[:, jb * Q_BLOCK:(jb + 1) * Q_BLOCK]
                lr = _log1m_beta(zb)
                mask = _block_mask(row, col, c0 + jb * Q_BLOCK - r0 if masked else None)
                l = lr if mask is None else jnp.where(mask, lr, 0.0)
                ab = jnp.exp(zb + lr + (totl - pl_ - _keysum(l, upto)))
                if mask is not None:
                    ab = jnp.where(mask, ab, 0.0)
                g = ab * da[:, jb * Q_BLOCK:(jb + 1) * Q_BLOCK]
                dzb = g * jnp.exp(lr) - jnp.exp(zb + lr) * (pg + _keysum(g, before))
                if mask is not None:
                    dzb = jnp.where(mask, dzb, 0.0)
                a[jb] = ab.astype(BF16)
                dz[jb] = (dzb * scale).astype(BF16)
                pl_ = pl_ + jnp.sum(l, axis=1, keepdims=True)
                pg = pg + jnp.sum(g, axis=1, keepdims=True)
            a = jnp.concatenate(a, axis=1)
            dz = jnp.concatenate(dz, axis=1)
            for h in range(2):
                dq = dq + jnp.dot(dz[h * Q_BLOCK:(h + 1) * Q_BLOCK], km_ref[h, pl.ds(c0, tk), :],
                                  preferred_element_type=F32)
            dk_ref[pl.ds(c0, tk), :] += lax.dot_general(dz, qq, _DOT_DIMS["tn"], preferred_element_type=F32)
            dv_ref[pl.ds(c0, tk), :] += lax.dot_general(a, dd, _DOT_DIMS["tn"], preferred_element_type=F32)
            return dq, pl_, pg

        def qblock(i, _):
            r0 = pl.multiple_of(i * Q_BLOCK, Q_BLOCK)
            qq = _stack_heads(q_ref[pl.ds(r0, Q_BLOCK), :], heads)
            dd = _stack_heads(do_ref[pl.ds(r0, Q_BLOCK), :].astype(BF16), heads)
            tot2 = tot_ref[pl.ds(r0, Q_BLOCK), :]
            totl = jnp.concatenate([tot2[:, 0:1], tot2[:, HEAD_DIM:HEAD_DIM + 1]], axis=0)
            last = i // nkb
            zc = jnp.zeros((2 * Q_BLOCK, 1), F32)
            carry = (jnp.zeros((Q_BLOCK, LANES), F32), zc, zc)
            carry = lax.fori_loop(0, last, lambda sb, c: step(sb, c, qq, dd, totl, r0, False), carry)
            carry = step(last, carry, qq, dd, totl, r0, True)
            dq_ref[pl.ds(r0, Q_BLOCK), :] = carry[0].astype(dq_ref.dtype)
            return 0

        lax.fori_loop(0, nb, qblock, 0)

    spec = pl.BlockSpec((S, LANES), lambda h: (0, h))
    args = [q, k, v, tot, do] + ([dk0, dv0] if has_init else [])
    return pl.pallas_call(
        body, grid=(D // LANES,), in_specs=[spec] * len(args), out_specs=[spec] * 3,
        out_shape=[jax.ShapeDtypeStruct((S, D), BF16), jax.ShapeDtypeStruct((S, D), F32), jax.ShapeDtypeStruct((S, D), F32)],
        scratch_shapes=[pltpu.VMEM((2, S, LANES), BF16)], compiler_params=_cp("parallel"), name=name)(*args)


def _dev_index(px, py, pc):
    return 4 * px + 2 * py + pc


def all_gather(bufs):
    nb = len(bufs)

    def body(*refs):
        ins, outs = refs[:nb], refs[nb:2 * nb]
        send_sems, recv_sems, local_sems = refs[2 * nb:]
        x, y, c = lax.axis_index("x"), lax.axis_index("y"), lax.axis_index("c")
        me, sibling = (x, y, c), (x, y, 1 - c)
        chips = [(1 - x, y), (x, 1 - y), (1 - x, 1 - y)]

        def copy(b, k, block, to, from_input=False):
            slot = outs[b].at[_dev_index(*block)]
            return pltpu.make_async_remote_copy(
                src_ref=ins[b] if from_input else slot, dst_ref=slot,
                send_sem=send_sems.at[7 * b + k], recv_sem=recv_sems.at[7 * b + k], device_id=to, device_id_type=MESH)

        mine = [pltpu.make_async_copy(ins[b], outs[b].at[_dev_index(*me)], local_sems.at[b]) for b in range(nb)]
        for cp in mine:
            cp.start()
        first = []
        for b in range(nb):
            first.append(copy(b, 0, me, sibling, from_input=True))
            first += [copy(b, 1 + j, me, (*chip, c), from_input=True) for j, chip in enumerate(chips)]
        for cp in first:
            cp.start()
        passed = []
        for j, chip in enumerate(chips):
            for b in range(nb):
                copy(b, 1 + j, (*chip, c), me).wait_recv()
                fwd = copy(b, 4 + j, (*chip, c), sibling)
                fwd.start()
                passed.append(fwd)
        for b in range(nb):
            copy(b, 0, sibling, me).wait_recv()
            for j, chip in enumerate(chips):
                copy(b, 4 + j, (*chip, 1 - c), me).wait_recv()
        for cp in first + passed:
            cp.wait_send()
        for cp in mine:
            cp.wait()

    any_spec = pl.BlockSpec(memory_space=pl.ANY)
    return pl.pallas_call(
        body, in_specs=[any_spec] * nb, out_specs=[any_spec] * nb,
        out_shape=[jax.ShapeDtypeStruct((N_DEV,) + b.shape, b.dtype) for b in bufs],
        scratch_shapes=[pltpu.SemaphoreType.DMA((7 * nb,)), pltpu.SemaphoreType.DMA((7 * nb,)),
                        pltpu.SemaphoreType.DMA((nb,))],
        name="all_gather_weights")(*bufs)


def exchange_grads(groups):
    srcs = [s for g in groups for (s, _) in g[3]]
    ns, ng = len(srcs), len(groups)

    def body(*refs):
        src_refs, outs = refs[:ns], refs[ns:ns + ng]
        send_sems, recv_sems, local_sems = refs[ns + ng:]
        x, y, c = lax.axis_index("x"), lax.axis_index("y"), lax.axis_index("c")
        me = _dev_index(x, y, c)
        layout = []
        si = 0
        for g, (_, _, _, lst) in enumerate(groups):
            for (s, off) in lst:
                layout.append((g, src_refs[si], off, s.shape[1]))
                si += 1
        mine = [pltpu.make_async_copy(ref.at[me], outs[g].at[me, pl.ds(off, r)], local_sems.at[i])
                for i, (g, ref, off, r) in enumerate(layout)]
        for cp in mine:
            cp.start()
        slots = []
        for flip in range(1, N_DEV):
            fx, fy, fc = (flip >> 2) & 1, (flip >> 1) & 1, flip & 1
            peer = (1 - x if fx else x, 1 - y if fy else y, 1 - c if fc else c)
            pidx = _dev_index(*peer)
            for (g, ref, off, r) in layout:
                k = 7 * g + flip - 1
                pltpu.make_async_remote_copy(
                    src_ref=ref.at[pidx], dst_ref=outs[g].at[me, pl.ds(off, r)], send_sem=send_sems.at[k],
                    recv_sem=recv_sems.at[k], device_id=peer, device_id_type=MESH).start()
            for g in range(ng):
                k = 7 * g + flip - 1
                slots.append(pltpu.make_async_remote_copy(
                    src_ref=outs[g].at[pidx], dst_ref=outs[g].at[pidx], send_sem=send_sems.at[k],
                    recv_sem=recv_sems.at[k], device_id=peer, device_id_type=MESH))
        for w in slots:
            w.wait_recv()
        for w in slots:
            w.wait_send()
        for cp in mine:
            cp.wait()

    any_spec = pl.BlockSpec(memory_space=pl.ANY)
    return pl.pallas_call(
        body, in_specs=[any_spec] * ns, out_specs=[any_spec] * ng,
        out_shape=[jax.ShapeDtypeStruct((N_DEV, r, w), dt) for (r, w, dt, _) in groups],
        scratch_shapes=[pltpu.SemaphoreType.DMA((7 * ng,)), pltpu.SemaphoreType.DMA((7 * ng,)),
                        pltpu.SemaphoreType.DMA((ns,))],
        name="exchange_grads")(*srcs)


def _row_tile(rows, off, target):
    for t in (512, 256, 128, 64, 32, 16, 8):
        if t <= target and rows % t == 0 and off % t == 0:
            return t
    raise ValueError((rows, off))


def adamw(recv, off, w, m, v, *, name):
    rows, W = w.shape
    tr = _row_tile(rows, off, 128)
    o = off // tr
    c1 = 1.0 - ADAM_B1 ** ADAM_STEP
    c2 = 1.0 - ADAM_B2 ** ADAM_STEP

    def body(r_ref, w_ref, m_ref, v_ref, g_ref, d_ref, mo_ref, vo_ref):
        g = r_ref[0].astype(F32)
        for j in range(1, N_DEV):
            g = g + r_ref[j].astype(F32)
        mn = ADAM_B1 * m_ref[...] + (1.0 - ADAM_B1) * g
        vn = ADAM_B2 * v_ref[...] + (1.0 - ADAM_B2) * (g * g)
        g_ref[...] = g
        mo_ref[...] = mn
        vo_ref[...] = vn
        d_ref[...] = -ADAM_LR * ((mn / c1) / (jnp.sqrt(vn / c2) + ADAM_EPS) + ADAM_WD * w_ref[...])

    spec = pl.BlockSpec((tr, W), lambda i: (i, 0))
    return pl.pallas_call(
        body, grid=(rows // tr,), in_specs=[pl.BlockSpec((N_DEV, tr, W), lambda i: (0, i + o, 0)), spec, spec, spec],
        out_specs=[spec] * 4, out_shape=[jax.ShapeDtypeStruct((rows, W), F32)] * 4,
        compiler_params=_cp("parallel"), name=name)(recv, w, m, v)


def join_columns(gathered, off, K, *, name):
    _, _, n = gathered.shape
    tr = _row_tile(K, off, 256)
    o = off // tr

    def body(i_ref, o_ref):
        for d in range(N_DEV):
            o_ref[:, d * n:(d + 1) * n] = i_ref[d]

    return pl.pallas_call(
        body, grid=(K // tr,), in_specs=[pl.BlockSpec((N_DEV, tr, n), lambda i: (0, i + o, 0))],
        out_specs=pl.BlockSpec((tr, N_DEV * n), lambda i: (i, 0)),
        out_shape=jax.ShapeDtypeStruct((K, N_DEV * n), gathered.dtype), compiler_params=_cp("parallel"),
        name=name)(gathered)


def split_columns(full, *, name):
    K, N = full.shape
    n = N // N_DEV
    tr = _row_tile(K, 0, 256)

    def body(i_ref, o_ref):
        for d in range(N_DEV):
            o_ref[d] = i_ref[:, d * n:(d + 1) * n].astype(o_ref.dtype)

    return pl.pallas_call(
        body, grid=(K // tr,), in_specs=[pl.BlockSpec((tr, N), lambda i: (i, 0))],
        out_specs=pl.BlockSpec((N_DEV, tr, n), lambda i: (0, i, 0)),
        out_shape=jax.ShapeDtypeStruct((N_DEV, K, n), BF16), compiler_params=_cp("parallel"), name=name)(full)


def _pack(arrs, dtype, row_mult):
    flat = jnp.concatenate([a.reshape(-1).astype(dtype) for a in arrs])
    rows = -(-flat.shape[0] // PACK_W)
    rows = -(-rows // row_mult) * row_mult
    return jnp.pad(flat, (0, rows * PACK_W - flat.shape[0])).reshape(rows, PACK_W)


def _pack_dev(arrs, dtype, row_mult):
    flat = jnp.concatenate([a.reshape(N_DEV, -1).astype(dtype) for a in arrs], axis=1)
    rows = -(-flat.shape[1] // PACK_W)
    rows = -(-rows // row_mult) * row_mult
    return jnp.pad(flat, ((0, 0), (0, rows * PACK_W - flat.shape[1]))).reshape(N_DEV, rows, PACK_W)


def _unpack(buf, shapes):
    lead = buf.shape[:-2]
    flat = buf.reshape(lead + (-1,))
    outs, off = [], 0
    for s in shapes:
        n = math.prod(s)
        outs.append(flat[..., off:off + n].reshape(lead + tuple(s)))
        off += n
    return outs


def _join(g, axis):
    g = jnp.moveaxis(g, 0, axis)
    return g.reshape(g.shape[:axis] + (g.shape[axis] * g.shape[axis + 1],) + g.shape[axis + 2:])


def _split(full, axis):
    s = full.shape
    g = full.reshape(s[:axis] + (N_DEV, s[axis] // N_DEV) + s[axis + 1:])
    return jnp.moveaxis(g, axis, 0)


def kernel(x, p, a_pw1_w, a_pw1_b, a_dw_w, a_dw_b, a_ln_g, a_ln_b, a_pw2_w, a_pw2_b, b_wq, kv_wk, kv_wv, b_wo, ln_mix_g, ln_mix_b, ffn_w_up, ffn_w_gate, ffn_conv_w, ffn_conv_b, ffn_w_down, ple_w_gate, ple_w_proj, ln_ffn_g, ln_ffn_b, loss_target, m_a_pw1_w, m_a_pw1_b, m_a_dw_w, m_a_dw_b, m_a_ln_g, m_a_ln_b, m_a_pw2_w, m_a_pw2_b, m_b_wq, m_kv_wk, m_kv_wv, m_b_wo, m_ln_mix_g, m_ln_mix_b, m_ffn_w_up, m_ffn_w_gate, m_ffn_conv_w, m_ffn_conv_b, m_ffn_w_down, m_ple_w_gate, m_ple_w_proj, m_ln_ffn_g, m_ln_ffn_b, v_a_pw1_w, v_a_pw1_b, v_a_dw_w, v_a_dw_b, v_a_ln_g, v_a_ln_b, v_a_pw2_w, v_a_pw2_b, v_b_wq, v_kv_wk, v_kv_wv, v_b_wo, v_ln_mix_g, v_ln_mix_b, v_ffn_w_up, v_ffn_w_gate, v_ffn_conv_w, v_ffn_conv_b, v_ffn_w_down, v_ple_w_gate, v_ple_w_proj, v_ln_ffn_g, v_ln_ffn_b):
    local = dict(a_pw1_w=a_pw1_w, a_pw1_b=a_pw1_b, a_dw_w=a_dw_w, a_dw_b=a_dw_b, a_ln_g=a_ln_g, a_ln_b=a_ln_b, a_pw2_w=a_pw2_w, a_pw2_b=a_pw2_b, b_wq=b_wq, kv_wk=kv_wk, kv_wv=kv_wv, b_wo=b_wo, ln_mix_g=ln_mix_g, ln_mix_b=ln_mix_b, ffn_w_up=ffn_w_up, ffn_w_gate=ffn_w_gate, ffn_conv_w=ffn_conv_w, ffn_conv_b=ffn_conv_b, ffn_w_down=ffn_w_down, ple_w_gate=ple_w_gate, ple_w_proj=ple_w_proj, ln_ffn_g=ln_ffn_g, ln_ffn_b=ln_ffn_b)
    mom1 = dict(a_pw1_w=m_a_pw1_w, a_pw1_b=m_a_pw1_b, a_dw_w=m_a_dw_w, a_dw_b=m_a_dw_b, a_ln_g=m_a_ln_g, a_ln_b=m_a_ln_b, a_pw2_w=m_a_pw2_w, a_pw2_b=m_a_pw2_b, b_wq=m_b_wq, kv_wk=m_kv_wk, kv_wv=m_kv_wv, b_wo=m_b_wo, ln_mix_g=m_ln_mix_g, ln_mix_b=m_ln_mix_b, ffn_w_up=m_ffn_w_up, ffn_w_gate=m_ffn_w_gate, ffn_conv_w=m_ffn_conv_w, ffn_conv_b=m_ffn_conv_b, ffn_w_down=m_ffn_w_down, ple_w_gate=m_ple_w_gate, ple_w_proj=m_ple_w_proj, ln_ffn_g=m_ln_ffn_g, ln_ffn_b=m_ln_ffn_b)
    mom2 = dict(a_pw1_w=v_a_pw1_w, a_pw1_b=v_a_pw1_b, a_dw_w=v_a_dw_w, a_dw_b=v_a_dw_b, a_ln_g=v_a_ln_g, a_ln_b=v_a_ln_b, a_pw2_w=v_a_pw2_w, a_pw2_b=v_a_pw2_b, b_wq=v_b_wq, kv_wk=v_kv_wk, kv_wv=v_kv_wv, b_wo=v_b_wo, ln_mix_g=v_ln_mix_g, ln_mix_b=v_ln_mix_b, ffn_w_up=v_ffn_w_up, ffn_w_gate=v_ffn_w_gate, ffn_conv_w=v_ffn_conv_w, ffn_conv_b=v_ffn_conv_b, ffn_w_down=v_ffn_w_down, ple_w_gate=v_ple_w_gate, ple_w_proj=v_ple_w_proj, ln_ffn_g=v_ln_ffn_g, ln_ffn_b=v_ln_ffn_b)
    small_names = [n for n, _ in SMALL]
    small_shapes = [local[n].shape for n in small_names]
    repl_shapes = [local[n].shape for n in REPL]

    widths = sorted({local[n].shape[-1] for n, _ in BIG}, reverse=True)
    groups = {w: [n for n, _ in BIG if local[n].shape[-1] == w] for w in widths}
    offset, rows_of = {}, {}
    for w, names in groups.items():
        off = 0
        for n in names:
            offset[n], rows_of[n] = off, math.prod(local[n].shape[:-1])
            off += rows_of[n]
    sends = [jnp.concatenate([local[n].reshape(-1, w).astype(BF16) for n in names]) for w, names in groups.items()]
    gathered = all_gather(sends + [_pack([local[n] for n in small_names], F32, SUBLANES)])
    gath = dict(zip(widths, gathered[:-1]))
    W = {}
    for n, ax in BIG:
        w = local[n].shape[-1]
        nl = local[n].shape[0] if local[n].ndim == 3 else 1
        per = rows_of[n] // nl
        if ax == local[n].ndim - 1:
            W[n] = [join_columns(gath[w], offset[n] + l * per, per, name=f"join_{n}_{l}") for l in range(nl)]
        else:
            W[n] = [gath[w][:, offset[n] + l * per:offset[n] + (l + 1) * per].reshape(N_DEV * per, w) for l in range(nl)]
    for n in ("kv_wk", "kv_wv"):
        W[n] = W[n][0]
    W.update({n: _join(g, ax) for (n, ax), g in zip(SMALL, _unpack(gathered[-1], small_shapes))})
    W.update({n: local[n] for n in REPL})

    xs = x[0]
    S, D = xs.shape
    x_in, r1s, x1s, r2s, us, gps, gs, hhs, pgls, pps = [], [], [], [], [], [], [], [], [], []
    h1s, h2s, h3s, h5s, qs, os_, tots = {}, {}, {}, {}, {}, {}, {}
    kk = vv = None
    for i in range(DEPTH):
        x_in.append(xs)
        if i < N_A:
            h1 = mm(xs, W["a_pw1_w"][i], "nn", bias=W["a_pw1_b"][i][None], name=f"pw1_{i}")
            h2 = glu_fwd(h1, name=f"glu_{i}")
            h3, h5 = conv_ln_silu_fwd(h2, W["a_dw_w"][i], W["a_dw_b"][i][None], W["a_ln_g"][i][None],
                                      W["a_ln_b"][i][None], name=f"dwconv_{i}")
            mix = mm(h5, W["a_pw2_w"][i], "nn", bias=W["a_pw2_b"][i][None], name=f"pw2_{i}")
            h1s[i], h2s[i], h3s[i], h5s[i] = h1, h2, h3, h5
        else:
            j = i - N_A
            if kk is None:
                kk = mm(xs, W["kv_wk"], "nn", out_dtype=BF16, name="proj_k")
                vv = mm(xs, W["kv_wv"], "nn", out_dtype=BF16, name="proj_v")
            q = mm(xs, W["b_wq"][j], "nn", out_dtype=BF16, name=f"proj_q_{i}")
            o, tot = attn_fwd(q, kk, vv, name=f"attn_{i}")
            mix = mm(o, W["b_wo"][j], "nn", name=f"proj_o_{i}")
            qs[i], os_[i], tots[i] = q, o, tot
        r1, x1 = res_ln(xs, mix, W["ln_mix_g"][i][None], W["ln_mix_b"][i][None], name=f"ln_mix_{i}")
        u = mm(x1, W["ffn_w_up"][i], "nn", name=f"ffn_up_{i}")
        gp = mm(x1, W["ffn_w_gate"][i], "nn", name=f"ffn_gate_{i}")
        g, hh = conv_act_fwd(gp, u, W["ffn_conv_w"][i], W["ffn_conv_b"][i][None], name=f"ffn_conv_{i}")
        f = mm(hh, W["ffn_w_down"][i], "nn", name=f"ffn_down_{i}")
        pgl = mm(x1, W["ple_w_gate"][i], "nn", name=f"ple_gate_{i}")
        pp = mm(p[i, 0], W["ple_w_proj"][i], "nn", name=f"ple_proj_{i}")
        r2, xs = res_ln(x1, f, W["ln_ffn_g"][i][None], W["ln_ffn_b"][i][None], ple=(pgl, pp), name=f"ln_ffn_{i}")
        for lst, val in ((r1s, r1), (x1s, x1), (r2s, r2), (us, u), (gps, gp), (gs, g), (hhs, hh), (pgls, pgl), (pps, pp)):
            lst.append(val)

    dx, loss_part = loss_grad(xs, loss_target[0], name="loss")
    G = {n: [None] * local[n].shape[0] for n in WEIGHTS if n not in ("kv_wk", "kv_wv")}
    dk = dv = None
    for i in reversed(range(DEPTH)):
        x1 = x1s[i]
        dr2, G["ln_ffn_g"][i], G["ln_ffn_b"][i], _ = ln_bwd(r2s[i], W["ln_ffn_g"][i][None], dx, name=f"ln_ffn_bwd_{i}")
        dhh = mm(dr2, W["ffn_w_down"][i], "nt", name=f"ffn_down_dx_{i}")
        G["ffn_w_down"][i] = mm(hhs[i], dr2, "tn", out_dtype=BF16, name=f"ffn_down_dw_{i}")
        dpp, dpgl = ple_bwd(dr2, pgls[i], pps[i], name=f"ple_bwd_{i}")
        G["ple_w_proj"][i] = mm(p[i, 0], dpp, "tn", out_dtype=BF16, name=f"ple_proj_dw_{i}")
        G["ple_w_gate"][i] = mm(x1, dpgl, "tn", out_dtype=BF16, name=f"ple_gate_dw_{i}")
        du, dg = ffn_act_bwd(dhh, us[i], gs[i], name=f"ffn_act_bwd_{i}")
        dgp = conv_bwd_x(dg, W["ffn_conv_w"][i], out_dtype=BF16, name=f"ffn_conv_dx_{i}")
        G["ffn_conv_w"][i], G["ffn_conv_b"][i] = conv_bwd_w(gps[i], dg, FFN_CONV_W, name=f"ffn_conv_dw_{i}")
        G["ffn_w_up"][i] = mm(x1, du, "tn", out_dtype=BF16, name=f"ffn_up_dw_{i}")
        G["ffn_w_gate"][i] = mm(x1, dgp, "tn", out_dtype=BF16, name=f"ffn_gate_dw_{i}")
        dx1 = mm(du, W["ffn_w_up"][i], "nt", add=dr2, add_scale=DN_ALPHA, name=f"ffn_up_dx_{i}")
        dx1 = mm(dgp, W["ffn_w_gate"][i], "nt", add=dx1, name=f"ffn_gate_dx_{i}")
        dx1 = mm(dpgl, W["ple_w_gate"][i], "nt", add=dx1, name=f"ple_gate_dx_{i}")
        dr1, G["ln_mix_g"][i], G["ln_mix_b"][i], dr1_sum = ln_bwd(r1s[i], W["ln_mix_g"][i][None], dx1, name=f"ln_mix_bwd_{i}")
        if i < N_A:
            G["a_pw2_w"][i] = mm(h5s[i], dr1, "tn", out_dtype=BF16, name=f"pw2_dw_{i}")
            G["a_pw2_b"][i] = dr1_sum
            dh5 = mm(dr1, W["a_pw2_w"][i], "nt", name=f"pw2_dx_{i}")
            dh3, G["a_ln_g"][i], G["a_ln_b"][i] = ln_silu_bwd(h3s[i], W["a_ln_g"][i][None], W["a_ln_b"][i][None], dh5,
                                                             name=f"dwconv_ln_bwd_{i}")
            dh2 = conv_bwd_x(dh3, W["a_dw_w"][i], out_dtype=F32, name=f"dwconv_dx_{i}")
            G["a_dw_w"][i], G["a_dw_b"][i] = conv_bwd_w(h2s[i], dh3, CONV_W, name=f"dwconv_dw_{i}")
            dh1, G["a_pw1_b"][i] = glu_bwd(h1s[i], dh2, name=f"glu_bwd_{i}")
            G["a_pw1_w"][i] = mm(x_in[i], dh1, "tn", out_dtype=BF16, name=f"pw1_dw_{i}")
            dx = mm(dh1, W["a_pw1_w"][i], "nt", add=dr1, add_scale=DN_ALPHA, name=f"pw1_dx_{i}")
        else:
            j = i - N_A
            G["b_wo"][j] = mm(os_[i], dr1, "tn", out_dtype=BF16, name=f"proj_o_dw_{i}")
            do = mm(dr1, W["b_wo"][j], "nt", name=f"proj_o_dx_{i}")
            dq, dk, dv = attn_bwd(qs[i], kk, vv, tots[i], do, dk, dv, name=f"attn_bwd_{i}")
            G["b_wq"][j] = mm(x_in[i], dq, "tn", out_dtype=BF16, name=f"proj_q_dw_{i}")
            dx = mm(dq, W["b_wq"][j], "nt", add=dr1, add_scale=DN_ALPHA, name=f"proj_q_dx_{i}")
            if j == 0:
                G["kv_wk"] = mm(x_in[i], dk, "tn", out_dtype=BF16, name="proj_k_dw")
                G["kv_wv"] = mm(x_in[i], dv, "tn", out_dtype=BF16, name="proj_v_dw")
                dx = mm(dk, W["kv_wk"], "nt", add=dx, name="proj_k_dx")
                dx = mm(dv, W["kv_wv"], "nt", add=dx, name="proj_v_dx")
    grad_x = dx[None]
    shard_axis = dict(BIG + SMALL)
    for n in small_names + list(REPL):
        full = list(local[n].shape)
        if n in shard_axis:
            full[shard_axis[n]] *= N_DEV
        G[n] = jnp.stack(G[n]).reshape(full)

    n_small = sum(math.prod(s) for s in small_shapes)
    n_repl = sum(math.prod(s) for s in repl_shapes)
    repl_flat = jnp.concatenate([G[n].reshape(-1) for n in REPL] + [loss_part.reshape(-1)[:1]])
    send_small = _pack_dev([_split(G[n], ax) for n, ax in SMALL] + [jnp.broadcast_to(repl_flat, (N_DEV, n_repl + 1))],
                           F32, SUBLANES)
    ex_groups = []
    for w, names in groups.items():
        lst = []
        for n in names:
            layers = G[n] if isinstance(G[n], list) else [G[n]]
            per = rows_of[n] // len(layers)
            for l, g in enumerate(layers):
                if shard_axis[n] == local[n].ndim - 1:
                    src = split_columns(g, name=f"split_{n}_{l}")
                else:
                    src = g.reshape(N_DEV, per, w)
                lst.append((src, offset[n] + l * per))
        ex_groups.append((sum(rows_of[n] for n in names), w, BF16, lst))
    ex_groups.append((send_small.shape[1], PACK_W, F32, [(send_small, 0)]))
    recvs = exchange_grads(ex_groups)
    recv = dict(zip(widths, recvs[:-1]))
    recv_small = recvs[-1]

    out = {}
    for n, _ in BIG:
        w = local[n].shape[-1]
        res = adamw(recv[w], offset[n], local[n].reshape(-1, w), mom1[n].reshape(-1, w), mom2[n].reshape(-1, w),
                    name=f"adamw_{n}")
        out[n] = [r.reshape(local[n].shape) for r in res]

    def state(d):
        small = _pack([d[n] for n in small_names] + [d[n] for n in REPL], F32, SUBLANES)
        return jnp.pad(small, ((0, recv_small.shape[1] - small.shape[0]), (0, 0)))

    out_small = adamw(recv_small, 0, state(local), state(mom1), state(mom2), name="adamw_vectors")
    loss = out_small[0].reshape(-1)[n_small + n_repl]
    vecs = [dict(zip(small_names + list(REPL), _unpack(o, small_shapes + repl_shapes))) for o in out_small]
    per_kind = [[out[n][kind] if n in out else vecs[kind][n] for n in WEIGHTS] for kind in range(4)]
    grads, deltas, new_m, new_v = per_kind
    return (loss, grad_x, *grads, *deltas, *new_m, *new_v)
```

```python
import functools
import math

import jax
import jax.numpy as jnp
from jax import lax
from jax.experimental import pallas as pl
from jax.experimental.pallas import tpu as pltpu

F32 = jnp.float32
BF16 = jnp.bfloat16
MESH = pl.DeviceIdType.MESH

N_DEV = 8
DEPTH = 4
N_A = 2
HEAD_DIM = 64
Q_BLOCK = 128
CONV_W = 31
FFN_CONV_W = 3
LN_EPS = 1e-5
DN_ALPHA = (2.0 * DEPTH) ** 0.25
ADAM_LR = 0.001
ADAM_B1 = 0.9
ADAM_B2 = 0.999
ADAM_EPS = 1e-08
ADAM_WD = 0.01
ADAM_STEP = 10

LANES = 128
SUBLANES = 8
PACK_W = 1024
VMEM_LIMIT = 56 * 1024 * 1024

BIG = (("a_pw1_w", 2), ("a_pw2_w", 1), ("b_wq", 1), ("kv_wk", 0), ("kv_wv", 0), ("b_wo", 1),
       ("ffn_w_up", 2), ("ffn_w_gate", 2), ("ffn_w_down", 1), ("ple_w_gate", 1), ("ple_w_proj", 2))
SMALL = (("a_pw1_b", 1), ("a_dw_w", 2), ("a_dw_b", 1), ("a_ln_g", 1), ("a_ln_b", 1), ("a_pw2_b", 1),
         ("ffn_conv_w", 2))
REPL = ("ln_mix_g", "ln_mix_b", "ffn_conv_b", "ln_ffn_g", "ln_ffn_b")
WEIGHTS = ("a_pw1_w", "a_pw1_b", "a_dw_w", "a_dw_b", "a_ln_g", "a_ln_b", "a_pw2_w", "a_pw2_b", "b_wq", "kv_wk",
           "kv_wv", "b_wo", "ln_mix_g", "ln_mix_b", "ffn_w_up", "ffn_w_gate", "ffn_conv_w", "ffn_conv_b",
           "ffn_w_down", "ple_w_gate", "ple_w_proj", "ln_ffn_g", "ln_ffn_b")


def _cp(*sem):
    return pltpu.CompilerParams(dimension_semantics=sem, vmem_limit_bytes=VMEM_LIMIT)


def _pick(dim, target, align=LANES):
    if dim <= target:
        return dim
    t = (target // align) * align
    while t >= align:
        if dim % t == 0:
            return t
        t -= align
    return dim


_DOT_DIMS = {"nn": (((1,), (0,)), ((), ())), "nt": (((1,), (1,)), ((), ())), "tn": (((0,), (0,)), ((), ()))}


def mm(a, b, mode, *, bias=None, add=None, add_scale=1.0, out_dtype=F32, name):
    if mode == "tn":
        K, M = a.shape
    else:
        M, K = a.shape
    N = b.shape[0] if mode == "nt" else b.shape[1]
    tm, tn, tk = _pick(M, 1536 if mode == "tn" else 512), _pick(N, 1536), _pick(K, 1536)
    nk = K // tk
    dims = _DOT_DIMS[mode]

    def body(*refs):
        a_ref, b_ref = refs[0], refs[1]
        pos = 2
        bias_ref = add_ref = None
        if bias is not None:
            bias_ref = refs[pos]
            pos += 1
        if add is not None:
            add_ref = refs[pos]
            pos += 1
        o_ref, acc_ref = refs[pos], refs[pos + 1]
        k = pl.program_id(2)

        @pl.when(k == 0)
        def _():
            acc_ref[...] = jnp.zeros_like(acc_ref)

        acc_ref[...] += lax.dot_general(a_ref[...].astype(BF16), b_ref[...].astype(BF16), dims,
                                        preferred_element_type=F32)

        @pl.when(k == nk - 1)
        def _():
            r = acc_ref[...]
            if bias_ref is not None:
                r = r + bias_ref[...]
            if add_ref is not None:
                r = r + add_scale * add_ref[...].astype(F32)
            o_ref[...] = r.astype(o_ref.dtype)

    a_spec = pl.BlockSpec((tk, tm), lambda j, i, k: (k, i)) if mode == "tn" else pl.BlockSpec((tm, tk), lambda j, i, k: (i, k))
    b_spec = pl.BlockSpec((tn, tk), lambda j, i, k: (j, k)) if mode == "nt" else pl.BlockSpec((tk, tn), lambda j, i, k: (k, j))
    in_specs, args = [a_spec, b_spec], [a, b]
    if bias is not None:
        in_specs.append(pl.BlockSpec((1, tn), lambda j, i, k: (0, j)))
        args.append(bias)
    if add is not None:
        in_specs.append(pl.BlockSpec((tm, tn), lambda j, i, k: (i, j)))
        args.append(add)
    return pl.pallas_call(
        body, grid=(N // tn, M // tm, nk), in_specs=in_specs,
        out_specs=pl.BlockSpec((tm, tn), lambda j, i, k: (i, j)),
        out_shape=jax.ShapeDtypeStruct((M, N), out_dtype),
        scratch_shapes=[pltpu.VMEM((tm, tn), F32)],
        compiler_params=_cp("parallel", "parallel", "arbitrary"), name=name)(*args)


def _rows(body, *, n_rows, tm, row_ins, full_ins=(), row_outs=(), acc_outs=(), scratch=(), reverse=False, name):
    n = n_rows // tm

    def rmap(i):
        return (n - 1 - i, 0) if reverse else (i, 0)

    in_specs = [pl.BlockSpec((tm, a.shape[1]), rmap) for a in row_ins]
    in_specs += [pl.BlockSpec(a.shape, lambda i, nd=a.ndim: (0,) * nd) for a in full_ins]
    out_shape = [jax.ShapeDtypeStruct((n_rows, w), dt) for (w, dt) in row_outs]
    out_shape += [jax.ShapeDtypeStruct(s, dt) for (s, dt) in acc_outs]
    out_specs = [pl.BlockSpec((tm, w), rmap) for (w, dt) in row_outs]
    out_specs += [pl.BlockSpec(s, lambda i, nd=len(s): (0,) * nd) for (s, dt) in acc_outs]
    return pl.pallas_call(
        functools.partial(body, n), grid=(n,), in_specs=in_specs, out_specs=out_specs, out_shape=out_shape,
        scratch_shapes=list(scratch), compiler_params=_cp("arbitrary"), name=name)(*row_ins, *full_ins)


def _sigmoid(x):
    return 1.0 / (1.0 + jnp.exp(-x))


def _ln_hat(r):
    mu = jnp.mean(r, axis=-1, keepdims=True)
    xc = r - mu
    var = jnp.mean(xc * xc, axis=-1, keepdims=True)
    rstd = lax.rsqrt(var + LN_EPS)
    return xc * rstd, rstd


def _ln_back(xhat, rstd, g, dy):
    dxh = dy * g
    m1 = jnp.mean(dxh, axis=-1, keepdims=True)
    m2 = jnp.mean(dxh * xhat, axis=-1, keepdims=True)
    return rstd * (dxh - m1 - xhat * m2)


def _colsum(x):
    return jnp.sum(x, axis=0, keepdims=True)


def _acc(i, ref, val):
    @pl.when(i == 0)
    def _():
        ref[...] = val

    @pl.when(i > 0)
    def _():
        ref[...] += val


def res_ln(x, mix, g, b, *, ple=None, name):
    S, D = x.shape

    def body(n, *refs):
        if ple is None:
            x_ref, m_ref, g_ref, b_ref, r_ref, y_ref = refs
            r = DN_ALPHA * x_ref[...] + m_ref[...]
        else:
            x_ref, m_ref, pgl_ref, pp_ref, g_ref, b_ref, r_ref, y_ref = refs
            r = DN_ALPHA * x_ref[...] + m_ref[...] + _sigmoid(pgl_ref[...]) * pp_ref[...]
        xhat, _ = _ln_hat(r)
        r_ref[...] = r
        y_ref[...] = xhat * g_ref[...] + b_ref[...]

    row_ins = [x, mix] + ([] if ple is None else list(ple))
    return _rows(body, n_rows=S, tm=_pick(S, 256, SUBLANES), row_ins=row_ins, full_ins=[g, b],
                 row_outs=[(D, F32), (D, F32)], name=name)


def ln_bwd(r, g, dy, *, name):
    S, D = r.shape

    def body(n, r_ref, dy_ref, g_ref, dr_ref, dg_ref, db_ref, ds_ref):
        i = pl.program_id(0)
        xhat, rstd = _ln_hat(r_ref[...])
        dy_v = dy_ref[...]
        dr = _ln_back(xhat, rstd, g_ref[...], dy_v)
        dr_ref[...] = dr
        _acc(i, dg_ref, _colsum(dy_v * xhat))
        _acc(i, db_ref, _colsum(dy_v))
        _acc(i, ds_ref, _colsum(dr))

    return _rows(body, n_rows=S, tm=_pick(S, 256, SUBLANES), row_ins=[r, dy], full_ins=[g],
                 row_outs=[(D, F32)], acc_outs=[((1, D), F32)] * 3, name=name)


def glu_fwd(h1, *, name):
    S, D2 = h1.shape
    D = D2 // 2

    def body(n, h_ref, o_ref):
        o_ref[...] = h_ref[:, :D] * _sigmoid(h_ref[:, D:])

    return _rows(body, n_rows=S, tm=_pick(S, 256, SUBLANES), row_ins=[h1], row_outs=[(D, F32)], name=name)[0]


def glu_bwd(h1, dh2, *, name):
    S, D2 = h1.shape
    D = D2 // 2

    def body(n, h_ref, d_ref, o_ref, s_ref):
        i = pl.program_id(0)
        a, sg, d = h_ref[:, :D], _sigmoid(h_ref[:, D:]), d_ref[...]
        da = d * sg
        dg = d * a * sg * (1.0 - sg)
        o_ref[:, :D] = da.astype(o_ref.dtype)
        o_ref[:, D:] = dg.astype(o_ref.dtype)
        _acc(i, s_ref, jnp.concatenate([_colsum(da), _colsum(dg)], axis=1))

    return _rows(body, n_rows=S, tm=_pick(S, 256, SUBLANES), row_ins=[h1, dh2], row_outs=[(D2, BF16)],
                 acc_outs=[((1, D2), F32)], name=name)


CONV_ROWS = 32
CONV_LANES = 256


def _halo(k):
    return -(-(k - 1) // SUBLANES) * SUBLANES


def _conv_taps(buf_ref, w_ref, offs, tm, width, emit):
    rows = min(CONV_ROWS, tm)
    for lc in range(0, width, CONV_LANES):
        lw = min(CONV_LANES, width - lc)
        for rc in range(0, tm, rows):
            acc = None
            for k, off in enumerate(offs):
                t = buf_ref[pl.ds(off + rc, rows), pl.ds(lc, lw)] * w_ref[pl.ds(k, 1), pl.ds(lc, lw)]
                acc = t if acc is None else acc + t
            emit(rc, lc, lw, rows, acc)


def _fill_causal(i, buf_ref, x_ref, halo, tm):
    @pl.when(i == 0)
    def _():
        buf_ref[pl.ds(0, halo), :] = jnp.zeros((halo, buf_ref.shape[1]), F32)

    @pl.when(i > 0)
    def _():
        buf_ref[pl.ds(0, halo), :] = buf_ref[pl.ds(tm, halo), :]

    buf_ref[pl.ds(halo, tm), :] = x_ref[...].astype(F32)


def conv_ln_silu_fwd(x, w, b, g, beta, *, name):
    S, C = x.shape
    K = w.shape[0]
    halo = _halo(K)
    tm = _pick(S, 256, SUBLANES)
    offs = [halo - (K - 1) + k for k in range(K)]

    def body(n, x_ref, w_ref, b_ref, g_ref, beta_ref, h3_ref, h5_ref, buf_ref):
        i = pl.program_id(0)
        _fill_causal(i, buf_ref, x_ref, halo, tm)

        def emit(rc, lc, lw, rows, acc):
            h3_ref[pl.ds(rc, rows), pl.ds(lc, lw)] = acc + b_ref[:, pl.ds(lc, lw)]

        _conv_taps(buf_ref, w_ref, offs, tm, C, emit)
        xhat, _ = _ln_hat(h3_ref[...])
        h4 = xhat * g_ref[...] + beta_ref[...]
        h5_ref[...] = (h4 * _sigmoid(h4)).astype(h5_ref.dtype)

    return _rows(body, n_rows=S, tm=tm, row_ins=[x], full_ins=[w, b, g, beta], row_outs=[(C, F32), (C, BF16)],
                 scratch=[pltpu.VMEM((tm + halo, C), F32)], name=name)


def conv_act_fwd(gp, u, w, b, *, name):
    S, C = gp.shape
    K = w.shape[0]
    halo = _halo(K)
    tm = _pick(S, 256, SUBLANES)
    offs = [halo - (K - 1) + k for k in range(K)]

    def body(n, x_ref, u_ref, w_ref, b_ref, g_ref, hh_ref, buf_ref):
        i = pl.program_id(0)
        _fill_causal(i, buf_ref, x_ref, halo, tm)

        def emit(rc, lc, lw, rows, acc):
            gv = acc + b_ref[:, pl.ds(lc, lw)]
            g_ref[pl.ds(rc, rows), pl.ds(lc, lw)] = gv.astype(g_ref.dtype)
            hh_ref[pl.ds(rc, rows), pl.ds(lc, lw)] = (gv * _sigmoid(gv) * u_ref[pl.ds(rc, rows), pl.ds(lc, lw)].astype(F32)).astype(hh_ref.dtype)

        _conv_taps(buf_ref, w_ref, offs, tm, C, emit)

    return _rows(body, n_rows=S, tm=tm, row_ins=[gp, u], full_ins=[w, b], row_outs=[(C, BF16), (C, BF16)],
                 scratch=[pltpu.VMEM((tm + halo, C), F32)], name=name)


def conv_bwd_x(dy, w, *, out_dtype, name):
    S, C = dy.shape
    K = w.shape[0]
    halo = _halo(K)
    tm = _pick(S, 256, SUBLANES)
    offs = [K - 1 - k for k in range(K)]

    def body(n, dy_ref, w_ref, dx_ref, buf_ref):
        i = pl.program_id(0)

        @pl.when(i == 0)
        def _():
            buf_ref[pl.ds(tm, halo), :] = jnp.zeros((halo, C), F32)

        @pl.when(i > 0)
        def _():
            buf_ref[pl.ds(tm, halo), :] = buf_ref[pl.ds(0, halo), :]

        buf_ref[pl.ds(0, tm), :] = dy_ref[...].astype(F32)

        def emit(rc, lc, lw, rows, acc):
            dx_ref[pl.ds(rc, rows), pl.ds(lc, lw)] = acc.astype(dx_ref.dtype)

        _conv_taps(buf_ref, w_ref, offs, tm, C, emit)

    return _rows(body, n_rows=S, tm=tm, row_ins=[dy], full_ins=[w], row_outs=[(C, out_dtype)],
                 scratch=[pltpu.VMEM((tm + halo, C), F32)], reverse=True, name=name)[0]


def conv_bwd_w(x, dy, K, *, name):
    S, C = x.shape
    halo = _halo(K)
    tm = _pick(S, 256, SUBLANES)
    offs = [halo - (K - 1) + k for k in range(K)]
    rows = min(CONV_ROWS, tm)

    def body(n, x_ref, dy_ref, dw_ref, db_ref, buf_ref, acc_ref):
        i = pl.program_id(0)
        _fill_causal(i, buf_ref, x_ref, halo, tm)

        @pl.when(i == 0)
        def _():
            acc_ref[...] = jnp.zeros_like(acc_ref)

        for lc in range(0, C, CONV_LANES):
            lw = min(CONV_LANES, C - lc)
            for k, off in enumerate(offs):
                s = None
                for rc in range(0, tm, rows):
                    t = dy_ref[pl.ds(rc, rows), pl.ds(lc, lw)].astype(F32) * buf_ref[pl.ds(off + rc, rows), pl.ds(lc, lw)]
                    s = t if s is None else s + t
                s8 = s[0:SUBLANES]
                for q in range(1, rows // SUBLANES):
                    s8 = s8 + s[q * SUBLANES:(q + 1) * SUBLANES]
                acc_ref[pl.ds(k * SUBLANES, SUBLANES), pl.ds(lc, lw)] += s8
        _acc(i, db_ref, _colsum(dy_ref[...].astype(F32)))

        @pl.when(i == n - 1)
        def _():
            for k in range(K):
                dw_ref[pl.ds(k, 1), :] = _colsum(acc_ref[pl.ds(k * SUBLANES, SUBLANES), :])

    return _rows(body, n_rows=S, tm=tm, row_ins=[x, dy], acc_outs=[((K, C), F32), ((1, C), F32)],
                 scratch=[pltpu.VMEM((tm + halo, C), F32), pltpu.VMEM((K * SUBLANES, C), F32)], name=name)


def ln_silu_bwd(h3, g, beta, dh5, *, name):
    S, C = h3.shape

    def body(n, h_ref, d_ref, g_ref, beta_ref, dh_ref, dg_ref, db_ref):
        i = pl.program_id(0)
        xhat, rstd = _ln_hat(h_ref[...])
        h4 = xhat * g_ref[...] + beta_ref[...]
        sg = _sigmoid(h4)
        dh4 = d_ref[...] * sg * (1.0 + h4 * (1.0 - sg))
        dh_ref[...] = _ln_back(xhat, rstd, g_ref[...], dh4)
        _acc(i, dg_ref, _colsum(dh4 * xhat))
        _acc(i, db_ref, _colsum(dh4))

    return _rows(body, n_rows=S, tm=_pick(S, 256, SUBLANES), row_ins=[h3, dh5], full_ins=[g, beta],
                 row_outs=[(C, F32)], acc_outs=[((1, C), F32)] * 2, name=name)


def ffn_act_bwd(dhh, u, g, *, name):
    S, C = u.shape

    def body(n, d_ref, u_ref, g_ref, du_ref, dg_ref):
        d, gv = d_ref[...].astype(F32), g_ref[...].astype(F32)
        sg = _sigmoid(gv)
        du_ref[...] = (d * gv * sg).astype(du_ref.dtype)
        dg_ref[...] = (d * u_ref[...].astype(F32) * sg * (1.0 + gv * (1.0 - sg))).astype(dg_ref.dtype)

    return _rows(body, n_rows=S, tm=_pick(S, 256, SUBLANES), row_ins=[dhh, u, g], row_outs=[(C, BF16), (C, BF16)],
                 name=name)


def ple_bwd(dr, pgl, pp, *, name):
    S, D = dr.shape

    def body(n, d_ref, l_ref, p_ref, dpp_ref, dpl_ref):
        d, sg = d_ref[...], _sigmoid(l_ref[...])
        dpp_ref[...] = (d * sg).astype(dpp_ref.dtype)
        dpl_ref[...] = (d * p_ref[...] * sg * (1.0 - sg)).astype(dpl_ref.dtype)

    return _rows(body, n_rows=S, tm=_pick(S, 256, SUBLANES), row_ins=[dr, pgl, pp], row_outs=[(D, BF16), (D, BF16)],
                 name=name)


def loss_grad(y, target, *, name):
    S, D = y.shape

    def body(n, y_ref, t_ref, dy_ref, l_ref):
        i = pl.program_id(0)
        e = y_ref[...] - t_ref[...]
        dy_ref[...] = e * (1.0 / D)
        s = jnp.sum(_colsum(e * e), axis=1, keepdims=True) * (0.5 / D)
        _acc(i, l_ref, jnp.broadcast_to(s, (1, LANES)))

    return _rows(body, n_rows=S, tm=_pick(S, 256, SUBLANES), row_ins=[y, target], row_outs=[(D, F32)],
                 acc_outs=[((1, LANES), F32)], name=name)


def _key_step(S):
    return min(1024, S // 2)


def _attn_consts():
    lane = lax.broadcasted_iota(jnp.int32, (1, LANES), 1)
    heads = (lane < HEAD_DIM, lane >= HEAD_DIM)
    row = lax.broadcasted_iota(jnp.int32, (Q_BLOCK, Q_BLOCK), 0)
    col = lax.broadcasted_iota(jnp.int32, (Q_BLOCK, Q_BLOCK), 1)
    causal = jnp.concatenate([col < row] * 2, axis=0)
    return heads, row, col, causal


def _tri(cond):
    return jnp.where(cond, 1.0, 0.0).astype(BF16)


def _keysum2(x, tri):
    hi = x.astype(BF16)
    lo = (x - hi.astype(F32)).astype(BF16)
    return jnp.dot(jnp.concatenate([hi, lo], axis=1), jnp.concatenate([tri, tri], axis=0),
                   preferred_element_type=F32)


def _stack_heads(x, heads):
    return jnp.concatenate([jnp.where(m, x, jnp.zeros_like(x)) for m in heads], axis=0)


def _log1m_beta(z):
    return -(jnp.maximum(z, 0.0) + jnp.log(1.0 + jnp.exp(-jnp.abs(z))))


def attn_fwd(q, k, v, *, name):
    S, D = q.shape
    nb = S // Q_BLOCK
    tk = _key_step(S)
    nkb = tk // Q_BLOCK
    scale = 1.0 / math.sqrt(HEAD_DIM)

    def body(q_ref, k_ref, v_ref, o_ref, tot_ref, vm_ref):
        heads, row, col, causal = _attn_consts()
        above = _tri(row > col)
        for h in range(2):
            vm_ref[h] = jnp.where(heads[h], v_ref[...], jnp.zeros_like(v_ref[...]))

        def step(sb, carry, qq, nblk, diag):
            acc, cl = carry
            c0 = pl.multiple_of(sb * tk, tk)
            z = lax.dot_general(qq, k_ref[pl.ds(c0, nblk * Q_BLOCK), :], _DOT_DIMS["nt"], preferred_element_type=F32)
            zl, es, rs = [], [], []
            for jb in range(nblk):
                zb = z[:, jb * Q_BLOCK:(jb + 1) * Q_BLOCK]
                lr = _log1m_beta(zb)
                l = jnp.where(causal, lr, 0.0) if diag and jb == nblk - 1 else lr
                zl.append(zb + lr)
                es.append(_keysum2(l, above))
                rs.append(jnp.sum(l, axis=1, keepdims=True))
            a = [None] * nblk
            for jb in reversed(range(nblk)):
                ab = jnp.exp(zl[jb] + es[jb] + cl)
                if diag and jb == nblk - 1:
                    ab = jnp.where(causal, ab, 0.0)
                a[jb] = ab.astype(BF16)
                cl = cl + rs[jb]
            a = jnp.concatenate(a, axis=1)
            for h in range(2):
                acc = acc + jnp.dot(a[h * Q_BLOCK:(h + 1) * Q_BLOCK], vm_ref[h, pl.ds(c0, nblk * Q_BLOCK), :],
                                    preferred_element_type=F32)
            return acc, cl

        def qblock(i, _):
            r0 = pl.multiple_of(i * Q_BLOCK, Q_BLOCK)
            qq = _stack_heads(q_ref[pl.ds(r0, Q_BLOCK), :] * scale, heads)
            last = i // nkb
            carry = (jnp.zeros((Q_BLOCK, LANES), F32), jnp.zeros((2 * Q_BLOCK, 1), F32))
            carry = lax.switch(i % nkb, [functools.partial(step, last, qq=qq, nblk=m + 1, diag=True) for m in range(nkb)],
                               carry)
            acc, cl = lax.fori_loop(0, last, lambda jj, c: step(last - 1 - jj, c, qq, nkb, False), carry)
            o_ref[pl.ds(r0, Q_BLOCK), :] = acc
            tot_ref[pl.ds(r0, Q_BLOCK), :] = jnp.where(heads[0], cl[:Q_BLOCK], cl[Q_BLOCK:])
            return 0

        lax.fori_loop(0, nb, qblock, 0)

    spec = pl.BlockSpec((S, LANES), lambda h: (0, h))
    return pl.pallas_call(body, grid=(D // LANES,), in_specs=[spec] * 3, out_specs=[spec] * 2,
                          out_shape=[jax.ShapeDtypeStruct((S, D), F32)] * 2,
                          scratch_shapes=[pltpu.VMEM((2, S, LANES), BF16)], compiler_params=_cp("parallel"),
                          name=name)(q, k, v)


def attn_bwd(q, k, v, tot, do, dk0, dv0, *, name):
    S, D = q.shape
    nb = S // Q_BLOCK
    tk = _key_step(S)
    nkb = tk // Q_BLOCK
    scale = 1.0 / math.sqrt(HEAD_DIM)
    has_init = dk0 is not None

    def body(*refs):
        if has_init:
            q_ref, k_ref, v_ref, tot_ref, do_ref, dk0_ref, dv0_ref, dq_ref, dk_ref, dv_ref, km_ref = refs
            dk_ref[...] = dk0_ref[...]
            dv_ref[...] = dv0_ref[...]
        else:
            q_ref, k_ref, v_ref, tot_ref, do_ref, dq_ref, dk_ref, dv_ref, km_ref = refs
            dk_ref[...] = jnp.zeros_like(dk_ref)
            dv_ref[...] = jnp.zeros_like(dv_ref)
        heads, row, col, causal = _attn_consts()
        upto = _tri(row <= col)
        before = _tri(row < col)
        for h in range(2):
            km_ref[h] = jnp.where(heads[h], k_ref[...], jnp.zeros_like(k_ref[...]))

        def step(sb, carry, qq, dd, totl, nblk, diag):
            dq, pl_, pg = carry
            c0 = pl.multiple_of(sb * tk, tk)
            keys = pl.ds(c0, nblk * Q_BLOCK)
            z = lax.dot_general(qq, k_ref[keys, :], _DOT_DIMS["nt"], preferred_element_type=F32)
            da = lax.dot_general(dd, v_ref[keys, :], _DOT_DIMS["nt"], preferred_element_type=F32)
            blocks = range(nblk)
            masked = [diag and jb == nblk - 1 for jb in blocks]
            zb = [z[:, jb * Q_BLOCK:(jb + 1) * Q_BLOCK] for jb in blocks]
            lr = [_log1m_beta(zb[jb]) for jb in blocks]
            l = [jnp.where(causal, lr[jb], 0.0) if masked[jb] else lr[jb] for jb in blocks]
            lsum = [_keysum2(l[jb], upto) for jb in blocks]
            lrow = [jnp.sum(l[jb], axis=1, keepdims=True) for jb in blocks]
            a, g = [None] * nblk, [None] * nblk
            for jb in blocks:
                ab = jnp.exp(zb[jb] + lr[jb] + (totl - pl_ - lsum[jb]))
                if masked[jb]:
                    ab = jnp.where(causal, ab, 0.0)
                g[jb] = ab * da[:, jb * Q_BLOCK:(jb + 1) * Q_BLOCK]
                a[jb] = ab.astype(BF16)
                pl_ = pl_ + lrow[jb]
            gsum = [jnp.dot(g[jb].astype(BF16), before, preferred_element_type=F32) for jb in blocks]
            grow = [jnp.sum(g[jb], axis=1, keepdims=True) for jb in blocks]
            dz = [None] * nblk
            for jb in blocks:
                dzb = g[jb] * jnp.exp(lr[jb]) - jnp.exp(zb[jb] + lr[jb]) * (pg + gsum[jb])
                if masked[jb]:
                    dzb = jnp.where(causal, dzb, 0.0)
                dz[jb] = dzb.astype(BF16)
                pg = pg + grow[jb]
            a = jnp.concatenate(a, axis=1)
            dz = jnp.concatenate(dz, axis=1)
            for h in range(2):
                dq = dq + jnp.dot(dz[h * Q_BLOCK:(h + 1) * Q_BLOCK], km_ref[h, keys, :], preferred_element_type=F32)
            dk_ref[keys, :] += lax.dot_general(dz, qq, _DOT_DIMS["tn"], preferred_element_type=F32)
            dv_ref[keys, :] += lax.dot_general(a, dd, _DOT_DIMS["tn"], preferred_element_type=F32)
            return dq, pl_, pg

        def qblock(i, _):
            r0 = pl.multiple_of(i * Q_BLOCK, Q_BLOCK)
            qq = _stack_heads(q_ref[pl.ds(r0, Q_BLOCK), :] * scale, heads)
            dd = _stack_heads(do_ref[pl.ds(r0, Q_BLOCK), :].astype(BF16), heads)
            tot2 = tot_ref[pl.ds(r0, Q_BLOCK), :]
            totl = jnp.concatenate([tot2[:, 0:1], tot2[:, HEAD_DIM:HEAD_DIM + 1]], axis=0)
            last = i // nkb
            zc = jnp.zeros((2 * Q_BLOCK, 1), F32)
            carry = (jnp.zeros((Q_BLOCK, LANES), F32), zc, zc)
            carry = lax.fori_loop(0, last, lambda sb, c: step(sb, c, qq, dd, totl, nkb, False), carry)
            carry = lax.switch(i % nkb, [functools.partial(step, last, qq=qq, dd=dd, totl=totl, nblk=m + 1, diag=True)
                                         for m in range(nkb)], carry)
            dq_ref[pl.ds(r0, Q_BLOCK), :] = (carry[0] * scale).astype(dq_ref.dtype)
            return 0

        lax.fori_loop(0, nb, qblock, 0)

    spec = pl.BlockSpec((S, LANES), lambda h: (0, h))
    args = [q, k, v, tot, do] + ([dk0, dv0] if has_init else [])
    return pl.pallas_call(
        body, grid=(D // LANES,), in_specs=[spec] * len(args), out_specs=[spec] * 3,
        out_shape=[jax.ShapeDtypeStruct((S, D), BF16), jax.ShapeDtypeStruct((S, D), F32), jax.ShapeDtypeStruct((S, D), F32)],
        scratch_shapes=[pltpu.VMEM((2, S, LANES), BF16)], compiler_params=_cp("parallel"), name=name)(*args)


def _dev_index(px, py, pc):
    return 4 * px + 2 * py + pc


def all_gather(bufs):
    nb = len(bufs)

    def body(*refs):
        ins, outs = refs[:nb], refs[nb:2 * nb]
        send_sems, recv_sems, local_sems = refs[2 * nb:]
        x, y, c = lax.axis_index("x"), lax.axis_index("y"), lax.axis_index("c")
        me, sibling = (x, y, c), (x, y, 1 - c)
        chips = [(1 - x, y), (x, 1 - y), (1 - x, 1 - y)]

        def copy(b, k, block, to, from_input=False):
            slot = outs[b].at[_dev_index(*block)]
            return pltpu.make_async_remote_copy(
                src_ref=ins[b] if from_input else slot, dst_ref=slot,
                send_sem=send_sems.at[7 * b + k], recv_sem=recv_sems.at[7 * b + k], device_id=to, device_id_type=MESH)

        mine = [pltpu.make_async_copy(ins[b], outs[b].at[_dev_index(*me)], local_sems.at[b]) for b in range(nb)]
        for cp in mine:
            cp.start()
        first = []
        for b in range(nb):
            first.append(copy(b, 0, me, sibling, from_input=True))
            first += [copy(b, 1 + j, me, (*chip, c), from_input=True) for j, chip in enumerate(chips)]
        for cp in first:
            cp.start()
        passed = []
        for j, chip in enumerate(chips):
            for b in range(nb):
                copy(b, 1 + j, (*chip, c), me).wait_recv()
                fwd = copy(b, 4 + j, (*chip, c), sibling)
                fwd.start()
                passed.append(fwd)
        for b in range(nb):
            copy(b, 0, sibling, me).wait_recv()
            for j, chip in enumerate(chips):
                copy(b, 4 + j, (*chip, 1 - c), me).wait_recv()
        for cp in first + passed:
            cp.wait_send()
        for cp in mine:
            cp.wait()

    any_spec = pl.BlockSpec(memory_space=pl.ANY)
    return pl.pallas_call(
        body, in_specs=[any_spec] * nb, out_specs=[any_spec] * nb,
        out_shape=[jax.ShapeDtypeStruct((N_DEV,) + b.shape, b.dtype) for b in bufs],
        scratch_shapes=[pltpu.SemaphoreType.DMA((7 * nb,)), pltpu.SemaphoreType.DMA((7 * nb,)),
                        pltpu.SemaphoreType.DMA((nb,))],
        name="all_gather_weights")(*bufs)


def exchange_grads(groups):
    srcs = [s for g in groups for (s, _) in g[3]]
    ns, ng = len(srcs), len(groups)

    def body(*refs):
        src_refs, outs = refs[:ns], refs[ns:ns + ng]
        send_sems, recv_sems, local_sems = refs[ns + ng:]
        x, y, c = lax.axis_index("x"), lax.axis_index("y"), lax.axis_index("c")
        me = _dev_index(x, y, c)
        layout = []
        si = 0
        for g, (_, _, _, lst) in enumerate(groups):
            for (s, off) in lst:
                layout.append((g, src_refs[si], off, s.shape[1]))
                si += 1
        mine = [pltpu.make_async_copy(ref.at[me], outs[g].at[me, pl.ds(off, r)], local_sems.at[i])
                for i, (g, ref, off, r) in enumerate(layout)]
        for cp in mine:
            cp.start()
        slots = []
        for flip in range(1, N_DEV):
            fx, fy, fc = (flip >> 2) & 1, (flip >> 1) & 1, flip & 1
            peer = (1 - x if fx else x, 1 - y if fy else y, 1 - c if fc else c)
            pidx = _dev_index(*peer)
            for (g, ref, off, r) in layout:
                k = 7 * g + flip - 1
                pltpu.make_async_remote_copy(
                    src_ref=ref.at[pidx], dst_ref=outs[g].at[me, pl.ds(off, r)], send_sem=send_sems.at[k],
                    recv_sem=recv_sems.at[k], device_id=peer, device_id_type=MESH).start()
            for g in range(ng):
                k = 7 * g + flip - 1
                slots.append(pltpu.make_async_remote_copy(
                    src_ref=outs[g].at[pidx], dst_ref=outs[g].at[pidx], send_sem=send_sems.at[k],
                    recv_sem=recv_sems.at[k], device_id=peer, device_id_type=MESH))
        for w in slots:
            w.wait_recv()
        for w in slots:
            w.wait_send()
        for cp in mine:
            cp.wait()

    any_spec = pl.BlockSpec(memory_space=pl.ANY)
    return pl.pallas_call(
        body, in_specs=[any_spec] * ns, out_specs=[any_spec] * ng,
        out_shape=[jax.ShapeDtypeStruct((N_DEV, r, w), dt) for (r, w, dt, _) in groups],
        scratch_shapes=[pltpu.SemaphoreType.DMA((7 * ng,)), pltpu.SemaphoreType.DMA((7 * ng,)),
                        pltpu.SemaphoreType.DMA((ns,))],
        name="exchange_grads")(*srcs)


def _row_tile(rows, off, target):
    for t in (512, 256, 128, 64, 32, 16, 8):
        if t <= target and rows % t == 0 and off % t == 0:
            return t
    raise ValueError((rows, off))


def adamw(recv, off, w, m, v, *, name):
    rows, W = w.shape
    tr = _row_tile(rows, off, 128)
    o = off // tr
    c1 = 1.0 - ADAM_B1 ** ADAM_STEP
    c2 = 1.0 - ADAM_B2 ** ADAM_STEP

    def body(r_ref, w_ref, m_ref, v_ref, g_ref, d_ref, mo_ref, vo_ref):
        g = r_ref[0].astype(F32)
        for j in range(1, N_DEV):
            g = g + r_ref[j].astype(F32)
        mn = ADAM_B1 * m_ref[...] + (1.0 - ADAM_B1) * g
        vn = ADAM_B2 * v_ref[...] + (1.0 - ADAM_B2) * (g * g)
        g_ref[...] = g
        mo_ref[...] = mn
        vo_ref[...] = vn
        d_ref[...] = -ADAM_LR * ((mn / c1) / (jnp.sqrt(vn / c2) + ADAM_EPS) + ADAM_WD * w_ref[...])

    spec = pl.BlockSpec((tr, W), lambda i: (i, 0))
    return pl.pallas_call(
        body, grid=(rows // tr,), in_specs=[pl.BlockSpec((N_DEV, tr, W), lambda i: (0, i + o, 0)), spec, spec, spec],
        out_specs=[spec] * 4, out_shape=[jax.ShapeDtypeStruct((rows, W), F32)] * 4,
        compiler_params=_cp("parallel"), name=name)(recv, w, m, v)


def join_columns(gathered, off, K, *, name):
    _, _, n = gathered.shape
    tr = _row_tile(K, off, 256)
    o = off // tr

    def body(i_ref, o_ref):
        for d in range(N_DEV):
            o_ref[:, d * n:(d + 1) * n] = i_ref[d]

    return pl.pallas_call(
        body, grid=(K // tr,), in_specs=[pl.BlockSpec((N_DEV, tr, n), lambda i: (0, i + o, 0))],
        out_specs=pl.BlockSpec((tr, N_DEV * n), lambda i: (i, 0)),
        out_shape=jax.ShapeDtypeStruct((K, N_DEV * n), gathered.dtype), compiler_params=_cp("parallel"),
        name=name)(gathered)


def split_columns(full, *, name):
    K, N = full.shape
    n = N // N_DEV
    tr = _row_tile(K, 0, 256)

    def body(i_ref, o_ref):
        for d in range(N_DEV):
            o_ref[d] = i_ref[:, d * n:(d + 1) * n].astype(o_ref.dtype)

    return pl.pallas_call(
        body, grid=(K // tr,), in_specs=[pl.BlockSpec((tr, N), lambda i: (i, 0))],
        out_specs=pl.BlockSpec((N_DEV, tr, n), lambda i: (0, i, 0)),
        out_shape=jax.ShapeDtypeStruct((N_DEV, K, n), BF16), compiler_params=_cp("parallel"), name=name)(full)


def _pack(arrs, dtype, row_mult):
    flat = jnp.concatenate([a.reshape(-1).astype(dtype) for a in arrs])
    rows = -(-flat.shape[0] // PACK_W)
    rows = -(-rows // row_mult) * row_mult
    return jnp.pad(flat, (0, rows * PACK_W - flat.shape[0])).reshape(rows, PACK_W)


def _pack_dev(arrs, dtype, row_mult):
    flat = jnp.concatenate([a.reshape(N_DEV, -1).astype(dtype) for a in arrs], axis=1)
    rows = -(-flat.shape[1] // PACK_W)
    rows = -(-rows // row_mult) * row_mult
    return jnp.pad(flat, ((0, 0), (0, rows * PACK_W - flat.shape[1]))).reshape(N_DEV, rows, PACK_W)


def _unpack(buf, shapes):
    lead = buf.shape[:-2]
    flat = buf.reshape(lead + (-1,))
    outs, off = [], 0
    for s in shapes:
        n = math.prod(s)
        outs.append(flat[..., off:off + n].reshape(lead + tuple(s)))
        off += n
    return outs


def _join(g, axis):
    g = jnp.moveaxis(g, 0, axis)
    return g.reshape(g.shape[:axis] + (g.shape[axis] * g.shape[axis + 1],) + g.shape[axis + 2:])


def _split(full, axis):
    s = full.shape
    g = full.reshape(s[:axis] + (N_DEV, s[axis] // N_DEV) + s[axis + 1:])
    return jnp.moveaxis(g, axis, 0)


def kernel(x, p, a_pw1_w, a_pw1_b, a_dw_w, a_dw_b, a_ln_g, a_ln_b, a_pw2_w, a_pw2_b, b_wq, kv_wk, kv_wv, b_wo, ln_mix_g, ln_mix_b, ffn_w_up, ffn_w_gate, ffn_conv_w, ffn_conv_b, ffn_w_down, ple_w_gate, ple_w_proj, ln_ffn_g, ln_ffn_b, loss_target, m_a_pw1_w, m_a_pw1_b, m_a_dw_w, m_a_dw_b, m_a_ln_g, m_a_ln_b, m_a_pw2_w, m_a_pw2_b, m_b_wq, m_kv_wk, m_kv_wv, m_b_wo, m_ln_mix_g, m_ln_mix_b, m_ffn_w_up, m_ffn_w_gate, m_ffn_conv_w, m_ffn_conv_b, m_ffn_w_down, m_ple_w_gate, m_ple_w_proj, m_ln_ffn_g, m_ln_ffn_b, v_a_pw1_w, v_a_pw1_b, v_a_dw_w, v_a_dw_b, v_a_ln_g, v_a_ln_b, v_a_pw2_w, v_a_pw2_b, v_b_wq, v_kv_wk, v_kv_wv, v_b_wo, v_ln_mix_g, v_ln_mix_b, v_ffn_w_up, v_ffn_w_gate, v_ffn_conv_w, v_ffn_conv_b, v_ffn_w_down, v_ple_w_gate, v_ple_w_proj, v_ln_ffn_g, v_ln_ffn_b):
    local = dict(a_pw1_w=a_pw1_w, a_pw1_b=a_pw1_b, a_dw_w=a_dw_w, a_dw_b=a_dw_b, a_ln_g=a_ln_g, a_ln_b=a_ln_b, a_pw2_w=a_pw2_w, a_pw2_b=a_pw2_b, b_wq=b_wq, kv_wk=kv_wk, kv_wv=kv_wv, b_wo=b_wo, ln_mix_g=ln_mix_g, ln_mix_b=ln_mix_b, ffn_w_up=ffn_w_up, ffn_w_gate=ffn_w_gate, ffn_conv_w=ffn_conv_w, ffn_conv_b=ffn_conv_b, ffn_w_down=ffn_w_down, ple_w_gate=ple_w_gate, ple_w_proj=ple_w_proj, ln_ffn_g=ln_ffn_g, ln_ffn_b=ln_ffn_b)
    mom1 = dict(a_pw1_w=m_a_pw1_w, a_pw1_b=m_a_pw1_b, a_dw_w=m_a_dw_w, a_dw_b=m_a_dw_b, a_ln_g=m_a_ln_g, a_ln_b=m_a_ln_b, a_pw2_w=m_a_pw2_w, a_pw2_b=m_a_pw2_b, b_wq=m_b_wq, kv_wk=m_kv_wk, kv_wv=m_kv_wv, b_wo=m_b_wo, ln_mix_g=m_ln_mix_g, ln_mix_b=m_ln_mix_b, ffn_w_up=m_ffn_w_up, ffn_w_gate=m_ffn_w_gate, ffn_conv_w=m_ffn_conv_w, ffn_conv_b=m_ffn_conv_b, ffn_w_down=m_ffn_w_down, ple_w_gate=m_ple_w_gate, ple_w_proj=m_ple_w_proj, ln_ffn_g=m_ln_ffn_g, ln_ffn_b=m_ln_ffn_b)
    mom2 = dict(a_pw1_w=v_a_pw1_w, a_pw1_b=v_a_pw1_b, a_dw_w=v_a_dw_w, a_dw_b=v_a_dw_b, a_ln_g=v_a_ln_g, a_ln_b=v_a_ln_b, a_pw2_w=v_a_pw2_w, a_pw2_b=v_a_pw2_b, b_wq=v_b_wq, kv_wk=v_kv_wk, kv_wv=v_kv_wv, b_wo=v_b_wo, ln_mix_g=v_ln_mix_g, ln_mix_b=v_ln_mix_b, ffn_w_up=v_ffn_w_up, ffn_w_gate=v_ffn_w_gate, ffn_conv_w=v_ffn_conv_w, ffn_conv_b=v_ffn_conv_b, ffn_w_down=v_ffn_w_down, ple_w_gate=v_ple_w_gate, ple_w_proj=v_ple_w_proj, ln_ffn_g=v_ln_ffn_g, ln_ffn_b=v_ln_ffn_b)
    small_names = [n for n, _ in SMALL]
    small_shapes = [local[n].shape for n in small_names]
    repl_shapes = [local[n].shape for n in REPL]

    widths = sorted({local[n].shape[-1] for n, _ in BIG}, reverse=True)
    groups = {w: [n for n, _ in BIG if local[n].shape[-1] == w] for w in widths}
    offset, rows_of = {}, {}
    for w, names in groups.items():
        off = 0
        for n in names:
            offset[n], rows_of[n] = off, math.prod(local[n].shape[:-1])
            off += rows_of[n]
    sends = [jnp.concatenate([local[n].reshape(-1, w).astype(BF16) for n in names]) for w, names in groups.items()]
    gathered = all_gather(sends + [_pack([local[n] for n in small_names], F32, SUBLANES)])
    gath = dict(zip(widths, gathered[:-1]))
    W = {}
    for n, ax in BIG:
        w = local[n].shape[-1]
        nl = local[n].shape[0] if local[n].ndim == 3 else 1
        per = rows_of[n] // nl
        if ax == local[n].ndim - 1:
            W[n] = [join_columns(gath[w], offset[n] + l * per, per, name=f"join_{n}_{l}") for l in range(nl)]
        else:
            W[n] = [gath[w][:, offset[n] + l * per:offset[n] + (l + 1) * per].reshape(N_DEV * per, w) for l in range(nl)]
    for n in ("kv_wk", "kv_wv"):
        W[n] = W[n][0]
    W.update({n: _join(g, ax) for (n, ax), g in zip(SMALL, _unpack(gathered[-1], small_shapes))})
    W.update({n: local[n] for n in REPL})

    xs = x[0]
    S, D = xs.shape
    x_in, r1s, x1s, r2s, us, gps, gs, hhs, pgls, pps = [], [], [], [], [], [], [], [], [], []
    h1s, h2s, h3s, h5s, qs, os_, tots = {}, {}, {}, {}, {}, {}, {}
    kk = vv = None
    for i in range(DEPTH):
        x_in.append(xs)
        if i < N_A:
            h1 = mm(xs, W["a_pw1_w"][i], "nn", bias=W["a_pw1_b"][i][None], name=f"pw1_{i}")
            h2 = glu_fwd(h1, name=f"glu_{i}")
            h3, h5 = conv_ln_silu_fwd(h2, W["a_dw_w"][i], W["a_dw_b"][i][None], W["a_ln_g"][i][None],
                                      W["a_ln_b"][i][None], name=f"dwconv_{i}")
            mix = mm(h5, W["a_pw2_w"][i], "nn", bias=W["a_pw2_b"][i][None], name=f"pw2_{i}")
            h1s[i], h2s[i], h3s[i], h5s[i] = h1, h2, h3, h5
        else:
            j = i - N_A
            if kk is None:
                kk = mm(xs, W["kv_wk"], "nn", out_dtype=BF16, name="proj_k")
                vv = mm(xs, W["kv_wv"], "nn", out_dtype=BF16, name="proj_v")
            q = mm(xs, W["b_wq"][j], "nn", out_dtype=BF16, name=f"proj_q_{i}")
            o, tot = attn_fwd(q, kk, vv, name=f"attn_{i}")
            mix = mm(o, W["b_wo"][j], "nn", name=f"proj_o_{i}")
            qs[i], os_[i], tots[i] = q, o, tot
        r1, x1 = res_ln(xs, mix, W["ln_mix_g"][i][None], W["ln_mix_b"][i][None], name=f"ln_mix_{i}")
        u = mm(x1, W["ffn_w_up"][i], "nn", out_dtype=BF16, name=f"ffn_up_{i}")
        gp = mm(x1, W["ffn_w_gate"][i], "nn", out_dtype=BF16, name=f"ffn_gate_{i}")
        g, hh = conv_act_fwd(gp, u, W["ffn_conv_w"][i], W["ffn_conv_b"][i][None], name=f"ffn_conv_{i}")
        f = mm(hh, W["ffn_w_down"][i], "nn", name=f"ffn_down_{i}")
        pgl = mm(x1, W["ple_w_gate"][i], "nn", name=f"ple_gate_{i}")
        pp = mm(p[i, 0], W["ple_w_proj"][i], "nn", name=f"ple_proj_{i}")
        r2, xs = res_ln(x1, f, W["ln_ffn_g"][i][None], W["ln_ffn_b"][i][None], ple=(pgl, pp), name=f"ln_ffn_{i}")
        for lst, val in ((r1s, r1), (x1s, x1), (r2s, r2), (us, u), (gps, gp), (gs, g), (hhs, hh), (pgls, pgl), (pps, pp)):
            lst.append(val)

    dx, loss_part = loss_grad(xs, loss_target[0], name="loss")
    G = {n: [None] * local[n].shape[0] for n in WEIGHTS if n not in ("kv_wk", "kv_wv")}
    dk = dv = None
    for i in reversed(range(DEPTH)):
        x1 = x1s[i]
        dr2, G["ln_ffn_g"][i], G["ln_ffn_b"][i], _ = ln_bwd(r2s[i], W["ln_ffn_g"][i][None], dx, name=f"ln_ffn_bwd_{i}")
        dhh = mm(dr2, W["ffn_w_down"][i], "nt", out_dtype=BF16, name=f"ffn_down_dx_{i}")
        G["ffn_w_down"][i] = mm(hhs[i], dr2, "tn", out_dtype=BF16, name=f"ffn_down_dw_{i}")
        dpp, dpgl = ple_bwd(dr2, pgls[i], pps[i], name=f"ple_bwd_{i}")
        G["ple_w_proj"][i] = mm(p[i, 0], dpp, "tn", out_dtype=BF16, name=f"ple_proj_dw_{i}")
        G["ple_w_gate"][i] = mm(x1, dpgl, "tn", out_dtype=BF16, name=f"ple_gate_dw_{i}")
        du, dg = ffn_act_bwd(dhh, us[i], gs[i], name=f"ffn_act_bwd_{i}")
        dgp = conv_bwd_x(dg, W["ffn_conv_w"][i], out_dtype=BF16, name=f"ffn_conv_dx_{i}")
        G["ffn_conv_w"][i], G["ffn_conv_b"][i] = conv_bwd_w(gps[i], dg, FFN_CONV_W, name=f"ffn_conv_dw_{i}")
        G["ffn_w_up"][i] = mm(x1, du, "tn", out_dtype=BF16, name=f"ffn_up_dw_{i}")
        G["ffn_w_gate"][i] = mm(x1, dgp, "tn", out_dtype=BF16, name=f"ffn_gate_dw_{i}")
        dx1 = mm(du, W["ffn_w_up"][i], "nt", add=dr2, add_scale=DN_ALPHA, name=f"ffn_up_dx_{i}")
        dx1 = mm(dgp, W["ffn_w_gate"][i], "nt", add=dx1, name=f"ffn_gate_dx_{i}")
        dx1 = mm(dpgl, W["ple_w_gate"][i], "nt", add=dx1, name=f"ple_gate_dx_{i}")
        dr1, G["ln_mix_g"][i], G["ln_mix_b"][i], dr1_sum = ln_bwd(r1s[i], W["ln_mix_g"][i][None], dx1, name=f"ln_mix_bwd_{i}")
        if i < N_A:
            G["a_pw2_w"][i] = mm(h5s[i], dr1, "tn", out_dtype=BF16, name=f"pw2_dw_{i}")
            G["a_pw2_b"][i] = dr1_sum
            dh5 = mm(dr1, W["a_pw2_w"][i], "nt", name=f"pw2_dx_{i}")
            dh3, G["a_ln_g"][i], G["a_ln_b"][i] = ln_silu_bwd(h3s[i], W["a_ln_g"][i][None], W["a_ln_b"][i][None], dh5,
                                                             name=f"dwconv_ln_bwd_{i}")
            dh2 = conv_bwd_x(dh3, W["a_dw_w"][i], out_dtype=F32, name=f"dwconv_dx_{i}")
            G["a_dw_w"][i], G["a_dw_b"][i] = conv_bwd_w(h2s[i], dh3, CONV_W, name=f"dwconv_dw_{i}")
            dh1, G["a_pw1_b"][i] = glu_bwd(h1s[i], dh2, name=f"glu_bwd_{i}")
            G["a_pw1_w"][i] = mm(x_in[i], dh1, "tn", out_dtype=BF16, name=f"pw1_dw_{i}")
            dx = mm(dh1, W["a_pw1_w"][i], "nt", add=dr1, add_scale=DN_ALPHA, name=f"pw1_dx_{i}")
        else:
            j = i - N_A
            G["b_wo"][j] = mm(os_[i], dr1, "tn", out_dtype=BF16, name=f"proj_o_dw_{i}")
            do = mm(dr1, W["b_wo"][j], "nt", name=f"proj_o_dx_{i}")
            dq, dk, dv = attn_bwd(qs[i], kk, vv, tots[i], do, dk, dv, name=f"attn_bwd_{i}")
            G["b_wq"][j] = mm(x_in[i], dq, "tn", out_dtype=BF16, name=f"proj_q_dw_{i}")
            dx = mm(dq, W["b_wq"][j], "nt", add=dr1, add_scale=DN_ALPHA, name=f"proj_q_dx_{i}")
            if j == 0:
                G["kv_wk"] = mm(x_in[i], dk, "tn", out_dtype=BF16, name="proj_k_dw")
                G["kv_wv"] = mm(x_in[i], dv, "tn", out_dtype=BF16, name="proj_v_dw")
                dx = mm(dk, W["kv_wk"], "nt", add=dx, name="proj_k_dx")
                dx = mm(dv, W["kv_wv"], "nt", add=dx, name="proj_v_dx")
    grad_x = dx[None]
    shard_axis = dict(BIG + SMALL)
    for n in small_names + list(REPL):
        full = list(local[n].shape)
        if n in shard_axis:
            full[shard_axis[n]] *= N_DEV
        G[n] = jnp.stack(G[n]).reshape(full)

    n_small = sum(math.prod(s) for s in small_shapes)
    n_repl = sum(math.prod(s) for s in repl_shapes)
    repl_flat = jnp.concatenate([G[n].reshape(-1) for n in REPL] + [loss_part.reshape(-1)[:1]])
    send_small = _pack_dev([_split(G[n], ax) for n, ax in SMALL] + [jnp.broadcast_to(repl_flat, (N_DEV, n_repl + 1))],
                           F32, SUBLANES)
    ex_groups = []
    for w, names in groups.items():
        lst = []
        for n in names:
            layers = G[n] if isinstance(G[n], list) else [G[n]]
            per = rows_of[n] // len(layers)
            for l, g in enumerate(layers):
                if shard_axis[n] == local[n].ndim - 1:
                    src = split_columns(g, name=f"split_{n}_{l}")
                else:
                    src = g.reshape(N_DEV, per, w)
                lst.append((src, offset[n] + l * per))
        ex_groups.append((sum(rows_of[n] for n in names), w, BF16, lst))
    ex_groups.append((send_small.shape[1], PACK_W, F32, [(send_small, 0)]))
    recvs = exchange_grads(ex_groups)
    recv = dict(zip(widths, recvs[:-1]))
    recv_small = recvs[-1]

    out = {}
    for n, _ in BIG:
        w = local[n].shape[-1]
        res = adamw(recv[w], offset[n], local[n].reshape(-1, w), mom1[n].reshape(-1, w), mom2[n].reshape(-1, w),
                    name=f"adamw_{n}")
        out[n] = [r.reshape(local[n].shape) for r in res]

    def state(d):
        small = _pack([d[n] for n in small_names] + [d[n] for n in REPL], F32, SUBLANES)
        return jnp.pad(small, ((0, recv_small.shape[1] - small.shape[0]), (0, 0)))

    out_small = adamw(recv_small, 0, state(local), state(mom1), state(mom2), name="adamw_vectors")
    loss = out_small[0].reshape(-1)[n_small + n_repl]
    vecs = [dict(zip(small_names + list(REPL), _unpack(o, small_shapes + repl_shapes))) for o in out_small]
    per_kind = [[out[n][kind] if n in out else vecs[kind][n] for n in WEIGHTS] for kind in range(4)]
    grads, deltas, new_m, new_v = per_kind
    return (loss, grad_x, *grads, *deltas, *new_m, *new_v)
```

```python
import functools
import math

import jax
import jax.numpy as jnp
from jax import lax
from jax.experimental import pallas as pl
from jax.experimental.pallas import tpu as pltpu

F32 = jnp.float32
BF16 = jnp.bfloat16
MESH = pl.DeviceIdType.MESH

N_DEV = 8
DEPTH = 4
N_A = 2
HEAD_DIM = 64
Q_BLOCK = 128
CONV_W = 31
FFN_CONV_W = 3
LN_EPS = 1e-5
DN_ALPHA = (2.0 * DEPTH) ** 0.25
ADAM_LR = 0.001
ADAM_B1 = 0.9
ADAM_B2 = 0.999
ADAM_EPS = 1e-08
ADAM_WD = 0.01
ADAM_STEP = 10

LANES = 128
SUBLANES = 8
PACK_W = 1024
VMEM_LIMIT = 56 * 1024 * 1024

BIG = (("a_pw1_w", 2), ("a_pw2_w", 1), ("b_wq", 1), ("kv_wk", 0), ("kv_wv", 0), ("b_wo", 1),
       ("ffn_w_up", 2), ("ffn_w_gate", 2), ("ffn_w_down", 1), ("ple_w_gate", 1), ("ple_w_proj", 2))
SMALL = (("a_pw1_b", 1), ("a_dw_w", 2), ("a_dw_b", 1), ("a_ln_g", 1), ("a_ln_b", 1), ("a_pw2_b", 1),
         ("ffn_conv_w", 2))
REPL = ("ln_mix_g", "ln_mix_b", "ffn_conv_b", "ln_ffn_g", "ln_ffn_b")
WEIGHTS = ("a_pw1_w", "a_pw1_b", "a_dw_w", "a_dw_b", "a_ln_g", "a_ln_b", "a_pw2_w", "a_pw2_b", "b_wq", "kv_wk",
           "kv_wv", "b_wo", "ln_mix_g", "ln_mix_b", "ffn_w_up", "ffn_w_gate", "ffn_conv_w", "ffn_conv_b",
           "ffn_w_down", "ple_w_gate", "ple_w_proj", "ln_ffn_g", "ln_ffn_b")


def _cp(*sem):
    return pltpu.CompilerParams(dimension_semantics=sem, vmem_limit_bytes=VMEM_LIMIT)


def _pick(dim, target, align=LANES):
    if dim <= target:
        return dim
    t = (target // align) * align
    while t >= align:
        if dim % t == 0:
            return t
        t -= align
    return dim


_DOT_DIMS = {"nn": (((1,), (0,)), ((), ())), "nt": (((1,), (1,)), ((), ())), "tn": (((0,), (0,)), ((), ()))}


def mm(a, b, mode, *, bias=None, add=None, add_scale=1.0, out_dtype=F32, name):
    if mode == "tn":
        K, M = a.shape
    else:
        M, K = a.shape
    N = b.shape[0] if mode == "nt" else b.shape[1]
    tm, tn, tk = _pick(M, 1536 if mode == "tn" else 512), _pick(N, 1536), _pick(K, 1536)
    nk = K // tk
    dims = _DOT_DIMS[mode]

    def body(*refs):
        a_ref, b_ref = refs[0], refs[1]
        pos = 2
        bias_ref = add_ref = None
        if bias is not None:
            bias_ref = refs[pos]
            pos += 1
        if add is not None:
            add_ref = refs[pos]
            pos += 1
        o_ref, acc_ref = refs[pos], refs[pos + 1]
        k = pl.program_id(2)

        @pl.when(k == 0)
        def _():
            acc_ref[...] = jnp.zeros_like(acc_ref)

        acc_ref[...] += lax.dot_general(a_ref[...].astype(BF16), b_ref[...].astype(BF16), dims,
                                        preferred_element_type=F32)

        @pl.when(k == nk - 1)
        def _():
            r = acc_ref[...]
            if bias_ref is not None:
                r = r + bias_ref[...]
            if add_ref is not None:
                r = r + add_scale * add_ref[...].astype(F32)
            o_ref[...] = r.astype(o_ref.dtype)

    a_spec = pl.BlockSpec((tk, tm), lambda j, i, k: (k, i)) if mode == "tn" else pl.BlockSpec((tm, tk), lambda j, i, k: (i, k))
    b_spec = pl.BlockSpec((tn, tk), lambda j, i, k: (j, k)) if mode == "nt" else pl.BlockSpec((tk, tn), lambda j, i, k: (k, j))
    in_specs, args = [a_spec, b_spec], [a, b]
    if bias is not None:
        in_specs.append(pl.BlockSpec((1, tn), lambda j, i, k: (0, j)))
        args.append(bias)
    if add is not None:
        in_specs.append(pl.BlockSpec((tm, tn), lambda j, i, k: (i, j)))
        args.append(add)
    return pl.pallas_call(
        body, grid=(N // tn, M // tm, nk), in_specs=in_specs,
        out_specs=pl.BlockSpec((tm, tn), lambda j, i, k: (i, j)),
        out_shape=jax.ShapeDtypeStruct((M, N), out_dtype),
        scratch_shapes=[pltpu.VMEM((tm, tn), F32)],
        compiler_params=_cp("parallel", "parallel", "arbitrary"), name=name)(*args)


def _rows(body, *, n_rows, tm, row_ins, full_ins=(), row_outs=(), acc_outs=(), scratch=(), reverse=False, name):
    n = n_rows // tm

    def rmap(i):
        return (n - 1 - i, 0) if reverse else (i, 0)

    in_specs = [pl.BlockSpec((tm, a.shape[1]), rmap) for a in row_ins]
    in_specs += [pl.BlockSpec(a.shape, lambda i, nd=a.ndim: (0,) * nd) for a in full_ins]
    out_shape = [jax.ShapeDtypeStruct((n_rows, w), dt) for (w, dt) in row_outs]
    out_shape += [jax.ShapeDtypeStruct(s, dt) for (s, dt) in acc_outs]
    out_specs = [pl.BlockSpec((tm, w), rmap) for (w, dt) in row_outs]
    out_specs += [pl.BlockSpec(s, lambda i, nd=len(s): (0,) * nd) for (s, dt) in acc_outs]
    return pl.pallas_call(
        functools.partial(body, n), grid=(n,), in_specs=in_specs, out_specs=out_specs, out_shape=out_shape,
        scratch_shapes=list(scratch), compiler_params=_cp("arbitrary"), name=name)(*row_ins, *full_ins)


def _sigmoid(x):
    return 1.0 / (1.0 + jnp.exp(-x))


def _ln_hat(r):
    mu = jnp.mean(r, axis=-1, keepdims=True)
    xc = r - mu
    var = jnp.mean(xc * xc, axis=-1, keepdims=True)
    rstd = lax.rsqrt(var + LN_EPS)
    return xc * rstd, rstd


def _ln_back(xhat, rstd, g, dy):
    dxh = dy * g
    m1 = jnp.mean(dxh, axis=-1, keepdims=True)
    m2 = jnp.mean(dxh * xhat, axis=-1, keepdims=True)
    return rstd * (dxh - m1 - xhat * m2)


def _colsum(x):
    return jnp.sum(x, axis=0, keepdims=True)


def _acc(i, ref, val):
    @pl.when(i == 0)
    def _():
        ref[...] = val

    @pl.when(i > 0)
    def _():
        ref[...] += val


def res_ln(x, mix, g, b, *, ple=None, name):
    S, D = x.shape

    def body(n, *refs):
        if ple is None:
            x_ref, m_ref, g_ref, b_ref, r_ref, y_ref = refs
            r = DN_ALPHA * x_ref[...] + m_ref[...]
        else:
            x_ref, m_ref, pgl_ref, pp_ref, g_ref, b_ref, r_ref, y_ref = refs
            r = DN_ALPHA * x_ref[...] + m_ref[...] + _sigmoid(pgl_ref[...]) * pp_ref[...]
        xhat, _ = _ln_hat(r)
        r_ref[...] = r
        y_ref[...] = xhat * g_ref[...] + b_ref[...]

    row_ins = [x, mix] + ([] if ple is None else list(ple))
    return _rows(body, n_rows=S, tm=_pick(S, 256, SUBLANES), row_ins=row_ins, full_ins=[g, b],
                 row_outs=[(D, F32), (D, F32)], name=name)


def ln_bwd(r, g, dy, *, name):
    S, D = r.shape

    def body(n, r_ref, dy_ref, g_ref, dr_ref, dg_ref, db_ref, ds_ref):
        i = pl.program_id(0)
        xhat, rstd = _ln_hat(r_ref[...])
        dy_v = dy_ref[...]
        dr = _ln_back(xhat, rstd, g_ref[...], dy_v)
        dr_ref[...] = dr
        _acc(i, dg_ref, _colsum(dy_v * xhat))
        _acc(i, db_ref, _colsum(dy_v))
        _acc(i, ds_ref, _colsum(dr))

    return _rows(body, n_rows=S, tm=_pick(S, 256, SUBLANES), row_ins=[r, dy], full_ins=[g],
                 row_outs=[(D, F32)], acc_outs=[((1, D), F32)] * 3, name=name)


def glu_fwd(h1, *, name):
    S, D2 = h1.shape
    D = D2 // 2

    def body(n, h_ref, o_ref):
        o_ref[...] = h_ref[:, :D] * _sigmoid(h_ref[:, D:])

    return _rows(body, n_rows=S, tm=_pick(S, 256, SUBLANES), row_ins=[h1], row_outs=[(D, F32)], name=name)[0]


def glu_bwd(h1, dh2, *, name):
    S, D2 = h1.shape
    D = D2 // 2

    def body(n, h_ref, d_ref, o_ref, s_ref):
        i = pl.program_id(0)
        a, sg, d = h_ref[:, :D], _sigmoid(h_ref[:, D:]), d_ref[...]
        da = d * sg
        dg = d * a * sg * (1.0 - sg)
        o_ref[:, :D] = da.astype(o_ref.dtype)
        o_ref[:, D:] = dg.astype(o_ref.dtype)
        _acc(i, s_ref, jnp.concatenate([_colsum(da), _colsum(dg)], axis=1))

    return _rows(body, n_rows=S, tm=_pick(S, 256, SUBLANES), row_ins=[h1, dh2], row_outs=[(D2, BF16)],
                 acc_outs=[((1, D2), F32)], name=name)


CONV_ROWS = 32
CONV_LANES = 256


def _halo(k):
    return -(-(k - 1) // SUBLANES) * SUBLANES


def _phases(offs):
    return sorted({o % SUBLANES for o in offs} - {0})


def _shift_scratch(offs, n_rows, width):
    return pltpu.VMEM((max(len(_phases(offs)), 1), n_rows, width), F32)


def _make_shifted(buf_ref, sh_ref, offs):
    n = buf_ref.shape[0] - SUBLANES
    for p, b in enumerate(_phases(offs)):
        sh_ref[p, pl.ds(0, n), :] = buf_ref[pl.ds(b, n), :]


def _tap(buf_ref, sh_ref, offs, k, rc, rows, lc, lw):
    b = offs[k] % SUBLANES
    src = buf_ref if b == 0 else sh_ref.at[_phases(offs).index(b)]
    return src[pl.ds(offs[k] - b + rc, rows), pl.ds(lc, lw)]


def _conv_taps(buf_ref, sh_ref, w_ref, offs, tm, width, emit):
    _make_shifted(buf_ref, sh_ref, offs)
    rows = min(CONV_ROWS, tm)
    for lc in range(0, width, CONV_LANES):
        lw = min(CONV_LANES, width - lc)
        for rc in range(0, tm, rows):
            acc = None
            for k in range(len(offs)):
                t = _tap(buf_ref, sh_ref, offs, k, rc, rows, lc, lw) * w_ref[pl.ds(k, 1), pl.ds(lc, lw)]
                acc = t if acc is None else acc + t
            emit(rc, lc, lw, rows, acc)


def _fill_causal(i, buf_ref, x_ref, halo, tm):
    @pl.when(i == 0)
    def _():
        buf_ref[pl.ds(0, halo), :] = jnp.zeros((halo, buf_ref.shape[1]), F32)

    @pl.when(i > 0)
    def _():
        buf_ref[pl.ds(0, halo), :] = buf_ref[pl.ds(tm, halo), :]

    buf_ref[pl.ds(halo, tm), :] = x_ref[...].astype(F32)


def conv_ln_silu_fwd(x, w, b, g, beta, *, name):
    S, C = x.shape
    K = w.shape[0]
    halo = _halo(K)
    tm = _pick(S, 256, SUBLANES)
    offs = [halo - (K - 1) + k for k in range(K)]

    def body(n, x_ref, w_ref, b_ref, g_ref, beta_ref, h3_ref, h5_ref, buf_ref, sh_ref):
        i = pl.program_id(0)
        _fill_causal(i, buf_ref, x_ref, halo, tm)

        def emit(rc, lc, lw, rows, acc):
            h3_ref[pl.ds(rc, rows), pl.ds(lc, lw)] = acc + b_ref[:, pl.ds(lc, lw)]

        _conv_taps(buf_ref, sh_ref, w_ref, offs, tm, C, emit)
        xhat, _ = _ln_hat(h3_ref[...])
        h4 = xhat * g_ref[...] + beta_ref[...]
        h5_ref[...] = (h4 * _sigmoid(h4)).astype(h5_ref.dtype)

    return _rows(body, n_rows=S, tm=tm, row_ins=[x], full_ins=[w, b, g, beta], row_outs=[(C, F32), (C, BF16)],
                 scratch=[pltpu.VMEM((tm + halo, C), F32), _shift_scratch(offs, tm + halo, C)], name=name)


def conv_act_fwd(gp, u, w, b, *, name):
    S, C = gp.shape
    K = w.shape[0]
    halo = _halo(K)
    tm = _pick(S, 256, SUBLANES)
    offs = [halo - (K - 1) + k for k in range(K)]

    def body(n, x_ref, u_ref, w_ref, b_ref, g_ref, hh_ref, buf_ref, sh_ref):
        i = pl.program_id(0)
        _fill_causal(i, buf_ref, x_ref, halo, tm)

        def emit(rc, lc, lw, rows, acc):
            gv = acc + b_ref[:, pl.ds(lc, lw)]
            g_ref[pl.ds(rc, rows), pl.ds(lc, lw)] = gv.astype(g_ref.dtype)
            hh_ref[pl.ds(rc, rows), pl.ds(lc, lw)] = (gv * _sigmoid(gv) * u_ref[pl.ds(rc, rows), pl.ds(lc, lw)].astype(F32)).astype(hh_ref.dtype)

        _conv_taps(buf_ref, sh_ref, w_ref, offs, tm, C, emit)

    return _rows(body, n_rows=S, tm=tm, row_ins=[gp, u], full_ins=[w, b], row_outs=[(C, BF16), (C, BF16)],
                 scratch=[pltpu.VMEM((tm + halo, C), F32), _shift_scratch(offs, tm + halo, C)], name=name)


def conv_bwd_x(dy, w, *, out_dtype, name):
    S, C = dy.shape
    K = w.shape[0]
    halo = _halo(K)
    tm = _pick(S, 256, SUBLANES)
    offs = [K - 1 - k for k in range(K)]

    def body(n, dy_ref, w_ref, dx_ref, buf_ref, sh_ref):
        i = pl.program_id(0)

        @pl.when(i == 0)
        def _():
            buf_ref[pl.ds(tm, halo), :] = jnp.zeros((halo, C), F32)

        @pl.when(i > 0)
        def _():
            buf_ref[pl.ds(tm, halo), :] = buf_ref[pl.ds(0, halo), :]

        buf_ref[pl.ds(0, tm), :] = dy_ref[...].astype(F32)

        def emit(rc, lc, lw, rows, acc):
            dx_ref[pl.ds(rc, rows), pl.ds(lc, lw)] = acc.astype(dx_ref.dtype)

        _conv_taps(buf_ref, sh_ref, w_ref, offs, tm, C, emit)

    return _rows(body, n_rows=S, tm=tm, row_ins=[dy], full_ins=[w], row_outs=[(C, out_dtype)],
                 scratch=[pltpu.VMEM((tm + halo, C), F32), _shift_scratch(offs, tm + halo, C)], reverse=True, name=name)[0]


def conv_bwd_w(x, dy, K, *, name):
    S, C = x.shape
    halo = _halo(K)
    tm = _pick(S, 256, SUBLANES)
    offs = [halo - (K - 1) + k for k in range(K)]
    rows = min(CONV_ROWS, tm)

    def body(n, x_ref, dy_ref, dw_ref, db_ref, buf_ref, acc_ref, sh_ref):
        i = pl.program_id(0)
        _fill_causal(i, buf_ref, x_ref, halo, tm)
        _make_shifted(buf_ref, sh_ref, offs)

        @pl.when(i == 0)
        def _():
            acc_ref[...] = jnp.zeros_like(acc_ref)

        for lc in range(0, C, CONV_LANES):
            lw = min(CONV_LANES, C - lc)
            for k in range(K):
                s = None
                for rc in range(0, tm, rows):
                    t = dy_ref[pl.ds(rc, rows), pl.ds(lc, lw)].astype(F32) * _tap(buf_ref, sh_ref, offs, k, rc, rows, lc, lw)
                    s = t if s is None else s + t
                s8 = s[0:SUBLANES]
                for q in range(1, rows // SUBLANES):
                    s8 = s8 + s[q * SUBLANES:(q + 1) * SUBLANES]
                acc_ref[pl.ds(k * SUBLANES, SUBLANES), pl.ds(lc, lw)] += s8
        _acc(i, db_ref, _colsum(dy_ref[...].astype(F32)))

        @pl.when(i == n - 1)
        def _():
            for k in range(K):
                dw_ref[pl.ds(k, 1), :] = _colsum(acc_ref[pl.ds(k * SUBLANES, SUBLANES), :])

    return _rows(body, n_rows=S, tm=tm, row_ins=[x, dy], acc_outs=[((K, C), F32), ((1, C), F32)],
                 scratch=[pltpu.VMEM((tm + halo, C), F32), pltpu.VMEM((K * SUBLANES, C), F32),
                          _shift_scratch(offs, tm + halo, C)], name=name)


def ln_silu_bwd(h3, g, beta, dh5, *, name):
    S, C = h3.shape

    def body(n, h_ref, d_ref, g_ref, beta_ref, dh_ref, dg_ref, db_ref):
        i = pl.program_id(0)
        xhat, rstd = _ln_hat(h_ref[...])
        h4 = xhat * g_ref[...] + beta_ref[...]
        sg = _sigmoid(h4)
        dh4 = d_ref[...] * sg * (1.0 + h4 * (1.0 - sg))
        dh_ref[...] = _ln_back(xhat, rstd, g_ref[...], dh4)
        _acc(i, dg_ref, _colsum(dh4 * xhat))
        _acc(i, db_ref, _colsum(dh4))

    return _rows(body, n_rows=S, tm=_pick(S, 256, SUBLANES), row_ins=[h3, dh5], full_ins=[g, beta],
                 row_outs=[(C, F32)], acc_outs=[((1, C), F32)] * 2, name=name)


def ffn_act_bwd(dhh, u, g, *, name):
    S, C = u.shape

    def body(n, d_ref, u_ref, g_ref, du_ref, dg_ref):
        d, gv = d_ref[...].astype(F32), g_ref[...].astype(F32)
        sg = _sigmoid(gv)
        du_ref[...] = (d * gv * sg).astype(du_ref.dtype)
        dg_ref[...] = (d * u_ref[...].astype(F32) * sg * (1.0 + gv * (1.0 - sg))).astype(dg_ref.dtype)

    return _rows(body, n_rows=S, tm=_pick(S, 256, SUBLANES), row_ins=[dhh, u, g], row_outs=[(C, BF16), (C, BF16)],
                 name=name)


def ple_bwd(dr, pgl, pp, *, name):
    S, D = dr.shape

    def body(n, d_ref, l_ref, p_ref, dpp_ref, dpl_ref):
        d, sg = d_ref[...], _sigmoid(l_ref[...])
        dpp_ref[...] = (d * sg).astype(dpp_ref.dtype)
        dpl_ref[...] = (d * p_ref[...] * sg * (1.0 - sg)).astype(dpl_ref.dtype)

    return _rows(body, n_rows=S, tm=_pick(S, 256, SUBLANES), row_ins=[dr, pgl, pp], row_outs=[(D, BF16), (D, BF16)],
                 name=name)


def loss_grad(y, target, *, name):
    S, D = y.shape

    def body(n, y_ref, t_ref, dy_ref, l_ref):
        i = pl.program_id(0)
        e = y_ref[...] - t_ref[...]
        dy_ref[...] = e * (1.0 / D)
        s = jnp.sum(_colsum(e * e), axis=1, keepdims=True) * (0.5 / D)
        _acc(i, l_ref, jnp.broadcast_to(s, (1, LANES)))

    return _rows(body, n_rows=S, tm=_pick(S, 256, SUBLANES), row_ins=[y, target], row_outs=[(D, F32)],
                 acc_outs=[((1, LANES), F32)], name=name)


def _key_step(S):
    return min(1024, S // 2)


EXIT_LOG = -110.0


def _attn_consts():
    lane = lax.broadcasted_iota(jnp.int32, (1, LANES), 1)
    heads = (lane < HEAD_DIM, lane >= HEAD_DIM)
    row = lax.broadcasted_iota(jnp.int32, (Q_BLOCK, Q_BLOCK), 0)
    col = lax.broadcasted_iota(jnp.int32, (Q_BLOCK, Q_BLOCK), 1)
    causal = jnp.concatenate([col < row] * 2, axis=0)
    return heads, row, col, causal


def _tri(cond):
    return jnp.where(cond, 1.0, 0.0).astype(BF16)


def _keysum2(x, tri):
    hi = x.astype(BF16)
    lo = (x - hi.astype(F32)).astype(BF16)
    return jnp.dot(jnp.concatenate([hi, lo], axis=1), jnp.concatenate([tri, tri], axis=0),
                   preferred_element_type=F32)


def _stack_heads(x, heads):
    return jnp.concatenate([jnp.where(m, x, jnp.zeros_like(x)) for m in heads], axis=0)


def _log1m_beta(z):
    return -(jnp.maximum(z, 0.0) + jnp.log(1.0 + jnp.exp(-jnp.abs(z))))


def attn_fwd(q, k, v, *, name):
    S, D = q.shape
    nb = S // Q_BLOCK
    tk = _key_step(S)
    nkb = tk // Q_BLOCK
    scale = 1.0 / math.sqrt(HEAD_DIM)

    def body(q_ref, k_ref, v_ref, o_ref, tot_ref, seen_ref, vm_ref):
        heads, row, col, causal = _attn_consts()
        above = _tri(row > col)
        for h in range(2):
            vm_ref[h] = jnp.where(heads[h], v_ref[...], jnp.zeros_like(v_ref[...]))

        def step(sb, carry, qq, nblk, diag):
            acc, cl = carry
            c0 = pl.multiple_of(sb * tk, tk)
            z = lax.dot_general(qq, k_ref[pl.ds(c0, nblk * Q_BLOCK), :], _DOT_DIMS["nt"], preferred_element_type=F32)
            zl, es, rs = [], [], []
            for jb in range(nblk):
                zb = z[:, jb * Q_BLOCK:(jb + 1) * Q_BLOCK]
                lr = _log1m_beta(zb)
                l = jnp.where(causal, lr, 0.0) if diag and jb == nblk - 1 else lr
                zl.append(zb + lr)
                es.append(_keysum2(l, above))
                rs.append(jnp.sum(l, axis=1, keepdims=True))
            a = [None] * nblk
            for jb in reversed(range(nblk)):
                ab = jnp.exp(zl[jb] + es[jb] + cl)
                if diag and jb == nblk - 1:
                    ab = jnp.where(causal, ab, 0.0)
                a[jb] = ab.astype(BF16)
                cl = cl + rs[jb]
            a = jnp.concatenate(a, axis=1)
            for h in range(2):
                acc = acc + jnp.dot(a[h * Q_BLOCK:(h + 1) * Q_BLOCK], vm_ref[h, pl.ds(c0, nblk * Q_BLOCK), :],
                                    preferred_element_type=F32)
            return acc, cl

        def qblock(i, _):
            r0 = pl.multiple_of(i * Q_BLOCK, Q_BLOCK)
            qq = _stack_heads(q_ref[pl.ds(r0, Q_BLOCK), :] * scale, heads)
            last = i // nkb
            carry = (jnp.zeros((Q_BLOCK, LANES), F32), jnp.zeros((2 * Q_BLOCK, 1), F32))
            carry = lax.switch(i % nkb, [functools.partial(step, last, qq=qq, nblk=m + 1, diag=True) for m in range(nkb)],
                               carry)

            def more(c):
                return jnp.logical_and(c[0] < last, jnp.max(c[2]) >= EXIT_LOG)

            def left(c):
                return (c[0] + 1, *step(last - 1 - c[0], c[1:], qq, nkb, False))

            seen, acc, cl = lax.while_loop(more, left, (jnp.int32(0), *carry))
            o_ref[pl.ds(r0, Q_BLOCK), :] = acc
            tot_ref[pl.ds(r0, Q_BLOCK), :] = jnp.where(heads[0], cl[:Q_BLOCK], cl[Q_BLOCK:])
            seen_ref[pl.ds(pl.multiple_of(i * SUBLANES, SUBLANES), SUBLANES), :] = jnp.full((SUBLANES, LANES), seen, F32)
            return 0

        lax.fori_loop(0, nb, qblock, 0)

    spec = pl.BlockSpec((S, LANES), lambda h: (0, h))
    seen_spec = pl.BlockSpec((nb * SUBLANES, LANES), lambda h: (0, h))
    return pl.pallas_call(body, grid=(D // LANES,), in_specs=[spec] * 3, out_specs=[spec, spec, seen_spec],
                          out_shape=[jax.ShapeDtypeStruct((S, D), F32)] * 2 + [jax.ShapeDtypeStruct((nb * SUBLANES, D), F32)],
                          scratch_shapes=[pltpu.VMEM((2, S, LANES), BF16)], compiler_params=_cp("parallel"),
                          name=name)(q, k, v)


def attn_bwd(q, k, v, tot, seen, do, dk0, dv0, *, name):
    S, D = q.shape
    nb = S // Q_BLOCK
    tk = _key_step(S)
    nkb = tk // Q_BLOCK
    scale = 1.0 / math.sqrt(HEAD_DIM)
    has_init = dk0 is not None

    def body(*refs):
        if has_init:
            q_ref, k_ref, v_ref, tot_ref, seen_ref, do_ref, dk0_ref, dv0_ref, dq_ref, dk_ref, dv_ref, km_ref = refs
            dk_ref[...] = dk0_ref[...]
            dv_ref[...] = dv0_ref[...]
        else:
            q_ref, k_ref, v_ref, tot_ref, seen_ref, do_ref, dq_ref, dk_ref, dv_ref, km_ref = refs
            dk_ref[...] = jnp.zeros_like(dk_ref)
            dv_ref[...] = jnp.zeros_like(dv_ref)
        heads, row, col, causal = _attn_consts()
        upto = _tri(row <= col)
        before = _tri(row < col)
        for h in range(2):
            km_ref[h] = jnp.where(heads[h], k_ref[...], jnp.zeros_like(k_ref[...]))

        def step(sb, carry, qq, dd, totl, nblk, diag):
            dq, pl_, pg = carry
            c0 = pl.multiple_of(sb * tk, tk)
            keys = pl.ds(c0, nblk * Q_BLOCK)
            z = lax.dot_general(qq, k_ref[keys, :], _DOT_DIMS["nt"], preferred_element_type=F32)
            da = lax.dot_general(dd, v_ref[keys, :], _DOT_DIMS["nt"], preferred_element_type=F32)
            blocks = range(nblk)
            masked = [diag and jb == nblk - 1 for jb in blocks]
            zb = [z[:, jb * Q_BLOCK:(jb + 1) * Q_BLOCK] for jb in blocks]
            lr = [_log1m_beta(zb[jb]) for jb in blocks]
            l = [jnp.where(causal, lr[jb], 0.0) if masked[jb] else lr[jb] for jb in blocks]
            lsum = [_keysum2(l[jb], upto) for jb in blocks]
            lrow = [jnp.sum(l[jb], axis=1, keepdims=True) for jb in blocks]
            a, g = [None] * nblk, [None] * nblk
            for jb in blocks:
                ab = jnp.exp(zb[jb] + lr[jb] + (totl - pl_ - lsum[jb]))
                if masked[jb]:
                    ab = jnp.where(causal, ab, 0.0)
                g[jb] = ab * da[:, jb * Q_BLOCK:(jb + 1) * Q_BLOCK]
                a[jb] = ab.astype(BF16)
                pl_ = pl_ + lrow[jb]
            gsum = [jnp.dot(g[jb].astype(BF16), before, preferred_element_type=F32) for jb in blocks]
            grow = [jnp.sum(g[jb], axis=1, keepdims=True) for jb in blocks]
            dz = [None] * nblk
            for jb in blocks:
                dzb = g[jb] * jnp.exp(lr[jb]) - jnp.exp(zb[jb] + lr[jb]) * (pg + gsum[jb])
                if masked[jb]:
                    dzb = jnp.where(causal, dzb, 0.0)
                dz[jb] = dzb.astype(BF16)
                pg = pg + grow[jb]
            a = jnp.concatenate(a, axis=1)
            dz = jnp.concatenate(dz, axis=1)
            for h in range(2):
                dq = dq + jnp.dot(dz[h * Q_BLOCK:(h + 1) * Q_BLOCK], km_ref[h, keys, :], preferred_element_type=F32)
            dk_ref[keys, :] += lax.dot_general(dz, qq, _DOT_DIMS["tn"], preferred_element_type=F32)
            dv_ref[keys, :] += lax.dot_general(a, dd, _DOT_DIMS["tn"], preferred_element_type=F32)
            return dq, pl_, pg

        def qblock(i, _):
            r0 = pl.multiple_of(i * Q_BLOCK, Q_BLOCK)
            qq = _stack_heads(q_ref[pl.ds(r0, Q_BLOCK), :] * scale, heads)
            dd = _stack_heads(do_ref[pl.ds(r0, Q_BLOCK), :].astype(BF16), heads)
            tot2 = tot_ref[pl.ds(r0, Q_BLOCK), :]
            totl = jnp.concatenate([tot2[:, 0:1], tot2[:, HEAD_DIM:HEAD_DIM + 1]], axis=0)
            last = i // nkb
            zc = jnp.zeros((2 * Q_BLOCK, 1), F32)
            carry = (jnp.zeros((Q_BLOCK, LANES), F32), zc, zc)
            walked = jnp.max(seen_ref[pl.ds(pl.multiple_of(i * SUBLANES, SUBLANES), SUBLANES), :]).astype(jnp.int32)
            first = last - jnp.clip(walked, 0, last)
            carry = lax.fori_loop(first, last, lambda sb, c: step(sb, c, qq, dd, totl, nkb, False), carry)
            carry = lax.switch(i % nkb, [functools.partial(step, last, qq=qq, dd=dd, totl=totl, nblk=m + 1, diag=True)
                                         for m in range(nkb)], carry)
            dq_ref[pl.ds(r0, Q_BLOCK), :] = (carry[0] * scale).astype(dq_ref.dtype)
            return 0

        lax.fori_loop(0, nb, qblock, 0)

    spec = pl.BlockSpec((S, LANES), lambda h: (0, h))
    seen_spec = pl.BlockSpec((nb * SUBLANES, LANES), lambda h: (0, h))
    args = [q, k, v, tot, seen, do] + ([dk0, dv0] if has_init else [])
    return pl.pallas_call(
        body, grid=(D // LANES,), in_specs=[spec] * 4 + [seen_spec] + [spec] * (len(args) - 5), out_specs=[spec] * 3,
        out_shape=[jax.ShapeDtypeStruct((S, D), BF16), jax.ShapeDtypeStruct((S, D), F32), jax.ShapeDtypeStruct((S, D), F32)],
        scratch_shapes=[pltpu.VMEM((2, S, LANES), BF16)], compiler_params=_cp("parallel"), name=name)(*args)


def _dev_index(px, py, pc):
    return 4 * px + 2 * py + pc


def all_gather(bufs):
    nb = len(bufs)

    def body(*refs):
        ins, outs = refs[:nb], refs[nb:2 * nb]
        send_sems, recv_sems, local_sems = refs[2 * nb:]
        x, y, c = lax.axis_index("x"), lax.axis_index("y"), lax.axis_index("c")
        me, sibling = (x, y, c), (x, y, 1 - c)
        chips = [(1 - x, y), (x, 1 - y), (1 - x, 1 - y)]

        def copy(b, k, block, to, from_input=False):
            slot = outs[b].at[_dev_index(*block)]
            return pltpu.make_async_remote_copy(
                src_ref=ins[b] if from_input else slot, dst_ref=slot,
                send_sem=send_sems.at[7 * b + k], recv_sem=recv_sems.at[7 * b + k], device_id=to, device_id_type=MESH)

        mine = [pltpu.make_async_copy(ins[b], outs[b].at[_dev_index(*me)], local_sems.at[b]) for b in range(nb)]
        for cp in mine:
            cp.start()
        first = []
        for b in range(nb):
            first.append(copy(b, 0, me, sibling, from_input=True))
            first += [copy(b, 1 + j, me, (*chip, c), from_input=True) for j, chip in enumerate(chips)]
        for cp in first:
            cp.start()
        passed = []
        for j, chip in enumerate(chips):
            for b in range(nb):
                copy(b, 1 + j, (*chip, c), me).wait_recv()
                fwd = copy(b, 4 + j, (*chip, c), sibling)
                fwd.start()
                passed.append(fwd)
        for b in range(nb):
            copy(b, 0, sibling, me).wait_recv()
            for j, chip in enumerate(chips):
                copy(b, 4 + j, (*chip, 1 - c), me).wait_recv()
        for cp in first + passed:
            cp.wait_send()
        for cp in mine:
            cp.wait()

    any_spec = pl.BlockSpec(memory_space=pl.ANY)
    return pl.pallas_call(
        body, in_specs=[any_spec] * nb, out_specs=[any_spec] * nb,
        out_shape=[jax.ShapeDtypeStruct((N_DEV,) + b.shape, b.dtype) for b in bufs],
        scratch_shapes=[pltpu.SemaphoreType.DMA((7 * nb,)), pltpu.SemaphoreType.DMA((7 * nb,)),
                        pltpu.SemaphoreType.DMA((nb,))],
        name="all_gather_weights")(*bufs)


def _sources(groups):
    return [s for g in groups for (s, _) in g[3]]


def _layout(groups, refs):
    out, si = [], 0
    for g, (_, _, _, lst) in enumerate(groups):
        for (s, off) in lst:
            out.append((g, refs[si], off, s.shape[-2]))
            si += 1
    return out


def pair_exchange(groups):
    srcs = _sources(groups)
    ns, ng = len(srcs), len(groups)

    def body(*refs):
        outs = refs[ns:ns + ng]
        send_sems, recv_sems = refs[ns + ng:]
        x, y, c = lax.axis_index("x"), lax.axis_index("y"), lax.axis_index("c")
        sibling = (x, y, 1 - c)
        for (g, ref, off, r) in _layout(groups, refs[:ns]):
            for q in range(N_DEV // 2):
                pltpu.make_async_remote_copy(
                    src_ref=ref.at[2 * q + 1 - c], dst_ref=outs[g].at[q, pl.ds(off, r)], send_sem=send_sems.at[g],
                    recv_sem=recv_sems.at[g], device_id=sibling, device_id_type=MESH).start()
        whole = [pltpu.make_async_remote_copy(
            src_ref=outs[g], dst_ref=outs[g], send_sem=send_sems.at[g], recv_sem=recv_sems.at[g],
            device_id=sibling, device_id_type=MESH) for g in range(ng)]
        for w in whole:
            w.wait_recv()
        for w in whole:
            w.wait_send()

    any_spec = pl.BlockSpec(memory_space=pl.ANY)
    return pl.pallas_call(
        body, in_specs=[any_spec] * ns, out_specs=[any_spec] * ng,
        out_shape=[jax.ShapeDtypeStruct((N_DEV // 2, r, w), dt) for (r, w, dt, _) in groups],
        scratch_shapes=[pltpu.SemaphoreType.DMA((ng,)), pltpu.SemaphoreType.DMA((ng,))],
        name="pair_exchange")(*srcs)


def pair_sum(src, got, off, *, name):
    _, r, W = src.shape
    tr = _row_tile(r, off, 256)
    o = off // tr

    def body(s_ref, g_ref, o_ref):
        c = lax.axis_index("c")
        o_ref[...] = (s_ref[c].astype(F32) + g_ref[...].astype(F32)).astype(o_ref.dtype)

    return pl.pallas_call(
        body, grid=(N_DEV // 2, r // tr),
        in_specs=[pl.BlockSpec((None, 2, tr, W), lambda q, i: (q, 0, i, 0)),
                  pl.BlockSpec((None, tr, W), lambda q, i: (q, i + o, 0))],
        out_specs=pl.BlockSpec((None, tr, W), lambda q, i: (q, i, 0)),
        out_shape=jax.ShapeDtypeStruct((N_DEV // 2, r, W), src.dtype), compiler_params=_cp("parallel", "parallel"),
        name=name)(src.reshape(N_DEV // 2, 2, r, W), got)


def chip_exchange(groups):
    srcs = _sources(groups)
    ns, ng = len(srcs), len(groups)

    def body(*refs):
        outs = refs[ns:ns + ng]
        send_sems, recv_sems, local_sems = refs[ns + ng:]
        x, y, c = lax.axis_index("x"), lax.axis_index("y"), lax.axis_index("c")
        me = 2 * x + y
        layout = _layout(groups, refs[:ns])
        mine = [pltpu.make_async_copy(ref.at[me], outs[g].at[me, pl.ds(off, r)], local_sems.at[i])
                for i, (g, ref, off, r) in enumerate(layout)]
        for cp in mine:
            cp.start()
        slots = []
        for flip in range(1, N_DEV // 2):
            px, py = (1 - x if flip & 2 else x), (1 - y if flip & 1 else y)
            peer, pq = (px, py, c), 2 * px + py
            for (g, ref, off, r) in layout:
                k = 3 * g + flip - 1
                pltpu.make_async_remote_copy(
                    src_ref=ref.at[pq], dst_ref=outs[g].at[me, pl.ds(off, r)], send_sem=send_sems.at[k],
                    recv_sem=recv_sems.at[k], device_id=peer, device_id_type=MESH).start()
            for g in range(ng):
                k = 3 * g + flip - 1
                slots.append(pltpu.make_async_remote_copy(
                    src_ref=outs[g].at[pq], dst_ref=outs[g].at[pq], send_sem=send_sems.at[k],
                    recv_sem=recv_sems.at[k], device_id=peer, device_id_type=MESH))
        for w in slots:
            w.wait_recv()
        for w in slots:
            w.wait_send()
        for cp in mine:
            cp.wait()

    any_spec = pl.BlockSpec(memory_space=pl.ANY)
    return pl.pallas_call(
        body, in_specs=[any_spec] * ns, out_specs=[any_spec] * ng,
        out_shape=[jax.ShapeDtypeStruct((N_DEV // 2, r, w), dt) for (r, w, dt, _) in groups],
        scratch_shapes=[pltpu.SemaphoreType.DMA((3 * ng,)), pltpu.SemaphoreType.DMA((3 * ng,)),
                        pltpu.SemaphoreType.DMA((ns,))],
        name="chip_exchange")(*srcs)


def _row_tile(rows, off, target):
    for t in (512, 256, 128, 64, 32, 16, 8):
        if t <= target and rows % t == 0 and off % t == 0:
            return t
    raise ValueError((rows, off))


def adamw(recv, off, w, m, v, *, name):
    rows, W = w.shape
    nslot = recv.shape[0]
    tr = _row_tile(rows, off, 128)
    o = off // tr
    c1 = 1.0 - ADAM_B1 ** ADAM_STEP
    c2 = 1.0 - ADAM_B2 ** ADAM_STEP

    def body(r_ref, w_ref, m_ref, v_ref, g_ref, d_ref, mo_ref, vo_ref):
        g = r_ref[0].astype(F32)
        for j in range(1, nslot):
            g = g + r_ref[j].astype(F32)
        mn = ADAM_B1 * m_ref[...] + (1.0 - ADAM_B1) * g
        vn = ADAM_B2 * v_ref[...] + (1.0 - ADAM_B2) * (g * g)
        g_ref[...] = g
        mo_ref[...] = mn
        vo_ref[...] = vn
        d_ref[...] = -ADAM_LR * ((mn / c1) / (jnp.sqrt(vn / c2) + ADAM_EPS) + ADAM_WD * w_ref[...])

    spec = pl.BlockSpec((tr, W), lambda i: (i, 0))
    return pl.pallas_call(
        body, grid=(rows // tr,), in_specs=[pl.BlockSpec((nslot, tr, W), lambda i: (0, i + o, 0)), spec, spec, spec],
        out_specs=[spec] * 4, out_shape=[jax.ShapeDtypeStruct((rows, W), F32)] * 4,
        compiler_params=_cp("parallel"), name=name)(recv, w, m, v)


def join_columns(gathered, off, K, *, name):
    _, _, n = gathered.shape
    tr = _row_tile(K, off, 256)
    o = off // tr

    def body(i_ref, o_ref):
        for d in range(N_DEV):
            o_ref[:, d * n:(d + 1) * n] = i_ref[d]

    return pl.pallas_call(
        body, grid=(K // tr,), in_specs=[pl.BlockSpec((N_DEV, tr, n), lambda i: (0, i + o, 0))],
        out_specs=pl.BlockSpec((tr, N_DEV * n), lambda i: (i, 0)),
        out_shape=jax.ShapeDtypeStruct((K, N_DEV * n), gathered.dtype), compiler_params=_cp("parallel"),
        name=name)(gathered)


def split_columns(full, *, name):
    K, N = full.shape
    n = N // N_DEV
    tr = _row_tile(K, 0, 256)

    def body(i_ref, o_ref):
        for d in range(N_DEV):
            o_ref[d] = i_ref[:, d * n:(d + 1) * n].astype(o_ref.dtype)

    return pl.pallas_call(
        body, grid=(K // tr,), in_specs=[pl.BlockSpec((tr, N), lambda i: (i, 0))],
        out_specs=pl.BlockSpec((N_DEV, tr, n), lambda i: (0, i, 0)),
        out_shape=jax.ShapeDtypeStruct((N_DEV, K, n), BF16), compiler_params=_cp("parallel"), name=name)(full)


def _pack(arrs, dtype, row_mult):
    flat = jnp.concatenate([a.reshape(-1).astype(dtype) for a in arrs])
    rows = -(-flat.shape[0] // PACK_W)
    rows = -(-rows // row_mult) * row_mult
    return jnp.pad(flat, (0, rows * PACK_W - flat.shape[0])).reshape(rows, PACK_W)


def _pack_dev(arrs, dtype, row_mult):
    flat = jnp.concatenate([a.reshape(N_DEV, -1).astype(dtype) for a in arrs], axis=1)
    rows = -(-flat.shape[1] // PACK_W)
    rows = -(-rows // row_mult) * row_mult
    return jnp.pad(flat, ((0, 0), (0, rows * PACK_W - flat.shape[1]))).reshape(N_DEV, rows, PACK_W)


def _unpack(buf, shapes):
    lead = buf.shape[:-2]
    flat = buf.reshape(lead + (-1,))
    outs, off = [], 0
    for s in shapes:
        n = math.prod(s)
        outs.append(flat[..., off:off + n].reshape(lead + tuple(s)))
        off += n
    return outs


def _join(g, axis):
    g = jnp.moveaxis(g, 0, axis)
    return g.reshape(g.shape[:axis] + (g.shape[axis] * g.shape[axis + 1],) + g.shape[axis + 2:])


def _split(full, axis):
    s = full.shape
    g = full.reshape(s[:axis] + (N_DEV, s[axis] // N_DEV) + s[axis + 1:])
    return jnp.moveaxis(g, axis, 0)


def kernel(x, p, a_pw1_w, a_pw1_b, a_dw_w, a_dw_b, a_ln_g, a_ln_b, a_pw2_w, a_pw2_b, b_wq, kv_wk, kv_wv, b_wo, ln_mix_g, ln_mix_b, ffn_w_up, ffn_w_gate, ffn_conv_w, ffn_conv_b, ffn_w_down, ple_w_gate, ple_w_proj, ln_ffn_g, ln_ffn_b, loss_target, m_a_pw1_w, m_a_pw1_b, m_a_dw_w, m_a_dw_b, m_a_ln_g, m_a_ln_b, m_a_pw2_w, m_a_pw2_b, m_b_wq, m_kv_wk, m_kv_wv, m_b_wo, m_ln_mix_g, m_ln_mix_b, m_ffn_w_up, m_ffn_w_gate, m_ffn_conv_w, m_ffn_conv_b, m_ffn_w_down, m_ple_w_gate, m_ple_w_proj, m_ln_ffn_g, m_ln_ffn_b, v_a_pw1_w, v_a_pw1_b, v_a_dw_w, v_a_dw_b, v_a_ln_g, v_a_ln_b, v_a_pw2_w, v_a_pw2_b, v_b_wq, v_kv_wk, v_kv_wv, v_b_wo, v_ln_mix_g, v_ln_mix_b, v_ffn_w_up, v_ffn_w_gate, v_ffn_conv_w, v_ffn_conv_b, v_ffn_w_down, v_ple_w_gate, v_ple_w_proj, v_ln_ffn_g, v_ln_ffn_b):
    local = dict(a_pw1_w=a_pw1_w, a_pw1_b=a_pw1_b, a_dw_w=a_dw_w, a_dw_b=a_dw_b, a_ln_g=a_ln_g, a_ln_b=a_ln_b, a_pw2_w=a_pw2_w, a_pw2_b=a_pw2_b, b_wq=b_wq, kv_wk=kv_wk, kv_wv=kv_wv, b_wo=b_wo, ln_mix_g=ln_mix_g, ln_mix_b=ln_mix_b, ffn_w_up=ffn_w_up, ffn_w_gate=ffn_w_gate, ffn_conv_w=ffn_conv_w, ffn_conv_b=ffn_conv_b, ffn_w_down=ffn_w_down, ple_w_gate=ple_w_gate, ple_w_proj=ple_w_proj, ln_ffn_g=ln_ffn_g, ln_ffn_b=ln_ffn_b)
    mom1 = dict(a_pw1_w=m_a_pw1_w, a_pw1_b=m_a_pw1_b, a_dw_w=m_a_dw_w, a_dw_b=m_a_dw_b, a_ln_g=m_a_ln_g, a_ln_b=m_a_ln_b, a_pw2_w=m_a_pw2_w, a_pw2_b=m_a_pw2_b, b_wq=m_b_wq, kv_wk=m_kv_wk, kv_wv=m_kv_wv, b_wo=m_b_wo, ln_mix_g=m_ln_mix_g, ln_mix_b=m_ln_mix_b, ffn_w_up=m_ffn_w_up, ffn_w_gate=m_ffn_w_gate, ffn_conv_w=m_ffn_conv_w, ffn_conv_b=m_ffn_conv_b, ffn_w_down=m_ffn_w_down, ple_w_gate=m_ple_w_gate, ple_w_proj=m_ple_w_proj, ln_ffn_g=m_ln_ffn_g, ln_ffn_b=m_ln_ffn_b)
    mom2 = dict(a_pw1_w=v_a_pw1_w, a_pw1_b=v_a_pw1_b, a_dw_w=v_a_dw_w, a_dw_b=v_a_dw_b, a_ln_g=v_a_ln_g, a_ln_b=v_a_ln_b, a_pw2_w=v_a_pw2_w, a_pw2_b=v_a_pw2_b, b_wq=v_b_wq, kv_wk=v_kv_wk, kv_wv=v_kv_wv, b_wo=v_b_wo, ln_mix_g=v_ln_mix_g, ln_mix_b=v_ln_mix_b, ffn_w_up=v_ffn_w_up, ffn_w_gate=v_ffn_w_gate, ffn_conv_w=v_ffn_conv_w, ffn_conv_b=v_ffn_conv_b, ffn_w_down=v_ffn_w_down, ple_w_gate=v_ple_w_gate, ple_w_proj=v_ple_w_proj, ln_ffn_g=v_ln_ffn_g, ln_ffn_b=v_ln_ffn_b)
    small_names = [n for n, _ in SMALL]
    small_shapes = [local[n].shape for n in small_names]
    repl_shapes = [local[n].shape for n in REPL]

    widths = sorted({local[n].shape[-1] for n, _ in BIG}, reverse=True)
    groups = {w: [n for n, _ in BIG if local[n].shape[-1] == w] for w in widths}
    offset, rows_of = {}, {}
    for w, names in groups.items():
        off = 0
        for n in names:
            offset[n], rows_of[n] = off, math.prod(local[n].shape[:-1])
            off += rows_of[n]
    sends = [jnp.concatenate([local[n].reshape(-1, w).astype(BF16) for n in names]) for w, names in groups.items()]
    gathered = all_gather(sends + [_pack([local[n] for n in small_names], F32, SUBLANES)])
    gath = dict(zip(widths, gathered[:-1]))
    W = {}
    for n, ax in BIG:
        w = local[n].shape[-1]
        nl = local[n].shape[0] if local[n].ndim == 3 else 1
        per = rows_of[n] // nl
        if ax == local[n].ndim - 1:
            W[n] = [join_columns(gath[w], offset[n] + l * per, per, name=f"join_{n}_{l}") for l in range(nl)]
        else:
            W[n] = [gath[w][:, offset[n] + l * per:offset[n] + (l + 1) * per].reshape(N_DEV * per, w) for l in range(nl)]
    for n in ("kv_wk", "kv_wv"):
        W[n] = W[n][0]
    W.update({n: _join(g, ax) for (n, ax), g in zip(SMALL, _unpack(gathered[-1], small_shapes))})
    W.update({n: local[n] for n in REPL})

    xs = x[0]
    S, D = xs.shape
    x_in, r1s, x1s, r2s, us, gps, gs, hhs, pgls, pps = [], [], [], [], [], [], [], [], [], []
    h1s, h2s, h3s, h5s, qs, os_, tots = {}, {}, {}, {}, {}, {}, {}
    kk = vv = None
    for i in range(DEPTH):
        x_in.append(xs)
        if i < N_A:
            h1 = mm(xs, W["a_pw1_w"][i], "nn", bias=W["a_pw1_b"][i][None], name=f"pw1_{i}")
            h2 = glu_fwd(h1, name=f"glu_{i}")
            h3, h5 = conv_ln_silu_fwd(h2, W["a_dw_w"][i], W["a_dw_b"][i][None], W["a_ln_g"][i][None],
                                      W["a_ln_b"][i][None], name=f"dwconv_{i}")
            mix = mm(h5, W["a_pw2_w"][i], "nn", bias=W["a_pw2_b"][i][None], name=f"pw2_{i}")
            h1s[i], h2s[i], h3s[i], h5s[i] = h1, h2, h3, h5
        else:
            j = i - N_A
            if kk is None:
                kk = mm(xs, W["kv_wk"], "nn", out_dtype=BF16, name="proj_k")
                vv = mm(xs, W["kv_wv"], "nn", out_dtype=BF16, name="proj_v")
            q = mm(xs, W["b_wq"][j], "nn", out_dtype=BF16, name=f"proj_q_{i}")
            o, tot, seen = attn_fwd(q, kk, vv, name=f"attn_{i}")
            mix = mm(o, W["b_wo"][j], "nn", name=f"proj_o_{i}")
            qs[i], os_[i], tots[i] = q, o, (tot, seen)
        r1, x1 = res_ln(xs, mix, W["ln_mix_g"][i][None], W["ln_mix_b"][i][None], name=f"ln_mix_{i}")
        u = mm(x1, W["ffn_w_up"][i], "nn", out_dtype=BF16, name=f"ffn_up_{i}")
        gp = mm(x1, W["ffn_w_gate"][i], "nn", out_dtype=BF16, name=f"ffn_gate_{i}")
        g, hh = conv_act_fwd(gp, u, W["ffn_conv_w"][i], W["ffn_conv_b"][i][None], name=f"ffn_conv_{i}")
        f = mm(hh, W["ffn_w_down"][i], "nn", name=f"ffn_down_{i}")
        pgl = mm(x1, W["ple_w_gate"][i], "nn", name=f"ple_gate_{i}")
        pp = mm(p[i, 0], W["ple_w_proj"][i], "nn", name=f"ple_proj_{i}")
        r2, xs = res_ln(x1, f, W["ln_ffn_g"][i][None], W["ln_ffn_b"][i][None], ple=(pgl, pp), name=f"ln_ffn_{i}")
        for lst, val in ((r1s, r1), (x1s, x1), (r2s, r2), (us, u), (gps, gp), (gs, g), (hhs, hh), (pgls, pgl), (pps, pp)):
            lst.append(val)

    dx, loss_part = loss_grad(xs, loss_target[0], name="loss")
    G = {n: [None] * local[n].shape[0] for n in WEIGHTS if n not in ("kv_wk", "kv_wv")}
    dk = dv = None
    for i in reversed(range(DEPTH)):
        x1 = x1s[i]
        dr2, G["ln_ffn_g"][i], G["ln_ffn_b"][i], _ = ln_bwd(r2s[i], W["ln_ffn_g"][i][None], dx, name=f"ln_ffn_bwd_{i}")
        dhh = mm(dr2, W["ffn_w_down"][i], "nt", out_dtype=BF16, name=f"ffn_down_dx_{i}")
        G["ffn_w_down"][i] = mm(hhs[i], dr2, "tn", out_dtype=BF16, name=f"ffn_down_dw_{i}")
        dpp, dpgl = ple_bwd(dr2, pgls[i], pps[i], name=f"ple_bwd_{i}")
        G["ple_w_proj"][i] = mm(p[i, 0], dpp, "tn", out_dtype=BF16, name=f"ple_proj_dw_{i}")
        G["ple_w_gate"][i] = mm(x1, dpgl, "tn", out_dtype=BF16, name=f"ple_gate_dw_{i}")
        du, dg = ffn_act_bwd(dhh, us[i], gs[i], name=f"ffn_act_bwd_{i}")
        dgp = conv_bwd_x(dg, W["ffn_conv_w"][i], out_dtype=BF16, name=f"ffn_conv_dx_{i}")
        G["ffn_conv_w"][i], G["ffn_conv_b"][i] = conv_bwd_w(gps[i], dg, FFN_CONV_W, name=f"ffn_conv_dw_{i}")
        G["ffn_w_up"][i] = mm(x1, du, "tn", out_dtype=BF16, name=f"ffn_up_dw_{i}")
        G["ffn_w_gate"][i] = mm(x1, dgp, "tn", out_dtype=BF16, name=f"ffn_gate_dw_{i}")
        dx1 = mm(du, W["ffn_w_up"][i], "nt", add=dr2, add_scale=DN_ALPHA, name=f"ffn_up_dx_{i}")
        dx1 = mm(dgp, W["ffn_w_gate"][i], "nt", add=dx1, name=f"ffn_gate_dx_{i}")
        dx1 = mm(dpgl, W["ple_w_gate"][i], "nt", add=dx1, name=f"ple_gate_dx_{i}")
        dr1, G["ln_mix_g"][i], G["ln_mix_b"][i], dr1_sum = ln_bwd(r1s[i], W["ln_mix_g"][i][None], dx1, name=f"ln_mix_bwd_{i}")
        if i < N_A:
            G["a_pw2_w"][i] = mm(h5s[i], dr1, "tn", out_dtype=BF16, name=f"pw2_dw_{i}")
            G["a_pw2_b"][i] = dr1_sum
            dh5 = mm(dr1, W["a_pw2_w"][i], "nt", name=f"pw2_dx_{i}")
            dh3, G["a_ln_g"][i], G["a_ln_b"][i] = ln_silu_bwd(h3s[i], W["a_ln_g"][i][None], W["a_ln_b"][i][None], dh5,
                                                             name=f"dwconv_ln_bwd_{i}")
            dh2 = conv_bwd_x(dh3, W["a_dw_w"][i], out_dtype=F32, name=f"dwconv_dx_{i}")
            G["a_dw_w"][i], G["a_dw_b"][i] = conv_bwd_w(h2s[i], dh3, CONV_W, name=f"dwconv_dw_{i}")
            dh1, G["a_pw1_b"][i] = glu_bwd(h1s[i], dh2, name=f"glu_bwd_{i}")
            G["a_pw1_w"][i] = mm(x_in[i], dh1, "tn", out_dtype=BF16, name=f"pw1_dw_{i}")
            dx = mm(dh1, W["a_pw1_w"][i], "nt", add=dr1, add_scale=DN_ALPHA, name=f"pw1_dx_{i}")
        else:
            j = i - N_A
            G["b_wo"][j] = mm(os_[i], dr1, "tn", out_dtype=BF16, name=f"proj_o_dw_{i}")
            do = mm(dr1, W["b_wo"][j], "nt", name=f"proj_o_dx_{i}")
            dq, dk, dv = attn_bwd(qs[i], kk, vv, *tots[i], do, dk, dv, name=f"attn_bwd_{i}")
            G["b_wq"][j] = mm(x_in[i], dq, "tn", out_dtype=BF16, name=f"proj_q_dw_{i}")
            dx = mm(dq, W["b_wq"][j], "nt", add=dr1, add_scale=DN_ALPHA, name=f"proj_q_dx_{i}")
            if j == 0:
                G["kv_wk"] = mm(x_in[i], dk, "tn", out_dtype=BF16, name="proj_k_dw")
                G["kv_wv"] = mm(x_in[i], dv, "tn", out_dtype=BF16, name="proj_v_dw")
                dx = mm(dk, W["kv_wk"], "nt", add=dx, name="proj_k_dx")
                dx = mm(dv, W["kv_wv"], "nt", add=dx, name="proj_v_dx")
    grad_x = dx[None]
    shard_axis = dict(BIG + SMALL)
    for n in small_names + list(REPL):
        full = list(local[n].shape)
        if n in shard_axis:
            full[shard_axis[n]] *= N_DEV
        G[n] = jnp.stack(G[n]).reshape(full)

    n_small = sum(math.prod(s) for s in small_shapes)
    n_repl = sum(math.prod(s) for s in repl_shapes)
    repl_flat = jnp.concatenate([G[n].reshape(-1) for n in REPL] + [loss_part.reshape(-1)[:1]])
    send_small = _pack_dev([_split(G[n], ax) for n, ax in SMALL] + [jnp.broadcast_to(repl_flat, (N_DEV, n_repl + 1))],
                           F32, SUBLANES)
    ex_groups = []
    for w, names in groups.items():
        lst = []
        for n in names:
            layers = G[n] if isinstance(G[n], list) else [G[n]]
            per = rows_of[n] // len(layers)
            for l, g in enumerate(layers):
                if shard_axis[n] == local[n].ndim - 1:
                    src = split_columns(g, name=f"split_{n}_{l}")
                else:
                    src = g.reshape(N_DEV, per, w)
                lst.append((src, offset[n] + l * per))
        ex_groups.append((sum(rows_of[n] for n in names), w, BF16, lst))
    ex_groups.append((send_small.shape[1], PACK_W, F32, [(send_small, 0)]))
    gots = pair_exchange(ex_groups)
    sum_groups = [(rows, w, dt, [(pair_sum(src, got, off, name=f"pair_sum_{gi}_{si}"), off) for si, (src, off) in enumerate(lst)])
                  for gi, ((rows, w, dt, lst), got) in enumerate(zip(ex_groups, gots))]
    recvs = chip_exchange(sum_groups)
    recv = dict(zip(widths, recvs[:-1]))
    recv_small = recvs[-1]

    out = {}
    for n, _ in BIG:
        w = local[n].shape[-1]
        res = adamw(recv[w], offset[n], local[n].reshape(-1, w), mom1[n].reshape(-1, w), mom2[n].reshape(-1, w),
                    name=f"adamw_{n}")
        out[n] = [r.reshape(local[n].shape) for r in res]

    def state(d):
        small = _pack([d[n] for n in small_names] + [d[n] for n in REPL], F32, SUBLANES)
        return jnp.pad(small, ((0, recv_small.shape[1] - small.shape[0]), (0, 0)))

    out_small = adamw(recv_small, 0, state(local), state(mom1), state(mom2), name="adamw_vectors")
    loss = out_small[0].reshape(-1)[n_small + n_repl]
    vecs = [dict(zip(small_names + list(REPL), _unpack(o, small_shapes + repl_shapes))) for o in out_small]
    per_kind = [[out[n][kind] if n in out else vecs[kind][n] for n in WEIGHTS] for kind in range(4)]
    grads, deltas, new_m, new_v = per_kind
    return (loss, grad_x, *grads, *deltas, *new_m, *new_v)
```

```python
import functools
import math

import jax
import jax.numpy as jnp
from jax import lax
from jax.experimental import pallas as pl
from jax.experimental.pallas import tpu as pltpu

F32 = jnp.float32
BF16 = jnp.bfloat16
MESH = pl.DeviceIdType.MESH

N_DEV = 8
DEPTH = 4
N_A = 2
HEAD_DIM = 64
Q_BLOCK = 128
CONV_W = 31
FFN_CONV_W = 3
LN_EPS = 1e-5
DN_ALPHA = (2.0 * DEPTH) ** 0.25
ADAM_LR = 0.001
ADAM_B1 = 0.9
ADAM_B2 = 0.999
ADAM_EPS = 1e-08
ADAM_WD = 0.01
ADAM_STEP = 10

LANES = 128
SUBLANES = 8
PACK_W = 1024
VMEM_LIMIT = 56 * 1024 * 1024

BIG = (("a_pw1_w", 2), ("a_pw2_w", 1), ("b_wq", 1), ("kv_wk", 0), ("kv_wv", 0), ("b_wo", 1),
       ("ffn_w_up", 2), ("ffn_w_gate", 2), ("ffn_w_down", 1), ("ple_w_gate", 1), ("ple_w_proj", 2))
SMALL = (("a_pw1_b", 1), ("a_dw_w", 2), ("a_dw_b", 1), ("a_ln_g", 1), ("a_ln_b", 1), ("a_pw2_b", 1),
         ("ffn_conv_w", 2))
REPL = ("ln_mix_g", "ln_mix_b", "ffn_conv_b", "ln_ffn_g", "ln_ffn_b")
WEIGHTS = ("a_pw1_w", "a_pw1_b", "a_dw_w", "a_dw_b", "a_ln_g", "a_ln_b", "a_pw2_w", "a_pw2_b", "b_wq", "kv_wk",
           "kv_wv", "b_wo", "ln_mix_g", "ln_mix_b", "ffn_w_up", "ffn_w_gate", "ffn_conv_w", "ffn_conv_b",
           "ffn_w_down", "ple_w_gate", "ple_w_proj", "ln_ffn_g", "ln_ffn_b")


def _cp(*sem):
    return pltpu.CompilerParams(dimension_semantics=sem, vmem_limit_bytes=VMEM_LIMIT)


def _pick(dim, target, align=LANES):
    if dim <= target:
        return dim
    t = (target // align) * align
    while t >= align:
        if dim % t == 0:
            return t
        t -= align
    return dim


_DOT_DIMS = {"nn": (((1,), (0,)), ((), ())), "nt": (((1,), (1,)), ((), ())), "tn": (((0,), (0,)), ((), ()))}


def mm(a, b, mode, *, bias=None, add=None, add_scale=1.0, out_dtype=F32, name):
    if mode == "tn":
        K, M = a.shape
    else:
        M, K = a.shape
    N = b.shape[0] if mode == "nt" else b.shape[1]
    tm, tn, tk = _pick(M, 1536 if mode == "tn" else 512), _pick(N, 1536), _pick(K, 1536 if mode == "tn" else 2816)
    nk = K // tk
    dims = _DOT_DIMS[mode]

    def body(*refs):
        a_ref, b_ref = refs[0], refs[1]
        pos = 2
        bias_ref = add_ref = None
        if bias is not None:
            bias_ref = refs[pos]
            pos += 1
        if add is not None:
            add_ref = refs[pos]
            pos += 1
        o_ref, acc_ref = refs[pos], refs[pos + 1]
        k = pl.program_id(2)

        @pl.when(k == 0)
        def _():
            acc_ref[...] = jnp.zeros_like(acc_ref)

        acc_ref[...] += lax.dot_general(a_ref[...].astype(BF16), b_ref[...].astype(BF16), dims,
                                        preferred_element_type=F32)

        @pl.when(k == nk - 1)
        def _():
            r = acc_ref[...]
            if bias_ref is not None:
                r = r + bias_ref[...]
            if add_ref is not None:
                r = r + add_scale * add_ref[...].astype(F32)
            o_ref[...] = r.astype(o_ref.dtype)

    a_spec = pl.BlockSpec((tk, tm), lambda j, i, k: (k, i)) if mode == "tn" else pl.BlockSpec((tm, tk), lambda j, i, k: (i, k))
    b_spec = pl.BlockSpec((tn, tk), lambda j, i, k: (j, k)) if mode == "nt" else pl.BlockSpec((tk, tn), lambda j, i, k: (k, j))
    in_specs, args = [a_spec, b_spec], [a, b]
    if bias is not None:
        in_specs.append(pl.BlockSpec((1, tn), lambda j, i, k: (0, j)))
        args.append(bias)
    if add is not None:
        in_specs.append(pl.BlockSpec((tm, tn), lambda j, i, k: (i, j)))
        args.append(add)
    return pl.pallas_call(
        body, grid=(N // tn, M // tm, nk), in_specs=in_specs,
        out_specs=pl.BlockSpec((tm, tn), lambda j, i, k: (i, j)),
        out_shape=jax.ShapeDtypeStruct((M, N), out_dtype),
        scratch_shapes=[pltpu.VMEM((tm, tn), F32)],
        compiler_params=_cp("parallel", "parallel", "arbitrary"), name=name)(*args)


def _rows(body, *, n_rows, tm, row_ins, full_ins=(), row_outs=(), acc_outs=(), scratch=(), reverse=False, name):
    n = n_rows // tm

    def rmap(i):
        return (n - 1 - i, 0) if reverse else (i, 0)

    in_specs = [pl.BlockSpec((tm, a.shape[1]), rmap) for a in row_ins]
    in_specs += [pl.BlockSpec(a.shape, lambda i, nd=a.ndim: (0,) * nd) for a in full_ins]
    out_shape = [jax.ShapeDtypeStruct((n_rows, w), dt) for (w, dt) in row_outs]
    out_shape += [jax.ShapeDtypeStruct(s, dt) for (s, dt) in acc_outs]
    out_specs = [pl.BlockSpec((tm, w), rmap) for (w, dt) in row_outs]
    out_specs += [pl.BlockSpec(s, lambda i, nd=len(s): (0,) * nd) for (s, dt) in acc_outs]
    return pl.pallas_call(
        functools.partial(body, n), grid=(n,), in_specs=in_specs, out_specs=out_specs, out_shape=out_shape,
        scratch_shapes=list(scratch), compiler_params=_cp("arbitrary"), name=name)(*row_ins, *full_ins)


def _sigmoid(x):
    return 1.0 / (1.0 + jnp.exp(-x))


def _ln_hat(r):
    mu = jnp.mean(r, axis=-1, keepdims=True)
    xc = r - mu
    var = jnp.mean(xc * xc, axis=-1, keepdims=True)
    rstd = lax.rsqrt(var + LN_EPS)
    return xc * rstd, rstd


def _ln_back(xhat, rstd, g, dy):
    dxh = dy * g
    m1 = jnp.mean(dxh, axis=-1, keepdims=True)
    m2 = jnp.mean(dxh * xhat, axis=-1, keepdims=True)
    return rstd * (dxh - m1 - xhat * m2)


def _colsum(x):
    return jnp.sum(x, axis=0, keepdims=True)


def _acc(i, ref, val):
    @pl.when(i == 0)
    def _():
        ref[...] = val

    @pl.when(i > 0)
    def _():
        ref[...] += val


def res_ln(x, mix, g, b, *, ple=None, name):
    S, D = x.shape

    def body(n, *refs):
        if ple is None:
            x_ref, m_ref, g_ref, b_ref, r_ref, y_ref = refs
            r = DN_ALPHA * x_ref[...] + m_ref[...]
        else:
            x_ref, m_ref, pgl_ref, pp_ref, g_ref, b_ref, r_ref, y_ref = refs
            r = DN_ALPHA * x_ref[...] + m_ref[...] + _sigmoid(pgl_ref[...]) * pp_ref[...]
        xhat, _ = _ln_hat(r)
        r_ref[...] = r
        y_ref[...] = xhat * g_ref[...] + b_ref[...]

    row_ins = [x, mix] + ([] if ple is None else list(ple))
    return _rows(body, n_rows=S, tm=_pick(S, 256, SUBLANES), row_ins=row_ins, full_ins=[g, b],
                 row_outs=[(D, F32), (D, F32)], name=name)


def ln_bwd(r, g, dy, *, name):
    S, D = r.shape

    def body(n, r_ref, dy_ref, g_ref, dr_ref, dg_ref, db_ref, ds_ref):
        i = pl.program_id(0)
        xhat, rstd = _ln_hat(r_ref[...])
        dy_v = dy_ref[...]
        dr = _ln_back(xhat, rstd, g_ref[...], dy_v)
        dr_ref[...] = dr
        _acc(i, dg_ref, _colsum(dy_v * xhat))
        _acc(i, db_ref, _colsum(dy_v))
        _acc(i, ds_ref, _colsum(dr))

    return _rows(body, n_rows=S, tm=_pick(S, 256, SUBLANES), row_ins=[r, dy], full_ins=[g],
                 row_outs=[(D, F32)], acc_outs=[((1, D), F32)] * 3, name=name)


def glu_fwd(h1, *, name):
    S, D2 = h1.shape
    D = D2 // 2

    def body(n, h_ref, o_ref):
        o_ref[...] = h_ref[:, :D] * _sigmoid(h_ref[:, D:])

    return _rows(body, n_rows=S, tm=_pick(S, 256, SUBLANES), row_ins=[h1], row_outs=[(D, F32)], name=name)[0]


def glu_bwd(h1, dh2, *, name):
    S, D2 = h1.shape
    D = D2 // 2

    def body(n, h_ref, d_ref, o_ref, s_ref):
        i = pl.program_id(0)
        a, sg, d = h_ref[:, :D], _sigmoid(h_ref[:, D:]), d_ref[...]
        da = d * sg
        dg = d * a * sg * (1.0 - sg)
        o_ref[:, :D] = da.astype(o_ref.dtype)
        o_ref[:, D:] = dg.astype(o_ref.dtype)
        _acc(i, s_ref, jnp.concatenate([_colsum(da), _colsum(dg)], axis=1))

    return _rows(body, n_rows=S, tm=_pick(S, 256, SUBLANES), row_ins=[h1, dh2], row_outs=[(D2, BF16)],
                 acc_outs=[((1, D2), F32)], name=name)


CONV_ROWS = 32
CONV_LANES = 256


def _halo(k):
    return -(-(k - 1) // SUBLANES) * SUBLANES


def _phases(offs):
    return sorted({o % SUBLANES for o in offs} - {0})


def _shift_scratch(offs, n_rows, width):
    return pltpu.VMEM((max(len(_phases(offs)), 1), n_rows, width), F32)


def _make_shifted(buf_ref, sh_ref, offs):
    n = buf_ref.shape[0] - SUBLANES
    for p, b in enumerate(_phases(offs)):
        sh_ref[p, pl.ds(0, n), :] = buf_ref[pl.ds(b, n), :]


def _tap(buf_ref, sh_ref, offs, k, rc, rows, lc, lw):
    b = offs[k] % SUBLANES
    src = buf_ref if b == 0 else sh_ref.at[_phases(offs).index(b)]
    return src[pl.ds(offs[k] - b + rc, rows), pl.ds(lc, lw)]


def _conv_taps(buf_ref, sh_ref, w_ref, offs, tm, width, emit):
    _make_shifted(buf_ref, sh_ref, offs)
    rows = min(CONV_ROWS, tm)
    for lc in range(0, width, CONV_LANES):
        lw = min(CONV_LANES, width - lc)
        for rc in range(0, tm, rows):
            acc = None
            for k in range(len(offs)):
                t = _tap(buf_ref, sh_ref, offs, k, rc, rows, lc, lw) * w_ref[pl.ds(k, 1), pl.ds(lc, lw)]
                acc = t if acc is None else acc + t
            emit(rc, lc, lw, rows, acc)


def _fill_causal(i, buf_ref, x_ref, halo, tm):
    @pl.when(i == 0)
    def _():
        buf_ref[pl.ds(0, halo), :] = jnp.zeros((halo, buf_ref.shape[1]), F32)

    @pl.when(i > 0)
    def _():
        buf_ref[pl.ds(0, halo), :] = buf_ref[pl.ds(tm, halo), :]

    buf_ref[pl.ds(halo, tm), :] = x_ref[...].astype(F32)


def conv_ln_silu_fwd(x, w, b, g, beta, *, name):
    S, C = x.shape
    K = w.shape[0]
    halo = _halo(K)
    tm = _pick(S, 256, SUBLANES)
    offs = [halo - (K - 1) + k for k in range(K)]

    def body(n, x_ref, w_ref, b_ref, g_ref, beta_ref, h3_ref, h5_ref, buf_ref, sh_ref):
        i = pl.program_id(0)
        _fill_causal(i, buf_ref, x_ref, halo, tm)

        def emit(rc, lc, lw, rows, acc):
            h3_ref[pl.ds(rc, rows), pl.ds(lc, lw)] = acc + b_ref[:, pl.ds(lc, lw)]

        _conv_taps(buf_ref, sh_ref, w_ref, offs, tm, C, emit)
        xhat, _ = _ln_hat(h3_ref[...])
        h4 = xhat * g_ref[...] + beta_ref[...]
        h5_ref[...] = (h4 * _sigmoid(h4)).astype(h5_ref.dtype)

    return _rows(body, n_rows=S, tm=tm, row_ins=[x], full_ins=[w, b, g, beta], row_outs=[(C, F32), (C, BF16)],
                 scratch=[pltpu.VMEM((tm + halo, C), F32), _shift_scratch(offs, tm + halo, C)], name=name)


def conv_act_fwd(gp, u, w, b, *, name):
    S, C = gp.shape
    K = w.shape[0]
    halo = _halo(K)
    tm = _pick(S, 256, SUBLANES)
    offs = [halo - (K - 1) + k for k in range(K)]

    def body(n, x_ref, u_ref, w_ref, b_ref, g_ref, hh_ref, buf_ref, sh_ref):
        i = pl.program_id(0)
        _fill_causal(i, buf_ref, x_ref, halo, tm)

        def emit(rc, lc, lw, rows, acc):
            gv = acc + b_ref[:, pl.ds(lc, lw)]
            g_ref[pl.ds(rc, rows), pl.ds(lc, lw)] = gv.astype(g_ref.dtype)
            hh_ref[pl.ds(rc, rows), pl.ds(lc, lw)] = (gv * _sigmoid(gv) * u_ref[pl.ds(rc, rows), pl.ds(lc, lw)].astype(F32)).astype(hh_ref.dtype)

        _conv_taps(buf_ref, sh_ref, w_ref, offs, tm, C, emit)

    return _rows(body, n_rows=S, tm=tm, row_ins=[gp, u], full_ins=[w, b], row_outs=[(C, BF16), (C, BF16)],
                 scratch=[pltpu.VMEM((tm + halo, C), F32), _shift_scratch(offs, tm + halo, C)], name=name)


def conv_bwd_x(dy, w, *, out_dtype, name):
    S, C = dy.shape
    K = w.shape[0]
    halo = _halo(K)
    tm = _pick(S, 256, SUBLANES)
    offs = [K - 1 - k for k in range(K)]

    def body(n, dy_ref, w_ref, dx_ref, buf_ref, sh_ref):
        i = pl.program_id(0)

        @pl.when(i == 0)
        def _():
            buf_ref[pl.ds(tm, halo), :] = jnp.zeros((halo, C), F32)

        @pl.when(i > 0)
        def _():
            buf_ref[pl.ds(tm, halo), :] = buf_ref[pl.ds(0, halo), :]

        buf_ref[pl.ds(0, tm), :] = dy_ref[...].astype(F32)

        def emit(rc, lc, lw, rows, acc):
            dx_ref[pl.ds(rc, rows), pl.ds(lc, lw)] = acc.astype(dx_ref.dtype)

        _conv_taps(buf_ref, sh_ref, w_ref, offs, tm, C, emit)

    return _rows(body, n_rows=S, tm=tm, row_ins=[dy], full_ins=[w], row_outs=[(C, out_dtype)],
                 scratch=[pltpu.VMEM((tm + halo, C), F32), _shift_scratch(offs, tm + halo, C)], reverse=True, name=name)[0]


def conv_bwd_w(x, dy, K, *, name):
    S, C = x.shape
    halo = _halo(K)
    tm = _pick(S, 256, SUBLANES)
    offs = [halo - (K - 1) + k for k in range(K)]
    rows = min(CONV_ROWS, tm)

    def body(n, x_ref, dy_ref, dw_ref, db_ref, buf_ref, acc_ref, sh_ref):
        i = pl.program_id(0)
        _fill_causal(i, buf_ref, x_ref, halo, tm)
        _make_shifted(buf_ref, sh_ref, offs)

        @pl.when(i == 0)
        def _():
            acc_ref[...] = jnp.zeros_like(acc_ref)

        for lc in range(0, C, CONV_LANES):
            lw = min(CONV_LANES, C - lc)
            for k in range(K):
                s = None
                for rc in range(0, tm, rows):
                    t = dy_ref[pl.ds(rc, rows), pl.ds(lc, lw)].astype(F32) * _tap(buf_ref, sh_ref, offs, k, rc, rows, lc, lw)
                    s = t if s is None else s + t
                s8 = s[0:SUBLANES]
                for q in range(1, rows // SUBLANES):
                    s8 = s8 + s[q * SUBLANES:(q + 1) * SUBLANES]
                acc_ref[pl.ds(k * SUBLANES, SUBLANES), pl.ds(lc, lw)] += s8
        _acc(i, db_ref, _colsum(dy_ref[...].astype(F32)))

        @pl.when(i == n - 1)
        def _():
            for k in range(K):
                dw_ref[pl.ds(k, 1), :] = _colsum(acc_ref[pl.ds(k * SUBLANES, SUBLANES), :])

    return _rows(body, n_rows=S, tm=tm, row_ins=[x, dy], acc_outs=[((K, C), F32), ((1, C), F32)],
                 scratch=[pltpu.VMEM((tm + halo, C), F32), pltpu.VMEM((K * SUBLANES, C), F32),
                          _shift_scratch(offs, tm + halo, C)], name=name)


def ln_silu_bwd(h3, g, beta, dh5, *, name):
    S, C = h3.shape

    def body(n, h_ref, d_ref, g_ref, beta_ref, dh_ref, dg_ref, db_ref):
        i = pl.program_id(0)
        xhat, rstd = _ln_hat(h_ref[...])
        h4 = xhat * g_ref[...] + beta_ref[...]
        sg = _sigmoid(h4)
        dh4 = d_ref[...] * sg * (1.0 + h4 * (1.0 - sg))
        dh_ref[...] = _ln_back(xhat, rstd, g_ref[...], dh4)
        _acc(i, dg_ref, _colsum(dh4 * xhat))
        _acc(i, db_ref, _colsum(dh4))

    return _rows(body, n_rows=S, tm=_pick(S, 256, SUBLANES), row_ins=[h3, dh5], full_ins=[g, beta],
                 row_outs=[(C, F32)], acc_outs=[((1, C), F32)] * 2, name=name)


def ffn_act_bwd(dhh, u, g, *, name):
    S, C = u.shape

    def body(n, d_ref, u_ref, g_ref, du_ref, dg_ref):
        d, gv = d_ref[...].astype(F32), g_ref[...].astype(F32)
        sg = _sigmoid(gv)
        du_ref[...] = (d * gv * sg).astype(du_ref.dtype)
        dg_ref[...] = (d * u_ref[...].astype(F32) * sg * (1.0 + gv * (1.0 - sg))).astype(dg_ref.dtype)

    return _rows(body, n_rows=S, tm=_pick(S, 256, SUBLANES), row_ins=[dhh, u, g], row_outs=[(C, BF16), (C, BF16)],
                 name=name)


def ple_bwd(dr, pgl, pp, *, name):
    S, D = dr.shape

    def body(n, d_ref, l_ref, p_ref, dpp_ref, dpl_ref):
        d, sg = d_ref[...], _sigmoid(l_ref[...])
        dpp_ref[...] = (d * sg).astype(dpp_ref.dtype)
        dpl_ref[...] = (d * p_ref[...] * sg * (1.0 - sg)).astype(dpl_ref.dtype)

    return _rows(body, n_rows=S, tm=_pick(S, 256, SUBLANES), row_ins=[dr, pgl, pp], row_outs=[(D, BF16), (D, BF16)],
                 name=name)


def loss_grad(y, target, *, name):
    S, D = y.shape

    def body(n, y_ref, t_ref, dy_ref, l_ref):
        i = pl.program_id(0)
        e = y_ref[...] - t_ref[...]
        dy_ref[...] = e * (1.0 / D)
        s = jnp.sum(_colsum(e * e), axis=1, keepdims=True) * (0.5 / D)
        _acc(i, l_ref, jnp.broadcast_to(s, (1, LANES)))

    return _rows(body, n_rows=S, tm=_pick(S, 256, SUBLANES), row_ins=[y, target], row_outs=[(D, F32)],
                 acc_outs=[((1, LANES), F32)], name=name)


def _key_step(S):
    return min(512, S // 2)


EXIT_LOG = -110.0


def _attn_consts():
    lane = lax.broadcasted_iota(jnp.int32, (1, LANES), 1)
    heads = (lane < HEAD_DIM, lane >= HEAD_DIM)
    row = lax.broadcasted_iota(jnp.int32, (Q_BLOCK, Q_BLOCK), 0)
    col = lax.broadcasted_iota(jnp.int32, (Q_BLOCK, Q_BLOCK), 1)
    causal = jnp.concatenate([col < row] * 2, axis=0)
    return heads, row, col, causal


def _tri(cond):
    return jnp.where(cond, 1.0, 0.0).astype(BF16)


def _keysum2(x, tri):
    hi = x.astype(BF16)
    lo = (x - hi.astype(F32)).astype(BF16)
    return jnp.dot(jnp.concatenate([hi, lo], axis=1), jnp.concatenate([tri, tri], axis=0),
                   preferred_element_type=F32)


def _stack_heads(x, heads):
    return jnp.concatenate([jnp.where(m, x, jnp.zeros_like(x)) for m in heads], axis=0)


def _log1m_beta(z):
    return -(jnp.maximum(z, 0.0) + jnp.log(1.0 + jnp.exp(-jnp.abs(z))))


def attn_fwd(q, k, v, *, name):
    S, D = q.shape
    nb = S // Q_BLOCK
    tk = _key_step(S)
    nkb = tk // Q_BLOCK
    scale = 1.0 / math.sqrt(HEAD_DIM)

    def body(q_ref, k_ref, v_ref, o_ref, tot_ref, seen_ref, vm_ref):
        heads, row, col, causal = _attn_consts()
        above = _tri(row > col)
        for h in range(2):
            vm_ref[h] = jnp.where(heads[h], v_ref[...], jnp.zeros_like(v_ref[...]))

        def step(sb, carry, qq, nblk, diag):
            acc, cl = carry
            c0 = pl.multiple_of(sb * tk, tk)
            z = lax.dot_general(qq, k_ref[pl.ds(c0, nblk * Q_BLOCK), :], _DOT_DIMS["nt"], preferred_element_type=F32)
            zl, es, rs = [], [], []
            for jb in range(nblk):
                zb = z[:, jb * Q_BLOCK:(jb + 1) * Q_BLOCK]
                lr = _log1m_beta(zb)
                l = jnp.where(causal, lr, 0.0) if diag and jb == nblk - 1 else lr
                zl.append(zb + lr)
                es.append(_keysum2(l, above))
                rs.append(jnp.sum(l, axis=1, keepdims=True))
            a = [None] * nblk
            for jb in reversed(range(nblk)):
                ab = jnp.exp(zl[jb] + es[jb] + cl)
                if diag and jb == nblk - 1:
                    ab = jnp.where(causal, ab, 0.0)
                a[jb] = ab.astype(BF16)
                cl = cl + rs[jb]
            a = jnp.concatenate(a, axis=1)
            for h in range(2):
                acc = acc + jnp.dot(a[h * Q_BLOCK:(h + 1) * Q_BLOCK], vm_ref[h, pl.ds(c0, nblk * Q_BLOCK), :],
                                    preferred_element_type=F32)
            return acc, cl

        def qblock(i, _):
            r0 = pl.multiple_of(i * Q_BLOCK, Q_BLOCK)
            qq = _stack_heads(q_ref[pl.ds(r0, Q_BLOCK), :] * scale, heads)
            last = i // nkb
            carry = (jnp.zeros((Q_BLOCK, LANES), F32), jnp.zeros((2 * Q_BLOCK, 1), F32))
            carry = lax.switch(i % nkb, [functools.partial(step, last, qq=qq, nblk=m + 1, diag=True) for m in range(nkb)],
                               carry)

            def more(c):
                return jnp.logical_and(c[0] < last, jnp.max(c[2]) >= EXIT_LOG)

            def left(c):
                return (c[0] + 1, *step(last - 1 - c[0], c[1:], qq, nkb, False))

            seen, acc, cl = lax.while_loop(more, left, (jnp.int32(0), *carry))
            o_ref[pl.ds(r0, Q_BLOCK), :] = acc
            tot_ref[pl.ds(r0, Q_BLOCK), :] = jnp.where(heads[0], cl[:Q_BLOCK], cl[Q_BLOCK:])
            seen_ref[pl.ds(pl.multiple_of(i * SUBLANES, SUBLANES), SUBLANES), :] = jnp.full((SUBLANES, LANES), seen, F32)
            return 0

        lax.fori_loop(0, nb, qblock, 0)

    spec = pl.BlockSpec((S, LANES), lambda h: (0, h))
    seen_spec = pl.BlockSpec((nb * SUBLANES, LANES), lambda h: (0, h))
    return pl.pallas_call(body, grid=(D // LANES,), in_specs=[spec] * 3, out_specs=[spec, spec, seen_spec],
                          out_shape=[jax.ShapeDtypeStruct((S, D), F32)] * 2 + [jax.ShapeDtypeStruct((nb * SUBLANES, D), F32)],
                          scratch_shapes=[pltpu.VMEM((2, S, LANES), BF16)], compiler_params=_cp("parallel"),
                          name=name)(q, k, v)


def attn_bwd(q, k, v, tot, seen, do, dk0, dv0, *, name):
    S, D = q.shape
    nb = S // Q_BLOCK
    tk = _key_step(S)
    nkb = tk // Q_BLOCK
    scale = 1.0 / math.sqrt(HEAD_DIM)
    has_init = dk0 is not None

    def body(*refs):
        if has_init:
            q_ref, k_ref, v_ref, tot_ref, seen_ref, do_ref, dk0_ref, dv0_ref, dq_ref, dk_ref, dv_ref, km_ref = refs
            dk_ref[...] = dk0_ref[...]
            dv_ref[...] = dv0_ref[...]
        else:
            q_ref, k_ref, v_ref, tot_ref, seen_ref, do_ref, dq_ref, dk_ref, dv_ref, km_ref = refs
            dk_ref[...] = jnp.zeros_like(dk_ref)
            dv_ref[...] = jnp.zeros_like(dv_ref)
        heads, row, col, causal = _attn_consts()
        upto = _tri(row <= col)
        before = _tri(row < col)
        for h in range(2):
            km_ref[h] = jnp.where(heads[h], k_ref[...], jnp.zeros_like(k_ref[...]))

        def step(sb, carry, qq, dd, totl, nblk, diag):
            dq, pl_, pg = carry
            c0 = pl.multiple_of(sb * tk, tk)
            keys = pl.ds(c0, nblk * Q_BLOCK)
            z = lax.dot_general(qq, k_ref[keys, :], _DOT_DIMS["nt"], preferred_element_type=F32)
            da = lax.dot_general(dd, v_ref[keys, :], _DOT_DIMS["nt"], preferred_element_type=F32)
            blocks = range(nblk)
            masked = [diag and jb == nblk - 1 for jb in blocks]
            zb = [z[:, jb * Q_BLOCK:(jb + 1) * Q_BLOCK] for jb in blocks]
            lr = [_log1m_beta(zb[jb]) for jb in blocks]
            l = [jnp.where(causal, lr[jb], 0.0) if masked[jb] else lr[jb] for jb in blocks]
            lsum = [_keysum2(l[jb], upto) for jb in blocks]
            lrow = [jnp.sum(l[jb], axis=1, keepdims=True) for jb in blocks]
            a, g = [None] * nblk, [None] * nblk
            for jb in blocks:
                ab = jnp.exp(zb[jb] + lr[jb] + (totl - pl_ - lsum[jb]))
                if masked[jb]:
                    ab = jnp.where(causal, ab, 0.0)
                g[jb] = ab * da[:, jb * Q_BLOCK:(jb + 1) * Q_BLOCK]
                a[jb] = ab.astype(BF16)
                pl_ = pl_ + lrow[jb]
            gsum = [jnp.dot(g[jb].astype(BF16), before, preferred_element_type=F32) for jb in blocks]
            grow = [jnp.sum(g[jb], axis=1, keepdims=True) for jb in blocks]
            dz = [None] * nblk
            for jb in blocks:
                dzb = g[jb] * jnp.exp(lr[jb]) - jnp.exp(zb[jb] + lr[jb]) * (pg + gsum[jb])
                if masked[jb]:
                    dzb = jnp.where(causal, dzb, 0.0)
                dz[jb] = dzb.astype(BF16)
                pg = pg + grow[jb]
            a = jnp.concatenate(a, axis=1)
            dz = jnp.concatenate(dz, axis=1)
            for h in range(2):
                dq = dq + jnp.dot(dz[h * Q_BLOCK:(h + 1) * Q_BLOCK], km_ref[h, keys, :], preferred_element_type=F32)
            dk_ref[keys, :] += lax.dot_general(dz, qq, _DOT_DIMS["tn"], preferred_element_type=F32)
            dv_ref[keys, :] += lax.dot_general(a, dd, _DOT_DIMS["tn"], preferred_element_type=F32)
            return dq, pl_, pg

        def qblock(i, _):
            r0 = pl.multiple_of(i * Q_BLOCK, Q_BLOCK)
            qq = _stack_heads(q_ref[pl.ds(r0, Q_BLOCK), :] * scale, heads)
            dd = _stack_heads(do_ref[pl.ds(r0, Q_BLOCK), :].astype(BF16), heads)
            tot2 = tot_ref[pl.ds(r0, Q_BLOCK), :]
            totl = jnp.concatenate([tot2[:, 0:1], tot2[:, HEAD_DIM:HEAD_DIM + 1]], axis=0)
            last = i // nkb
            zc = jnp.zeros((2 * Q_BLOCK, 1), F32)
            carry = (jnp.zeros((Q_BLOCK, LANES), F32), zc, zc)
            walked = jnp.max(seen_ref[pl.ds(pl.multiple_of(i * SUBLANES, SUBLANES), SUBLANES), :]).astype(jnp.int32)
            first = last - jnp.clip(walked, 0, last)
            carry = lax.fori_loop(first, last, lambda sb, c: step(sb, c, qq, dd, totl, nkb, False), carry)
            carry = lax.switch(i % nkb, [functools.partial(step, last, qq=qq, dd=dd, totl=totl, nblk=m + 1, diag=True)
                                         for m in range(nkb)], carry)
            dq_ref[pl.ds(r0, Q_BLOCK), :] = (carry[0] * scale).astype(dq_ref.dtype)
            return 0

        lax.fori_loop(0, nb, qblock, 0)

    spec = pl.BlockSpec((S, LANES), lambda h: (0, h))
    seen_spec = pl.BlockSpec((nb * SUBLANES, LANES), lambda h: (0, h))
    args = [q, k, v, tot, seen, do] + ([dk0, dv0] if has_init else [])
    return pl.pallas_call(
        body, grid=(D // LANES,), in_specs=[spec] * 4 + [seen_spec] + [spec] * (len(args) - 5), out_specs=[spec] * 3,
        out_shape=[jax.ShapeDtypeStruct((S, D), BF16), jax.ShapeDtypeStruct((S, D), F32), jax.ShapeDtypeStruct((S, D), F32)],
        scratch_shapes=[pltpu.VMEM((2, S, LANES), BF16)], compiler_params=_cp("parallel"), name=name)(*args)


def _dev_index(px, py, pc):
    return 4 * px + 2 * py + pc


def all_gather(bufs):
    nb = len(bufs)

    def body(*refs):
        ins, outs = refs[:nb], refs[nb:2 * nb]
        send_sems, recv_sems, local_sems = refs[2 * nb:]
        x, y, c = lax.axis_index("x"), lax.axis_index("y"), lax.axis_index("c")
        me, sibling = (x, y, c), (x, y, 1 - c)
        chips = [(1 - x, y), (x, 1 - y), (1 - x, 1 - y)]

        def copy(b, k, block, to, from_input=False):
            slot = outs[b].at[_dev_index(*block)]
            return pltpu.make_async_remote_copy(
                src_ref=ins[b] if from_input else slot, dst_ref=slot,
                send_sem=send_sems.at[7 * b + k], recv_sem=recv_sems.at[7 * b + k], device_id=to, device_id_type=MESH)

        mine = [pltpu.make_async_copy(ins[b], outs[b].at[_dev_index(*me)], local_sems.at[b]) for b in range(nb)]
        for cp in mine:
            cp.start()
        first = []
        for b in range(nb):
            first.append(copy(b, 0, me, sibling, from_input=True))
            first += [copy(b, 1 + j, me, (*chip, c), from_input=True) for j, chip in enumerate(chips)]
        for cp in first:
            cp.start()
        passed = []
        for j, chip in enumerate(chips):
            for b in range(nb):
                copy(b, 1 + j, (*chip, c), me).wait_recv()
                fwd = copy(b, 4 + j, (*chip, c), sibling)
                fwd.start()
                passed.append(fwd)
        for b in range(nb):
            copy(b, 0, sibling, me).wait_recv()
            for j, chip in enumerate(chips):
                copy(b, 4 + j, (*chip, 1 - c), me).wait_recv()
        for cp in first + passed:
            cp.wait_send()
        for cp in mine:
            cp.wait()

    any_spec = pl.BlockSpec(memory_space=pl.ANY)
    return pl.pallas_call(
        body, in_specs=[any_spec] * nb, out_specs=[any_spec] * nb,
        out_shape=[jax.ShapeDtypeStruct((N_DEV,) + b.shape, b.dtype) for b in bufs],
        scratch_shapes=[pltpu.SemaphoreType.DMA((7 * nb,)), pltpu.SemaphoreType.DMA((7 * nb,)),
                        pltpu.SemaphoreType.DMA((nb,))],
        name="all_gather_weights")(*bufs)


def _sources(groups):
    return [s for g in groups for (s, _) in g[3]]


def _layout(groups, refs):
    out, si = [], 0
    for g, (_, _, _, lst) in enumerate(groups):
        for (s, off) in lst:
            out.append((g, refs[si], off, s.shape[-2]))
            si += 1
    return out


def pair_exchange(groups):
    srcs = _sources(groups)
    ns, ng = len(srcs), len(groups)

    def body(*refs):
        outs = refs[ns:ns + ng]
        send_sems, recv_sems = refs[ns + ng:]
        x, y, c = lax.axis_index("x"), lax.axis_index("y"), lax.axis_index("c")
        sibling = (x, y, 1 - c)
        for (g, ref, off, r) in _layout(groups, refs[:ns]):
            for q in range(N_DEV // 2):
                pltpu.make_async_remote_copy(
                    src_ref=ref.at[2 * q + 1 - c], dst_ref=outs[g].at[q, pl.ds(off, r)], send_sem=send_sems.at[g],
                    recv_sem=recv_sems.at[g], device_id=sibling, device_id_type=MESH).start()
        whole = [pltpu.make_async_remote_copy(
            src_ref=outs[g], dst_ref=outs[g], send_sem=send_sems.at[g], recv_sem=recv_sems.at[g],
            device_id=sibling, device_id_type=MESH) for g in range(ng)]
        for w in whole:
            w.wait_recv()
        for w in whole:
            w.wait_send()

    any_spec = pl.BlockSpec(memory_space=pl.ANY)
    return pl.pallas_call(
        body, in_specs=[any_spec] * ns, out_specs=[any_spec] * ng,
        out_shape=[jax.ShapeDtypeStruct((N_DEV // 2, r, w), dt) for (r, w, dt, _) in groups],
        scratch_shapes=[pltpu.SemaphoreType.DMA((ng,)), pltpu.SemaphoreType.DMA((ng,))],
        name="pair_exchange")(*srcs)


def pair_sum(src, got, off, *, name):
    _, r, W = src.shape
    tr = _row_tile(r, off, 1024)
    o = off // tr

    def body(s_ref, g_ref, o_ref):
        c = lax.axis_index("c")
        o_ref[...] = (s_ref[c].astype(F32) + g_ref[...].astype(F32)).astype(o_ref.dtype)

    return pl.pallas_call(
        body, grid=(N_DEV // 2, r // tr),
        in_specs=[pl.BlockSpec((None, 2, tr, W), lambda q, i: (q, 0, i, 0)),
                  pl.BlockSpec((None, tr, W), lambda q, i: (q, i + o, 0))],
        out_specs=pl.BlockSpec((None, tr, W), lambda q, i: (q, i, 0)),
        out_shape=jax.ShapeDtypeStruct((N_DEV // 2, r, W), src.dtype), compiler_params=_cp("parallel", "parallel"),
        name=name)(src.reshape(N_DEV // 2, 2, r, W), got)


def chip_exchange(groups):
    srcs = _sources(groups)
    ns, ng = len(srcs), len(groups)

    def body(*refs):
        outs = refs[ns:ns + ng]
        send_sems, recv_sems, local_sems = refs[ns + ng:]
        x, y, c = lax.axis_index("x"), lax.axis_index("y"), lax.axis_index("c")
        me = 2 * x + y
        layout = _layout(groups, refs[:ns])
        mine = [pltpu.make_async_copy(ref.at[me], outs[g].at[me, pl.ds(off, r)], local_sems.at[i])
                for i, (g, ref, off, r) in enumerate(layout)]
        for cp in mine:
            cp.start()
        slots = []
        for flip in range(1, N_DEV // 2):
            px, py = (1 - x if flip & 2 else x), (1 - y if flip & 1 else y)
            peer, pq = (px, py, c), 2 * px + py
            for (g, ref, off, r) in layout:
                k = 3 * g + flip - 1
                pltpu.make_async_remote_copy(
                    src_ref=ref.at[pq], dst_ref=outs[g].at[me, pl.ds(off, r)], send_sem=send_sems.at[k],
                    recv_sem=recv_sems.at[k], device_id=peer, device_id_type=MESH).start()
            for g in range(ng):
                k = 3 * g + flip - 1
                slots.append(pltpu.make_async_remote_copy(
                    src_ref=outs[g].at[pq], dst_ref=outs[g].at[pq], send_sem=send_sems.at[k],
                    recv_sem=recv_sems.at[k], device_id=peer, device_id_type=MESH))
        for w in slots:
            w.wait_recv()
        for w in slots:
            w.wait_send()
        for cp in mine:
            cp.wait()

    any_spec = pl.BlockSpec(memory_space=pl.ANY)
    return pl.pallas_call(
        body, in_specs=[any_spec] * ns, out_specs=[any_spec] * ng,
        out_shape=[jax.ShapeDtypeStruct((N_DEV // 2, r, w), dt) for (r, w, dt, _) in groups],
        scratch_shapes=[pltpu.SemaphoreType.DMA((3 * ng,)), pltpu.SemaphoreType.DMA((3 * ng,)),
                        pltpu.SemaphoreType.DMA((ns,))],
        name="chip_exchange")(*srcs)


def _row_tile(rows, off, target):
    for t in (1024, 512, 256, 128, 64, 32, 16, 8):
        if t <= target and rows % t == 0 and off % t == 0:
            return t
    raise ValueError((rows, off))


def adamw(recv, off, w, m, v, *, name):
    rows, W = w.shape
    nslot = recv.shape[0]
    tr = _row_tile(rows, off, 256)
    o = off // tr
    c1 = 1.0 - ADAM_B1 ** ADAM_STEP
    c2 = 1.0 - ADAM_B2 ** ADAM_STEP

    def body(r_ref, w_ref, m_ref, v_ref, g_ref, d_ref, mo_ref, vo_ref):
        g = r_ref[0].astype(F32)
        for j in range(1, nslot):
            g = g + r_ref[j].astype(F32)
        mn = ADAM_B1 * m_ref[...] + (1.0 - ADAM_B1) * g
        vn = ADAM_B2 * v_ref[...] + (1.0 - ADAM_B2) * (g * g)
        g_ref[...] = g
        mo_ref[...] = mn
        vo_ref[...] = vn
        d_ref[...] = -ADAM_LR * ((mn / c1) / (jnp.sqrt(vn / c2) + ADAM_EPS) + ADAM_WD * w_ref[...])

    spec = pl.BlockSpec((tr, W), lambda i: (i, 0))
    return pl.pallas_call(
        body, grid=(rows // tr,), in_specs=[pl.BlockSpec((nslot, tr, W), lambda i: (0, i + o, 0)), spec, spec, spec],
        out_specs=[spec] * 4, out_shape=[jax.ShapeDtypeStruct((rows, W), F32)] * 4,
        compiler_params=_cp("parallel"), name=name)(recv, w, m, v)


def join_columns(gathered, off, K, *, name):
    _, _, n = gathered.shape
    tr = _row_tile(K, off, 256)
    o = off // tr

    def body(i_ref, o_ref):
        for d in range(N_DEV):
            o_ref[:, d * n:(d + 1) * n] = i_ref[d]

    return pl.pallas_call(
        body, grid=(K // tr,), in_specs=[pl.BlockSpec((N_DEV, tr, n), lambda i: (0, i + o, 0))],
        out_specs=pl.BlockSpec((tr, N_DEV * n), lambda i: (i, 0)),
        out_shape=jax.ShapeDtypeStruct((K, N_DEV * n), gathered.dtype), compiler_params=_cp("parallel"),
        name=name)(gathered)


def split_columns(full, *, name):
    K, N = full.shape
    n = N // N_DEV
    tr = _row_tile(K, 0, 256)

    def body(i_ref, o_ref):
        for d in range(N_DEV):
            o_ref[d] = i_ref[:, d * n:(d + 1) * n].astype(o_ref.dtype)

    return pl.pallas_call(
        body, grid=(K // tr,), in_specs=[pl.BlockSpec((tr, N), lambda i: (i, 0))],
        out_specs=pl.BlockSpec((N_DEV, tr, n), lambda i: (0, i, 0)),
        out_shape=jax.ShapeDtypeStruct((N_DEV, K, n), BF16), compiler_params=_cp("parallel"), name=name)(full)


def _pack(arrs, dtype, row_mult):
    flat = jnp.concatenate([a.reshape(-1).astype(dtype) for a in arrs])
    rows = -(-flat.shape[0] // PACK_W)
    rows = -(-rows // row_mult) * row_mult
    return jnp.pad(flat, (0, rows * PACK_W - flat.shape[0])).reshape(rows, PACK_W)


def _pack_dev(arrs, dtype, row_mult):
    flat = jnp.concatenate([a.reshape(N_DEV, -1).astype(dtype) for a in arrs], axis=1)
    rows = -(-flat.shape[1] // PACK_W)
    rows = -(-rows // row_mult) * row_mult
    return jnp.pad(flat, ((0, 0), (0, rows * PACK_W - flat.shape[1]))).reshape(N_DEV, rows, PACK_W)


def _unpack(buf, shapes):
    lead = buf.shape[:-2]
    flat = buf.reshape(lead + (-1,))
    outs, off = [], 0
    for s in shapes:
        n = math.prod(s)
        outs.append(flat[..., off:off + n].reshape(lead + tuple(s)))
        off += n
    return outs


def _join(g, axis):
    g = jnp.moveaxis(g, 0, axis)
    return g.reshape(g.shape[:axis] + (g.shape[axis] * g.shape[axis + 1],) + g.shape[axis + 2:])


def _split(full, axis):
    s = full.shape
    g = full.reshape(s[:axis] + (N_DEV, s[axis] // N_DEV) + s[axis + 1:])
    return jnp.moveaxis(g, axis, 0)


def kernel(x, p, a_pw1_w, a_pw1_b, a_dw_w, a_dw_b, a_ln_g, a_ln_b, a_pw2_w, a_pw2_b, b_wq, kv_wk, kv_wv, b_wo, ln_mix_g, ln_mix_b, ffn_w_up, ffn_w_gate, ffn_conv_w, ffn_conv_b, ffn_w_down, ple_w_gate, ple_w_proj, ln_ffn_g, ln_ffn_b, loss_target, m_a_pw1_w, m_a_pw1_b, m_a_dw_w, m_a_dw_b, m_a_ln_g, m_a_ln_b, m_a_pw2_w, m_a_pw2_b, m_b_wq, m_kv_wk, m_kv_wv, m_b_wo, m_ln_mix_g, m_ln_mix_b, m_ffn_w_up, m_ffn_w_gate, m_ffn_conv_w, m_ffn_conv_b, m_ffn_w_down, m_ple_w_gate, m_ple_w_proj, m_ln_ffn_g, m_ln_ffn_b, v_a_pw1_w, v_a_pw1_b, v_a_dw_w, v_a_dw_b, v_a_ln_g, v_a_ln_b, v_a_pw2_w, v_a_pw2_b, v_b_wq, v_kv_wk, v_kv_wv, v_b_wo, v_ln_mix_g, v_ln_mix_b, v_ffn_w_up, v_ffn_w_gate, v_ffn_conv_w, v_ffn_conv_b, v_ffn_w_down, v_ple_w_gate, v_ple_w_proj, v_ln_ffn_g, v_ln_ffn_b):
    local = dict(a_pw1_w=a_pw1_w, a_pw1_b=a_pw1_b, a_dw_w=a_dw_w, a_dw_b=a_dw_b, a_ln_g=a_ln_g, a_ln_b=a_ln_b, a_pw2_w=a_pw2_w, a_pw2_b=a_pw2_b, b_wq=b_wq, kv_wk=kv_wk, kv_wv=kv_wv, b_wo=b_wo, ln_mix_g=ln_mix_g, ln_mix_b=ln_mix_b, ffn_w_up=ffn_w_up, ffn_w_gate=ffn_w_gate, ffn_conv_w=ffn_conv_w, ffn_conv_b=ffn_conv_b, ffn_w_down=ffn_w_down, ple_w_gate=ple_w_gate, ple_w_proj=ple_w_proj, ln_ffn_g=ln_ffn_g, ln_ffn_b=ln_ffn_b)
    mom1 = dict(a_pw1_w=m_a_pw1_w, a_pw1_b=m_a_pw1_b, a_dw_w=m_a_dw_w, a_dw_b=m_a_dw_b, a_ln_g=m_a_ln_g, a_ln_b=m_a_ln_b, a_pw2_w=m_a_pw2_w, a_pw2_b=m_a_pw2_b, b_wq=m_b_wq, kv_wk=m_kv_wk, kv_wv=m_kv_wv, b_wo=m_b_wo, ln_mix_g=m_ln_mix_g, ln_mix_b=m_ln_mix_b, ffn_w_up=m_ffn_w_up, ffn_w_gate=m_ffn_w_gate, ffn_conv_w=m_ffn_conv_w, ffn_conv_b=m_ffn_conv_b, ffn_w_down=m_ffn_w_down, ple_w_gate=m_ple_w_gate, ple_w_proj=m_ple_w_proj, ln_ffn_g=m_ln_ffn_g, ln_ffn_b=m_ln_ffn_b)
    mom2 = dict(a_pw1_w=v_a_pw1_w, a_pw1_b=v_a_pw1_b, a_dw_w=v_a_dw_w, a_dw_b=v_a_dw_b, a_ln_g=v_a_ln_g, a_ln_b=v_a_ln_b, a_pw2_w=v_a_pw2_w, a_pw2_b=v_a_pw2_b, b_wq=v_b_wq, kv_wk=v_kv_wk, kv_wv=v_kv_wv, b_wo=v_b_wo, ln_mix_g=v_ln_mix_g, ln_mix_b=v_ln_mix_b, ffn_w_up=v_ffn_w_up, ffn_w_gate=v_ffn_w_gate, ffn_conv_w=v_ffn_conv_w, ffn_conv_b=v_ffn_conv_b, ffn_w_down=v_ffn_w_down, ple_w_gate=v_ple_w_gate, ple_w_proj=v_ple_w_proj, ln_ffn_g=v_ln_ffn_g, ln_ffn_b=v_ln_ffn_b)
    small_names = [n for n, _ in SMALL]
    small_shapes = [local[n].shape for n in small_names]
    repl_shapes = [local[n].shape for n in REPL]

    widths = sorted({local[n].shape[-1] for n, _ in BIG}, reverse=True)
    groups = {w: [n for n, _ in BIG if local[n].shape[-1] == w] for w in widths}
    offset, rows_of = {}, {}
    for w, names in groups.items():
        off = 0
        for n in names:
            offset[n], rows_of[n] = off, math.prod(local[n].shape[:-1])
            off += rows_of[n]
    sends = [jnp.concatenate([local[n].reshape(-1, w).astype(BF16) for n in names]) for w, names in groups.items()]
    gathered = all_gather(sends + [_pack([local[n] for n in small_names], F32, SUBLANES)])
    gath = dict(zip(widths, gathered[:-1]))
    W = {}
    for n, ax in BIG:
        w = local[n].shape[-1]
        nl = local[n].shape[0] if local[n].ndim == 3 else 1
        per = rows_of[n] // nl
        if ax == local[n].ndim - 1:
            W[n] = [join_columns(gath[w], offset[n] + l * per, per, name=f"join_{n}_{l}") for l in range(nl)]
        else:
            W[n] = [gath[w][:, offset[n] + l * per:offset[n] + (l + 1) * per].reshape(N_DEV * per, w) for l in range(nl)]
    for n in ("kv_wk", "kv_wv"):
        W[n] = W[n][0]
    W.update({n: _join(g, ax) for (n, ax), g in zip(SMALL, _unpack(gathered[-1], small_shapes))})
    W.update({n: local[n] for n in REPL})

    xs = x[0]
    S, D = xs.shape
    x_in, r1s, x1s, r2s, us, gps, gs, hhs, pgls, pps = [], [], [], [], [], [], [], [], [], []
    h1s, h2s, h3s, h5s, qs, os_, tots = {}, {}, {}, {}, {}, {}, {}
    kk = vv = None
    for i in range(DEPTH):
        x_in.append(xs)
        if i < N_A:
            h1 = mm(xs, W["a_pw1_w"][i], "nn", bias=W["a_pw1_b"][i][None], name=f"pw1_{i}")
            h2 = glu_fwd(h1, name=f"glu_{i}")
            h3, h5 = conv_ln_silu_fwd(h2, W["a_dw_w"][i], W["a_dw_b"][i][None], W["a_ln_g"][i][None],
                                      W["a_ln_b"][i][None], name=f"dwconv_{i}")
            mix = mm(h5, W["a_pw2_w"][i], "nn", bias=W["a_pw2_b"][i][None], name=f"pw2_{i}")
            h1s[i], h2s[i], h3s[i], h5s[i] = h1, h2, h3, h5
        else:
            j = i - N_A
            if kk is None:
                kk = mm(xs, W["kv_wk"], "nn", out_dtype=BF16, name="proj_k")
                vv = mm(xs, W["kv_wv"], "nn", out_dtype=BF16, name="proj_v")
            q = mm(xs, W["b_wq"][j], "nn", out_dtype=BF16, name=f"proj_q_{i}")
            o, tot, seen = attn_fwd(q, kk, vv, name=f"attn_{i}")
            mix = mm(o, W["b_wo"][j], "nn", name=f"proj_o_{i}")
            qs[i], os_[i], tots[i] = q, o, (tot, seen)
        r1, x1 = res_ln(xs, mix, W["ln_mix_g"][i][None], W["ln_mix_b"][i][None], name=f"ln_mix_{i}")
        u = mm(x1, W["ffn_w_up"][i], "nn", out_dtype=BF16, name=f"ffn_up_{i}")
        gp = mm(x1, W["ffn_w_gate"][i], "nn", out_dtype=BF16, name=f"ffn_gate_{i}")
        g, hh = conv_act_fwd(gp, u, W["ffn_conv_w"][i], W["ffn_conv_b"][i][None], name=f"ffn_conv_{i}")
        f = mm(hh, W["ffn_w_down"][i], "nn", name=f"ffn_down_{i}")
        pgl = mm(x1, W["ple_w_gate"][i], "nn", name=f"ple_gate_{i}")
        pp = mm(p[i, 0], W["ple_w_proj"][i], "nn", name=f"ple_proj_{i}")
        r2, xs = res_ln(x1, f, W["ln_ffn_g"][i][None], W["ln_ffn_b"][i][None], ple=(pgl, pp), name=f"ln_ffn_{i}")
        for lst, val in ((r1s, r1), (x1s, x1), (r2s, r2), (us, u), (gps, gp), (gs, g), (hhs, hh), (pgls, pgl), (pps, pp)):
            lst.append(val)

    dx, loss_part = loss_grad(xs, loss_target[0], name="loss")
    G = {n: [None] * local[n].shape[0] for n in WEIGHTS if n not in ("kv_wk", "kv_wv")}
    dk = dv = None
    for i in reversed(range(DEPTH)):
        x1 = x1s[i]
        dr2, G["ln_ffn_g"][i], G["ln_ffn_b"][i], _ = ln_bwd(r2s[i], W["ln_ffn_g"][i][None], dx, name=f"ln_ffn_bwd_{i}")
        dhh = mm(dr2, W["ffn_w_down"][i], "nt", out_dtype=BF16, name=f"ffn_down_dx_{i}")
        G["ffn_w_down"][i] = mm(hhs[i], dr2, "tn", out_dtype=BF16, name=f"ffn_down_dw_{i}")
        dpp, dpgl = ple_bwd(dr2, pgls[i], pps[i], name=f"ple_bwd_{i}")
        G["ple_w_proj"][i] = mm(p[i, 0], dpp, "tn", out_dtype=BF16, name=f"ple_proj_dw_{i}")
        G["ple_w_gate"][i] = mm(x1, dpgl, "tn", out_dtype=BF16, name=f"ple_gate_dw_{i}")
        du, dg = ffn_act_bwd(dhh, us[i], gs[i], name=f"ffn_act_bwd_{i}")
        dgp = conv_bwd_x(dg, W["ffn_conv_w"][i], out_dtype=BF16, name=f"ffn_conv_dx_{i}")
        G["ffn_conv_w"][i], G["ffn_conv_b"][i] = conv_bwd_w(gps[i], dg, FFN_CONV_W, name=f"ffn_conv_dw_{i}")
        G["ffn_w_up"][i] = mm(x1, du, "tn", out_dtype=BF16, name=f"ffn_up_dw_{i}")
        G["ffn_w_gate"][i] = mm(x1, dgp, "tn", out_dtype=BF16, name=f"ffn_gate_dw_{i}")
        dx1 = mm(du, W["ffn_w_up"][i], "nt", add=dr2, add_scale=DN_ALPHA, name=f"ffn_up_dx_{i}")
        dx1 = mm(dgp, W["ffn_w_gate"][i], "nt", add=dx1, name=f"ffn_gate_dx_{i}")
        dx1 = mm(dpgl, W["ple_w_gate"][i], "nt", add=dx1, name=f"ple_gate_dx_{i}")
        dr1, G["ln_mix_g"][i], G["ln_mix_b"][i], dr1_sum = ln_bwd(r1s[i], W["ln_mix_g"][i][None], dx1, name=f"ln_mix_bwd_{i}")
        if i < N_A:
            G["a_pw2_w"][i] = mm(h5s[i], dr1, "tn", out_dtype=BF16, name=f"pw2_dw_{i}")
            G["a_pw2_b"][i] = dr1_sum
            dh5 = mm(dr1, W["a_pw2_w"][i], "nt", name=f"pw2_dx_{i}")
            dh3, G["a_ln_g"][i], G["a_ln_b"][i] = ln_silu_bwd(h3s[i], W["a_ln_g"][i][None], W["a_ln_b"][i][None], dh5,
                                                             name=f"dwconv_ln_bwd_{i}")
            dh2 = conv_bwd_x(dh3, W["a_dw_w"][i], out_dtype=F32, name=f"dwconv_dx_{i}")
            G["a_dw_w"][i], G["a_dw_b"][i] = conv_bwd_w(h2s[i], dh3, CONV_W, name=f"dwconv_dw_{i}")
            dh1, G["a_pw1_b"][i] = glu_bwd(h1s[i], dh2, name=f"glu_bwd_{i}")
            G["a_pw1_w"][i] = mm(x_in[i], dh1, "tn", out_dtype=BF16, name=f"pw1_dw_{i}")
            dx = mm(dh1, W["a_pw1_w"][i], "nt", add=dr1, add_scale=DN_ALPHA, name=f"pw1_dx_{i}")
        else:
            j = i - N_A
            G["b_wo"][j] = mm(os_[i], dr1, "tn", out_dtype=BF16, name=f"proj_o_dw_{i}")
            do = mm(dr1, W["b_wo"][j], "nt", name=f"proj_o_dx_{i}")
            dq, dk, dv = attn_bwd(qs[i], kk, vv, *tots[i], do, dk, dv, name=f"attn_bwd_{i}")
            G["b_wq"][j] = mm(x_in[i], dq, "tn", out_dtype=BF16, name=f"proj_q_dw_{i}")
            dx = mm(dq, W["b_wq"][j], "nt", add=dr1, add_scale=DN_ALPHA, name=f"proj_q_dx_{i}")
            if j == 0:
                G["kv_wk"] = mm(x_in[i], dk, "tn", out_dtype=BF16, name="proj_k_dw")
                G["kv_wv"] = mm(x_in[i], dv, "tn", out_dtype=BF16, name="proj_v_dw")
                dx = mm(dk, W["kv_wk"], "nt", add=dx, name="proj_k_dx")
                dx = mm(dv, W["kv_wv"], "nt", add=dx, name="proj_v_dx")
    grad_x = dx[None]
    shard_axis = dict(BIG + SMALL)
    for n in small_names + list(REPL):
        full = list(local[n].shape)
        if n in shard_axis:
            full[shard_axis[n]] *= N_DEV
        G[n] = jnp.stack(G[n]).reshape(full)

    n_small = sum(math.prod(s) for s in small_shapes)
    n_repl = sum(math.prod(s) for s in repl_shapes)
    repl_flat = jnp.concatenate([G[n].reshape(-1) for n in REPL] + [loss_part.reshape(-1)[:1]])
    send_small = _pack_dev([_split(G[n], ax) for n, ax in SMALL] + [jnp.broadcast_to(repl_flat, (N_DEV, n_repl + 1))],
                           F32, SUBLANES)
    ex_groups = []
    for w, names in groups.items():
        lst = []
        for n in names:
            layers = G[n] if isinstance(G[n], list) else [G[n]]
            per = rows_of[n] // len(layers)
            for l, g in enumerate(layers):
                if shard_axis[n] == local[n].ndim - 1:
                    src = split_columns(g, name=f"split_{n}_{l}")
                else:
                    src = g.reshape(N_DEV, per, w)
                lst.append((src, offset[n] + l * per))
        ex_groups.append((sum(rows_of[n] for n in names), w, BF16, lst))
    ex_groups.append((send_small.shape[1], PACK_W, F32, [(send_small, 0)]))
    gots = pair_exchange(ex_groups)
    sum_groups = [(rows, w, dt, [(pair_sum(src, got, off, name=f"pair_sum_{gi}_{si}"), off) for si, (src, off) in enumerate(lst)])
                  for gi, ((rows, w, dt, lst), got) in enumerate(zip(ex_groups, gots))]
    recvs = chip_exchange(sum_groups)
    recv = dict(zip(widths, recvs[:-1]))
    recv_small = recvs[-1]

    out = {}
    for n, _ in BIG:
        w = local[n].shape[-1]
        res = adamw(recv[w], offset[n], local[n].reshape(-1, w), mom1[n].reshape(-1, w), mom2[n].reshape(-1, w),
                    name=f"adamw_{n}")
        out[n] = [r.reshape(local[n].shape) for r in res]

    def state(d):
        small = _pack([d[n] for n in small_names] + [d[n] for n in REPL], F32, SUBLANES)
        return jnp.pad(small, ((0, recv_small.shape[1] - small.shape[0]), (0, 0)))

    out_small = adamw(recv_small, 0, state(local), state(mom1), state(mom2), name="adamw_vectors")
    loss = out_small[0].reshape(-1)[n_small + n_repl]
    vecs = [dict(zip(small_names + list(REPL), _unpack(o, small_shapes + repl_shapes))) for o in out_small]
    per_kind = [[out[n][kind] if n in out else vecs[kind][n] for n in WEIGHTS] for kind in range(4)]
    grads, deltas, new_m, new_v = per_kind
    return (loss, grad_x, *grads, *deltas, *new_m, *new_v)
```

```python
import functools
import math

import jax
import jax.numpy as jnp
from jax import lax
from jax.experimental import pallas as pl
from jax.experimental.pallas import tpu as pltpu

F32 = jnp.float32
BF16 = jnp.bfloat16
MESH = pl.DeviceIdType.MESH

N_DEV = 8
DEPTH = 4
N_A = 2
HEAD_DIM = 64
Q_BLOCK = 128
CONV_W = 31
FFN_CONV_W = 3
LN_EPS = 1e-5
DN_ALPHA = (2.0 * DEPTH) ** 0.25
ADAM_LR = 0.001
ADAM_B1 = 0.9
ADAM_B2 = 0.999
ADAM_EPS = 1e-08
ADAM_WD = 0.01
ADAM_STEP = 10

LANES = 128
SUBLANES = 8
PACK_W = 1024
VMEM_LIMIT = 56 * 1024 * 1024

BIG = (("a_pw1_w", 2), ("a_pw2_w", 1), ("b_wq", 1), ("kv_wk", 0), ("kv_wv", 0), ("b_wo", 1),
       ("ffn_w_up", 2), ("ffn_w_gate", 2), ("ffn_w_down", 1), ("ple_w_gate", 1), ("ple_w_proj", 2))
SMALL = (("a_pw1_b", 1), ("a_dw_w", 2), ("a_dw_b", 1), ("a_ln_g", 1), ("a_ln_b", 1), ("a_pw2_b", 1),
         ("ffn_conv_w", 2))
REPL = ("ln_mix_g", "ln_mix_b", "ffn_conv_b", "ln_ffn_g", "ln_ffn_b")
WEIGHTS = ("a_pw1_w", "a_pw1_b", "a_dw_w", "a_dw_b", "a_ln_g", "a_ln_b", "a_pw2_w", "a_pw2_b", "b_wq", "kv_wk",
           "kv_wv", "b_wo", "ln_mix_g", "ln_mix_b", "ffn_w_up", "ffn_w_gate", "ffn_conv_w", "ffn_conv_b",
           "ffn_w_down", "ple_w_gate", "ple_w_proj", "ln_ffn_g", "ln_ffn_b")


def _cp(*sem):
    return pltpu.CompilerParams(dimension_semantics=sem, vmem_limit_bytes=VMEM_LIMIT)


def _pick(dim, target, align=LANES):
    if dim <= target:
        return dim
    t = (target // align) * align
    while t >= align:
        if dim % t == 0:
            return t
        t -= align
    return dim


_DOT_DIMS = {"nn": (((1,), (0,)), ((), ())), "nt": (((1,), (1,)), ((), ())), "tn": (((0,), (0,)), ((), ()))}


def mm(a, b, mode, *, bias=None, add=None, add_scale=1.0, out_dtype=F32, name):
    if mode == "tn":
        K, M = a.shape
    else:
        M, K = a.shape
    N = b.shape[0] if mode == "nt" else b.shape[1]
    tm, tn, tk = _pick(M, 1536 if mode == "tn" else 512), _pick(N, 1536), _pick(K, 1536 if mode == "tn" else 2816)
    nk = K // tk
    dims = _DOT_DIMS[mode]

    def body(*refs):
        a_ref, b_ref = refs[0], refs[1]
        pos = 2
        bias_ref = add_ref = None
        if bias is not None:
            bias_ref = refs[pos]
            pos += 1
        if add is not None:
            add_ref = refs[pos]
            pos += 1
        o_ref, acc_ref = refs[pos], refs[pos + 1]
        k = pl.program_id(2)

        @pl.when(k == 0)
        def _():
            acc_ref[...] = jnp.zeros_like(acc_ref)

        acc_ref[...] += lax.dot_general(a_ref[...].astype(BF16), b_ref[...].astype(BF16), dims,
                                        preferred_element_type=F32)

        @pl.when(k == nk - 1)
        def _():
            r = acc_ref[...]
            if bias_ref is not None:
                r = r + bias_ref[...]
            if add_ref is not None:
                r = r + add_scale * add_ref[...].astype(F32)
            o_ref[...] = r.astype(o_ref.dtype)

    a_spec = pl.BlockSpec((tk, tm), lambda j, i, k: (k, i)) if mode == "tn" else pl.BlockSpec((tm, tk), lambda j, i, k: (i, k))
    b_spec = pl.BlockSpec((tn, tk), lambda j, i, k: (j, k)) if mode == "nt" else pl.BlockSpec((tk, tn), lambda j, i, k: (k, j))
    in_specs, args = [a_spec, b_spec], [a, b]
    if bias is not None:
        in_specs.append(pl.BlockSpec((1, tn), lambda j, i, k: (0, j)))
        args.append(bias)
    if add is not None:
        in_specs.append(pl.BlockSpec((tm, tn), lambda j, i, k: (i, j)))
        args.append(add)
    return pl.pallas_call(
        body, grid=(N // tn, M // tm, nk), in_specs=in_specs,
        out_specs=pl.BlockSpec((tm, tn), lambda j, i, k: (i, j)),
        out_shape=jax.ShapeDtypeStruct((M, N), out_dtype),
        scratch_shapes=[pltpu.VMEM((tm, tn), F32)],
        compiler_params=_cp("parallel", "parallel", "arbitrary"), name=name)(*args)


def _rows(body, *, n_rows, tm, row_ins, full_ins=(), row_outs=(), acc_outs=(), scratch=(), reverse=False, name):
    n = n_rows // tm

    def rmap(i):
        return (n - 1 - i, 0) if reverse else (i, 0)

    in_specs = [pl.BlockSpec((tm, a.shape[1]), rmap) for a in row_ins]
    in_specs += [pl.BlockSpec(a.shape, lambda i, nd=a.ndim: (0,) * nd) for a in full_ins]
    out_shape = [jax.ShapeDtypeStruct((n_rows, w), dt) for (w, dt) in row_outs]
    out_shape += [jax.ShapeDtypeStruct(s, dt) for (s, dt) in acc_outs]
    out_specs = [pl.BlockSpec((tm, w), rmap) for (w, dt) in row_outs]
    out_specs += [pl.BlockSpec(s, lambda i, nd=len(s): (0,) * nd) for (s, dt) in acc_outs]
    return pl.pallas_call(
        functools.partial(body, n), grid=(n,), in_specs=in_specs, out_specs=out_specs, out_shape=out_shape,
        scratch_shapes=list(scratch), compiler_params=_cp("arbitrary"), name=name)(*row_ins, *full_ins)


def _sigmoid(x):
    return 1.0 / (1.0 + jnp.exp(-x))


def _ln_hat(r):
    mu = jnp.mean(r, axis=-1, keepdims=True)
    xc = r - mu
    var = jnp.mean(xc * xc, axis=-1, keepdims=True)
    rstd = lax.rsqrt(var + LN_EPS)
    return xc * rstd, rstd


def _ln_back(xhat, rstd, g, dy):
    dxh = dy * g
    m1 = jnp.mean(dxh, axis=-1, keepdims=True)
    m2 = jnp.mean(dxh * xhat, axis=-1, keepdims=True)
    return rstd * (dxh - m1 - xhat * m2)


def _colsum(x):
    return jnp.sum(x, axis=0, keepdims=True)


def _acc(i, ref, val):
    @pl.when(i == 0)
    def _():
        ref[...] = val

    @pl.when(i > 0)
    def _():
        ref[...] += val


def res_ln(x, mix, g, b, *, ple=None, name):
    S, D = x.shape

    def body(n, *refs):
        if ple is None:
            x_ref, m_ref, g_ref, b_ref, r_ref, y_ref = refs
            r = DN_ALPHA * x_ref[...] + m_ref[...]
        else:
            x_ref, m_ref, pgl_ref, pp_ref, g_ref, b_ref, r_ref, y_ref = refs
            r = DN_ALPHA * x_ref[...] + m_ref[...] + _sigmoid(pgl_ref[...]) * pp_ref[...]
        xhat, _ = _ln_hat(r)
        r_ref[...] = r
        y_ref[...] = xhat * g_ref[...] + b_ref[...]

    row_ins = [x, mix] + ([] if ple is None else list(ple))
    return _rows(body, n_rows=S, tm=_pick(S, 256, SUBLANES), row_ins=row_ins, full_ins=[g, b],
                 row_outs=[(D, F32), (D, F32)], name=name)


def ln_bwd(r, g, dy, *, name):
    S, D = r.shape

    def body(n, r_ref, dy_ref, g_ref, dr_ref, dg_ref, db_ref, ds_ref):
        i = pl.program_id(0)
        xhat, rstd = _ln_hat(r_ref[...])
        dy_v = dy_ref[...]
        dr = _ln_back(xhat, rstd, g_ref[...], dy_v)
        dr_ref[...] = dr
        _acc(i, dg_ref, _colsum(dy_v * xhat))
        _acc(i, db_ref, _colsum(dy_v))
        _acc(i, ds_ref, _colsum(dr))

    return _rows(body, n_rows=S, tm=_pick(S, 256, SUBLANES), row_ins=[r, dy], full_ins=[g],
                 row_outs=[(D, F32)], acc_outs=[((1, D), F32)] * 3, name=name)


def glu_fwd(h1, *, name):
    S, D2 = h1.shape
    D = D2 // 2

    def body(n, h_ref, o_ref):
        o_ref[...] = h_ref[:, :D] * _sigmoid(h_ref[:, D:])

    return _rows(body, n_rows=S, tm=_pick(S, 256, SUBLANES), row_ins=[h1], row_outs=[(D, F32)], name=name)[0]


def glu_bwd(h1, dh2, *, name):
    S, D2 = h1.shape
    D = D2 // 2

    def body(n, h_ref, d_ref, o_ref, s_ref):
        i = pl.program_id(0)
        a, sg, d = h_ref[:, :D], _sigmoid(h_ref[:, D:]), d_ref[...]
        da = d * sg
        dg = d * a * sg * (1.0 - sg)
        o_ref[:, :D] = da.astype(o_ref.dtype)
        o_ref[:, D:] = dg.astype(o_ref.dtype)
        _acc(i, s_ref, jnp.concatenate([_colsum(da), _colsum(dg)], axis=1))

    return _rows(body, n_rows=S, tm=_pick(S, 256, SUBLANES), row_ins=[h1, dh2], row_outs=[(D2, BF16)],
                 acc_outs=[((1, D2), F32)], name=name)


CONV_ROWS = 32
CONV_LANES = 256


def _halo(k):
    return -(-(k - 1) // SUBLANES) * SUBLANES


def _phases(offs):
    return sorted({o % SUBLANES for o in offs} - {0})


def _shift_scratch(offs, n_rows, width):
    return pltpu.VMEM((max(len(_phases(offs)), 1), n_rows, width), F32)


def _make_shifted(buf_ref, sh_ref, offs):
    n = buf_ref.shape[0] - SUBLANES
    for p, b in enumerate(_phases(offs)):
        sh_ref[p, pl.ds(0, n), :] = buf_ref[pl.ds(b, n), :]


def _tap(buf_ref, sh_ref, offs, k, rc, rows, lc, lw):
    b = offs[k] % SUBLANES
    src = buf_ref if b == 0 else sh_ref.at[_phases(offs).index(b)]
    return src[pl.ds(offs[k] - b + rc, rows), pl.ds(lc, lw)]


def _conv_taps(buf_ref, sh_ref, w_ref, offs, tm, width, emit):
    _make_shifted(buf_ref, sh_ref, offs)
    rows = min(CONV_ROWS, tm)
    for lc in range(0, width, CONV_LANES):
        lw = min(CONV_LANES, width - lc)
        for rc in range(0, tm, rows):
            acc = None
            for k in range(len(offs)):
                t = _tap(buf_ref, sh_ref, offs, k, rc, rows, lc, lw) * w_ref[pl.ds(k, 1), pl.ds(lc, lw)]
                acc = t if acc is None else acc + t
            emit(rc, lc, lw, rows, acc)


def _fill_causal(i, buf_ref, x_ref, halo, tm):
    @pl.when(i == 0)
    def _():
        buf_ref[pl.ds(0, halo), :] = jnp.zeros((halo, buf_ref.shape[1]), F32)

    @pl.when(i > 0)
    def _():
        buf_ref[pl.ds(0, halo), :] = buf_ref[pl.ds(tm, halo), :]

    buf_ref[pl.ds(halo, tm), :] = x_ref[...].astype(F32)


def conv_ln_silu_fwd(x, w, b, g, beta, *, name):
    S, C = x.shape
    K = w.shape[0]
    halo = _halo(K)
    tm = _pick(S, 256, SUBLANES)
    offs = [halo - (K - 1) + k for k in range(K)]

    def body(n, x_ref, w_ref, b_ref, g_ref, beta_ref, h3_ref, h5_ref, buf_ref, sh_ref):
        i = pl.program_id(0)
        _fill_causal(i, buf_ref, x_ref, halo, tm)

        def emit(rc, lc, lw, rows, acc):
            h3_ref[pl.ds(rc, rows), pl.ds(lc, lw)] = acc + b_ref[:, pl.ds(lc, lw)]

        _conv_taps(buf_ref, sh_ref, w_ref, offs, tm, C, emit)
        xhat, _ = _ln_hat(h3_ref[...])
        h4 = xhat * g_ref[...] + beta_ref[...]
        h5_ref[...] = (h4 * _sigmoid(h4)).astype(h5_ref.dtype)

    return _rows(body, n_rows=S, tm=tm, row_ins=[x], full_ins=[w, b, g, beta], row_outs=[(C, F32), (C, BF16)],
                 scratch=[pltpu.VMEM((tm + halo, C), F32), _shift_scratch(offs, tm + halo, C)], name=name)


def conv_act_fwd(gp, u, w, b, *, name):
    S, C = gp.shape
    K = w.shape[0]
    halo = _halo(K)
    tm = _pick(S, 256, SUBLANES)
    offs = [halo - (K - 1) + k for k in range(K)]

    def body(n, x_ref, u_ref, w_ref, b_ref, g_ref, hh_ref, buf_ref, sh_ref):
        i = pl.program_id(0)
        _fill_causal(i, buf_ref, x_ref, halo, tm)

        def emit(rc, lc, lw, rows, acc):
            gv = acc + b_ref[:, pl.ds(lc, lw)]
            g_ref[pl.ds(rc, rows), pl.ds(lc, lw)] = gv.astype(g_ref.dtype)
            hh_ref[pl.ds(rc, rows), pl.ds(lc, lw)] = (gv * _sigmoid(gv) * u_ref[pl.ds(rc, rows), pl.ds(lc, lw)].astype(F32)).astype(hh_ref.dtype)

        _conv_taps(buf_ref, sh_ref, w_ref, offs, tm, C, emit)

    return _rows(body, n_rows=S, tm=tm, row_ins=[gp, u], full_ins=[w, b], row_outs=[(C, BF16), (C, BF16)],
                 scratch=[pltpu.VMEM((tm + halo, C), F32), _shift_scratch(offs, tm + halo, C)], name=name)


def conv_bwd_x(dy, w, *, out_dtype, name):
    S, C = dy.shape
    K = w.shape[0]
    halo = _halo(K)
    tm = _pick(S, 256, SUBLANES)
    offs = [K - 1 - k for k in range(K)]

    def body(n, dy_ref, w_ref, dx_ref, buf_ref, sh_ref):
        i = pl.program_id(0)

        @pl.when(i == 0)
        def _():
            buf_ref[pl.ds(tm, halo), :] = jnp.zeros((halo, C), F32)

        @pl.when(i > 0)
        def _():
            buf_ref[pl.ds(tm, halo), :] = buf_ref[pl.ds(0, halo), :]

        buf_ref[pl.ds(0, tm), :] = dy_ref[...].astype(F32)

        def emit(rc, lc, lw, rows, acc):
            dx_ref[pl.ds(rc, rows), pl.ds(lc, lw)] = acc.astype(dx_ref.dtype)

        _conv_taps(buf_ref, sh_ref, w_ref, offs, tm, C, emit)

    return _rows(body, n_rows=S, tm=tm, row_ins=[dy], full_ins=[w], row_outs=[(C, out_dtype)],
                 scratch=[pltpu.VMEM((tm + halo, C), F32), _shift_scratch(offs, tm + halo, C)], reverse=True, name=name)[0]


def conv_bwd_w(x, dy, K, *, name):
    S, C = x.shape
    halo = _halo(K)
    tm = _pick(S, 256, SUBLANES)
    offs = [halo - (K - 1) + k for k in range(K)]
    rows = min(CONV_ROWS, tm)

    def body(n, x_ref, dy_ref, dw_ref, db_ref, buf_ref, acc_ref, sh_ref):
        i = pl.program_id(0)
        _fill_causal(i, buf_ref, x_ref, halo, tm)
        _make_shifted(buf_ref, sh_ref, offs)

        @pl.when(i == 0)
        def _():
            acc_ref[...] = jnp.zeros_like(acc_ref)

        for lc in range(0, C, CONV_LANES):
            lw = min(CONV_LANES, C - lc)
            for k in range(K):
                s = None
                for rc in range(0, tm, rows):
                    t = dy_ref[pl.ds(rc, rows), pl.ds(lc, lw)].astype(F32) * _tap(buf_ref, sh_ref, offs, k, rc, rows, lc, lw)
                    s = t if s is None else s + t
                s8 = s[0:SUBLANES]
                for q in range(1, rows // SUBLANES):
                    s8 = s8 + s[q * SUBLANES:(q + 1) * SUBLANES]
                acc_ref[pl.ds(k * SUBLANES, SUBLANES), pl.ds(lc, lw)] += s8
        _acc(i, db_ref, _colsum(dy_ref[...].astype(F32)))

        @pl.when(i == n - 1)
        def _():
            for k in range(K):
                dw_ref[pl.ds(k, 1), :] = _colsum(acc_ref[pl.ds(k * SUBLANES, SUBLANES), :])

    return _rows(body, n_rows=S, tm=tm, row_ins=[x, dy], acc_outs=[((K, C), F32), ((1, C), F32)],
                 scratch=[pltpu.VMEM((tm + halo, C), F32), pltpu.VMEM((K * SUBLANES, C), F32),
                          _shift_scratch(offs, tm + halo, C)], name=name)


def ln_silu_bwd(h3, g, beta, dh5, *, name):
    S, C = h3.shape

    def body(n, h_ref, d_ref, g_ref, beta_ref, dh_ref, dg_ref, db_ref):
        i = pl.program_id(0)
        xhat, rstd = _ln_hat(h_ref[...])
        h4 = xhat * g_ref[...] + beta_ref[...]
        sg = _sigmoid(h4)
        dh4 = d_ref[...] * sg * (1.0 + h4 * (1.0 - sg))
        dh_ref[...] = _ln_back(xhat, rstd, g_ref[...], dh4)
        _acc(i, dg_ref, _colsum(dh4 * xhat))
        _acc(i, db_ref, _colsum(dh4))

    return _rows(body, n_rows=S, tm=_pick(S, 256, SUBLANES), row_ins=[h3, dh5], full_ins=[g, beta],
                 row_outs=[(C, F32)], acc_outs=[((1, C), F32)] * 2, name=name)


def ffn_gate_bwd(dhh, u, g, gp, w, *, name):
    S, C = u.shape
    K = w.shape[0]
    halo = _halo(K)
    tm = _pick(S, 256, SUBLANES)
    offs = [K - 1 - k for k in range(K)]
    rows = min(CONV_ROWS, tm)

    def body(n, d_ref, u_ref, g_ref, gp_ref, w_ref, du_ref, dgp_ref, dw_ref, db_ref, buf_ref, sh_ref, acc_ref):
        i = pl.program_id(0)

        @pl.when(i == 0)
        def _():
            buf_ref[pl.ds(tm, halo), :] = jnp.zeros((halo, C), F32)
            acc_ref[...] = jnp.zeros_like(acc_ref)

        @pl.when(i > 0)
        def _():
            buf_ref[pl.ds(tm, halo), :] = buf_ref[pl.ds(0, halo), :]

        d, gv = d_ref[...].astype(F32), g_ref[...].astype(F32)
        sg = _sigmoid(gv)
        du_ref[...] = (d * gv * sg).astype(du_ref.dtype)
        dg = d * u_ref[...].astype(F32) * sg * (1.0 + gv * (1.0 - sg))
        buf_ref[pl.ds(0, tm), :] = dg
        _acc(i, db_ref, _colsum(dg))

        def emit(rc, lc, lw, nrows, acc):
            dgp_ref[pl.ds(rc, nrows), pl.ds(lc, lw)] = acc.astype(dgp_ref.dtype)

        _conv_taps(buf_ref, sh_ref, w_ref, offs, tm, C, emit)
        for lc in range(0, C, CONV_LANES):
            lw = min(CONV_LANES, C - lc)
            for k in range(K):
                s_ = None
                for rc in range(0, tm, rows):
                    t = gp_ref[pl.ds(rc, rows), pl.ds(lc, lw)].astype(F32) * _tap(buf_ref, sh_ref, offs, k, rc, rows, lc, lw)
                    s_ = t if s_ is None else s_ + t
                s8 = s_[0:SUBLANES]
                for q in range(1, rows // SUBLANES):
                    s8 = s8 + s_[q * SUBLANES:(q + 1) * SUBLANES]
                acc_ref[pl.ds(k * SUBLANES, SUBLANES), pl.ds(lc, lw)] += s8

        @pl.when(i == n - 1)
        def _():
            for k in range(K):
                dw_ref[pl.ds(k, 1), :] = _colsum(acc_ref[pl.ds(k * SUBLANES, SUBLANES), :])

    return _rows(body, n_rows=S, tm=tm, row_ins=[dhh, u, g, gp], full_ins=[w], row_outs=[(C, BF16), (C, BF16)],
                 acc_outs=[((K, C), F32), ((1, C), F32)],
                 scratch=[pltpu.VMEM((tm + halo, C), F32), _shift_scratch(offs, tm + halo, C),
                          pltpu.VMEM((K * SUBLANES, C), F32)], reverse=True, name=name)


def ln_ple_bwd(r, g, dy, pgl, pp, *, name):
    S, D = r.shape

    def body(n, r_ref, dy_ref, l_ref, p_ref, g_ref, dr_ref, dpp_ref, dpl_ref, dg_ref, db_ref):
        i = pl.program_id(0)
        xhat, rstd = _ln_hat(r_ref[...])
        dy_v = dy_ref[...]
        dr = _ln_back(xhat, rstd, g_ref[...], dy_v)
        dr_ref[...] = dr
        sg = _sigmoid(l_ref[...])
        dpp_ref[...] = (dr * sg).astype(dpp_ref.dtype)
        dpl_ref[...] = (dr * p_ref[...] * sg * (1.0 - sg)).astype(dpl_ref.dtype)
        _acc(i, dg_ref, _colsum(dy_v * xhat))
        _acc(i, db_ref, _colsum(dy_v))

    return _rows(body, n_rows=S, tm=_pick(S, 256, SUBLANES), row_ins=[r, dy, pgl, pp], full_ins=[g],
                 row_outs=[(D, F32), (D, BF16), (D, BF16)], acc_outs=[((1, D), F32)] * 2, name=name)


def loss_grad(y, target, *, name):
    S, D = y.shape

    def body(n, y_ref, t_ref, dy_ref, l_ref):
        i = pl.program_id(0)
        e = y_ref[...] - t_ref[...]
        dy_ref[...] = e * (1.0 / D)
        s = jnp.sum(_colsum(e * e), axis=1, keepdims=True) * (0.5 / D)
        _acc(i, l_ref, jnp.broadcast_to(s, (1, LANES)))

    return _rows(body, n_rows=S, tm=_pick(S, 256, SUBLANES), row_ins=[y, target], row_outs=[(D, F32)],
                 acc_outs=[((1, LANES), F32)], name=name)


def _key_step(S):
    return min(512, S // 2)


EXIT_LOG = -110.0


def _attn_consts():
    lane = lax.broadcasted_iota(jnp.int32, (1, LANES), 1)
    heads = (lane < HEAD_DIM, lane >= HEAD_DIM)
    row = lax.broadcasted_iota(jnp.int32, (Q_BLOCK, Q_BLOCK), 0)
    col = lax.broadcasted_iota(jnp.int32, (Q_BLOCK, Q_BLOCK), 1)
    causal = jnp.concatenate([col < row] * 2, axis=0)
    return heads, row, col, causal


def _tri(cond):
    return jnp.where(cond, 1.0, 0.0).astype(BF16)


def _keysum2(x, tri):
    hi = x.astype(BF16)
    lo = (x - hi.astype(F32)).astype(BF16)
    return jnp.dot(jnp.concatenate([hi, lo], axis=1), jnp.concatenate([tri, tri], axis=0),
                   preferred_element_type=F32)


def _stack_heads(x, heads):
    return jnp.concatenate([jnp.where(m, x, jnp.zeros_like(x)) for m in heads], axis=0)


def _log1m_beta(z):
    return -(jnp.maximum(z, 0.0) + jnp.log(1.0 + jnp.exp(-jnp.abs(z))))


def attn_fwd(q, k, v, *, name):
    S, D = q.shape
    nb = S // Q_BLOCK
    tk = _key_step(S)
    nkb = tk // Q_BLOCK
    scale = 1.0 / math.sqrt(HEAD_DIM)

    def body(q_ref, k_ref, v_ref, o_ref, tot_ref, seen_ref, vm_ref):
        heads, row, col, causal = _attn_consts()
        above = _tri(row > col)
        for h in range(2):
            vm_ref[h] = jnp.where(heads[h], v_ref[...], jnp.zeros_like(v_ref[...]))

        def step(sb, carry, qq, nblk, diag):
            acc, cl = carry
            c0 = pl.multiple_of(sb * tk, tk)
            z = lax.dot_general(qq, k_ref[pl.ds(c0, nblk * Q_BLOCK), :], _DOT_DIMS["nt"], preferred_element_type=F32)
            zl, es, rs = [], [], []
            for jb in range(nblk):
                zb = z[:, jb * Q_BLOCK:(jb + 1) * Q_BLOCK]
                lr = _log1m_beta(zb)
                l = jnp.where(causal, lr, 0.0) if diag and jb == nblk - 1 else lr
                zl.append(zb + lr)
                es.append(_keysum2(l, above))
                rs.append(jnp.sum(l, axis=1, keepdims=True))
            a = [None] * nblk
            for jb in reversed(range(nblk)):
                ab = jnp.exp(zl[jb] + es[jb] + cl)
                if diag and jb == nblk - 1:
                    ab = jnp.where(causal, ab, 0.0)
                a[jb] = ab.astype(BF16)
                cl = cl + rs[jb]
            a = jnp.concatenate(a, axis=1)
            for h in range(2):
                acc = acc + jnp.dot(a[h * Q_BLOCK:(h + 1) * Q_BLOCK], vm_ref[h, pl.ds(c0, nblk * Q_BLOCK), :],
                                    preferred_element_type=F32)
            return acc, cl

        def qblock(i, _):
            r0 = pl.multiple_of(i * Q_BLOCK, Q_BLOCK)
            qq = _stack_heads(q_ref[pl.ds(r0, Q_BLOCK), :] * scale, heads)
            last = i // nkb
            carry = (jnp.zeros((Q_BLOCK, LANES), F32), jnp.zeros((2 * Q_BLOCK, 1), F32))
            carry = lax.switch(i % nkb, [functools.partial(step, last, qq=qq, nblk=m + 1, diag=True) for m in range(nkb)],
                               carry)

            def more(c):
                return jnp.logical_and(c[0] < last, jnp.max(c[2]) >= EXIT_LOG)

            def left(c):
                return (c[0] + 1, *step(last - 1 - c[0], c[1:], qq, nkb, False))

            seen, acc, cl = lax.while_loop(more, left, (jnp.int32(0), *carry))
            o_ref[pl.ds(r0, Q_BLOCK), :] = acc.astype(o_ref.dtype)
            tot_ref[pl.ds(r0, Q_BLOCK), :] = jnp.where(heads[0], cl[:Q_BLOCK], cl[Q_BLOCK:])
            seen_ref[pl.ds(pl.multiple_of(i * SUBLANES, SUBLANES), SUBLANES), :] = jnp.full((SUBLANES, LANES), seen, F32)
            return 0

        lax.fori_loop(0, nb, qblock, 0)

    spec = pl.BlockSpec((S, LANES), lambda h: (0, h))
    seen_spec = pl.BlockSpec((nb * SUBLANES, LANES), lambda h: (0, h))
    return pl.pallas_call(body, grid=(D // LANES,), in_specs=[spec] * 3, out_specs=[spec, spec, seen_spec],
                          out_shape=[jax.ShapeDtypeStruct((S, D), BF16), jax.ShapeDtypeStruct((S, D), F32),
                                     jax.ShapeDtypeStruct((nb * SUBLANES, D), F32)],
                          scratch_shapes=[pltpu.VMEM((2, S, LANES), BF16)], compiler_params=_cp("parallel"),
                          name=name)(q, k, v)


def attn_bwd(q, k, v, tot, seen, do, dk0, dv0, *, name):
    S, D = q.shape
    nb = S // Q_BLOCK
    tk = _key_step(S)
    nkb = tk // Q_BLOCK
    scale = 1.0 / math.sqrt(HEAD_DIM)
    has_init = dk0 is not None

    def body(*refs):
        if has_init:
            q_ref, k_ref, v_ref, tot_ref, seen_ref, do_ref, dk0_ref, dv0_ref, dq_ref, dk_ref, dv_ref, km_ref = refs
            dk_ref[...] = dk0_ref[...]
            dv_ref[...] = dv0_ref[...]
        else:
            q_ref, k_ref, v_ref, tot_ref, seen_ref, do_ref, dq_ref, dk_ref, dv_ref, km_ref = refs
            dk_ref[...] = jnp.zeros_like(dk_ref)
            dv_ref[...] = jnp.zeros_like(dv_ref)
        heads, row, col, causal = _attn_consts()
        upto = _tri(row <= col)
        before = _tri(row < col)
        for h in range(2):
            km_ref[h] = jnp.where(heads[h], k_ref[...], jnp.zeros_like(k_ref[...]))

        def step(sb, carry, qq, dd, totl, nblk, diag):
            dq, pl_, pg = carry
            c0 = pl.multiple_of(sb * tk, tk)
            keys = pl.ds(c0, nblk * Q_BLOCK)
            z = lax.dot_general(qq, k_ref[keys, :], _DOT_DIMS["nt"], preferred_element_type=F32)
            da = lax.dot_general(dd, v_ref[keys, :], _DOT_DIMS["nt"], preferred_element_type=F32)
            blocks = range(nblk)
            masked = [diag and jb == nblk - 1 for jb in blocks]
            zb = [z[:, jb * Q_BLOCK:(jb + 1) * Q_BLOCK] for jb in blocks]
            lr = [_log1m_beta(zb[jb]) for jb in blocks]
            l = [jnp.where(causal, lr[jb], 0.0) if masked[jb] else lr[jb] for jb in blocks]
            lsum = [_keysum2(l[jb], upto) for jb in blocks]
            lrow = [jnp.sum(l[jb], axis=1, keepdims=True) for jb in blocks]
            a, g = [None] * nblk, [None] * nblk
            for jb in blocks:
                ab = jnp.exp(zb[jb] + lr[jb] + (totl - pl_ - lsum[jb]))
                if masked[jb]:
                    ab = jnp.where(causal, ab, 0.0)
                g[jb] = ab * da[:, jb * Q_BLOCK:(jb + 1) * Q_BLOCK]
                a[jb] = ab.astype(BF16)
                pl_ = pl_ + lrow[jb]
            gsum = [jnp.dot(g[jb].astype(BF16), before, preferred_element_type=F32) for jb in blocks]
            grow = [jnp.sum(g[jb], axis=1, keepdims=True) for jb in blocks]
            dz = [None] * nblk
            for jb in blocks:
                dzb = g[jb] * jnp.exp(lr[jb]) - jnp.exp(zb[jb] + lr[jb]) * (pg + gsum[jb])
                if masked[jb]:
                    dzb = jnp.where(causal, dzb, 0.0)
                dz[jb] = dzb.astype(BF16)
                pg = pg + grow[jb]
            a = jnp.concatenate(a, axis=1)
            dz = jnp.concatenate(dz, axis=1)
            for h in range(2):
                dq = dq + jnp.dot(dz[h * Q_BLOCK:(h + 1) * Q_BLOCK], km_ref[h, keys, :], preferred_element_type=F32)
            dk_ref[keys, :] += lax.dot_general(dz, qq, _DOT_DIMS["tn"], preferred_element_type=F32)
            dv_ref[keys, :] += lax.dot_general(a, dd, _DOT_DIMS["tn"], preferred_element_type=F32)
            return dq, pl_, pg

        def qblock(i, _):
            r0 = pl.multiple_of(i * Q_BLOCK, Q_BLOCK)
            qq = _stack_heads(q_ref[pl.ds(r0, Q_BLOCK), :] * scale, heads)
            dd = _stack_heads(do_ref[pl.ds(r0, Q_BLOCK), :].astype(BF16), heads)
            tot2 = tot_ref[pl.ds(r0, Q_BLOCK), :]
            totl = jnp.concatenate([tot2[:, 0:1], tot2[:, HEAD_DIM:HEAD_DIM + 1]], axis=0)
            last = i // nkb
            zc = jnp.zeros((2 * Q_BLOCK, 1), F32)
            carry = (jnp.zeros((Q_BLOCK, LANES), F32), zc, zc)
            walked = jnp.max(seen_ref[pl.ds(pl.multiple_of(i * SUBLANES, SUBLANES), SUBLANES), :]).astype(jnp.int32)
            first = last - jnp.clip(walked, 0, last)
            carry = lax.fori_loop(first, last, lambda sb, c: step(sb, c, qq, dd, totl, nkb, False), carry)
            carry = lax.switch(i % nkb, [functools.partial(step, last, qq=qq, dd=dd, totl=totl, nblk=m + 1, diag=True)
                                         for m in range(nkb)], carry)
            dq_ref[pl.ds(r0, Q_BLOCK), :] = (carry[0] * scale).astype(dq_ref.dtype)
            return 0

        lax.fori_loop(0, nb, qblock, 0)

    spec = pl.BlockSpec((S, LANES), lambda h: (0, h))
    seen_spec = pl.BlockSpec((nb * SUBLANES, LANES), lambda h: (0, h))
    args = [q, k, v, tot, seen, do] + ([dk0, dv0] if has_init else [])
    return pl.pallas_call(
        body, grid=(D // LANES,), in_specs=[spec] * 4 + [seen_spec] + [spec] * (len(args) - 5), out_specs=[spec] * 3,
        out_shape=[jax.ShapeDtypeStruct((S, D), BF16), jax.ShapeDtypeStruct((S, D), F32), jax.ShapeDtypeStruct((S, D), F32)],
        scratch_shapes=[pltpu.VMEM((2, S, LANES), BF16)], compiler_params=_cp("parallel"), name=name)(*args)


def _dev_index(px, py, pc):
    return 4 * px + 2 * py + pc


def all_gather(bufs):
    nb = len(bufs)

    def body(*refs):
        ins, outs = refs[:nb], refs[nb:2 * nb]
        send_sems, recv_sems, local_sems = refs[2 * nb:]
        x, y, c = lax.axis_index("x"), lax.axis_index("y"), lax.axis_index("c")
        me, sibling = (x, y, c), (x, y, 1 - c)
        chips = [(1 - x, y), (x, 1 - y), (1 - x, 1 - y)]

        def copy(b, k, block, to, from_input=False):
            slot = outs[b].at[_dev_index(*block)]
            return pltpu.make_async_remote_copy(
                src_ref=ins[b] if from_input else slot, dst_ref=slot,
                send_sem=send_sems.at[7 * b + k], recv_sem=recv_sems.at[7 * b + k], device_id=to, device_id_type=MESH)

        mine = [pltpu.make_async_copy(ins[b], outs[b].at[_dev_index(*me)], local_sems.at[b]) for b in range(nb)]
        for cp in mine:
            cp.start()
        first = []
        for b in range(nb):
            first.append(copy(b, 0, me, sibling, from_input=True))
            first += [copy(b, 1 + j, me, (*chip, c), from_input=True) for j, chip in enumerate(chips)]
        for cp in first:
            cp.start()
        passed = []
        for j, chip in enumerate(chips):
            for b in range(nb):
                copy(b, 1 + j, (*chip, c), me).wait_recv()
                fwd = copy(b, 4 + j, (*chip, c), sibling)
                fwd.start()
                passed.append(fwd)
        for b in range(nb):
            copy(b, 0, sibling, me).wait_recv()
            for j, chip in enumerate(chips):
                copy(b, 4 + j, (*chip, 1 - c), me).wait_recv()
        for cp in first + passed:
            cp.wait_send()
        for cp in mine:
            cp.wait()

    any_spec = pl.BlockSpec(memory_space=pl.ANY)
    return pl.pallas_call(
        body, in_specs=[any_spec] * nb, out_specs=[any_spec] * nb,
        out_shape=[jax.ShapeDtypeStruct((N_DEV,) + b.shape, b.dtype) for b in bufs],
        scratch_shapes=[pltpu.SemaphoreType.DMA((7 * nb,)), pltpu.SemaphoreType.DMA((7 * nb,)),
                        pltpu.SemaphoreType.DMA((nb,))],
        name="all_gather_weights")(*bufs)


def _sources(groups):
    return [s for g in groups for (s, _) in g[3]]


def _layout(groups, refs):
    out, si = [], 0
    for g, (_, _, _, lst) in enumerate(groups):
        for (s, off) in lst:
            out.append((g, refs[si], off, s.shape[-2]))
            si += 1
    return out


def pair_exchange(groups):
    srcs = _sources(groups)
    ns, ng = len(srcs), len(groups)

    def body(*refs):
        outs = refs[ns:ns + ng]
        send_sems, recv_sems = refs[ns + ng:]
        x, y, c = lax.axis_index("x"), lax.axis_index("y"), lax.axis_index("c")
        sibling = (x, y, 1 - c)
        for (g, ref, off, r) in _layout(groups, refs[:ns]):
            for q in range(N_DEV // 2):
                pltpu.make_async_remote_copy(
                    src_ref=ref.at[2 * q + 1 - c], dst_ref=outs[g].at[q, pl.ds(off, r)], send_sem=send_sems.at[g],
                    recv_sem=recv_sems.at[g], device_id=sibling, device_id_type=MESH).start()
        whole = [pltpu.make_async_remote_copy(
            src_ref=outs[g], dst_ref=outs[g], send_sem=send_sems.at[g], recv_sem=recv_sems.at[g],
            device_id=sibling, device_id_type=MESH) for g in range(ng)]
        for w in whole:
            w.wait_recv()
        for w in whole:
            w.wait_send()

    any_spec = pl.BlockSpec(memory_space=pl.ANY)
    return pl.pallas_call(
        body, in_specs=[any_spec] * ns, out_specs=[any_spec] * ng,
        out_shape=[jax.ShapeDtypeStruct((N_DEV // 2, r, w), dt) for (r, w, dt, _) in groups],
        scratch_shapes=[pltpu.SemaphoreType.DMA((ng,)), pltpu.SemaphoreType.DMA((ng,))],
        name="pair_exchange")(*srcs)


def pair_sum(src, got, off, core, *, name):
    _, r, W = src.shape
    tr = _row_tile(r, off, 1024)
    o = off // tr

    def body(c_ref, s_ref, g_ref, o_ref):
        o_ref[...] = (s_ref[...].astype(F32) + g_ref[...].astype(F32)).astype(o_ref.dtype)

    return pl.pallas_call(
        body,
        grid_spec=pltpu.PrefetchScalarGridSpec(
            num_scalar_prefetch=1, grid=(N_DEV // 2, r // tr),
            in_specs=[pl.BlockSpec((None, None, tr, W), lambda q, i, c: (q, c[0], i, 0)),
                      pl.BlockSpec((None, tr, W), lambda q, i, c: (q, i + o, 0))],
            out_specs=pl.BlockSpec((None, tr, W), lambda q, i, c: (q, i, 0))),
        out_shape=jax.ShapeDtypeStruct((N_DEV // 2, r, W), src.dtype), compiler_params=_cp("parallel", "parallel"),
        name=name)(core, src.reshape(N_DEV // 2, 2, r, W), got)


def chip_exchange(groups):
    srcs = _sources(groups)
    ns, ng = len(srcs), len(groups)

    def body(*refs):
        outs = refs[ns:ns + ng]
        send_sems, recv_sems, local_sems = refs[ns + ng:]
        x, y, c = lax.axis_index("x"), lax.axis_index("y"), lax.axis_index("c")
        me = 2 * x + y
        layout = _layout(groups, refs[:ns])
        mine = [pltpu.make_async_copy(ref.at[me], outs[g].at[me, pl.ds(off, r)], local_sems.at[i])
                for i, (g, ref, off, r) in enumerate(layout)]
        for cp in mine:
            cp.start()
        slots = []
        for flip in range(1, N_DEV // 2):
            px, py = (1 - x if flip & 2 else x), (1 - y if flip & 1 else y)
            peer, pq = (px, py, c), 2 * px + py
            for (g, ref, off, r) in layout:
                k = 3 * g + flip - 1
                pltpu.make_async_remote_copy(
                    src_ref=ref.at[pq], dst_ref=outs[g].at[me, pl.ds(off, r)], send_sem=send_sems.at[k],
                    recv_sem=recv_sems.at[k], device_id=peer, device_id_type=MESH).start()
            for g in range(ng):
                k = 3 * g + flip - 1
                slots.append(pltpu.make_async_remote_copy(
                    src_ref=outs[g].at[pq], dst_ref=outs[g].at[pq], send_sem=send_sems.at[k],
                    recv_sem=recv_sems.at[k], device_id=peer, device_id_type=MESH))
        for w in slots:
            w.wait_recv()
        for w in slots:
            w.wait_send()
        for cp in mine:
            cp.wait()

    any_spec = pl.BlockSpec(memory_space=pl.ANY)
    return pl.pallas_call(
        body, in_specs=[any_spec] * ns, out_specs=[any_spec] * ng,
        out_shape=[jax.ShapeDtypeStruct((N_DEV // 2, r, w), dt) for (r, w, dt, _) in groups],
        scratch_shapes=[pltpu.SemaphoreType.DMA((3 * ng,)), pltpu.SemaphoreType.DMA((3 * ng,)),
                        pltpu.SemaphoreType.DMA((ns,))],
        name="chip_exchange")(*srcs)


def _row_tile(rows, off, target):
    for t in (1024, 512, 256, 128, 64, 32, 16, 8):
        if t <= target and rows % t == 0 and off % t == 0:
            return t
    raise ValueError((rows, off))


def adamw(recv, off, w, m, v, *, name):
    rows, W = w.shape
    nslot = recv.shape[0]
    tr = _row_tile(rows, off, 256)
    o = off // tr
    c1 = 1.0 - ADAM_B1 ** ADAM_STEP
    c2 = 1.0 - ADAM_B2 ** ADAM_STEP

    def body(r_ref, w_ref, m_ref, v_ref, g_ref, d_ref, mo_ref, vo_ref):
        g = r_ref[0].astype(F32)
        for j in range(1, nslot):
            g = g + r_ref[j].astype(F32)
        mn = ADAM_B1 * m_ref[...] + (1.0 - ADAM_B1) * g
        vn = ADAM_B2 * v_ref[...] + (1.0 - ADAM_B2) * (g * g)
        g_ref[...] = g
        mo_ref[...] = mn
        vo_ref[...] = vn
        d_ref[...] = -ADAM_LR * ((mn / c1) / (jnp.sqrt(vn / c2) + ADAM_EPS) + ADAM_WD * w_ref[...])

    spec = pl.BlockSpec((tr, W), lambda i: (i, 0))
    return pl.pallas_call(
        body, grid=(rows // tr,), in_specs=[pl.BlockSpec((nslot, tr, W), lambda i: (0, i + o, 0)), spec, spec, spec],
        out_specs=[spec] * 4, out_shape=[jax.ShapeDtypeStruct((rows, W), F32)] * 4,
        compiler_params=_cp("parallel"), name=name)(recv, w, m, v)


def join_columns(gathered, off, K, *, name):
    _, _, n = gathered.shape
    tr = _row_tile(K, off, 256)
    o = off // tr

    def body(i_ref, o_ref):
        for d in range(N_DEV):
            o_ref[:, d * n:(d + 1) * n] = i_ref[d]

    return pl.pallas_call(
        body, grid=(K // tr,), in_specs=[pl.BlockSpec((N_DEV, tr, n), lambda i: (0, i + o, 0))],
        out_specs=pl.BlockSpec((tr, N_DEV * n), lambda i: (i, 0)),
        out_shape=jax.ShapeDtypeStruct((K, N_DEV * n), gathered.dtype), compiler_params=_cp("parallel"),
        name=name)(gathered)


def split_columns(full, *, name):
    K, N = full.shape
    n = N // N_DEV
    tr = _row_tile(K, 0, 256)

    def body(i_ref, o_ref):
        for d in range(N_DEV):
            o_ref[d] = i_ref[:, d * n:(d + 1) * n].astype(o_ref.dtype)

    return pl.pallas_call(
        body, grid=(K // tr,), in_specs=[pl.BlockSpec((tr, N), lambda i: (i, 0))],
        out_specs=pl.BlockSpec((N_DEV, tr, n), lambda i: (0, i, 0)),
        out_shape=jax.ShapeDtypeStruct((N_DEV, K, n), BF16), compiler_params=_cp("parallel"), name=name)(full)


def _pack(arrs, dtype, row_mult):
    flat = jnp.concatenate([a.reshape(-1).astype(dtype) for a in arrs])
    rows = -(-flat.shape[0] // PACK_W)
    rows = -(-rows // row_mult) * row_mult
    return jnp.pad(flat, (0, rows * PACK_W - flat.shape[0])).reshape(rows, PACK_W)


def _pack_dev(arrs, dtype, row_mult):
    flat = jnp.concatenate([a.reshape(N_DEV, -1).astype(dtype) for a in arrs], axis=1)
    rows = -(-flat.shape[1] // PACK_W)
    rows = -(-rows // row_mult) * row_mult
    return jnp.pad(flat, ((0, 0), (0, rows * PACK_W - flat.shape[1]))).reshape(N_DEV, rows, PACK_W)


def _unpack(buf, shapes):
    lead = buf.shape[:-2]
    flat = buf.reshape(lead + (-1,))
    outs, off = [], 0
    for s in shapes:
        n = math.prod(s)
        outs.append(flat[..., off:off + n].reshape(lead + tuple(s)))
        off += n
    return outs


def _join(g, axis):
    g = jnp.moveaxis(g, 0, axis)
    return g.reshape(g.shape[:axis] + (g.shape[axis] * g.shape[axis + 1],) + g.shape[axis + 2:])


def _split(full, axis):
    s = full.shape
    g = full.reshape(s[:axis] + (N_DEV, s[axis] // N_DEV) + s[axis + 1:])
    return jnp.moveaxis(g, axis, 0)


def kernel(x, p, a_pw1_w, a_pw1_b, a_dw_w, a_dw_b, a_ln_g, a_ln_b, a_pw2_w, a_pw2_b, b_wq, kv_wk, kv_wv, b_wo, ln_mix_g, ln_mix_b, ffn_w_up, ffn_w_gate, ffn_conv_w, ffn_conv_b, ffn_w_down, ple_w_gate, ple_w_proj, ln_ffn_g, ln_ffn_b, loss_target, m_a_pw1_w, m_a_pw1_b, m_a_dw_w, m_a_dw_b, m_a_ln_g, m_a_ln_b, m_a_pw2_w, m_a_pw2_b, m_b_wq, m_kv_wk, m_kv_wv, m_b_wo, m_ln_mix_g, m_ln_mix_b, m_ffn_w_up, m_ffn_w_gate, m_ffn_conv_w, m_ffn_conv_b, m_ffn_w_down, m_ple_w_gate, m_ple_w_proj, m_ln_ffn_g, m_ln_ffn_b, v_a_pw1_w, v_a_pw1_b, v_a_dw_w, v_a_dw_b, v_a_ln_g, v_a_ln_b, v_a_pw2_w, v_a_pw2_b, v_b_wq, v_kv_wk, v_kv_wv, v_b_wo, v_ln_mix_g, v_ln_mix_b, v_ffn_w_up, v_ffn_w_gate, v_ffn_conv_w, v_ffn_conv_b, v_ffn_w_down, v_ple_w_gate, v_ple_w_proj, v_ln_ffn_g, v_ln_ffn_b):
    local = dict(a_pw1_w=a_pw1_w, a_pw1_b=a_pw1_b, a_dw_w=a_dw_w, a_dw_b=a_dw_b, a_ln_g=a_ln_g, a_ln_b=a_ln_b, a_pw2_w=a_pw2_w, a_pw2_b=a_pw2_b, b_wq=b_wq, kv_wk=kv_wk, kv_wv=kv_wv, b_wo=b_wo, ln_mix_g=ln_mix_g, ln_mix_b=ln_mix_b, ffn_w_up=ffn_w_up, ffn_w_gate=ffn_w_gate, ffn_conv_w=ffn_conv_w, ffn_conv_b=ffn_conv_b, ffn_w_down=ffn_w_down, ple_w_gate=ple_w_gate, ple_w_proj=ple_w_proj, ln_ffn_g=ln_ffn_g, ln_ffn_b=ln_ffn_b)
    mom1 = dict(a_pw1_w=m_a_pw1_w, a_pw1_b=m_a_pw1_b, a_dw_w=m_a_dw_w, a_dw_b=m_a_dw_b, a_ln_g=m_a_ln_g, a_ln_b=m_a_ln_b, a_pw2_w=m_a_pw2_w, a_pw2_b=m_a_pw2_b, b_wq=m_b_wq, kv_wk=m_kv_wk, kv_wv=m_kv_wv, b_wo=m_b_wo, ln_mix_g=m_ln_mix_g, ln_mix_b=m_ln_mix_b, ffn_w_up=m_ffn_w_up, ffn_w_gate=m_ffn_w_gate, ffn_conv_w=m_ffn_conv_w, ffn_conv_b=m_ffn_conv_b, ffn_w_down=m_ffn_w_down, ple_w_gate=m_ple_w_gate, ple_w_proj=m_ple_w_proj, ln_ffn_g=m_ln_ffn_g, ln_ffn_b=m_ln_ffn_b)
    mom2 = dict(a_pw1_w=v_a_pw1_w, a_pw1_b=v_a_pw1_b, a_dw_w=v_a_dw_w, a_dw_b=v_a_dw_b, a_ln_g=v_a_ln_g, a_ln_b=v_a_ln_b, a_pw2_w=v_a_pw2_w, a_pw2_b=v_a_pw2_b, b_wq=v_b_wq, kv_wk=v_kv_wk, kv_wv=v_kv_wv, b_wo=v_b_wo, ln_mix_g=v_ln_mix_g, ln_mix_b=v_ln_mix_b, ffn_w_up=v_ffn_w_up, ffn_w_gate=v_ffn_w_gate, ffn_conv_w=v_ffn_conv_w, ffn_conv_b=v_ffn_conv_b, ffn_w_down=v_ffn_w_down, ple_w_gate=v_ple_w_gate, ple_w_proj=v_ple_w_proj, ln_ffn_g=v_ln_ffn_g, ln_ffn_b=v_ln_ffn_b)
    small_names = [n for n, _ in SMALL]
    small_shapes = [local[n].shape for n in small_names]
    repl_shapes = [local[n].shape for n in REPL]

    widths = sorted({local[n].shape[-1] for n, _ in BIG}, reverse=True)
    groups = {w: [n for n, _ in BIG if local[n].shape[-1] == w] for w in widths}
    offset, rows_of = {}, {}
    for w, names in groups.items():
        off = 0
        for n in names:
            offset[n], rows_of[n] = off, math.prod(local[n].shape[:-1])
            off += rows_of[n]
    sends = [jnp.concatenate([local[n].reshape(-1, w).astype(BF16) for n in names]) for w, names in groups.items()]
    gathered = all_gather(sends + [_pack([local[n] for n in small_names], F32, SUBLANES)])
    gath = dict(zip(widths, gathered[:-1]))
    W = {}
    for n, ax in BIG:
        w = local[n].shape[-1]
        nl = local[n].shape[0] if local[n].ndim == 3 else 1
        per = rows_of[n] // nl
        if ax == local[n].ndim - 1:
            W[n] = [join_columns(gath[w], offset[n] + l * per, per, name=f"join_{n}_{l}") for l in range(nl)]
        else:
            W[n] = [gath[w][:, offset[n] + l * per:offset[n] + (l + 1) * per].reshape(N_DEV * per, w) for l in range(nl)]
    for n in ("kv_wk", "kv_wv"):
        W[n] = W[n][0]
    W.update({n: _join(g, ax) for (n, ax), g in zip(SMALL, _unpack(gathered[-1], small_shapes))})
    W.update({n: local[n] for n in REPL})

    xs = x[0]
    S, D = xs.shape
    x_in, r1s, x1s, r2s, us, gps, gs, hhs, pgls, pps = [], [], [], [], [], [], [], [], [], []
    h1s, h2s, h3s, h5s, qs, os_, tots = {}, {}, {}, {}, {}, {}, {}
    kk = vv = None
    for i in range(DEPTH):
        x_in.append(xs)
        if i < N_A:
            h1 = mm(xs, W["a_pw1_w"][i], "nn", bias=W["a_pw1_b"][i][None], name=f"pw1_{i}")
            h2 = glu_fwd(h1, name=f"glu_{i}")
            h3, h5 = conv_ln_silu_fwd(h2, W["a_dw_w"][i], W["a_dw_b"][i][None], W["a_ln_g"][i][None],
                                      W["a_ln_b"][i][None], name=f"dwconv_{i}")
            mix = mm(h5, W["a_pw2_w"][i], "nn", bias=W["a_pw2_b"][i][None], name=f"pw2_{i}")
            h1s[i], h2s[i], h3s[i], h5s[i] = h1, h2, h3, h5
        else:
            j = i - N_A
            if kk is None:
                kk = mm(xs, W["kv_wk"], "nn", out_dtype=BF16, name="proj_k")
                vv = mm(xs, W["kv_wv"], "nn", out_dtype=BF16, name="proj_v")
            q = mm(xs, W["b_wq"][j], "nn", out_dtype=BF16, name=f"proj_q_{i}")
            o, tot, seen = attn_fwd(q, kk, vv, name=f"attn_{i}")
            mix = mm(o, W["b_wo"][j], "nn", name=f"proj_o_{i}")
            qs[i], os_[i], tots[i] = q, o, (tot, seen)
        r1, x1 = res_ln(xs, mix, W["ln_mix_g"][i][None], W["ln_mix_b"][i][None], name=f"ln_mix_{i}")
        u = mm(x1, W["ffn_w_up"][i], "nn", out_dtype=BF16, name=f"ffn_up_{i}")
        gp = mm(x1, W["ffn_w_gate"][i], "nn", out_dtype=BF16, name=f"ffn_gate_{i}")
        g, hh = conv_act_fwd(gp, u, W["ffn_conv_w"][i], W["ffn_conv_b"][i][None], name=f"ffn_conv_{i}")
        f = mm(hh, W["ffn_w_down"][i], "nn", name=f"ffn_down_{i}")
        pgl = mm(x1, W["ple_w_gate"][i], "nn", name=f"ple_gate_{i}")
        pp = mm(p[i, 0], W["ple_w_proj"][i], "nn", name=f"ple_proj_{i}")
        r2, xs = res_ln(x1, f, W["ln_ffn_g"][i][None], W["ln_ffn_b"][i][None], ple=(pgl, pp), name=f"ln_ffn_{i}")
        for lst, val in ((r1s, r1), (x1s, x1), (r2s, r2), (us, u), (gps, gp), (gs, g), (hhs, hh), (pgls, pgl), (pps, pp)):
            lst.append(val)

    dx, loss_part = loss_grad(xs, loss_target[0], name="loss")
    G = {n: [None] * local[n].shape[0] for n in WEIGHTS if n not in ("kv_wk", "kv_wv")}
    dk = dv = None
    for i in reversed(range(DEPTH)):
        x1 = x1s[i]
        dr2, dpp, dpgl, G["ln_ffn_g"][i], G["ln_ffn_b"][i] = ln_ple_bwd(r2s[i], W["ln_ffn_g"][i][None], dx, pgls[i], pps[i],
                                                                      name=f"ln_ffn_bwd_{i}")
        dhh = mm(dr2, W["ffn_w_down"][i], "nt", out_dtype=BF16, name=f"ffn_down_dx_{i}")
        G["ffn_w_down"][i] = mm(hhs[i], dr2, "tn", out_dtype=BF16, name=f"ffn_down_dw_{i}")
        G["ple_w_proj"][i] = mm(p[i, 0], dpp, "tn", out_dtype=BF16, name=f"ple_proj_dw_{i}")
        G["ple_w_gate"][i] = mm(x1, dpgl, "tn", out_dtype=BF16, name=f"ple_gate_dw_{i}")
        du, dgp, G["ffn_conv_w"][i], G["ffn_conv_b"][i] = ffn_gate_bwd(dhh, us[i], gs[i], gps[i], W["ffn_conv_w"][i],
                                                                       name=f"ffn_gate_bwd_{i}")
        G["ffn_w_up"][i] = mm(x1, du, "tn", out_dtype=BF16, name=f"ffn_up_dw_{i}")
        G["ffn_w_gate"][i] = mm(x1, dgp, "tn", out_dtype=BF16, name=f"ffn_gate_dw_{i}")
        dx1 = mm(du, W["ffn_w_up"][i], "nt", add=dr2, add_scale=DN_ALPHA, name=f"ffn_up_dx_{i}")
        dx1 = mm(dgp, W["ffn_w_gate"][i], "nt", add=dx1, name=f"ffn_gate_dx_{i}")
        dx1 = mm(dpgl, W["ple_w_gate"][i], "nt", add=dx1, name=f"ple_gate_dx_{i}")
        dr1, G["ln_mix_g"][i], G["ln_mix_b"][i], dr1_sum = ln_bwd(r1s[i], W["ln_mix_g"][i][None], dx1, name=f"ln_mix_bwd_{i}")
        if i < N_A:
            G["a_pw2_w"][i] = mm(h5s[i], dr1, "tn", out_dtype=BF16, name=f"pw2_dw_{i}")
            G["a_pw2_b"][i] = dr1_sum
            dh5 = mm(dr1, W["a_pw2_w"][i], "nt", name=f"pw2_dx_{i}")
            dh3, G["a_ln_g"][i], G["a_ln_b"][i] = ln_silu_bwd(h3s[i], W["a_ln_g"][i][None], W["a_ln_b"][i][None], dh5,
                                                             name=f"dwconv_ln_bwd_{i}")
            dh2 = conv_bwd_x(dh3, W["a_dw_w"][i], out_dtype=F32, name=f"dwconv_dx_{i}")
            G["a_dw_w"][i], G["a_dw_b"][i] = conv_bwd_w(h2s[i], dh3, CONV_W, name=f"dwconv_dw_{i}")
            dh1, G["a_pw1_b"][i] = glu_bwd(h1s[i], dh2, name=f"glu_bwd_{i}")
            G["a_pw1_w"][i] = mm(x_in[i], dh1, "tn", out_dtype=BF16, name=f"pw1_dw_{i}")
            dx = mm(dh1, W["a_pw1_w"][i], "nt", add=dr1, add_scale=DN_ALPHA, name=f"pw1_dx_{i}")
        else:
            j = i - N_A
            G["b_wo"][j] = mm(os_[i], dr1, "tn", out_dtype=BF16, name=f"proj_o_dw_{i}")
            do = mm(dr1, W["b_wo"][j], "nt", out_dtype=BF16, name=f"proj_o_dx_{i}")
            dq, dk, dv = attn_bwd(qs[i], kk, vv, *tots[i], do, dk, dv, name=f"attn_bwd_{i}")
            G["b_wq"][j] = mm(x_in[i], dq, "tn", out_dtype=BF16, name=f"proj_q_dw_{i}")
            dx = mm(dq, W["b_wq"][j], "nt", add=dr1, add_scale=DN_ALPHA, name=f"proj_q_dx_{i}")
            if j == 0:
                G["kv_wk"] = mm(x_in[i], dk, "tn", out_dtype=BF16, name="proj_k_dw")
                G["kv_wv"] = mm(x_in[i], dv, "tn", out_dtype=BF16, name="proj_v_dw")
                dx = mm(dk, W["kv_wk"], "nt", add=dx, name="proj_k_dx")
                dx = mm(dv, W["kv_wv"], "nt", add=dx, name="proj_v_dx")
    grad_x = dx[None]
    shard_axis = dict(BIG + SMALL)
    for n in small_names + list(REPL):
        full = list(local[n].shape)
        if n in shard_axis:
            full[shard_axis[n]] *= N_DEV
        G[n] = jnp.stack(G[n]).reshape(full)

    n_small = sum(math.prod(s) for s in small_shapes)
    n_repl = sum(math.prod(s) for s in repl_shapes)
    repl_flat = jnp.concatenate([G[n].reshape(-1) for n in REPL] + [loss_part.reshape(-1)[:1]])
    send_small = _pack_dev([_split(G[n], ax) for n, ax in SMALL] + [jnp.broadcast_to(repl_flat, (N_DEV, n_repl + 1))],
                           F32, SUBLANES)
    ex_groups = []
    for w, names in groups.items():
        lst = []
        for n in names:
            layers = G[n] if isinstance(G[n], list) else [G[n]]
            per = rows_of[n] // len(layers)
            for l, g in enumerate(layers):
                if shard_axis[n] == local[n].ndim - 1:
                    src = split_columns(g, name=f"split_{n}_{l}")
                else:
                    src = g.reshape(N_DEV, per, w)
                lst.append((src, offset[n] + l * per))
        ex_groups.append((sum(rows_of[n] for n in names), w, BF16, lst))
    ex_groups.append((send_small.shape[1], PACK_W, F32, [(send_small, 0)]))
    gots = pair_exchange(ex_groups)
    core = lax.axis_index("c").astype(jnp.int32).reshape(1)
    sum_groups = [(rows, w, dt, [(pair_sum(src, got, off, core, name=f"pair_sum_{gi}_{si}"), off) for si, (src, off) in enumerate(lst)])
                  for gi, ((rows, w, dt, lst), got) in enumerate(zip(ex_groups, gots))]
    recvs = chip_exchange(sum_groups)
    recv = dict(zip(widths, recvs[:-1]))
    recv_small = recvs[-1]

    out = {}
    for n, _ in BIG:
        w = local[n].shape[-1]
        res = adamw(recv[w], offset[n], local[n].reshape(-1, w), mom1[n].reshape(-1, w), mom2[n].reshape(-1, w),
                    name=f"adamw_{n}")
        out[n] = [r.reshape(local[n].shape) for r in res]

    def state(d):
        small = _pack([d[n] for n in small_names] + [d[n] for n in REPL], F32, SUBLANES)
        return jnp.pad(small, ((0, recv_small.shape[1] - small.shape[0]), (0, 0)))

    out_small = adamw(recv_small, 0, state(local), state(mom1), state(mom2), name="adamw_vectors")
    loss = out_small[0].reshape(-1)[n_small + n_repl]
    vecs = [dict(zip(small_names + list(REPL), _unpack(o, small_shapes + repl_shapes))) for o in out_small]
    per_kind = [[out[n][kind] if n in out else vecs[kind][n] for n in WEIGHTS] for kind in range(4)]
    grads, deltas, new_m, new_v = per_kind
    return (loss, grad_x, *grads, *deltas, *new_m, *new_v)
```

```python
import functools
import math

import jax
import jax.numpy as jnp
from jax import lax
from jax.experimental import pallas as pl
from jax.experimental.pallas import tpu as pltpu

F32 = jnp.float32
BF16 = jnp.bfloat16
MESH = pl.DeviceIdType.MESH

N_DEV = 8
DEPTH = 4
N_A = 2
HEAD_DIM = 64
Q_BLOCK = 128
CONV_W = 31
FFN_CONV_W = 3
LN_EPS = 1e-5
DN_ALPHA = (2.0 * DEPTH) ** 0.25
ADAM_LR = 0.001
ADAM_B1 = 0.9
ADAM_B2 = 0.999
ADAM_EPS = 1e-08
ADAM_WD = 0.01
ADAM_STEP = 10

LANES = 128
SUBLANES = 8
PACK_W = 1024
VMEM_LIMIT = 56 * 1024 * 1024

BIG = (("a_pw1_w", 2), ("a_pw2_w", 1), ("b_wq", 1), ("kv_wk", 0), ("kv_wv", 0), ("b_wo", 1),
       ("ffn_w_up", 2), ("ffn_w_gate", 2), ("ffn_w_down", 1), ("ple_w_gate", 1), ("ple_w_proj", 2))
SMALL = (("a_pw1_b", 1), ("a_dw_w", 2), ("a_dw_b", 1), ("a_ln_g", 1), ("a_ln_b", 1), ("a_pw2_b", 1),
         ("ffn_conv_w", 2))
REPL = ("ln_mix_g", "ln_mix_b", "ffn_conv_b", "ln_ffn_g", "ln_ffn_b")
WEIGHTS = ("a_pw1_w", "a_pw1_b", "a_dw_w", "a_dw_b", "a_ln_g", "a_ln_b", "a_pw2_w", "a_pw2_b", "b_wq", "kv_wk",
           "kv_wv", "b_wo", "ln_mix_g", "ln_mix_b", "ffn_w_up", "ffn_w_gate", "ffn_conv_w", "ffn_conv_b",
           "ffn_w_down", "ple_w_gate", "ple_w_proj", "ln_ffn_g", "ln_ffn_b")


def _cp(*sem):
    return pltpu.CompilerParams(dimension_semantics=sem, vmem_limit_bytes=VMEM_LIMIT)


def _pick(dim, target, align=LANES):
    if dim <= target:
        return dim
    t = (target // align) * align
    while t >= align:
        if dim % t == 0:
            return t
        t -= align
    return dim


MM_ROWS = 512
MM_ROWS_TN = 1536
MM_COLS = 1536
MM_DEPTH = 2816
MM_DEPTH_TN = 1536

_DOT_DIMS = {"nn": (((1,), (0,)), ((), ())), "nt": (((1,), (1,)), ((), ())), "tn": (((0,), (0,)), ((), ()))}


def mm(a, b, mode, *, bias=None, add=None, add_scale=1.0, out_dtype=F32, name):
    if mode == "tn":
        K, M = a.shape
    else:
        M, K = a.shape
    N = b.shape[0] if mode == "nt" else b.shape[1]
    tm = _pick(M, MM_ROWS_TN if mode == "tn" else MM_ROWS)
    tn = _pick(N, MM_COLS)
    tk = _pick(K, MM_DEPTH_TN if mode == "tn" else MM_DEPTH)
    nk = K // tk
    dims = _DOT_DIMS[mode]

    def body(*refs):
        a_ref, b_ref = refs[0], refs[1]
        pos = 2
        bias_ref = add_ref = None
        if bias is not None:
            bias_ref = refs[pos]
            pos += 1
        if add is not None:
            add_ref = refs[pos]
            pos += 1
        o_ref, acc_ref = refs[pos], refs[pos + 1]
        k = pl.program_id(2)

        @pl.when(k == 0)
        def _():
            acc_ref[...] = jnp.zeros_like(acc_ref)

        acc_ref[...] += lax.dot_general(a_ref[...].astype(BF16), b_ref[...].astype(BF16), dims,
                                        preferred_element_type=F32)

        @pl.when(k == nk - 1)
        def _():
            r = acc_ref[...]
            if bias_ref is not None:
                r = r + bias_ref[...]
            if add_ref is not None:
                r = r + add_scale * add_ref[...].astype(F32)
            o_ref[...] = r.astype(o_ref.dtype)

    a_spec = pl.BlockSpec((tk, tm), lambda j, i, k: (k, i)) if mode == "tn" else pl.BlockSpec((tm, tk), lambda j, i, k: (i, k))
    b_spec = pl.BlockSpec((tn, tk), lambda j, i, k: (j, k)) if mode == "nt" else pl.BlockSpec((tk, tn), lambda j, i, k: (k, j))
    in_specs, args = [a_spec, b_spec], [a, b]
    if bias is not None:
        in_specs.append(pl.BlockSpec((1, tn), lambda j, i, k: (0, j)))
        args.append(bias)
    if add is not None:
        in_specs.append(pl.BlockSpec((tm, tn), lambda j, i, k: (i, j)))
        args.append(add)
    return pl.pallas_call(
        body, grid=(N // tn, M // tm, nk), in_specs=in_specs,
        out_specs=pl.BlockSpec((tm, tn), lambda j, i, k: (i, j)),
        out_shape=jax.ShapeDtypeStruct((M, N), out_dtype),
        scratch_shapes=[pltpu.VMEM((tm, tn), F32)],
        compiler_params=_cp("parallel", "parallel", "arbitrary"), name=name)(*args)


def _rows(body, *, n_rows, tm, row_ins, full_ins=(), row_outs=(), acc_outs=(), scratch=(), reverse=False, name):
    n = n_rows // tm

    def rmap(i):
        return (n - 1 - i, 0) if reverse else (i, 0)

    in_specs = [pl.BlockSpec((tm, a.shape[1]), rmap) for a in row_ins]
    in_specs += [pl.BlockSpec(a.shape, lambda i, nd=a.ndim: (0,) * nd) for a in full_ins]
    out_shape = [jax.ShapeDtypeStruct((n_rows, w), dt) for (w, dt) in row_outs]
    out_shape += [jax.ShapeDtypeStruct(s, dt) for (s, dt) in acc_outs]
    out_specs = [pl.BlockSpec((tm, w), rmap) for (w, dt) in row_outs]
    out_specs += [pl.BlockSpec(s, lambda i, nd=len(s): (0,) * nd) for (s, dt) in acc_outs]
    return pl.pallas_call(
        functools.partial(body, n), grid=(n,), in_specs=in_specs, out_specs=out_specs, out_shape=out_shape,
        scratch_shapes=list(scratch), compiler_params=_cp("arbitrary"), name=name)(*row_ins, *full_ins)


def _sigmoid(x):
    return 1.0 / (1.0 + jnp.exp(-x))


def _ln_hat(r):
    mu = jnp.mean(r, axis=-1, keepdims=True)
    xc = r - mu
    var = jnp.mean(xc * xc, axis=-1, keepdims=True)
    rstd = lax.rsqrt(var + LN_EPS)
    return xc * rstd, rstd


def _ln_back(xhat, rstd, g, dy):
    dxh = dy * g
    m1 = jnp.mean(dxh, axis=-1, keepdims=True)
    m2 = jnp.mean(dxh * xhat, axis=-1, keepdims=True)
    return rstd * (dxh - m1 - xhat * m2)


def _colsum(x):
    return jnp.sum(x, axis=0, keepdims=True)


def _acc(i, ref, val):
    @pl.when(i == 0)
    def _():
        ref[...] = val

    @pl.when(i > 0)
    def _():
        ref[...] += val


def res_ln(x, mix, g, b, *, ple=None, name):
    S, D = x.shape

    def body(n, *refs):
        if ple is None:
            x_ref, m_ref, g_ref, b_ref, r_ref, y_ref = refs
            r = DN_ALPHA * x_ref[...] + m_ref[...]
        else:
            x_ref, m_ref, pgl_ref, pp_ref, g_ref, b_ref, r_ref, y_ref = refs
            r = DN_ALPHA * x_ref[...] + m_ref[...] + _sigmoid(pgl_ref[...].astype(F32)) * pp_ref[...].astype(F32)
        xhat, _ = _ln_hat(r)
        r_ref[...] = r
        y_ref[...] = xhat * g_ref[...] + b_ref[...]

    row_ins = [x, mix] + ([] if ple is None else list(ple))
    return _rows(body, n_rows=S, tm=_pick(S, 256, SUBLANES), row_ins=row_ins, full_ins=[g, b],
                 row_outs=[(D, F32), (D, F32)], name=name)


def ln_bwd(r, g, dy, *, name):
    S, D = r.shape

    def body(n, r_ref, dy_ref, g_ref, dr_ref, dg_ref, db_ref, ds_ref):
        i = pl.program_id(0)
        xhat, rstd = _ln_hat(r_ref[...])
        dy_v = dy_ref[...]
        dr = _ln_back(xhat, rstd, g_ref[...], dy_v)
        dr_ref[...] = dr
        _acc(i, dg_ref, _colsum(dy_v * xhat))
        _acc(i, db_ref, _colsum(dy_v))
        _acc(i, ds_ref, _colsum(dr))

    return _rows(body, n_rows=S, tm=_pick(S, 256, SUBLANES), row_ins=[r, dy], full_ins=[g],
                 row_outs=[(D, F32)], acc_outs=[((1, D), F32)] * 3, name=name)


def glu_fwd(h1, *, name):
    S, D2 = h1.shape
    D = D2 // 2

    def body(n, h_ref, o_ref):
        o_ref[...] = h_ref[:, :D].astype(F32) * _sigmoid(h_ref[:, D:].astype(F32))

    return _rows(body, n_rows=S, tm=_pick(S, 256, SUBLANES), row_ins=[h1], row_outs=[(D, F32)], name=name)[0]


def glu_bwd(h1, dh2, *, name):
    S, D2 = h1.shape
    D = D2 // 2

    def body(n, h_ref, d_ref, o_ref, s_ref):
        i = pl.program_id(0)
        a, sg, d = h_ref[:, :D].astype(F32), _sigmoid(h_ref[:, D:].astype(F32)), d_ref[...]
        da = d * sg
        dg = d * a * sg * (1.0 - sg)
        o_ref[:, :D] = da.astype(o_ref.dtype)
        o_ref[:, D:] = dg.astype(o_ref.dtype)
        _acc(i, s_ref, jnp.concatenate([_colsum(da), _colsum(dg)], axis=1))

    return _rows(body, n_rows=S, tm=_pick(S, 256, SUBLANES), row_ins=[h1, dh2], row_outs=[(D2, BF16)],
                 acc_outs=[((1, D2), F32)], name=name)


CONV_ROWS = 32
CONV_LANES = 256


def _halo(k):
    return -(-(k - 1) // SUBLANES) * SUBLANES


def _phases(offs):
    return sorted({o % SUBLANES for o in offs} - {0})


def _shift_scratch(offs, n_rows, width):
    return pltpu.VMEM((max(len(_phases(offs)), 1), n_rows, width), F32)


def _make_shifted(buf_ref, sh_ref, offs):
    n = buf_ref.shape[0] - SUBLANES
    for p, b in enumerate(_phases(offs)):
        sh_ref[p, pl.ds(0, n), :] = buf_ref[pl.ds(b, n), :]


def _tap(buf_ref, sh_ref, offs, k, rc, rows, lc, lw):
    b = offs[k] % SUBLANES
    src = buf_ref if b == 0 else sh_ref.at[_phases(offs).index(b)]
    return src[pl.ds(offs[k] - b + rc, rows), pl.ds(lc, lw)]


def _conv_taps(buf_ref, sh_ref, w_ref, offs, tm, width, emit):
    _make_shifted(buf_ref, sh_ref, offs)
    rows = min(CONV_ROWS, tm)
    for lc in range(0, width, CONV_LANES):
        lw = min(CONV_LANES, width - lc)
        for rc in range(0, tm, rows):
            acc = None
            for k in range(len(offs)):
                t = _tap(buf_ref, sh_ref, offs, k, rc, rows, lc, lw) * w_ref[pl.ds(k, 1), pl.ds(lc, lw)]
                acc = t if acc is None else acc + t
            emit(rc, lc, lw, rows, acc)


def _fill_causal(i, buf_ref, x_ref, halo, tm):
    @pl.when(i == 0)
    def _():
        buf_ref[pl.ds(0, halo), :] = jnp.zeros((halo, buf_ref.shape[1]), F32)

    @pl.when(i > 0)
    def _():
        buf_ref[pl.ds(0, halo), :] = buf_ref[pl.ds(tm, halo), :]

    buf_ref[pl.ds(halo, tm), :] = x_ref[...].astype(F32)


def conv_ln_silu_fwd(x, w, b, g, beta, *, name):
    S, C = x.shape
    K = w.shape[0]
    halo = _halo(K)
    tm = _pick(S, 256, SUBLANES)
    offs = [halo - (K - 1) + k for k in range(K)]

    def body(n, x_ref, w_ref, b_ref, g_ref, beta_ref, h3_ref, h5_ref, buf_ref, sh_ref):
        i = pl.program_id(0)
        _fill_causal(i, buf_ref, x_ref, halo, tm)

        def emit(rc, lc, lw, rows, acc):
            h3_ref[pl.ds(rc, rows), pl.ds(lc, lw)] = acc + b_ref[:, pl.ds(lc, lw)]

        _conv_taps(buf_ref, sh_ref, w_ref, offs, tm, C, emit)
        xhat, _ = _ln_hat(h3_ref[...])
        h4 = xhat * g_ref[...] + beta_ref[...]
        h5_ref[...] = (h4 * _sigmoid(h4)).astype(h5_ref.dtype)

    return _rows(body, n_rows=S, tm=tm, row_ins=[x], full_ins=[w, b, g, beta], row_outs=[(C, F32), (C, BF16)],
                 scratch=[pltpu.VMEM((tm + halo, C), F32), _shift_scratch(offs, tm + halo, C)], name=name)


def conv_act_fwd(gp, u, w, b, *, name):
    S, C = gp.shape
    K = w.shape[0]
    halo = _halo(K)
    tm = _pick(S, 256, SUBLANES)
    offs = [halo - (K - 1) + k for k in range(K)]

    def body(n, x_ref, u_ref, w_ref, b_ref, g_ref, hh_ref, buf_ref, sh_ref):
        i = pl.program_id(0)
        _fill_causal(i, buf_ref, x_ref, halo, tm)

        def emit(rc, lc, lw, rows, acc):
            gv = acc + b_ref[:, pl.ds(lc, lw)]
            g_ref[pl.ds(rc, rows), pl.ds(lc, lw)] = gv.astype(g_ref.dtype)
            hh_ref[pl.ds(rc, rows), pl.ds(lc, lw)] = (gv * _sigmoid(gv) * u_ref[pl.ds(rc, rows), pl.ds(lc, lw)].astype(F32)).astype(hh_ref.dtype)

        _conv_taps(buf_ref, sh_ref, w_ref, offs, tm, C, emit)

    return _rows(body, n_rows=S, tm=tm, row_ins=[gp, u], full_ins=[w, b], row_outs=[(C, BF16), (C, BF16)],
                 scratch=[pltpu.VMEM((tm + halo, C), F32), _shift_scratch(offs, tm + halo, C)], name=name)


def conv_bwd_x(dy, w, *, out_dtype, name):
    S, C = dy.shape
    K = w.shape[0]
    halo = _halo(K)
    tm = _pick(S, 256, SUBLANES)
    offs = [K - 1 - k for k in range(K)]

    def body(n, dy_ref, w_ref, dx_ref, buf_ref, sh_ref):
        i = pl.program_id(0)

        @pl.when(i == 0)
        def _():
            buf_ref[pl.ds(tm, halo), :] = jnp.zeros((halo, C), F32)

        @pl.when(i > 0)
        def _():
            buf_ref[pl.ds(tm, halo), :] = buf_ref[pl.ds(0, halo), :]

        buf_ref[pl.ds(0, tm), :] = dy_ref[...].astype(F32)

        def emit(rc, lc, lw, rows, acc):
            dx_ref[pl.ds(rc, rows), pl.ds(lc, lw)] = acc.astype(dx_ref.dtype)

        _conv_taps(buf_ref, sh_ref, w_ref, offs, tm, C, emit)

    return _rows(body, n_rows=S, tm=tm, row_ins=[dy], full_ins=[w], row_outs=[(C, out_dtype)],
                 scratch=[pltpu.VMEM((tm + halo, C), F32), _shift_scratch(offs, tm + halo, C)], reverse=True, name=name)[0]


def conv_bwd_w(x, dy, K, *, name):
    S, C = x.shape
    halo = _halo(K)
    tm = _pick(S, 256, SUBLANES)
    offs = [halo - (K - 1) + k for k in range(K)]
    rows = min(CONV_ROWS, tm)

    def body(n, x_ref, dy_ref, dw_ref, db_ref, buf_ref, acc_ref, sh_ref):
        i = pl.program_id(0)
        _fill_causal(i, buf_ref, x_ref, halo, tm)
        _make_shifted(buf_ref, sh_ref, offs)

        @pl.when(i == 0)
        def _():
            acc_ref[...] = jnp.zeros_like(acc_ref)

        for lc in range(0, C, CONV_LANES):
            lw = min(CONV_LANES, C - lc)
            for k in range(K):
                s = None
                for rc in range(0, tm, rows):
                    t = dy_ref[pl.ds(rc, rows), pl.ds(lc, lw)].astype(F32) * _tap(buf_ref, sh_ref, offs, k, rc, rows, lc, lw)
                    s = t if s is None else s + t
                s8 = s[0:SUBLANES]
                for q in range(1, rows // SUBLANES):
                    s8 = s8 + s[q * SUBLANES:(q + 1) * SUBLANES]
                acc_ref[pl.ds(k * SUBLANES, SUBLANES), pl.ds(lc, lw)] += s8
        _acc(i, db_ref, _colsum(dy_ref[...].astype(F32)))

        @pl.when(i == n - 1)
        def _():
            for k in range(K):
                dw_ref[pl.ds(k, 1), :] = _colsum(acc_ref[pl.ds(k * SUBLANES, SUBLANES), :])

    return _rows(body, n_rows=S, tm=tm, row_ins=[x, dy], acc_outs=[((K, C), F32), ((1, C), F32)],
                 scratch=[pltpu.VMEM((tm + halo, C), F32), pltpu.VMEM((K * SUBLANES, C), F32),
                          _shift_scratch(offs, tm + halo, C)], name=name)


def ln_silu_bwd(h3, g, beta, dh5, *, name):
    S, C = h3.shape

    def body(n, h_ref, d_ref, g_ref, beta_ref, dh_ref, dg_ref, db_ref):
        i = pl.program_id(0)
        xhat, rstd = _ln_hat(h_ref[...])
        h4 = xhat * g_ref[...] + beta_ref[...]
        sg = _sigmoid(h4)
        dh4 = d_ref[...] * sg * (1.0 + h4 * (1.0 - sg))
        dh_ref[...] = _ln_back(xhat, rstd, g_ref[...], dh4)
        _acc(i, dg_ref, _colsum(dh4 * xhat))
        _acc(i, db_ref, _colsum(dh4))

    return _rows(body, n_rows=S, tm=_pick(S, 256, SUBLANES), row_ins=[h3, dh5], full_ins=[g, beta],
                 row_outs=[(C, F32)], acc_outs=[((1, C), F32)] * 2, name=name)


def ffn_gate_bwd(dhh, u, g, gp, w, *, name):
    S, C = u.shape
    K = w.shape[0]
    halo = _halo(K)
    tm = _pick(S, 256, SUBLANES)
    offs = [K - 1 - k for k in range(K)]
    rows = min(CONV_ROWS, tm)

    def body(n, d_ref, u_ref, g_ref, gp_ref, w_ref, du_ref, dgp_ref, dw_ref, db_ref, buf_ref, sh_ref, acc_ref):
        i = pl.program_id(0)

        @pl.when(i == 0)
        def _():
            buf_ref[pl.ds(tm, halo), :] = jnp.zeros((halo, C), F32)
            acc_ref[...] = jnp.zeros_like(acc_ref)

        @pl.when(i > 0)
        def _():
            buf_ref[pl.ds(tm, halo), :] = buf_ref[pl.ds(0, halo), :]

        d, gv = d_ref[...].astype(F32), g_ref[...].astype(F32)
        sg = _sigmoid(gv)
        du_ref[...] = (d * gv * sg).astype(du_ref.dtype)
        dg = d * u_ref[...].astype(F32) * sg * (1.0 + gv * (1.0 - sg))
        buf_ref[pl.ds(0, tm), :] = dg
        _acc(i, db_ref, _colsum(dg))

        def emit(rc, lc, lw, nrows, acc):
            dgp_ref[pl.ds(rc, nrows), pl.ds(lc, lw)] = acc.astype(dgp_ref.dtype)

        _conv_taps(buf_ref, sh_ref, w_ref, offs, tm, C, emit)
        for lc in range(0, C, CONV_LANES):
            lw = min(CONV_LANES, C - lc)
            for k in range(K):
                s_ = None
                for rc in range(0, tm, rows):
                    t = gp_ref[pl.ds(rc, rows), pl.ds(lc, lw)].astype(F32) * _tap(buf_ref, sh_ref, offs, k, rc, rows, lc, lw)
                    s_ = t if s_ is None else s_ + t
                s8 = s_[0:SUBLANES]
                for q in range(1, rows // SUBLANES):
                    s8 = s8 + s_[q * SUBLANES:(q + 1) * SUBLANES]
                acc_ref[pl.ds(k * SUBLANES, SUBLANES), pl.ds(lc, lw)] += s8

        @pl.when(i == n - 1)
        def _():
            for k in range(K):
                dw_ref[pl.ds(k, 1), :] = _colsum(acc_ref[pl.ds(k * SUBLANES, SUBLANES), :])

    return _rows(body, n_rows=S, tm=tm, row_ins=[dhh, u, g, gp], full_ins=[w], row_outs=[(C, BF16), (C, BF16)],
                 acc_outs=[((K, C), F32), ((1, C), F32)],
                 scratch=[pltpu.VMEM((tm + halo, C), F32), _shift_scratch(offs, tm + halo, C),
                          pltpu.VMEM((K * SUBLANES, C), F32)], reverse=True, name=name)


def ln_ple_bwd(r, g, dy, pgl, pp, *, name):
    S, D = r.shape

    def body(n, r_ref, dy_ref, l_ref, p_ref, g_ref, dr_ref, dpp_ref, dpl_ref, dg_ref, db_ref):
        i = pl.program_id(0)
        xhat, rstd = _ln_hat(r_ref[...])
        dy_v = dy_ref[...]
        dr = _ln_back(xhat, rstd, g_ref[...], dy_v)
        dr_ref[...] = dr
        sg = _sigmoid(l_ref[...].astype(F32))
        dpp_ref[...] = (dr * sg).astype(dpp_ref.dtype)
        dpl_ref[...] = (dr * p_ref[...].astype(F32) * sg * (1.0 - sg)).astype(dpl_ref.dtype)
        _acc(i, dg_ref, _colsum(dy_v * xhat))
        _acc(i, db_ref, _colsum(dy_v))

    return _rows(body, n_rows=S, tm=_pick(S, 256, SUBLANES), row_ins=[r, dy, pgl, pp], full_ins=[g],
                 row_outs=[(D, F32), (D, BF16), (D, BF16)], acc_outs=[((1, D), F32)] * 2, name=name)


def loss_grad(y, target, *, name):
    S, D = y.shape

    def body(n, y_ref, t_ref, dy_ref, l_ref):
        i = pl.program_id(0)
        e = y_ref[...] - t_ref[...]
        dy_ref[...] = e * (1.0 / D)
        s = jnp.sum(_colsum(e * e), axis=1, keepdims=True) * (0.5 / D)
        _acc(i, l_ref, jnp.broadcast_to(s, (1, LANES)))

    return _rows(body, n_rows=S, tm=_pick(S, 256, SUBLANES), row_ins=[y, target], row_outs=[(D, F32)],
                 acc_outs=[((1, LANES), F32)], name=name)


def _key_step(S):
    return min(512, S // 2)


EXIT_LOG = -110.0


def _attn_consts():
    lane = lax.broadcasted_iota(jnp.int32, (1, LANES), 1)
    heads = (lane < HEAD_DIM, lane >= HEAD_DIM)
    row = lax.broadcasted_iota(jnp.int32, (Q_BLOCK, Q_BLOCK), 0)
    col = lax.broadcasted_iota(jnp.int32, (Q_BLOCK, Q_BLOCK), 1)
    causal = jnp.concatenate([col < row] * 2, axis=0)
    return heads, row, col, causal


def _tri(cond):
    return jnp.where(cond, 1.0, 0.0).astype(BF16)


def _keysum2(x, tri):
    hi = x.astype(BF16)
    lo = (x - hi.astype(F32)).astype(BF16)
    return jnp.dot(jnp.concatenate([hi, lo], axis=1), jnp.concatenate([tri, tri], axis=0),
                   preferred_element_type=F32)


def _stack_heads(x, heads):
    return jnp.concatenate([jnp.where(m, x, jnp.zeros_like(x)) for m in heads], axis=0)


def _log1m_beta(z):
    return -(jnp.maximum(z, 0.0) + jnp.log(1.0 + jnp.exp(-jnp.abs(z))))


def attn_fwd(q, k, v, *, name):
    S, D = q.shape
    nb = S // Q_BLOCK
    tk = _key_step(S)
    nkb = tk // Q_BLOCK
    scale = 1.0 / math.sqrt(HEAD_DIM)

    def body(q_ref, k_ref, v_ref, o_ref, tot_ref, seen_ref, vm_ref):
        heads, row, col, causal = _attn_consts()
        above = _tri(row > col)
        for h in range(2):
            vm_ref[h] = jnp.where(heads[h], v_ref[...], jnp.zeros_like(v_ref[...]))

        def step(sb, carry, qq, nblk, diag):
            acc, cl = carry
            c0 = pl.multiple_of(sb * tk, tk)
            z = lax.dot_general(qq, k_ref[pl.ds(c0, nblk * Q_BLOCK), :], _DOT_DIMS["nt"], preferred_element_type=F32)
            zl, es, rs = [], [], []
            for jb in range(nblk):
                zb = z[:, jb * Q_BLOCK:(jb + 1) * Q_BLOCK]
                lr = _log1m_beta(zb)
                l = jnp.where(causal, lr, 0.0) if diag and jb == nblk - 1 else lr
                zl.append(zb + lr)
                es.append(_keysum2(l, above))
                rs.append(jnp.sum(l, axis=1, keepdims=True))
            a = [None] * nblk
            for jb in reversed(range(nblk)):
                ab = jnp.exp(zl[jb] + es[jb] + cl)
                if diag and jb == nblk - 1:
                    ab = jnp.where(causal, ab, 0.0)
                a[jb] = ab.astype(BF16)
                cl = cl + rs[jb]
            a = jnp.concatenate(a, axis=1)
            for h in range(2):
                acc = acc + jnp.dot(a[h * Q_BLOCK:(h + 1) * Q_BLOCK], vm_ref[h, pl.ds(c0, nblk * Q_BLOCK), :],
                                    preferred_element_type=F32)
            return acc, cl

        def qblock(i, _):
            r0 = pl.multiple_of(i * Q_BLOCK, Q_BLOCK)
            qq = _stack_heads(q_ref[pl.ds(r0, Q_BLOCK), :] * scale, heads)
            last = i // nkb
            carry = (jnp.zeros((Q_BLOCK, LANES), F32), jnp.zeros((2 * Q_BLOCK, 1), F32))
            carry = lax.switch(i % nkb, [functools.partial(step, last, qq=qq, nblk=m + 1, diag=True) for m in range(nkb)],
                               carry)

            def more(c):
                return jnp.logical_and(c[0] < last, jnp.max(c[2]) >= EXIT_LOG)

            def left(c):
                return (c[0] + 1, *step(last - 1 - c[0], c[1:], qq, nkb, False))

            seen, acc, cl = lax.while_loop(more, left, (jnp.int32(0), *carry))
            o_ref[pl.ds(r0, Q_BLOCK), :] = acc.astype(o_ref.dtype)
            tot_ref[pl.ds(r0, Q_BLOCK), :] = jnp.where(heads[0], cl[:Q_BLOCK], cl[Q_BLOCK:])
            seen_ref[pl.ds(pl.multiple_of(i * SUBLANES, SUBLANES), SUBLANES), :] = jnp.full((SUBLANES, LANES), seen, F32)
            return 0

        lax.fori_loop(0, nb, qblock, 0)

    spec = pl.BlockSpec((S, LANES), lambda h: (0, h))
    seen_spec = pl.BlockSpec((nb * SUBLANES, LANES), lambda h: (0, h))
    return pl.pallas_call(body, grid=(D // LANES,), in_specs=[spec] * 3, out_specs=[spec, spec, seen_spec],
                          out_shape=[jax.ShapeDtypeStruct((S, D), BF16), jax.ShapeDtypeStruct((S, D), F32),
                                     jax.ShapeDtypeStruct((nb * SUBLANES, D), F32)],
                          scratch_shapes=[pltpu.VMEM((2, S, LANES), BF16)], compiler_params=_cp("parallel"),
                          name=name)(q, k, v)


def attn_bwd(q, k, v, tot, seen, do, dk0, dv0, *, name):
    S, D = q.shape
    nb = S // Q_BLOCK
    tk = _key_step(S)
    nkb = tk // Q_BLOCK
    scale = 1.0 / math.sqrt(HEAD_DIM)
    has_init = dk0 is not None

    def body(*refs):
        if has_init:
            q_ref, k_ref, v_ref, tot_ref, seen_ref, do_ref, dk0_ref, dv0_ref, dq_ref, dk_ref, dv_ref, km_ref = refs
            dk_ref[...] = dk0_ref[...]
            dv_ref[...] = dv0_ref[...]
        else:
            q_ref, k_ref, v_ref, tot_ref, seen_ref, do_ref, dq_ref, dk_ref, dv_ref, km_ref = refs
            dk_ref[...] = jnp.zeros_like(dk_ref)
            dv_ref[...] = jnp.zeros_like(dv_ref)
        heads, row, col, causal = _attn_consts()
        upto = _tri(row <= col)
        before = _tri(row < col)
        for h in range(2):
            km_ref[h] = jnp.where(heads[h], k_ref[...], jnp.zeros_like(k_ref[...]))

        def step(sb, carry, qq, dd, totl, nblk, diag):
            dq, pl_, pg = carry
            c0 = pl.multiple_of(sb * tk, tk)
            keys = pl.ds(c0, nblk * Q_BLOCK)
            z = lax.dot_general(qq, k_ref[keys, :], _DOT_DIMS["nt"], preferred_element_type=F32)
            da = lax.dot_general(dd, v_ref[keys, :], _DOT_DIMS["nt"], preferred_element_type=F32)
            blocks = range(nblk)
            masked = [diag and jb == nblk - 1 for jb in blocks]
            zb = [z[:, jb * Q_BLOCK:(jb + 1) * Q_BLOCK] for jb in blocks]
            lr = [_log1m_beta(zb[jb]) for jb in blocks]
            l = [jnp.where(causal, lr[jb], 0.0) if masked[jb] else lr[jb] for jb in blocks]
            lsum = [_keysum2(l[jb], upto) for jb in blocks]
            lrow = [jnp.sum(l[jb], axis=1, keepdims=True) for jb in blocks]
            a, g = [None] * nblk, [None] * nblk
            for jb in blocks:
                ab = jnp.exp(zb[jb] + lr[jb] + (totl - pl_ - lsum[jb]))
                if masked[jb]:
                    ab = jnp.where(causal, ab, 0.0)
                g[jb] = ab * da[:, jb * Q_BLOCK:(jb + 1) * Q_BLOCK]
                a[jb] = ab.astype(BF16)
                pl_ = pl_ + lrow[jb]
            gsum = [jnp.dot(g[jb].astype(BF16), before, preferred_element_type=F32) for jb in blocks]
            grow = [jnp.sum(g[jb], axis=1, keepdims=True) for jb in blocks]
            dz = [None] * nblk
            for jb in blocks:
                dzb = g[jb] * jnp.exp(lr[jb]) - jnp.exp(zb[jb] + lr[jb]) * (pg + gsum[jb])
                if masked[jb]:
                    dzb = jnp.where(causal, dzb, 0.0)
                dz[jb] = dzb.astype(BF16)
                pg = pg + grow[jb]
            a = jnp.concatenate(a, axis=1)
            dz = jnp.concatenate(dz, axis=1)
            for h in range(2):
                dq = dq + jnp.dot(dz[h * Q_BLOCK:(h + 1) * Q_BLOCK], km_ref[h, keys, :], preferred_element_type=F32)
            dk_ref[keys, :] += lax.dot_general(dz, qq, _DOT_DIMS["tn"], preferred_element_type=F32)
            dv_ref[keys, :] += lax.dot_general(a, dd, _DOT_DIMS["tn"], preferred_element_type=F32)
            return dq, pl_, pg

        def qblock(i, _):
            r0 = pl.multiple_of(i * Q_BLOCK, Q_BLOCK)
            qq = _stack_heads(q_ref[pl.ds(r0, Q_BLOCK), :] * scale, heads)
            dd = _stack_heads(do_ref[pl.ds(r0, Q_BLOCK), :].astype(BF16), heads)
            tot2 = tot_ref[pl.ds(r0, Q_BLOCK), :]
            totl = jnp.concatenate([tot2[:, 0:1], tot2[:, HEAD_DIM:HEAD_DIM + 1]], axis=0)
            last = i // nkb
            zc = jnp.zeros((2 * Q_BLOCK, 1), F32)
            carry = (jnp.zeros((Q_BLOCK, LANES), F32), zc, zc)
            walked = jnp.max(seen_ref[pl.ds(pl.multiple_of(i * SUBLANES, SUBLANES), SUBLANES), :]).astype(jnp.int32)
            first = last - jnp.clip(walked, 0, last)
            carry = lax.fori_loop(first, last, lambda sb, c: step(sb, c, qq, dd, totl, nkb, False), carry)
            carry = lax.switch(i % nkb, [functools.partial(step, last, qq=qq, dd=dd, totl=totl, nblk=m + 1, diag=True)
                                         for m in range(nkb)], carry)
            dq_ref[pl.ds(r0, Q_BLOCK), :] = (carry[0] * scale).astype(dq_ref.dtype)
            return 0

        lax.fori_loop(0, nb, qblock, 0)

    spec = pl.BlockSpec((S, LANES), lambda h: (0, h))
    seen_spec = pl.BlockSpec((nb * SUBLANES, LANES), lambda h: (0, h))
    args = [q, k, v, tot, seen, do] + ([dk0, dv0] if has_init else [])
    return pl.pallas_call(
        body, grid=(D // LANES,), in_specs=[spec] * 4 + [seen_spec] + [spec] * (len(args) - 5), out_specs=[spec] * 3,
        out_shape=[jax.ShapeDtypeStruct((S, D), BF16), jax.ShapeDtypeStruct((S, D), F32), jax.ShapeDtypeStruct((S, D), F32)],
        scratch_shapes=[pltpu.VMEM((2, S, LANES), BF16)], compiler_params=_cp("parallel"), name=name)(*args)


def _dev_index(px, py, pc):
    return 4 * px + 2 * py + pc


def all_gather(bufs):
    nb = len(bufs)

    def body(*refs):
        ins, outs = refs[:nb], refs[nb:2 * nb]
        send_sems, recv_sems, local_sems = refs[2 * nb:]
        x, y, c = lax.axis_index("x"), lax.axis_index("y"), lax.axis_index("c")
        me, sibling = (x, y, c), (x, y, 1 - c)
        chips = [(1 - x, y), (x, 1 - y), (1 - x, 1 - y)]

        def copy(b, k, block, to, from_input=False):
            slot = outs[b].at[_dev_index(*block)]
            return pltpu.make_async_remote_copy(
                src_ref=ins[b] if from_input else slot, dst_ref=slot,
                send_sem=send_sems.at[7 * b + k], recv_sem=recv_sems.at[7 * b + k], device_id=to, device_id_type=MESH)

        mine = [pltpu.make_async_copy(ins[b], outs[b].at[_dev_index(*me)], local_sems.at[b]) for b in range(nb)]
        for cp in mine:
            cp.start()
        first = []
        for b in range(nb):
            first.append(copy(b, 0, me, sibling, from_input=True))
            first += [copy(b, 1 + j, me, (*chip, c), from_input=True) for j, chip in enumerate(chips)]
        for cp in first:
            cp.start()
        passed = []
        for j, chip in enumerate(chips):
            for b in range(nb):
                copy(b, 1 + j, (*chip, c), me).wait_recv()
                fwd = copy(b, 4 + j, (*chip, c), sibling)
                fwd.start()
                passed.append(fwd)
        for b in range(nb):
            copy(b, 0, sibling, me).wait_recv()
            for j, chip in enumerate(chips):
                copy(b, 4 + j, (*chip, 1 - c), me).wait_recv()
        for cp in first + passed:
            cp.wait_send()
        for cp in mine:
            cp.wait()

    any_spec = pl.BlockSpec(memory_space=pl.ANY)
    return pl.pallas_call(
        body, in_specs=[any_spec] * nb, out_specs=[any_spec] * nb,
        out_shape=[jax.ShapeDtypeStruct((N_DEV,) + b.shape, b.dtype) for b in bufs],
        scratch_shapes=[pltpu.SemaphoreType.DMA((7 * nb,)), pltpu.SemaphoreType.DMA((7 * nb,)),
                        pltpu.SemaphoreType.DMA((nb,))],
        name="all_gather_weights")(*bufs)


def _sources(groups):
    return [s for g in groups for (s, _) in g[3]]


def _layout(groups, refs):
    out, si = [], 0
    for g, (_, _, _, lst) in enumerate(groups):
        for (s, off) in lst:
            out.append((g, refs[si], off, s.shape[-2]))
            si += 1
    return out


def pair_exchange(groups):
    srcs = _sources(groups)
    ns, ng = len(srcs), len(groups)

    def body(*refs):
        outs = refs[ns:ns + ng]
        send_sems, recv_sems = refs[ns + ng:]
        x, y, c = lax.axis_index("x"), lax.axis_index("y"), lax.axis_index("c")
        sibling = (x, y, 1 - c)
        for (g, ref, off, r) in _layout(groups, refs[:ns]):
            for q in range(N_DEV // 2):
                pltpu.make_async_remote_copy(
                    src_ref=ref.at[2 * q + 1 - c], dst_ref=outs[g].at[q, pl.ds(off, r)], send_sem=send_sems.at[g],
                    recv_sem=recv_sems.at[g], device_id=sibling, device_id_type=MESH).start()
        whole = [pltpu.make_async_remote_copy(
            src_ref=outs[g], dst_ref=outs[g], send_sem=send_sems.at[g], recv_sem=recv_sems.at[g],
            device_id=sibling, device_id_type=MESH) for g in range(ng)]
        for w in whole:
            w.wait_recv()
        for w in whole:
            w.wait_send()

    any_spec = pl.BlockSpec(memory_space=pl.ANY)
    return pl.pallas_call(
        body, in_specs=[any_spec] * ns, out_specs=[any_spec] * ng,
        out_shape=[jax.ShapeDtypeStruct((N_DEV // 2, r, w), dt) for (r, w, dt, _) in groups],
        scratch_shapes=[pltpu.SemaphoreType.DMA((ng,)), pltpu.SemaphoreType.DMA((ng,))],
        name="pair_exchange")(*srcs)


def pair_sum(src, got, off, core, *, name):
    _, r, W = src.shape
    tr = _row_tile(r, off, 1024)
    o = off // tr

    def body(c_ref, s_ref, g_ref, o_ref):
        o_ref[...] = (s_ref[...].astype(F32) + g_ref[...].astype(F32)).astype(o_ref.dtype)

    return pl.pallas_call(
        body,
        grid_spec=pltpu.PrefetchScalarGridSpec(
            num_scalar_prefetch=1, grid=(N_DEV // 2, r // tr),
            in_specs=[pl.BlockSpec((None, None, tr, W), lambda q, i, c: (q, c[0], i, 0)),
                      pl.BlockSpec((None, tr, W), lambda q, i, c: (q, i + o, 0))],
            out_specs=pl.BlockSpec((None, tr, W), lambda q, i, c: (q, i, 0))),
        out_shape=jax.ShapeDtypeStruct((N_DEV // 2, r, W), src.dtype), compiler_params=_cp("parallel", "parallel"),
        name=name)(core, src.reshape(N_DEV // 2, 2, r, W), got)


def chip_exchange(groups):
    srcs = _sources(groups)
    ns, ng = len(srcs), len(groups)

    def body(*refs):
        outs = refs[ns:ns + ng]
        send_sems, recv_sems, local_sems = refs[ns + ng:]
        x, y, c = lax.axis_index("x"), lax.axis_index("y"), lax.axis_index("c")
        me = 2 * x + y
        layout = _layout(groups, refs[:ns])
        mine = [pltpu.make_async_copy(ref.at[me], outs[g].at[me, pl.ds(off, r)], local_sems.at[i])
                for i, (g, ref, off, r) in enumerate(layout)]
        for cp in mine:
            cp.start()
        slots = []
        for flip in range(1, N_DEV // 2):
            px, py = (1 - x if flip & 2 else x), (1 - y if flip & 1 else y)
            peer, pq = (px, py, c), 2 * px + py
            for (g, ref, off, r) in layout:
                k = 3 * g + flip - 1
                pltpu.make_async_remote_copy(
                    src_ref=ref.at[pq], dst_ref=outs[g].at[me, pl.ds(off, r)], send_sem=send_sems.at[k],
                    recv_sem=recv_sems.at[k], device_id=peer, device_id_type=MESH).start()
            for g in range(ng):
                k = 3 * g + flip - 1
                slots.append(pltpu.make_async_remote_copy(
                    src_ref=outs[g].at[pq], dst_ref=outs[g].at[pq], send_sem=send_sems.at[k],
                    recv_sem=recv_sems.at[k], device_id=peer, device_id_type=MESH))
        for w in slots:
            w.wait_recv()
        for w in slots:
            w.wait_send()
        for cp in mine:
            cp.wait()

    any_spec = pl.BlockSpec(memory_space=pl.ANY)
    return pl.pallas_call(
        body, in_specs=[any_spec] * ns, out_specs=[any_spec] * ng,
        out_shape=[jax.ShapeDtypeStruct((N_DEV // 2, r, w), dt) for (r, w, dt, _) in groups],
        scratch_shapes=[pltpu.SemaphoreType.DMA((3 * ng,)), pltpu.SemaphoreType.DMA((3 * ng,)),
                        pltpu.SemaphoreType.DMA((ns,))],
        name="chip_exchange")(*srcs)


def _row_tile(rows, off, target):
    for t in (1024, 512, 256, 128, 64, 32, 16, 8):
        if t <= target and rows % t == 0 and off % t == 0:
            return t
    raise ValueError((rows, off))


def adamw(recv, off, w, m, v, *, name):
    rows, W = w.shape
    nslot = recv.shape[0]
    tr = _row_tile(rows, off, 256)
    o = off // tr
    c1 = 1.0 - ADAM_B1 ** ADAM_STEP
    c2 = 1.0 - ADAM_B2 ** ADAM_STEP

    def body(r_ref, w_ref, m_ref, v_ref, g_ref, d_ref, mo_ref, vo_ref):
        g = r_ref[0].astype(F32)
        for j in range(1, nslot):
            g = g + r_ref[j].astype(F32)
        mn = ADAM_B1 * m_ref[...] + (1.0 - ADAM_B1) * g
        vn = ADAM_B2 * v_ref[...] + (1.0 - ADAM_B2) * (g * g)
        g_ref[...] = g
        mo_ref[...] = mn
        vo_ref[...] = vn
        d_ref[...] = -ADAM_LR * ((mn / c1) / (jnp.sqrt(vn / c2) + ADAM_EPS) + ADAM_WD * w_ref[...])

    spec = pl.BlockSpec((tr, W), lambda i: (i, 0))
    return pl.pallas_call(
        body, grid=(rows // tr,), in_specs=[pl.BlockSpec((nslot, tr, W), lambda i: (0, i + o, 0)), spec, spec, spec],
        out_specs=[spec] * 4, out_shape=[jax.ShapeDtypeStruct((rows, W), F32)] * 4,
        compiler_params=_cp("parallel"), name=name)(recv, w, m, v)


def join_columns(gathered, off, K, *, name):
    _, _, n = gathered.shape
    tr = _row_tile(K, off, 256)
    o = off // tr

    def body(i_ref, o_ref):
        for d in range(N_DEV):
            o_ref[:, d * n:(d + 1) * n] = i_ref[d]

    return pl.pallas_call(
        body, grid=(K // tr,), in_specs=[pl.BlockSpec((N_DEV, tr, n), lambda i: (0, i + o, 0))],
        out_specs=pl.BlockSpec((tr, N_DEV * n), lambda i: (i, 0)),
        out_shape=jax.ShapeDtypeStruct((K, N_DEV * n), gathered.dtype), compiler_params=_cp("parallel"),
        name=name)(gathered)


def split_columns(full, *, name):
    K, N = full.shape
    n = N // N_DEV
    tr = _row_tile(K, 0, 256)

    def body(i_ref, o_ref):
        for d in range(N_DEV):
            o_ref[d] = i_ref[:, d * n:(d + 1) * n].astype(o_ref.dtype)

    return pl.pallas_call(
        body, grid=(K // tr,), in_specs=[pl.BlockSpec((tr, N), lambda i: (i, 0))],
        out_specs=pl.BlockSpec((N_DEV, tr, n), lambda i: (0, i, 0)),
        out_shape=jax.ShapeDtypeStruct((N_DEV, K, n), BF16), compiler_params=_cp("parallel"), name=name)(full)


def _pack(arrs, dtype, row_mult):
    flat = jnp.concatenate([a.reshape(-1).astype(dtype) for a in arrs])
    rows = -(-flat.shape[0] // PACK_W)
    rows = -(-rows // row_mult) * row_mult
    return jnp.pad(flat, (0, rows * PACK_W - flat.shape[0])).reshape(rows, PACK_W)


def _pack_dev(arrs, dtype, row_mult):
    flat = jnp.concatenate([a.reshape(N_DEV, -1).astype(dtype) for a in arrs], axis=1)
    rows = -(-flat.shape[1] // PACK_W)
    rows = -(-rows // row_mult) * row_mult
    return jnp.pad(flat, ((0, 0), (0, rows * PACK_W - flat.shape[1]))).reshape(N_DEV, rows, PACK_W)


def _unpack(buf, shapes):
    lead = buf.shape[:-2]
    flat = buf.reshape(lead + (-1,))
    outs, off = [], 0
    for s in shapes:
        n = math.prod(s)
        outs.append(flat[..., off:off + n].reshape(lead + tuple(s)))
        off += n
    return outs


def _join(g, axis):
    g = jnp.moveaxis(g, 0, axis)
    return g.reshape(g.shape[:axis] + (g.shape[axis] * g.shape[axis + 1],) + g.shape[axis + 2:])


def _split(full, axis):
    s = full.shape
    g = full.reshape(s[:axis] + (N_DEV, s[axis] // N_DEV) + s[axis + 1:])
    return jnp.moveaxis(g, axis, 0)


def kernel(x, p, a_pw1_w, a_pw1_b, a_dw_w, a_dw_b, a_ln_g, a_ln_b, a_pw2_w, a_pw2_b, b_wq, kv_wk, kv_wv, b_wo, ln_mix_g, ln_mix_b, ffn_w_up, ffn_w_gate, ffn_conv_w, ffn_conv_b, ffn_w_down, ple_w_gate, ple_w_proj, ln_ffn_g, ln_ffn_b, loss_target, m_a_pw1_w, m_a_pw1_b, m_a_dw_w, m_a_dw_b, m_a_ln_g, m_a_ln_b, m_a_pw2_w, m_a_pw2_b, m_b_wq, m_kv_wk, m_kv_wv, m_b_wo, m_ln_mix_g, m_ln_mix_b, m_ffn_w_up, m_ffn_w_gate, m_ffn_conv_w, m_ffn_conv_b, m_ffn_w_down, m_ple_w_gate, m_ple_w_proj, m_ln_ffn_g, m_ln_ffn_b, v_a_pw1_w, v_a_pw1_b, v_a_dw_w, v_a_dw_b, v_a_ln_g, v_a_ln_b, v_a_pw2_w, v_a_pw2_b, v_b_wq, v_kv_wk, v_kv_wv, v_b_wo, v_ln_mix_g, v_ln_mix_b, v_ffn_w_up, v_ffn_w_gate, v_ffn_conv_w, v_ffn_conv_b, v_ffn_w_down, v_ple_w_gate, v_ple_w_proj, v_ln_ffn_g, v_ln_ffn_b):
    local = dict(a_pw1_w=a_pw1_w, a_pw1_b=a_pw1_b, a_dw_w=a_dw_w, a_dw_b=a_dw_b, a_ln_g=a_ln_g, a_ln_b=a_ln_b, a_pw2_w=a_pw2_w, a_pw2_b=a_pw2_b, b_wq=b_wq, kv_wk=kv_wk, kv_wv=kv_wv, b_wo=b_wo, ln_mix_g=ln_mix_g, ln_mix_b=ln_mix_b, ffn_w_up=ffn_w_up, ffn_w_gate=ffn_w_gate, ffn_conv_w=ffn_conv_w, ffn_conv_b=ffn_conv_b, ffn_w_down=ffn_w_down, ple_w_gate=ple_w_gate, ple_w_proj=ple_w_proj, ln_ffn_g=ln_ffn_g, ln_ffn_b=ln_ffn_b)
    mom1 = dict(a_pw1_w=m_a_pw1_w, a_pw1_b=m_a_pw1_b, a_dw_w=m_a_dw_w, a_dw_b=m_a_dw_b, a_ln_g=m_a_ln_g, a_ln_b=m_a_ln_b, a_pw2_w=m_a_pw2_w, a_pw2_b=m_a_pw2_b, b_wq=m_b_wq, kv_wk=m_kv_wk, kv_wv=m_kv_wv, b_wo=m_b_wo, ln_mix_g=m_ln_mix_g, ln_mix_b=m_ln_mix_b, ffn_w_up=m_ffn_w_up, ffn_w_gate=m_ffn_w_gate, ffn_conv_w=m_ffn_conv_w, ffn_conv_b=m_ffn_conv_b, ffn_w_down=m_ffn_w_down, ple_w_gate=m_ple_w_gate, ple_w_proj=m_ple_w_proj, ln_ffn_g=m_ln_ffn_g, ln_ffn_b=m_ln_ffn_b)
    mom2 = dict(a_pw1_w=v_a_pw1_w, a_pw1_b=v_a_pw1_b, a_dw_w=v_a_dw_w, a_dw_b=v_a_dw_b, a_ln_g=v_a_ln_g, a_ln_b=v_a_ln_b, a_pw2_w=v_a_pw2_w, a_pw2_b=v_a_pw2_b, b_wq=v_b_wq, kv_wk=v_kv_wk, kv_wv=v_kv_wv, b_wo=v_b_wo, ln_mix_g=v_ln_mix_g, ln_mix_b=v_ln_mix_b, ffn_w_up=v_ffn_w_up, ffn_w_gate=v_ffn_w_gate, ffn_conv_w=v_ffn_conv_w, ffn_conv_b=v_ffn_conv_b, ffn_w_down=v_ffn_w_down, ple_w_gate=v_ple_w_gate, ple_w_proj=v_ple_w_proj, ln_ffn_g=v_ln_ffn_g, ln_ffn_b=v_ln_ffn_b)
    small_names = [n for n, _ in SMALL]
    small_shapes = [local[n].shape for n in small_names]
    repl_shapes = [local[n].shape for n in REPL]

    widths = sorted({local[n].shape[-1] for n, _ in BIG}, reverse=True)
    groups = {w: [n for n, _ in BIG if local[n].shape[-1] == w] for w in widths}
    offset, rows_of = {}, {}
    for w, names in groups.items():
        off = 0
        for n in names:
            offset[n], rows_of[n] = off, math.prod(local[n].shape[:-1])
            off += rows_of[n]
    sends = [jnp.concatenate([local[n].reshape(-1, w).astype(BF16) for n in names]) for w, names in groups.items()]
    gathered = all_gather(sends + [_pack([local[n] for n in small_names], F32, SUBLANES)])
    gath = dict(zip(widths, gathered[:-1]))
    W = {}
    for n, ax in BIG:
        w = local[n].shape[-1]
        nl = local[n].shape[0] if local[n].ndim == 3 else 1
        per = rows_of[n] // nl
        if ax == local[n].ndim - 1:
            W[n] = [join_columns(gath[w], offset[n] + l * per, per, name=f"join_{n}_{l}") for l in range(nl)]
        else:
            W[n] = [gath[w][:, offset[n] + l * per:offset[n] + (l + 1) * per].reshape(N_DEV * per, w) for l in range(nl)]
    for n in ("kv_wk", "kv_wv"):
        W[n] = W[n][0]
    W.update({n: _join(g, ax) for (n, ax), g in zip(SMALL, _unpack(gathered[-1], small_shapes))})
    W.update({n: local[n] for n in REPL})

    xs = x[0]
    S, D = xs.shape
    x_in, r1s, x1s, r2s, us, gps, gs, hhs, pgls, pps = [], [], [], [], [], [], [], [], [], []
    h1s, h2s, h3s, h5s, qs, os_, tots = {}, {}, {}, {}, {}, {}, {}
    kk = vv = None
    for i in range(DEPTH):
        x_in.append(xs)
        if i < N_A:
            h1 = mm(xs, W["a_pw1_w"][i], "nn", bias=W["a_pw1_b"][i][None], out_dtype=BF16, name=f"pw1_{i}")
            h2 = glu_fwd(h1, name=f"glu_{i}")
            h3, h5 = conv_ln_silu_fwd(h2, W["a_dw_w"][i], W["a_dw_b"][i][None], W["a_ln_g"][i][None],
                                      W["a_ln_b"][i][None], name=f"dwconv_{i}")
            mix = mm(h5, W["a_pw2_w"][i], "nn", bias=W["a_pw2_b"][i][None], name=f"pw2_{i}")
            h1s[i], h2s[i], h3s[i], h5s[i] = h1, h2, h3, h5
        else:
            j = i - N_A
            if kk is None:
                kk = mm(xs, W["kv_wk"], "nn", out_dtype=BF16, name="proj_k")
                vv = mm(xs, W["kv_wv"], "nn", out_dtype=BF16, name="proj_v")
            q = mm(xs, W["b_wq"][j], "nn", out_dtype=BF16, name=f"proj_q_{i}")
            o, tot, seen = attn_fwd(q, kk, vv, name=f"attn_{i}")
            mix = mm(o, W["b_wo"][j], "nn", name=f"proj_o_{i}")
            qs[i], os_[i], tots[i] = q, o, (tot, seen)
        r1, x1 = res_ln(xs, mix, W["ln_mix_g"][i][None], W["ln_mix_b"][i][None], name=f"ln_mix_{i}")
        u = mm(x1, W["ffn_w_up"][i], "nn", out_dtype=BF16, name=f"ffn_up_{i}")
        gp = mm(x1, W["ffn_w_gate"][i], "nn", out_dtype=BF16, name=f"ffn_gate_{i}")
        g, hh = conv_act_fwd(gp, u, W["ffn_conv_w"][i], W["ffn_conv_b"][i][None], name=f"ffn_conv_{i}")
        f = mm(hh, W["ffn_w_down"][i], "nn", name=f"ffn_down_{i}")
        pgl = mm(x1, W["ple_w_gate"][i], "nn", out_dtype=BF16, name=f"ple_gate_{i}")
        pp = mm(p[i, 0], W["ple_w_proj"][i], "nn", out_dtype=BF16, name=f"ple_proj_{i}")
        r2, xs = res_ln(x1, f, W["ln_ffn_g"][i][None], W["ln_ffn_b"][i][None], ple=(pgl, pp), name=f"ln_ffn_{i}")
        for lst, val in ((r1s, r1), (x1s, x1), (r2s, r2), (us, u), (gps, gp), (gs, g), (hhs, hh), (pgls, pgl), (pps, pp)):
            lst.append(val)

    dx, loss_part = loss_grad(xs, loss_target[0], name="loss")
    G = {n: [None] * local[n].shape[0] for n in WEIGHTS if n not in ("kv_wk", "kv_wv")}
    dk = dv = None
    for i in reversed(range(DEPTH)):
        x1 = x1s[i]
        dr2, dpp, dpgl, G["ln_ffn_g"][i], G["ln_ffn_b"][i] = ln_ple_bwd(r2s[i], W["ln_ffn_g"][i][None], dx, pgls[i], pps[i],
                                                                      name=f"ln_ffn_bwd_{i}")
        dhh = mm(dr2, W["ffn_w_down"][i], "nt", out_dtype=BF16, name=f"ffn_down_dx_{i}")
        G["ffn_w_down"][i] = mm(hhs[i], dr2, "tn", out_dtype=BF16, name=f"ffn_down_dw_{i}")
        G["ple_w_proj"][i] = mm(p[i, 0], dpp, "tn", out_dtype=BF16, name=f"ple_proj_dw_{i}")
        G["ple_w_gate"][i] = mm(x1, dpgl, "tn", out_dtype=BF16, name=f"ple_gate_dw_{i}")
        du, dgp, G["ffn_conv_w"][i], G["ffn_conv_b"][i] = ffn_gate_bwd(dhh, us[i], gs[i], gps[i], W["ffn_conv_w"][i],
                                                                       name=f"ffn_gate_bwd_{i}")
        G["ffn_w_up"][i] = mm(x1, du, "tn", out_dtype=BF16, name=f"ffn_up_dw_{i}")
        G["ffn_w_gate"][i] = mm(x1, dgp, "tn", out_dtype=BF16, name=f"ffn_gate_dw_{i}")
        dx1 = mm(du, W["ffn_w_up"][i], "nt", add=dr2, add_scale=DN_ALPHA, name=f"ffn_up_dx_{i}")
        dx1 = mm(dgp, W["ffn_w_gate"][i], "nt", add=dx1, name=f"ffn_gate_dx_{i}")
        dx1 = mm(dpgl, W["ple_w_gate"][i], "nt", add=dx1, name=f"ple_gate_dx_{i}")
        dr1, G["ln_mix_g"][i], G["ln_mix_b"][i], dr1_sum = ln_bwd(r1s[i], W["ln_mix_g"][i][None], dx1, name=f"ln_mix_bwd_{i}")
        if i < N_A:
            G["a_pw2_w"][i] = mm(h5s[i], dr1, "tn", out_dtype=BF16, name=f"pw2_dw_{i}")
            G["a_pw2_b"][i] = dr1_sum
            dh5 = mm(dr1, W["a_pw2_w"][i], "nt", name=f"pw2_dx_{i}")
            dh3, G["a_ln_g"][i], G["a_ln_b"][i] = ln_silu_bwd(h3s[i], W["a_ln_g"][i][None], W["a_ln_b"][i][None], dh5,
                                                             name=f"dwconv_ln_bwd_{i}")
            dh2 = conv_bwd_x(dh3, W["a_dw_w"][i], out_dtype=F32, name=f"dwconv_dx_{i}")
            G["a_dw_w"][i], G["a_dw_b"][i] = conv_bwd_w(h2s[i], dh3, CONV_W, name=f"dwconv_dw_{i}")
            dh1, G["a_pw1_b"][i] = glu_bwd(h1s[i], dh2, name=f"glu_bwd_{i}")
            G["a_pw1_w"][i] = mm(x_in[i], dh1, "tn", out_dtype=BF16, name=f"pw1_dw_{i}")
            dx = mm(dh1, W["a_pw1_w"][i], "nt", add=dr1, add_scale=DN_ALPHA, name=f"pw1_dx_{i}")
        else:
            j = i - N_A
            G["b_wo"][j] = mm(os_[i], dr1, "tn", out_dtype=BF16, name=f"proj_o_dw_{i}")
            do = mm(dr1, W["b_wo"][j], "nt", out_dtype=BF16, name=f"proj_o_dx_{i}")
            dq, dk, dv = attn_bwd(qs[i], kk, vv, *tots[i], do, dk, dv, name=f"attn_bwd_{i}")
            G["b_wq"][j] = mm(x_in[i], dq, "tn", out_dtype=BF16, name=f"proj_q_dw_{i}")
            dx = mm(dq, W["b_wq"][j], "nt", add=dr1, add_scale=DN_ALPHA, name=f"proj_q_dx_{i}")
            if j == 0:
                G["kv_wk"] = mm(x_in[i], dk, "tn", out_dtype=BF16, name="proj_k_dw")
                G["kv_wv"] = mm(x_in[i], dv, "tn", out_dtype=BF16, name="proj_v_dw")
                dx = mm(dk, W["kv_wk"], "nt", add=dx, name="proj_k_dx")
                dx = mm(dv, W["kv_wv"], "nt", add=dx, name="proj_v_dx")
    grad_x = dx[None]
    shard_axis = dict(BIG + SMALL)
    for n in small_names + list(REPL):
        full = list(local[n].shape)
        if n in shard_axis:
            full[shard_axis[n]] *= N_DEV
        G[n] = jnp.stack(G[n]).reshape(full)

    n_small = sum(math.prod(s) for s in small_shapes)
    n_repl = sum(math.prod(s) for s in repl_shapes)
    repl_flat = jnp.concatenate([G[n].reshape(-1) for n in REPL] + [loss_part.reshape(-1)[:1]])
    send_small = _pack_dev([_split(G[n], ax) for n, ax in SMALL] + [jnp.broadcast_to(repl_flat, (N_DEV, n_repl + 1))],
                           F32, SUBLANES)
    ex_groups = []
    for w, names in groups.items():
        lst = []
        for n in names:
            layers = G[n] if isinstance(G[n], list) else [G[n]]
            per = rows_of[n] // len(layers)
            for l, g in enumerate(layers):
                if shard_axis[n] == local[n].ndim - 1:
                    src = split_columns(g, name=f"split_{n}_{l}")
                else:
                    src = g.reshape(N_DEV, per, w)
                lst.append((src, offset[n] + l * per))
        ex_groups.append((sum(rows_of[n] for n in names), w, BF16, lst))
    ex_groups.append((send_small.shape[1], PACK_W, F32, [(send_small, 0)]))
    gots = pair_exchange(ex_groups)
    core = lax.axis_index("c").astype(jnp.int32).reshape(1)
    sum_groups = [(rows, w, dt, [(pair_sum(src, got, off, core, name=f"pair_sum_{gi}_{si}"), off) for si, (src, off) in enumerate(lst)])
                  for gi, ((rows, w, dt, lst), got) in enumerate(zip(ex_groups, gots))]
    recvs = chip_exchange(sum_groups)
    recv = dict(zip(widths, recvs[:-1]))
    recv_small = recvs[-1]

    out = {}
    for n, _ in BIG:
        w = local[n].shape[-1]
        res = adamw(recv[w], offset[n], local[n].reshape(-1, w), mom1[n].reshape(-1, w), mom2[n].reshape(-1, w),
                    name=f"adamw_{n}")
        out[n] = [r.reshape(local[n].shape) for r in res]

    def state(d):
        small = _pack([d[n] for n in small_names] + [d[n] for n in REPL], F32, SUBLANES)
        return jnp.pad(small, ((0, recv_small.shape[1] - small.shape[0]), (0, 0)))

    out_small = adamw(recv_small, 0, state(local), state(mom1), state(mom2), name="adamw_vectors")
    loss = out_small[0].reshape(-1)[n_small + n_repl]
    vecs = [dict(zip(small_names + list(REPL), _unpack(o, small_shapes + repl_shapes))) for o in out_small]
    per_kind = [[out[n][kind] if n in out else vecs[kind][n] for n in WEIGHTS] for kind in range(4)]
    grads, deltas, new_m, new_v = per_kind
    return (loss, grad_x, *grads, *deltas, *new_m, *new_v)
```

```python
import functools
import math

import jax
import jax.numpy as jnp
from jax import lax
from jax.experimental import pallas as pl
from jax.experimental.pallas import tpu as pltpu

F32 = jnp.float32
BF16 = jnp.bfloat16
MESH = pl.DeviceIdType.MESH

N_DEV = 8
DEPTH = 4
N_A = 2
HEAD_DIM = 64
Q_BLOCK = 128
CONV_W = 31
FFN_CONV_W = 3
LN_EPS = 1e-5
DN_ALPHA = (2.0 * DEPTH) ** 0.25
ADAM_LR = 0.001
ADAM_B1 = 0.9
ADAM_B2 = 0.999
ADAM_EPS = 1e-08
ADAM_WD = 0.01
ADAM_STEP = 10

LANES = 128
SUBLANES = 8
PACK_W = 1024
VMEM_LIMIT = 56 * 1024 * 1024

BIG = (("a_pw1_w", 2), ("a_pw2_w", 1), ("b_wq", 1), ("kv_wk", 0), ("kv_wv", 0), ("b_wo", 1),
       ("ffn_w_up", 2), ("ffn_w_gate", 2), ("ffn_w_down", 1), ("ple_w_gate", 1), ("ple_w_proj", 2))
SMALL = (("a_pw1_b", 1), ("a_dw_w", 2), ("a_dw_b", 1), ("a_ln_g", 1), ("a_ln_b", 1), ("a_pw2_b", 1),
         ("ffn_conv_w", 2))
REPL = ("ln_mix_g", "ln_mix_b", "ffn_conv_b", "ln_ffn_g", "ln_ffn_b")
WEIGHTS = ("a_pw1_w", "a_pw1_b", "a_dw_w", "a_dw_b", "a_ln_g", "a_ln_b", "a_pw2_w", "a_pw2_b", "b_wq", "kv_wk",
           "kv_wv", "b_wo", "ln_mix_g", "ln_mix_b", "ffn_w_up", "ffn_w_gate", "ffn_conv_w", "ffn_conv_b",
           "ffn_w_down", "ple_w_gate", "ple_w_proj", "ln_ffn_g", "ln_ffn_b")


def _cp(*sem):
    return pltpu.CompilerParams(dimension_semantics=sem, vmem_limit_bytes=VMEM_LIMIT)


def _pick(dim, target, align=LANES):
    if dim <= target:
        return dim
    t = (target // align) * align
    while t >= align:
        if dim % t == 0:
            return t
        t -= align
    return dim


MM_ROWS = 512
MM_ROWS_TN = 1536
MM_COLS = 1536
MM_DEPTH = 2816
MM_DEPTH_TN = 1536

_DOT_DIMS = {"nn": (((1,), (0,)), ((), ())), "nt": (((1,), (1,)), ((), ())), "tn": (((0,), (0,)), ((), ()))}


def mm(a, b, mode, *, bias=None, add=None, add_scale=1.0, out_dtype=F32, also=None, plus=None, name):
    if mode == "tn":
        K, M = a.shape
    else:
        M, K = a.shape
    N = b.shape[0] if mode == "nt" else b.shape[1]
    tm = _pick(M, MM_ROWS_TN if mode == "tn" else MM_ROWS if plus is None else MM_ROWS // 2)
    tn = _pick(N, MM_COLS)
    tk = _pick(K, MM_DEPTH_TN if mode == "tn" else MM_DEPTH)
    nk = K // tk
    dims = _DOT_DIMS[mode]
    n_out = 1 if also is None else 2

    def body(*refs):
        refs = list(refs)
        a_ref, b_ref = refs.pop(0), refs.pop(0)
        b2_ref = refs.pop(0) if also is not None else None
        a3_ref, b3_ref = (refs.pop(0), refs.pop(0)) if plus is not None else (None, None)
        bias_ref = refs.pop(0) if bias is not None else None
        add_ref = refs.pop(0) if add is not None else None
        o_refs, acc_refs = refs[:n_out], refs[n_out:]
        k = pl.program_id(2)

        @pl.when(k == 0)
        def _():
            for acc_ref in acc_refs:
                acc_ref[...] = jnp.zeros_like(acc_ref)

        a_v = a_ref[...].astype(BF16)
        acc_refs[0][...] += lax.dot_general(a_v, b_ref[...].astype(BF16), dims, preferred_element_type=F32)
        if b2_ref is not None:
            acc_refs[1][...] += lax.dot_general(a_v, b2_ref[...].astype(BF16), dims, preferred_element_type=F32)
        if a3_ref is not None:
            acc_refs[0][...] += lax.dot_general(a3_ref[...].astype(BF16), b3_ref[...].astype(BF16), dims,
                                                preferred_element_type=F32)

        @pl.when(k == nk - 1)
        def _():
            for o_ref, acc_ref in zip(o_refs, acc_refs):
                r = acc_ref[...]
                if bias_ref is not None:
                    r = r + bias_ref[...]
                if add_ref is not None:
                    r = r + add_scale * add_ref[...].astype(F32)
                o_ref[...] = r.astype(o_ref.dtype)

    a_spec = pl.BlockSpec((tk, tm), lambda j, i, k: (k, i)) if mode == "tn" else pl.BlockSpec((tm, tk), lambda j, i, k: (i, k))
    b_spec = pl.BlockSpec((tn, tk), lambda j, i, k: (j, k)) if mode == "nt" else pl.BlockSpec((tk, tn), lambda j, i, k: (k, j))
    in_specs, args = [a_spec, b_spec], [a, b]
    if also is not None:
        in_specs.append(b_spec)
        args.append(also)
    if plus is not None:
        in_specs += [a_spec, b_spec]
        args += list(plus)
    if bias is not None:
        in_specs.append(pl.BlockSpec((1, tn), lambda j, i, k: (0, j)))
        args.append(bias)
    if add is not None:
        in_specs.append(pl.BlockSpec((tm, tn), lambda j, i, k: (i, j)))
        args.append(add)
    o_spec = pl.BlockSpec((tm, tn), lambda j, i, k: (i, j))
    outs = pl.pallas_call(
        body, grid=(N // tn, M // tm, nk), in_specs=in_specs,
        out_specs=[o_spec] * n_out, out_shape=[jax.ShapeDtypeStruct((M, N), out_dtype)] * n_out,
        scratch_shapes=[pltpu.VMEM((tm, tn), F32)] * n_out,
        compiler_params=_cp("parallel", "parallel", "arbitrary"), name=name)(*args)
    return outs[0] if also is None else tuple(outs)


def _rows(body, *, n_rows, tm, row_ins, full_ins=(), row_outs=(), acc_outs=(), scratch=(), reverse=False, name):
    n = n_rows // tm

    def rmap(i):
        return (n - 1 - i, 0) if reverse else (i, 0)

    in_specs = [pl.BlockSpec((tm, a.shape[1]), rmap) for a in row_ins]
    in_specs += [pl.BlockSpec(a.shape, lambda i, nd=a.ndim: (0,) * nd) for a in full_ins]
    out_shape = [jax.ShapeDtypeStruct((n_rows, w), dt) for (w, dt) in row_outs]
    out_shape += [jax.ShapeDtypeStruct(s, dt) for (s, dt) in acc_outs]
    out_specs = [pl.BlockSpec((tm, w), rmap) for (w, dt) in row_outs]
    out_specs += [pl.BlockSpec(s, lambda i, nd=len(s): (0,) * nd) for (s, dt) in acc_outs]
    return pl.pallas_call(
        functools.partial(body, n), grid=(n,), in_specs=in_specs, out_specs=out_specs, out_shape=out_shape,
        scratch_shapes=list(scratch), compiler_params=_cp("arbitrary"), name=name)(*row_ins, *full_ins)


def _sigmoid(x):
    return 1.0 / (1.0 + jnp.exp(-x))


def _ln_hat(r):
    mu = jnp.mean(r, axis=-1, keepdims=True)
    xc = r - mu
    var = jnp.mean(xc * xc, axis=-1, keepdims=True)
    rstd = lax.rsqrt(var + LN_EPS)
    return xc * rstd, rstd


def _ln_back(xhat, rstd, g, dy):
    dxh = dy * g
    m1 = jnp.mean(dxh, axis=-1, keepdims=True)
    m2 = jnp.mean(dxh * xhat, axis=-1, keepdims=True)
    return rstd * (dxh - m1 - xhat * m2)


def _colsum(x):
    return jnp.sum(x, axis=0, keepdims=True)


def _acc(i, ref, val):
    @pl.when(i == 0)
    def _():
        ref[...] = val

    @pl.when(i > 0)
    def _():
        ref[...] += val


def res_ln(x, mix, g, b, *, ple=None, name):
    S, D = x.shape

    def body(n, *refs):
        if ple is None:
            x_ref, m_ref, g_ref, b_ref, r_ref, y_ref = refs
            r = DN_ALPHA * x_ref[...] + m_ref[...]
        else:
            x_ref, m_ref, pgl_ref, pp_ref, g_ref, b_ref, r_ref, y_ref = refs
            r = DN_ALPHA * x_ref[...] + m_ref[...] + _sigmoid(pgl_ref[...].astype(F32)) * pp_ref[...].astype(F32)
        xhat, _ = _ln_hat(r)
        r_ref[...] = r
        y_ref[...] = xhat * g_ref[...] + b_ref[...]

    row_ins = [x, mix] + ([] if ple is None else list(ple))
    return _rows(body, n_rows=S, tm=_pick(S, 256, SUBLANES), row_ins=row_ins, full_ins=[g, b],
                 row_outs=[(D, F32), (D, F32)], name=name)


def ln_bwd(r, g, dy, *, name):
    S, D = r.shape

    def body(n, r_ref, dy_ref, g_ref, dr_ref, dg_ref, db_ref, ds_ref):
        i = pl.program_id(0)
        xhat, rstd = _ln_hat(r_ref[...])
        dy_v = dy_ref[...]
        dr = _ln_back(xhat, rstd, g_ref[...], dy_v)
        dr_ref[...] = dr
        _acc(i, dg_ref, _colsum(dy_v * xhat))
        _acc(i, db_ref, _colsum(dy_v))
        _acc(i, ds_ref, _colsum(dr))

    return _rows(body, n_rows=S, tm=_pick(S, 256, SUBLANES), row_ins=[r, dy], full_ins=[g],
                 row_outs=[(D, F32)], acc_outs=[((1, D), F32)] * 3, name=name)


def glu_fwd(h1, *, name):
    S, D2 = h1.shape
    D = D2 // 2

    def body(n, h_ref, o_ref):
        o_ref[...] = h_ref[:, :D].astype(F32) * _sigmoid(h_ref[:, D:].astype(F32))

    return _rows(body, n_rows=S, tm=_pick(S, 256, SUBLANES), row_ins=[h1], row_outs=[(D, F32)], name=name)[0]


def glu_bwd(h1, dh2, *, name):
    S, D2 = h1.shape
    D = D2 // 2

    def body(n, h_ref, d_ref, o_ref, s_ref):
        i = pl.program_id(0)
        a, sg, d = h_ref[:, :D].astype(F32), _sigmoid(h_ref[:, D:].astype(F32)), d_ref[...]
        da = d * sg
        dg = d * a * sg * (1.0 - sg)
        o_ref[:, :D] = da.astype(o_ref.dtype)
        o_ref[:, D:] = dg.astype(o_ref.dtype)
        _acc(i, s_ref, jnp.concatenate([_colsum(da), _colsum(dg)], axis=1))

    return _rows(body, n_rows=S, tm=_pick(S, 256, SUBLANES), row_ins=[h1, dh2], row_outs=[(D2, BF16)],
                 acc_outs=[((1, D2), F32)], name=name)


CONV_ROWS = 32
CONV_LANES = 256


def _halo(k):
    return -(-(k - 1) // SUBLANES) * SUBLANES


def _phases(offs):
    return sorted({o % SUBLANES for o in offs} - {0})


def _shift_scratch(offs, n_rows, width):
    return pltpu.VMEM((max(len(_phases(offs)), 1), n_rows, width), F32)


def _make_shifted(buf_ref, sh_ref, offs):
    n = buf_ref.shape[0] - SUBLANES
    for p, b in enumerate(_phases(offs)):
        sh_ref[p, pl.ds(0, n), :] = buf_ref[pl.ds(b, n), :]


def _tap(buf_ref, sh_ref, offs, k, rc, rows, lc, lw):
    b = offs[k] % SUBLANES
    src = buf_ref if b == 0 else sh_ref.at[_phases(offs).index(b)]
    return src[pl.ds(offs[k] - b + rc, rows), pl.ds(lc, lw)]


def _conv_taps(buf_ref, sh_ref, w_ref, offs, tm, width, emit):
    _make_shifted(buf_ref, sh_ref, offs)
    rows = min(CONV_ROWS, tm)
    for lc in range(0, width, CONV_LANES):
        lw = min(CONV_LANES, width - lc)
        for rc in range(0, tm, rows):
            acc = None
            for k in range(len(offs)):
                t = _tap(buf_ref, sh_ref, offs, k, rc, rows, lc, lw) * w_ref[pl.ds(k, 1), pl.ds(lc, lw)]
                acc = t if acc is None else acc + t
            emit(rc, lc, lw, rows, acc)


def _fill_causal(i, buf_ref, x_ref, halo, tm):
    @pl.when(i == 0)
    def _():
        buf_ref[pl.ds(0, halo), :] = jnp.zeros((halo, buf_ref.shape[1]), F32)

    @pl.when(i > 0)
    def _():
        buf_ref[pl.ds(0, halo), :] = buf_ref[pl.ds(tm, halo), :]

    buf_ref[pl.ds(halo, tm), :] = x_ref[...].astype(F32)


def conv_ln_silu_fwd(x, w, b, g, beta, *, name):
    S, C = x.shape
    K = w.shape[0]
    halo = _halo(K)
    tm = _pick(S, 256, SUBLANES)
    offs = [halo - (K - 1) + k for k in range(K)]

    def body(n, x_ref, w_ref, b_ref, g_ref, beta_ref, h3_ref, h5_ref, buf_ref, sh_ref):
        i = pl.program_id(0)
        _fill_causal(i, buf_ref, x_ref, halo, tm)

        def emit(rc, lc, lw, rows, acc):
            h3_ref[pl.ds(rc, rows), pl.ds(lc, lw)] = acc + b_ref[:, pl.ds(lc, lw)]

        _conv_taps(buf_ref, sh_ref, w_ref, offs, tm, C, emit)
        xhat, _ = _ln_hat(h3_ref[...])
        h4 = xhat * g_ref[...] + beta_ref[...]
        h5_ref[...] = (h4 * _sigmoid(h4)).astype(h5_ref.dtype)

    return _rows(body, n_rows=S, tm=tm, row_ins=[x], full_ins=[w, b, g, beta], row_outs=[(C, F32), (C, BF16)],
                 scratch=[pltpu.VMEM((tm + halo, C), F32), _shift_scratch(offs, tm + halo, C)], name=name)


def conv_act_fwd(gp, u, w, b, *, name):
    S, C = gp.shape
    K = w.shape[0]
    halo = _halo(K)
    tm = _pick(S, 256, SUBLANES)
    offs = [halo - (K - 1) + k for k in range(K)]

    def body(n, x_ref, u_ref, w_ref, b_ref, g_ref, hh_ref, buf_ref, sh_ref):
        i = pl.program_id(0)
        _fill_causal(i, buf_ref, x_ref, halo, tm)

        def emit(rc, lc, lw, rows, acc):
            gv = acc + b_ref[:, pl.ds(lc, lw)]
            g_ref[pl.ds(rc, rows), pl.ds(lc, lw)] = gv.astype(g_ref.dtype)
            hh_ref[pl.ds(rc, rows), pl.ds(lc, lw)] = (gv * _sigmoid(gv) * u_ref[pl.ds(rc, rows), pl.ds(lc, lw)].astype(F32)).astype(hh_ref.dtype)

        _conv_taps(buf_ref, sh_ref, w_ref, offs, tm, C, emit)

    return _rows(body, n_rows=S, tm=tm, row_ins=[gp, u], full_ins=[w, b], row_outs=[(C, BF16), (C, BF16)],
                 scratch=[pltpu.VMEM((tm + halo, C), F32), _shift_scratch(offs, tm + halo, C)], name=name)


def conv_bwd_x(dy, w, *, out_dtype, name):
    S, C = dy.shape
    K = w.shape[0]
    halo = _halo(K)
    tm = _pick(S, 256, SUBLANES)
    offs = [K - 1 - k for k in range(K)]

    def body(n, dy_ref, w_ref, dx_ref, buf_ref, sh_ref):
        i = pl.program_id(0)

        @pl.when(i == 0)
        def _():
            buf_ref[pl.ds(tm, halo), :] = jnp.zeros((halo, C), F32)

        @pl.when(i > 0)
        def _():
            buf_ref[pl.ds(tm, halo), :] = buf_ref[pl.ds(0, halo), :]

        buf_ref[pl.ds(0, tm), :] = dy_ref[...].astype(F32)

        def emit(rc, lc, lw, rows, acc):
            dx_ref[pl.ds(rc, rows), pl.ds(lc, lw)] = acc.astype(dx_ref.dtype)

        _conv_taps(buf_ref, sh_ref, w_ref, offs, tm, C, emit)

    return _rows(body, n_rows=S, tm=tm, row_ins=[dy], full_ins=[w], row_outs=[(C, out_dtype)],
                 scratch=[pltpu.VMEM((tm + halo, C), F32), _shift_scratch(offs, tm + halo, C)], reverse=True, name=name)[0]


def conv_bwd_w(x, dy, K, *, name):
    S, C = x.shape
    halo = _halo(K)
    tm = _pick(S, 256, SUBLANES)
    offs = [halo - (K - 1) + k for k in range(K)]
    rows = min(CONV_ROWS, tm)

    def body(n, x_ref, dy_ref, dw_ref, db_ref, buf_ref, acc_ref, sh_ref):
        i = pl.program_id(0)
        _fill_causal(i, buf_ref, x_ref, halo, tm)
        _make_shifted(buf_ref, sh_ref, offs)

        @pl.when(i == 0)
        def _():
            acc_ref[...] = jnp.zeros_like(acc_ref)

        for lc in range(0, C, CONV_LANES):
            lw = min(CONV_LANES, C - lc)
            for k in range(K):
                s = None
                for rc in range(0, tm, rows):
                    t = dy_ref[pl.ds(rc, rows), pl.ds(lc, lw)].astype(F32) * _tap(buf_ref, sh_ref, offs, k, rc, rows, lc, lw)
                    s = t if s is None else s + t
                s8 = s[0:SUBLANES]
                for q in range(1, rows // SUBLANES):
                    s8 = s8 + s[q * SUBLANES:(q + 1) * SUBLANES]
                acc_ref[pl.ds(k * SUBLANES, SUBLANES), pl.ds(lc, lw)] += s8
        _acc(i, db_ref, _colsum(dy_ref[...].astype(F32)))

        @pl.when(i == n - 1)
        def _():
            for k in range(K):
                dw_ref[pl.ds(k, 1), :] = _colsum(acc_ref[pl.ds(k * SUBLANES, SUBLANES), :])

    return _rows(body, n_rows=S, tm=tm, row_ins=[x, dy], acc_outs=[((K, C), F32), ((1, C), F32)],
                 scratch=[pltpu.VMEM((tm + halo, C), F32), pltpu.VMEM((K * SUBLANES, C), F32),
                          _shift_scratch(offs, tm + halo, C)], name=name)


def ln_silu_bwd(h3, g, beta, dh5, *, name):
    S, C = h3.shape

    def body(n, h_ref, d_ref, g_ref, beta_ref, dh_ref, dg_ref, db_ref):
        i = pl.program_id(0)
        xhat, rstd = _ln_hat(h_ref[...])
        h4 = xhat * g_ref[...] + beta_ref[...]
        sg = _sigmoid(h4)
        dh4 = d_ref[...] * sg * (1.0 + h4 * (1.0 - sg))
        dh_ref[...] = _ln_back(xhat, rstd, g_ref[...], dh4)
        _acc(i, dg_ref, _colsum(dh4 * xhat))
        _acc(i, db_ref, _colsum(dh4))

    return _rows(body, n_rows=S, tm=_pick(S, 256, SUBLANES), row_ins=[h3, dh5], full_ins=[g, beta],
                 row_outs=[(C, F32)], acc_outs=[((1, C), F32)] * 2, name=name)


def ffn_gate_bwd(dhh, u, g, gp, w, *, name):
    S, C = u.shape
    K = w.shape[0]
    halo = _halo(K)
    tm = _pick(S, 256, SUBLANES)
    offs = [K - 1 - k for k in range(K)]
    rows = min(CONV_ROWS, tm)

    def body(n, d_ref, u_ref, g_ref, gp_ref, w_ref, du_ref, dgp_ref, dw_ref, db_ref, buf_ref, sh_ref, acc_ref):
        i = pl.program_id(0)

        @pl.when(i == 0)
        def _():
            buf_ref[pl.ds(tm, halo), :] = jnp.zeros((halo, C), F32)
            acc_ref[...] = jnp.zeros_like(acc_ref)

        @pl.when(i > 0)
        def _():
            buf_ref[pl.ds(tm, halo), :] = buf_ref[pl.ds(0, halo), :]

        d, gv = d_ref[...].astype(F32), g_ref[...].astype(F32)
        sg = _sigmoid(gv)
        du_ref[...] = (d * gv * sg).astype(du_ref.dtype)
        dg = d * u_ref[...].astype(F32) * sg * (1.0 + gv * (1.0 - sg))
        buf_ref[pl.ds(0, tm), :] = dg
        _acc(i, db_ref, _colsum(dg))

        def emit(rc, lc, lw, nrows, acc):
            dgp_ref[pl.ds(rc, nrows), pl.ds(lc, lw)] = acc.astype(dgp_ref.dtype)

        _conv_taps(buf_ref, sh_ref, w_ref, offs, tm, C, emit)
        for lc in range(0, C, CONV_LANES):
            lw = min(CONV_LANES, C - lc)
            for k in range(K):
                s_ = None
                for rc in range(0, tm, rows):
                    t = gp_ref[pl.ds(rc, rows), pl.ds(lc, lw)].astype(F32) * _tap(buf_ref, sh_ref, offs, k, rc, rows, lc, lw)
                    s_ = t if s_ is None else s_ + t
                s8 = s_[0:SUBLANES]
                for q in range(1, rows // SUBLANES):
                    s8 = s8 + s_[q * SUBLANES:(q + 1) * SUBLANES]
                acc_ref[pl.ds(k * SUBLANES, SUBLANES), pl.ds(lc, lw)] += s8

        @pl.when(i == n - 1)
        def _():
            for k in range(K):
                dw_ref[pl.ds(k, 1), :] = _colsum(acc_ref[pl.ds(k * SUBLANES, SUBLANES), :])

    return _rows(body, n_rows=S, tm=tm, row_ins=[dhh, u, g, gp], full_ins=[w], row_outs=[(C, BF16), (C, BF16)],
                 acc_outs=[((K, C), F32), ((1, C), F32)],
                 scratch=[pltpu.VMEM((tm + halo, C), F32), _shift_scratch(offs, tm + halo, C),
                          pltpu.VMEM((K * SUBLANES, C), F32)], reverse=True, name=name)


def ln_ple_bwd(r, g, dy, pgl, pp, *, name):
    S, D = r.shape

    def body(n, r_ref, dy_ref, l_ref, p_ref, g_ref, dr_ref, dpp_ref, dpl_ref, dg_ref, db_ref):
        i = pl.program_id(0)
        xhat, rstd = _ln_hat(r_ref[...])
        dy_v = dy_ref[...]
        dr = _ln_back(xhat, rstd, g_ref[...], dy_v)
        dr_ref[...] = dr
        sg = _sigmoid(l_ref[...].astype(F32))
        dpp_ref[...] = (dr * sg).astype(dpp_ref.dtype)
        dpl_ref[...] = (dr * p_ref[...].astype(F32) * sg * (1.0 - sg)).astype(dpl_ref.dtype)
        _acc(i, dg_ref, _colsum(dy_v * xhat))
        _acc(i, db_ref, _colsum(dy_v))

    return _rows(body, n_rows=S, tm=_pick(S, 256, SUBLANES), row_ins=[r, dy, pgl, pp], full_ins=[g],
                 row_outs=[(D, F32), (D, BF16), (D, BF16)], acc_outs=[((1, D), F32)] * 2, name=name)


def loss_grad(y, target, *, name):
    S, D = y.shape

    def body(n, y_ref, t_ref, dy_ref, l_ref):
        i = pl.program_id(0)
        e = y_ref[...] - t_ref[...]
        dy_ref[...] = e * (1.0 / D)
        s = jnp.sum(_colsum(e * e), axis=1, keepdims=True) * (0.5 / D)
        _acc(i, l_ref, jnp.broadcast_to(s, (1, LANES)))

    return _rows(body, n_rows=S, tm=_pick(S, 256, SUBLANES), row_ins=[y, target], row_outs=[(D, F32)],
                 acc_outs=[((1, LANES), F32)], name=name)


def _key_step(S):
    return min(512, S // 2)


EXIT_LOG = -110.0


def _attn_consts():
    lane = lax.broadcasted_iota(jnp.int32, (1, LANES), 1)
    heads = (lane < HEAD_DIM, lane >= HEAD_DIM)
    row = lax.broadcasted_iota(jnp.int32, (Q_BLOCK, Q_BLOCK), 0)
    col = lax.broadcasted_iota(jnp.int32, (Q_BLOCK, Q_BLOCK), 1)
    causal = jnp.concatenate([col < row] * 2, axis=0)
    return heads, row, col, causal


def _tri(cond):
    return jnp.where(cond, 1.0, 0.0).astype(BF16)


def _keysum2(x, tri):
    hi = x.astype(BF16)
    lo = (x - hi.astype(F32)).astype(BF16)
    return jnp.dot(jnp.concatenate([hi, lo], axis=1), jnp.concatenate([tri, tri], axis=0),
                   preferred_element_type=F32)


def _stack_heads(x, heads):
    return jnp.concatenate([jnp.where(m, x, jnp.zeros_like(x)) for m in heads], axis=0)


def _log1m_beta(z):
    return -(jnp.maximum(z, 0.0) + jnp.log(1.0 + jnp.exp(-jnp.abs(z))))


def attn_fwd(q, k, v, *, name):
    S, D = q.shape
    nb = S // Q_BLOCK
    tk = _key_step(S)
    nkb = tk // Q_BLOCK
    scale = 1.0 / math.sqrt(HEAD_DIM)

    def body(q_ref, k_ref, v_ref, o_ref, tot_ref, seen_ref, vm_ref):
        heads, row, col, causal = _attn_consts()
        above = _tri(row > col)
        for h in range(2):
            vm_ref[h] = jnp.where(heads[h], v_ref[...], jnp.zeros_like(v_ref[...]))

        def step(sb, carry, qq, nblk, diag):
            acc, cl = carry
            c0 = pl.multiple_of(sb * tk, tk)
            z = lax.dot_general(qq, k_ref[pl.ds(c0, nblk * Q_BLOCK), :], _DOT_DIMS["nt"], preferred_element_type=F32)
            zl, es, rs = [], [], []
            for jb in range(nblk):
                zb = z[:, jb * Q_BLOCK:(jb + 1) * Q_BLOCK]
                lr = _log1m_beta(zb)
                l = jnp.where(causal, lr, 0.0) if diag and jb == nblk - 1 else lr
                zl.append(zb + lr)
                es.append(_keysum2(l, above))
                rs.append(jnp.sum(l, axis=1, keepdims=True))
            a = [None] * nblk
            for jb in reversed(range(nblk)):
                ab = jnp.exp(zl[jb] + es[jb] + cl)
                if diag and jb == nblk - 1:
                    ab = jnp.where(causal, ab, 0.0)
                a[jb] = ab.astype(BF16)
                cl = cl + rs[jb]
            a = jnp.concatenate(a, axis=1)
            for h in range(2):
                acc = acc + jnp.dot(a[h * Q_BLOCK:(h + 1) * Q_BLOCK], vm_ref[h, pl.ds(c0, nblk * Q_BLOCK), :],
                                    preferred_element_type=F32)
            return acc, cl

        def qblock(i, _):
            r0 = pl.multiple_of(i * Q_BLOCK, Q_BLOCK)
            qq = _stack_heads(q_ref[pl.ds(r0, Q_BLOCK), :] * scale, heads)
            last = i // nkb
            carry = (jnp.zeros((Q_BLOCK, LANES), F32), jnp.zeros((2 * Q_BLOCK, 1), F32))
            carry = lax.switch(i % nkb, [functools.partial(step, last, qq=qq, nblk=m + 1, diag=True) for m in range(nkb)],
                               carry)

            def more(c):
                return jnp.logical_and(c[0] < last, jnp.max(c[2]) >= EXIT_LOG)

            def left(c):
                return (c[0] + 1, *step(last - 1 - c[0], c[1:], qq, nkb, False))

            seen, acc, cl = lax.while_loop(more, left, (jnp.int32(0), *carry))
            o_ref[pl.ds(r0, Q_BLOCK), :] = acc.astype(o_ref.dtype)
            tot_ref[pl.ds(r0, Q_BLOCK), :] = jnp.where(heads[0], cl[:Q_BLOCK], cl[Q_BLOCK:])
            seen_ref[pl.ds(pl.multiple_of(i * SUBLANES, SUBLANES), SUBLANES), :] = jnp.full((SUBLANES, LANES), seen, F32)
            return 0

        lax.fori_loop(0, nb, qblock, 0)

    spec = pl.BlockSpec((S, LANES), lambda h: (0, h))
    seen_spec = pl.BlockSpec((nb * SUBLANES, LANES), lambda h: (0, h))
    return pl.pallas_call(body, grid=(D // LANES,), in_specs=[spec] * 3, out_specs=[spec, spec, seen_spec],
                          out_shape=[jax.ShapeDtypeStruct((S, D), BF16), jax.ShapeDtypeStruct((S, D), F32),
                                     jax.ShapeDtypeStruct((nb * SUBLANES, D), F32)],
                          scratch_shapes=[pltpu.VMEM((2, S, LANES), BF16)], compiler_params=_cp("parallel"),
                          name=name)(q, k, v)


def attn_bwd(q, k, v, tot, seen, do, dk0, dv0, *, name):
    S, D = q.shape
    nb = S // Q_BLOCK
    tk = _key_step(S)
    nkb = tk // Q_BLOCK
    scale = 1.0 / math.sqrt(HEAD_DIM)
    has_init = dk0 is not None

    def body(*refs):
        if has_init:
            q_ref, k_ref, v_ref, tot_ref, seen_ref, do_ref, dk0_ref, dv0_ref, dq_ref, dk_ref, dv_ref, km_ref = refs
            dk_ref[...] = dk0_ref[...]
            dv_ref[...] = dv0_ref[...]
        else:
            q_ref, k_ref, v_ref, tot_ref, seen_ref, do_ref, dq_ref, dk_ref, dv_ref, km_ref = refs
            dk_ref[...] = jnp.zeros_like(dk_ref)
            dv_ref[...] = jnp.zeros_like(dv_ref)
        heads, row, col, causal = _attn_consts()
        upto = _tri(row <= col)
        before = _tri(row < col)
        for h in range(2):
            km_ref[h] = jnp.where(heads[h], k_ref[...], jnp.zeros_like(k_ref[...]))

        def step(sb, carry, qq, dd, totl, nblk, diag):
            dq, pl_, pg = carry
            c0 = pl.multiple_of(sb * tk, tk)
            keys = pl.ds(c0, nblk * Q_BLOCK)
            z = lax.dot_general(qq, k_ref[keys, :], _DOT_DIMS["nt"], preferred_element_type=F32)
            da = lax.dot_general(dd, v_ref[keys, :], _DOT_DIMS["nt"], preferred_element_type=F32)
            blocks = range(nblk)
            masked = [diag and jb == nblk - 1 for jb in blocks]
            zb = [z[:, jb * Q_BLOCK:(jb + 1) * Q_BLOCK] for jb in blocks]
            lr = [_log1m_beta(zb[jb]) for jb in blocks]
            l = [jnp.where(causal, lr[jb], 0.0) if masked[jb] else lr[jb] for jb in blocks]
            lsum = [_keysum2(l[jb], upto) for jb in blocks]
            lrow = [jnp.sum(l[jb], axis=1, keepdims=True) for jb in blocks]
            a, g = [None] * nblk, [None] * nblk
            for jb in blocks:
                ab = jnp.exp(zb[jb] + lr[jb] + (totl - pl_ - lsum[jb]))
                if masked[jb]:
                    ab = jnp.where(causal, ab, 0.0)
                g[jb] = ab * da[:, jb * Q_BLOCK:(jb + 1) * Q_BLOCK]
                a[jb] = ab.astype(BF16)
                pl_ = pl_ + lrow[jb]
            gsum = [jnp.dot(g[jb].astype(BF16), before, preferred_element_type=F32) for jb in blocks]
            grow = [jnp.sum(g[jb], axis=1, keepdims=True) for jb in blocks]
            dz = [None] * nblk
            for jb in blocks:
                dzb = g[jb] * jnp.exp(lr[jb]) - jnp.exp(zb[jb] + lr[jb]) * (pg + gsum[jb])
                if masked[jb]:
                    dzb = jnp.where(causal, dzb, 0.0)
                dz[jb] = dzb.astype(BF16)
                pg = pg + grow[jb]
            a = jnp.concatenate(a, axis=1)
            dz = jnp.concatenate(dz, axis=1)
            for h in range(2):
                dq = dq + jnp.dot(dz[h * Q_BLOCK:(h + 1) * Q_BLOCK], km_ref[h, keys, :], preferred_element_type=F32)
            dk_ref[keys, :] += lax.dot_general(dz, qq, _DOT_DIMS["tn"], preferred_element_type=F32)
            dv_ref[keys, :] += lax.dot_general(a, dd, _DOT_DIMS["tn"], preferred_element_type=F32)
            return dq, pl_, pg

        def qblock(i, _):
            r0 = pl.multiple_of(i * Q_BLOCK, Q_BLOCK)
            qq = _stack_heads(q_ref[pl.ds(r0, Q_BLOCK), :] * scale, heads)
            dd = _stack_heads(do_ref[pl.ds(r0, Q_BLOCK), :].astype(BF16), heads)
            tot2 = tot_ref[pl.ds(r0, Q_BLOCK), :]
            totl = jnp.concatenate([tot2[:, 0:1], tot2[:, HEAD_DIM:HEAD_DIM + 1]], axis=0)
            last = i // nkb
            zc = jnp.zeros((2 * Q_BLOCK, 1), F32)
            carry = (jnp.zeros((Q_BLOCK, LANES), F32), zc, zc)
            walked = jnp.max(seen_ref[pl.ds(pl.multiple_of(i * SUBLANES, SUBLANES), SUBLANES), :]).astype(jnp.int32)
            first = last - jnp.clip(walked, 0, last)
            carry = lax.fori_loop(first, last, lambda sb, c: step(sb, c, qq, dd, totl, nkb, False), carry)
            carry = lax.switch(i % nkb, [functools.partial(step, last, qq=qq, dd=dd, totl=totl, nblk=m + 1, diag=True)
                                         for m in range(nkb)], carry)
            dq_ref[pl.ds(r0, Q_BLOCK), :] = (carry[0] * scale).astype(dq_ref.dtype)
            return 0

        lax.fori_loop(0, nb, qblock, 0)

    spec = pl.BlockSpec((S, LANES), lambda h: (0, h))
    seen_spec = pl.BlockSpec((nb * SUBLANES, LANES), lambda h: (0, h))
    args = [q, k, v, tot, seen, do] + ([dk0, dv0] if has_init else [])
    return pl.pallas_call(
        body, grid=(D // LANES,), in_specs=[spec] * 4 + [seen_spec] + [spec] * (len(args) - 5), out_specs=[spec] * 3,
        out_shape=[jax.ShapeDtypeStruct((S, D), BF16), jax.ShapeDtypeStruct((S, D), F32), jax.ShapeDtypeStruct((S, D), F32)],
        scratch_shapes=[pltpu.VMEM((2, S, LANES), BF16)], compiler_params=_cp("parallel"), name=name)(*args)


def _dev_index(px, py, pc):
    return 4 * px + 2 * py + pc


def all_gather(bufs):
    nb = len(bufs)

    def body(*refs):
        ins, outs = refs[:nb], refs[nb:2 * nb]
        send_sems, recv_sems, local_sems = refs[2 * nb:]
        x, y, c = lax.axis_index("x"), lax.axis_index("y"), lax.axis_index("c")
        me, sibling = (x, y, c), (x, y, 1 - c)
        chips = [(1 - x, y), (x, 1 - y), (1 - x, 1 - y)]

        def copy(b, k, block, to, from_input=False):
            slot = outs[b].at[_dev_index(*block)]
            return pltpu.make_async_remote_copy(
                src_ref=ins[b] if from_input else slot, dst_ref=slot,
                send_sem=send_sems.at[7 * b + k], recv_sem=recv_sems.at[7 * b + k], device_id=to, device_id_type=MESH)

        mine = [pltpu.make_async_copy(ins[b], outs[b].at[_dev_index(*me)], local_sems.at[b]) for b in range(nb)]
        for cp in mine:
            cp.start()
        first = []
        for b in range(nb):
            first.append(copy(b, 0, me, sibling, from_input=True))
            first += [copy(b, 1 + j, me, (*chip, c), from_input=True) for j, chip in enumerate(chips)]
        for cp in first:
            cp.start()
        passed = []
        for j, chip in enumerate(chips):
            for b in range(nb):
                copy(b, 1 + j, (*chip, c), me).wait_recv()
                fwd = copy(b, 4 + j, (*chip, c), sibling)
                fwd.start()
                passed.append(fwd)
        for b in range(nb):
            copy(b, 0, sibling, me).wait_recv()
            for j, chip in enumerate(chips):
                copy(b, 4 + j, (*chip, 1 - c), me).wait_recv()
        for cp in first + passed:
            cp.wait_send()
        for cp in mine:
            cp.wait()

    any_spec = pl.BlockSpec(memory_space=pl.ANY)
    return pl.pallas_call(
        body, in_specs=[any_spec] * nb, out_specs=[any_spec] * nb,
        out_shape=[jax.ShapeDtypeStruct((N_DEV,) + b.shape, b.dtype) for b in bufs],
        scratch_shapes=[pltpu.SemaphoreType.DMA((7 * nb,)), pltpu.SemaphoreType.DMA((7 * nb,)),
                        pltpu.SemaphoreType.DMA((nb,))],
        name="all_gather_weights")(*bufs)


def _sources(groups):
    return [s for g in groups for (s, _) in g[3]]


def _layout(groups, refs):
    out, si = [], 0
    for g, (_, _, _, lst) in enumerate(groups):
        for (s, off) in lst:
            out.append((g, refs[si], off, s.shape[-2]))
            si += 1
    return out


def pair_exchange(groups):
    srcs = _sources(groups)
    ns, ng = len(srcs), len(groups)

    def body(*refs):
        outs = refs[ns:ns + ng]
        send_sems, recv_sems = refs[ns + ng:]
        x, y, c = lax.axis_index("x"), lax.axis_index("y"), lax.axis_index("c")
        sibling = (x, y, 1 - c)
        for (g, ref, off, r) in _layout(groups, refs[:ns]):
            for q in range(N_DEV // 2):
                pltpu.make_async_remote_copy(
                    src_ref=ref.at[2 * q + 1 - c], dst_ref=outs[g].at[q, pl.ds(off, r)], send_sem=send_sems.at[g],
                    recv_sem=recv_sems.at[g], device_id=sibling, device_id_type=MESH).start()
        whole = [pltpu.make_async_remote_copy(
            src_ref=outs[g], dst_ref=outs[g], send_sem=send_sems.at[g], recv_sem=recv_sems.at[g],
            device_id=sibling, device_id_type=MESH) for g in range(ng)]
        for w in whole:
            w.wait_recv()
        for w in whole:
            w.wait_send()

    any_spec = pl.BlockSpec(memory_space=pl.ANY)
    return pl.pallas_call(
        body, in_specs=[any_spec] * ns, out_specs=[any_spec] * ng,
        out_shape=[jax.ShapeDtypeStruct((N_DEV // 2, r, w), dt) for (r, w, dt, _) in groups],
        scratch_shapes=[pltpu.SemaphoreType.DMA((ng,)), pltpu.SemaphoreType.DMA((ng,))],
        name="pair_exchange")(*srcs)


def pair_sum(src, got, off, core, *, name):
    _, r, W = src.shape
    tr = _row_tile(r, off, 1024)
    o = off // tr

    def body(c_ref, s_ref, g_ref, o_ref):
        o_ref[...] = (s_ref[...].astype(F32) + g_ref[...].astype(F32)).astype(o_ref.dtype)

    return pl.pallas_call(
        body,
        grid_spec=pltpu.PrefetchScalarGridSpec(
            num_scalar_prefetch=1, grid=(N_DEV // 2, r // tr),
            in_specs=[pl.BlockSpec((None, None, tr, W), lambda q, i, c: (q, c[0], i, 0)),
                      pl.BlockSpec((None, tr, W), lambda q, i, c: (q, i + o, 0))],
            out_specs=pl.BlockSpec((None, tr, W), lambda q, i, c: (q, i, 0))),
        out_shape=jax.ShapeDtypeStruct((N_DEV // 2, r, W), src.dtype), compiler_params=_cp("parallel", "parallel"),
        name=name)(core, src.reshape(N_DEV // 2, 2, r, W), got)


def chip_exchange(groups):
    srcs = _sources(groups)
    ns, ng = len(srcs), len(groups)

    def body(*refs):
        outs = refs[ns:ns + ng]
        send_sems, recv_sems, local_sems = refs[ns + ng:]
        x, y, c = lax.axis_index("x"), lax.axis_index("y"), lax.axis_index("c")
        me = 2 * x + y
        layout = _layout(groups, refs[:ns])
        mine = [pltpu.make_async_copy(ref.at[me], outs[g].at[me, pl.ds(off, r)], local_sems.at[i])
                for i, (g, ref, off, r) in enumerate(layout)]
        for cp in mine:
            cp.start()
        slots = []
        for flip in range(1, N_DEV // 2):
            px, py = (1 - x if flip & 2 else x), (1 - y if flip & 1 else y)
            peer, pq = (px, py, c), 2 * px + py
            for (g, ref, off, r) in layout:
                k = 3 * g + flip - 1
                pltpu.make_async_remote_copy(
                    src_ref=ref.at[pq], dst_ref=outs[g].at[me, pl.ds(off, r)], send_sem=send_sems.at[k],
                    recv_sem=recv_sems.at[k], device_id=peer, device_id_type=MESH).start()
            for g in range(ng):
                k = 3 * g + flip - 1
                slots.append(pltpu.make_async_remote_copy(
                    src_ref=outs[g].at[pq], dst_ref=outs[g].at[pq], send_sem=send_sems.at[k],
                    recv_sem=recv_sems.at[k], device_id=peer, device_id_type=MESH))
        for w in slots:
            w.wait_recv()
        for w in slots:
            w.wait_send()
        for cp in mine:
            cp.wait()

    any_spec = pl.BlockSpec(memory_space=pl.ANY)
    return pl.pallas_call(
        body, in_specs=[any_spec] * ns, out_specs=[any_spec] * ng,
        out_shape=[jax.ShapeDtypeStruct((N_DEV // 2, r, w), dt) for (r, w, dt, _) in groups],
        scratch_shapes=[pltpu.SemaphoreType.DMA((3 * ng,)), pltpu.SemaphoreType.DMA((3 * ng,)),
                        pltpu.SemaphoreType.DMA((ns,))],
        name="chip_exchange")(*srcs)


def _row_tile(rows, off, target):
    for t in (1024, 512, 256, 128, 64, 32, 16, 8):
        if t <= target and rows % t == 0 and off % t == 0:
            return t
    raise ValueError((rows, off))


def adamw(recv, off, w, m, v, *, name):
    rows, W = w.shape
    nslot = recv.shape[0]
    tr = _row_tile(rows, off, 256)
    o = off // tr
    c1 = 1.0 - ADAM_B1 ** ADAM_STEP
    c2 = 1.0 - ADAM_B2 ** ADAM_STEP

    def body(r_ref, w_ref, m_ref, v_ref, g_ref, d_ref, mo_ref, vo_ref):
        g = r_ref[0].astype(F32)
        for j in range(1, nslot):
            g = g + r_ref[j].astype(F32)
        mn = ADAM_B1 * m_ref[...] + (1.0 - ADAM_B1) * g
        vn = ADAM_B2 * v_ref[...] + (1.0 - ADAM_B2) * (g * g)
        g_ref[...] = g
        mo_ref[...] = mn
        vo_ref[...] = vn
        d_ref[...] = -ADAM_LR * ((mn / c1) / (jnp.sqrt(vn / c2) + ADAM_EPS) + ADAM_WD * w_ref[...])

    spec = pl.BlockSpec((tr, W), lambda i: (i, 0))
    return pl.pallas_call(
        body, grid=(rows // tr,), in_specs=[pl.BlockSpec((nslot, tr, W), lambda i: (0, i + o, 0)), spec, spec, spec],
        out_specs=[spec] * 4, out_shape=[jax.ShapeDtypeStruct((rows, W), F32)] * 4,
        compiler_params=_cp("parallel"), name=name)(recv, w, m, v)


def join_columns(gathered, off, K, *, name):
    _, _, n = gathered.shape
    tr = _row_tile(K, off, 256)
    o = off // tr

    def body(i_ref, o_ref):
        for d in range(N_DEV):
            o_ref[:, d * n:(d + 1) * n] = i_ref[d]

    return pl.pallas_call(
        body, grid=(K // tr,), in_specs=[pl.BlockSpec((N_DEV, tr, n), lambda i: (0, i + o, 0))],
        out_specs=pl.BlockSpec((tr, N_DEV * n), lambda i: (i, 0)),
        out_shape=jax.ShapeDtypeStruct((K, N_DEV * n), gathered.dtype), compiler_params=_cp("parallel"),
        name=name)(gathered)


def split_columns(full, *, name):
    K, N = full.shape
    n = N // N_DEV
    tr = _row_tile(K, 0, 256)

    def body(i_ref, o_ref):
        for d in range(N_DEV):
            o_ref[d] = i_ref[:, d * n:(d + 1) * n].astype(o_ref.dtype)

    return pl.pallas_call(
        body, grid=(K // tr,), in_specs=[pl.BlockSpec((tr, N), lambda i: (i, 0))],
        out_specs=pl.BlockSpec((N_DEV, tr, n), lambda i: (0, i, 0)),
        out_shape=jax.ShapeDtypeStruct((N_DEV, K, n), BF16), compiler_params=_cp("parallel"), name=name)(full)


def _pack(arrs, dtype, row_mult):
    flat = jnp.concatenate([a.reshape(-1).astype(dtype) for a in arrs])
    rows = -(-flat.shape[0] // PACK_W)
    rows = -(-rows // row_mult) * row_mult
    return jnp.pad(flat, (0, rows * PACK_W - flat.shape[0])).reshape(rows, PACK_W)


def _pack_dev(arrs, dtype, row_mult):
    flat = jnp.concatenate([a.reshape(N_DEV, -1).astype(dtype) for a in arrs], axis=1)
    rows = -(-flat.shape[1] // PACK_W)
    rows = -(-rows // row_mult) * row_mult
    return jnp.pad(flat, ((0, 0), (0, rows * PACK_W - flat.shape[1]))).reshape(N_DEV, rows, PACK_W)


def _unpack(buf, shapes):
    lead = buf.shape[:-2]
    flat = buf.reshape(lead + (-1,))
    outs, off = [], 0
    for s in shapes:
        n = math.prod(s)
        outs.append(flat[..., off:off + n].reshape(lead + tuple(s)))
        off += n
    return outs


def _join(g, axis):
    g = jnp.moveaxis(g, 0, axis)
    return g.reshape(g.shape[:axis] + (g.shape[axis] * g.shape[axis + 1],) + g.shape[axis + 2:])


def _split(full, axis):
    s = full.shape
    g = full.reshape(s[:axis] + (N_DEV, s[axis] // N_DEV) + s[axis + 1:])
    return jnp.moveaxis(g, axis, 0)


def kernel(x, p, a_pw1_w, a_pw1_b, a_dw_w, a_dw_b, a_ln_g, a_ln_b, a_pw2_w, a_pw2_b, b_wq, kv_wk, kv_wv, b_wo, ln_mix_g, ln_mix_b, ffn_w_up, ffn_w_gate, ffn_conv_w, ffn_conv_b, ffn_w_down, ple_w_gate, ple_w_proj, ln_ffn_g, ln_ffn_b, loss_target, m_a_pw1_w, m_a_pw1_b, m_a_dw_w, m_a_dw_b, m_a_ln_g, m_a_ln_b, m_a_pw2_w, m_a_pw2_b, m_b_wq, m_kv_wk, m_kv_wv, m_b_wo, m_ln_mix_g, m_ln_mix_b, m_ffn_w_up, m_ffn_w_gate, m_ffn_conv_w, m_ffn_conv_b, m_ffn_w_down, m_ple_w_gate, m_ple_w_proj, m_ln_ffn_g, m_ln_ffn_b, v_a_pw1_w, v_a_pw1_b, v_a_dw_w, v_a_dw_b, v_a_ln_g, v_a_ln_b, v_a_pw2_w, v_a_pw2_b, v_b_wq, v_kv_wk, v_kv_wv, v_b_wo, v_ln_mix_g, v_ln_mix_b, v_ffn_w_up, v_ffn_w_gate, v_ffn_conv_w, v_ffn_conv_b, v_ffn_w_down, v_ple_w_gate, v_ple_w_proj, v_ln_ffn_g, v_ln_ffn_b):
    local = dict(a_pw1_w=a_pw1_w, a_pw1_b=a_pw1_b, a_dw_w=a_dw_w, a_dw_b=a_dw_b, a_ln_g=a_ln_g, a_ln_b=a_ln_b, a_pw2_w=a_pw2_w, a_pw2_b=a_pw2_b, b_wq=b_wq, kv_wk=kv_wk, kv_wv=kv_wv, b_wo=b_wo, ln_mix_g=ln_mix_g, ln_mix_b=ln_mix_b, ffn_w_up=ffn_w_up, ffn_w_gate=ffn_w_gate, ffn_conv_w=ffn_conv_w, ffn_conv_b=ffn_conv_b, ffn_w_down=ffn_w_down, ple_w_gate=ple_w_gate, ple_w_proj=ple_w_proj, ln_ffn_g=ln_ffn_g, ln_ffn_b=ln_ffn_b)
    mom1 = dict(a_pw1_w=m_a_pw1_w, a_pw1_b=m_a_pw1_b, a_dw_w=m_a_dw_w, a_dw_b=m_a_dw_b, a_ln_g=m_a_ln_g, a_ln_b=m_a_ln_b, a_pw2_w=m_a_pw2_w, a_pw2_b=m_a_pw2_b, b_wq=m_b_wq, kv_wk=m_kv_wk, kv_wv=m_kv_wv, b_wo=m_b_wo, ln_mix_g=m_ln_mix_g, ln_mix_b=m_ln_mix_b, ffn_w_up=m_ffn_w_up, ffn_w_gate=m_ffn_w_gate, ffn_conv_w=m_ffn_conv_w, ffn_conv_b=m_ffn_conv_b, ffn_w_down=m_ffn_w_down, ple_w_gate=m_ple_w_gate, ple_w_proj=m_ple_w_proj, ln_ffn_g=m_ln_ffn_g, ln_ffn_b=m_ln_ffn_b)
    mom2 = dict(a_pw1_w=v_a_pw1_w, a_pw1_b=v_a_pw1_b, a_dw_w=v_a_dw_w, a_dw_b=v_a_dw_b, a_ln_g=v_a_ln_g, a_ln_b=v_a_ln_b, a_pw2_w=v_a_pw2_w, a_pw2_b=v_a_pw2_b, b_wq=v_b_wq, kv_wk=v_kv_wk, kv_wv=v_kv_wv, b_wo=v_b_wo, ln_mix_g=v_ln_mix_g, ln_mix_b=v_ln_mix_b, ffn_w_up=v_ffn_w_up, ffn_w_gate=v_ffn_w_gate, ffn_conv_w=v_ffn_conv_w, ffn_conv_b=v_ffn_conv_b, ffn_w_down=v_ffn_w_down, ple_w_gate=v_ple_w_gate, ple_w_proj=v_ple_w_proj, ln_ffn_g=v_ln_ffn_g, ln_ffn_b=v_ln_ffn_b)
    small_names = [n for n, _ in SMALL]
    small_shapes = [local[n].shape for n in small_names]
    repl_shapes = [local[n].shape for n in REPL]

    widths = sorted({local[n].shape[-1] for n, _ in BIG}, reverse=True)
    groups = {w: [n for n, _ in BIG if local[n].shape[-1] == w] for w in widths}
    offset, rows_of = {}, {}
    for w, names in groups.items():
        off = 0
        for n in names:
            offset[n], rows_of[n] = off, math.prod(local[n].shape[:-1])
            off += rows_of[n]
    sends = [jnp.concatenate([local[n].reshape(-1, w).astype(BF16) for n in names]) for w, names in groups.items()]
    gathered = all_gather(sends + [_pack([local[n] for n in small_names], F32, SUBLANES)])
    gath = dict(zip(widths, gathered[:-1]))
    W = {}
    for n, ax in BIG:
        w = local[n].shape[-1]
        nl = local[n].shape[0] if local[n].ndim == 3 else 1
        per = rows_of[n] // nl
        if ax == local[n].ndim - 1:
            W[n] = [join_columns(gath[w], offset[n] + l * per, per, name=f"join_{n}_{l}") for l in range(nl)]
        else:
            W[n] = [gath[w][:, offset[n] + l * per:offset[n] + (l + 1) * per].reshape(N_DEV * per, w) for l in range(nl)]
    for n in ("kv_wk", "kv_wv"):
        W[n] = W[n][0]
    W.update({n: _join(g, ax) for (n, ax), g in zip(SMALL, _unpack(gathered[-1], small_shapes))})
    W.update({n: local[n] for n in REPL})

    xs = x[0]
    S, D = xs.shape
    x_in, r1s, x1s, r2s, us, gps, gs, hhs, pgls, pps = [], [], [], [], [], [], [], [], [], []
    h1s, h2s, h3s, h5s, qs, os_, tots = {}, {}, {}, {}, {}, {}, {}
    kk = vv = None
    for i in range(DEPTH):
        x_in.append(xs)
        if i < N_A:
            h1 = mm(xs, W["a_pw1_w"][i], "nn", bias=W["a_pw1_b"][i][None], out_dtype=BF16, name=f"pw1_{i}")
            h2 = glu_fwd(h1, name=f"glu_{i}")
            h3, h5 = conv_ln_silu_fwd(h2, W["a_dw_w"][i], W["a_dw_b"][i][None], W["a_ln_g"][i][None],
                                      W["a_ln_b"][i][None], name=f"dwconv_{i}")
            mix = mm(h5, W["a_pw2_w"][i], "nn", bias=W["a_pw2_b"][i][None], name=f"pw2_{i}")
            h1s[i], h2s[i], h3s[i], h5s[i] = h1, h2, h3, h5
        else:
            j = i - N_A
            if kk is None:
                kk = mm(xs, W["kv_wk"], "nn", out_dtype=BF16, name="proj_k")
                vv = mm(xs, W["kv_wv"], "nn", out_dtype=BF16, name="proj_v")
            q = mm(xs, W["b_wq"][j], "nn", out_dtype=BF16, name=f"proj_q_{i}")
            o, tot, seen = attn_fwd(q, kk, vv, name=f"attn_{i}")
            mix = mm(o, W["b_wo"][j], "nn", name=f"proj_o_{i}")
            qs[i], os_[i], tots[i] = q, o, (tot, seen)
        r1, x1 = res_ln(xs, mix, W["ln_mix_g"][i][None], W["ln_mix_b"][i][None], name=f"ln_mix_{i}")
        u, gp = mm(x1, W["ffn_w_up"][i], "nn", out_dtype=BF16, also=W["ffn_w_gate"][i], name=f"ffn_up_gate_{i}")
        g, hh = conv_act_fwd(gp, u, W["ffn_conv_w"][i], W["ffn_conv_b"][i][None], name=f"ffn_conv_{i}")
        f = mm(hh, W["ffn_w_down"][i], "nn", name=f"ffn_down_{i}")
        pgl = mm(x1, W["ple_w_gate"][i], "nn", out_dtype=BF16, name=f"ple_gate_{i}")
        pp = mm(p[i, 0], W["ple_w_proj"][i], "nn", out_dtype=BF16, name=f"ple_proj_{i}")
        r2, xs = res_ln(x1, f, W["ln_ffn_g"][i][None], W["ln_ffn_b"][i][None], ple=(pgl, pp), name=f"ln_ffn_{i}")
        for lst, val in ((r1s, r1), (x1s, x1), (r2s, r2), (us, u), (gps, gp), (gs, g), (hhs, hh), (pgls, pgl), (pps, pp)):
            lst.append(val)

    dx, loss_part = loss_grad(xs, loss_target[0], name="loss")
    G = {n: [None] * local[n].shape[0] for n in WEIGHTS if n not in ("kv_wk", "kv_wv")}
    dk = dv = None
    for i in reversed(range(DEPTH)):
        x1 = x1s[i]
        dr2, dpp, dpgl, G["ln_ffn_g"][i], G["ln_ffn_b"][i] = ln_ple_bwd(r2s[i], W["ln_ffn_g"][i][None], dx, pgls[i], pps[i],
                                                                      name=f"ln_ffn_bwd_{i}")
        dhh = mm(dr2, W["ffn_w_down"][i], "nt", out_dtype=BF16, name=f"ffn_down_dx_{i}")
        G["ffn_w_down"][i] = mm(hhs[i], dr2, "tn", out_dtype=BF16, name=f"ffn_down_dw_{i}")
        G["ple_w_proj"][i] = mm(p[i, 0], dpp, "tn", out_dtype=BF16, name=f"ple_proj_dw_{i}")
        G["ple_w_gate"][i] = mm(x1, dpgl, "tn", out_dtype=BF16, name=f"ple_gate_dw_{i}")
        du, dgp, G["ffn_conv_w"][i], G["ffn_conv_b"][i] = ffn_gate_bwd(dhh, us[i], gs[i], gps[i], W["ffn_conv_w"][i],
                                                                       name=f"ffn_gate_bwd_{i}")
        G["ffn_w_up"][i], G["ffn_w_gate"][i] = mm(x1, du, "tn", out_dtype=BF16, also=dgp, name=f"ffn_up_gate_dw_{i}")
        dx1 = mm(du, W["ffn_w_up"][i], "nt", add=dr2, add_scale=DN_ALPHA, plus=(dgp, W["ffn_w_gate"][i]),
                 name=f"ffn_up_gate_dx_{i}")
        dx1 = mm(dpgl, W["ple_w_gate"][i], "nt", add=dx1, name=f"ple_gate_dx_{i}")
        dr1, G["ln_mix_g"][i], G["ln_mix_b"][i], dr1_sum = ln_bwd(r1s[i], W["ln_mix_g"][i][None], dx1, name=f"ln_mix_bwd_{i}")
        if i < N_A:
            G["a_pw2_w"][i] = mm(h5s[i], dr1, "tn", out_dtype=BF16, name=f"pw2_dw_{i}")
            G["a_pw2_b"][i] = dr1_sum
            dh5 = mm(dr1, W["a_pw2_w"][i], "nt", name=f"pw2_dx_{i}")
            dh3, G["a_ln_g"][i], G["a_ln_b"][i] = ln_silu_bwd(h3s[i], W["a_ln_g"][i][None], W["a_ln_b"][i][None], dh5,
                                                             name=f"dwconv_ln_bwd_{i}")
            dh2 = conv_bwd_x(dh3, W["a_dw_w"][i], out_dtype=F32, name=f"dwconv_dx_{i}")
            G["a_dw_w"][i], G["a_dw_b"][i] = conv_bwd_w(h2s[i], dh3, CONV_W, name=f"dwconv_dw_{i}")
            dh1, G["a_pw1_b"][i] = glu_bwd(h1s[i], dh2, name=f"glu_bwd_{i}")
            G["a_pw1_w"][i] = mm(x_in[i], dh1, "tn", out_dtype=BF16, name=f"pw1_dw_{i}")
            dx = mm(dh1, W["a_pw1_w"][i], "nt", add=dr1, add_scale=DN_ALPHA, name=f"pw1_dx_{i}")
        else:
            j = i - N_A
            G["b_wo"][j] = mm(os_[i], dr1, "tn", out_dtype=BF16, name=f"proj_o_dw_{i}")
            do = mm(dr1, W["b_wo"][j], "nt", out_dtype=BF16, name=f"proj_o_dx_{i}")
            dq, dk, dv = attn_bwd(qs[i], kk, vv, *tots[i], do, dk, dv, name=f"attn_bwd_{i}")
            G["b_wq"][j] = mm(x_in[i], dq, "tn", out_dtype=BF16, name=f"proj_q_dw_{i}")
            dx = mm(dq, W["b_wq"][j], "nt", add=dr1, add_scale=DN_ALPHA, name=f"proj_q_dx_{i}")
            if j == 0:
                G["kv_wk"] = mm(x_in[i], dk, "tn", out_dtype=BF16, name="proj_k_dw")
                G["kv_wv"] = mm(x_in[i], dv, "tn", out_dtype=BF16, name="proj_v_dw")
                dx = mm(dk, W["kv_wk"], "nt", add=dx, name="proj_k_dx")
                dx = mm(dv, W["kv_wv"], "nt", add=dx, name="proj_v_dx")
    grad_x = dx[None]
    shard_axis = dict(BIG + SMALL)
    for n in small_names + list(REPL):
        full = list(local[n].shape)
        if n in shard_axis:
            full[shard_axis[n]] *= N_DEV
        G[n] = jnp.stack(G[n]).reshape(full)

    n_small = sum(math.prod(s) for s in small_shapes)
    n_repl = sum(math.prod(s) for s in repl_shapes)
    repl_flat = jnp.concatenate([G[n].reshape(-1) for n in REPL] + [loss_part.reshape(-1)[:1]])
    send_small = _pack_dev([_split(G[n], ax) for n, ax in SMALL] + [jnp.broadcast_to(repl_flat, (N_DEV, n_repl + 1))],
                           F32, SUBLANES)
    ex_groups = []
    for w, names in groups.items():
        lst = []
        for n in names:
            layers = G[n] if isinstance(G[n], list) else [G[n]]
            per = rows_of[n] // len(layers)
            for l, g in enumerate(layers):
                if shard_axis[n] == local[n].ndim - 1:
                    src = split_columns(g, name=f"split_{n}_{l}")
                else:
                    src = g.reshape(N_DEV, per, w)
                lst.append((src, offset[n] + l * per))
        ex_groups.append((sum(rows_of[n] for n in names), w, BF16, lst))
    ex_groups.append((send_small.shape[1], PACK_W, F32, [(send_small, 0)]))
    gots = pair_exchange(ex_groups)
    core = lax.axis_index("c").astype(jnp.int32).reshape(1)
    sum_groups = [(rows, w, dt, [(pair_sum(src, got, off, core, name=f"pair_sum_{gi}_{si}"), off) for si, (src, off) in enumerate(lst)])
                  for gi, ((rows, w, dt, lst), got) in enumerate(zip(ex_groups, gots))]
    recvs = chip_exchange(sum_groups)
    recv = dict(zip(widths, recvs[:-1]))
    recv_small = recvs[-1]

    out = {}
    for n, _ in BIG:
        w = local[n].shape[-1]
        res = adamw(recv[w], offset[n], local[n].reshape(-1, w), mom1[n].reshape(-1, w), mom2[n].reshape(-1, w),
                    name=f"adamw_{n}")
        out[n] = [r.reshape(local[n].shape) for r in res]

    def state(d):
        small = _pack([d[n] for n in small_names] + [d[n] for n in REPL], F32, SUBLANES)
        return jnp.pad(small, ((0, recv_small.shape[1] - small.shape[0]), (0, 0)))

    out_small = adamw(recv_small, 0, state(local), state(mom1), state(mom2), name="adamw_vectors")
    loss = out_small[0].reshape(-1)[n_small + n_repl]
    vecs = [dict(zip(small_names + list(REPL), _unpack(o, small_shapes + repl_shapes))) for o in out_small]
    per_kind = [[out[n][kind] if n in out else vecs[kind][n] for n in WEIGHTS] for kind in range(4)]
    grads, deltas, new_m, new_v = per_kind
    return (loss, grad_x, *grads, *deltas, *new_m, *new_v)
```

```python
import functools
import math

import jax
import jax.numpy as jnp
from jax import lax
from jax.experimental import pallas as pl
from jax.experimental.pallas import tpu as pltpu

F32 = jnp.float32
BF16 = jnp.bfloat16
MESH = pl.DeviceIdType.MESH

N_DEV = 8
DEPTH = 4
N_A = 2
HEAD_DIM = 64
Q_BLOCK = 128
CONV_W = 31
FFN_CONV_W = 3
LN_EPS = 1e-5
DN_ALPHA = (2.0 * DEPTH) ** 0.25
ADAM_LR = 0.001
ADAM_B1 = 0.9
ADAM_B2 = 0.999
ADAM_EPS = 1e-08
ADAM_WD = 0.01
ADAM_STEP = 10

LANES = 128
SUBLANES = 8
PACK_W = 1024
VMEM_LIMIT = 56 * 1024 * 1024

BIG = (("a_pw1_w", 2), ("a_pw2_w", 1), ("b_wq", 1), ("kv_wk", 0), ("kv_wv", 0), ("b_wo", 1),
       ("ffn_w_up", 2), ("ffn_w_gate", 2), ("ffn_w_down", 1), ("ple_w_gate", 1), ("ple_w_proj", 2))
SMALL = (("a_pw1_b", 1), ("a_dw_w", 2), ("a_dw_b", 1), ("a_ln_g", 1), ("a_ln_b", 1), ("a_pw2_b", 1),
         ("ffn_conv_w", 2))
REPL = ("ln_mix_g", "ln_mix_b", "ffn_conv_b", "ln_ffn_g", "ln_ffn_b")
WEIGHTS = ("a_pw1_w", "a_pw1_b", "a_dw_w", "a_dw_b", "a_ln_g", "a_ln_b", "a_pw2_w", "a_pw2_b", "b_wq", "kv_wk",
           "kv_wv", "b_wo", "ln_mix_g", "ln_mix_b", "ffn_w_up", "ffn_w_gate", "ffn_conv_w", "ffn_conv_b",
           "ffn_w_down", "ple_w_gate", "ple_w_proj", "ln_ffn_g", "ln_ffn_b")


def _cp(*sem):
    return pltpu.CompilerParams(dimension_semantics=sem, vmem_limit_bytes=VMEM_LIMIT)


def _pick(dim, target, align=LANES):
    if dim <= target:
        return dim
    t = (target // align) * align
    while t >= align:
        if dim % t == 0:
            return t
        t -= align
    return dim


MM_ROWS = 512
MM_ROWS_TN = 1536
MM_COLS = 1536
MM_DEPTH = 2816
MM_DEPTH_TN = 1536

_DOT_DIMS = {"nn": (((1,), (0,)), ((), ())), "nt": (((1,), (1,)), ((), ())), "tn": (((0,), (0,)), ((), ()))}


def mm(a, b, mode, *, bias=None, add=None, add_scale=1.0, out_dtype=F32, also=None, plus=None, name):
    if mode == "tn":
        K, M = a.shape
    else:
        M, K = a.shape
    N = b.shape[0] if mode == "nt" else b.shape[1]
    tm = _pick(M, MM_ROWS_TN if mode == "tn" else MM_ROWS if plus is None else MM_ROWS // 2)
    tn = _pick(N, MM_COLS)
    tk = _pick(K, MM_DEPTH_TN if mode == "tn" else MM_DEPTH)
    nk = K // tk
    dims = _DOT_DIMS[mode]
    n_out = 1 if also is None else 2

    def body(*refs):
        refs = list(refs)
        a_ref, b_ref = refs.pop(0), refs.pop(0)
        b2_ref = refs.pop(0) if also is not None else None
        a3_ref, b3_ref = (refs.pop(0), refs.pop(0)) if plus is not None else (None, None)
        bias_ref = refs.pop(0) if bias is not None else None
        add_ref = refs.pop(0) if add is not None else None
        o_refs, acc_refs = refs[:n_out], refs[n_out:]
        k = pl.program_id(2)

        @pl.when(k == 0)
        def _():
            for acc_ref in acc_refs:
                acc_ref[...] = jnp.zeros_like(acc_ref)

        a_v = a_ref[...].astype(BF16)
        acc_refs[0][...] += lax.dot_general(a_v, b_ref[...].astype(BF16), dims, preferred_element_type=F32)
        if b2_ref is not None:
            acc_refs[1][...] += lax.dot_general(a_v, b2_ref[...].astype(BF16), dims, preferred_element_type=F32)
        if a3_ref is not None:
            acc_refs[0][...] += lax.dot_general(a3_ref[...].astype(BF16), b3_ref[...].astype(BF16), dims,
                                                preferred_element_type=F32)

        @pl.when(k == nk - 1)
        def _():
            for o_ref, acc_ref in zip(o_refs, acc_refs):
                r = acc_ref[...]
                if bias_ref is not None:
                    r = r + bias_ref[...]
                if add_ref is not None:
                    r = r + add_scale * add_ref[...].astype(F32)
                o_ref[...] = r.astype(o_ref.dtype)

    a_spec = pl.BlockSpec((tk, tm), lambda j, i, k: (k, i)) if mode == "tn" else pl.BlockSpec((tm, tk), lambda j, i, k: (i, k))
    b_spec = pl.BlockSpec((tn, tk), lambda j, i, k: (j, k)) if mode == "nt" else pl.BlockSpec((tk, tn), lambda j, i, k: (k, j))
    in_specs, args = [a_spec, b_spec], [a, b]
    if also is not None:
        in_specs.append(b_spec)
        args.append(also)
    if plus is not None:
        in_specs += [a_spec, b_spec]
        args += list(plus)
    if bias is not None:
        in_specs.append(pl.BlockSpec((1, tn), lambda j, i, k: (0, j)))
        args.append(bias)
    if add is not None:
        in_specs.append(pl.BlockSpec((tm, tn), lambda j, i, k: (i, j)))
        args.append(add)
    o_spec = pl.BlockSpec((tm, tn), lambda j, i, k: (i, j))
    outs = pl.pallas_call(
        body, grid=(N // tn, M // tm, nk), in_specs=in_specs,
        out_specs=[o_spec] * n_out, out_shape=[jax.ShapeDtypeStruct((M, N), out_dtype)] * n_out,
        scratch_shapes=[pltpu.VMEM((tm, tn), F32)] * n_out,
        compiler_params=_cp("parallel", "parallel", "arbitrary"), name=name)(*args)
    return outs[0] if also is None else tuple(outs)


ROW_BLOCK = 512


def _rows(body, *, n_rows, tm, row_ins, full_ins=(), row_outs=(), acc_outs=(), scratch=(), reverse=False, name):
    n = n_rows // tm

    def rmap(i):
        return (n - 1 - i, 0) if reverse else (i, 0)

    in_specs = [pl.BlockSpec((tm, a.shape[1]), rmap) for a in row_ins]
    in_specs += [pl.BlockSpec(a.shape, lambda i, nd=a.ndim: (0,) * nd) for a in full_ins]
    out_shape = [jax.ShapeDtypeStruct((n_rows, w), dt) for (w, dt) in row_outs]
    out_shape += [jax.ShapeDtypeStruct(s, dt) for (s, dt) in acc_outs]
    out_specs = [pl.BlockSpec((tm, w), rmap) for (w, dt) in row_outs]
    out_specs += [pl.BlockSpec(s, lambda i, nd=len(s): (0,) * nd) for (s, dt) in acc_outs]
    return pl.pallas_call(
        functools.partial(body, n), grid=(n,), in_specs=in_specs, out_specs=out_specs, out_shape=out_shape,
        scratch_shapes=list(scratch), compiler_params=_cp("arbitrary"), name=name)(*row_ins, *full_ins)


def _sigmoid(x):
    return 1.0 / (1.0 + jnp.exp(-x))


def _ln_hat(r):
    mu = jnp.mean(r, axis=-1, keepdims=True)
    xc = r - mu
    var = jnp.mean(xc * xc, axis=-1, keepdims=True)
    rstd = lax.rsqrt(var + LN_EPS)
    return xc * rstd, rstd


def _ln_back(xhat, rstd, g, dy):
    dxh = dy * g
    m1 = jnp.mean(dxh, axis=-1, keepdims=True)
    m2 = jnp.mean(dxh * xhat, axis=-1, keepdims=True)
    return rstd * (dxh - m1 - xhat * m2)


def _colsum(x):
    return jnp.sum(x, axis=0, keepdims=True)


def _acc(i, ref, val):
    @pl.when(i == 0)
    def _():
        ref[...] = val

    @pl.when(i > 0)
    def _():
        ref[...] += val


def res_ln(x, mix, g, b, *, ple=None, name):
    S, D = x.shape

    def body(n, *refs):
        if ple is None:
            x_ref, m_ref, g_ref, b_ref, r_ref, y_ref = refs
            r = DN_ALPHA * x_ref[...] + m_ref[...]
        else:
            x_ref, m_ref, pgl_ref, pp_ref, g_ref, b_ref, r_ref, y_ref = refs
            r = DN_ALPHA * x_ref[...] + m_ref[...] + _sigmoid(pgl_ref[...].astype(F32)) * pp_ref[...].astype(F32)
        xhat, _ = _ln_hat(r)
        r_ref[...] = r
        y_ref[...] = xhat * g_ref[...] + b_ref[...]

    row_ins = [x, mix] + ([] if ple is None else list(ple))
    return _rows(body, n_rows=S, tm=_pick(S, ROW_BLOCK, SUBLANES), row_ins=row_ins, full_ins=[g, b],
                 row_outs=[(D, F32), (D, F32)], name=name)


def ln_bwd(r, g, dy, *, name):
    S, D = r.shape

    def body(n, r_ref, dy_ref, g_ref, dr_ref, dg_ref, db_ref, ds_ref):
        i = pl.program_id(0)
        xhat, rstd = _ln_hat(r_ref[...])
        dy_v = dy_ref[...]
        dr = _ln_back(xhat, rstd, g_ref[...], dy_v)
        dr_ref[...] = dr
        _acc(i, dg_ref, _colsum(dy_v * xhat))
        _acc(i, db_ref, _colsum(dy_v))
        _acc(i, ds_ref, _colsum(dr))

    return _rows(body, n_rows=S, tm=_pick(S, ROW_BLOCK, SUBLANES), row_ins=[r, dy], full_ins=[g],
                 row_outs=[(D, F32)], acc_outs=[((1, D), F32)] * 3, name=name)


def glu_fwd(h1, *, name):
    S, D2 = h1.shape
    D = D2 // 2

    def body(n, h_ref, o_ref):
        o_ref[...] = h_ref[:, :D].astype(F32) * _sigmoid(h_ref[:, D:].astype(F32))

    return _rows(body, n_rows=S, tm=_pick(S, ROW_BLOCK, SUBLANES), row_ins=[h1], row_outs=[(D, F32)], name=name)[0]


def glu_bwd(h1, dh2, *, name):
    S, D2 = h1.shape
    D = D2 // 2

    def body(n, h_ref, d_ref, o_ref, s_ref):
        i = pl.program_id(0)
        a, sg, d = h_ref[:, :D].astype(F32), _sigmoid(h_ref[:, D:].astype(F32)), d_ref[...]
        da = d * sg
        dg = d * a * sg * (1.0 - sg)
        o_ref[:, :D] = da.astype(o_ref.dtype)
        o_ref[:, D:] = dg.astype(o_ref.dtype)
        _acc(i, s_ref, jnp.concatenate([_colsum(da), _colsum(dg)], axis=1))

    return _rows(body, n_rows=S, tm=_pick(S, ROW_BLOCK, SUBLANES), row_ins=[h1, dh2], row_outs=[(D2, BF16)],
                 acc_outs=[((1, D2), F32)], name=name)


CONV_ROWS = 32
CONV_LANES = 256


def _halo(k):
    return -(-(k - 1) // SUBLANES) * SUBLANES


def _phases(offs):
    return sorted({o % SUBLANES for o in offs} - {0})


def _shift_scratch(offs, n_rows, width):
    return pltpu.VMEM((max(len(_phases(offs)), 1), n_rows, width), F32)


def _make_shifted(buf_ref, sh_ref, offs):
    n = buf_ref.shape[0] - SUBLANES
    for p, b in enumerate(_phases(offs)):
        sh_ref[p, pl.ds(0, n), :] = buf_ref[pl.ds(b, n), :]


def _tap(buf_ref, sh_ref, offs, k, rc, rows, lc, lw):
    b = offs[k] % SUBLANES
    src = buf_ref if b == 0 else sh_ref.at[_phases(offs).index(b)]
    return src[pl.ds(offs[k] - b + rc, rows), pl.ds(lc, lw)]


def _conv_taps(buf_ref, sh_ref, w_ref, offs, tm, width, emit):
    _make_shifted(buf_ref, sh_ref, offs)
    rows = min(CONV_ROWS, tm)
    for lc in range(0, width, CONV_LANES):
        lw = min(CONV_LANES, width - lc)
        for rc in range(0, tm, rows):
            acc = None
            for k in range(len(offs)):
                t = _tap(buf_ref, sh_ref, offs, k, rc, rows, lc, lw) * w_ref[pl.ds(k, 1), pl.ds(lc, lw)]
                acc = t if acc is None else acc + t
            emit(rc, lc, lw, rows, acc)


def _fill_causal(i, buf_ref, x_ref, halo, tm):
    @pl.when(i == 0)
    def _():
        buf_ref[pl.ds(0, halo), :] = jnp.zeros((halo, buf_ref.shape[1]), F32)

    @pl.when(i > 0)
    def _():
        buf_ref[pl.ds(0, halo), :] = buf_ref[pl.ds(tm, halo), :]

    buf_ref[pl.ds(halo, tm), :] = x_ref[...].astype(F32)


def conv_ln_silu_fwd(x, w, b, g, beta, *, name):
    S, C = x.shape
    K = w.shape[0]
    halo = _halo(K)
    tm = _pick(S, 256, SUBLANES)
    offs = [halo - (K - 1) + k for k in range(K)]

    def body(n, x_ref, w_ref, b_ref, g_ref, beta_ref, h3_ref, h5_ref, buf_ref, sh_ref):
        i = pl.program_id(0)
        _fill_causal(i, buf_ref, x_ref, halo, tm)

        def emit(rc, lc, lw, rows, acc):
            h3_ref[pl.ds(rc, rows), pl.ds(lc, lw)] = acc + b_ref[:, pl.ds(lc, lw)]

        _conv_taps(buf_ref, sh_ref, w_ref, offs, tm, C, emit)
        xhat, _ = _ln_hat(h3_ref[...])
        h4 = xhat * g_ref[...] + beta_ref[...]
        h5_ref[...] = (h4 * _sigmoid(h4)).astype(h5_ref.dtype)

    return _rows(body, n_rows=S, tm=tm, row_ins=[x], full_ins=[w, b, g, beta], row_outs=[(C, F32), (C, BF16)],
                 scratch=[pltpu.VMEM((tm + halo, C), F32), _shift_scratch(offs, tm + halo, C)], name=name)


def conv_act_fwd(gp, u, w, b, *, name):
    S, C = gp.shape
    K = w.shape[0]
    halo = _halo(K)
    tm = _pick(S, 256, SUBLANES)
    offs = [halo - (K - 1) + k for k in range(K)]

    def body(n, x_ref, u_ref, w_ref, b_ref, g_ref, hh_ref, buf_ref, sh_ref):
        i = pl.program_id(0)
        _fill_causal(i, buf_ref, x_ref, halo, tm)

        def emit(rc, lc, lw, rows, acc):
            gv = acc + b_ref[:, pl.ds(lc, lw)]
            g_ref[pl.ds(rc, rows), pl.ds(lc, lw)] = gv.astype(g_ref.dtype)
            hh_ref[pl.ds(rc, rows), pl.ds(lc, lw)] = (gv * _sigmoid(gv) * u_ref[pl.ds(rc, rows), pl.ds(lc, lw)].astype(F32)).astype(hh_ref.dtype)

        _conv_taps(buf_ref, sh_ref, w_ref, offs, tm, C, emit)

    return _rows(body, n_rows=S, tm=tm, row_ins=[gp, u], full_ins=[w, b], row_outs=[(C, BF16), (C, BF16)],
                 scratch=[pltpu.VMEM((tm + halo, C), F32), _shift_scratch(offs, tm + halo, C)], name=name)


def conv_bwd_x(dy, w, *, out_dtype, name):
    S, C = dy.shape
    K = w.shape[0]
    halo = _halo(K)
    tm = _pick(S, 256, SUBLANES)
    offs = [K - 1 - k for k in range(K)]

    def body(n, dy_ref, w_ref, dx_ref, buf_ref, sh_ref):
        i = pl.program_id(0)

        @pl.when(i == 0)
        def _():
            buf_ref[pl.ds(tm, halo), :] = jnp.zeros((halo, C), F32)

        @pl.when(i > 0)
        def _():
            buf_ref[pl.ds(tm, halo), :] = buf_ref[pl.ds(0, halo), :]

        buf_ref[pl.ds(0, tm), :] = dy_ref[...].astype(F32)

        def emit(rc, lc, lw, rows, acc):
            dx_ref[pl.ds(rc, rows), pl.ds(lc, lw)] = acc.astype(dx_ref.dtype)

        _conv_taps(buf_ref, sh_ref, w_ref, offs, tm, C, emit)

    return _rows(body, n_rows=S, tm=tm, row_ins=[dy], full_ins=[w], row_outs=[(C, out_dtype)],
                 scratch=[pltpu.VMEM((tm + halo, C), F32), _shift_scratch(offs, tm + halo, C)], reverse=True, name=name)[0]


def conv_bwd_w(x, dy, K, *, name):
    S, C = x.shape
    halo = _halo(K)
    tm = _pick(S, 256, SUBLANES)
    offs = [halo - (K - 1) + k for k in range(K)]
    rows = min(CONV_ROWS, tm)

    def body(n, x_ref, dy_ref, dw_ref, db_ref, buf_ref, acc_ref, sh_ref):
        i = pl.program_id(0)
        _fill_causal(i, buf_ref, x_ref, halo, tm)
        _make_shifted(buf_ref, sh_ref, offs)

        @pl.when(i == 0)
        def _():
            acc_ref[...] = jnp.zeros_like(acc_ref)

        for lc in range(0, C, CONV_LANES):
            lw = min(CONV_LANES, C - lc)
            for k in range(K):
                s = None
                for rc in range(0, tm, rows):
                    t = dy_ref[pl.ds(rc, rows), pl.ds(lc, lw)].astype(F32) * _tap(buf_ref, sh_ref, offs, k, rc, rows, lc, lw)
                    s = t if s is None else s + t
                s8 = s[0:SUBLANES]
                for q in range(1, rows // SUBLANES):
                    s8 = s8 + s[q * SUBLANES:(q + 1) * SUBLANES]
                acc_ref[pl.ds(k * SUBLANES, SUBLANES), pl.ds(lc, lw)] += s8
        _acc(i, db_ref, _colsum(dy_ref[...].astype(F32)))

        @pl.when(i == n - 1)
        def _():
            for k in range(K):
                dw_ref[pl.ds(k, 1), :] = _colsum(acc_ref[pl.ds(k * SUBLANES, SUBLANES), :])

    return _rows(body, n_rows=S, tm=tm, row_ins=[x, dy], acc_outs=[((K, C), F32), ((1, C), F32)],
                 scratch=[pltpu.VMEM((tm + halo, C), F32), pltpu.VMEM((K * SUBLANES, C), F32),
                          _shift_scratch(offs, tm + halo, C)], name=name)


def ln_silu_bwd(h3, g, beta, dh5, *, name):
    S, C = h3.shape

    def body(n, h_ref, d_ref, g_ref, beta_ref, dh_ref, dg_ref, db_ref):
        i = pl.program_id(0)
        xhat, rstd = _ln_hat(h_ref[...])
        h4 = xhat * g_ref[...] + beta_ref[...]
        sg = _sigmoid(h4)
        dh4 = d_ref[...] * sg * (1.0 + h4 * (1.0 - sg))
        dh_ref[...] = _ln_back(xhat, rstd, g_ref[...], dh4)
        _acc(i, dg_ref, _colsum(dh4 * xhat))
        _acc(i, db_ref, _colsum(dh4))

    return _rows(body, n_rows=S, tm=_pick(S, ROW_BLOCK, SUBLANES), row_ins=[h3, dh5], full_ins=[g, beta],
                 row_outs=[(C, F32)], acc_outs=[((1, C), F32)] * 2, name=name)


def ffn_gate_bwd(dhh, u, g, gp, w, *, name):
    S, C = u.shape
    K = w.shape[0]
    halo = _halo(K)
    tm = _pick(S, 256, SUBLANES)
    offs = [K - 1 - k for k in range(K)]
    rows = min(CONV_ROWS, tm)

    def body(n, d_ref, u_ref, g_ref, gp_ref, w_ref, du_ref, dgp_ref, dw_ref, db_ref, buf_ref, sh_ref, acc_ref):
        i = pl.program_id(0)

        @pl.when(i == 0)
        def _():
            buf_ref[pl.ds(tm, halo), :] = jnp.zeros((halo, C), F32)
            acc_ref[...] = jnp.zeros_like(acc_ref)

        @pl.when(i > 0)
        def _():
            buf_ref[pl.ds(tm, halo), :] = buf_ref[pl.ds(0, halo), :]

        d, gv = d_ref[...].astype(F32), g_ref[...].astype(F32)
        sg = _sigmoid(gv)
        du_ref[...] = (d * gv * sg).astype(du_ref.dtype)
        dg = d * u_ref[...].astype(F32) * sg * (1.0 + gv * (1.0 - sg))
        buf_ref[pl.ds(0, tm), :] = dg
        _acc(i, db_ref, _colsum(dg))

        def emit(rc, lc, lw, nrows, acc):
            dgp_ref[pl.ds(rc, nrows), pl.ds(lc, lw)] = acc.astype(dgp_ref.dtype)

        _conv_taps(buf_ref, sh_ref, w_ref, offs, tm, C, emit)
        for lc in range(0, C, CONV_LANES):
            lw = min(CONV_LANES, C - lc)
            for k in range(K):
                s_ = None
                for rc in range(0, tm, rows):
                    t = gp_ref[pl.ds(rc, rows), pl.ds(lc, lw)].astype(F32) * _tap(buf_ref, sh_ref, offs, k, rc, rows, lc, lw)
                    s_ = t if s_ is None else s_ + t
                s8 = s_[0:SUBLANES]
                for q in range(1, rows // SUBLANES):
                    s8 = s8 + s_[q * SUBLANES:(q + 1) * SUBLANES]
                acc_ref[pl.ds(k * SUBLANES, SUBLANES), pl.ds(lc, lw)] += s8

        @pl.when(i == n - 1)
        def _():
            for k in range(K):
                dw_ref[pl.ds(k, 1), :] = _colsum(acc_ref[pl.ds(k * SUBLANES, SUBLANES), :])

    return _rows(body, n_rows=S, tm=tm, row_ins=[dhh, u, g, gp], full_ins=[w], row_outs=[(C, BF16), (C, BF16)],
                 acc_outs=[((K, C), F32), ((1, C), F32)],
                 scratch=[pltpu.VMEM((tm + halo, C), F32), _shift_scratch(offs, tm + halo, C),
                          pltpu.VMEM((K * SUBLANES, C), F32)], reverse=True, name=name)


def ln_ple_bwd(r, g, dy, pgl, pp, *, name):
    S, D = r.shape

    def body(n, r_ref, dy_ref, l_ref, p_ref, g_ref, dr_ref, dpp_ref, dpl_ref, dg_ref, db_ref):
        i = pl.program_id(0)
        xhat, rstd = _ln_hat(r_ref[...])
        dy_v = dy_ref[...]
        dr = _ln_back(xhat, rstd, g_ref[...], dy_v)
        dr_ref[...] = dr
        sg = _sigmoid(l_ref[...].astype(F32))
        dpp_ref[...] = (dr * sg).astype(dpp_ref.dtype)
        dpl_ref[...] = (dr * p_ref[...].astype(F32) * sg * (1.0 - sg)).astype(dpl_ref.dtype)
        _acc(i, dg_ref, _colsum(dy_v * xhat))
        _acc(i, db_ref, _colsum(dy_v))

    return _rows(body, n_rows=S, tm=_pick(S, ROW_BLOCK, SUBLANES), row_ins=[r, dy, pgl, pp], full_ins=[g],
                 row_outs=[(D, F32), (D, BF16), (D, BF16)], acc_outs=[((1, D), F32)] * 2, name=name)


def loss_grad(y, target, *, name):
    S, D = y.shape

    def body(n, y_ref, t_ref, dy_ref, l_ref):
        i = pl.program_id(0)
        e = y_ref[...] - t_ref[...]
        dy_ref[...] = e * (1.0 / D)
        s = jnp.sum(_colsum(e * e), axis=1, keepdims=True) * (0.5 / D)
        _acc(i, l_ref, jnp.broadcast_to(s, (1, LANES)))

    return _rows(body, n_rows=S, tm=_pick(S, ROW_BLOCK, SUBLANES), row_ins=[y, target], row_outs=[(D, F32)],
                 acc_outs=[((1, LANES), F32)], name=name)


def _key_step(S):
    return min(512, S // 2)


EXIT_LOG = -110.0


def _attn_consts():
    lane = lax.broadcasted_iota(jnp.int32, (1, LANES), 1)
    heads = (lane < HEAD_DIM, lane >= HEAD_DIM)
    row = lax.broadcasted_iota(jnp.int32, (Q_BLOCK, Q_BLOCK), 0)
    col = lax.broadcasted_iota(jnp.int32, (Q_BLOCK, Q_BLOCK), 1)
    causal = jnp.concatenate([col < row] * 2, axis=0)
    return heads, row, col, causal


def _tri(cond):
    return jnp.where(cond, 1.0, 0.0).astype(BF16)


def _keysum2(x, tri):
    hi = x.astype(BF16)
    lo = (x - hi.astype(F32)).astype(BF16)
    return jnp.dot(jnp.concatenate([hi, lo], axis=1), jnp.concatenate([tri, tri], axis=0),
                   preferred_element_type=F32)


def _stack_heads(x, heads):
    return jnp.concatenate([jnp.where(m, x, jnp.zeros_like(x)) for m in heads], axis=0)


def _log1m_beta(z):
    return -(jnp.maximum(z, 0.0) + jnp.log(1.0 + jnp.exp(-jnp.abs(z))))


def attn_fwd(q, k, v, *, name):
    S, D = q.shape
    nb = S // Q_BLOCK
    tk = _key_step(S)
    nkb = tk // Q_BLOCK
    scale = 1.0 / math.sqrt(HEAD_DIM)

    def body(q_ref, k_ref, v_ref, o_ref, tot_ref, seen_ref, vm_ref):
        heads, row, col, causal = _attn_consts()
        above = _tri(row > col)
        for h in range(2):
            vm_ref[h] = jnp.where(heads[h], v_ref[...], jnp.zeros_like(v_ref[...]))

        def step(sb, carry, qq, nblk, diag):
            acc, cl = carry
            c0 = pl.multiple_of(sb * tk, tk)
            z = lax.dot_general(qq, k_ref[pl.ds(c0, nblk * Q_BLOCK), :], _DOT_DIMS["nt"], preferred_element_type=F32)
            zl, es, rs = [], [], []
            for jb in range(nblk):
                zb = z[:, jb * Q_BLOCK:(jb + 1) * Q_BLOCK]
                lr = _log1m_beta(zb)
                l = jnp.where(causal, lr, 0.0) if diag and jb == nblk - 1 else lr
                zl.append(zb + lr)
                es.append(_keysum2(l, above))
                rs.append(jnp.sum(l, axis=1, keepdims=True))
            a = [None] * nblk
            for jb in reversed(range(nblk)):
                ab = jnp.exp(zl[jb] + es[jb] + cl)
                if diag and jb == nblk - 1:
                    ab = jnp.where(causal, ab, 0.0)
                a[jb] = ab.astype(BF16)
                cl = cl + rs[jb]
            a = jnp.concatenate(a, axis=1)
            for h in range(2):
                acc = acc + jnp.dot(a[h * Q_BLOCK:(h + 1) * Q_BLOCK], vm_ref[h, pl.ds(c0, nblk * Q_BLOCK), :],
                                    preferred_element_type=F32)
            return acc, cl

        def qblock(i, _):
            r0 = pl.multiple_of(i * Q_BLOCK, Q_BLOCK)
            qq = _stack_heads(q_ref[pl.ds(r0, Q_BLOCK), :] * scale, heads)
            last = i // nkb
            carry = (jnp.zeros((Q_BLOCK, LANES), F32), jnp.zeros((2 * Q_BLOCK, 1), F32))
            carry = lax.switch(i % nkb, [functools.partial(step, last, qq=qq, nblk=m + 1, diag=True) for m in range(nkb)],
                               carry)

            def more(c):
                return jnp.logical_and(c[0] < last, jnp.max(c[2]) >= EXIT_LOG)

            def left(c):
                return (c[0] + 1, *step(last - 1 - c[0], c[1:], qq, nkb, False))

            seen, acc, cl = lax.while_loop(more, left, (jnp.int32(0), *carry))
            o_ref[pl.ds(r0, Q_BLOCK), :] = acc.astype(o_ref.dtype)
            tot_ref[pl.ds(r0, Q_BLOCK), :] = jnp.where(heads[0], cl[:Q_BLOCK], cl[Q_BLOCK:])
            seen_ref[pl.ds(pl.multiple_of(i * SUBLANES, SUBLANES), SUBLANES), :] = jnp.full((SUBLANES, LANES), seen, F32)
            return 0

        lax.fori_loop(0, nb, qblock, 0)

    spec = pl.BlockSpec((S, LANES), lambda h: (0, h))
    seen_spec = pl.BlockSpec((nb * SUBLANES, LANES), lambda h: (0, h))
    return pl.pallas_call(body, grid=(D // LANES,), in_specs=[spec] * 3, out_specs=[spec, spec, seen_spec],
                          out_shape=[jax.ShapeDtypeStruct((S, D), BF16), jax.ShapeDtypeStruct((S, D), F32),
                                     jax.ShapeDtypeStruct((nb * SUBLANES, D), F32)],
                          scratch_shapes=[pltpu.VMEM((2, S, LANES), BF16)], compiler_params=_cp("parallel"),
                          name=name)(q, k, v)


def attn_bwd(q, k, v, tot, seen, do, dk0, dv0, *, name):
    S, D = q.shape
    nb = S // Q_BLOCK
    tk = _key_step(S)
    nkb = tk // Q_BLOCK
    scale = 1.0 / math.sqrt(HEAD_DIM)
    has_init = dk0 is not None

    def body(*refs):
        if has_init:
            q_ref, k_ref, v_ref, tot_ref, seen_ref, do_ref, dk0_ref, dv0_ref, dq_ref, dk_ref, dv_ref, km_ref = refs
            dk_ref[...] = dk0_ref[...]
            dv_ref[...] = dv0_ref[...]
        else:
            q_ref, k_ref, v_ref, tot_ref, seen_ref, do_ref, dq_ref, dk_ref, dv_ref, km_ref = refs
            dk_ref[...] = jnp.zeros_like(dk_ref)
            dv_ref[...] = jnp.zeros_like(dv_ref)
        heads, row, col, causal = _attn_consts()
        upto = _tri(row <= col)
        before = _tri(row < col)
        for h in range(2):
            km_ref[h] = jnp.where(heads[h], k_ref[...], jnp.zeros_like(k_ref[...]))

        def step(sb, carry, qq, dd, totl, nblk, diag):
            dq, pl_, pg = carry
            c0 = pl.multiple_of(sb * tk, tk)
            keys = pl.ds(c0, nblk * Q_BLOCK)
            z = lax.dot_general(qq, k_ref[keys, :], _DOT_DIMS["nt"], preferred_element_type=F32)
            da = lax.dot_general(dd, v_ref[keys, :], _DOT_DIMS["nt"], preferred_element_type=F32)
            blocks = range(nblk)
            masked = [diag and jb == nblk - 1 for jb in blocks]
            zb = [z[:, jb * Q_BLOCK:(jb + 1) * Q_BLOCK] for jb in blocks]
            lr = [_log1m_beta(zb[jb]) for jb in blocks]
            l = [jnp.where(causal, lr[jb], 0.0) if masked[jb] else lr[jb] for jb in blocks]
            lsum = [_keysum2(l[jb], upto) for jb in blocks]
            lrow = [jnp.sum(l[jb], axis=1, keepdims=True) for jb in blocks]
            a, g = [None] * nblk, [None] * nblk
            for jb in blocks:
                ab = jnp.exp(zb[jb] + lr[jb] + (totl - pl_ - lsum[jb]))
                if masked[jb]:
                    ab = jnp.where(causal, ab, 0.0)
                g[jb] = ab * da[:, jb * Q_BLOCK:(jb + 1) * Q_BLOCK]
                a[jb] = ab.astype(BF16)
                pl_ = pl_ + lrow[jb]
            gsum = [jnp.dot(g[jb].astype(BF16), before, preferred_element_type=F32) for jb in blocks]
            grow = [jnp.sum(g[jb], axis=1, keepdims=True) for jb in blocks]
            dz = [None] * nblk
            for jb in blocks:
                dzb = g[jb] * jnp.exp(lr[jb]) - jnp.exp(zb[jb] + lr[jb]) * (pg + gsum[jb])
                if masked[jb]:
                    dzb = jnp.where(causal, dzb, 0.0)
                dz[jb] = dzb.astype(BF16)
                pg = pg + grow[jb]
            a = jnp.concatenate(a, axis=1)
            dz = jnp.concatenate(dz, axis=1)
            for h in range(2):
                dq = dq + jnp.dot(dz[h * Q_BLOCK:(h + 1) * Q_BLOCK], km_ref[h, keys, :], preferred_element_type=F32)
            dk_ref[keys, :] += lax.dot_general(dz, qq, _DOT_DIMS["tn"], preferred_element_type=F32)
            dv_ref[keys, :] += lax.dot_general(a, dd, _DOT_DIMS["tn"], preferred_element_type=F32)
            return dq, pl_, pg

        def qblock(i, _):
            r0 = pl.multiple_of(i * Q_BLOCK, Q_BLOCK)
            qq = _stack_heads(q_ref[pl.ds(r0, Q_BLOCK), :] * scale, heads)
            dd = _stack_heads(do_ref[pl.ds(r0, Q_BLOCK), :].astype(BF16), heads)
            tot2 = tot_ref[pl.ds(r0, Q_BLOCK), :]
            totl = jnp.concatenate([tot2[:, 0:1], tot2[:, HEAD_DIM:HEAD_DIM + 1]], axis=0)
            last = i // nkb
            zc = jnp.zeros((2 * Q_BLOCK, 1), F32)
            carry = (jnp.zeros((Q_BLOCK, LANES), F32), zc, zc)
            walked = jnp.max(seen_ref[pl.ds(pl.multiple_of(i * SUBLANES, SUBLANES), SUBLANES), :]).astype(jnp.int32)
            first = last - jnp.clip(walked, 0, last)
            carry = lax.fori_loop(first, last, lambda sb, c: step(sb, c, qq, dd, totl, nkb, False), carry)
            carry = lax.switch(i % nkb, [functools.partial(step, last, qq=qq, dd=dd, totl=totl, nblk=m + 1, diag=True)
                                         for m in range(nkb)], carry)
            dq_ref[pl.ds(r0, Q_BLOCK), :] = (carry[0] * scale).astype(dq_ref.dtype)
            return 0

        lax.fori_loop(0, nb, qblock, 0)

    spec = pl.BlockSpec((S, LANES), lambda h: (0, h))
    seen_spec = pl.BlockSpec((nb * SUBLANES, LANES), lambda h: (0, h))
    args = [q, k, v, tot, seen, do] + ([dk0, dv0] if has_init else [])
    return pl.pallas_call(
        body, grid=(D // LANES,), in_specs=[spec] * 4 + [seen_spec] + [spec] * (len(args) - 5), out_specs=[spec] * 3,
        out_shape=[jax.ShapeDtypeStruct((S, D), BF16), jax.ShapeDtypeStruct((S, D), F32), jax.ShapeDtypeStruct((S, D), F32)],
        scratch_shapes=[pltpu.VMEM((2, S, LANES), BF16)], compiler_params=_cp("parallel"), name=name)(*args)


def _dev_index(px, py, pc):
    return 4 * px + 2 * py + pc


def all_gather(bufs):
    nb = len(bufs)

    def body(*refs):
        ins, outs = refs[:nb], refs[nb:2 * nb]
        send_sems, recv_sems, local_sems = refs[2 * nb:]
        x, y, c = lax.axis_index("x"), lax.axis_index("y"), lax.axis_index("c")
        me, sibling = (x, y, c), (x, y, 1 - c)
        chips = [(1 - x, y), (x, 1 - y), (1 - x, 1 - y)]

        def copy(b, k, block, to, from_input=False):
            slot = outs[b].at[_dev_index(*block)]
            return pltpu.make_async_remote_copy(
                src_ref=ins[b] if from_input else slot, dst_ref=slot,
                send_sem=send_sems.at[7 * b + k], recv_sem=recv_sems.at[7 * b + k], device_id=to, device_id_type=MESH)

        mine = [pltpu.make_async_copy(ins[b], outs[b].at[_dev_index(*me)], local_sems.at[b]) for b in range(nb)]
        for cp in mine:
            cp.start()
        first = []
        for b in range(nb):
            first.append(copy(b, 0, me, sibling, from_input=True))
            first += [copy(b, 1 + j, me, (*chip, c), from_input=True) for j, chip in enumerate(chips)]
        for cp in first:
            cp.start()
        passed = []
        for j, chip in enumerate(chips):
            for b in range(nb):
                copy(b, 1 + j, (*chip, c), me).wait_recv()
                fwd = copy(b, 4 + j, (*chip, c), sibling)
                fwd.start()
                passed.append(fwd)
        for b in range(nb):
            copy(b, 0, sibling, me).wait_recv()
            for j, chip in enumerate(chips):
                copy(b, 4 + j, (*chip, 1 - c), me).wait_recv()
        for cp in first + passed:
            cp.wait_send()
        for cp in mine:
            cp.wait()

    any_spec = pl.BlockSpec(memory_space=pl.ANY)
    return pl.pallas_call(
        body, in_specs=[any_spec] * nb, out_specs=[any_spec] * nb,
        out_shape=[jax.ShapeDtypeStruct((N_DEV,) + b.shape, b.dtype) for b in bufs],
        scratch_shapes=[pltpu.SemaphoreType.DMA((7 * nb,)), pltpu.SemaphoreType.DMA((7 * nb,)),
                        pltpu.SemaphoreType.DMA((nb,))],
        name="all_gather_weights")(*bufs)


def _sources(groups):
    return [s for g in groups for (s, _) in g[3]]


def _layout(groups, refs):
    out, si = [], 0
    for g, (_, _, _, lst) in enumerate(groups):
        for (s, off) in lst:
            out.append((g, refs[si], off, s.shape[-2]))
            si += 1
    return out


def pair_exchange(groups):
    srcs = _sources(groups)
    ns, ng = len(srcs), len(groups)

    def body(*refs):
        outs = refs[ns:ns + ng]
        send_sems, recv_sems = refs[ns + ng:]
        x, y, c = lax.axis_index("x"), lax.axis_index("y"), lax.axis_index("c")
        sibling = (x, y, 1 - c)
        for (g, ref, off, r) in _layout(groups, refs[:ns]):
            for q in range(N_DEV // 2):
                pltpu.make_async_remote_copy(
                    src_ref=ref.at[2 * q + 1 - c], dst_ref=outs[g].at[q, pl.ds(off, r)], send_sem=send_sems.at[g],
                    recv_sem=recv_sems.at[g], device_id=sibling, device_id_type=MESH).start()
        whole = [pltpu.make_async_remote_copy(
            src_ref=outs[g], dst_ref=outs[g], send_sem=send_sems.at[g], recv_sem=recv_sems.at[g],
            device_id=sibling, device_id_type=MESH) for g in range(ng)]
        for w in whole:
            w.wait_recv()
        for w in whole:
            w.wait_send()

    any_spec = pl.BlockSpec(memory_space=pl.ANY)
    return pl.pallas_call(
        body, in_specs=[any_spec] * ns, out_specs=[any_spec] * ng,
        out_shape=[jax.ShapeDtypeStruct((N_DEV // 2, r, w), dt) for (r, w, dt, _) in groups],
        scratch_shapes=[pltpu.SemaphoreType.DMA((ng,)), pltpu.SemaphoreType.DMA((ng,))],
        name="pair_exchange")(*srcs)


def pair_sum(src, got, off, core, *, name):
    _, r, W = src.shape
    tr = _row_tile(r, off, 1024)
    o = off // tr

    def body(c_ref, s_ref, g_ref, o_ref):
        o_ref[...] = (s_ref[...].astype(F32) + g_ref[...].astype(F32)).astype(o_ref.dtype)

    return pl.pallas_call(
        body,
        grid_spec=pltpu.PrefetchScalarGridSpec(
            num_scalar_prefetch=1, grid=(N_DEV // 2, r // tr),
            in_specs=[pl.BlockSpec((None, None, tr, W), lambda q, i, c: (q, c[0], i, 0)),
                      pl.BlockSpec((None, tr, W), lambda q, i, c: (q, i + o, 0))],
            out_specs=pl.BlockSpec((None, tr, W), lambda q, i, c: (q, i, 0))),
        out_shape=jax.ShapeDtypeStruct((N_DEV // 2, r, W), src.dtype), compiler_params=_cp("parallel", "parallel"),
        name=name)(core, src.reshape(N_DEV // 2, 2, r, W), got)


def chip_exchange(groups):
    srcs = _sources(groups)
    ns, ng = len(srcs), len(groups)

    def body(*refs):
        outs = refs[ns:ns + ng]
        send_sems, recv_sems, local_sems = refs[ns + ng:]
        x, y, c = lax.axis_index("x"), lax.axis_index("y"), lax.axis_index("c")
        me = 2 * x + y
        layout = _layout(groups, refs[:ns])
        mine = [pltpu.make_async_copy(ref.at[me], outs[g].at[me, pl.ds(off, r)], local_sems.at[i])
                for i, (g, ref, off, r) in enumerate(layout)]
        for cp in mine:
            cp.start()
        slots = []
        for flip in range(1, N_DEV // 2):
            px, py = (1 - x if flip & 2 else x), (1 - y if flip & 1 else y)
            peer, pq = (px, py, c), 2 * px + py
            for (g, ref, off, r) in layout:
                k = 3 * g + flip - 1
                pltpu.make_async_remote_copy(
                    src_ref=ref.at[pq], dst_ref=outs[g].at[me, pl.ds(off, r)], send_sem=send_sems.at[k],
                    recv_sem=recv_sems.at[k], device_id=peer, device_id_type=MESH).start()
            for g in range(ng):
                k = 3 * g + flip - 1
                slots.append(pltpu.make_async_remote_copy(
                    src_ref=outs[g].at[pq], dst_ref=outs[g].at[pq], send_sem=send_sems.at[k],
                    recv_sem=recv_sems.at[k], device_id=peer, device_id_type=MESH))
        for w in slots:
            w.wait_recv()
        for w in slots:
            w.wait_send()
        for cp in mine:
            cp.wait()

    any_spec = pl.BlockSpec(memory_space=pl.ANY)
    return pl.pallas_call(
        body, in_specs=[any_spec] * ns, out_specs=[any_spec] * ng,
        out_shape=[jax.ShapeDtypeStruct((N_DEV // 2, r, w), dt) for (r, w, dt, _) in groups],
        scratch_shapes=[pltpu.SemaphoreType.DMA((3 * ng,)), pltpu.SemaphoreType.DMA((3 * ng,)),
                        pltpu.SemaphoreType.DMA((ns,))],
        name="chip_exchange")(*srcs)


def _row_tile(rows, off, target):
    for t in (1024, 512, 256, 128, 64, 32, 16, 8):
        if t <= target and rows % t == 0 and off % t == 0:
            return t
    raise ValueError((rows, off))


def adamw(recv, off, w, m, v, *, name):
    rows, W = w.shape
    nslot = recv.shape[0]
    tr = _row_tile(rows, off, 256)
    o = off // tr
    c1 = 1.0 - ADAM_B1 ** ADAM_STEP
    c2 = 1.0 - ADAM_B2 ** ADAM_STEP

    def body(r_ref, w_ref, m_ref, v_ref, g_ref, d_ref, mo_ref, vo_ref):
        g = r_ref[0].astype(F32)
        for j in range(1, nslot):
            g = g + r_ref[j].astype(F32)
        mn = ADAM_B1 * m_ref[...] + (1.0 - ADAM_B1) * g
        vn = ADAM_B2 * v_ref[...] + (1.0 - ADAM_B2) * (g * g)
        g_ref[...] = g
        mo_ref[...] = mn
        vo_ref[...] = vn
        d_ref[...] = -ADAM_LR * ((mn / c1) / (jnp.sqrt(vn / c2) + ADAM_EPS) + ADAM_WD * w_ref[...])

    spec = pl.BlockSpec((tr, W), lambda i: (i, 0))
    return pl.pallas_call(
        body, grid=(rows // tr,), in_specs=[pl.BlockSpec((nslot, tr, W), lambda i: (0, i + o, 0)), spec, spec, spec],
        out_specs=[spec] * 4, out_shape=[jax.ShapeDtypeStruct((rows, W), F32)] * 4,
        compiler_params=_cp("parallel"), name=name)(recv, w, m, v)


def join_columns(gathered, off, K, *, name):
    _, _, n = gathered.shape
    tr = _row_tile(K, off, 256)
    o = off // tr

    def body(i_ref, o_ref):
        for d in range(N_DEV):
            o_ref[:, d * n:(d + 1) * n] = i_ref[d]

    return pl.pallas_call(
        body, grid=(K // tr,), in_specs=[pl.BlockSpec((N_DEV, tr, n), lambda i: (0, i + o, 0))],
        out_specs=pl.BlockSpec((tr, N_DEV * n), lambda i: (i, 0)),
        out_shape=jax.ShapeDtypeStruct((K, N_DEV * n), gathered.dtype), compiler_params=_cp("parallel"),
        name=name)(gathered)


def split_columns(full, *, name):
    K, N = full.shape
    n = N // N_DEV
    tr = _row_tile(K, 0, 256)

    def body(i_ref, o_ref):
        for d in range(N_DEV):
            o_ref[d] = i_ref[:, d * n:(d + 1) * n].astype(o_ref.dtype)

    return pl.pallas_call(
        body, grid=(K // tr,), in_specs=[pl.BlockSpec((tr, N), lambda i: (i, 0))],
        out_specs=pl.BlockSpec((N_DEV, tr, n), lambda i: (0, i, 0)),
        out_shape=jax.ShapeDtypeStruct((N_DEV, K, n), BF16), compiler_params=_cp("parallel"), name=name)(full)


def _pack(arrs, dtype, row_mult):
    flat = jnp.concatenate([a.reshape(-1).astype(dtype) for a in arrs])
    rows = -(-flat.shape[0] // PACK_W)
    rows = -(-rows // row_mult) * row_mult
    return jnp.pad(flat, (0, rows * PACK_W - flat.shape[0])).reshape(rows, PACK_W)


def _pack_dev(arrs, dtype, row_mult):
    flat = jnp.concatenate([a.reshape(N_DEV, -1).astype(dtype) for a in arrs], axis=1)
    rows = -(-flat.shape[1] // PACK_W)
    rows = -(-rows // row_mult) * row_mult
    return jnp.pad(flat, ((0, 0), (0, rows * PACK_W - flat.shape[1]))).reshape(N_DEV, rows, PACK_W)


def _unpack(buf, shapes):
    lead = buf.shape[:-2]
    flat = buf.reshape(lead + (-1,))
    outs, off = [], 0
    for s in shapes:
        n = math.prod(s)
        outs.append(flat[..., off:off + n].reshape(lead + tuple(s)))
        off += n
    return outs


def _join(g, axis):
    g = jnp.moveaxis(g, 0, axis)
    return g.reshape(g.shape[:axis] + (g.shape[axis] * g.shape[axis + 1],) + g.shape[axis + 2:])


def _split(full, axis):
    s = full.shape
    g = full.reshape(s[:axis] + (N_DEV, s[axis] // N_DEV) + s[axis + 1:])
    return jnp.moveaxis(g, axis, 0)


def kernel(x, p, a_pw1_w, a_pw1_b, a_dw_w, a_dw_b, a_ln_g, a_ln_b, a_pw2_w, a_pw2_b, b_wq, kv_wk, kv_wv, b_wo, ln_mix_g, ln_mix_b, ffn_w_up, ffn_w_gate, ffn_conv_w, ffn_conv_b, ffn_w_down, ple_w_gate, ple_w_proj, ln_ffn_g, ln_ffn_b, loss_target, m_a_pw1_w, m_a_pw1_b, m_a_dw_w, m_a_dw_b, m_a_ln_g, m_a_ln_b, m_a_pw2_w, m_a_pw2_b, m_b_wq, m_kv_wk, m_kv_wv, m_b_wo, m_ln_mix_g, m_ln_mix_b, m_ffn_w_up, m_ffn_w_gate, m_ffn_conv_w, m_ffn_conv_b, m_ffn_w_down, m_ple_w_gate, m_ple_w_proj, m_ln_ffn_g, m_ln_ffn_b, v_a_pw1_w, v_a_pw1_b, v_a_dw_w, v_a_dw_b, v_a_ln_g, v_a_ln_b, v_a_pw2_w, v_a_pw2_b, v_b_wq, v_kv_wk, v_kv_wv, v_b_wo, v_ln_mix_g, v_ln_mix_b, v_ffn_w_up, v_ffn_w_gate, v_ffn_conv_w, v_ffn_conv_b, v_ffn_w_down, v_ple_w_gate, v_ple_w_proj, v_ln_ffn_g, v_ln_ffn_b):
    local = dict(a_pw1_w=a_pw1_w, a_pw1_b=a_pw1_b, a_dw_w=a_dw_w, a_dw_b=a_dw_b, a_ln_g=a_ln_g, a_ln_b=a_ln_b, a_pw2_w=a_pw2_w, a_pw2_b=a_pw2_b, b_wq=b_wq, kv_wk=kv_wk, kv_wv=kv_wv, b_wo=b_wo, ln_mix_g=ln_mix_g, ln_mix_b=ln_mix_b, ffn_w_up=ffn_w_up, ffn_w_gate=ffn_w_gate, ffn_conv_w=ffn_conv_w, ffn_conv_b=ffn_conv_b, ffn_w_down=ffn_w_down, ple_w_gate=ple_w_gate, ple_w_proj=ple_w_proj, ln_ffn_g=ln_ffn_g, ln_ffn_b=ln_ffn_b)
    mom1 = dict(a_pw1_w=m_a_pw1_w, a_pw1_b=m_a_pw1_b, a_dw_w=m_a_dw_w, a_dw_b=m_a_dw_b, a_ln_g=m_a_ln_g, a_ln_b=m_a_ln_b, a_pw2_w=m_a_pw2_w, a_pw2_b=m_a_pw2_b, b_wq=m_b_wq, kv_wk=m_kv_wk, kv_wv=m_kv_wv, b_wo=m_b_wo, ln_mix_g=m_ln_mix_g, ln_mix_b=m_ln_mix_b, ffn_w_up=m_ffn_w_up, ffn_w_gate=m_ffn_w_gate, ffn_conv_w=m_ffn_conv_w, ffn_conv_b=m_ffn_conv_b, ffn_w_down=m_ffn_w_down, ple_w_gate=m_ple_w_gate, ple_w_proj=m_ple_w_proj, ln_ffn_g=m_ln_ffn_g, ln_ffn_b=m_ln_ffn_b)
    mom2 = dict(a_pw1_w=v_a_pw1_w, a_pw1_b=v_a_pw1_b, a_dw_w=v_a_dw_w, a_dw_b=v_a_dw_b, a_ln_g=v_a_ln_g, a_ln_b=v_a_ln_b, a_pw2_w=v_a_pw2_w, a_pw2_b=v_a_pw2_b, b_wq=v_b_wq, kv_wk=v_kv_wk, kv_wv=v_kv_wv, b_wo=v_b_wo, ln_mix_g=v_ln_mix_g, ln_mix_b=v_ln_mix_b, ffn_w_up=v_ffn_w_up, ffn_w_gate=v_ffn_w_gate, ffn_conv_w=v_ffn_conv_w, ffn_conv_b=v_ffn_conv_b, ffn_w_down=v_ffn_w_down, ple_w_gate=v_ple_w_gate, ple_w_proj=v_ple_w_proj, ln_ffn_g=v_ln_ffn_g, ln_ffn_b=v_ln_ffn_b)
    small_names = [n for n, _ in SMALL]
    small_shapes = [local[n].shape for n in small_names]
    repl_shapes = [local[n].shape for n in REPL]

    widths = sorted({local[n].shape[-1] for n, _ in BIG}, reverse=True)
    groups = {w: [n for n, _ in BIG if local[n].shape[-1] == w] for w in widths}
    offset, rows_of = {}, {}
    for w, names in groups.items():
        off = 0
        for n in names:
            offset[n], rows_of[n] = off, math.prod(local[n].shape[:-1])
            off += rows_of[n]
    sends = [jnp.concatenate([local[n].reshape(-1, w).astype(BF16) for n in names]) for w, names in groups.items()]
    gathered = all_gather(sends + [_pack([local[n] for n in small_names], F32, SUBLANES)])
    gath = dict(zip(widths, gathered[:-1]))
    W = {}
    for n, ax in BIG:
        w = local[n].shape[-1]
        nl = local[n].shape[0] if local[n].ndim == 3 else 1
        per = rows_of[n] // nl
        if ax == local[n].ndim - 1:
            W[n] = [join_columns(gath[w], offset[n] + l * per, per, name=f"join_{n}_{l}") for l in range(nl)]
        else:
            W[n] = [gath[w][:, offset[n] + l * per:offset[n] + (l + 1) * per].reshape(N_DEV * per, w) for l in range(nl)]
    for n in ("kv_wk", "kv_wv"):
        W[n] = W[n][0]
    W.update({n: _join(g, ax) for (n, ax), g in zip(SMALL, _unpack(gathered[-1], small_shapes))})
    W.update({n: local[n] for n in REPL})

    xs = x[0]
    S, D = xs.shape
    x_in, r1s, x1s, r2s, us, gps, gs, hhs, pgls, pps = [], [], [], [], [], [], [], [], [], []
    h1s, h2s, h3s, h5s, qs, os_, tots = {}, {}, {}, {}, {}, {}, {}
    kk = vv = None
    for i in range(DEPTH):
        x_in.append(xs)
        if i < N_A:
            h1 = mm(xs, W["a_pw1_w"][i], "nn", bias=W["a_pw1_b"][i][None], out_dtype=BF16, name=f"pw1_{i}")
            h2 = glu_fwd(h1, name=f"glu_{i}")
            h3, h5 = conv_ln_silu_fwd(h2, W["a_dw_w"][i], W["a_dw_b"][i][None], W["a_ln_g"][i][None],
                                      W["a_ln_b"][i][None], name=f"dwconv_{i}")
            mix = mm(h5, W["a_pw2_w"][i], "nn", bias=W["a_pw2_b"][i][None], name=f"pw2_{i}")
            h1s[i], h2s[i], h3s[i], h5s[i] = h1, h2, h3, h5
        else:
            j = i - N_A
            if kk is None:
                kk, vv = mm(xs, W["kv_wk"], "nn", out_dtype=BF16, also=W["kv_wv"], name="proj_kv")
            q = mm(xs, W["b_wq"][j], "nn", out_dtype=BF16, name=f"proj_q_{i}")
            o, tot, seen = attn_fwd(q, kk, vv, name=f"attn_{i}")
            mix = mm(o, W["b_wo"][j], "nn", name=f"proj_o_{i}")
            qs[i], os_[i], tots[i] = q, o, (tot, seen)
        r1, x1 = res_ln(xs, mix, W["ln_mix_g"][i][None], W["ln_mix_b"][i][None], name=f"ln_mix_{i}")
        u, gp = mm(x1, W["ffn_w_up"][i], "nn", out_dtype=BF16, also=W["ffn_w_gate"][i], name=f"ffn_up_gate_{i}")
        g, hh = conv_act_fwd(gp, u, W["ffn_conv_w"][i], W["ffn_conv_b"][i][None], name=f"ffn_conv_{i}")
        f = mm(hh, W["ffn_w_down"][i], "nn", name=f"ffn_down_{i}")
        pgl = mm(x1, W["ple_w_gate"][i], "nn", out_dtype=BF16, name=f"ple_gate_{i}")
        pp = mm(p[i, 0], W["ple_w_proj"][i], "nn", out_dtype=BF16, name=f"ple_proj_{i}")
        r2, xs = res_ln(x1, f, W["ln_ffn_g"][i][None], W["ln_ffn_b"][i][None], ple=(pgl, pp), name=f"ln_ffn_{i}")
        for lst, val in ((r1s, r1), (x1s, x1), (r2s, r2), (us, u), (gps, gp), (gs, g), (hhs, hh), (pgls, pgl), (pps, pp)):
            lst.append(val)

    dx, loss_part = loss_grad(xs, loss_target[0], name="loss")
    G = {n: [None] * local[n].shape[0] for n in WEIGHTS if n not in ("kv_wk", "kv_wv")}
    dk = dv = None
    for i in reversed(range(DEPTH)):
        x1 = x1s[i]
        dr2, dpp, dpgl, G["ln_ffn_g"][i], G["ln_ffn_b"][i] = ln_ple_bwd(r2s[i], W["ln_ffn_g"][i][None], dx, pgls[i], pps[i],
                                                                      name=f"ln_ffn_bwd_{i}")
        dhh = mm(dr2, W["ffn_w_down"][i], "nt", out_dtype=BF16, name=f"ffn_down_dx_{i}")
        G["ffn_w_down"][i] = mm(hhs[i], dr2, "tn", out_dtype=BF16, name=f"ffn_down_dw_{i}")
        G["ple_w_proj"][i] = mm(p[i, 0], dpp, "tn", out_dtype=BF16, name=f"ple_proj_dw_{i}")
        G["ple_w_gate"][i] = mm(x1, dpgl, "tn", out_dtype=BF16, name=f"ple_gate_dw_{i}")
        du, dgp, G["ffn_conv_w"][i], G["ffn_conv_b"][i] = ffn_gate_bwd(dhh, us[i], gs[i], gps[i], W["ffn_conv_w"][i],
                                                                       name=f"ffn_gate_bwd_{i}")
        G["ffn_w_up"][i], G["ffn_w_gate"][i] = mm(x1, du, "tn", out_dtype=BF16, also=dgp, name=f"ffn_up_gate_dw_{i}")
        dx1 = mm(du, W["ffn_w_up"][i], "nt", add=dr2, add_scale=DN_ALPHA, plus=(dgp, W["ffn_w_gate"][i]),
                 name=f"ffn_up_gate_dx_{i}")
        dx1 = mm(dpgl, W["ple_w_gate"][i], "nt", add=dx1, name=f"ple_gate_dx_{i}")
        dr1, G["ln_mix_g"][i], G["ln_mix_b"][i], dr1_sum = ln_bwd(r1s[i], W["ln_mix_g"][i][None], dx1, name=f"ln_mix_bwd_{i}")
        if i < N_A:
            G["a_pw2_w"][i] = mm(h5s[i], dr1, "tn", out_dtype=BF16, name=f"pw2_dw_{i}")
            G["a_pw2_b"][i] = dr1_sum
            dh5 = mm(dr1, W["a_pw2_w"][i], "nt", name=f"pw2_dx_{i}")
            dh3, G["a_ln_g"][i], G["a_ln_b"][i] = ln_silu_bwd(h3s[i], W["a_ln_g"][i][None], W["a_ln_b"][i][None], dh5,
                                                             name=f"dwconv_ln_bwd_{i}")
            dh2 = conv_bwd_x(dh3, W["a_dw_w"][i], out_dtype=F32, name=f"dwconv_dx_{i}")
            G["a_dw_w"][i], G["a_dw_b"][i] = conv_bwd_w(h2s[i], dh3, CONV_W, name=f"dwconv_dw_{i}")
            dh1, G["a_pw1_b"][i] = glu_bwd(h1s[i], dh2, name=f"glu_bwd_{i}")
            G["a_pw1_w"][i] = mm(x_in[i], dh1, "tn", out_dtype=BF16, name=f"pw1_dw_{i}")
            dx = mm(dh1, W["a_pw1_w"][i], "nt", add=dr1, add_scale=DN_ALPHA, name=f"pw1_dx_{i}")
        else:
            j = i - N_A
            G["b_wo"][j] = mm(os_[i], dr1, "tn", out_dtype=BF16, name=f"proj_o_dw_{i}")
            do = mm(dr1, W["b_wo"][j], "nt", out_dtype=BF16, name=f"proj_o_dx_{i}")
            dq, dk, dv = attn_bwd(qs[i], kk, vv, *tots[i], do, dk, dv, name=f"attn_bwd_{i}")
            G["b_wq"][j] = mm(x_in[i], dq, "tn", out_dtype=BF16, name=f"proj_q_dw_{i}")
            dx = mm(dq, W["b_wq"][j], "nt", add=dr1, add_scale=DN_ALPHA, name=f"proj_q_dx_{i}")
            if j == 0:
                G["kv_wk"], G["kv_wv"] = mm(x_in[i], dk, "tn", out_dtype=BF16, also=dv, name="proj_kv_dw")
                dx = mm(dk, W["kv_wk"], "nt", add=dx, plus=(dv, W["kv_wv"]), name="proj_kv_dx")
    grad_x = dx[None]
    shard_axis = dict(BIG + SMALL)
    for n in small_names + list(REPL):
        full = list(local[n].shape)
        if n in shard_axis:
            full[shard_axis[n]] *= N_DEV
        G[n] = jnp.stack(G[n]).reshape(full)

    n_small = sum(math.prod(s) for s in small_shapes)
    n_repl = sum(math.prod(s) for s in repl_shapes)
    repl_flat = jnp.concatenate([G[n].reshape(-1) for n in REPL] + [loss_part.reshape(-1)[:1]])
    send_small = _pack_dev([_split(G[n], ax) for n, ax in SMALL] + [jnp.broadcast_to(repl_flat, (N_DEV, n_repl + 1))],
                           F32, SUBLANES)
    ex_groups = []
    for w, names in groups.items():
        lst = []
        for n in names:
            layers = G[n] if isinstance(G[n], list) else [G[n]]
            per = rows_of[n] // len(layers)
            for l, g in enumerate(layers):
                if shard_axis[n] == local[n].ndim - 1:
                    src = split_columns(g, name=f"split_{n}_{l}")
                else:
                    src = g.reshape(N_DEV, per, w)
                lst.append((src, offset[n] + l * per))
        ex_groups.append((sum(rows_of[n] for n in names), w, BF16, lst))
    ex_groups.append((send_small.shape[1], PACK_W, F32, [(send_small, 0)]))
    gots = pair_exchange(ex_groups)
    core = lax.axis_index("c").astype(jnp.int32).reshape(1)
    sum_groups = [(rows, w, dt, [(pair_sum(src, got, off, core, name=f"pair_sum_{gi}_{si}"), off) for si, (src, off) in enumerate(lst)])
                  for gi, ((rows, w, dt, lst), got) in enumerate(zip(ex_groups, gots))]
    recvs = chip_exchange(sum_groups)
    recv = dict(zip(widths, recvs[:-1]))
    recv_small = recvs[-1]

    out = {}
    for n, _ in BIG:
        w = local[n].shape[-1]
        res = adamw(recv[w], offset[n], local[n].reshape(-1, w), mom1[n].reshape(-1, w), mom2[n].reshape(-1, w),
                    name=f"adamw_{n}")
        out[n] = [r.reshape(local[n].shape) for r in res]

    def state(d):
        small = _pack([d[n] for n in small_names] + [d[n] for n in REPL], F32, SUBLANES)
        return jnp.pad(small, ((0, recv_small.shape[1] - small.shape[0]), (0, 0)))

    out_small = adamw(recv_small, 0, state(local), state(mom1), state(mom2), name="adamw_vectors")
    loss = out_small[0].reshape(-1)[n_small + n_repl]
    vecs = [dict(zip(small_names + list(REPL), _unpack(o, small_shapes + repl_shapes))) for o in out_small]
    per_kind = [[out[n][kind] if n in out else vecs[kind][n] for n in WEIGHTS] for kind in range(4)]
    grads, deltas, new_m, new_v = per_kind
    return (loss, grad_x, *grads, *deltas, *new_m, *new_v)
```

```python
import functools
import math

import jax
import jax.numpy as jnp
from jax import lax
from jax.experimental import pallas as pl
from jax.experimental.pallas import tpu as pltpu

F32 = jnp.float32
BF16 = jnp.bfloat16
MESH = pl.DeviceIdType.MESH

N_DEV = 8
DEPTH = 4
N_A = 2
HEAD_DIM = 64
Q_BLOCK = 128
CONV_W = 31
FFN_CONV_W = 3
LN_EPS = 1e-5
DN_ALPHA = (2.0 * DEPTH) ** 0.25
ADAM_LR = 0.001
ADAM_B1 = 0.9
ADAM_B2 = 0.999
ADAM_EPS = 1e-08
ADAM_WD = 0.01
ADAM_STEP = 10

LANES = 128
SUBLANES = 8
PACK_W = 1024
VMEM_LIMIT = 56 * 1024 * 1024

BIG = (("a_pw1_w", 2), ("a_pw2_w", 1), ("b_wq", 1), ("kv_wk", 0), ("kv_wv", 0), ("b_wo", 1),
       ("ffn_w_up", 2), ("ffn_w_gate", 2), ("ffn_w_down", 1), ("ple_w_gate", 1), ("ple_w_proj", 2))
SMALL = (("a_pw1_b", 1), ("a_dw_w", 2), ("a_dw_b", 1), ("a_ln_g", 1), ("a_ln_b", 1), ("a_pw2_b", 1),
         ("ffn_conv_w", 2))
REPL = ("ln_mix_g", "ln_mix_b", "ffn_conv_b", "ln_ffn_g", "ln_ffn_b")
WEIGHTS = ("a_pw1_w", "a_pw1_b", "a_dw_w", "a_dw_b", "a_ln_g", "a_ln_b", "a_pw2_w", "a_pw2_b", "b_wq", "kv_wk",
           "kv_wv", "b_wo", "ln_mix_g", "ln_mix_b", "ffn_w_up", "ffn_w_gate", "ffn_conv_w", "ffn_conv_b",
           "ffn_w_down", "ple_w_gate", "ple_w_proj", "ln_ffn_g", "ln_ffn_b")


def _cp(*sem):
    return pltpu.CompilerParams(dimension_semantics=sem, vmem_limit_bytes=VMEM_LIMIT)


def _pick(dim, target, align=LANES):
    if dim <= target:
        return dim
    t = (target // align) * align
    while t >= align:
        if dim % t == 0:
            return t
        t -= align
    return dim


MM_ROWS = 1024
MM_ROWS_TN = 1536
MM_COLS = 1536
MM_DEPTH = 2816
MM_DEPTH_TN = 1536

_DOT_DIMS = {"nn": (((1,), (0,)), ((), ())), "nt": (((1,), (1,)), ((), ())), "tn": (((0,), (0,)), ((), ()))}


def mm(a, b, mode, *, bias=None, add=None, add_scale=1.0, out_dtype=F32, also=None, plus=None, name):
    if mode == "tn":
        K, M = a.shape
    else:
        M, K = a.shape
    N = b.shape[0] if mode == "nt" else b.shape[1]
    tm = _pick(M, MM_ROWS_TN if mode == "tn" else MM_ROWS if plus is None else MM_ROWS // 2)
    tn = _pick(N, MM_COLS)
    tk = _pick(K, MM_DEPTH_TN if mode == "tn" else MM_DEPTH)
    nk = K // tk
    dims = _DOT_DIMS[mode]
    n_out = 1 if also is None else 2

    def body(*refs):
        refs = list(refs)
        a_ref, b_ref = refs.pop(0), refs.pop(0)
        b2_ref = refs.pop(0) if also is not None else None
        a3_ref, b3_ref = (refs.pop(0), refs.pop(0)) if plus is not None else (None, None)
        bias_ref = refs.pop(0) if bias is not None else None
        add_ref = refs.pop(0) if add is not None else None
        o_refs, acc_refs = refs[:n_out], refs[n_out:]
        k = pl.program_id(2)

        @pl.when(k == 0)
        def _():
            for acc_ref in acc_refs:
                acc_ref[...] = jnp.zeros_like(acc_ref)

        a_v = a_ref[...].astype(BF16)
        acc_refs[0][...] += lax.dot_general(a_v, b_ref[...].astype(BF16), dims, preferred_element_type=F32)
        if b2_ref is not None:
            acc_refs[1][...] += lax.dot_general(a_v, b2_ref[...].astype(BF16), dims, preferred_element_type=F32)
        if a3_ref is not None:
            acc_refs[0][...] += lax.dot_general(a3_ref[...].astype(BF16), b3_ref[...].astype(BF16), dims,
                                                preferred_element_type=F32)

        @pl.when(k == nk - 1)
        def _():
            for o_ref, acc_ref in zip(o_refs, acc_refs):
                r = acc_ref[...]
                if bias_ref is not None:
                    r = r + bias_ref[...]
                if add_ref is not None:
                    r = r + add_scale * add_ref[...].astype(F32)
                o_ref[...] = r.astype(o_ref.dtype)

    a_spec = pl.BlockSpec((tk, tm), lambda j, i, k: (k, i)) if mode == "tn" else pl.BlockSpec((tm, tk), lambda j, i, k: (i, k))
    b_spec = pl.BlockSpec((tn, tk), lambda j, i, k: (j, k)) if mode == "nt" else pl.BlockSpec((tk, tn), lambda j, i, k: (k, j))
    in_specs, args = [a_spec, b_spec], [a, b]
    if also is not None:
        in_specs.append(b_spec)
        args.append(also)
    if plus is not None:
        in_specs += [a_spec, b_spec]
        args += list(plus)
    if bias is not None:
        in_specs.append(pl.BlockSpec((1, tn), lambda j, i, k: (0, j)))
        args.append(bias)
    if add is not None:
        in_specs.append(pl.BlockSpec((tm, tn), lambda j, i, k: (i, j)))
        args.append(add)
    o_spec = pl.BlockSpec((tm, tn), lambda j, i, k: (i, j))
    outs = pl.pallas_call(
        body, grid=(N // tn, M // tm, nk), in_specs=in_specs,
        out_specs=[o_spec] * n_out, out_shape=[jax.ShapeDtypeStruct((M, N), out_dtype)] * n_out,
        scratch_shapes=[pltpu.VMEM((tm, tn), F32)] * n_out,
        compiler_params=_cp("parallel", "parallel", "arbitrary"), name=name)(*args)
    return outs[0] if also is None else tuple(outs)


ROW_BLOCK = 512


def _rows(body, *, n_rows, tm, row_ins, full_ins=(), row_outs=(), acc_outs=(), scratch=(), reverse=False, name):
    n = n_rows // tm

    def rmap(i):
        return (n - 1 - i, 0) if reverse else (i, 0)

    in_specs = [pl.BlockSpec((tm, a.shape[1]), rmap) for a in row_ins]
    in_specs += [pl.BlockSpec(a.shape, lambda i, nd=a.ndim: (0,) * nd) for a in full_ins]
    out_shape = [jax.ShapeDtypeStruct((n_rows, w), dt) for (w, dt) in row_outs]
    out_shape += [jax.ShapeDtypeStruct(s, dt) for (s, dt) in acc_outs]
    out_specs = [pl.BlockSpec((tm, w), rmap) for (w, dt) in row_outs]
    out_specs += [pl.BlockSpec(s, lambda i, nd=len(s): (0,) * nd) for (s, dt) in acc_outs]
    return pl.pallas_call(
        functools.partial(body, n), grid=(n,), in_specs=in_specs, out_specs=out_specs, out_shape=out_shape,
        scratch_shapes=list(scratch), compiler_params=_cp("arbitrary"), name=name)(*row_ins, *full_ins)


def _sigmoid(x):
    return 1.0 / (1.0 + jnp.exp(-x))


def _ln_hat(r):
    mu = jnp.mean(r, axis=-1, keepdims=True)
    xc = r - mu
    var = jnp.mean(xc * xc, axis=-1, keepdims=True)
    rstd = lax.rsqrt(var + LN_EPS)
    return xc * rstd, rstd


def _ln_back(xhat, rstd, g, dy):
    dxh = dy * g
    m1 = jnp.mean(dxh, axis=-1, keepdims=True)
    m2 = jnp.mean(dxh * xhat, axis=-1, keepdims=True)
    return rstd * (dxh - m1 - xhat * m2)


def _colsum(x):
    return jnp.sum(x, axis=0, keepdims=True)


def _acc(i, ref, val):
    @pl.when(i == 0)
    def _():
        ref[...] = val

    @pl.when(i > 0)
    def _():
        ref[...] += val


def res_ln(x, mix, g, b, *, ple=None, name):
    S, D = x.shape

    def body(n, *refs):
        if ple is None:
            x_ref, m_ref, g_ref, b_ref, r_ref, y_ref = refs
            r = DN_ALPHA * x_ref[...] + m_ref[...]
        else:
            x_ref, m_ref, pgl_ref, pp_ref, g_ref, b_ref, r_ref, y_ref = refs
            r = DN_ALPHA * x_ref[...] + m_ref[...] + _sigmoid(pgl_ref[...].astype(F32)) * pp_ref[...].astype(F32)
        xhat, _ = _ln_hat(r)
        r_ref[...] = r
        y_ref[...] = xhat * g_ref[...] + b_ref[...]

    row_ins = [x, mix] + ([] if ple is None else list(ple))
    return _rows(body, n_rows=S, tm=_pick(S, ROW_BLOCK, SUBLANES), row_ins=row_ins, full_ins=[g, b],
                 row_outs=[(D, F32), (D, F32)], name=name)


def ln_bwd(r, g, dy, *, name):
    S, D = r.shape

    def body(n, r_ref, dy_ref, g_ref, dr_ref, dg_ref, db_ref, ds_ref):
        i = pl.program_id(0)
        xhat, rstd = _ln_hat(r_ref[...])
        dy_v = dy_ref[...]
        dr = _ln_back(xhat, rstd, g_ref[...], dy_v)
        dr_ref[...] = dr
        _acc(i, dg_ref, _colsum(dy_v * xhat))
        _acc(i, db_ref, _colsum(dy_v))
        _acc(i, ds_ref, _colsum(dr))

    return _rows(body, n_rows=S, tm=_pick(S, ROW_BLOCK, SUBLANES), row_ins=[r, dy], full_ins=[g],
                 row_outs=[(D, F32)], acc_outs=[((1, D), F32)] * 3, name=name)


def glu_fwd(h1, *, name):
    S, D2 = h1.shape
    D = D2 // 2

    def body(n, h_ref, o_ref):
        o_ref[...] = h_ref[:, :D].astype(F32) * _sigmoid(h_ref[:, D:].astype(F32))

    return _rows(body, n_rows=S, tm=_pick(S, ROW_BLOCK, SUBLANES), row_ins=[h1], row_outs=[(D, F32)], name=name)[0]


def glu_bwd(h1, dh2, *, name):
    S, D2 = h1.shape
    D = D2 // 2

    def body(n, h_ref, d_ref, o_ref, s_ref):
        i = pl.program_id(0)
        a, sg, d = h_ref[:, :D].astype(F32), _sigmoid(h_ref[:, D:].astype(F32)), d_ref[...]
        da = d * sg
        dg = d * a * sg * (1.0 - sg)
        o_ref[:, :D] = da.astype(o_ref.dtype)
        o_ref[:, D:] = dg.astype(o_ref.dtype)
        _acc(i, s_ref, jnp.concatenate([_colsum(da), _colsum(dg)], axis=1))

    return _rows(body, n_rows=S, tm=_pick(S, ROW_BLOCK, SUBLANES), row_ins=[h1, dh2], row_outs=[(D2, BF16)],
                 acc_outs=[((1, D2), F32)], name=name)


CONV_ROWS = 32
CONV_LANES = 256


def _halo(k):
    return -(-(k - 1) // SUBLANES) * SUBLANES


def _phases(offs):
    return sorted({o % SUBLANES for o in offs} - {0})


def _shift_scratch(offs, n_rows, width):
    return pltpu.VMEM((max(len(_phases(offs)), 1), n_rows, width), F32)


def _make_shifted(buf_ref, sh_ref, offs):
    n = buf_ref.shape[0] - SUBLANES
    for p, b in enumerate(_phases(offs)):
        sh_ref[p, pl.ds(0, n), :] = buf_ref[pl.ds(b, n), :]


def _tap(buf_ref, sh_ref, offs, k, rc, rows, lc, lw):
    b = offs[k] % SUBLANES
    src = buf_ref if b == 0 else sh_ref.at[_phases(offs).index(b)]
    return src[pl.ds(offs[k] - b + rc, rows), pl.ds(lc, lw)]


def _conv_taps(buf_ref, sh_ref, w_ref, offs, tm, width, emit):
    _make_shifted(buf_ref, sh_ref, offs)
    rows = min(CONV_ROWS, tm)
    for lc in range(0, width, CONV_LANES):
        lw = min(CONV_LANES, width - lc)
        for rc in range(0, tm, rows):
            acc = None
            for k in range(len(offs)):
                t = _tap(buf_ref, sh_ref, offs, k, rc, rows, lc, lw) * w_ref[pl.ds(k, 1), pl.ds(lc, lw)]
                acc = t if acc is None else acc + t
            emit(rc, lc, lw, rows, acc)


def _fill_causal(i, buf_ref, x_ref, halo, tm):
    @pl.when(i == 0)
    def _():
        buf_ref[pl.ds(0, halo), :] = jnp.zeros((halo, buf_ref.shape[1]), F32)

    @pl.when(i > 0)
    def _():
        buf_ref[pl.ds(0, halo), :] = buf_ref[pl.ds(tm, halo), :]

    buf_ref[pl.ds(halo, tm), :] = x_ref[...].astype(F32)


def conv_ln_silu_fwd(x, w, b, g, beta, *, name):
    S, C = x.shape
    K = w.shape[0]
    halo = _halo(K)
    tm = _pick(S, 256, SUBLANES)
    offs = [halo - (K - 1) + k for k in range(K)]

    def body(n, x_ref, w_ref, b_ref, g_ref, beta_ref, h3_ref, h5_ref, buf_ref, sh_ref):
        i = pl.program_id(0)
        _fill_causal(i, buf_ref, x_ref, halo, tm)

        def emit(rc, lc, lw, rows, acc):
            h3_ref[pl.ds(rc, rows), pl.ds(lc, lw)] = acc + b_ref[:, pl.ds(lc, lw)]

        _conv_taps(buf_ref, sh_ref, w_ref, offs, tm, C, emit)
        xhat, _ = _ln_hat(h3_ref[...])
        h4 = xhat * g_ref[...] + beta_ref[...]
        h5_ref[...] = (h4 * _sigmoid(h4)).astype(h5_ref.dtype)

    return _rows(body, n_rows=S, tm=tm, row_ins=[x], full_ins=[w, b, g, beta], row_outs=[(C, F32), (C, BF16)],
                 scratch=[pltpu.VMEM((tm + halo, C), F32), _shift_scratch(offs, tm + halo, C)], name=name)


def conv_act_fwd(gp, u, w, b, *, name):
    S, C = gp.shape
    K = w.shape[0]
    halo = _halo(K)
    tm = _pick(S, 256, SUBLANES)
    offs = [halo - (K - 1) + k for k in range(K)]

    def body(n, x_ref, u_ref, w_ref, b_ref, g_ref, hh_ref, buf_ref, sh_ref):
        i = pl.program_id(0)
        _fill_causal(i, buf_ref, x_ref, halo, tm)

        def emit(rc, lc, lw, rows, acc):
            gv = acc + b_ref[:, pl.ds(lc, lw)]
            g_ref[pl.ds(rc, rows), pl.ds(lc, lw)] = gv.astype(g_ref.dtype)
            hh_ref[pl.ds(rc, rows), pl.ds(lc, lw)] = (gv * _sigmoid(gv) * u_ref[pl.ds(rc, rows), pl.ds(lc, lw)].astype(F32)).astype(hh_ref.dtype)

        _conv_taps(buf_ref, sh_ref, w_ref, offs, tm, C, emit)

    return _rows(body, n_rows=S, tm=tm, row_ins=[gp, u], full_ins=[w, b], row_outs=[(C, BF16), (C, BF16)],
                 scratch=[pltpu.VMEM((tm + halo, C), F32), _shift_scratch(offs, tm + halo, C)], name=name)


def conv_bwd_x(dy, w, *, out_dtype, name):
    S, C = dy.shape
    K = w.shape[0]
    halo = _halo(K)
    tm = _pick(S, 256, SUBLANES)
    offs = [K - 1 - k for k in range(K)]

    def body(n, dy_ref, w_ref, dx_ref, buf_ref, sh_ref):
        i = pl.program_id(0)

        @pl.when(i == 0)
        def _():
            buf_ref[pl.ds(tm, halo), :] = jnp.zeros((halo, C), F32)

        @pl.when(i > 0)
        def _():
            buf_ref[pl.ds(tm, halo), :] = buf_ref[pl.ds(0, halo), :]

        buf_ref[pl.ds(0, tm), :] = dy_ref[...].astype(F32)

        def emit(rc, lc, lw, rows, acc):
            dx_ref[pl.ds(rc, rows), pl.ds(lc, lw)] = acc.astype(dx_ref.dtype)

        _conv_taps(buf_ref, sh_ref, w_ref, offs, tm, C, emit)

    return _rows(body, n_rows=S, tm=tm, row_ins=[dy], full_ins=[w], row_outs=[(C, out_dtype)],
                 scratch=[pltpu.VMEM((tm + halo, C), F32), _shift_scratch(offs, tm + halo, C)], reverse=True, name=name)[0]


def conv_bwd_w(x, dy, K, *, name):
    S, C = x.shape
    halo = _halo(K)
    tm = _pick(S, 256, SUBLANES)
    offs = [halo - (K - 1) + k for k in range(K)]
    rows = min(CONV_ROWS, tm)

    def body(n, x_ref, dy_ref, dw_ref, db_ref, buf_ref, acc_ref, sh_ref):
        i = pl.program_id(0)
        _fill_causal(i, buf_ref, x_ref, halo, tm)
        _make_shifted(buf_ref, sh_ref, offs)

        @pl.when(i == 0)
        def _():
            acc_ref[...] = jnp.zeros_like(acc_ref)

        for lc in range(0, C, CONV_LANES):
            lw = min(CONV_LANES, C - lc)
            for k in range(K):
                s = None
                for rc in range(0, tm, rows):
                    t = dy_ref[pl.ds(rc, rows), pl.ds(lc, lw)].astype(F32) * _tap(buf_ref, sh_ref, offs, k, rc, rows, lc, lw)
                    s = t if s is None else s + t
                s8 = s[0:SUBLANES]
                for q in range(1, rows // SUBLANES):
                    s8 = s8 + s[q * SUBLANES:(q + 1) * SUBLANES]
                acc_ref[pl.ds(k * SUBLANES, SUBLANES), pl.ds(lc, lw)] += s8
        _acc(i, db_ref, _colsum(dy_ref[...].astype(F32)))

        @pl.when(i == n - 1)
        def _():
            for k in range(K):
                dw_ref[pl.ds(k, 1), :] = _colsum(acc_ref[pl.ds(k * SUBLANES, SUBLANES), :])

    return _rows(body, n_rows=S, tm=tm, row_ins=[x, dy], acc_outs=[((K, C), F32), ((1, C), F32)],
                 scratch=[pltpu.VMEM((tm + halo, C), F32), pltpu.VMEM((K * SUBLANES, C), F32),
                          _shift_scratch(offs, tm + halo, C)], name=name)


def ln_silu_bwd(h3, g, beta, dh5, *, name):
    S, C = h3.shape

    def body(n, h_ref, d_ref, g_ref, beta_ref, dh_ref, dg_ref, db_ref):
        i = pl.program_id(0)
        xhat, rstd = _ln_hat(h_ref[...])
        h4 = xhat * g_ref[...] + beta_ref[...]
        sg = _sigmoid(h4)
        dh4 = d_ref[...] * sg * (1.0 + h4 * (1.0 - sg))
        dh_ref[...] = _ln_back(xhat, rstd, g_ref[...], dh4)
        _acc(i, dg_ref, _colsum(dh4 * xhat))
        _acc(i, db_ref, _colsum(dh4))

    return _rows(body, n_rows=S, tm=_pick(S, ROW_BLOCK, SUBLANES), row_ins=[h3, dh5], full_ins=[g, beta],
                 row_outs=[(C, F32)], acc_outs=[((1, C), F32)] * 2, name=name)


def ffn_gate_bwd(dhh, u, g, gp, w, *, name):
    S, C = u.shape
    K = w.shape[0]
    halo = _halo(K)
    tm = _pick(S, 256, SUBLANES)
    offs = [K - 1 - k for k in range(K)]
    rows = min(CONV_ROWS, tm)

    def body(n, d_ref, u_ref, g_ref, gp_ref, w_ref, du_ref, dgp_ref, dw_ref, db_ref, buf_ref, sh_ref, acc_ref):
        i = pl.program_id(0)

        @pl.when(i == 0)
        def _():
            buf_ref[pl.ds(tm, halo), :] = jnp.zeros((halo, C), F32)
            acc_ref[...] = jnp.zeros_like(acc_ref)

        @pl.when(i > 0)
        def _():
            buf_ref[pl.ds(tm, halo), :] = buf_ref[pl.ds(0, halo), :]

        d, gv = d_ref[...].astype(F32), g_ref[...].astype(F32)
        sg = _sigmoid(gv)
        du_ref[...] = (d * gv * sg).astype(du_ref.dtype)
        dg = d * u_ref[...].astype(F32) * sg * (1.0 + gv * (1.0 - sg))
        buf_ref[pl.ds(0, tm), :] = dg
        _acc(i, db_ref, _colsum(dg))

        def emit(rc, lc, lw, nrows, acc):
            dgp_ref[pl.ds(rc, nrows), pl.ds(lc, lw)] = acc.astype(dgp_ref.dtype)

        _conv_taps(buf_ref, sh_ref, w_ref, offs, tm, C, emit)
        for lc in range(0, C, CONV_LANES):
            lw = min(CONV_LANES, C - lc)
            for k in range(K):
                s_ = None
                for rc in range(0, tm, rows):
                    t = gp_ref[pl.ds(rc, rows), pl.ds(lc, lw)].astype(F32) * _tap(buf_ref, sh_ref, offs, k, rc, rows, lc, lw)
                    s_ = t if s_ is None else s_ + t
                s8 = s_[0:SUBLANES]
                for q in range(1, rows // SUBLANES):
                    s8 = s8 + s_[q * SUBLANES:(q + 1) * SUBLANES]
                acc_ref[pl.ds(k * SUBLANES, SUBLANES), pl.ds(lc, lw)] += s8

        @pl.when(i == n - 1)
        def _():
            for k in range(K):
                dw_ref[pl.ds(k, 1), :] = _colsum(acc_ref[pl.ds(k * SUBLANES, SUBLANES), :])

    return _rows(body, n_rows=S, tm=tm, row_ins=[dhh, u, g, gp], full_ins=[w], row_outs=[(C, BF16), (C, BF16)],
                 acc_outs=[((K, C), F32), ((1, C), F32)],
                 scratch=[pltpu.VMEM((tm + halo, C), F32), _shift_scratch(offs, tm + halo, C),
                          pltpu.VMEM((K * SUBLANES, C), F32)], reverse=True, name=name)


def ln_ple_bwd(r, g, dy, pgl, pp, *, name):
    S, D = r.shape

    def body(n, r_ref, dy_ref, l_ref, p_ref, g_ref, dr_ref, dpp_ref, dpl_ref, dg_ref, db_ref):
        i = pl.program_id(0)
        xhat, rstd = _ln_hat(r_ref[...])
        dy_v = dy_ref[...]
        dr = _ln_back(xhat, rstd, g_ref[...], dy_v)
        dr_ref[...] = dr
        sg = _sigmoid(l_ref[...].astype(F32))
        dpp_ref[...] = (dr * sg).astype(dpp_ref.dtype)
        dpl_ref[...] = (dr * p_ref[...].astype(F32) * sg * (1.0 - sg)).astype(dpl_ref.dtype)
        _acc(i, dg_ref, _colsum(dy_v * xhat))
        _acc(i, db_ref, _colsum(dy_v))

    return _rows(body, n_rows=S, tm=_pick(S, ROW_BLOCK, SUBLANES), row_ins=[r, dy, pgl, pp], full_ins=[g],
                 row_outs=[(D, F32), (D, BF16), (D, BF16)], acc_outs=[((1, D), F32)] * 2, name=name)


def loss_grad(y, target, *, name):
    S, D = y.shape

    def body(n, y_ref, t_ref, dy_ref, l_ref):
        i = pl.program_id(0)
        e = y_ref[...] - t_ref[...]
        dy_ref[...] = e * (1.0 / D)
        s = jnp.sum(_colsum(e * e), axis=1, keepdims=True) * (0.5 / D)
        _acc(i, l_ref, jnp.broadcast_to(s, (1, LANES)))

    return _rows(body, n_rows=S, tm=_pick(S, ROW_BLOCK, SUBLANES), row_ins=[y, target], row_outs=[(D, F32)],
                 acc_outs=[((1, LANES), F32)], name=name)


def _key_step(S):
    return min(512, S // 2)


EXIT_LOG = -110.0


def _attn_consts():
    lane = lax.broadcasted_iota(jnp.int32, (1, LANES), 1)
    heads = (lane < HEAD_DIM, lane >= HEAD_DIM)
    row = lax.broadcasted_iota(jnp.int32, (Q_BLOCK, Q_BLOCK), 0)
    col = lax.broadcasted_iota(jnp.int32, (Q_BLOCK, Q_BLOCK), 1)
    causal = jnp.concatenate([col < row] * 2, axis=0)
    return heads, row, col, causal


def _tri(cond):
    return jnp.where(cond, 1.0, 0.0).astype(BF16)


def _keysum2(x, tri):
    hi = x.astype(BF16)
    lo = (x - hi.astype(F32)).astype(BF16)
    return jnp.dot(jnp.concatenate([hi, lo], axis=1), jnp.concatenate([tri, tri], axis=0),
                   preferred_element_type=F32)


def _stack_heads(x, heads):
    return jnp.concatenate([jnp.where(m, x, jnp.zeros_like(x)) for m in heads], axis=0)


def _log1m_beta(z):
    return -(jnp.maximum(z, 0.0) + jnp.log(1.0 + jnp.exp(-jnp.abs(z))))


def attn_fwd(q, k, v, *, name):
    S, D = q.shape
    nb = S // Q_BLOCK
    tk = _key_step(S)
    nkb = tk // Q_BLOCK
    scale = 1.0 / math.sqrt(HEAD_DIM)

    def body(q_ref, k_ref, v_ref, o_ref, tot_ref, seen_ref, vm_ref):
        heads, row, col, causal = _attn_consts()
        above = _tri(row > col)
        for h in range(2):
            vm_ref[h] = jnp.where(heads[h], v_ref[...], jnp.zeros_like(v_ref[...]))

        def step(sb, carry, qq, nblk, diag):
            acc, cl = carry
            c0 = pl.multiple_of(sb * tk, tk)
            z = lax.dot_general(qq, k_ref[pl.ds(c0, nblk * Q_BLOCK), :], _DOT_DIMS["nt"], preferred_element_type=F32)
            zl, es, rs = [], [], []
            for jb in range(nblk):
                zb = z[:, jb * Q_BLOCK:(jb + 1) * Q_BLOCK]
                lr = _log1m_beta(zb)
                l = jnp.where(causal, lr, 0.0) if diag and jb == nblk - 1 else lr
                zl.append(zb + lr)
                es.append(_keysum2(l, above))
                rs.append(jnp.sum(l, axis=1, keepdims=True))
            a = [None] * nblk
            for jb in reversed(range(nblk)):
                ab = jnp.exp(zl[jb] + es[jb] + cl)
                if diag and jb == nblk - 1:
                    ab = jnp.where(causal, ab, 0.0)
                a[jb] = ab.astype(BF16)
                cl = cl + rs[jb]
            a = jnp.concatenate(a, axis=1)
            for h in range(2):
                acc = acc + jnp.dot(a[h * Q_BLOCK:(h + 1) * Q_BLOCK], vm_ref[h, pl.ds(c0, nblk * Q_BLOCK), :],
                                    preferred_element_type=F32)
            return acc, cl

        def qblock(i, _):
            r0 = pl.multiple_of(i * Q_BLOCK, Q_BLOCK)
            qq = _stack_heads(q_ref[pl.ds(r0, Q_BLOCK), :] * scale, heads)
            last = i // nkb
            carry = (jnp.zeros((Q_BLOCK, LANES), F32), jnp.zeros((2 * Q_BLOCK, 1), F32))
            carry = lax.switch(i % nkb, [functools.partial(step, last, qq=qq, nblk=m + 1, diag=True) for m in range(nkb)],
                               carry)

            def more(c):
                return jnp.logical_and(c[0] < last, jnp.max(c[2]) >= EXIT_LOG)

            def left(c):
                return (c[0] + 1, *step(last - 1 - c[0], c[1:], qq, nkb, False))

            seen, acc, cl = lax.while_loop(more, left, (jnp.int32(0), *carry))
            o_ref[pl.ds(r0, Q_BLOCK), :] = acc.astype(o_ref.dtype)
            tot_ref[pl.ds(r0, Q_BLOCK), :] = jnp.where(heads[0], cl[:Q_BLOCK], cl[Q_BLOCK:])
            seen_ref[pl.ds(pl.multiple_of(i * SUBLANES, SUBLANES), SUBLANES), :] = jnp.full((SUBLANES, LANES), seen, F32)
            return 0

        lax.fori_loop(0, nb, qblock, 0)

    spec = pl.BlockSpec((S, LANES), lambda h: (0, h))
    seen_spec = pl.BlockSpec((nb * SUBLANES, LANES), lambda h: (0, h))
    return pl.pallas_call(body, grid=(D // LANES,), in_specs=[spec] * 3, out_specs=[spec, spec, seen_spec],
                          out_shape=[jax.ShapeDtypeStruct((S, D), BF16), jax.ShapeDtypeStruct((S, D), F32),
                                     jax.ShapeDtypeStruct((nb * SUBLANES, D), F32)],
                          scratch_shapes=[pltpu.VMEM((2, S, LANES), BF16)], compiler_params=_cp("parallel"),
                          name=name)(q, k, v)


def attn_bwd(q, k, v, tot, seen, do, dk0, dv0, *, name):
    S, D = q.shape
    nb = S // Q_BLOCK
    tk = _key_step(S)
    nkb = tk // Q_BLOCK
    scale = 1.0 / math.sqrt(HEAD_DIM)
    has_init = dk0 is not None

    def body(*refs):
        if has_init:
            q_ref, k_ref, v_ref, tot_ref, seen_ref, do_ref, dk0_ref, dv0_ref, dq_ref, dk_ref, dv_ref, km_ref = refs
            dk_ref[...] = dk0_ref[...]
            dv_ref[...] = dv0_ref[...]
        else:
            q_ref, k_ref, v_ref, tot_ref, seen_ref, do_ref, dq_ref, dk_ref, dv_ref, km_ref = refs
            dk_ref[...] = jnp.zeros_like(dk_ref)
            dv_ref[...] = jnp.zeros_like(dv_ref)
        heads, row, col, causal = _attn_consts()
        upto = _tri(row <= col)
        before = _tri(row < col)
        for h in range(2):
            km_ref[h] = jnp.where(heads[h], k_ref[...], jnp.zeros_like(k_ref[...]))

        def step(sb, carry, qq, dd, totl, nblk, diag):
            dq, pl_, pg = carry
            c0 = pl.multiple_of(sb * tk, tk)
            keys = pl.ds(c0, nblk * Q_BLOCK)
            z = lax.dot_general(qq, k_ref[keys, :], _DOT_DIMS["nt"], preferred_element_type=F32)
            da = lax.dot_general(dd, v_ref[keys, :], _DOT_DIMS["nt"], preferred_element_type=F32)
            blocks = range(nblk)
            masked = [diag and jb == nblk - 1 for jb in blocks]
            zb = [z[:, jb * Q_BLOCK:(jb + 1) * Q_BLOCK] for jb in blocks]
            lr = [_log1m_beta(zb[jb]) for jb in blocks]
            l = [jnp.where(causal, lr[jb], 0.0) if masked[jb] else lr[jb] for jb in blocks]
            lsum = [_keysum2(l[jb], upto) for jb in blocks]
            lrow = [jnp.sum(l[jb], axis=1, keepdims=True) for jb in blocks]
            a, g = [None] * nblk, [None] * nblk
            for jb in blocks:
                ab = jnp.exp(zb[jb] + lr[jb] + (totl - pl_ - lsum[jb]))
                if masked[jb]:
                    ab = jnp.where(causal, ab, 0.0)
                g[jb] = ab * da[:, jb * Q_BLOCK:(jb + 1) * Q_BLOCK]
                a[jb] = ab.astype(BF16)
                pl_ = pl_ + lrow[jb]
            gsum = [jnp.dot(g[jb].astype(BF16), before, preferred_element_type=F32) for jb in blocks]
            grow = [jnp.sum(g[jb], axis=1, keepdims=True) for jb in blocks]
            dz = [None] * nblk
            for jb in blocks:
                dzb = g[jb] * jnp.exp(lr[jb]) - jnp.exp(zb[jb] + lr[jb]) * (pg + gsum[jb])
                if masked[jb]:
                    dzb = jnp.where(causal, dzb, 0.0)
                dz[jb] = dzb.astype(BF16)
                pg = pg + grow[jb]
            a = jnp.concatenate(a, axis=1)
            dz = jnp.concatenate(dz, axis=1)
            for h in range(2):
                dq = dq + jnp.dot(dz[h * Q_BLOCK:(h + 1) * Q_BLOCK], km_ref[h, keys, :], preferred_element_type=F32)
            dk_ref[keys, :] += lax.dot_general(dz, qq, _DOT_DIMS["tn"], preferred_element_type=F32)
            dv_ref[keys, :] += lax.dot_general(a, dd, _DOT_DIMS["tn"], preferred_element_type=F32)
            return dq, pl_, pg

        def qblock(i, _):
            r0 = pl.multiple_of(i * Q_BLOCK, Q_BLOCK)
            qq = _stack_heads(q_ref[pl.ds(r0, Q_BLOCK), :] * scale, heads)
            dd = _stack_heads(do_ref[pl.ds(r0, Q_BLOCK), :].astype(BF16), heads)
            tot2 = tot_ref[pl.ds(r0, Q_BLOCK), :]
            totl = jnp.concatenate([tot2[:, 0:1], tot2[:, HEAD_DIM:HEAD_DIM + 1]], axis=0)
            last = i // nkb
            zc = jnp.zeros((2 * Q_BLOCK, 1), F32)
            carry = (jnp.zeros((Q_BLOCK, LANES), F32), zc, zc)
            walked = jnp.max(seen_ref[pl.ds(pl.multiple_of(i * SUBLANES, SUBLANES), SUBLANES), :]).astype(jnp.int32)
            first = last - jnp.clip(walked, 0, last)
            carry = lax.fori_loop(first, last, lambda sb, c: step(sb, c, qq, dd, totl, nkb, False), carry)
            carry = lax.switch(i % nkb, [functools.partial(step, last, qq=qq, dd=dd, totl=totl, nblk=m + 1, diag=True)
                                         for m in range(nkb)], carry)
            dq_ref[pl.ds(r0, Q_BLOCK), :] = (carry[0] * scale).astype(dq_ref.dtype)
            return 0

        lax.fori_loop(0, nb, qblock, 0)

    spec = pl.BlockSpec((S, LANES), lambda h: (0, h))
    seen_spec = pl.BlockSpec((nb * SUBLANES, LANES), lambda h: (0, h))
    args = [q, k, v, tot, seen, do] + ([dk0, dv0] if has_init else [])
    return pl.pallas_call(
        body, grid=(D // LANES,), in_specs=[spec] * 4 + [seen_spec] + [spec] * (len(args) - 5), out_specs=[spec] * 3,
        out_shape=[jax.ShapeDtypeStruct((S, D), BF16), jax.ShapeDtypeStruct((S, D), F32), jax.ShapeDtypeStruct((S, D), F32)],
        scratch_shapes=[pltpu.VMEM((2, S, LANES), BF16)], compiler_params=_cp("parallel"), name=name)(*args)


def _dev_index(px, py, pc):
    return 4 * px + 2 * py + pc


def all_gather(bufs):
    nb = len(bufs)

    def body(*refs):
        ins, outs = refs[:nb], refs[nb:2 * nb]
        send_sems, recv_sems, local_sems = refs[2 * nb:]
        x, y, c = lax.axis_index("x"), lax.axis_index("y"), lax.axis_index("c")
        me, sibling = (x, y, c), (x, y, 1 - c)
        chips = [(1 - x, y), (x, 1 - y), (1 - x, 1 - y)]

        def copy(b, k, block, to, from_input=False):
            slot = outs[b].at[_dev_index(*block)]
            return pltpu.make_async_remote_copy(
                src_ref=ins[b] if from_input else slot, dst_ref=slot,
                send_sem=send_sems.at[7 * b + k], recv_sem=recv_sems.at[7 * b + k], device_id=to, device_id_type=MESH)

        mine = [pltpu.make_async_copy(ins[b], outs[b].at[_dev_index(*me)], local_sems.at[b]) for b in range(nb)]
        for cp in mine:
            cp.start()
        first = []
        for b in range(nb):
            first.append(copy(b, 0, me, sibling, from_input=True))
            first += [copy(b, 1 + j, me, (*chip, c), from_input=True) for j, chip in enumerate(chips)]
        for cp in first:
            cp.start()
        passed = []
        for j, chip in enumerate(chips):
            for b in range(nb):
                copy(b, 1 + j, (*chip, c), me).wait_recv()
                fwd = copy(b, 4 + j, (*chip, c), sibling)
                fwd.start()
                passed.append(fwd)
        for b in range(nb):
            copy(b, 0, sibling, me).wait_recv()
            for j, chip in enumerate(chips):
                copy(b, 4 + j, (*chip, 1 - c), me).wait_recv()
        for cp in first + passed:
            cp.wait_send()
        for cp in mine:
            cp.wait()

    any_spec = pl.BlockSpec(memory_space=pl.ANY)
    return pl.pallas_call(
        body, in_specs=[any_spec] * nb, out_specs=[any_spec] * nb,
        out_shape=[jax.ShapeDtypeStruct((N_DEV,) + b.shape, b.dtype) for b in bufs],
        scratch_shapes=[pltpu.SemaphoreType.DMA((7 * nb,)), pltpu.SemaphoreType.DMA((7 * nb,)),
                        pltpu.SemaphoreType.DMA((nb,))],
        name="all_gather_weights")(*bufs)


def _sources(groups):
    return [s for g in groups for (s, _) in g[3]]


def _layout(groups, refs):
    out, si = [], 0
    for g, (_, _, _, lst) in enumerate(groups):
        for (s, off) in lst:
            out.append((g, refs[si], off, s.shape[-2]))
            si += 1
    return out


def pair_exchange(groups):
    srcs = _sources(groups)
    ns, ng = len(srcs), len(groups)

    def body(*refs):
        outs = refs[ns:ns + ng]
        send_sems, recv_sems = refs[ns + ng:]
        x, y, c = lax.axis_index("x"), lax.axis_index("y"), lax.axis_index("c")
        sibling = (x, y, 1 - c)
        for (g, ref, off, r) in _layout(groups, refs[:ns]):
            for q in range(N_DEV // 2):
                pltpu.make_async_remote_copy(
                    src_ref=ref.at[2 * q + 1 - c], dst_ref=outs[g].at[q, pl.ds(off, r)], send_sem=send_sems.at[g],
                    recv_sem=recv_sems.at[g], device_id=sibling, device_id_type=MESH).start()
        whole = [pltpu.make_async_remote_copy(
            src_ref=outs[g], dst_ref=outs[g], send_sem=send_sems.at[g], recv_sem=recv_sems.at[g],
            device_id=sibling, device_id_type=MESH) for g in range(ng)]
        for w in whole:
            w.wait_recv()
        for w in whole:
            w.wait_send()

    any_spec = pl.BlockSpec(memory_space=pl.ANY)
    return pl.pallas_call(
        body, in_specs=[any_spec] * ns, out_specs=[any_spec] * ng,
        out_shape=[jax.ShapeDtypeStruct((N_DEV // 2, r, w), dt) for (r, w, dt, _) in groups],
        scratch_shapes=[pltpu.SemaphoreType.DMA((ng,)), pltpu.SemaphoreType.DMA((ng,))],
        name="pair_exchange")(*srcs)


def pair_sum(src, got, off, core, *, name):
    _, r, W = src.shape
    tr = _row_tile(r, off, 1024)
    o = off // tr

    def body(c_ref, s_ref, g_ref, o_ref):
        o_ref[...] = (s_ref[...].astype(F32) + g_ref[...].astype(F32)).astype(o_ref.dtype)

    return pl.pallas_call(
        body,
        grid_spec=pltpu.PrefetchScalarGridSpec(
            num_scalar_prefetch=1, grid=(N_DEV // 2, r // tr),
            in_specs=[pl.BlockSpec((None, None, tr, W), lambda q, i, c: (q, c[0], i, 0)),
                      pl.BlockSpec((None, tr, W), lambda q, i, c: (q, i + o, 0))],
            out_specs=pl.BlockSpec((None, tr, W), lambda q, i, c: (q, i, 0))),
        out_shape=jax.ShapeDtypeStruct((N_DEV // 2, r, W), src.dtype), compiler_params=_cp("parallel", "parallel"),
        name=name)(core, src.reshape(N_DEV // 2, 2, r, W), got)


def chip_exchange(groups):
    srcs = _sources(groups)
    ns, ng = len(srcs), len(groups)

    def body(*refs):
        outs = refs[ns:ns + ng]
        send_sems, recv_sems, local_sems = refs[ns + ng:]
        x, y, c = lax.axis_index("x"), lax.axis_index("y"), lax.axis_index("c")
        me = 2 * x + y
        layout = _layout(groups, refs[:ns])
        mine = [pltpu.make_async_copy(ref.at[me], outs[g].at[me, pl.ds(off, r)], local_sems.at[i])
                for i, (g, ref, off, r) in enumerate(layout)]
        for cp in mine:
            cp.start()
        slots = []
        for flip in range(1, N_DEV // 2):
            px, py = (1 - x if flip & 2 else x), (1 - y if flip & 1 else y)
            peer, pq = (px, py, c), 2 * px + py
            for (g, ref, off, r) in layout:
                k = 3 * g + flip - 1
                pltpu.make_async_remote_copy(
                    src_ref=ref.at[pq], dst_ref=outs[g].at[me, pl.ds(off, r)], send_sem=send_sems.at[k],
                    recv_sem=recv_sems.at[k], device_id=peer, device_id_type=MESH).start()
            for g in range(ng):
                k = 3 * g + flip - 1
                slots.append(pltpu.make_async_remote_copy(
                    src_ref=outs[g].at[pq], dst_ref=outs[g].at[pq], send_sem=send_sems.at[k],
                    recv_sem=recv_sems.at[k], device_id=peer, device_id_type=MESH))
        for w in slots:
            w.wait_recv()
        for w in slots:
            w.wait_send()
        for cp in mine:
            cp.wait()

    any_spec = pl.BlockSpec(memory_space=pl.ANY)
    return pl.pallas_call(
        body, in_specs=[any_spec] * ns, out_specs=[any_spec] * ng,
        out_shape=[jax.ShapeDtypeStruct((N_DEV // 2, r, w), dt) for (r, w, dt, _) in groups],
        scratch_shapes=[pltpu.SemaphoreType.DMA((3 * ng,)), pltpu.SemaphoreType.DMA((3 * ng,)),
                        pltpu.SemaphoreType.DMA((ns,))],
        name="chip_exchange")(*srcs)


def _row_tile(rows, off, target):
    for t in (1024, 512, 256, 128, 64, 32, 16, 8):
        if t <= target and rows % t == 0 and off % t == 0:
            return t
    raise ValueError((rows, off))


def adamw(recv, off, w, m, v, *, name):
    rows, W = w.shape
    nslot = recv.shape[0]
    tr = _row_tile(rows, off, 256)
    o = off // tr
    c1 = 1.0 - ADAM_B1 ** ADAM_STEP
    c2 = 1.0 - ADAM_B2 ** ADAM_STEP

    def body(r_ref, w_ref, m_ref, v_ref, g_ref, d_ref, mo_ref, vo_ref):
        g = r_ref[0].astype(F32)
        for j in range(1, nslot):
            g = g + r_ref[j].astype(F32)
        mn = ADAM_B1 * m_ref[...] + (1.0 - ADAM_B1) * g
        vn = ADAM_B2 * v_ref[...] + (1.0 - ADAM_B2) * (g * g)
        g_ref[...] = g
        mo_ref[...] = mn
        vo_ref[...] = vn
        d_ref[...] = -ADAM_LR * ((mn / c1) / (jnp.sqrt(vn / c2) + ADAM_EPS) + ADAM_WD * w_ref[...])

    spec = pl.BlockSpec((tr, W), lambda i: (i, 0))
    return pl.pallas_call(
        body, grid=(rows // tr,), in_specs=[pl.BlockSpec((nslot, tr, W), lambda i: (0, i + o, 0)), spec, spec, spec],
        out_specs=[spec] * 4, out_shape=[jax.ShapeDtypeStruct((rows, W), F32)] * 4,
        compiler_params=_cp("parallel"), name=name)(recv, w, m, v)


def join_columns(gathered, off, K, *, name):
    _, _, n = gathered.shape
    tr = _row_tile(K, off, 256)
    o = off // tr

    def body(i_ref, o_ref):
        for d in range(N_DEV):
            o_ref[:, d * n:(d + 1) * n] = i_ref[d]

    return pl.pallas_call(
        body, grid=(K // tr,), in_specs=[pl.BlockSpec((N_DEV, tr, n), lambda i: (0, i + o, 0))],
        out_specs=pl.BlockSpec((tr, N_DEV * n), lambda i: (i, 0)),
        out_shape=jax.ShapeDtypeStruct((K, N_DEV * n), gathered.dtype), compiler_params=_cp("parallel"),
        name=name)(gathered)


def split_columns(full, *, name):
    K, N = full.shape
    n = N // N_DEV
    tr = _row_tile(K, 0, 256)

    def body(i_ref, o_ref):
        for d in range(N_DEV):
            o_ref[d] = i_ref[:, d * n:(d + 1) * n].astype(o_ref.dtype)

    return pl.pallas_call(
        body, grid=(K // tr,), in_specs=[pl.BlockSpec((tr, N), lambda i: (i, 0))],
        out_specs=pl.BlockSpec((N_DEV, tr, n), lambda i: (0, i, 0)),
        out_shape=jax.ShapeDtypeStruct((N_DEV, K, n), BF16), compiler_params=_cp("parallel"), name=name)(full)


def _pack(arrs, dtype, row_mult):
    flat = jnp.concatenate([a.reshape(-1).astype(dtype) for a in arrs])
    rows = -(-flat.shape[0] // PACK_W)
    rows = -(-rows // row_mult) * row_mult
    return jnp.pad(flat, (0, rows * PACK_W - flat.shape[0])).reshape(rows, PACK_W)


def _pack_dev(arrs, dtype, row_mult):
    flat = jnp.concatenate([a.reshape(N_DEV, -1).astype(dtype) for a in arrs], axis=1)
    rows = -(-flat.shape[1] // PACK_W)
    rows = -(-rows // row_mult) * row_mult
    return jnp.pad(flat, ((0, 0), (0, rows * PACK_W - flat.shape[1]))).reshape(N_DEV, rows, PACK_W)


def _unpack(buf, shapes):
    lead = buf.shape[:-2]
    flat = buf.reshape(lead + (-1,))
    outs, off = [], 0
    for s in shapes:
        n = math.prod(s)
        outs.append(flat[..., off:off + n].reshape(lead + tuple(s)))
        off += n
    return outs


def _join(g, axis):
    g = jnp.moveaxis(g, 0, axis)
    return g.reshape(g.shape[:axis] + (g.shape[axis] * g.shape[axis + 1],) + g.shape[axis + 2:])


def _split(full, axis):
    s = full.shape
    g = full.reshape(s[:axis] + (N_DEV, s[axis] // N_DEV) + s[axis + 1:])
    return jnp.moveaxis(g, axis, 0)


def kernel(x, p, a_pw1_w, a_pw1_b, a_dw_w, a_dw_b, a_ln_g, a_ln_b, a_pw2_w, a_pw2_b, b_wq, kv_wk, kv_wv, b_wo, ln_mix_g, ln_mix_b, ffn_w_up, ffn_w_gate, ffn_conv_w, ffn_conv_b, ffn_w_down, ple_w_gate, ple_w_proj, ln_ffn_g, ln_ffn_b, loss_target, m_a_pw1_w, m_a_pw1_b, m_a_dw_w, m_a_dw_b, m_a_ln_g, m_a_ln_b, m_a_pw2_w, m_a_pw2_b, m_b_wq, m_kv_wk, m_kv_wv, m_b_wo, m_ln_mix_g, m_ln_mix_b, m_ffn_w_up, m_ffn_w_gate, m_ffn_conv_w, m_ffn_conv_b, m_ffn_w_down, m_ple_w_gate, m_ple_w_proj, m_ln_ffn_g, m_ln_ffn_b, v_a_pw1_w, v_a_pw1_b, v_a_dw_w, v_a_dw_b, v_a_ln_g, v_a_ln_b, v_a_pw2_w, v_a_pw2_b, v_b_wq, v_kv_wk, v_kv_wv, v_b_wo, v_ln_mix_g, v_ln_mix_b, v_ffn_w_up, v_ffn_w_gate, v_ffn_conv_w, v_ffn_conv_b, v_ffn_w_down, v_ple_w_gate, v_ple_w_proj, v_ln_ffn_g, v_ln_ffn_b):
    local = dict(a_pw1_w=a_pw1_w, a_pw1_b=a_pw1_b, a_dw_w=a_dw_w, a_dw_b=a_dw_b, a_ln_g=a_ln_g, a_ln_b=a_ln_b, a_pw2_w=a_pw2_w, a_pw2_b=a_pw2_b, b_wq=b_wq, kv_wk=kv_wk, kv_wv=kv_wv, b_wo=b_wo, ln_mix_g=ln_mix_g, ln_mix_b=ln_mix_b, ffn_w_up=ffn_w_up, ffn_w_gate=ffn_w_gate, ffn_conv_w=ffn_conv_w, ffn_conv_b=ffn_conv_b, ffn_w_down=ffn_w_down, ple_w_gate=ple_w_gate, ple_w_proj=ple_w_proj, ln_ffn_g=ln_ffn_g, ln_ffn_b=ln_ffn_b)
    mom1 = dict(a_pw1_w=m_a_pw1_w, a_pw1_b=m_a_pw1_b, a_dw_w=m_a_dw_w, a_dw_b=m_a_dw_b, a_ln_g=m_a_ln_g, a_ln_b=m_a_ln_b, a_pw2_w=m_a_pw2_w, a_pw2_b=m_a_pw2_b, b_wq=m_b_wq, kv_wk=m_kv_wk, kv_wv=m_kv_wv, b_wo=m_b_wo, ln_mix_g=m_ln_mix_g, ln_mix_b=m_ln_mix_b, ffn_w_up=m_ffn_w_up, ffn_w_gate=m_ffn_w_gate, ffn_conv_w=m_ffn_conv_w, ffn_conv_b=m_ffn_conv_b, ffn_w_down=m_ffn_w_down, ple_w_gate=m_ple_w_gate, ple_w_proj=m_ple_w_proj, ln_ffn_g=m_ln_ffn_g, ln_ffn_b=m_ln_ffn_b)
    mom2 = dict(a_pw1_w=v_a_pw1_w, a_pw1_b=v_a_pw1_b, a_dw_w=v_a_dw_w, a_dw_b=v_a_dw_b, a_ln_g=v_a_ln_g, a_ln_b=v_a_ln_b, a_pw2_w=v_a_pw2_w, a_pw2_b=v_a_pw2_b, b_wq=v_b_wq, kv_wk=v_kv_wk, kv_wv=v_kv_wv, b_wo=v_b_wo, ln_mix_g=v_ln_mix_g, ln_mix_b=v_ln_mix_b, ffn_w_up=v_ffn_w_up, ffn_w_gate=v_ffn_w_gate, ffn_conv_w=v_ffn_conv_w, ffn_conv_b=v_ffn_conv_b, ffn_w_down=v_ffn_w_down, ple_w_gate=v_ple_w_gate, ple_w_proj=v_ple_w_proj, ln_ffn_g=v_ln_ffn_g, ln_ffn_b=v_ln_ffn_b)
    small_names = [n for n, _ in SMALL]
    small_shapes = [local[n].shape for n in small_names]
    repl_shapes = [local[n].shape for n in REPL]

    widths = sorted({local[n].shape[-1] for n, _ in BIG}, reverse=True)
    groups = {w: [n for n, _ in BIG if local[n].shape[-1] == w] for w in widths}
    offset, rows_of = {}, {}
    for w, names in groups.items():
        off = 0
        for n in names:
            offset[n], rows_of[n] = off, math.prod(local[n].shape[:-1])
            off += rows_of[n]
    sends = [jnp.concatenate([local[n].reshape(-1, w).astype(BF16) for n in names]) for w, names in groups.items()]
    gathered = all_gather(sends + [_pack([local[n] for n in small_names], F32, SUBLANES)])
    gath = dict(zip(widths, gathered[:-1]))
    W = {}
    for n, ax in BIG:
        w = local[n].shape[-1]
        nl = local[n].shape[0] if local[n].ndim == 3 else 1
        per = rows_of[n] // nl
        if ax == local[n].ndim - 1:
            W[n] = [join_columns(gath[w], offset[n] + l * per, per, name=f"join_{n}_{l}") for l in range(nl)]
        else:
            W[n] = [gath[w][:, offset[n] + l * per:offset[n] + (l + 1) * per].reshape(N_DEV * per, w) for l in range(nl)]
    for n in ("kv_wk", "kv_wv"):
        W[n] = W[n][0]
    W.update({n: _join(g, ax) for (n, ax), g in zip(SMALL, _unpack(gathered[-1], small_shapes))})
    W.update({n: local[n] for n in REPL})

    xs = x[0]
    S, D = xs.shape
    x_in, r1s, x1s, r2s, us, gps, gs, hhs, pgls, pps = [], [], [], [], [], [], [], [], [], []
    h1s, h2s, h3s, h5s, qs, os_, tots = {}, {}, {}, {}, {}, {}, {}
    kk = vv = None
    for i in range(DEPTH):
        x_in.append(xs)
        if i < N_A:
            h1 = mm(xs, W["a_pw1_w"][i], "nn", bias=W["a_pw1_b"][i][None], out_dtype=BF16, name=f"pw1_{i}")
            h2 = glu_fwd(h1, name=f"glu_{i}")
            h3, h5 = conv_ln_silu_fwd(h2, W["a_dw_w"][i], W["a_dw_b"][i][None], W["a_ln_g"][i][None],
                                      W["a_ln_b"][i][None], name=f"dwconv_{i}")
            mix = mm(h5, W["a_pw2_w"][i], "nn", bias=W["a_pw2_b"][i][None], name=f"pw2_{i}")
            h1s[i], h2s[i], h3s[i], h5s[i] = h1, h2, h3, h5
        else:
            j = i - N_A
            if kk is None:
                kk, vv = mm(xs, W["kv_wk"], "nn", out_dtype=BF16, also=W["kv_wv"], name="proj_kv")
            q = mm(xs, W["b_wq"][j], "nn", out_dtype=BF16, name=f"proj_q_{i}")
            o, tot, seen = attn_fwd(q, kk, vv, name=f"attn_{i}")
            mix = mm(o, W["b_wo"][j], "nn", name=f"proj_o_{i}")
            qs[i], os_[i], tots[i] = q, o, (tot, seen)
        r1, x1 = res_ln(xs, mix, W["ln_mix_g"][i][None], W["ln_mix_b"][i][None], name=f"ln_mix_{i}")
        u, gp = mm(x1, W["ffn_w_up"][i], "nn", out_dtype=BF16, also=W["ffn_w_gate"][i], name=f"ffn_up_gate_{i}")
        g, hh = conv_act_fwd(gp, u, W["ffn_conv_w"][i], W["ffn_conv_b"][i][None], name=f"ffn_conv_{i}")
        f = mm(hh, W["ffn_w_down"][i], "nn", name=f"ffn_down_{i}")
        pgl = mm(x1, W["ple_w_gate"][i], "nn", out_dtype=BF16, name=f"ple_gate_{i}")
        pp = mm(p[i, 0], W["ple_w_proj"][i], "nn", out_dtype=BF16, name=f"ple_proj_{i}")
        r2, xs = res_ln(x1, f, W["ln_ffn_g"][i][None], W["ln_ffn_b"][i][None], ple=(pgl, pp), name=f"ln_ffn_{i}")
        for lst, val in ((r1s, r1), (x1s, x1), (r2s, r2), (us, u), (gps, gp), (gs, g), (hhs, hh), (pgls, pgl), (pps, pp)):
            lst.append(val)

    dx, loss_part = loss_grad(xs, loss_target[0], name="loss")
    G = {n: [None] * local[n].shape[0] for n in WEIGHTS if n not in ("kv_wk", "kv_wv")}
    dk = dv = None
    for i in reversed(range(DEPTH)):
        x1 = x1s[i]
        dr2, dpp, dpgl, G["ln_ffn_g"][i], G["ln_ffn_b"][i] = ln_ple_bwd(r2s[i], W["ln_ffn_g"][i][None], dx, pgls[i], pps[i],
                                                                      name=f"ln_ffn_bwd_{i}")
        dhh = mm(dr2, W["ffn_w_down"][i], "nt", out_dtype=BF16, name=f"ffn_down_dx_{i}")
        G["ffn_w_down"][i] = mm(hhs[i], dr2, "tn", out_dtype=BF16, name=f"ffn_down_dw_{i}")
        G["ple_w_proj"][i] = mm(p[i, 0], dpp, "tn", out_dtype=BF16, name=f"ple_proj_dw_{i}")
        G["ple_w_gate"][i] = mm(x1, dpgl, "tn", out_dtype=BF16, name=f"ple_gate_dw_{i}")
        du, dgp, G["ffn_conv_w"][i], G["ffn_conv_b"][i] = ffn_gate_bwd(dhh, us[i], gs[i], gps[i], W["ffn_conv_w"][i],
                                                                       name=f"ffn_gate_bwd_{i}")
        G["ffn_w_up"][i], G["ffn_w_gate"][i] = mm(x1, du, "tn", out_dtype=BF16, also=dgp, name=f"ffn_up_gate_dw_{i}")
        dx1 = mm(du, W["ffn_w_up"][i], "nt", add=dr2, add_scale=DN_ALPHA, plus=(dgp, W["ffn_w_gate"][i]),
                 name=f"ffn_up_gate_dx_{i}")
        dx1 = mm(dpgl, W["ple_w_gate"][i], "nt", add=dx1, name=f"ple_gate_dx_{i}")
        dr1, G["ln_mix_g"][i], G["ln_mix_b"][i], dr1_sum = ln_bwd(r1s[i], W["ln_mix_g"][i][None], dx1, name=f"ln_mix_bwd_{i}")
        if i < N_A:
            G["a_pw2_w"][i] = mm(h5s[i], dr1, "tn", out_dtype=BF16, name=f"pw2_dw_{i}")
            G["a_pw2_b"][i] = dr1_sum
            dh5 = mm(dr1, W["a_pw2_w"][i], "nt", name=f"pw2_dx_{i}")
            dh3, G["a_ln_g"][i], G["a_ln_b"][i] = ln_silu_bwd(h3s[i], W["a_ln_g"][i][None], W["a_ln_b"][i][None], dh5,
                                                             name=f"dwconv_ln_bwd_{i}")
            dh2 = conv_bwd_x(dh3, W["a_dw_w"][i], out_dtype=F32, name=f"dwconv_dx_{i}")
            G["a_dw_w"][i], G["a_dw_b"][i] = conv_bwd_w(h2s[i], dh3, CONV_W, name=f"dwconv_dw_{i}")
            dh1, G["a_pw1_b"][i] = glu_bwd(h1s[i], dh2, name=f"glu_bwd_{i}")
            G["a_pw1_w"][i] = mm(x_in[i], dh1, "tn", out_dtype=BF16, name=f"pw1_dw_{i}")
            dx = mm(dh1, W["a_pw1_w"][i], "nt", add=dr1, add_scale=DN_ALPHA, name=f"pw1_dx_{i}")
        else:
            j = i - N_A
            G["b_wo"][j] = mm(os_[i], dr1, "tn", out_dtype=BF16, name=f"proj_o_dw_{i}")
            do = mm(dr1, W["b_wo"][j], "nt", out_dtype=BF16, name=f"proj_o_dx_{i}")
            dq, dk, dv = attn_bwd(qs[i], kk, vv, *tots[i], do, dk, dv, name=f"attn_bwd_{i}")
            G["b_wq"][j] = mm(x_in[i], dq, "tn", out_dtype=BF16, name=f"proj_q_dw_{i}")
            dx = mm(dq, W["b_wq"][j], "nt", add=dr1, add_scale=DN_ALPHA, name=f"proj_q_dx_{i}")
            if j == 0:
                G["kv_wk"], G["kv_wv"] = mm(x_in[i], dk, "tn", out_dtype=BF16, also=dv, name="proj_kv_dw")
                dx = mm(dk, W["kv_wk"], "nt", add=dx, plus=(dv, W["kv_wv"]), name="proj_kv_dx")
    grad_x = dx[None]
    shard_axis = dict(BIG + SMALL)
    for n in small_names + list(REPL):
        full = list(local[n].shape)
        if n in shard_axis:
            full[shard_axis[n]] *= N_DEV
        G[n] = jnp.stack(G[n]).reshape(full)

    n_small = sum(math.prod(s) for s in small_shapes)
    n_repl = sum(math.prod(s) for s in repl_shapes)
    repl_flat = jnp.concatenate([G[n].reshape(-1) for n in REPL] + [loss_part.reshape(-1)[:1]])
    send_small = _pack_dev([_split(G[n], ax) for n, ax in SMALL] + [jnp.broadcast_to(repl_flat, (N_DEV, n_repl + 1))],
                           F32, SUBLANES)
    ex_groups = []
    for w, names in groups.items():
        lst = []
        for n in names:
            layers = G[n] if isinstance(G[n], list) else [G[n]]
            per = rows_of[n] // len(layers)
            for l, g in enumerate(layers):
                if shard_axis[n] == local[n].ndim - 1:
                    src = split_columns(g, name=f"split_{n}_{l}")
                else:
                    src = g.reshape(N_DEV, per, w)
                lst.append((src, offset[n] + l * per))
        ex_groups.append((sum(rows_of[n] for n in names), w, BF16, lst))
    ex_groups.append((send_small.shape[1], PACK_W, F32, [(send_small, 0)]))
    gots = pair_exchange(ex_groups)
    core = lax.axis_index("c").astype(jnp.int32).reshape(1)
    sum_groups = [(rows, w, dt, [(pair_sum(src, got, off, core, name=f"pair_sum_{gi}_{si}"), off) for si, (src, off) in enumerate(lst)])
                  for gi, ((rows, w, dt, lst), got) in enumerate(zip(ex_groups, gots))]
    recvs = chip_exchange(sum_groups)
    recv = dict(zip(widths, recvs[:-1]))
    recv_small = recvs[-1]

    out = {}
    for n, _ in BIG:
        w = local[n].shape[-1]
        res = adamw(recv[w], offset[n], local[n].reshape(-1, w), mom1[n].reshape(-1, w), mom2[n].reshape(-1, w),
                    name=f"adamw_{n}")
        out[n] = [r.reshape(local[n].shape) for r in res]

    def state(d):
        small = _pack([d[n] for n in small_names] + [d[n] for n in REPL], F32, SUBLANES)
        return jnp.pad(small, ((0, recv_small.shape[1] - small.shape[0]), (0, 0)))

    out_small = adamw(recv_small, 0, state(local), state(mom1), state(mom2), name="adamw_vectors")
    loss = out_small[0].reshape(-1)[n_small + n_repl]
    vecs = [dict(zip(small_names + list(REPL), _unpack(o, small_shapes + repl_shapes))) for o in out_small]
    per_kind = [[out[n][kind] if n in out else vecs[kind][n] for n in WEIGHTS] for kind in range(4)]
    grads, deltas, new_m, new_v = per_kind
    return (loss, grad_x, *grads, *deltas, *new_m, *new_v)
```

```python
import functools
import math

import jax
import jax.numpy as jnp
from jax import lax
from jax.experimental import pallas as pl
from jax.experimental.pallas import tpu as pltpu

F32 = jnp.float32
BF16 = jnp.bfloat16
MESH = pl.DeviceIdType.MESH

N_DEV = 8
DEPTH = 4
N_A = 2
HEAD_DIM = 64
Q_BLOCK = 128
CONV_W = 31
FFN_CONV_W = 3
LN_EPS = 1e-5
DN_ALPHA = (2.0 * DEPTH) ** 0.25
ADAM_LR = 0.001
ADAM_B1 = 0.9
ADAM_B2 = 0.999
ADAM_EPS = 1e-08
ADAM_WD = 0.01
ADAM_STEP = 10

LANES = 128
SUBLANES = 8
PACK_W = 1024
VMEM_LIMIT = 56 * 1024 * 1024

BIG = (("a_pw1_w", 2), ("a_pw2_w", 1), ("b_wq", 1), ("kv_wk", 0), ("kv_wv", 0), ("b_wo", 1),
       ("ffn_w_up", 2), ("ffn_w_gate", 2), ("ffn_w_down", 1), ("ple_w_gate", 1), ("ple_w_proj", 2))
SMALL = (("a_pw1_b", 1), ("a_dw_w", 2), ("a_dw_b", 1), ("a_ln_g", 1), ("a_ln_b", 1), ("a_pw2_b", 1),
         ("ffn_conv_w", 2))
REPL = ("ln_mix_g", "ln_mix_b", "ffn_conv_b", "ln_ffn_g", "ln_ffn_b")
WEIGHTS = ("a_pw1_w", "a_pw1_b", "a_dw_w", "a_dw_b", "a_ln_g", "a_ln_b", "a_pw2_w", "a_pw2_b", "b_wq", "kv_wk",
           "kv_wv", "b_wo", "ln_mix_g", "ln_mix_b", "ffn_w_up", "ffn_w_gate", "ffn_conv_w", "ffn_conv_b",
           "ffn_w_down", "ple_w_gate", "ple_w_proj", "ln_ffn_g", "ln_ffn_b")


def _cp(*sem):
    return pltpu.CompilerParams(dimension_semantics=sem, vmem_limit_bytes=VMEM_LIMIT)


def _pick(dim, target, align=LANES):
    if dim <= target:
        return dim
    t = (target // align) * align
    while t >= align:
        if dim % t == 0:
            return t
        t -= align
    return dim


MM_ROWS = 1024
MM_ROWS_TN = 1536
MM_COLS = 1536
MM_DEPTH = 2816
MM_DEPTH_TN = 1536

_DOT_DIMS = {"nn": (((1,), (0,)), ((), ())), "nt": (((1,), (1,)), ((), ())), "tn": (((0,), (0,)), ((), ()))}


def mm(a, b, mode, *, bias=None, add=None, add_scale=1.0, out_dtype=F32, also=None, plus=None, name):
    if mode == "tn":
        K, M = a.shape
    else:
        M, K = a.shape
    N = b.shape[0] if mode == "nt" else b.shape[1]
    tm = _pick(M, MM_ROWS_TN if mode == "tn" else MM_ROWS if plus is None else MM_ROWS // 2)
    tn = _pick(N, MM_COLS)
    tk = _pick(K, MM_DEPTH_TN if mode == "tn" else MM_DEPTH)
    nk = K // tk
    dims = _DOT_DIMS[mode]
    n_out = 1 if also is None else 2

    def body(*refs):
        refs = list(refs)
        a_ref, b_ref = refs.pop(0), refs.pop(0)
        b2_ref = refs.pop(0) if also is not None else None
        a3_ref, b3_ref = (refs.pop(0), refs.pop(0)) if plus is not None else (None, None)
        bias_ref = refs.pop(0) if bias is not None else None
        add_ref = refs.pop(0) if add is not None else None
        o_refs, acc_refs = refs[:n_out], refs[n_out:]
        k = pl.program_id(2)

        @pl.when(k == 0)
        def _():
            for acc_ref in acc_refs:
                acc_ref[...] = jnp.zeros_like(acc_ref)

        a_v = a_ref[...].astype(BF16)
        acc_refs[0][...] += lax.dot_general(a_v, b_ref[...].astype(BF16), dims, preferred_element_type=F32)
        if b2_ref is not None:
            acc_refs[1][...] += lax.dot_general(a_v, b2_ref[...].astype(BF16), dims, preferred_element_type=F32)
        if a3_ref is not None:
            acc_refs[0][...] += lax.dot_general(a3_ref[...].astype(BF16), b3_ref[...].astype(BF16), dims,
                                                preferred_element_type=F32)

        @pl.when(k == nk - 1)
        def _():
            for o_ref, acc_ref in zip(o_refs, acc_refs):
                r = acc_ref[...]
                if bias_ref is not None:
                    r = r + bias_ref[...]
                if add_ref is not None:
                    r = r + add_scale * add_ref[...].astype(F32)
                o_ref[...] = r.astype(o_ref.dtype)

    a_spec = pl.BlockSpec((tk, tm), lambda j, i, k: (k, i)) if mode == "tn" else pl.BlockSpec((tm, tk), lambda j, i, k: (i, k))
    b_spec = pl.BlockSpec((tn, tk), lambda j, i, k: (j, k)) if mode == "nt" else pl.BlockSpec((tk, tn), lambda j, i, k: (k, j))
    in_specs, args = [a_spec, b_spec], [a, b]
    if also is not None:
        in_specs.append(b_spec)
        args.append(also)
    if plus is not None:
        in_specs += [a_spec, b_spec]
        args += list(plus)
    if bias is not None:
        in_specs.append(pl.BlockSpec((1, tn), lambda j, i, k: (0, j)))
        args.append(bias)
    if add is not None:
        in_specs.append(pl.BlockSpec((tm, tn), lambda j, i, k: (i, j)))
        args.append(add)
    o_spec = pl.BlockSpec((tm, tn), lambda j, i, k: (i, j))
    outs = pl.pallas_call(
        body, grid=(N // tn, M // tm, nk), in_specs=in_specs,
        out_specs=[o_spec] * n_out, out_shape=[jax.ShapeDtypeStruct((M, N), out_dtype)] * n_out,
        scratch_shapes=[pltpu.VMEM((tm, tn), F32)] * n_out,
        compiler_params=_cp("parallel", "parallel", "arbitrary"), name=name)(*args)
    return outs[0] if also is None else tuple(outs)


ROW_BLOCK = 512


def _rows(body, *, n_rows, tm, row_ins, full_ins=(), row_outs=(), acc_outs=(), scratch=(), reverse=False, name):
    n = n_rows // tm

    def rmap(i):
        return (n - 1 - i, 0) if reverse else (i, 0)

    in_specs = [pl.BlockSpec((tm, a.shape[1]), rmap) for a in row_ins]
    in_specs += [pl.BlockSpec(a.shape, lambda i, nd=a.ndim: (0,) * nd) for a in full_ins]
    out_shape = [jax.ShapeDtypeStruct((n_rows, w), dt) for (w, dt) in row_outs]
    out_shape += [jax.ShapeDtypeStruct(s, dt) for (s, dt) in acc_outs]
    out_specs = [pl.BlockSpec((tm, w), rmap) for (w, dt) in row_outs]
    out_specs += [pl.BlockSpec(s, lambda i, nd=len(s): (0,) * nd) for (s, dt) in acc_outs]
    return pl.pallas_call(
        functools.partial(body, n), grid=(n,), in_specs=in_specs, out_specs=out_specs, out_shape=out_shape,
        scratch_shapes=list(scratch), compiler_params=_cp("arbitrary"), name=name)(*row_ins, *full_ins)


def _sigmoid(x):
    return 1.0 / (1.0 + jnp.exp(-x))


def _ln_hat(r):
    mu = jnp.mean(r, axis=-1, keepdims=True)
    xc = r - mu
    var = jnp.mean(xc * xc, axis=-1, keepdims=True)
    rstd = lax.rsqrt(var + LN_EPS)
    return xc * rstd, rstd


def _ln_back(xhat, rstd, g, dy):
    dxh = dy * g
    m1 = jnp.mean(dxh, axis=-1, keepdims=True)
    m2 = jnp.mean(dxh * xhat, axis=-1, keepdims=True)
    return rstd * (dxh - m1 - xhat * m2)


def _colsum(x):
    return jnp.sum(x, axis=0, keepdims=True)


def _acc(i, ref, val):
    @pl.when(i == 0)
    def _():
        ref[...] = val

    @pl.when(i > 0)
    def _():
        ref[...] += val


def res_ln(x, mix, g, b, *, ple=None, name):
    S, D = x.shape

    def body(n, *refs):
        if ple is None:
            x_ref, m_ref, g_ref, b_ref, r_ref, y_ref = refs
            r = DN_ALPHA * x_ref[...] + m_ref[...]
        else:
            x_ref, m_ref, pgl_ref, pp_ref, g_ref, b_ref, r_ref, y_ref = refs
            r = DN_ALPHA * x_ref[...] + m_ref[...] + _sigmoid(pgl_ref[...].astype(F32)) * pp_ref[...].astype(F32)
        xhat, _ = _ln_hat(r)
        r_ref[...] = r
        y_ref[...] = xhat * g_ref[...] + b_ref[...]

    row_ins = [x, mix] + ([] if ple is None else list(ple))
    return _rows(body, n_rows=S, tm=_pick(S, ROW_BLOCK, SUBLANES), row_ins=row_ins, full_ins=[g, b],
                 row_outs=[(D, F32), (D, F32)], name=name)


def ln_bwd(r, g, dy, *, name):
    S, D = r.shape

    def body(n, r_ref, dy_ref, g_ref, dr_ref, dg_ref, db_ref, ds_ref):
        i = pl.program_id(0)
        xhat, rstd = _ln_hat(r_ref[...])
        dy_v = dy_ref[...]
        dr = _ln_back(xhat, rstd, g_ref[...], dy_v)
        dr_ref[...] = dr
        _acc(i, dg_ref, _colsum(dy_v * xhat))
        _acc(i, db_ref, _colsum(dy_v))
        _acc(i, ds_ref, _colsum(dr))

    return _rows(body, n_rows=S, tm=_pick(S, ROW_BLOCK, SUBLANES), row_ins=[r, dy], full_ins=[g],
                 row_outs=[(D, F32)], acc_outs=[((1, D), F32)] * 3, name=name)


def glu_fwd(h1, *, name):
    S, D2 = h1.shape
    D = D2 // 2

    def body(n, h_ref, o_ref):
        o_ref[...] = h_ref[:, :D].astype(F32) * _sigmoid(h_ref[:, D:].astype(F32))

    return _rows(body, n_rows=S, tm=_pick(S, ROW_BLOCK, SUBLANES), row_ins=[h1], row_outs=[(D, F32)], name=name)[0]


def glu_bwd(h1, dh2, *, name):
    S, D2 = h1.shape
    D = D2 // 2

    def body(n, h_ref, d_ref, o_ref, s_ref):
        i = pl.program_id(0)
        a, sg, d = h_ref[:, :D].astype(F32), _sigmoid(h_ref[:, D:].astype(F32)), d_ref[...]
        da = d * sg
        dg = d * a * sg * (1.0 - sg)
        o_ref[:, :D] = da.astype(o_ref.dtype)
        o_ref[:, D:] = dg.astype(o_ref.dtype)
        _acc(i, s_ref, jnp.concatenate([_colsum(da), _colsum(dg)], axis=1))

    return _rows(body, n_rows=S, tm=_pick(S, ROW_BLOCK, SUBLANES), row_ins=[h1, dh2], row_outs=[(D2, BF16)],
                 acc_outs=[((1, D2), F32)], name=name)


CONV_ROWS = 32
CONV_LANES = 256


def _halo(k):
    return -(-(k - 1) // SUBLANES) * SUBLANES


def _phases(offs):
    return sorted({o % SUBLANES for o in offs} - {0})


def _shift_scratch(offs, n_rows, width):
    return pltpu.VMEM((max(len(_phases(offs)), 1), n_rows, width), F32)


def _make_shifted(buf_ref, sh_ref, offs):
    n = buf_ref.shape[0] - SUBLANES
    for p, b in enumerate(_phases(offs)):
        sh_ref[p, pl.ds(0, n), :] = buf_ref[pl.ds(b, n), :]


def _tap(buf_ref, sh_ref, offs, k, rc, rows, lc, lw):
    b = offs[k] % SUBLANES
    src = buf_ref if b == 0 else sh_ref.at[_phases(offs).index(b)]
    return src[pl.ds(offs[k] - b + rc, rows), pl.ds(lc, lw)]


def _conv_taps(buf_ref, sh_ref, w_ref, offs, tm, width, emit):
    _make_shifted(buf_ref, sh_ref, offs)
    rows = min(CONV_ROWS, tm)
    for lc in range(0, width, CONV_LANES):
        lw = min(CONV_LANES, width - lc)
        for rc in range(0, tm, rows):
            acc = None
            for k in range(len(offs)):
                t = _tap(buf_ref, sh_ref, offs, k, rc, rows, lc, lw) * w_ref[pl.ds(k, 1), pl.ds(lc, lw)]
                acc = t if acc is None else acc + t
            emit(rc, lc, lw, rows, acc)


def _fill_causal(i, buf_ref, x_ref, halo, tm):
    @pl.when(i == 0)
    def _():
        buf_ref[pl.ds(0, halo), :] = jnp.zeros((halo, buf_ref.shape[1]), F32)

    @pl.when(i > 0)
    def _():
        buf_ref[pl.ds(0, halo), :] = buf_ref[pl.ds(tm, halo), :]

    buf_ref[pl.ds(halo, tm), :] = x_ref[...].astype(F32)


def conv_ln_silu_fwd(x, w, b, g, beta, *, name):
    S, C = x.shape
    K = w.shape[0]
    halo = _halo(K)
    tm = _pick(S, 256, SUBLANES)
    offs = [halo - (K - 1) + k for k in range(K)]

    def body(n, x_ref, w_ref, b_ref, g_ref, beta_ref, h3_ref, h5_ref, buf_ref, sh_ref):
        i = pl.program_id(0)
        _fill_causal(i, buf_ref, x_ref, halo, tm)

        def emit(rc, lc, lw, rows, acc):
            h3_ref[pl.ds(rc, rows), pl.ds(lc, lw)] = acc + b_ref[:, pl.ds(lc, lw)]

        _conv_taps(buf_ref, sh_ref, w_ref, offs, tm, C, emit)
        xhat, _ = _ln_hat(h3_ref[...])
        h4 = xhat * g_ref[...] + beta_ref[...]
        h5_ref[...] = (h4 * _sigmoid(h4)).astype(h5_ref.dtype)

    return _rows(body, n_rows=S, tm=tm, row_ins=[x], full_ins=[w, b, g, beta], row_outs=[(C, F32), (C, BF16)],
                 scratch=[pltpu.VMEM((tm + halo, C), F32), _shift_scratch(offs, tm + halo, C)], name=name)


def conv_act_fwd(gp, u, w, b, *, name):
    S, C = gp.shape
    K = w.shape[0]
    halo = _halo(K)
    tm = _pick(S, 256, SUBLANES)
    offs = [halo - (K - 1) + k for k in range(K)]

    def body(n, x_ref, u_ref, w_ref, b_ref, g_ref, hh_ref, buf_ref, sh_ref):
        i = pl.program_id(0)
        _fill_causal(i, buf_ref, x_ref, halo, tm)

        def emit(rc, lc, lw, rows, acc):
            gv = acc + b_ref[:, pl.ds(lc, lw)]
            g_ref[pl.ds(rc, rows), pl.ds(lc, lw)] = gv.astype(g_ref.dtype)
            hh_ref[pl.ds(rc, rows), pl.ds(lc, lw)] = (gv * _sigmoid(gv) * u_ref[pl.ds(rc, rows), pl.ds(lc, lw)].astype(F32)).astype(hh_ref.dtype)

        _conv_taps(buf_ref, sh_ref, w_ref, offs, tm, C, emit)

    return _rows(body, n_rows=S, tm=tm, row_ins=[gp, u], full_ins=[w, b], row_outs=[(C, BF16), (C, BF16)],
                 scratch=[pltpu.VMEM((tm + halo, C), F32), _shift_scratch(offs, tm + halo, C)], name=name)


def conv_bwd_x(dy, w, *, out_dtype, name):
    S, C = dy.shape
    K = w.shape[0]
    halo = _halo(K)
    tm = _pick(S, 256, SUBLANES)
    offs = [K - 1 - k for k in range(K)]

    def body(n, dy_ref, w_ref, dx_ref, buf_ref, sh_ref):
        i = pl.program_id(0)

        @pl.when(i == 0)
        def _():
            buf_ref[pl.ds(tm, halo), :] = jnp.zeros((halo, C), F32)

        @pl.when(i > 0)
        def _():
            buf_ref[pl.ds(tm, halo), :] = buf_ref[pl.ds(0, halo), :]

        buf_ref[pl.ds(0, tm), :] = dy_ref[...].astype(F32)

        def emit(rc, lc, lw, rows, acc):
            dx_ref[pl.ds(rc, rows), pl.ds(lc, lw)] = acc.astype(dx_ref.dtype)

        _conv_taps(buf_ref, sh_ref, w_ref, offs, tm, C, emit)

    return _rows(body, n_rows=S, tm=tm, row_ins=[dy], full_ins=[w], row_outs=[(C, out_dtype)],
                 scratch=[pltpu.VMEM((tm + halo, C), F32), _shift_scratch(offs, tm + halo, C)], reverse=True, name=name)[0]


def conv_bwd_w(x, dy, K, *, name):
    S, C = x.shape
    halo = _halo(K)
    tm = _pick(S, 256, SUBLANES)
    offs = [halo - (K - 1) + k for k in range(K)]
    rows = min(CONV_ROWS, tm)

    def body(n, x_ref, dy_ref, dw_ref, db_ref, buf_ref, acc_ref, sh_ref):
        i = pl.program_id(0)
        _fill_causal(i, buf_ref, x_ref, halo, tm)
        _make_shifted(buf_ref, sh_ref, offs)

        @pl.when(i == 0)
        def _():
            acc_ref[...] = jnp.zeros_like(acc_ref)

        for lc in range(0, C, CONV_LANES):
            lw = min(CONV_LANES, C - lc)
            for k in range(K):
                s = None
                for rc in range(0, tm, rows):
                    t = dy_ref[pl.ds(rc, rows), pl.ds(lc, lw)].astype(F32) * _tap(buf_ref, sh_ref, offs, k, rc, rows, lc, lw)
                    s = t if s is None else s + t
                s8 = s[0:SUBLANES]
                for q in range(1, rows // SUBLANES):
                    s8 = s8 + s[q * SUBLANES:(q + 1) * SUBLANES]
                acc_ref[pl.ds(k * SUBLANES, SUBLANES), pl.ds(lc, lw)] += s8
        _acc(i, db_ref, _colsum(dy_ref[...].astype(F32)))

        @pl.when(i == n - 1)
        def _():
            for k in range(K):
                dw_ref[pl.ds(k, 1), :] = _colsum(acc_ref[pl.ds(k * SUBLANES, SUBLANES), :])

    return _rows(body, n_rows=S, tm=tm, row_ins=[x, dy], acc_outs=[((K, C), F32), ((1, C), F32)],
                 scratch=[pltpu.VMEM((tm + halo, C), F32), pltpu.VMEM((K * SUBLANES, C), F32),
                          _shift_scratch(offs, tm + halo, C)], name=name)


def ln_silu_bwd(h3, g, beta, dh5, *, name):
    S, C = h3.shape

    def body(n, h_ref, d_ref, g_ref, beta_ref, dh_ref, dg_ref, db_ref):
        i = pl.program_id(0)
        xhat, rstd = _ln_hat(h_ref[...])
        h4 = xhat * g_ref[...] + beta_ref[...]
        sg = _sigmoid(h4)
        dh4 = d_ref[...] * sg * (1.0 + h4 * (1.0 - sg))
        dh_ref[...] = _ln_back(xhat, rstd, g_ref[...], dh4)
        _acc(i, dg_ref, _colsum(dh4 * xhat))
        _acc(i, db_ref, _colsum(dh4))

    return _rows(body, n_rows=S, tm=_pick(S, ROW_BLOCK, SUBLANES), row_ins=[h3, dh5], full_ins=[g, beta],
                 row_outs=[(C, F32)], acc_outs=[((1, C), F32)] * 2, name=name)


def ffn_gate_bwd(dhh, u, g, gp, w, *, name):
    S, C = u.shape
    K = w.shape[0]
    halo = _halo(K)
    tm = _pick(S, 256, SUBLANES)
    offs = [K - 1 - k for k in range(K)]
    rows = min(CONV_ROWS, tm)

    def body(n, d_ref, u_ref, g_ref, gp_ref, w_ref, du_ref, dgp_ref, dw_ref, db_ref, buf_ref, sh_ref, acc_ref):
        i = pl.program_id(0)

        @pl.when(i == 0)
        def _():
            buf_ref[pl.ds(tm, halo), :] = jnp.zeros((halo, C), F32)
            acc_ref[...] = jnp.zeros_like(acc_ref)

        @pl.when(i > 0)
        def _():
            buf_ref[pl.ds(tm, halo), :] = buf_ref[pl.ds(0, halo), :]

        d, gv = d_ref[...].astype(F32), g_ref[...].astype(F32)
        sg = _sigmoid(gv)
        du_ref[...] = (d * gv * sg).astype(du_ref.dtype)
        dg = d * u_ref[...].astype(F32) * sg * (1.0 + gv * (1.0 - sg))
        buf_ref[pl.ds(0, tm), :] = dg
        _acc(i, db_ref, _colsum(dg))

        def emit(rc, lc, lw, nrows, acc):
            dgp_ref[pl.ds(rc, nrows), pl.ds(lc, lw)] = acc.astype(dgp_ref.dtype)

        _conv_taps(buf_ref, sh_ref, w_ref, offs, tm, C, emit)
        for lc in range(0, C, CONV_LANES):
            lw = min(CONV_LANES, C - lc)
            for k in range(K):
                s_ = None
                for rc in range(0, tm, rows):
                    t = gp_ref[pl.ds(rc, rows), pl.ds(lc, lw)].astype(F32) * _tap(buf_ref, sh_ref, offs, k, rc, rows, lc, lw)
                    s_ = t if s_ is None else s_ + t
                s8 = s_[0:SUBLANES]
                for q in range(1, rows // SUBLANES):
                    s8 = s8 + s_[q * SUBLANES:(q + 1) * SUBLANES]
                acc_ref[pl.ds(k * SUBLANES, SUBLANES), pl.ds(lc, lw)] += s8

        @pl.when(i == n - 1)
        def _():
            for k in range(K):
                dw_ref[pl.ds(k, 1), :] = _colsum(acc_ref[pl.ds(k * SUBLANES, SUBLANES), :])

    return _rows(body, n_rows=S, tm=tm, row_ins=[dhh, u, g, gp], full_ins=[w], row_outs=[(C, BF16), (C, BF16)],
                 acc_outs=[((K, C), F32), ((1, C), F32)],
                 scratch=[pltpu.VMEM((tm + halo, C), F32), _shift_scratch(offs, tm + halo, C),
                          pltpu.VMEM((K * SUBLANES, C), F32)], reverse=True, name=name)


def ln_ple_bwd(r, g, dy, pgl, pp, *, name):
    S, D = r.shape

    def body(n, r_ref, dy_ref, l_ref, p_ref, g_ref, dr_ref, dpp_ref, dpl_ref, dg_ref, db_ref):
        i = pl.program_id(0)
        xhat, rstd = _ln_hat(r_ref[...])
        dy_v = dy_ref[...]
        dr = _ln_back(xhat, rstd, g_ref[...], dy_v)
        dr_ref[...] = dr
        sg = _sigmoid(l_ref[...].astype(F32))
        dpp_ref[...] = (dr * sg).astype(dpp_ref.dtype)
        dpl_ref[...] = (dr * p_ref[...].astype(F32) * sg * (1.0 - sg)).astype(dpl_ref.dtype)
        _acc(i, dg_ref, _colsum(dy_v * xhat))
        _acc(i, db_ref, _colsum(dy_v))

    return _rows(body, n_rows=S, tm=_pick(S, ROW_BLOCK, SUBLANES), row_ins=[r, dy, pgl, pp], full_ins=[g],
                 row_outs=[(D, F32), (D, BF16), (D, BF16)], acc_outs=[((1, D), F32)] * 2, name=name)


def loss_grad(y, target, *, name):
    S, D = y.shape

    def body(n, y_ref, t_ref, dy_ref, l_ref):
        i = pl.program_id(0)
        e = y_ref[...] - t_ref[...]
        dy_ref[...] = e * (1.0 / D)
        s = jnp.sum(_colsum(e * e), axis=1, keepdims=True) * (0.5 / D)
        _acc(i, l_ref, jnp.broadcast_to(s, (1, LANES)))

    return _rows(body, n_rows=S, tm=_pick(S, ROW_BLOCK, SUBLANES), row_ins=[y, target], row_outs=[(D, F32)],
                 acc_outs=[((1, LANES), F32)], name=name)


def _key_step(S):
    return min(512, S // 2)


EXIT_LOG = -110.0


def _attn_consts():
    lane = lax.broadcasted_iota(jnp.int32, (1, LANES), 1)
    heads = (lane < HEAD_DIM, lane >= HEAD_DIM)
    row = lax.broadcasted_iota(jnp.int32, (Q_BLOCK, Q_BLOCK), 0)
    col = lax.broadcasted_iota(jnp.int32, (Q_BLOCK, Q_BLOCK), 1)
    causal = jnp.concatenate([col < row] * 2, axis=0)
    return heads, row, col, causal


def _tri(cond):
    return jnp.where(cond, 1.0, 0.0).astype(BF16)


def _keysum2(x, tri):
    hi = x.astype(BF16)
    lo = (x - hi.astype(F32)).astype(BF16)
    return jnp.dot(jnp.concatenate([hi, lo], axis=1), jnp.concatenate([tri, tri], axis=0),
                   preferred_element_type=F32)


def _stack_heads(x, heads):
    return jnp.concatenate([jnp.where(m, x, jnp.zeros_like(x)) for m in heads], axis=0)


def _log1m_beta(z):
    return -(jnp.maximum(z, 0.0) + jnp.log(1.0 + jnp.exp(-jnp.abs(z))))


def attn_fwd(q, k, v, *, name):
    S, D = q.shape
    nb = S // Q_BLOCK
    tk = _key_step(S)
    nkb = tk // Q_BLOCK
    scale = 1.0 / math.sqrt(HEAD_DIM)

    def body(q_ref, k_ref, v_ref, o_ref, tot_ref, seen_ref, vm_ref):
        heads, row, col, causal = _attn_consts()
        above = _tri(row > col)
        for h in range(2):
            vm_ref[h] = jnp.where(heads[h], v_ref[...], jnp.zeros_like(v_ref[...]))

        def step(sb, carry, qq, nblk, diag):
            acc, cl = carry
            c0 = pl.multiple_of(sb * tk, tk)
            z = lax.dot_general(qq, k_ref[pl.ds(c0, nblk * Q_BLOCK), :], _DOT_DIMS["nt"], preferred_element_type=F32)
            zl, es, rs = [], [], []
            for jb in range(nblk):
                zb = z[:, jb * Q_BLOCK:(jb + 1) * Q_BLOCK]
                lr = _log1m_beta(zb)
                l = jnp.where(causal, lr, 0.0) if diag and jb == nblk - 1 else lr
                zl.append(zb + lr)
                es.append(_keysum2(l, above))
                rs.append(jnp.sum(l, axis=1, keepdims=True))
            a = [None] * nblk
            for jb in reversed(range(nblk)):
                ab = jnp.exp(zl[jb] + es[jb] + cl)
                if diag and jb == nblk - 1:
                    ab = jnp.where(causal, ab, 0.0)
                a[jb] = ab.astype(BF16)
                cl = cl + rs[jb]
            a = jnp.concatenate(a, axis=1)
            for h in range(2):
                acc = acc + jnp.dot(a[h * Q_BLOCK:(h + 1) * Q_BLOCK], vm_ref[h, pl.ds(c0, nblk * Q_BLOCK), :],
                                    preferred_element_type=F32)
            return acc, cl

        def qblock(i, _):
            r0 = pl.multiple_of(i * Q_BLOCK, Q_BLOCK)
            qq = _stack_heads(q_ref[pl.ds(r0, Q_BLOCK), :] * scale, heads)
            last = i // nkb
            carry = (jnp.zeros((Q_BLOCK, LANES), F32), jnp.zeros((2 * Q_BLOCK, 1), F32))
            carry = lax.switch(i % nkb, [functools.partial(step, last, qq=qq, nblk=m + 1, diag=True) for m in range(nkb)],
                               carry)

            def more(c):
                return jnp.logical_and(c[0] < last, jnp.max(c[2]) >= EXIT_LOG)

            def left(c):
                return (c[0] + 1, *step(last - 1 - c[0], c[1:], qq, nkb, False))

            seen, acc, cl = lax.while_loop(more, left, (jnp.int32(0), *carry))
            o_ref[pl.ds(r0, Q_BLOCK), :] = acc.astype(o_ref.dtype)
            tot_ref[pl.ds(r0, Q_BLOCK), :] = jnp.where(heads[0], cl[:Q_BLOCK], cl[Q_BLOCK:])
            seen_ref[pl.ds(pl.multiple_of(i * SUBLANES, SUBLANES), SUBLANES), :] = jnp.full((SUBLANES, LANES), seen, F32)
            return 0

        lax.fori_loop(0, nb, qblock, 0)

    spec = pl.BlockSpec((S, LANES), lambda h: (0, h))
    seen_spec = pl.BlockSpec((nb * SUBLANES, LANES), lambda h: (0, h))
    return pl.pallas_call(body, grid=(D // LANES,), in_specs=[spec] * 3, out_specs=[spec, spec, seen_spec],
                          out_shape=[jax.ShapeDtypeStruct((S, D), BF16), jax.ShapeDtypeStruct((S, D), F32),
                                     jax.ShapeDtypeStruct((nb * SUBLANES, D), F32)],
                          scratch_shapes=[pltpu.VMEM((2, S, LANES), BF16)], compiler_params=_cp("parallel"),
                          name=name)(q, k, v)


def attn_bwd(q, k, v, tot, seen, do, dk0, dv0, *, name):
    S, D = q.shape
    nb = S // Q_BLOCK
    tk = _key_step(S)
    nkb = tk // Q_BLOCK
    scale = 1.0 / math.sqrt(HEAD_DIM)
    has_init = dk0 is not None

    def body(*refs):
        if has_init:
            q_ref, k_ref, v_ref, tot_ref, seen_ref, do_ref, dk0_ref, dv0_ref, dq_ref, dk_ref, dv_ref, km_ref = refs
            dk_ref[...] = dk0_ref[...]
            dv_ref[...] = dv0_ref[...]
        else:
            q_ref, k_ref, v_ref, tot_ref, seen_ref, do_ref, dq_ref, dk_ref, dv_ref, km_ref = refs
            dk_ref[...] = jnp.zeros_like(dk_ref)
            dv_ref[...] = jnp.zeros_like(dv_ref)
        heads, row, col, causal = _attn_consts()
        upto = _tri(row <= col)
        before = _tri(row < col)
        for h in range(2):
            km_ref[h] = jnp.where(heads[h], k_ref[...], jnp.zeros_like(k_ref[...]))

        def step(sb, carry, qq, dd, totl, nblk, diag):
            dq, pl_, pg = carry
            c0 = pl.multiple_of(sb * tk, tk)
            keys = pl.ds(c0, nblk * Q_BLOCK)
            z = lax.dot_general(qq, k_ref[keys, :], _DOT_DIMS["nt"], preferred_element_type=F32)
            da = lax.dot_general(dd, v_ref[keys, :], _DOT_DIMS["nt"], preferred_element_type=F32)
            blocks = range(nblk)
            masked = [diag and jb == nblk - 1 for jb in blocks]
            zb = [z[:, jb * Q_BLOCK:(jb + 1) * Q_BLOCK] for jb in blocks]
            lr = [_log1m_beta(zb[jb]) for jb in blocks]
            l = [jnp.where(causal, lr[jb], 0.0) if masked[jb] else lr[jb] for jb in blocks]
            lsum = [_keysum2(l[jb], upto) for jb in blocks]
            lrow = [jnp.sum(l[jb], axis=1, keepdims=True) for jb in blocks]
            a, g = [None] * nblk, [None] * nblk
            for jb in blocks:
                ab = jnp.exp(zb[jb] + lr[jb] + (totl - pl_ - lsum[jb]))
                if masked[jb]:
                    ab = jnp.where(causal, ab, 0.0)
                g[jb] = ab * da[:, jb * Q_BLOCK:(jb + 1) * Q_BLOCK]
                a[jb] = ab.astype(BF16)
                pl_ = pl_ + lrow[jb]
            gsum = [jnp.dot(g[jb].astype(BF16), before, preferred_element_type=F32) for jb in blocks]
            grow = [jnp.sum(g[jb], axis=1, keepdims=True) for jb in blocks]
            dz = [None] * nblk
            for jb in blocks:
                dzb = g[jb] * jnp.exp(lr[jb]) - jnp.exp(zb[jb] + lr[jb]) * (pg + gsum[jb])
                if masked[jb]:
                    dzb = jnp.where(causal, dzb, 0.0)
                dz[jb] = dzb.astype(BF16)
                pg = pg + grow[jb]
            a = jnp.concatenate(a, axis=1)
            dz = jnp.concatenate(dz, axis=1)
            for h in range(2):
                dq = dq + jnp.dot(dz[h * Q_BLOCK:(h + 1) * Q_BLOCK], km_ref[h, keys, :], preferred_element_type=F32)
            dk_ref[keys, :] += lax.dot_general(dz, qq, _DOT_DIMS["tn"], preferred_element_type=F32)
            dv_ref[keys, :] += lax.dot_general(a, dd, _DOT_DIMS["tn"], preferred_element_type=F32)
            return dq, pl_, pg

        def qblock(i, _):
            r0 = pl.multiple_of(i * Q_BLOCK, Q_BLOCK)
            qq = _stack_heads(q_ref[pl.ds(r0, Q_BLOCK), :] * scale, heads)
            dd = _stack_heads(do_ref[pl.ds(r0, Q_BLOCK), :].astype(BF16), heads)
            tot2 = tot_ref[pl.ds(r0, Q_BLOCK), :]
            totl = jnp.concatenate([tot2[:, 0:1], tot2[:, HEAD_DIM:HEAD_DIM + 1]], axis=0)
            last = i // nkb
            zc = jnp.zeros((2 * Q_BLOCK, 1), F32)
            carry = (jnp.zeros((Q_BLOCK, LANES), F32), zc, zc)
            walked = jnp.max(seen_ref[pl.ds(pl.multiple_of(i * SUBLANES, SUBLANES), SUBLANES), :]).astype(jnp.int32)
            first = last - jnp.clip(walked, 0, last)
            carry = lax.fori_loop(first, last, lambda sb, c: step(sb, c, qq, dd, totl, nkb, False), carry)
            carry = lax.switch(i % nkb, [functools.partial(step, last, qq=qq, dd=dd, totl=totl, nblk=m + 1, diag=True)
                                         for m in range(nkb)], carry)
            dq_ref[pl.ds(r0, Q_BLOCK), :] = (carry[0] * scale).astype(dq_ref.dtype)
            return 0

        lax.fori_loop(0, nb, qblock, 0)

    spec = pl.BlockSpec((S, LANES), lambda h: (0, h))
    seen_spec = pl.BlockSpec((nb * SUBLANES, LANES), lambda h: (0, h))
    args = [q, k, v, tot, seen, do] + ([dk0, dv0] if has_init else [])
    return pl.pallas_call(
        body, grid=(D // LANES,), in_specs=[spec] * 4 + [seen_spec] + [spec] * (len(args) - 5), out_specs=[spec] * 3,
        out_shape=[jax.ShapeDtypeStruct((S, D), BF16), jax.ShapeDtypeStruct((S, D), F32), jax.ShapeDtypeStruct((S, D), F32)],
        scratch_shapes=[pltpu.VMEM((2, S, LANES), BF16)], compiler_params=_cp("parallel"), name=name)(*args)


def _dev_index(px, py, pc):
    return 4 * px + 2 * py + pc


def all_gather(bufs):
    nb = len(bufs)

    def body(*refs):
        ins, outs = refs[:nb], refs[nb:2 * nb]
        send_sems, recv_sems, local_sems = refs[2 * nb:]
        x, y, c = lax.axis_index("x"), lax.axis_index("y"), lax.axis_index("c")
        me, sibling = (x, y, c), (x, y, 1 - c)
        chips = [(1 - x, y), (x, 1 - y), (1 - x, 1 - y)]

        def copy(b, k, block, to, from_input=False):
            slot = outs[b].at[_dev_index(*block)]
            return pltpu.make_async_remote_copy(
                src_ref=ins[b] if from_input else slot, dst_ref=slot,
                send_sem=send_sems.at[7 * b + k], recv_sem=recv_sems.at[7 * b + k], device_id=to, device_id_type=MESH)

        mine = [pltpu.make_async_copy(ins[b], outs[b].at[_dev_index(*me)], local_sems.at[b]) for b in range(nb)]
        for cp in mine:
            cp.start()
        first = []
        for b in range(nb):
            first.append(copy(b, 0, me, sibling, from_input=True))
            first += [copy(b, 1 + j, me, (*chip, c), from_input=True) for j, chip in enumerate(chips)]
        for cp in first:
            cp.start()
        passed = []
        for j, chip in enumerate(chips):
            for b in range(nb):
                copy(b, 1 + j, (*chip, c), me).wait_recv()
                fwd = copy(b, 4 + j, (*chip, c), sibling)
                fwd.start()
                passed.append(fwd)
        for b in range(nb):
            copy(b, 0, sibling, me).wait_recv()
            for j, chip in enumerate(chips):
                copy(b, 4 + j, (*chip, 1 - c), me).wait_recv()
        for cp in first + passed:
            cp.wait_send()
        for cp in mine:
            cp.wait()

    any_spec = pl.BlockSpec(memory_space=pl.ANY)
    return pl.pallas_call(
        body, in_specs=[any_spec] * nb, out_specs=[any_spec] * nb,
        out_shape=[jax.ShapeDtypeStruct((N_DEV,) + b.shape, b.dtype) for b in bufs],
        scratch_shapes=[pltpu.SemaphoreType.DMA((7 * nb,)), pltpu.SemaphoreType.DMA((7 * nb,)),
                        pltpu.SemaphoreType.DMA((nb,))],
        name="all_gather_weights")(*bufs)


def _sources(groups):
    return [s for g in groups for (s, _) in g[3]]


def _layout(groups, refs):
    out, si = [], 0
    for g, (_, _, _, lst) in enumerate(groups):
        for (s, off) in lst:
            out.append((g, refs[si], off, s.shape[-2]))
            si += 1
    return out


def pair_exchange(groups):
    srcs = _sources(groups)
    ns, ng = len(srcs), len(groups)

    def body(*refs):
        outs = refs[ns:ns + ng]
        send_sems, recv_sems = refs[ns + ng:]
        x, y, c = lax.axis_index("x"), lax.axis_index("y"), lax.axis_index("c")
        sibling = (x, y, 1 - c)
        for (g, ref, off, r) in _layout(groups, refs[:ns]):
            for q in range(N_DEV // 2):
                pltpu.make_async_remote_copy(
                    src_ref=ref.at[2 * q + 1 - c], dst_ref=outs[g].at[q, pl.ds(off, r)], send_sem=send_sems.at[g],
                    recv_sem=recv_sems.at[g], device_id=sibling, device_id_type=MESH).start()
        whole = [pltpu.make_async_remote_copy(
            src_ref=outs[g], dst_ref=outs[g], send_sem=send_sems.at[g], recv_sem=recv_sems.at[g],
            device_id=sibling, device_id_type=MESH) for g in range(ng)]
        for w in whole:
            w.wait_recv()
        for w in whole:
            w.wait_send()

    any_spec = pl.BlockSpec(memory_space=pl.ANY)
    return pl.pallas_call(
        body, in_specs=[any_spec] * ns, out_specs=[any_spec] * ng,
        out_shape=[jax.ShapeDtypeStruct((N_DEV // 2, r, w), dt) for (r, w, dt, _) in groups],
        scratch_shapes=[pltpu.SemaphoreType.DMA((ng,)), pltpu.SemaphoreType.DMA((ng,))],
        name="pair_exchange")(*srcs)


def pair_sum(src, got, off, core, *, name):
    _, r, W = src.shape
    tr = _row_tile(r, off, 1024)
    o = off // tr

    def body(c_ref, s_ref, g_ref, o_ref):
        o_ref[...] = (s_ref[...].astype(F32) + g_ref[...].astype(F32)).astype(o_ref.dtype)

    return pl.pallas_call(
        body,
        grid_spec=pltpu.PrefetchScalarGridSpec(
            num_scalar_prefetch=1, grid=(N_DEV // 2, r // tr),
            in_specs=[pl.BlockSpec((None, None, tr, W), lambda q, i, c: (q, c[0], i, 0)),
                      pl.BlockSpec((None, tr, W), lambda q, i, c: (q, i + o, 0))],
            out_specs=pl.BlockSpec((None, tr, W), lambda q, i, c: (q, i, 0))),
        out_shape=jax.ShapeDtypeStruct((N_DEV // 2, r, W), src.dtype), compiler_params=_cp("parallel", "parallel"),
        name=name)(core, src.reshape(N_DEV // 2, 2, r, W), got)


def chip_exchange(groups):
    srcs = _sources(groups)
    ns, ng = len(srcs), len(groups)

    def body(*refs):
        outs = refs[ns:ns + ng]
        send_sems, recv_sems, local_sems = refs[ns + ng:]
        x, y, c = lax.axis_index("x"), lax.axis_index("y"), lax.axis_index("c")
        me = 2 * x + y
        layout = _layout(groups, refs[:ns])
        mine = [pltpu.make_async_copy(ref.at[me], outs[g].at[me, pl.ds(off, r)], local_sems.at[i])
                for i, (g, ref, off, r) in enumerate(layout)]
        for cp in mine:
            cp.start()
        slots = []
        for flip in range(1, N_DEV // 2):
            px, py = (1 - x if flip & 2 else x), (1 - y if flip & 1 else y)
            peer, pq = (px, py, c), 2 * px + py
            for (g, ref, off, r) in layout:
                k = 3 * g + flip - 1
                pltpu.make_async_remote_copy(
                    src_ref=ref.at[pq], dst_ref=outs[g].at[me, pl.ds(off, r)], send_sem=send_sems.at[k],
                    recv_sem=recv_sems.at[k], device_id=peer, device_id_type=MESH).start()
            for g in range(ng):
                k = 3 * g + flip - 1
                slots.append(pltpu.make_async_remote_copy(
                    src_ref=outs[g].at[pq], dst_ref=outs[g].at[pq], send_sem=send_sems.at[k],
                    recv_sem=recv_sems.at[k], device_id=peer, device_id_type=MESH))
        for w in slots:
            w.wait_recv()
        for w in slots:
            w.wait_send()
        for cp in mine:
            cp.wait()

    any_spec = pl.BlockSpec(memory_space=pl.ANY)
    return pl.pallas_call(
        body, in_specs=[any_spec] * ns, out_specs=[any_spec] * ng,
        out_shape=[jax.ShapeDtypeStruct((N_DEV // 2, r, w), dt) for (r, w, dt, _) in groups],
        scratch_shapes=[pltpu.SemaphoreType.DMA((3 * ng,)), pltpu.SemaphoreType.DMA((3 * ng,)),
                        pltpu.SemaphoreType.DMA((ns,))],
        name="chip_exchange")(*srcs)


def _row_tile(rows, off, target):
    for t in (1024, 512, 256, 128, 64, 32, 16, 8):
        if t <= target and rows % t == 0 and off % t == 0:
            return t
    raise ValueError((rows, off))


def adamw(recv, off, w, m, v, *, name):
    rows, W = w.shape
    nslot = recv.shape[0]
    tr = _row_tile(rows, off, 512)
    o = off // tr
    c1 = 1.0 - ADAM_B1 ** ADAM_STEP
    c2 = 1.0 - ADAM_B2 ** ADAM_STEP

    def body(r_ref, w_ref, m_ref, v_ref, g_ref, d_ref, mo_ref, vo_ref):
        g = r_ref[0].astype(F32)
        for j in range(1, nslot):
            g = g + r_ref[j].astype(F32)
        mn = ADAM_B1 * m_ref[...] + (1.0 - ADAM_B1) * g
        vn = ADAM_B2 * v_ref[...] + (1.0 - ADAM_B2) * (g * g)
        g_ref[...] = g
        mo_ref[...] = mn
        vo_ref[...] = vn
        d_ref[...] = -ADAM_LR * ((mn / c1) / (jnp.sqrt(vn / c2) + ADAM_EPS) + ADAM_WD * w_ref[...])

    spec = pl.BlockSpec((tr, W), lambda i: (i, 0))
    return pl.pallas_call(
        body, grid=(rows // tr,), in_specs=[pl.BlockSpec((nslot, tr, W), lambda i: (0, i + o, 0)), spec, spec, spec],
        out_specs=[spec] * 4, out_shape=[jax.ShapeDtypeStruct((rows, W), F32)] * 4,
        compiler_params=_cp("parallel"), name=name)(recv, w, m, v)


def join_columns(gathered, off, K, *, name):
    _, _, n = gathered.shape
    tr = _row_tile(K, off, 512)
    o = off // tr

    def body(i_ref, o_ref):
        for d in range(N_DEV):
            o_ref[:, d * n:(d + 1) * n] = i_ref[d]

    return pl.pallas_call(
        body, grid=(K // tr,), in_specs=[pl.BlockSpec((N_DEV, tr, n), lambda i: (0, i + o, 0))],
        out_specs=pl.BlockSpec((tr, N_DEV * n), lambda i: (i, 0)),
        out_shape=jax.ShapeDtypeStruct((K, N_DEV * n), gathered.dtype), compiler_params=_cp("parallel"),
        name=name)(gathered)


def split_columns(full, *, name):
    K, N = full.shape
    n = N // N_DEV
    tr = _row_tile(K, 0, 512)

    def body(i_ref, o_ref):
        for d in range(N_DEV):
            o_ref[d] = i_ref[:, d * n:(d + 1) * n].astype(o_ref.dtype)

    return pl.pallas_call(
        body, grid=(K // tr,), in_specs=[pl.BlockSpec((tr, N), lambda i: (i, 0))],
        out_specs=pl.BlockSpec((N_DEV, tr, n), lambda i: (0, i, 0)),
        out_shape=jax.ShapeDtypeStruct((N_DEV, K, n), BF16), compiler_params=_cp("parallel"), name=name)(full)


def _pack(arrs, dtype, row_mult):
    flat = jnp.concatenate([a.reshape(-1).astype(dtype) for a in arrs])
    rows = -(-flat.shape[0] // PACK_W)
    rows = -(-rows // row_mult) * row_mult
    return jnp.pad(flat, (0, rows * PACK_W - flat.shape[0])).reshape(rows, PACK_W)


def _pack_dev(arrs, dtype, row_mult):
    flat = jnp.concatenate([a.reshape(N_DEV, -1).astype(dtype) for a in arrs], axis=1)
    rows = -(-flat.shape[1] // PACK_W)
    rows = -(-rows // row_mult) * row_mult
    return jnp.pad(flat, ((0, 0), (0, rows * PACK_W - flat.shape[1]))).reshape(N_DEV, rows, PACK_W)


def _unpack(buf, shapes):
    lead = buf.shape[:-2]
    flat = buf.reshape(lead + (-1,))
    outs, off = [], 0
    for s in shapes:
        n = math.prod(s)
        outs.append(flat[..., off:off + n].reshape(lead + tuple(s)))
        off += n
    return outs


def _join(g, axis):
    g = jnp.moveaxis(g, 0, axis)
    return g.reshape(g.shape[:axis] + (g.shape[axis] * g.shape[axis + 1],) + g.shape[axis + 2:])


def _split(full, axis):
    s = full.shape
    g = full.reshape(s[:axis] + (N_DEV, s[axis] // N_DEV) + s[axis + 1:])
    return jnp.moveaxis(g, axis, 0)


def kernel(x, p, a_pw1_w, a_pw1_b, a_dw_w, a_dw_b, a_ln_g, a_ln_b, a_pw2_w, a_pw2_b, b_wq, kv_wk, kv_wv, b_wo, ln_mix_g, ln_mix_b, ffn_w_up, ffn_w_gate, ffn_conv_w, ffn_conv_b, ffn_w_down, ple_w_gate, ple_w_proj, ln_ffn_g, ln_ffn_b, loss_target, m_a_pw1_w, m_a_pw1_b, m_a_dw_w, m_a_dw_b, m_a_ln_g, m_a_ln_b, m_a_pw2_w, m_a_pw2_b, m_b_wq, m_kv_wk, m_kv_wv, m_b_wo, m_ln_mix_g, m_ln_mix_b, m_ffn_w_up, m_ffn_w_gate, m_ffn_conv_w, m_ffn_conv_b, m_ffn_w_down, m_ple_w_gate, m_ple_w_proj, m_ln_ffn_g, m_ln_ffn_b, v_a_pw1_w, v_a_pw1_b, v_a_dw_w, v_a_dw_b, v_a_ln_g, v_a_ln_b, v_a_pw2_w, v_a_pw2_b, v_b_wq, v_kv_wk, v_kv_wv, v_b_wo, v_ln_mix_g, v_ln_mix_b, v_ffn_w_up, v_ffn_w_gate, v_ffn_conv_w, v_ffn_conv_b, v_ffn_w_down, v_ple_w_gate, v_ple_w_proj, v_ln_ffn_g, v_ln_ffn_b):
    local = dict(a_pw1_w=a_pw1_w, a_pw1_b=a_pw1_b, a_dw_w=a_dw_w, a_dw_b=a_dw_b, a_ln_g=a_ln_g, a_ln_b=a_ln_b, a_pw2_w=a_pw2_w, a_pw2_b=a_pw2_b, b_wq=b_wq, kv_wk=kv_wk, kv_wv=kv_wv, b_wo=b_wo, ln_mix_g=ln_mix_g, ln_mix_b=ln_mix_b, ffn_w_up=ffn_w_up, ffn_w_gate=ffn_w_gate, ffn_conv_w=ffn_conv_w, ffn_conv_b=ffn_conv_b, ffn_w_down=ffn_w_down, ple_w_gate=ple_w_gate, ple_w_proj=ple_w_proj, ln_ffn_g=ln_ffn_g, ln_ffn_b=ln_ffn_b)
    mom1 = dict(a_pw1_w=m_a_pw1_w, a_pw1_b=m_a_pw1_b, a_dw_w=m_a_dw_w, a_dw_b=m_a_dw_b, a_ln_g=m_a_ln_g, a_ln_b=m_a_ln_b, a_pw2_w=m_a_pw2_w, a_pw2_b=m_a_pw2_b, b_wq=m_b_wq, kv_wk=m_kv_wk, kv_wv=m_kv_wv, b_wo=m_b_wo, ln_mix_g=m_ln_mix_g, ln_mix_b=m_ln_mix_b, ffn_w_up=m_ffn_w_up, ffn_w_gate=m_ffn_w_gate, ffn_conv_w=m_ffn_conv_w, ffn_conv_b=m_ffn_conv_b, ffn_w_down=m_ffn_w_down, ple_w_gate=m_ple_w_gate, ple_w_proj=m_ple_w_proj, ln_ffn_g=m_ln_ffn_g, ln_ffn_b=m_ln_ffn_b)
    mom2 = dict(a_pw1_w=v_a_pw1_w, a_pw1_b=v_a_pw1_b, a_dw_w=v_a_dw_w, a_dw_b=v_a_dw_b, a_ln_g=v_a_ln_g, a_ln_b=v_a_ln_b, a_pw2_w=v_a_pw2_w, a_pw2_b=v_a_pw2_b, b_wq=v_b_wq, kv_wk=v_kv_wk, kv_wv=v_kv_wv, b_wo=v_b_wo, ln_mix_g=v_ln_mix_g, ln_mix_b=v_ln_mix_b, ffn_w_up=v_ffn_w_up, ffn_w_gate=v_ffn_w_gate, ffn_conv_w=v_ffn_conv_w, ffn_conv_b=v_ffn_conv_b, ffn_w_down=v_ffn_w_down, ple_w_gate=v_ple_w_gate, ple_w_proj=v_ple_w_proj, ln_ffn_g=v_ln_ffn_g, ln_ffn_b=v_ln_ffn_b)
    small_names = [n for n, _ in SMALL]
    small_shapes = [local[n].shape for n in small_names]
    repl_shapes = [local[n].shape for n in REPL]

    widths = sorted({local[n].shape[-1] for n, _ in BIG}, reverse=True)
    groups = {w: [n for n, _ in BIG if local[n].shape[-1] == w] for w in widths}
    offset, rows_of = {}, {}
    for w, names in groups.items():
        off = 0
        for n in names:
            offset[n], rows_of[n] = off, math.prod(local[n].shape[:-1])
            off += rows_of[n]
    sends = [jnp.concatenate([local[n].reshape(-1, w).astype(BF16) for n in names]) for w, names in groups.items()]
    gathered = all_gather(sends + [_pack([local[n] for n in small_names], F32, SUBLANES)])
    gath = dict(zip(widths, gathered[:-1]))
    W = {}
    for n, ax in BIG:
        w = local[n].shape[-1]
        nl = local[n].shape[0] if local[n].ndim == 3 else 1
        per = rows_of[n] // nl
        if ax == local[n].ndim - 1:
            W[n] = [join_columns(gath[w], offset[n] + l * per, per, name=f"join_{n}_{l}") for l in range(nl)]
        else:
            W[n] = [gath[w][:, offset[n] + l * per:offset[n] + (l + 1) * per].reshape(N_DEV * per, w) for l in range(nl)]
    for n in ("kv_wk", "kv_wv"):
        W[n] = W[n][0]
    W.update({n: _join(g, ax) for (n, ax), g in zip(SMALL, _unpack(gathered[-1], small_shapes))})
    W.update({n: local[n] for n in REPL})

    xs = x[0]
    S, D = xs.shape
    x_in, r1s, x1s, r2s, us, gps, gs, hhs, pgls, pps = [], [], [], [], [], [], [], [], [], []
    h1s, h2s, h3s, h5s, qs, os_, tots = {}, {}, {}, {}, {}, {}, {}
    kk = vv = None
    for i in range(DEPTH):
        x_in.append(xs)
        if i < N_A:
            h1 = mm(xs, W["a_pw1_w"][i], "nn", bias=W["a_pw1_b"][i][None], out_dtype=BF16, name=f"pw1_{i}")
            h2 = glu_fwd(h1, name=f"glu_{i}")
            h3, h5 = conv_ln_silu_fwd(h2, W["a_dw_w"][i], W["a_dw_b"][i][None], W["a_ln_g"][i][None],
                                      W["a_ln_b"][i][None], name=f"dwconv_{i}")
            mix = mm(h5, W["a_pw2_w"][i], "nn", bias=W["a_pw2_b"][i][None], name=f"pw2_{i}")
            h1s[i], h2s[i], h3s[i], h5s[i] = h1, h2, h3, h5
        else:
            j = i - N_A
            if kk is None:
                kk, vv = mm(xs, W["kv_wk"], "nn", out_dtype=BF16, also=W["kv_wv"], name="proj_kv")
            q = mm(xs, W["b_wq"][j], "nn", out_dtype=BF16, name=f"proj_q_{i}")
            o, tot, seen = attn_fwd(q, kk, vv, name=f"attn_{i}")
            mix = mm(o, W["b_wo"][j], "nn", name=f"proj_o_{i}")
            qs[i], os_[i], tots[i] = q, o, (tot, seen)
        r1, x1 = res_ln(xs, mix, W["ln_mix_g"][i][None], W["ln_mix_b"][i][None], name=f"ln_mix_{i}")
        u, gp = mm(x1, W["ffn_w_up"][i], "nn", out_dtype=BF16, also=W["ffn_w_gate"][i], name=f"ffn_up_gate_{i}")
        g, hh = conv_act_fwd(gp, u, W["ffn_conv_w"][i], W["ffn_conv_b"][i][None], name=f"ffn_conv_{i}")
        f = mm(hh, W["ffn_w_down"][i], "nn", name=f"ffn_down_{i}")
        pgl = mm(x1, W["ple_w_gate"][i], "nn", out_dtype=BF16, name=f"ple_gate_{i}")
        pp = mm(p[i, 0], W["ple_w_proj"][i], "nn", out_dtype=BF16, name=f"ple_proj_{i}")
        r2, xs = res_ln(x1, f, W["ln_ffn_g"][i][None], W["ln_ffn_b"][i][None], ple=(pgl, pp), name=f"ln_ffn_{i}")
        for lst, val in ((r1s, r1), (x1s, x1), (r2s, r2), (us, u), (gps, gp), (gs, g), (hhs, hh), (pgls, pgl), (pps, pp)):
            lst.append(val)

    dx, loss_part = loss_grad(xs, loss_target[0], name="loss")
    G = {n: [None] * local[n].shape[0] for n in WEIGHTS if n not in ("kv_wk", "kv_wv")}
    dk = dv = None
    for i in reversed(range(DEPTH)):
        x1 = x1s[i]
        dr2, dpp, dpgl, G["ln_ffn_g"][i], G["ln_ffn_b"][i] = ln_ple_bwd(r2s[i], W["ln_ffn_g"][i][None], dx, pgls[i], pps[i],
                                                                      name=f"ln_ffn_bwd_{i}")
        dhh = mm(dr2, W["ffn_w_down"][i], "nt", out_dtype=BF16, name=f"ffn_down_dx_{i}")
        G["ffn_w_down"][i] = mm(hhs[i], dr2, "tn", out_dtype=BF16, name=f"ffn_down_dw_{i}")
        G["ple_w_proj"][i] = mm(p[i, 0], dpp, "tn", out_dtype=BF16, name=f"ple_proj_dw_{i}")
        G["ple_w_gate"][i] = mm(x1, dpgl, "tn", out_dtype=BF16, name=f"ple_gate_dw_{i}")
        du, dgp, G["ffn_conv_w"][i], G["ffn_conv_b"][i] = ffn_gate_bwd(dhh, us[i], gs[i], gps[i], W["ffn_conv_w"][i],
                                                                       name=f"ffn_gate_bwd_{i}")
        G["ffn_w_up"][i], G["ffn_w_gate"][i] = mm(x1, du, "tn", out_dtype=BF16, also=dgp, name=f"ffn_up_gate_dw_{i}")
        dx1 = mm(du, W["ffn_w_up"][i], "nt", add=dr2, add_scale=DN_ALPHA, plus=(dgp, W["ffn_w_gate"][i]),
                 name=f"ffn_up_gate_dx_{i}")
        dx1 = mm(dpgl, W["ple_w_gate"][i], "nt", add=dx1, name=f"ple_gate_dx_{i}")
        dr1, G["ln_mix_g"][i], G["ln_mix_b"][i], dr1_sum = ln_bwd(r1s[i], W["ln_mix_g"][i][None], dx1, name=f"ln_mix_bwd_{i}")
        if i < N_A:
            G["a_pw2_w"][i] = mm(h5s[i], dr1, "tn", out_dtype=BF16, name=f"pw2_dw_{i}")
            G["a_pw2_b"][i] = dr1_sum
            dh5 = mm(dr1, W["a_pw2_w"][i], "nt", name=f"pw2_dx_{i}")
            dh3, G["a_ln_g"][i], G["a_ln_b"][i] = ln_silu_bwd(h3s[i], W["a_ln_g"][i][None], W["a_ln_b"][i][None], dh5,
                                                             name=f"dwconv_ln_bwd_{i}")
            dh2 = conv_bwd_x(dh3, W["a_dw_w"][i], out_dtype=F32, name=f"dwconv_dx_{i}")
            G["a_dw_w"][i], G["a_dw_b"][i] = conv_bwd_w(h2s[i], dh3, CONV_W, name=f"dwconv_dw_{i}")
            dh1, G["a_pw1_b"][i] = glu_bwd(h1s[i], dh2, name=f"glu_bwd_{i}")
            G["a_pw1_w"][i] = mm(x_in[i], dh1, "tn", out_dtype=BF16, name=f"pw1_dw_{i}")
            dx = mm(dh1, W["a_pw1_w"][i], "nt", add=dr1, add_scale=DN_ALPHA, name=f"pw1_dx_{i}")
        else:
            j = i - N_A
            G["b_wo"][j] = mm(os_[i], dr1, "tn", out_dtype=BF16, name=f"proj_o_dw_{i}")
            do = mm(dr1, W["b_wo"][j], "nt", out_dtype=BF16, name=f"proj_o_dx_{i}")
            dq, dk, dv = attn_bwd(qs[i], kk, vv, *tots[i], do, dk, dv, name=f"attn_bwd_{i}")
            G["b_wq"][j] = mm(x_in[i], dq, "tn", out_dtype=BF16, name=f"proj_q_dw_{i}")
            dx = mm(dq, W["b_wq"][j], "nt", add=dr1, add_scale=DN_ALPHA, name=f"proj_q_dx_{i}")
            if j == 0:
                G["kv_wk"], G["kv_wv"] = mm(x_in[i], dk, "tn", out_dtype=BF16, also=dv, name="proj_kv_dw")
                dx = mm(dk, W["kv_wk"], "nt", add=dx, plus=(dv, W["kv_wv"]), name="proj_kv_dx")
    grad_x = dx[None]
    shard_axis = dict(BIG + SMALL)
    for n in small_names + list(REPL):
        full = list(local[n].shape)
        if n in shard_axis:
            full[shard_axis[n]] *= N_DEV
        G[n] = jnp.stack(G[n]).reshape(full)

    n_small = sum(math.prod(s) for s in small_shapes)
    n_repl = sum(math.prod(s) for s in repl_shapes)
    repl_flat = jnp.concatenate([G[n].reshape(-1) for n in REPL] + [loss_part.reshape(-1)[:1]])
    send_small = _pack_dev([_split(G[n], ax) for n, ax in SMALL] + [jnp.broadcast_to(repl_flat, (N_DEV, n_repl + 1))],
                           F32, SUBLANES)
    ex_groups = []
    for w, names in groups.items():
        lst = []
        for n in names:
            layers = G[n] if isinstance(G[n], list) else [G[n]]
            per = rows_of[n] // len(layers)
            for l, g in enumerate(layers):
                if shard_axis[n] == local[n].ndim - 1:
                    src = split_columns(g, name=f"split_{n}_{l}")
                else:
                    src = g.reshape(N_DEV, per, w)
                lst.append((src, offset[n] + l * per))
        ex_groups.append((sum(rows_of[n] for n in names), w, BF16, lst))
    ex_groups.append((send_small.shape[1], PACK_W, F32, [(send_small, 0)]))
    gots = pair_exchange(ex_groups)
    core = lax.axis_index("c").astype(jnp.int32).reshape(1)
    sum_groups = [(rows, w, dt, [(pair_sum(src, got, off, core, name=f"pair_sum_{gi}_{si}"), off) for si, (src, off) in enumerate(lst)])
                  for gi, ((rows, w, dt, lst), got) in enumerate(zip(ex_groups, gots))]
    recvs = chip_exchange(sum_groups)
    recv = dict(zip(widths, recvs[:-1]))
    recv_small = recvs[-1]

    out = {}
    for n, _ in BIG:
        w = local[n].shape[-1]
        res = adamw(recv[w], offset[n], local[n].reshape(-1, w), mom1[n].reshape(-1, w), mom2[n].reshape(-1, w),
                    name=f"adamw_{n}")
        out[n] = [r.reshape(local[n].shape) for r in res]

    def state(d):
        small = _pack([d[n] for n in small_names] + [d[n] for n in REPL], F32, SUBLANES)
        return jnp.pad(small, ((0, recv_small.shape[1] - small.shape[0]), (0, 0)))

    out_small = adamw(recv_small, 0, state(local), state(mom1), state(mom2), name="adamw_vectors")
    loss = out_small[0].reshape(-1)[n_small + n_repl]
    vecs = [dict(zip(small_names + list(REPL), _unpack(o, small_shapes + repl_shapes))) for o in out_small]
    per_kind = [[out[n][kind] if n in out else vecs[kind][n] for n in WEIGHTS] for kind in range(4)]
    grads, deltas, new_m, new_v = per_kind
    return (loss, grad_x, *grads, *deltas, *new_m, *new_v)
```

```python
import functools
import math

import jax
import jax.numpy as jnp
from jax import lax
from jax.experimental import pallas as pl
from jax.experimental.pallas import tpu as pltpu

F32 = jnp.float32
BF16 = jnp.bfloat16
MESH = pl.DeviceIdType.MESH

N_DEV = 8
DEPTH = 4
N_A = 2
HEAD_DIM = 64
Q_BLOCK = 128
CONV_W = 31
FFN_CONV_W = 3
LN_EPS = 1e-5
DN_ALPHA = (2.0 * DEPTH) ** 0.25
ADAM_LR = 0.001
ADAM_B1 = 0.9
ADAM_B2 = 0.999
ADAM_EPS = 1e-08
ADAM_WD = 0.01
ADAM_STEP = 10

LANES = 128
SUBLANES = 8
PACK_W = 1024
VMEM_LIMIT = 56 * 1024 * 1024

BIG = (("a_pw1_w", 2), ("a_pw2_w", 1), ("b_wq", 1), ("kv_wk", 0), ("kv_wv", 0), ("b_wo", 1),
       ("ffn_w_up", 2), ("ffn_w_gate", 2), ("ffn_w_down", 1), ("ple_w_gate", 1), ("ple_w_proj", 2))
SMALL = (("a_pw1_b", 1), ("a_dw_w", 2), ("a_dw_b", 1), ("a_ln_g", 1), ("a_ln_b", 1), ("a_pw2_b", 1),
         ("ffn_conv_w", 2))
REPL = ("ln_mix_g", "ln_mix_b", "ffn_conv_b", "ln_ffn_g", "ln_ffn_b")
WEIGHTS = ("a_pw1_w", "a_pw1_b", "a_dw_w", "a_dw_b", "a_ln_g", "a_ln_b", "a_pw2_w", "a_pw2_b", "b_wq", "kv_wk",
           "kv_wv", "b_wo", "ln_mix_g", "ln_mix_b", "ffn_w_up", "ffn_w_gate", "ffn_conv_w", "ffn_conv_b",
           "ffn_w_down", "ple_w_gate", "ple_w_proj", "ln_ffn_g", "ln_ffn_b")


def _cp(*sem):
    return pltpu.CompilerParams(dimension_semantics=sem, vmem_limit_bytes=VMEM_LIMIT)


def _pick(dim, target, align=LANES):
    if dim <= target:
        return dim
    t = (target // align) * align
    while t >= align:
        if dim % t == 0:
            return t
        t -= align
    return dim


MM_ROWS = 1024
MM_ROWS_TN = 1536
MM_COLS = 1536
MM_DEPTH = 2816
MM_DEPTH_TN = 1536

_DOT_DIMS = {"nn": (((1,), (0,)), ((), ())), "nt": (((1,), (1,)), ((), ())), "tn": (((0,), (0,)), ((), ()))}


def mm(a, b, mode, *, bias=None, add=None, add_scale=1.0, out_dtype=F32, also=None, plus=None, name):
    if mode == "tn":
        K, M = a.shape
    else:
        M, K = a.shape
    N = b.shape[0] if mode == "nt" else b.shape[1]
    tm = _pick(M, MM_ROWS_TN if mode == "tn" else MM_ROWS if plus is None else MM_ROWS // 2)
    tn = _pick(N, MM_COLS)
    tk = _pick(K, MM_DEPTH_TN if mode == "tn" else MM_DEPTH)
    nk = K // tk
    dims = _DOT_DIMS[mode]
    n_out = 1 if also is None else 2

    def body(*refs):
        refs = list(refs)
        a_ref, b_ref = refs.pop(0), refs.pop(0)
        b2_ref = refs.pop(0) if also is not None else None
        a3_ref, b3_ref = (refs.pop(0), refs.pop(0)) if plus is not None else (None, None)
        bias_ref = refs.pop(0) if bias is not None else None
        add_ref = refs.pop(0) if add is not None else None
        o_refs, acc_refs = refs[:n_out], refs[n_out:]
        k = pl.program_id(2)

        @pl.when(k == 0)
        def _():
            for acc_ref in acc_refs:
                acc_ref[...] = jnp.zeros_like(acc_ref)

        a_v = a_ref[...].astype(BF16)
        acc_refs[0][...] += lax.dot_general(a_v, b_ref[...].astype(BF16), dims, preferred_element_type=F32)
        if b2_ref is not None:
            acc_refs[1][...] += lax.dot_general(a_v, b2_ref[...].astype(BF16), dims, preferred_element_type=F32)
        if a3_ref is not None:
            acc_refs[0][...] += lax.dot_general(a3_ref[...].astype(BF16), b3_ref[...].astype(BF16), dims,
                                                preferred_element_type=F32)

        @pl.when(k == nk - 1)
        def _():
            for o_ref, acc_ref in zip(o_refs, acc_refs):
                r = acc_ref[...]
                if bias_ref is not None:
                    r = r + bias_ref[...]
                if add_ref is not None:
                    r = r + add_scale * add_ref[...].astype(F32)
                o_ref[...] = r.astype(o_ref.dtype)

    a_spec = pl.BlockSpec((tk, tm), lambda j, i, k: (k, i)) if mode == "tn" else pl.BlockSpec((tm, tk), lambda j, i, k: (i, k))
    b_spec = pl.BlockSpec((tn, tk), lambda j, i, k: (j, k)) if mode == "nt" else pl.BlockSpec((tk, tn), lambda j, i, k: (k, j))
    in_specs, args = [a_spec, b_spec], [a, b]
    if also is not None:
        in_specs.append(b_spec)
        args.append(also)
    if plus is not None:
        in_specs += [a_spec, b_spec]
        args += list(plus)
    if bias is not None:
        in_specs.append(pl.BlockSpec((1, tn), lambda j, i, k: (0, j)))
        args.append(bias)
    if add is not None:
        in_specs.append(pl.BlockSpec((tm, tn), lambda j, i, k: (i, j)))
        args.append(add)
    o_spec = pl.BlockSpec((tm, tn), lambda j, i, k: (i, j))
    outs = pl.pallas_call(
        body, grid=(N // tn, M // tm, nk), in_specs=in_specs,
        out_specs=[o_spec] * n_out, out_shape=[jax.ShapeDtypeStruct((M, N), out_dtype)] * n_out,
        scratch_shapes=[pltpu.VMEM((tm, tn), F32)] * n_out,
        compiler_params=_cp("parallel", "parallel", "arbitrary"), name=name)(*args)
    return outs[0] if also is None else tuple(outs)


ROW_BLOCK = 512


def _rows(body, *, n_rows, tm, row_ins, full_ins=(), row_outs=(), acc_outs=(), scratch=(), reverse=False, name):
    n = n_rows // tm

    def rmap(i):
        return (n - 1 - i, 0) if reverse else (i, 0)

    in_specs = [pl.BlockSpec((tm, a.shape[1]), rmap) for a in row_ins]
    in_specs += [pl.BlockSpec(a.shape, lambda i, nd=a.ndim: (0,) * nd) for a in full_ins]
    out_shape = [jax.ShapeDtypeStruct((n_rows, w), dt) for (w, dt) in row_outs]
    out_shape += [jax.ShapeDtypeStruct(s, dt) for (s, dt) in acc_outs]
    out_specs = [pl.BlockSpec((tm, w), rmap) for (w, dt) in row_outs]
    out_specs += [pl.BlockSpec(s, lambda i, nd=len(s): (0,) * nd) for (s, dt) in acc_outs]
    return pl.pallas_call(
        functools.partial(body, n), grid=(n,), in_specs=in_specs, out_specs=out_specs, out_shape=out_shape,
        scratch_shapes=list(scratch), compiler_params=_cp("arbitrary"), name=name)(*row_ins, *full_ins)


def _sigmoid(x):
    return 1.0 / (1.0 + jnp.exp(-x))


def _ln_hat(r):
    mu = jnp.mean(r, axis=-1, keepdims=True)
    xc = r - mu
    var = jnp.mean(xc * xc, axis=-1, keepdims=True)
    rstd = lax.rsqrt(var + LN_EPS)
    return xc * rstd, rstd


def _ln_back(xhat, rstd, g, dy):
    dxh = dy * g
    m1 = jnp.mean(dxh, axis=-1, keepdims=True)
    m2 = jnp.mean(dxh * xhat, axis=-1, keepdims=True)
    return rstd * (dxh - m1 - xhat * m2)


def _colsum(x):
    return jnp.sum(x, axis=0, keepdims=True)


def _acc(i, ref, val):
    @pl.when(i == 0)
    def _():
        ref[...] = val

    @pl.when(i > 0)
    def _():
        ref[...] += val


def res_ln(x, mix, g, b, *, ple=None, name):
    S, D = x.shape

    def body(n, *refs):
        if ple is None:
            x_ref, m_ref, g_ref, b_ref, r_ref, y_ref = refs
            r = DN_ALPHA * x_ref[...] + m_ref[...]
        else:
            x_ref, m_ref, pgl_ref, pp_ref, g_ref, b_ref, r_ref, y_ref = refs
            r = DN_ALPHA * x_ref[...] + m_ref[...] + _sigmoid(pgl_ref[...].astype(F32)) * pp_ref[...].astype(F32)
        xhat, _ = _ln_hat(r)
        r_ref[...] = r
        y_ref[...] = xhat * g_ref[...] + b_ref[...]

    row_ins = [x, mix] + ([] if ple is None else list(ple))
    return _rows(body, n_rows=S, tm=_pick(S, ROW_BLOCK, SUBLANES), row_ins=row_ins, full_ins=[g, b],
                 row_outs=[(D, F32), (D, F32)], name=name)


def ln_bwd(r, g, dy, *, name):
    S, D = r.shape

    def body(n, r_ref, dy_ref, g_ref, dr_ref, dg_ref, db_ref, ds_ref):
        i = pl.program_id(0)
        xhat, rstd = _ln_hat(r_ref[...])
        dy_v = dy_ref[...]
        dr = _ln_back(xhat, rstd, g_ref[...], dy_v)
        dr_ref[...] = dr
        _acc(i, dg_ref, _colsum(dy_v * xhat))
        _acc(i, db_ref, _colsum(dy_v))
        _acc(i, ds_ref, _colsum(dr))

    return _rows(body, n_rows=S, tm=_pick(S, ROW_BLOCK, SUBLANES), row_ins=[r, dy], full_ins=[g],
                 row_outs=[(D, F32)], acc_outs=[((1, D), F32)] * 3, name=name)


def glu_fwd(h1, *, name):
    S, D2 = h1.shape
    D = D2 // 2

    def body(n, h_ref, o_ref):
        o_ref[...] = h_ref[:, :D].astype(F32) * _sigmoid(h_ref[:, D:].astype(F32))

    return _rows(body, n_rows=S, tm=_pick(S, ROW_BLOCK, SUBLANES), row_ins=[h1], row_outs=[(D, F32)], name=name)[0]


def glu_bwd(h1, dh2, *, name):
    S, D2 = h1.shape
    D = D2 // 2

    def body(n, h_ref, d_ref, o_ref, s_ref):
        i = pl.program_id(0)
        a, sg, d = h_ref[:, :D].astype(F32), _sigmoid(h_ref[:, D:].astype(F32)), d_ref[...]
        da = d * sg
        dg = d * a * sg * (1.0 - sg)
        o_ref[:, :D] = da.astype(o_ref.dtype)
        o_ref[:, D:] = dg.astype(o_ref.dtype)
        _acc(i, s_ref, jnp.concatenate([_colsum(da), _colsum(dg)], axis=1))

    return _rows(body, n_rows=S, tm=_pick(S, ROW_BLOCK, SUBLANES), row_ins=[h1, dh2], row_outs=[(D2, BF16)],
                 acc_outs=[((1, D2), F32)], name=name)


CONV_ROWS = 32
CONV_LANES = 256


def _halo(k):
    return -(-(k - 1) // SUBLANES) * SUBLANES


def _phases(offs):
    return sorted({o % SUBLANES for o in offs} - {0})


def _shift_scratch(offs, n_rows, width):
    return pltpu.VMEM((max(len(_phases(offs)), 1), n_rows, width), F32)


def _make_shifted(buf_ref, sh_ref, offs):
    n = buf_ref.shape[0] - SUBLANES
    for p, b in enumerate(_phases(offs)):
        sh_ref[p, pl.ds(0, n), :] = buf_ref[pl.ds(b, n), :]


def _tap(buf_ref, sh_ref, offs, k, rc, rows, lc, lw):
    b = offs[k] % SUBLANES
    src = buf_ref if b == 0 else sh_ref.at[_phases(offs).index(b)]
    return src[pl.ds(offs[k] - b + rc, rows), pl.ds(lc, lw)]


def _conv_taps(buf_ref, sh_ref, w_ref, offs, tm, width, emit):
    _make_shifted(buf_ref, sh_ref, offs)
    rows = min(CONV_ROWS, tm)
    for lc in range(0, width, CONV_LANES):
        lw = min(CONV_LANES, width - lc)
        for rc in range(0, tm, rows):
            acc = None
            for k in range(len(offs)):
                t = _tap(buf_ref, sh_ref, offs, k, rc, rows, lc, lw) * w_ref[pl.ds(k, 1), pl.ds(lc, lw)]
                acc = t if acc is None else acc + t
            emit(rc, lc, lw, rows, acc)


def _fill_causal(i, buf_ref, x_ref, halo, tm):
    @pl.when(i == 0)
    def _():
        buf_ref[pl.ds(0, halo), :] = jnp.zeros((halo, buf_ref.shape[1]), F32)

    @pl.when(i > 0)
    def _():
        buf_ref[pl.ds(0, halo), :] = buf_ref[pl.ds(tm, halo), :]

    buf_ref[pl.ds(halo, tm), :] = x_ref[...].astype(F32)


def conv_ln_silu_fwd(x, w, b, g, beta, *, name):
    S, C = x.shape
    K = w.shape[0]
    halo = _halo(K)
    tm = _pick(S, 256, SUBLANES)
    offs = [halo - (K - 1) + k for k in range(K)]

    def body(n, x_ref, w_ref, b_ref, g_ref, beta_ref, h3_ref, h5_ref, buf_ref, sh_ref):
        i = pl.program_id(0)
        _fill_causal(i, buf_ref, x_ref, halo, tm)

        def emit(rc, lc, lw, rows, acc):
            h3_ref[pl.ds(rc, rows), pl.ds(lc, lw)] = acc + b_ref[:, pl.ds(lc, lw)]

        _conv_taps(buf_ref, sh_ref, w_ref, offs, tm, C, emit)
        xhat, _ = _ln_hat(h3_ref[...])
        h4 = xhat * g_ref[...] + beta_ref[...]
        h5_ref[...] = (h4 * _sigmoid(h4)).astype(h5_ref.dtype)

    return _rows(body, n_rows=S, tm=tm, row_ins=[x], full_ins=[w, b, g, beta], row_outs=[(C, F32), (C, BF16)],
                 scratch=[pltpu.VMEM((tm + halo, C), F32), _shift_scratch(offs, tm + halo, C)], name=name)


def conv_act_fwd(gp, u, w, b, *, name):
    S, C = gp.shape
    K = w.shape[0]
    halo = _halo(K)
    tm = _pick(S, 256, SUBLANES)
    offs = [halo - (K - 1) + k for k in range(K)]

    def body(n, x_ref, u_ref, w_ref, b_ref, g_ref, hh_ref, buf_ref, sh_ref):
        i = pl.program_id(0)
        _fill_causal(i, buf_ref, x_ref, halo, tm)

        def emit(rc, lc, lw, rows, acc):
            gv = acc + b_ref[:, pl.ds(lc, lw)]
            g_ref[pl.ds(rc, rows), pl.ds(lc, lw)] = gv.astype(g_ref.dtype)
            hh_ref[pl.ds(rc, rows), pl.ds(lc, lw)] = (gv * _sigmoid(gv) * u_ref[pl.ds(rc, rows), pl.ds(lc, lw)].astype(F32)).astype(hh_ref.dtype)

        _conv_taps(buf_ref, sh_ref, w_ref, offs, tm, C, emit)

    return _rows(body, n_rows=S, tm=tm, row_ins=[gp, u], full_ins=[w, b], row_outs=[(C, BF16), (C, BF16)],
                 scratch=[pltpu.VMEM((tm + halo, C), F32), _shift_scratch(offs, tm + halo, C)], name=name)


def conv_bwd_x(dy, w, *, out_dtype, name):
    S, C = dy.shape
    K = w.shape[0]
    halo = _halo(K)
    tm = _pick(S, 256, SUBLANES)
    offs = [K - 1 - k for k in range(K)]

    def body(n, dy_ref, w_ref, dx_ref, buf_ref, sh_ref):
        i = pl.program_id(0)

        @pl.when(i == 0)
        def _():
            buf_ref[pl.ds(tm, halo), :] = jnp.zeros((halo, C), F32)

        @pl.when(i > 0)
        def _():
            buf_ref[pl.ds(tm, halo), :] = buf_ref[pl.ds(0, halo), :]

        buf_ref[pl.ds(0, tm), :] = dy_ref[...].astype(F32)

        def emit(rc, lc, lw, rows, acc):
            dx_ref[pl.ds(rc, rows), pl.ds(lc, lw)] = acc.astype(dx_ref.dtype)

        _conv_taps(buf_ref, sh_ref, w_ref, offs, tm, C, emit)

    return _rows(body, n_rows=S, tm=tm, row_ins=[dy], full_ins=[w], row_outs=[(C, out_dtype)],
                 scratch=[pltpu.VMEM((tm + halo, C), F32), _shift_scratch(offs, tm + halo, C)], reverse=True, name=name)[0]


def conv_bwd_w(x, dy, K, *, name):
    S, C = x.shape
    halo = _halo(K)
    tm = _pick(S, 256, SUBLANES)
    offs = [halo - (K - 1) + k for k in range(K)]
    rows = min(CONV_ROWS, tm)

    def body(n, x_ref, dy_ref, dw_ref, db_ref, buf_ref, acc_ref, sh_ref):
        i = pl.program_id(0)
        _fill_causal(i, buf_ref, x_ref, halo, tm)
        _make_shifted(buf_ref, sh_ref, offs)

        @pl.when(i == 0)
        def _():
            acc_ref[...] = jnp.zeros_like(acc_ref)

        for lc in range(0, C, CONV_LANES):
            lw = min(CONV_LANES, C - lc)
            for k in range(K):
                s = None
                for rc in range(0, tm, rows):
                    t = dy_ref[pl.ds(rc, rows), pl.ds(lc, lw)].astype(F32) * _tap(buf_ref, sh_ref, offs, k, rc, rows, lc, lw)
                    s = t if s is None else s + t
                s8 = s[0:SUBLANES]
                for q in range(1, rows // SUBLANES):
                    s8 = s8 + s[q * SUBLANES:(q + 1) * SUBLANES]
                acc_ref[pl.ds(k * SUBLANES, SUBLANES), pl.ds(lc, lw)] += s8
        _acc(i, db_ref, _colsum(dy_ref[...].astype(F32)))

        @pl.when(i == n - 1)
        def _():
            for k in range(K):
                dw_ref[pl.ds(k, 1), :] = _colsum(acc_ref[pl.ds(k * SUBLANES, SUBLANES), :])

    return _rows(body, n_rows=S, tm=tm, row_ins=[x, dy], acc_outs=[((K, C), F32), ((1, C), F32)],
                 scratch=[pltpu.VMEM((tm + halo, C), F32), pltpu.VMEM((K * SUBLANES, C), F32),
                          _shift_scratch(offs, tm + halo, C)], name=name)


def ln_silu_bwd(h3, g, beta, dh5, *, name):
    S, C = h3.shape

    def body(n, h_ref, d_ref, g_ref, beta_ref, dh_ref, dg_ref, db_ref):
        i = pl.program_id(0)
        xhat, rstd = _ln_hat(h_ref[...])
        h4 = xhat * g_ref[...] + beta_ref[...]
        sg = _sigmoid(h4)
        dh4 = d_ref[...] * sg * (1.0 + h4 * (1.0 - sg))
        dh_ref[...] = _ln_back(xhat, rstd, g_ref[...], dh4)
        _acc(i, dg_ref, _colsum(dh4 * xhat))
        _acc(i, db_ref, _colsum(dh4))

    return _rows(body, n_rows=S, tm=_pick(S, ROW_BLOCK, SUBLANES), row_ins=[h3, dh5], full_ins=[g, beta],
                 row_outs=[(C, F32)], acc_outs=[((1, C), F32)] * 2, name=name)


def ffn_gate_bwd(dhh, u, g, gp, w, *, name):
    S, C = u.shape
    K = w.shape[0]
    halo = _halo(K)
    tm = _pick(S, 256, SUBLANES)
    offs = [K - 1 - k for k in range(K)]
    rows = min(CONV_ROWS, tm)

    def body(n, d_ref, u_ref, g_ref, gp_ref, w_ref, du_ref, dgp_ref, dw_ref, db_ref, buf_ref, sh_ref, acc_ref):
        i = pl.program_id(0)

        @pl.when(i == 0)
        def _():
            buf_ref[pl.ds(tm, halo), :] = jnp.zeros((halo, C), F32)
            acc_ref[...] = jnp.zeros_like(acc_ref)

        @pl.when(i > 0)
        def _():
            buf_ref[pl.ds(tm, halo), :] = buf_ref[pl.ds(0, halo), :]

        d, gv = d_ref[...].astype(F32), g_ref[...].astype(F32)
        sg = _sigmoid(gv)
        du_ref[...] = (d * gv * sg).astype(du_ref.dtype)
        dg = d * u_ref[...].astype(F32) * sg * (1.0 + gv * (1.0 - sg))
        buf_ref[pl.ds(0, tm), :] = dg
        _acc(i, db_ref, _colsum(dg))

        def emit(rc, lc, lw, nrows, acc):
            dgp_ref[pl.ds(rc, nrows), pl.ds(lc, lw)] = acc.astype(dgp_ref.dtype)

        _conv_taps(buf_ref, sh_ref, w_ref, offs, tm, C, emit)
        for lc in range(0, C, CONV_LANES):
            lw = min(CONV_LANES, C - lc)
            for k in range(K):
                s_ = None
                for rc in range(0, tm, rows):
                    t = gp_ref[pl.ds(rc, rows), pl.ds(lc, lw)].astype(F32) * _tap(buf_ref, sh_ref, offs, k, rc, rows, lc, lw)
                    s_ = t if s_ is None else s_ + t
                s8 = s_[0:SUBLANES]
                for q in range(1, rows // SUBLANES):
                    s8 = s8 + s_[q * SUBLANES:(q + 1) * SUBLANES]
                acc_ref[pl.ds(k * SUBLANES, SUBLANES), pl.ds(lc, lw)] += s8

        @pl.when(i == n - 1)
        def _():
            for k in range(K):
                dw_ref[pl.ds(k, 1), :] = _colsum(acc_ref[pl.ds(k * SUBLANES, SUBLANES), :])

    return _rows(body, n_rows=S, tm=tm, row_ins=[dhh, u, g, gp], full_ins=[w], row_outs=[(C, BF16), (C, BF16)],
                 acc_outs=[((K, C), F32), ((1, C), F32)],
                 scratch=[pltpu.VMEM((tm + halo, C), F32), _shift_scratch(offs, tm + halo, C),
                          pltpu.VMEM((K * SUBLANES, C), F32)], reverse=True, name=name)


def ln_ple_bwd(r, g, dy, pgl, pp, *, name):
    S, D = r.shape

    def body(n, r_ref, dy_ref, l_ref, p_ref, g_ref, dr_ref, dpp_ref, dpl_ref, dg_ref, db_ref):
        i = pl.program_id(0)
        xhat, rstd = _ln_hat(r_ref[...])
        dy_v = dy_ref[...]
        dr = _ln_back(xhat, rstd, g_ref[...], dy_v)
        dr_ref[...] = dr
        sg = _sigmoid(l_ref[...].astype(F32))
        dpp_ref[...] = (dr * sg).astype(dpp_ref.dtype)
        dpl_ref[...] = (dr * p_ref[...].astype(F32) * sg * (1.0 - sg)).astype(dpl_ref.dtype)
        _acc(i, dg_ref, _colsum(dy_v * xhat))
        _acc(i, db_ref, _colsum(dy_v))

    return _rows(body, n_rows=S, tm=_pick(S, ROW_BLOCK, SUBLANES), row_ins=[r, dy, pgl, pp], full_ins=[g],
                 row_outs=[(D, F32), (D, BF16), (D, BF16)], acc_outs=[((1, D), F32)] * 2, name=name)


def loss_grad(y, target, *, name):
    S, D = y.shape

    def body(n, y_ref, t_ref, dy_ref, l_ref):
        i = pl.program_id(0)
        e = y_ref[...] - t_ref[...]
        dy_ref[...] = e * (1.0 / D)
        s = jnp.sum(_colsum(e * e), axis=1, keepdims=True) * (0.5 / D)
        _acc(i, l_ref, jnp.broadcast_to(s, (1, LANES)))

    return _rows(body, n_rows=S, tm=_pick(S, ROW_BLOCK, SUBLANES), row_ins=[y, target], row_outs=[(D, F32)],
                 acc_outs=[((1, LANES), F32)], name=name)


def _key_step(S):
    return min(512, S // 2)


EXIT_LOG = -110.0


def _attn_consts():
    lane = lax.broadcasted_iota(jnp.int32, (1, LANES), 1)
    heads = (lane < HEAD_DIM, lane >= HEAD_DIM)
    row = lax.broadcasted_iota(jnp.int32, (Q_BLOCK, Q_BLOCK), 0)
    col = lax.broadcasted_iota(jnp.int32, (Q_BLOCK, Q_BLOCK), 1)
    causal = jnp.concatenate([col < row] * 2, axis=0)
    return heads, row, col, causal


def _tri(cond):
    return jnp.where(cond, 1.0, 0.0).astype(BF16)


def _keysum2(x, tri):
    hi = x.astype(BF16)
    lo = (x - hi.astype(F32)).astype(BF16)
    return jnp.dot(jnp.concatenate([hi, lo], axis=1), jnp.concatenate([tri, tri], axis=0),
                   preferred_element_type=F32)


def _stack_heads(x, heads):
    return jnp.concatenate([jnp.where(m, x, jnp.zeros_like(x)) for m in heads], axis=0)


def _log1m_beta(z):
    return -(jnp.maximum(z, 0.0) + jnp.log(1.0 + jnp.exp(-jnp.abs(z))))


def attn_fwd(q, k, v, *, name):
    S, D = q.shape
    nb = S // Q_BLOCK
    tk = _key_step(S)
    nkb = tk // Q_BLOCK
    scale = 1.0 / math.sqrt(HEAD_DIM)

    def body(q_ref, k_ref, v_ref, o_ref, tot_ref, seen_ref, vm_ref):
        heads, row, col, causal = _attn_consts()
        above = _tri(row > col)
        for h in range(2):
            vm_ref[h] = jnp.where(heads[h], v_ref[...], jnp.zeros_like(v_ref[...]))

        def step(sb, carry, qq, nblk, diag):
            acc, cl = carry
            c0 = pl.multiple_of(sb * tk, tk)
            z = lax.dot_general(qq, k_ref[pl.ds(c0, nblk * Q_BLOCK), :], _DOT_DIMS["nt"], preferred_element_type=F32)
            zl, es, rs = [], [], []
            for jb in range(nblk):
                zb = z[:, jb * Q_BLOCK:(jb + 1) * Q_BLOCK]
                lr = _log1m_beta(zb)
                l = jnp.where(causal, lr, 0.0) if diag and jb == nblk - 1 else lr
                zl.append(zb + lr)
                es.append(_keysum2(l, above))
                rs.append(jnp.sum(l, axis=1, keepdims=True))
            a = [None] * nblk
            for jb in reversed(range(nblk)):
                ab = jnp.exp(zl[jb] + es[jb] + cl)
                if diag and jb == nblk - 1:
                    ab = jnp.where(causal, ab, 0.0)
                a[jb] = ab.astype(BF16)
                cl = cl + rs[jb]
            a = jnp.concatenate(a, axis=1)
            for h in range(2):
                acc = acc + jnp.dot(a[h * Q_BLOCK:(h + 1) * Q_BLOCK], vm_ref[h, pl.ds(c0, nblk * Q_BLOCK), :],
                                    preferred_element_type=F32)
            return acc, cl

        def qblock(i, _):
            r0 = pl.multiple_of(i * Q_BLOCK, Q_BLOCK)
            qq = _stack_heads(q_ref[pl.ds(r0, Q_BLOCK), :] * scale, heads)
            last = i // nkb
            carry = (jnp.zeros((Q_BLOCK, LANES), F32), jnp.zeros((2 * Q_BLOCK, 1), F32))
            carry = lax.switch(i % nkb, [functools.partial(step, last, qq=qq, nblk=m + 1, diag=True) for m in range(nkb)],
                               carry)

            def more(c):
                return jnp.logical_and(c[0] < last, jnp.max(c[2]) >= EXIT_LOG)

            def left(c):
                return (c[0] + 1, *step(last - 1 - c[0], c[1:], qq, nkb, False))

            seen, acc, cl = lax.while_loop(more, left, (jnp.int32(0), *carry))
            o_ref[pl.ds(r0, Q_BLOCK), :] = acc.astype(o_ref.dtype)
            tot_ref[pl.ds(r0, Q_BLOCK), :] = jnp.where(heads[0], cl[:Q_BLOCK], cl[Q_BLOCK:])
            seen_ref[pl.ds(pl.multiple_of(i * SUBLANES, SUBLANES), SUBLANES), :] = jnp.full((SUBLANES, LANES), seen, F32)
            return 0

        lax.fori_loop(0, nb, qblock, 0)

    spec = pl.BlockSpec((S, LANES), lambda h: (0, h))
    seen_spec = pl.BlockSpec((nb * SUBLANES, LANES), lambda h: (0, h))
    return pl.pallas_call(body, grid=(D // LANES,), in_specs=[spec] * 3, out_specs=[spec, spec, seen_spec],
                          out_shape=[jax.ShapeDtypeStruct((S, D), BF16), jax.ShapeDtypeStruct((S, D), F32),
                                     jax.ShapeDtypeStruct((nb * SUBLANES, D), F32)],
                          scratch_shapes=[pltpu.VMEM((2, S, LANES), BF16)], compiler_params=_cp("parallel"),
                          name=name)(q, k, v)


def attn_bwd(q, k, v, tot, seen, do, dk0, dv0, *, name):
    S, D = q.shape
    nb = S // Q_BLOCK
    tk = _key_step(S)
    nkb = tk // Q_BLOCK
    scale = 1.0 / math.sqrt(HEAD_DIM)
    has_init = dk0 is not None

    def body(*refs):
        if has_init:
            q_ref, k_ref, v_ref, tot_ref, seen_ref, do_ref, dk0_ref, dv0_ref, dq_ref, dk_ref, dv_ref, km_ref = refs
            dk_ref[...] = dk0_ref[...]
            dv_ref[...] = dv0_ref[...]
        else:
            q_ref, k_ref, v_ref, tot_ref, seen_ref, do_ref, dq_ref, dk_ref, dv_ref, km_ref = refs
            dk_ref[...] = jnp.zeros_like(dk_ref)
            dv_ref[...] = jnp.zeros_like(dv_ref)
        heads, row, col, causal = _attn_consts()
        upto = _tri(row <= col)
        before = _tri(row < col)
        for h in range(2):
            km_ref[h] = jnp.where(heads[h], k_ref[...], jnp.zeros_like(k_ref[...]))

        def step(sb, carry, qq, dd, totl, nblk, diag):
            dq, pl_, pg = carry
            c0 = pl.multiple_of(sb * tk, tk)
            keys = pl.ds(c0, nblk * Q_BLOCK)
            z = lax.dot_general(qq, k_ref[keys, :], _DOT_DIMS["nt"], preferred_element_type=F32)
            da = lax.dot_general(dd, v_ref[keys, :], _DOT_DIMS["nt"], preferred_element_type=F32)
            blocks = range(nblk)
            masked = [diag and jb == nblk - 1 for jb in blocks]
            zb = [z[:, jb * Q_BLOCK:(jb + 1) * Q_BLOCK] for jb in blocks]
            lr = [_log1m_beta(zb[jb]) for jb in blocks]
            l = [jnp.where(causal, lr[jb], 0.0) if masked[jb] else lr[jb] for jb in blocks]
            lsum = [_keysum2(l[jb], upto) for jb in blocks]
            lrow = [jnp.sum(l[jb], axis=1, keepdims=True) for jb in blocks]
            a, g = [None] * nblk, [None] * nblk
            for jb in blocks:
                ab = jnp.exp(zb[jb] + lr[jb] + (totl - pl_ - lsum[jb]))
                if masked[jb]:
                    ab = jnp.where(causal, ab, 0.0)
                g[jb] = ab * da[:, jb * Q_BLOCK:(jb + 1) * Q_BLOCK]
                a[jb] = ab.astype(BF16)
                pl_ = pl_ + lrow[jb]
            gsum = [jnp.dot(g[jb].astype(BF16), before, preferred_element_type=F32) for jb in blocks]
            grow = [jnp.sum(g[jb], axis=1, keepdims=True) for jb in blocks]
            dz = [None] * nblk
            for jb in blocks:
                dzb = g[jb] * jnp.exp(lr[jb]) - jnp.exp(zb[jb] + lr[jb]) * (pg + gsum[jb])
                if masked[jb]:
                    dzb = jnp.where(causal, dzb, 0.0)
                dz[jb] = dzb.astype(BF16)
                pg = pg + grow[jb]
            a = jnp.concatenate(a, axis=1)
            dz = jnp.concatenate(dz, axis=1)
            for h in range(2):
                dq = dq + jnp.dot(dz[h * Q_BLOCK:(h + 1) * Q_BLOCK], km_ref[h, keys, :], preferred_element_type=F32)
            dk_ref[keys, :] += lax.dot_general(dz, qq, _DOT_DIMS["tn"], preferred_element_type=F32)
            dv_ref[keys, :] += lax.dot_general(a, dd, _DOT_DIMS["tn"], preferred_element_type=F32)
            return dq, pl_, pg

        def qblock(i, _):
            r0 = pl.multiple_of(i * Q_BLOCK, Q_BLOCK)
            qq = _stack_heads(q_ref[pl.ds(r0, Q_BLOCK), :] * scale, heads)
            dd = _stack_heads(do_ref[pl.ds(r0, Q_BLOCK), :].astype(BF16), heads)
            tot2 = tot_ref[pl.ds(r0, Q_BLOCK), :]
            totl = jnp.concatenate([tot2[:, 0:1], tot2[:, HEAD_DIM:HEAD_DIM + 1]], axis=0)
            last = i // nkb
            zc = jnp.zeros((2 * Q_BLOCK, 1), F32)
            carry = (jnp.zeros((Q_BLOCK, LANES), F32), zc, zc)
            walked = jnp.max(seen_ref[pl.ds(pl.multiple_of(i * SUBLANES, SUBLANES), SUBLANES), :]).astype(jnp.int32)
            first = last - jnp.clip(walked, 0, last)
            carry = lax.fori_loop(first, last, lambda sb, c: step(sb, c, qq, dd, totl, nkb, False), carry)
            carry = lax.switch(i % nkb, [functools.partial(step, last, qq=qq, dd=dd, totl=totl, nblk=m + 1, diag=True)
                                         for m in range(nkb)], carry)
            dq_ref[pl.ds(r0, Q_BLOCK), :] = (carry[0] * scale).astype(dq_ref.dtype)
            return 0

        lax.fori_loop(0, nb, qblock, 0)

    spec = pl.BlockSpec((S, LANES), lambda h: (0, h))
    seen_spec = pl.BlockSpec((nb * SUBLANES, LANES), lambda h: (0, h))
    args = [q, k, v, tot, seen, do] + ([dk0, dv0] if has_init else [])
    return pl.pallas_call(
        body, grid=(D // LANES,), in_specs=[spec] * 4 + [seen_spec] + [spec] * (len(args) - 5), out_specs=[spec] * 3,
        out_shape=[jax.ShapeDtypeStruct((S, D), BF16), jax.ShapeDtypeStruct((S, D), F32), jax.ShapeDtypeStruct((S, D), F32)],
        scratch_shapes=[pltpu.VMEM((2, S, LANES), BF16)], compiler_params=_cp("parallel"), name=name)(*args)


def _dev_index(px, py, pc):
    return 4 * px + 2 * py + pc


def all_gather(bufs):
    nb = len(bufs)

    def body(*refs):
        ins, outs = refs[:nb], refs[nb:2 * nb]
        send_sems, recv_sems, local_sems = refs[2 * nb:]
        x, y, c = lax.axis_index("x"), lax.axis_index("y"), lax.axis_index("c")
        me, sibling = (x, y, c), (x, y, 1 - c)
        chips = [(1 - x, y), (x, 1 - y), (1 - x, 1 - y)]

        def copy(b, k, block, to, from_input=False):
            slot = outs[b].at[_dev_index(*block)]
            return pltpu.make_async_remote_copy(
                src_ref=ins[b] if from_input else slot, dst_ref=slot,
                send_sem=send_sems.at[7 * b + k], recv_sem=recv_sems.at[7 * b + k], device_id=to, device_id_type=MESH)

        mine = [pltpu.make_async_copy(ins[b], outs[b].at[_dev_index(*me)], local_sems.at[b]) for b in range(nb)]
        for cp in mine:
            cp.start()
        first = []
        for b in range(nb):
            first.append(copy(b, 0, me, sibling, from_input=True))
            first += [copy(b, 1 + j, me, (*chip, c), from_input=True) for j, chip in enumerate(chips)]
        for cp in first:
            cp.start()
        passed = []
        for j, chip in enumerate(chips):
            for b in range(nb):
                copy(b, 1 + j, (*chip, c), me).wait_recv()
                fwd = copy(b, 4 + j, (*chip, c), sibling)
                fwd.start()
                passed.append(fwd)
        for b in range(nb):
            copy(b, 0, sibling, me).wait_recv()
            for j, chip in enumerate(chips):
                copy(b, 4 + j, (*chip, 1 - c), me).wait_recv()
        for cp in first + passed:
            cp.wait_send()
        for cp in mine:
            cp.wait()

    any_spec = pl.BlockSpec(memory_space=pl.ANY)
    return pl.pallas_call(
        body, in_specs=[any_spec] * nb, out_specs=[any_spec] * nb,
        out_shape=[jax.ShapeDtypeStruct((N_DEV,) + b.shape, b.dtype) for b in bufs],
        scratch_shapes=[pltpu.SemaphoreType.DMA((7 * nb,)), pltpu.SemaphoreType.DMA((7 * nb,)),
                        pltpu.SemaphoreType.DMA((nb,))],
        name="all_gather_weights")(*bufs)


def _sources(groups):
    return [s for g in groups for (s, _) in g[3]]


def _layout(groups, refs):
    out, si = [], 0
    for g, (_, _, _, lst) in enumerate(groups):
        for (s, off) in lst:
            out.append((g, refs[si], off, s.shape[-2]))
            si += 1
    return out


def pair_exchange(groups):
    srcs = _sources(groups)
    ns, ng = len(srcs), len(groups)

    def body(*refs):
        outs = refs[ns:ns + ng]
        send_sems, recv_sems = refs[ns + ng:]
        x, y, c = lax.axis_index("x"), lax.axis_index("y"), lax.axis_index("c")
        sibling = (x, y, 1 - c)
        for (g, ref, off, r) in _layout(groups, refs[:ns]):
            for q in range(N_DEV // 2):
                pltpu.make_async_remote_copy(
                    src_ref=ref.at[2 * q + 1 - c], dst_ref=outs[g].at[q, pl.ds(off, r)], send_sem=send_sems.at[g],
                    recv_sem=recv_sems.at[g], device_id=sibling, device_id_type=MESH).start()
        whole = [pltpu.make_async_remote_copy(
            src_ref=outs[g], dst_ref=outs[g], send_sem=send_sems.at[g], recv_sem=recv_sems.at[g],
            device_id=sibling, device_id_type=MESH) for g in range(ng)]
        for w in whole:
            w.wait_recv()
        for w in whole:
            w.wait_send()

    any_spec = pl.BlockSpec(memory_space=pl.ANY)
    return pl.pallas_call(
        body, in_specs=[any_spec] * ns, out_specs=[any_spec] * ng,
        out_shape=[jax.ShapeDtypeStruct((N_DEV // 2, r, w), dt) for (r, w, dt, _) in groups],
        scratch_shapes=[pltpu.SemaphoreType.DMA((ng,)), pltpu.SemaphoreType.DMA((ng,))],
        name="pair_exchange")(*srcs)


def pair_sum(src, got, off, core, *, name):
    _, r, W = src.shape
    tr = _row_tile(r, off, 1024)
    o = off // tr

    def body(c_ref, s_ref, g_ref, o_ref):
        o_ref[...] = (s_ref[...].astype(F32) + g_ref[...].astype(F32)).astype(o_ref.dtype)

    return pl.pallas_call(
        body,
        grid_spec=pltpu.PrefetchScalarGridSpec(
            num_scalar_prefetch=1, grid=(N_DEV // 2, r // tr),
            in_specs=[pl.BlockSpec((None, None, tr, W), lambda q, i, c: (q, c[0], i, 0)),
                      pl.BlockSpec((None, tr, W), lambda q, i, c: (q, i + o, 0))],
            out_specs=pl.BlockSpec((None, tr, W), lambda q, i, c: (q, i, 0))),
        out_shape=jax.ShapeDtypeStruct((N_DEV // 2, r, W), src.dtype), compiler_params=_cp("parallel", "parallel"),
        name=name)(core, src.reshape(N_DEV // 2, 2, r, W), got)


def chip_exchange(groups):
    srcs = _sources(groups)
    ns, ng = len(srcs), len(groups)

    def body(*refs):
        outs = refs[ns:ns + ng]
        send_sems, recv_sems, local_sems = refs[ns + ng:]
        x, y, c = lax.axis_index("x"), lax.axis_index("y"), lax.axis_index("c")
        me = 2 * x + y
        layout = _layout(groups, refs[:ns])
        mine = [pltpu.make_async_copy(ref.at[me], outs[g].at[me, pl.ds(off, r)], local_sems.at[i])
                for i, (g, ref, off, r) in enumerate(layout)]
        for cp in mine:
            cp.start()
        slots = []
        for flip in range(1, N_DEV // 2):
            px, py = (1 - x if flip & 2 else x), (1 - y if flip & 1 else y)
            peer, pq = (px, py, c), 2 * px + py
            for (g, ref, off, r) in layout:
                k = 3 * g + flip - 1
                pltpu.make_async_remote_copy(
                    src_ref=ref.at[pq], dst_ref=outs[g].at[me, pl.ds(off, r)], send_sem=send_sems.at[k],
                    recv_sem=recv_sems.at[k], device_id=peer, device_id_type=MESH).start()
            for g in range(ng):
                k = 3 * g + flip - 1
                slots.append(pltpu.make_async_remote_copy(
                    src_ref=outs[g].at[pq], dst_ref=outs[g].at[pq], send_sem=send_sems.at[k],
                    recv_sem=recv_sems.at[k], device_id=peer, device_id_type=MESH))
        for w in slots:
            w.wait_recv()
        for w in slots:
            w.wait_send()
        for cp in mine:
            cp.wait()

    any_spec = pl.BlockSpec(memory_space=pl.ANY)
    return pl.pallas_call(
        body, in_specs=[any_spec] * ns, out_specs=[any_spec] * ng,
        out_shape=[jax.ShapeDtypeStruct((N_DEV // 2, r, w), dt) for (r, w, dt, _) in groups],
        scratch_shapes=[pltpu.SemaphoreType.DMA((3 * ng,)), pltpu.SemaphoreType.DMA((3 * ng,)),
                        pltpu.SemaphoreType.DMA((ns,))],
        name="chip_exchange")(*srcs)


def _row_tile(rows, off, target):
    for t in (1024, 512, 256, 128, 64, 32, 16, 8):
        if t <= target and rows % t == 0 and off % t == 0:
            return t
    raise ValueError((rows, off))


def adamw(recv, off, w, m, v, *, name):
    rows, W = w.shape
    nslot = recv.shape[0]
    tr = _row_tile(rows, off, 512)
    o = off // tr
    c1 = 1.0 - ADAM_B1 ** ADAM_STEP
    c2 = 1.0 - ADAM_B2 ** ADAM_STEP

    def body(r_ref, w_ref, m_ref, v_ref, g_ref, d_ref, mo_ref, vo_ref):
        g = r_ref[0, :, pl.ds(0, W)].astype(F32)
        for j in range(1, nslot):
            g = g + r_ref[j, :, pl.ds(0, W)].astype(F32)
        mn = ADAM_B1 * m_ref[...] + (1.0 - ADAM_B1) * g
        vn = ADAM_B2 * v_ref[...] + (1.0 - ADAM_B2) * (g * g)
        g_ref[...] = g
        mo_ref[...] = mn
        vo_ref[...] = vn
        d_ref[...] = -ADAM_LR * ((mn / c1) / (jnp.sqrt(vn / c2) + ADAM_EPS) + ADAM_WD * w_ref[...])

    spec = pl.BlockSpec((tr, W), lambda i: (i, 0))
    return pl.pallas_call(
        body, grid=(rows // tr,), in_specs=[pl.BlockSpec((nslot, tr, recv.shape[2]), lambda i: (0, i + o, 0)), spec, spec, spec],
        out_specs=[spec] * 4, out_shape=[jax.ShapeDtypeStruct((rows, W), F32)] * 4,
        compiler_params=_cp("parallel"), name=name)(recv, w, m, v)


def join_columns(gathered, off, K, *, name):
    _, _, n = gathered.shape
    tr = _row_tile(K, off, 512)
    o = off // tr

    def body(i_ref, o_ref):
        for d in range(N_DEV):
            o_ref[:, d * n:(d + 1) * n] = i_ref[d]

    return pl.pallas_call(
        body, grid=(K // tr,), in_specs=[pl.BlockSpec((N_DEV, tr, n), lambda i: (0, i + o, 0))],
        out_specs=pl.BlockSpec((tr, N_DEV * n), lambda i: (i, 0)),
        out_shape=jax.ShapeDtypeStruct((K, N_DEV * n), gathered.dtype), compiler_params=_cp("parallel"),
        name=name)(gathered)


def _lane_pad(n):
    return -(-n // LANES) * LANES


def split_columns(full, *, name):
    K, N = full.shape
    n = N // N_DEV
    n_pad = _lane_pad(n)
    tr = _row_tile(K, 0, 512)

    def body(i_ref, o_ref):
        for d in range(N_DEV):
            o_ref[d, :, pl.ds(0, n)] = i_ref[:, d * n:(d + 1) * n].astype(o_ref.dtype)
            if n_pad > n:
                o_ref[d, :, pl.ds(n, n_pad - n)] = jnp.zeros((tr, n_pad - n), o_ref.dtype)

    return pl.pallas_call(
        body, grid=(K // tr,), in_specs=[pl.BlockSpec((tr, N), lambda i: (i, 0))],
        out_specs=pl.BlockSpec((N_DEV, tr, n_pad), lambda i: (0, i, 0)),
        out_shape=jax.ShapeDtypeStruct((N_DEV, K, n_pad), BF16), compiler_params=_cp("parallel"), name=name)(full)


def _pack(arrs, dtype, row_mult):
    flat = jnp.concatenate([a.reshape(-1).astype(dtype) for a in arrs])
    rows = -(-flat.shape[0] // PACK_W)
    rows = -(-rows // row_mult) * row_mult
    return jnp.pad(flat, (0, rows * PACK_W - flat.shape[0])).reshape(rows, PACK_W)


def _pack_dev(arrs, dtype, row_mult):
    flat = jnp.concatenate([a.reshape(N_DEV, -1).astype(dtype) for a in arrs], axis=1)
    rows = -(-flat.shape[1] // PACK_W)
    rows = -(-rows // row_mult) * row_mult
    return jnp.pad(flat, ((0, 0), (0, rows * PACK_W - flat.shape[1]))).reshape(N_DEV, rows, PACK_W)


def _unpack(buf, shapes):
    lead = buf.shape[:-2]
    flat = buf.reshape(lead + (-1,))
    outs, off = [], 0
    for s in shapes:
        n = math.prod(s)
        outs.append(flat[..., off:off + n].reshape(lead + tuple(s)))
        off += n
    return outs


def _join(g, axis):
    g = jnp.moveaxis(g, 0, axis)
    return g.reshape(g.shape[:axis] + (g.shape[axis] * g.shape[axis + 1],) + g.shape[axis + 2:])


def _split(full, axis):
    s = full.shape
    g = full.reshape(s[:axis] + (N_DEV, s[axis] // N_DEV) + s[axis + 1:])
    return jnp.moveaxis(g, axis, 0)


def kernel(x, p, a_pw1_w, a_pw1_b, a_dw_w, a_dw_b, a_ln_g, a_ln_b, a_pw2_w, a_pw2_b, b_wq, kv_wk, kv_wv, b_wo, ln_mix_g, ln_mix_b, ffn_w_up, ffn_w_gate, ffn_conv_w, ffn_conv_b, ffn_w_down, ple_w_gate, ple_w_proj, ln_ffn_g, ln_ffn_b, loss_target, m_a_pw1_w, m_a_pw1_b, m_a_dw_w, m_a_dw_b, m_a_ln_g, m_a_ln_b, m_a_pw2_w, m_a_pw2_b, m_b_wq, m_kv_wk, m_kv_wv, m_b_wo, m_ln_mix_g, m_ln_mix_b, m_ffn_w_up, m_ffn_w_gate, m_ffn_conv_w, m_ffn_conv_b, m_ffn_w_down, m_ple_w_gate, m_ple_w_proj, m_ln_ffn_g, m_ln_ffn_b, v_a_pw1_w, v_a_pw1_b, v_a_dw_w, v_a_dw_b, v_a_ln_g, v_a_ln_b, v_a_pw2_w, v_a_pw2_b, v_b_wq, v_kv_wk, v_kv_wv, v_b_wo, v_ln_mix_g, v_ln_mix_b, v_ffn_w_up, v_ffn_w_gate, v_ffn_conv_w, v_ffn_conv_b, v_ffn_w_down, v_ple_w_gate, v_ple_w_proj, v_ln_ffn_g, v_ln_ffn_b):
    local = dict(a_pw1_w=a_pw1_w, a_pw1_b=a_pw1_b, a_dw_w=a_dw_w, a_dw_b=a_dw_b, a_ln_g=a_ln_g, a_ln_b=a_ln_b, a_pw2_w=a_pw2_w, a_pw2_b=a_pw2_b, b_wq=b_wq, kv_wk=kv_wk, kv_wv=kv_wv, b_wo=b_wo, ln_mix_g=ln_mix_g, ln_mix_b=ln_mix_b, ffn_w_up=ffn_w_up, ffn_w_gate=ffn_w_gate, ffn_conv_w=ffn_conv_w, ffn_conv_b=ffn_conv_b, ffn_w_down=ffn_w_down, ple_w_gate=ple_w_gate, ple_w_proj=ple_w_proj, ln_ffn_g=ln_ffn_g, ln_ffn_b=ln_ffn_b)
    mom1 = dict(a_pw1_w=m_a_pw1_w, a_pw1_b=m_a_pw1_b, a_dw_w=m_a_dw_w, a_dw_b=m_a_dw_b, a_ln_g=m_a_ln_g, a_ln_b=m_a_ln_b, a_pw2_w=m_a_pw2_w, a_pw2_b=m_a_pw2_b, b_wq=m_b_wq, kv_wk=m_kv_wk, kv_wv=m_kv_wv, b_wo=m_b_wo, ln_mix_g=m_ln_mix_g, ln_mix_b=m_ln_mix_b, ffn_w_up=m_ffn_w_up, ffn_w_gate=m_ffn_w_gate, ffn_conv_w=m_ffn_conv_w, ffn_conv_b=m_ffn_conv_b, ffn_w_down=m_ffn_w_down, ple_w_gate=m_ple_w_gate, ple_w_proj=m_ple_w_proj, ln_ffn_g=m_ln_ffn_g, ln_ffn_b=m_ln_ffn_b)
    mom2 = dict(a_pw1_w=v_a_pw1_w, a_pw1_b=v_a_pw1_b, a_dw_w=v_a_dw_w, a_dw_b=v_a_dw_b, a_ln_g=v_a_ln_g, a_ln_b=v_a_ln_b, a_pw2_w=v_a_pw2_w, a_pw2_b=v_a_pw2_b, b_wq=v_b_wq, kv_wk=v_kv_wk, kv_wv=v_kv_wv, b_wo=v_b_wo, ln_mix_g=v_ln_mix_g, ln_mix_b=v_ln_mix_b, ffn_w_up=v_ffn_w_up, ffn_w_gate=v_ffn_w_gate, ffn_conv_w=v_ffn_conv_w, ffn_conv_b=v_ffn_conv_b, ffn_w_down=v_ffn_w_down, ple_w_gate=v_ple_w_gate, ple_w_proj=v_ple_w_proj, ln_ffn_g=v_ln_ffn_g, ln_ffn_b=v_ln_ffn_b)
    small_names = [n for n, _ in SMALL]
    small_shapes = [local[n].shape for n in small_names]
    repl_shapes = [local[n].shape for n in REPL]

    widths = sorted({local[n].shape[-1] for n, _ in BIG}, reverse=True)
    groups = {w: [n for n, _ in BIG if local[n].shape[-1] == w] for w in widths}
    offset, rows_of = {}, {}
    for w, names in groups.items():
        off = 0
        for n in names:
            offset[n], rows_of[n] = off, math.prod(local[n].shape[:-1])
            off += rows_of[n]
    sends = [jnp.concatenate([local[n].reshape(-1, w).astype(BF16) for n in names]) for w, names in groups.items()]
    gathered = all_gather(sends + [_pack([local[n] for n in small_names], F32, SUBLANES)])
    gath = dict(zip(widths, gathered[:-1]))
    W = {}
    for n, ax in BIG:
        w = local[n].shape[-1]
        nl = local[n].shape[0] if local[n].ndim == 3 else 1
        per = rows_of[n] // nl
        if ax == local[n].ndim - 1:
            W[n] = [join_columns(gath[w], offset[n] + l * per, per, name=f"join_{n}_{l}") for l in range(nl)]
        else:
            W[n] = [gath[w][:, offset[n] + l * per:offset[n] + (l + 1) * per].reshape(N_DEV * per, w) for l in range(nl)]
    for n in ("kv_wk", "kv_wv"):
        W[n] = W[n][0]
    W.update({n: _join(g, ax) for (n, ax), g in zip(SMALL, _unpack(gathered[-1], small_shapes))})
    W.update({n: local[n] for n in REPL})

    xs = x[0]
    S, D = xs.shape
    x_in, r1s, x1s, r2s, us, gps, gs, hhs, pgls, pps = [], [], [], [], [], [], [], [], [], []
    h1s, h2s, h3s, h5s, qs, os_, tots = {}, {}, {}, {}, {}, {}, {}
    kk = vv = None
    for i in range(DEPTH):
        x_in.append(xs)
        if i < N_A:
            h1 = mm(xs, W["a_pw1_w"][i], "nn", bias=W["a_pw1_b"][i][None], out_dtype=BF16, name=f"pw1_{i}")
            h2 = glu_fwd(h1, name=f"glu_{i}")
            h3, h5 = conv_ln_silu_fwd(h2, W["a_dw_w"][i], W["a_dw_b"][i][None], W["a_ln_g"][i][None],
                                      W["a_ln_b"][i][None], name=f"dwconv_{i}")
            mix = mm(h5, W["a_pw2_w"][i], "nn", bias=W["a_pw2_b"][i][None], name=f"pw2_{i}")
            h1s[i], h2s[i], h3s[i], h5s[i] = h1, h2, h3, h5
        else:
            j = i - N_A
            if kk is None:
                kk, vv = mm(xs, W["kv_wk"], "nn", out_dtype=BF16, also=W["kv_wv"], name="proj_kv")
            q = mm(xs, W["b_wq"][j], "nn", out_dtype=BF16, name=f"proj_q_{i}")
            o, tot, seen = attn_fwd(q, kk, vv, name=f"attn_{i}")
            mix = mm(o, W["b_wo"][j], "nn", name=f"proj_o_{i}")
            qs[i], os_[i], tots[i] = q, o, (tot, seen)
        r1, x1 = res_ln(xs, mix, W["ln_mix_g"][i][None], W["ln_mix_b"][i][None], name=f"ln_mix_{i}")
        u, gp = mm(x1, W["ffn_w_up"][i], "nn", out_dtype=BF16, also=W["ffn_w_gate"][i], name=f"ffn_up_gate_{i}")
        g, hh = conv_act_fwd(gp, u, W["ffn_conv_w"][i], W["ffn_conv_b"][i][None], name=f"ffn_conv_{i}")
        f = mm(hh, W["ffn_w_down"][i], "nn", name=f"ffn_down_{i}")
        pgl = mm(x1, W["ple_w_gate"][i], "nn", out_dtype=BF16, name=f"ple_gate_{i}")
        pp = mm(p[i, 0], W["ple_w_proj"][i], "nn", out_dtype=BF16, name=f"ple_proj_{i}")
        r2, xs = res_ln(x1, f, W["ln_ffn_g"][i][None], W["ln_ffn_b"][i][None], ple=(pgl, pp), name=f"ln_ffn_{i}")
        for lst, val in ((r1s, r1), (x1s, x1), (r2s, r2), (us, u), (gps, gp), (gs, g), (hhs, hh), (pgls, pgl), (pps, pp)):
            lst.append(val)

    dx, loss_part = loss_grad(xs, loss_target[0], name="loss")
    G = {n: [None] * local[n].shape[0] for n in WEIGHTS if n not in ("kv_wk", "kv_wv")}
    dk = dv = None
    for i in reversed(range(DEPTH)):
        x1 = x1s[i]
        dr2, dpp, dpgl, G["ln_ffn_g"][i], G["ln_ffn_b"][i] = ln_ple_bwd(r2s[i], W["ln_ffn_g"][i][None], dx, pgls[i], pps[i],
                                                                      name=f"ln_ffn_bwd_{i}")
        dhh = mm(dr2, W["ffn_w_down"][i], "nt", out_dtype=BF16, name=f"ffn_down_dx_{i}")
        G["ffn_w_down"][i] = mm(hhs[i], dr2, "tn", out_dtype=BF16, name=f"ffn_down_dw_{i}")
        G["ple_w_proj"][i] = mm(p[i, 0], dpp, "tn", out_dtype=BF16, name=f"ple_proj_dw_{i}")
        G["ple_w_gate"][i] = mm(x1, dpgl, "tn", out_dtype=BF16, name=f"ple_gate_dw_{i}")
        du, dgp, G["ffn_conv_w"][i], G["ffn_conv_b"][i] = ffn_gate_bwd(dhh, us[i], gs[i], gps[i], W["ffn_conv_w"][i],
                                                                       name=f"ffn_gate_bwd_{i}")
        G["ffn_w_up"][i], G["ffn_w_gate"][i] = mm(x1, du, "tn", out_dtype=BF16, also=dgp, name=f"ffn_up_gate_dw_{i}")
        dx1 = mm(du, W["ffn_w_up"][i], "nt", add=dr2, add_scale=DN_ALPHA, plus=(dgp, W["ffn_w_gate"][i]),
                 name=f"ffn_up_gate_dx_{i}")
        dx1 = mm(dpgl, W["ple_w_gate"][i], "nt", add=dx1, name=f"ple_gate_dx_{i}")
        dr1, G["ln_mix_g"][i], G["ln_mix_b"][i], dr1_sum = ln_bwd(r1s[i], W["ln_mix_g"][i][None], dx1, name=f"ln_mix_bwd_{i}")
        if i < N_A:
            G["a_pw2_w"][i] = mm(h5s[i], dr1, "tn", out_dtype=BF16, name=f"pw2_dw_{i}")
            G["a_pw2_b"][i] = dr1_sum
            dh5 = mm(dr1, W["a_pw2_w"][i], "nt", name=f"pw2_dx_{i}")
            dh3, G["a_ln_g"][i], G["a_ln_b"][i] = ln_silu_bwd(h3s[i], W["a_ln_g"][i][None], W["a_ln_b"][i][None], dh5,
                                                             name=f"dwconv_ln_bwd_{i}")
            dh2 = conv_bwd_x(dh3, W["a_dw_w"][i], out_dtype=F32, name=f"dwconv_dx_{i}")
            G["a_dw_w"][i], G["a_dw_b"][i] = conv_bwd_w(h2s[i], dh3, CONV_W, name=f"dwconv_dw_{i}")
            dh1, G["a_pw1_b"][i] = glu_bwd(h1s[i], dh2, name=f"glu_bwd_{i}")
            G["a_pw1_w"][i] = mm(x_in[i], dh1, "tn", out_dtype=BF16, name=f"pw1_dw_{i}")
            dx = mm(dh1, W["a_pw1_w"][i], "nt", add=dr1, add_scale=DN_ALPHA, name=f"pw1_dx_{i}")
        else:
            j = i - N_A
            G["b_wo"][j] = mm(os_[i], dr1, "tn", out_dtype=BF16, name=f"proj_o_dw_{i}")
            do = mm(dr1, W["b_wo"][j], "nt", out_dtype=BF16, name=f"proj_o_dx_{i}")
            dq, dk, dv = attn_bwd(qs[i], kk, vv, *tots[i], do, dk, dv, name=f"attn_bwd_{i}")
            G["b_wq"][j] = mm(x_in[i], dq, "tn", out_dtype=BF16, name=f"proj_q_dw_{i}")
            dx = mm(dq, W["b_wq"][j], "nt", add=dr1, add_scale=DN_ALPHA, name=f"proj_q_dx_{i}")
            if j == 0:
                G["kv_wk"], G["kv_wv"] = mm(x_in[i], dk, "tn", out_dtype=BF16, also=dv, name="proj_kv_dw")
                dx = mm(dk, W["kv_wk"], "nt", add=dx, plus=(dv, W["kv_wv"]), name="proj_kv_dx")
    grad_x = dx[None]
    shard_axis = dict(BIG + SMALL)
    for n in small_names + list(REPL):
        full = list(local[n].shape)
        if n in shard_axis:
            full[shard_axis[n]] *= N_DEV
        G[n] = jnp.stack(G[n]).reshape(full)

    n_small = sum(math.prod(s) for s in small_shapes)
    n_repl = sum(math.prod(s) for s in repl_shapes)
    repl_flat = jnp.concatenate([G[n].reshape(-1) for n in REPL] + [loss_part.reshape(-1)[:1]])
    send_small = _pack_dev([_split(G[n], ax) for n, ax in SMALL] + [jnp.broadcast_to(repl_flat, (N_DEV, n_repl + 1))],
                           F32, SUBLANES)
    ex_groups = []
    for w, names in groups.items():
        lst = []
        for n in names:
            layers = G[n] if isinstance(G[n], list) else [G[n]]
            per = rows_of[n] // len(layers)
            for l, g in enumerate(layers):
                if shard_axis[n] == local[n].ndim - 1:
                    src = split_columns(g, name=f"split_{n}_{l}")
                else:
                    src = g.reshape(N_DEV, per, w)
                lst.append((src, offset[n] + l * per))
        ex_groups.append((sum(rows_of[n] for n in names), lst[0][0].shape[-1], BF16, lst))
    ex_groups.append((send_small.shape[1], PACK_W, F32, [(send_small, 0)]))
    gots = pair_exchange(ex_groups)
    core = lax.axis_index("c").astype(jnp.int32).reshape(1)
    sum_groups = [(rows, w, dt, [(pair_sum(src, got, off, core, name=f"pair_sum_{gi}_{si}"), off) for si, (src, off) in enumerate(lst)])
                  for gi, ((rows, w, dt, lst), got) in enumerate(zip(ex_groups, gots))]
    recvs = chip_exchange(sum_groups)
    recv = dict(zip(widths, recvs[:-1]))
    recv_small = recvs[-1]

    out = {}
    for n, _ in BIG:
        w = local[n].shape[-1]
        res = adamw(recv[w], offset[n], local[n].reshape(-1, w), mom1[n].reshape(-1, w), mom2[n].reshape(-1, w),
                    name=f"adamw_{n}")
        out[n] = [r.reshape(local[n].shape) for r in res]

    def state(d):
        small = _pack([d[n] for n in small_names] + [d[n] for n in REPL], F32, SUBLANES)
        return jnp.pad(small, ((0, recv_small.shape[1] - small.shape[0]), (0, 0)))

    out_small = adamw(recv_small, 0, state(local), state(mom1), state(mom2), name="adamw_vectors")
    loss = out_small[0].reshape(-1)[n_small + n_repl]
    vecs = [dict(zip(small_names + list(REPL), _unpack(o, small_shapes + repl_shapes))) for o in out_small]
    per_kind = [[out[n][kind] if n in out else vecs[kind][n] for n in WEIGHTS] for kind in range(4)]
    grads, deltas, new_m, new_v = per_kind
    return (loss, grad_x, *grads, *deltas, *new_m, *new_v)
```

```python
import functools
import math

import jax
import jax.numpy as jnp
from jax import lax
from jax.experimental import pallas as pl
from jax.experimental.pallas import tpu as pltpu

F32 = jnp.float32
BF16 = jnp.bfloat16
MESH = pl.DeviceIdType.MESH

N_DEV = 8
DEPTH = 4
N_A = 2
HEAD_DIM = 64
Q_BLOCK = 128
CONV_W = 31
FFN_CONV_W = 3
LN_EPS = 1e-5
DN_ALPHA = (2.0 * DEPTH) ** 0.25
ADAM_LR = 0.001
ADAM_B1 = 0.9
ADAM_B2 = 0.999
ADAM_EPS = 1e-08
ADAM_WD = 0.01
ADAM_STEP = 10

LANES = 128
SUBLANES = 8
PACK_W = 1024
VMEM_LIMIT = 56 * 1024 * 1024

BIG = (("ffn_w_down", 1), ("a_pw1_w", 2), ("a_pw2_w", 1), ("b_wq", 1), ("kv_wk", 0), ("kv_wv", 0), ("b_wo", 1),
       ("ffn_w_up", 2), ("ffn_w_gate", 2), ("ple_w_gate", 1), ("ple_w_proj", 2))
SMALL = (("a_pw1_b", 1), ("a_dw_w", 2), ("a_dw_b", 1), ("a_ln_g", 1), ("a_ln_b", 1), ("a_pw2_b", 1),
         ("ffn_conv_w", 2))
REPL = ("ln_mix_g", "ln_mix_b", "ffn_conv_b", "ln_ffn_g", "ln_ffn_b")
WEIGHTS = ("a_pw1_w", "a_pw1_b", "a_dw_w", "a_dw_b", "a_ln_g", "a_ln_b", "a_pw2_w", "a_pw2_b", "b_wq", "kv_wk",
           "kv_wv", "b_wo", "ln_mix_g", "ln_mix_b", "ffn_w_up", "ffn_w_gate", "ffn_conv_w", "ffn_conv_b",
           "ffn_w_down", "ple_w_gate", "ple_w_proj", "ln_ffn_g", "ln_ffn_b")


def _cp(*sem):
    return pltpu.CompilerParams(dimension_semantics=sem, vmem_limit_bytes=VMEM_LIMIT)


def _pick(dim, target, align=LANES):
    if dim <= target:
        return dim
    t = (target // align) * align
    while t >= align:
        if dim % t == 0:
            return t
        t -= align
    return dim


MM_ROWS = 1024
MM_ROWS_TN = 1536
MM_COLS = 1536
MM_DEPTH = 2816
MM_DEPTH_TN = 1536

_DOT_DIMS = {"nn": (((1,), (0,)), ((), ())), "nt": (((1,), (1,)), ((), ())), "tn": (((0,), (0,)), ((), ()))}


def mm(a, b, mode, *, bias=None, add=None, add_scale=1.0, out_dtype=F32, also=None, plus=None, name):
    if mode == "tn":
        K, M = a.shape
    else:
        M, K = a.shape
    N = b.shape[0] if mode == "nt" else b.shape[1]
    tm = _pick(M, MM_ROWS_TN if mode == "tn" else MM_ROWS if plus is None else MM_ROWS // 2)
    tn = _pick(N, MM_COLS)
    tk = _pick(K, MM_DEPTH_TN if mode == "tn" else MM_DEPTH)
    nk = K // tk
    dims = _DOT_DIMS[mode]
    n_out = 1 if also is None else 2

    def body(*refs):
        refs = list(refs)
        a_ref, b_ref = refs.pop(0), refs.pop(0)
        b2_ref = refs.pop(0) if also is not None else None
        a3_ref, b3_ref = (refs.pop(0), refs.pop(0)) if plus is not None else (None, None)
        bias_ref = refs.pop(0) if bias is not None else None
        add_ref = refs.pop(0) if add is not None else None
        o_refs, acc_refs = refs[:n_out], refs[n_out:]
        k = pl.program_id(2)

        @pl.when(k == 0)
        def _():
            for acc_ref in acc_refs:
                acc_ref[...] = jnp.zeros_like(acc_ref)

        a_v = a_ref[...].astype(BF16)
        acc_refs[0][...] += lax.dot_general(a_v, b_ref[...].astype(BF16), dims, preferred_element_type=F32)
        if b2_ref is not None:
            acc_refs[1][...] += lax.dot_general(a_v, b2_ref[...].astype(BF16), dims, preferred_element_type=F32)
        if a3_ref is not None:
            acc_refs[0][...] += lax.dot_general(a3_ref[...].astype(BF16), b3_ref[...].astype(BF16), dims,
                                                preferred_element_type=F32)

        @pl.when(k == nk - 1)
        def _():
            for o_ref, acc_ref in zip(o_refs, acc_refs):
                r = acc_ref[...]
                if bias_ref is not None:
                    r = r + bias_ref[...]
                if add_ref is not None:
                    r = r + add_scale * add_ref[...].astype(F32)
                o_ref[...] = r.astype(o_ref.dtype)

    a_spec = pl.BlockSpec((tk, tm), lambda j, i, k: (k, i)) if mode == "tn" else pl.BlockSpec((tm, tk), lambda j, i, k: (i, k))
    b_spec = pl.BlockSpec((tn, tk), lambda j, i, k: (j, k)) if mode == "nt" else pl.BlockSpec((tk, tn), lambda j, i, k: (k, j))
    in_specs, args = [a_spec, b_spec], [a, b]
    if also is not None:
        in_specs.append(b_spec)
        args.append(also)
    if plus is not None:
        in_specs += [a_spec, b_spec]
        args += list(plus)
    if bias is not None:
        in_specs.append(pl.BlockSpec((1, tn), lambda j, i, k: (0, j)))
        args.append(bias)
    if add is not None:
        in_specs.append(pl.BlockSpec((tm, tn), lambda j, i, k: (i, j)))
        args.append(add)
    o_spec = pl.BlockSpec((tm, tn), lambda j, i, k: (i, j))
    outs = pl.pallas_call(
        body, grid=(N // tn, M // tm, nk), in_specs=in_specs,
        out_specs=[o_spec] * n_out, out_shape=[jax.ShapeDtypeStruct((M, N), out_dtype)] * n_out,
        scratch_shapes=[pltpu.VMEM((tm, tn), F32)] * n_out,
        compiler_params=_cp("parallel", "parallel", "arbitrary"), name=name)(*args)
    return outs[0] if also is None else tuple(outs)


ROW_BLOCK = 512


def _rows(body, *, n_rows, tm, row_ins, full_ins=(), row_outs=(), acc_outs=(), scratch=(), reverse=False, name):
    n = n_rows // tm

    def rmap(i):
        return (n - 1 - i, 0) if reverse else (i, 0)

    in_specs = [pl.BlockSpec((tm, a.shape[1]), rmap) for a in row_ins]
    in_specs += [pl.BlockSpec(a.shape, lambda i, nd=a.ndim: (0,) * nd) for a in full_ins]
    out_shape = [jax.ShapeDtypeStruct((n_rows, w), dt) for (w, dt) in row_outs]
    out_shape += [jax.ShapeDtypeStruct(s, dt) for (s, dt) in acc_outs]
    out_specs = [pl.BlockSpec((tm, w), rmap) for (w, dt) in row_outs]
    out_specs += [pl.BlockSpec(s, lambda i, nd=len(s): (0,) * nd) for (s, dt) in acc_outs]
    return pl.pallas_call(
        functools.partial(body, n), grid=(n,), in_specs=in_specs, out_specs=out_specs, out_shape=out_shape,
        scratch_shapes=list(scratch), compiler_params=_cp("arbitrary"), name=name)(*row_ins, *full_ins)


def _sigmoid(x):
    return 1.0 / (1.0 + jnp.exp(-x))


def _ln_hat(r):
    mu = jnp.mean(r, axis=-1, keepdims=True)
    xc = r - mu
    var = jnp.mean(xc * xc, axis=-1, keepdims=True)
    rstd = lax.rsqrt(var + LN_EPS)
    return xc * rstd, rstd


def _ln_back(xhat, rstd, g, dy):
    dxh = dy * g
    m1 = jnp.mean(dxh, axis=-1, keepdims=True)
    m2 = jnp.mean(dxh * xhat, axis=-1, keepdims=True)
    return rstd * (dxh - m1 - xhat * m2)


def _colsum(x):
    return jnp.sum(x, axis=0, keepdims=True)


def _acc(i, ref, val):
    @pl.when(i == 0)
    def _():
        ref[...] = val

    @pl.when(i > 0)
    def _():
        ref[...] += val


def res_ln(x, mix, g, b, *, ple=None, name):
    S, D = x.shape

    def body(n, *refs):
        if ple is None:
            x_ref, m_ref, g_ref, b_ref, r_ref, y_ref = refs
            r = DN_ALPHA * x_ref[...] + m_ref[...]
        else:
            x_ref, m_ref, pgl_ref, pp_ref, g_ref, b_ref, r_ref, y_ref = refs
            r = DN_ALPHA * x_ref[...] + m_ref[...] + _sigmoid(pgl_ref[...].astype(F32)) * pp_ref[...].astype(F32)
        xhat, _ = _ln_hat(r)
        r_ref[...] = r
        y_ref[...] = xhat * g_ref[...] + b_ref[...]

    row_ins = [x, mix] + ([] if ple is None else list(ple))
    return _rows(body, n_rows=S, tm=_pick(S, ROW_BLOCK, SUBLANES), row_ins=row_ins, full_ins=[g, b],
                 row_outs=[(D, F32), (D, F32)], name=name)


def ln_bwd(r, g, dy, *, name):
    S, D = r.shape

    def body(n, r_ref, dy_ref, g_ref, dr_ref, dg_ref, db_ref, ds_ref):
        i = pl.program_id(0)
        xhat, rstd = _ln_hat(r_ref[...])
        dy_v = dy_ref[...]
        dr = _ln_back(xhat, rstd, g_ref[...], dy_v)
        dr_ref[...] = dr
        _acc(i, dg_ref, _colsum(dy_v * xhat))
        _acc(i, db_ref, _colsum(dy_v))
        _acc(i, ds_ref, _colsum(dr))

    return _rows(body, n_rows=S, tm=_pick(S, ROW_BLOCK, SUBLANES), row_ins=[r, dy], full_ins=[g],
                 row_outs=[(D, F32)], acc_outs=[((1, D), F32)] * 3, name=name)


def glu_fwd(h1, *, name):
    S, D2 = h1.shape
    D = D2 // 2

    def body(n, h_ref, o_ref):
        o_ref[...] = h_ref[:, :D].astype(F32) * _sigmoid(h_ref[:, D:].astype(F32))

    return _rows(body, n_rows=S, tm=_pick(S, ROW_BLOCK, SUBLANES), row_ins=[h1], row_outs=[(D, F32)], name=name)[0]


def glu_bwd(h1, dh2, *, name):
    S, D2 = h1.shape
    D = D2 // 2

    def body(n, h_ref, d_ref, o_ref, s_ref):
        i = pl.program_id(0)
        a, sg, d = h_ref[:, :D].astype(F32), _sigmoid(h_ref[:, D:].astype(F32)), d_ref[...]
        da = d * sg
        dg = d * a * sg * (1.0 - sg)
        o_ref[:, :D] = da.astype(o_ref.dtype)
        o_ref[:, D:] = dg.astype(o_ref.dtype)
        _acc(i, s_ref, jnp.concatenate([_colsum(da), _colsum(dg)], axis=1))

    return _rows(body, n_rows=S, tm=_pick(S, ROW_BLOCK, SUBLANES), row_ins=[h1, dh2], row_outs=[(D2, BF16)],
                 acc_outs=[((1, D2), F32)], name=name)


CONV_ROWS = 32
CONV_LANES = 256


def _halo(k):
    return -(-(k - 1) // SUBLANES) * SUBLANES


def _phases(offs):
    return sorted({o % SUBLANES for o in offs} - {0})


def _shift_scratch(offs, n_rows, width):
    return pltpu.VMEM((max(len(_phases(offs)), 1), n_rows, width), F32)


def _make_shifted(buf_ref, sh_ref, offs):
    n = buf_ref.shape[0] - SUBLANES
    for p, b in enumerate(_phases(offs)):
        sh_ref[p, pl.ds(0, n), :] = buf_ref[pl.ds(b, n), :]


def _tap(buf_ref, sh_ref, offs, k, rc, rows, lc, lw):
    b = offs[k] % SUBLANES
    src = buf_ref if b == 0 else sh_ref.at[_phases(offs).index(b)]
    return src[pl.ds(offs[k] - b + rc, rows), pl.ds(lc, lw)]


def _conv_taps(buf_ref, sh_ref, w_ref, offs, tm, width, emit):
    _make_shifted(buf_ref, sh_ref, offs)
    rows = min(CONV_ROWS, tm)
    for lc in range(0, width, CONV_LANES):
        lw = min(CONV_LANES, width - lc)
        for rc in range(0, tm, rows):
            acc = None
            for k in range(len(offs)):
                t = _tap(buf_ref, sh_ref, offs, k, rc, rows, lc, lw) * w_ref[pl.ds(k, 1), pl.ds(lc, lw)]
                acc = t if acc is None else acc + t
            emit(rc, lc, lw, rows, acc)


def _fill_causal(i, buf_ref, x_ref, halo, tm):
    @pl.when(i == 0)
    def _():
        buf_ref[pl.ds(0, halo), :] = jnp.zeros((halo, buf_ref.shape[1]), F32)

    @pl.when(i > 0)
    def _():
        buf_ref[pl.ds(0, halo), :] = buf_ref[pl.ds(tm, halo), :]

    buf_ref[pl.ds(halo, tm), :] = x_ref[...].astype(F32)


def conv_ln_silu_fwd(x, w, b, g, beta, *, name):
    S, C = x.shape
    K = w.shape[0]
    halo = _halo(K)
    tm = _pick(S, 256, SUBLANES)
    offs = [halo - (K - 1) + k for k in range(K)]

    def body(n, x_ref, w_ref, b_ref, g_ref, beta_ref, h3_ref, h5_ref, buf_ref, sh_ref):
        i = pl.program_id(0)
        _fill_causal(i, buf_ref, x_ref, halo, tm)

        def emit(rc, lc, lw, rows, acc):
            h3_ref[pl.ds(rc, rows), pl.ds(lc, lw)] = acc + b_ref[:, pl.ds(lc, lw)]

        _conv_taps(buf_ref, sh_ref, w_ref, offs, tm, C, emit)
        xhat, _ = _ln_hat(h3_ref[...])
        h4 = xhat * g_ref[...] + beta_ref[...]
        h5_ref[...] = (h4 * _sigmoid(h4)).astype(h5_ref.dtype)

    return _rows(body, n_rows=S, tm=tm, row_ins=[x], full_ins=[w, b, g, beta], row_outs=[(C, F32), (C, BF16)],
                 scratch=[pltpu.VMEM((tm + halo, C), F32), _shift_scratch(offs, tm + halo, C)], name=name)


def conv_act_fwd(gp, u, w, b, *, name):
    S, C = gp.shape
    K = w.shape[0]
    halo = _halo(K)
    tm = _pick(S, 256, SUBLANES)
    offs = [halo - (K - 1) + k for k in range(K)]

    def body(n, x_ref, u_ref, w_ref, b_ref, g_ref, hh_ref, buf_ref, sh_ref):
        i = pl.program_id(0)
        _fill_causal(i, buf_ref, x_ref, halo, tm)

        def emit(rc, lc, lw, rows, acc):
            gv = acc + b_ref[:, pl.ds(lc, lw)]
            g_ref[pl.ds(rc, rows), pl.ds(lc, lw)] = gv.astype(g_ref.dtype)
            hh_ref[pl.ds(rc, rows), pl.ds(lc, lw)] = (gv * _sigmoid(gv) * u_ref[pl.ds(rc, rows), pl.ds(lc, lw)].astype(F32)).astype(hh_ref.dtype)

        _conv_taps(buf_ref, sh_ref, w_ref, offs, tm, C, emit)

    return _rows(body, n_rows=S, tm=tm, row_ins=[gp, u], full_ins=[w, b], row_outs=[(C, BF16), (C, BF16)],
                 scratch=[pltpu.VMEM((tm + halo, C), F32), _shift_scratch(offs, tm + halo, C)], name=name)


def conv_bwd_x(dy, w, *, out_dtype, name):
    S, C = dy.shape
    K = w.shape[0]
    halo = _halo(K)
    tm = _pick(S, 256, SUBLANES)
    offs = [K - 1 - k for k in range(K)]

    def body(n, dy_ref, w_ref, dx_ref, buf_ref, sh_ref):
        i = pl.program_id(0)

        @pl.when(i == 0)
        def _():
            buf_ref[pl.ds(tm, halo), :] = jnp.zeros((halo, C), F32)

        @pl.when(i > 0)
        def _():
            buf_ref[pl.ds(tm, halo), :] = buf_ref[pl.ds(0, halo), :]

        buf_ref[pl.ds(0, tm), :] = dy_ref[...].astype(F32)

        def emit(rc, lc, lw, rows, acc):
            dx_ref[pl.ds(rc, rows), pl.ds(lc, lw)] = acc.astype(dx_ref.dtype)

        _conv_taps(buf_ref, sh_ref, w_ref, offs, tm, C, emit)

    return _rows(body, n_rows=S, tm=tm, row_ins=[dy], full_ins=[w], row_outs=[(C, out_dtype)],
                 scratch=[pltpu.VMEM((tm + halo, C), F32), _shift_scratch(offs, tm + halo, C)], reverse=True, name=name)[0]


def conv_bwd_w(x, dy, K, *, name):
    S, C = x.shape
    halo = _halo(K)
    tm = _pick(S, 256, SUBLANES)
    offs = [halo - (K - 1) + k for k in range(K)]
    rows = min(CONV_ROWS, tm)

    def body(n, x_ref, dy_ref, dw_ref, db_ref, buf_ref, acc_ref, sh_ref):
        i = pl.program_id(0)
        _fill_causal(i, buf_ref, x_ref, halo, tm)
        _make_shifted(buf_ref, sh_ref, offs)

        @pl.when(i == 0)
        def _():
            acc_ref[...] = jnp.zeros_like(acc_ref)

        for lc in range(0, C, CONV_LANES):
            lw = min(CONV_LANES, C - lc)
            for k in range(K):
                s = None
                for rc in range(0, tm, rows):
                    t = dy_ref[pl.ds(rc, rows), pl.ds(lc, lw)].astype(F32) * _tap(buf_ref, sh_ref, offs, k, rc, rows, lc, lw)
                    s = t if s is None else s + t
                s8 = s[0:SUBLANES]
                for q in range(1, rows // SUBLANES):
                    s8 = s8 + s[q * SUBLANES:(q + 1) * SUBLANES]
                acc_ref[pl.ds(k * SUBLANES, SUBLANES), pl.ds(lc, lw)] += s8
        _acc(i, db_ref, _colsum(dy_ref[...].astype(F32)))

        @pl.when(i == n - 1)
        def _():
            for k in range(K):
                dw_ref[pl.ds(k, 1), :] = _colsum(acc_ref[pl.ds(k * SUBLANES, SUBLANES), :])

    return _rows(body, n_rows=S, tm=tm, row_ins=[x, dy], acc_outs=[((K, C), F32), ((1, C), F32)],
                 scratch=[pltpu.VMEM((tm + halo, C), F32), pltpu.VMEM((K * SUBLANES, C), F32),
                          _shift_scratch(offs, tm + halo, C)], name=name)


def ln_silu_bwd(h3, g, beta, dh5, *, name):
    S, C = h3.shape

    def body(n, h_ref, d_ref, g_ref, beta_ref, dh_ref, dg_ref, db_ref):
        i = pl.program_id(0)
        xhat, rstd = _ln_hat(h_ref[...])
        h4 = xhat * g_ref[...] + beta_ref[...]
        sg = _sigmoid(h4)
        dh4 = d_ref[...] * sg * (1.0 + h4 * (1.0 - sg))
        dh_ref[...] = _ln_back(xhat, rstd, g_ref[...], dh4)
        _acc(i, dg_ref, _colsum(dh4 * xhat))
        _acc(i, db_ref, _colsum(dh4))

    return _rows(body, n_rows=S, tm=_pick(S, ROW_BLOCK, SUBLANES), row_ins=[h3, dh5], full_ins=[g, beta],
                 row_outs=[(C, F32)], acc_outs=[((1, C), F32)] * 2, name=name)


def ffn_gate_bwd(dhh, u, g, gp, w, *, name):
    S, C = u.shape
    K = w.shape[0]
    halo = _halo(K)
    tm = _pick(S, 256, SUBLANES)
    offs = [K - 1 - k for k in range(K)]
    rows = min(CONV_ROWS, tm)

    def body(n, d_ref, u_ref, g_ref, gp_ref, w_ref, du_ref, dgp_ref, dw_ref, db_ref, buf_ref, sh_ref, acc_ref):
        i = pl.program_id(0)

        @pl.when(i == 0)
        def _():
            buf_ref[pl.ds(tm, halo), :] = jnp.zeros((halo, C), F32)
            acc_ref[...] = jnp.zeros_like(acc_ref)

        @pl.when(i > 0)
        def _():
            buf_ref[pl.ds(tm, halo), :] = buf_ref[pl.ds(0, halo), :]

        d, gv = d_ref[...].astype(F32), g_ref[...].astype(F32)
        sg = _sigmoid(gv)
        du_ref[...] = (d * gv * sg).astype(du_ref.dtype)
        dg = d * u_ref[...].astype(F32) * sg * (1.0 + gv * (1.0 - sg))
        buf_ref[pl.ds(0, tm), :] = dg
        _acc(i, db_ref, _colsum(dg))

        def emit(rc, lc, lw, nrows, acc):
            dgp_ref[pl.ds(rc, nrows), pl.ds(lc, lw)] = acc.astype(dgp_ref.dtype)

        _conv_taps(buf_ref, sh_ref, w_ref, offs, tm, C, emit)
        for lc in range(0, C, CONV_LANES):
            lw = min(CONV_LANES, C - lc)
            for k in range(K):
                s_ = None
                for rc in range(0, tm, rows):
                    t = gp_ref[pl.ds(rc, rows), pl.ds(lc, lw)].astype(F32) * _tap(buf_ref, sh_ref, offs, k, rc, rows, lc, lw)
                    s_ = t if s_ is None else s_ + t
                s8 = s_[0:SUBLANES]
                for q in range(1, rows // SUBLANES):
                    s8 = s8 + s_[q * SUBLANES:(q + 1) * SUBLANES]
                acc_ref[pl.ds(k * SUBLANES, SUBLANES), pl.ds(lc, lw)] += s8

        @pl.when(i == n - 1)
        def _():
            for k in range(K):
                dw_ref[pl.ds(k, 1), :] = _colsum(acc_ref[pl.ds(k * SUBLANES, SUBLANES), :])

    return _rows(body, n_rows=S, tm=tm, row_ins=[dhh, u, g, gp], full_ins=[w], row_outs=[(C, BF16), (C, BF16)],
                 acc_outs=[((K, C), F32), ((1, C), F32)],
                 scratch=[pltpu.VMEM((tm + halo, C), F32), _shift_scratch(offs, tm + halo, C),
                          pltpu.VMEM((K * SUBLANES, C), F32)], reverse=True, name=name)


def ln_ple_bwd(r, g, dy, pgl, pp, *, name):
    S, D = r.shape

    def body(n, r_ref, dy_ref, l_ref, p_ref, g_ref, dr_ref, dpp_ref, dpl_ref, dg_ref, db_ref):
        i = pl.program_id(0)
        xhat, rstd = _ln_hat(r_ref[...])
        dy_v = dy_ref[...]
        dr = _ln_back(xhat, rstd, g_ref[...], dy_v)
        dr_ref[...] = dr
        sg = _sigmoid(l_ref[...].astype(F32))
        dpp_ref[...] = (dr * sg).astype(dpp_ref.dtype)
        dpl_ref[...] = (dr * p_ref[...].astype(F32) * sg * (1.0 - sg)).astype(dpl_ref.dtype)
        _acc(i, dg_ref, _colsum(dy_v * xhat))
        _acc(i, db_ref, _colsum(dy_v))

    return _rows(body, n_rows=S, tm=_pick(S, ROW_BLOCK, SUBLANES), row_ins=[r, dy, pgl, pp], full_ins=[g],
                 row_outs=[(D, F32), (D, BF16), (D, BF16)], acc_outs=[((1, D), F32)] * 2, name=name)


def loss_grad(y, target, *, name):
    S, D = y.shape

    def body(n, y_ref, t_ref, dy_ref, l_ref):
        i = pl.program_id(0)
        e = y_ref[...] - t_ref[...]
        dy_ref[...] = e * (1.0 / D)
        s = jnp.sum(_colsum(e * e), axis=1, keepdims=True) * (0.5 / D)
        _acc(i, l_ref, jnp.broadcast_to(s, (1, LANES)))

    return _rows(body, n_rows=S, tm=_pick(S, ROW_BLOCK, SUBLANES), row_ins=[y, target], row_outs=[(D, F32)],
                 acc_outs=[((1, LANES), F32)], name=name)


def _key_step(S):
    return min(512, S // 2)


EXIT_LOG = -110.0


def _attn_consts():
    lane = lax.broadcasted_iota(jnp.int32, (1, LANES), 1)
    heads = (lane < HEAD_DIM, lane >= HEAD_DIM)
    row = lax.broadcasted_iota(jnp.int32, (Q_BLOCK, Q_BLOCK), 0)
    col = lax.broadcasted_iota(jnp.int32, (Q_BLOCK, Q_BLOCK), 1)
    causal = jnp.concatenate([col < row] * 2, axis=0)
    return heads, row, col, causal


def _tri(cond):
    return jnp.where(cond, 1.0, 0.0).astype(BF16)


def _keysum2(x, tri):
    hi = x.astype(BF16)
    lo = (x - hi.astype(F32)).astype(BF16)
    return jnp.dot(jnp.concatenate([hi, lo], axis=1), jnp.concatenate([tri, tri], axis=0),
                   preferred_element_type=F32)


def _stack_heads(x, heads):
    return jnp.concatenate([jnp.where(m, x, jnp.zeros_like(x)) for m in heads], axis=0)


def _log1m_beta(z):
    return -(jnp.maximum(z, 0.0) + jnp.log(1.0 + jnp.exp(-jnp.abs(z))))


def attn_fwd(q, k, v, *, name):
    S, D = q.shape
    nb = S // Q_BLOCK
    tk = _key_step(S)
    nkb = tk // Q_BLOCK
    scale = 1.0 / math.sqrt(HEAD_DIM)

    def body(q_ref, k_ref, v_ref, o_ref, tot_ref, seen_ref, vm_ref):
        heads, row, col, causal = _attn_consts()
        above = _tri(row > col)
        for h in range(2):
            vm_ref[h] = jnp.where(heads[h], v_ref[...], jnp.zeros_like(v_ref[...]))

        def step(sb, carry, qq, nblk, diag):
            acc, cl = carry
            c0 = pl.multiple_of(sb * tk, tk)
            z = lax.dot_general(qq, k_ref[pl.ds(c0, nblk * Q_BLOCK), :], _DOT_DIMS["nt"], preferred_element_type=F32)
            zl, es, rs = [], [], []
            for jb in range(nblk):
                zb = z[:, jb * Q_BLOCK:(jb + 1) * Q_BLOCK]
                lr = _log1m_beta(zb)
                l = jnp.where(causal, lr, 0.0) if diag and jb == nblk - 1 else lr
                zl.append(zb + lr)
                es.append(_keysum2(l, above))
                rs.append(jnp.sum(l, axis=1, keepdims=True))
            a = [None] * nblk
            for jb in reversed(range(nblk)):
                ab = jnp.exp(zl[jb] + es[jb] + cl)
                if diag and jb == nblk - 1:
                    ab = jnp.where(causal, ab, 0.0)
                a[jb] = ab.astype(BF16)
                cl = cl + rs[jb]
            a = jnp.concatenate(a, axis=1)
            for h in range(2):
                acc = acc + jnp.dot(a[h * Q_BLOCK:(h + 1) * Q_BLOCK], vm_ref[h, pl.ds(c0, nblk * Q_BLOCK), :],
                                    preferred_element_type=F32)
            return acc, cl

        def qblock(i, _):
            r0 = pl.multiple_of(i * Q_BLOCK, Q_BLOCK)
            qq = _stack_heads(q_ref[pl.ds(r0, Q_BLOCK), :] * scale, heads)
            last = i // nkb
            carry = (jnp.zeros((Q_BLOCK, LANES), F32), jnp.zeros((2 * Q_BLOCK, 1), F32))
            carry = lax.switch(i % nkb, [functools.partial(step, last, qq=qq, nblk=m + 1, diag=True) for m in range(nkb)],
                               carry)

            def more(c):
                return jnp.logical_and(c[0] < last, jnp.max(c[2]) >= EXIT_LOG)

            def left(c):
                return (c[0] + 1, *step(last - 1 - c[0], c[1:], qq, nkb, False))

            seen, acc, cl = lax.while_loop(more, left, (jnp.int32(0), *carry))
            o_ref[pl.ds(r0, Q_BLOCK), :] = acc.astype(o_ref.dtype)
            tot_ref[pl.ds(r0, Q_BLOCK), :] = jnp.where(heads[0], cl[:Q_BLOCK], cl[Q_BLOCK:])
            seen_ref[pl.ds(pl.multiple_of(i * SUBLANES, SUBLANES), SUBLANES), :] = jnp.full((SUBLANES, LANES), seen, F32)
            return 0

        lax.fori_loop(0, nb, qblock, 0)

    spec = pl.BlockSpec((S, LANES), lambda h: (0, h))
    seen_spec = pl.BlockSpec((nb * SUBLANES, LANES), lambda h: (0, h))
    return pl.pallas_call(body, grid=(D // LANES,), in_specs=[spec] * 3, out_specs=[spec, spec, seen_spec],
                          out_shape=[jax.ShapeDtypeStruct((S, D), BF16), jax.ShapeDtypeStruct((S, D), F32),
                                     jax.ShapeDtypeStruct((nb * SUBLANES, D), F32)],
                          scratch_shapes=[pltpu.VMEM((2, S, LANES), BF16)], compiler_params=_cp("parallel"),
                          name=name)(q, k, v)


def attn_bwd(q, k, v, tot, seen, do, dk0, dv0, *, name):
    S, D = q.shape
    nb = S // Q_BLOCK
    tk = _key_step(S)
    nkb = tk // Q_BLOCK
    scale = 1.0 / math.sqrt(HEAD_DIM)
    has_init = dk0 is not None

    def body(*refs):
        if has_init:
            q_ref, k_ref, v_ref, tot_ref, seen_ref, do_ref, dk0_ref, dv0_ref, dq_ref, dk_ref, dv_ref, km_ref = refs
            dk_ref[...] = dk0_ref[...]
            dv_ref[...] = dv0_ref[...]
        else:
            q_ref, k_ref, v_ref, tot_ref, seen_ref, do_ref, dq_ref, dk_ref, dv_ref, km_ref = refs
            dk_ref[...] = jnp.zeros_like(dk_ref)
            dv_ref[...] = jnp.zeros_like(dv_ref)
        heads, row, col, causal = _attn_consts()
        upto = _tri(row <= col)
        before = _tri(row < col)
        for h in range(2):
            km_ref[h] = jnp.where(heads[h], k_ref[...], jnp.zeros_like(k_ref[...]))

        def step(sb, carry, qq, dd, totl, nblk, diag):
            dq, pl_, pg = carry
            c0 = pl.multiple_of(sb * tk, tk)
            keys = pl.ds(c0, nblk * Q_BLOCK)
            z = lax.dot_general(qq, k_ref[keys, :], _DOT_DIMS["nt"], preferred_element_type=F32)
            da = lax.dot_general(dd, v_ref[keys, :], _DOT_DIMS["nt"], preferred_element_type=F32)
            blocks = range(nblk)
            masked = [diag and jb == nblk - 1 for jb in blocks]
            zb = [z[:, jb * Q_BLOCK:(jb + 1) * Q_BLOCK] for jb in blocks]
            lr = [_log1m_beta(zb[jb]) for jb in blocks]
            l = [jnp.where(causal, lr[jb], 0.0) if masked[jb] else lr[jb] for jb in blocks]
            lsum = [_keysum2(l[jb], upto) for jb in blocks]
            lrow = [jnp.sum(l[jb], axis=1, keepdims=True) for jb in blocks]
            a, g = [None] * nblk, [None] * nblk
            for jb in blocks:
                ab = jnp.exp(zb[jb] + lr[jb] + (totl - pl_ - lsum[jb]))
                if masked[jb]:
                    ab = jnp.where(causal, ab, 0.0)
                g[jb] = ab * da[:, jb * Q_BLOCK:(jb + 1) * Q_BLOCK]
                a[jb] = ab.astype(BF16)
                pl_ = pl_ + lrow[jb]
            gsum = [jnp.dot(g[jb].astype(BF16), before, preferred_element_type=F32) for jb in blocks]
            grow = [jnp.sum(g[jb], axis=1, keepdims=True) for jb in blocks]
            dz = [None] * nblk
            for jb in blocks:
                dzb = g[jb] * jnp.exp(lr[jb]) - jnp.exp(zb[jb] + lr[jb]) * (pg + gsum[jb])
                if masked[jb]:
                    dzb = jnp.where(causal, dzb, 0.0)
                dz[jb] = dzb.astype(BF16)
                pg = pg + grow[jb]
            a = jnp.concatenate(a, axis=1)
            dz = jnp.concatenate(dz, axis=1)
            for h in range(2):
                dq = dq + jnp.dot(dz[h * Q_BLOCK:(h + 1) * Q_BLOCK], km_ref[h, keys, :], preferred_element_type=F32)
            dk_ref[keys, :] += lax.dot_general(dz, qq, _DOT_DIMS["tn"], preferred_element_type=F32)
            dv_ref[keys, :] += lax.dot_general(a, dd, _DOT_DIMS["tn"], preferred_element_type=F32)
            return dq, pl_, pg

        def qblock(i, _):
            r0 = pl.multiple_of(i * Q_BLOCK, Q_BLOCK)
            qq = _stack_heads(q_ref[pl.ds(r0, Q_BLOCK), :] * scale, heads)
            dd = _stack_heads(do_ref[pl.ds(r0, Q_BLOCK), :].astype(BF16), heads)
            tot2 = tot_ref[pl.ds(r0, Q_BLOCK), :]
            totl = jnp.concatenate([tot2[:, 0:1], tot2[:, HEAD_DIM:HEAD_DIM + 1]], axis=0)
            last = i // nkb
            zc = jnp.zeros((2 * Q_BLOCK, 1), F32)
            carry = (jnp.zeros((Q_BLOCK, LANES), F32), zc, zc)
            walked = jnp.max(seen_ref[pl.ds(pl.multiple_of(i * SUBLANES, SUBLANES), SUBLANES), :]).astype(jnp.int32)
            first = last - jnp.clip(walked, 0, last)
            carry = lax.fori_loop(first, last, lambda sb, c: step(sb, c, qq, dd, totl, nkb, False), carry)
            carry = lax.switch(i % nkb, [functools.partial(step, last, qq=qq, dd=dd, totl=totl, nblk=m + 1, diag=True)
                                         for m in range(nkb)], carry)
            dq_ref[pl.ds(r0, Q_BLOCK), :] = (carry[0] * scale).astype(dq_ref.dtype)
            return 0

        lax.fori_loop(0, nb, qblock, 0)

    spec = pl.BlockSpec((S, LANES), lambda h: (0, h))
    seen_spec = pl.BlockSpec((nb * SUBLANES, LANES), lambda h: (0, h))
    args = [q, k, v, tot, seen, do] + ([dk0, dv0] if has_init else [])
    return pl.pallas_call(
        body, grid=(D // LANES,), in_specs=[spec] * 4 + [seen_spec] + [spec] * (len(args) - 5), out_specs=[spec] * 3,
        out_shape=[jax.ShapeDtypeStruct((S, D), BF16), jax.ShapeDtypeStruct((S, D), F32), jax.ShapeDtypeStruct((S, D), F32)],
        scratch_shapes=[pltpu.VMEM((2, S, LANES), BF16)], compiler_params=_cp("parallel"), name=name)(*args)


def _dev_index(px, py, pc):
    return 4 * px + 2 * py + pc


def all_gather(bufs):
    nb = len(bufs)

    def body(*refs):
        ins, outs = refs[:nb], refs[nb:2 * nb]
        send_sems, recv_sems, local_sems = refs[2 * nb:]
        x, y, c = lax.axis_index("x"), lax.axis_index("y"), lax.axis_index("c")
        me, sibling = (x, y, c), (x, y, 1 - c)
        chips = [(1 - x, y), (x, 1 - y), (1 - x, 1 - y)]

        def copy(b, k, block, to, from_input=False):
            slot = outs[b].at[_dev_index(*block)]
            return pltpu.make_async_remote_copy(
                src_ref=ins[b] if from_input else slot, dst_ref=slot,
                send_sem=send_sems.at[7 * b + k], recv_sem=recv_sems.at[7 * b + k], device_id=to, device_id_type=MESH)

        mine = [pltpu.make_async_copy(ins[b], outs[b].at[_dev_index(*me)], local_sems.at[b]) for b in range(nb)]
        for cp in mine:
            cp.start()
        first = []
        for b in range(nb):
            first.append(copy(b, 0, me, sibling, from_input=True))
            first += [copy(b, 1 + j, me, (*chip, c), from_input=True) for j, chip in enumerate(chips)]
        for cp in first:
            cp.start()
        passed = []
        for j, chip in enumerate(chips):
            for b in range(nb):
                copy(b, 1 + j, (*chip, c), me).wait_recv()
                fwd = copy(b, 4 + j, (*chip, c), sibling)
                fwd.start()
                passed.append(fwd)
        for b in range(nb):
            copy(b, 0, sibling, me).wait_recv()
            for j, chip in enumerate(chips):
                copy(b, 4 + j, (*chip, 1 - c), me).wait_recv()
        for cp in first + passed:
            cp.wait_send()
        for cp in mine:
            cp.wait()

    any_spec = pl.BlockSpec(memory_space=pl.ANY)
    return pl.pallas_call(
        body, in_specs=[any_spec] * nb, out_specs=[any_spec] * nb,
        out_shape=[jax.ShapeDtypeStruct((N_DEV,) + b.shape, b.dtype) for b in bufs],
        scratch_shapes=[pltpu.SemaphoreType.DMA((7 * nb,)), pltpu.SemaphoreType.DMA((7 * nb,)),
                        pltpu.SemaphoreType.DMA((nb,))],
        name="all_gather_weights")(*bufs)


def _sources(groups):
    return [s for g in groups for (s, _) in g[3]]


def _layout(groups, refs):
    out, si = [], 0
    for g, (_, _, _, lst) in enumerate(groups):
        for (s, off) in lst:
            out.append((g, refs[si], off, s.shape[-2]))
            si += 1
    return out


def pair_exchange(groups):
    srcs = _sources(groups)
    ns, ng = len(srcs), len(groups)

    def body(*refs):
        outs = refs[ns:ns + ng]
        send_sems, recv_sems = refs[ns + ng:]
        x, y, c = lax.axis_index("x"), lax.axis_index("y"), lax.axis_index("c")
        sibling = (x, y, 1 - c)
        for (g, ref, off, r) in _layout(groups, refs[:ns]):
            for q in range(N_DEV // 2):
                pltpu.make_async_remote_copy(
                    src_ref=ref.at[2 * q + 1 - c], dst_ref=outs[g].at[q, pl.ds(off, r)], send_sem=send_sems.at[g],
                    recv_sem=recv_sems.at[g], device_id=sibling, device_id_type=MESH).start()
        whole = [pltpu.make_async_remote_copy(
            src_ref=outs[g], dst_ref=outs[g], send_sem=send_sems.at[g], recv_sem=recv_sems.at[g],
            device_id=sibling, device_id_type=MESH) for g in range(ng)]
        for w in whole:
            w.wait_recv()
        for w in whole:
            w.wait_send()

    any_spec = pl.BlockSpec(memory_space=pl.ANY)
    return pl.pallas_call(
        body, in_specs=[any_spec] * ns, out_specs=[any_spec] * ng,
        out_shape=[jax.ShapeDtypeStruct((N_DEV // 2, r, w), dt) for (r, w, dt, _) in groups],
        scratch_shapes=[pltpu.SemaphoreType.DMA((ng,)), pltpu.SemaphoreType.DMA((ng,))],
        name="pair_exchange")(*srcs)


def pair_sum(src, got, off, core, *, name):
    _, r, W = src.shape
    tr = _row_tile(r, off, 1024)
    o = off // tr

    def body(c_ref, s_ref, g_ref, o_ref):
        o_ref[...] = (s_ref[...].astype(F32) + g_ref[...].astype(F32)).astype(o_ref.dtype)

    return pl.pallas_call(
        body,
        grid_spec=pltpu.PrefetchScalarGridSpec(
            num_scalar_prefetch=1, grid=(N_DEV // 2, r // tr),
            in_specs=[pl.BlockSpec((None, None, tr, W), lambda q, i, c: (q, c[0], i, 0)),
                      pl.BlockSpec((None, tr, W), lambda q, i, c: (q, i + o, 0))],
            out_specs=pl.BlockSpec((None, tr, W), lambda q, i, c: (q, i, 0))),
        out_shape=jax.ShapeDtypeStruct((N_DEV // 2, r, W), src.dtype), compiler_params=_cp("parallel", "parallel"),
        name=name)(core, src.reshape(N_DEV // 2, 2, r, W), got)


def chip_exchange(groups):
    srcs = _sources(groups)
    ns, ng = len(srcs), len(groups)

    def body(*refs):
        outs = refs[ns:ns + ng]
        send_sems, recv_sems, local_sems = refs[ns + ng:]
        x, y, c = lax.axis_index("x"), lax.axis_index("y"), lax.axis_index("c")
        me = 2 * x + y
        layout = _layout(groups, refs[:ns])
        mine = [pltpu.make_async_copy(ref.at[me], outs[g].at[me, pl.ds(off, r)], local_sems.at[i])
                for i, (g, ref, off, r) in enumerate(layout)]
        for cp in mine:
            cp.start()
        slots = []
        for flip in range(1, N_DEV // 2):
            px, py = (1 - x if flip & 2 else x), (1 - y if flip & 1 else y)
            peer, pq = (px, py, c), 2 * px + py
            for (g, ref, off, r) in layout:
                k = 3 * g + flip - 1
                pltpu.make_async_remote_copy(
                    src_ref=ref.at[pq], dst_ref=outs[g].at[me, pl.ds(off, r)], send_sem=send_sems.at[k],
                    recv_sem=recv_sems.at[k], device_id=peer, device_id_type=MESH).start()
            for g in range(ng):
                k = 3 * g + flip - 1
                slots.append(pltpu.make_async_remote_copy(
                    src_ref=outs[g].at[pq], dst_ref=outs[g].at[pq], send_sem=send_sems.at[k],
                    recv_sem=recv_sems.at[k], device_id=peer, device_id_type=MESH))
        for w in slots:
            w.wait_recv()
        for w in slots:
            w.wait_send()
        for cp in mine:
            cp.wait()

    any_spec = pl.BlockSpec(memory_space=pl.ANY)
    return pl.pallas_call(
        body, in_specs=[any_spec] * ns, out_specs=[any_spec] * ng,
        out_shape=[jax.ShapeDtypeStruct((N_DEV // 2, r, w), dt) for (r, w, dt, _) in groups],
        scratch_shapes=[pltpu.SemaphoreType.DMA((3 * ng,)), pltpu.SemaphoreType.DMA((3 * ng,)),
                        pltpu.SemaphoreType.DMA((ns,))],
        name="chip_exchange")(*srcs)


def _row_tile(rows, off, target):
    if rows <= target and off % rows == 0:
        return rows
    for t in (1024, 512, 256, 128, 64, 32, 16, 8):
        if t <= target and rows % t == 0 and off % t == 0:
            return t
    raise ValueError((rows, off))


def adamw(recv, off, w, m, v, *, name):
    rows, W = w.shape
    nslot = recv.shape[0]
    tr = _row_tile(rows, off, 512)
    o = off // tr
    c1 = 1.0 - ADAM_B1 ** ADAM_STEP
    c2 = 1.0 - ADAM_B2 ** ADAM_STEP

    def body(r_ref, w_ref, m_ref, v_ref, g_ref, d_ref, mo_ref, vo_ref):
        g = r_ref[0].astype(F32)
        for j in range(1, nslot):
            g = g + r_ref[j].astype(F32)
        mn = ADAM_B1 * m_ref[...] + (1.0 - ADAM_B1) * g
        vn = ADAM_B2 * v_ref[...] + (1.0 - ADAM_B2) * (g * g)
        g_ref[...] = g
        mo_ref[...] = mn
        vo_ref[...] = vn
        d_ref[...] = -ADAM_LR * ((mn / c1) / (jnp.sqrt(vn / c2) + ADAM_EPS) + ADAM_WD * w_ref[...])

    spec = pl.BlockSpec((tr, W), lambda i: (i, 0))
    return pl.pallas_call(
        body, grid=(rows // tr,), in_specs=[pl.BlockSpec((nslot, tr, W), lambda i: (0, i + o, 0)), spec, spec, spec],
        out_specs=[spec] * 4, out_shape=[jax.ShapeDtypeStruct((rows, W), F32)] * 4,
        compiler_params=_cp("parallel"), name=name)(recv, w, m, v)


def join_columns(gathered, off, K, *, name):
    _, _, n = gathered.shape
    tr = _row_tile(K, off, 512)
    o = off // tr

    def body(i_ref, o_ref):
        for d in range(N_DEV):
            o_ref[:, d * n:(d + 1) * n] = i_ref[d]

    return pl.pallas_call(
        body, grid=(K // tr,), in_specs=[pl.BlockSpec((N_DEV, tr, n), lambda i: (0, i + o, 0))],
        out_specs=pl.BlockSpec((tr, N_DEV * n), lambda i: (i, 0)),
        out_shape=jax.ShapeDtypeStruct((K, N_DEV * n), gathered.dtype), compiler_params=_cp("parallel"),
        name=name)(gathered)


def split_columns(full, *, name):
    K, N = full.shape
    n = N // N_DEV
    tr = _row_tile(K, 0, 512)

    def body(i_ref, o_ref):
        for d in range(N_DEV):
            o_ref[d] = i_ref[:, d * n:(d + 1) * n].astype(o_ref.dtype)

    return pl.pallas_call(
        body, grid=(K // tr,), in_specs=[pl.BlockSpec((tr, N), lambda i: (i, 0))],
        out_specs=pl.BlockSpec((N_DEV, tr, n), lambda i: (0, i, 0)),
        out_shape=jax.ShapeDtypeStruct((N_DEV, K, n), BF16), compiler_params=_cp("parallel"), name=name)(full)


def _pack(arrs, dtype, row_mult):
    flat = jnp.concatenate([a.reshape(-1).astype(dtype) for a in arrs])
    rows = -(-flat.shape[0] // PACK_W)
    rows = -(-rows // row_mult) * row_mult
    return jnp.pad(flat, (0, rows * PACK_W - flat.shape[0])).reshape(rows, PACK_W)


def _pack_dev(arrs, dtype, row_mult):
    flat = jnp.concatenate([a.reshape(N_DEV, -1).astype(dtype) for a in arrs], axis=1)
    rows = -(-flat.shape[1] // PACK_W)
    rows = -(-rows // row_mult) * row_mult
    return jnp.pad(flat, ((0, 0), (0, rows * PACK_W - flat.shape[1]))).reshape(N_DEV, rows, PACK_W)


def _unpack(buf, shapes):
    lead = buf.shape[:-2]
    flat = buf.reshape(lead + (-1,))
    outs, off = [], 0
    for s in shapes:
        n = math.prod(s)
        outs.append(flat[..., off:off + n].reshape(lead + tuple(s)))
        off += n
    return outs


def _join(g, axis):
    g = jnp.moveaxis(g, 0, axis)
    return g.reshape(g.shape[:axis] + (g.shape[axis] * g.shape[axis + 1],) + g.shape[axis + 2:])


def _split(full, axis):
    s = full.shape
    g = full.reshape(s[:axis] + (N_DEV, s[axis] // N_DEV) + s[axis + 1:])
    return jnp.moveaxis(g, axis, 0)


def kernel(x, p, a_pw1_w, a_pw1_b, a_dw_w, a_dw_b, a_ln_g, a_ln_b, a_pw2_w, a_pw2_b, b_wq, kv_wk, kv_wv, b_wo, ln_mix_g, ln_mix_b, ffn_w_up, ffn_w_gate, ffn_conv_w, ffn_conv_b, ffn_w_down, ple_w_gate, ple_w_proj, ln_ffn_g, ln_ffn_b, loss_target, m_a_pw1_w, m_a_pw1_b, m_a_dw_w, m_a_dw_b, m_a_ln_g, m_a_ln_b, m_a_pw2_w, m_a_pw2_b, m_b_wq, m_kv_wk, m_kv_wv, m_b_wo, m_ln_mix_g, m_ln_mix_b, m_ffn_w_up, m_ffn_w_gate, m_ffn_conv_w, m_ffn_conv_b, m_ffn_w_down, m_ple_w_gate, m_ple_w_proj, m_ln_ffn_g, m_ln_ffn_b, v_a_pw1_w, v_a_pw1_b, v_a_dw_w, v_a_dw_b, v_a_ln_g, v_a_ln_b, v_a_pw2_w, v_a_pw2_b, v_b_wq, v_kv_wk, v_kv_wv, v_b_wo, v_ln_mix_g, v_ln_mix_b, v_ffn_w_up, v_ffn_w_gate, v_ffn_conv_w, v_ffn_conv_b, v_ffn_w_down, v_ple_w_gate, v_ple_w_proj, v_ln_ffn_g, v_ln_ffn_b):
    local = dict(a_pw1_w=a_pw1_w, a_pw1_b=a_pw1_b, a_dw_w=a_dw_w, a_dw_b=a_dw_b, a_ln_g=a_ln_g, a_ln_b=a_ln_b, a_pw2_w=a_pw2_w, a_pw2_b=a_pw2_b, b_wq=b_wq, kv_wk=kv_wk, kv_wv=kv_wv, b_wo=b_wo, ln_mix_g=ln_mix_g, ln_mix_b=ln_mix_b, ffn_w_up=ffn_w_up, ffn_w_gate=ffn_w_gate, ffn_conv_w=ffn_conv_w, ffn_conv_b=ffn_conv_b, ffn_w_down=ffn_w_down, ple_w_gate=ple_w_gate, ple_w_proj=ple_w_proj, ln_ffn_g=ln_ffn_g, ln_ffn_b=ln_ffn_b)
    mom1 = dict(a_pw1_w=m_a_pw1_w, a_pw1_b=m_a_pw1_b, a_dw_w=m_a_dw_w, a_dw_b=m_a_dw_b, a_ln_g=m_a_ln_g, a_ln_b=m_a_ln_b, a_pw2_w=m_a_pw2_w, a_pw2_b=m_a_pw2_b, b_wq=m_b_wq, kv_wk=m_kv_wk, kv_wv=m_kv_wv, b_wo=m_b_wo, ln_mix_g=m_ln_mix_g, ln_mix_b=m_ln_mix_b, ffn_w_up=m_ffn_w_up, ffn_w_gate=m_ffn_w_gate, ffn_conv_w=m_ffn_conv_w, ffn_conv_b=m_ffn_conv_b, ffn_w_down=m_ffn_w_down, ple_w_gate=m_ple_w_gate, ple_w_proj=m_ple_w_proj, ln_ffn_g=m_ln_ffn_g, ln_ffn_b=m_ln_ffn_b)
    mom2 = dict(a_pw1_w=v_a_pw1_w, a_pw1_b=v_a_pw1_b, a_dw_w=v_a_dw_w, a_dw_b=v_a_dw_b, a_ln_g=v_a_ln_g, a_ln_b=v_a_ln_b, a_pw2_w=v_a_pw2_w, a_pw2_b=v_a_pw2_b, b_wq=v_b_wq, kv_wk=v_kv_wk, kv_wv=v_kv_wv, b_wo=v_b_wo, ln_mix_g=v_ln_mix_g, ln_mix_b=v_ln_mix_b, ffn_w_up=v_ffn_w_up, ffn_w_gate=v_ffn_w_gate, ffn_conv_w=v_ffn_conv_w, ffn_conv_b=v_ffn_conv_b, ffn_w_down=v_ffn_w_down, ple_w_gate=v_ple_w_gate, ple_w_proj=v_ple_w_proj, ln_ffn_g=v_ln_ffn_g, ln_ffn_b=v_ln_ffn_b)
    small_names = [n for n, _ in SMALL]
    small_shapes = [local[n].shape for n in small_names]
    repl_shapes = [local[n].shape for n in REPL]

    widths = sorted({local[n].shape[-1] for n, _ in BIG}, reverse=True)
    groups = {w: [n for n, _ in BIG if local[n].shape[-1] == w] for w in widths}
    offset, rows_of = {}, {}
    for w, names in groups.items():
        off = 0
        for n in names:
            offset[n], rows_of[n] = off, math.prod(local[n].shape[:-1])
            off += rows_of[n]
    sends = [jnp.concatenate([local[n].reshape(-1, w).astype(BF16) for n in names]) for w, names in groups.items()]
    gathered = all_gather(sends + [_pack([local[n] for n in small_names], F32, SUBLANES)])
    gath = dict(zip(widths, gathered[:-1]))
    W = {}
    for n, ax in BIG:
        w = local[n].shape[-1]
        nl = local[n].shape[0] if local[n].ndim == 3 else 1
        per = rows_of[n] // nl
        if ax == local[n].ndim - 1:
            W[n] = [join_columns(gath[w], offset[n] + l * per, per, name=f"join_{n}_{l}") for l in range(nl)]
        else:
            W[n] = [gath[w][:, offset[n] + l * per:offset[n] + (l + 1) * per].reshape(N_DEV * per, w) for l in range(nl)]
    for n in ("kv_wk", "kv_wv"):
        W[n] = W[n][0]
    W.update({n: _join(g, ax) for (n, ax), g in zip(SMALL, _unpack(gathered[-1], small_shapes))})
    W.update({n: local[n] for n in REPL})

    xs = x[0]
    S, D = xs.shape
    x_in, r1s, x1s, r2s, us, gps, gs, hhs, pgls, pps = [], [], [], [], [], [], [], [], [], []
    h1s, h2s, h3s, h5s, qs, os_, tots = {}, {}, {}, {}, {}, {}, {}
    kk = vv = None
    for i in range(DEPTH):
        x_in.append(xs)
        if i < N_A:
            h1 = mm(xs, W["a_pw1_w"][i], "nn", bias=W["a_pw1_b"][i][None], out_dtype=BF16, name=f"pw1_{i}")
            h2 = glu_fwd(h1, name=f"glu_{i}")
            h3, h5 = conv_ln_silu_fwd(h2, W["a_dw_w"][i], W["a_dw_b"][i][None], W["a_ln_g"][i][None],
                                      W["a_ln_b"][i][None], name=f"dwconv_{i}")
            mix = mm(h5, W["a_pw2_w"][i], "nn", bias=W["a_pw2_b"][i][None], name=f"pw2_{i}")
            h1s[i], h2s[i], h3s[i], h5s[i] = h1, h2, h3, h5
        else:
            j = i - N_A
            if kk is None:
                kk, vv = mm(xs, W["kv_wk"], "nn", out_dtype=BF16, also=W["kv_wv"], name="proj_kv")
            q = mm(xs, W["b_wq"][j], "nn", out_dtype=BF16, name=f"proj_q_{i}")
            o, tot, seen = attn_fwd(q, kk, vv, name=f"attn_{i}")
            mix = mm(o, W["b_wo"][j], "nn", name=f"proj_o_{i}")
            qs[i], os_[i], tots[i] = q, o, (tot, seen)
        r1, x1 = res_ln(xs, mix, W["ln_mix_g"][i][None], W["ln_mix_b"][i][None], name=f"ln_mix_{i}")
        u, gp = mm(x1, W["ffn_w_up"][i], "nn", out_dtype=BF16, also=W["ffn_w_gate"][i], name=f"ffn_up_gate_{i}")
        g, hh = conv_act_fwd(gp, u, W["ffn_conv_w"][i], W["ffn_conv_b"][i][None], name=f"ffn_conv_{i}")
        f = mm(hh, W["ffn_w_down"][i], "nn", name=f"ffn_down_{i}")
        pgl = mm(x1, W["ple_w_gate"][i], "nn", out_dtype=BF16, name=f"ple_gate_{i}")
        pp = mm(p[i, 0], W["ple_w_proj"][i], "nn", out_dtype=BF16, name=f"ple_proj_{i}")
        r2, xs = res_ln(x1, f, W["ln_ffn_g"][i][None], W["ln_ffn_b"][i][None], ple=(pgl, pp), name=f"ln_ffn_{i}")
        for lst, val in ((r1s, r1), (x1s, x1), (r2s, r2), (us, u), (gps, gp), (gs, g), (hhs, hh), (pgls, pgl), (pps, pp)):
            lst.append(val)

    dx, loss_part = loss_grad(xs, loss_target[0], name="loss")
    G = {n: [None] * local[n].shape[0] for n in WEIGHTS if n not in ("kv_wk", "kv_wv")}
    dk = dv = None
    for i in reversed(range(DEPTH)):
        x1 = x1s[i]
        dr2, dpp, dpgl, G["ln_ffn_g"][i], G["ln_ffn_b"][i] = ln_ple_bwd(r2s[i], W["ln_ffn_g"][i][None], dx, pgls[i], pps[i],
                                                                      name=f"ln_ffn_bwd_{i}")
        dhh = mm(dr2, W["ffn_w_down"][i], "nt", out_dtype=BF16, name=f"ffn_down_dx_{i}")
        G["ffn_w_down"][i] = mm(hhs[i], dr2, "tn", out_dtype=BF16, name=f"ffn_down_dw_{i}")
        G["ple_w_proj"][i] = mm(p[i, 0], dpp, "tn", out_dtype=BF16, name=f"ple_proj_dw_{i}")
        G["ple_w_gate"][i] = mm(x1, dpgl, "tn", out_dtype=BF16, name=f"ple_gate_dw_{i}")
        du, dgp, G["ffn_conv_w"][i], G["ffn_conv_b"][i] = ffn_gate_bwd(dhh, us[i], gs[i], gps[i], W["ffn_conv_w"][i],
                                                                       name=f"ffn_gate_bwd_{i}")
        G["ffn_w_up"][i], G["ffn_w_gate"][i] = mm(x1, du, "tn", out_dtype=BF16, also=dgp, name=f"ffn_up_gate_dw_{i}")
        dx1 = mm(du, W["ffn_w_up"][i], "nt", add=dr2, add_scale=DN_ALPHA, plus=(dgp, W["ffn_w_gate"][i]),
                 name=f"ffn_up_gate_dx_{i}")
        dx1 = mm(dpgl, W["ple_w_gate"][i], "nt", add=dx1, name=f"ple_gate_dx_{i}")
        dr1, G["ln_mix_g"][i], G["ln_mix_b"][i], dr1_sum = ln_bwd(r1s[i], W["ln_mix_g"][i][None], dx1, name=f"ln_mix_bwd_{i}")
        if i < N_A:
            G["a_pw2_w"][i] = mm(h5s[i], dr1, "tn", out_dtype=BF16, name=f"pw2_dw_{i}")
            G["a_pw2_b"][i] = dr1_sum
            dh5 = mm(dr1, W["a_pw2_w"][i], "nt", name=f"pw2_dx_{i}")
            dh3, G["a_ln_g"][i], G["a_ln_b"][i] = ln_silu_bwd(h3s[i], W["a_ln_g"][i][None], W["a_ln_b"][i][None], dh5,
                                                             name=f"dwconv_ln_bwd_{i}")
            dh2 = conv_bwd_x(dh3, W["a_dw_w"][i], out_dtype=F32, name=f"dwconv_dx_{i}")
            G["a_dw_w"][i], G["a_dw_b"][i] = conv_bwd_w(h2s[i], dh3, CONV_W, name=f"dwconv_dw_{i}")
            dh1, G["a_pw1_b"][i] = glu_bwd(h1s[i], dh2, name=f"glu_bwd_{i}")
            G["a_pw1_w"][i] = mm(x_in[i], dh1, "tn", out_dtype=BF16, name=f"pw1_dw_{i}")
            dx = mm(dh1, W["a_pw1_w"][i], "nt", add=dr1, add_scale=DN_ALPHA, name=f"pw1_dx_{i}")
        else:
            j = i - N_A
            G["b_wo"][j] = mm(os_[i], dr1, "tn", out_dtype=BF16, name=f"proj_o_dw_{i}")
            do = mm(dr1, W["b_wo"][j], "nt", out_dtype=BF16, name=f"proj_o_dx_{i}")
            dq, dk, dv = attn_bwd(qs[i], kk, vv, *tots[i], do, dk, dv, name=f"attn_bwd_{i}")
            G["b_wq"][j] = mm(x_in[i], dq, "tn", out_dtype=BF16, name=f"proj_q_dw_{i}")
            dx = mm(dq, W["b_wq"][j], "nt", add=dr1, add_scale=DN_ALPHA, name=f"proj_q_dx_{i}")
            if j == 0:
                G["kv_wk"], G["kv_wv"] = mm(x_in[i], dk, "tn", out_dtype=BF16, also=dv, name="proj_kv_dw")
                dx = mm(dk, W["kv_wk"], "nt", add=dx, plus=(dv, W["kv_wv"]), name="proj_kv_dx")
    grad_x = dx[None]
    shard_axis = dict(BIG + SMALL)
    for n in small_names + list(REPL):
        full = list(local[n].shape)
        if n in shard_axis:
            full[shard_axis[n]] *= N_DEV
        G[n] = jnp.stack(G[n]).reshape(full)

    n_small = sum(math.prod(s) for s in small_shapes)
    n_repl = sum(math.prod(s) for s in repl_shapes)
    repl_flat = jnp.concatenate([G[n].reshape(-1) for n in REPL] + [loss_part.reshape(-1)[:1]])
    send_small = _pack_dev([_split(G[n], ax) for n, ax in SMALL] + [jnp.broadcast_to(repl_flat, (N_DEV, n_repl + 1))],
                           F32, SUBLANES)
    ex_groups = []
    for w, names in groups.items():
        lst = []
        for n in names:
            layers = G[n] if isinstance(G[n], list) else [G[n]]
            per = rows_of[n] // len(layers)
            for l, g in enumerate(layers):
                if shard_axis[n] == local[n].ndim - 1:
                    src = split_columns(g, name=f"split_{n}_{l}")
                else:
                    src = g.reshape(N_DEV, per, w)
                lst.append((src, offset[n] + l * per))
        ex_groups.append((sum(rows_of[n] for n in names), w, BF16, lst))
    ex_groups.append((send_small.shape[1], PACK_W, F32, [(send_small, 0)]))
    gots = pair_exchange(ex_groups)
    core = lax.axis_index("c").astype(jnp.int32).reshape(1)
    sum_groups = [(rows, w, dt, [(pair_sum(src, got, off, core, name=f"pair_sum_{gi}_{si}"), off) for si, (src, off) in enumerate(lst)])
                  for gi, ((rows, w, dt, lst), got) in enumerate(zip(ex_groups, gots))]
    recvs = chip_exchange(sum_groups)
    recv = dict(zip(widths, recvs[:-1]))
    recv_small = recvs[-1]

    out = {}
    for n, _ in BIG:
        w = local[n].shape[-1]
        res = adamw(recv[w], offset[n], local[n].reshape(-1, w), mom1[n].reshape(-1, w), mom2[n].reshape(-1, w),
                    name=f"adamw_{n}")
        out[n] = [r.reshape(local[n].shape) for r in res]

    def state(d):
        small = _pack([d[n] for n in small_names] + [d[n] for n in REPL], F32, SUBLANES)
        return jnp.pad(small, ((0, recv_small.shape[1] - small.shape[0]), (0, 0)))

    out_small = adamw(recv_small, 0, state(local), state(mom1), state(mom2), name="adamw_vectors")
    loss = out_small[0].reshape(-1)[n_small + n_repl]
    vecs = [dict(zip(small_names + list(REPL), _unpack(o, small_shapes + repl_shapes))) for o in out_small]
    per_kind = [[out[n][kind] if n in out else vecs[kind][n] for n in WEIGHTS] for kind in range(4)]
    grads, deltas, new_m, new_v = per_kind
    return (loss, grad_x, *grads, *deltas, *new_m, *new_v)
```
